```python
import jax, jax.numpy as jnp
from jax import lax
import numpy as np

D_MODEL = 1024
BATCH = 16
SEQ = 4096
DEPTH = 4

N_META = 16
EPS = 1e-6
CONV_WIDTH = D_MODEL // 2
CONV_GROUPS = 8
SHORT_CONV_K = 3
LRU_WIDTH = D_MODEL // 2
LRU_HEADS = 8
LRU_HEAD_DIM = LRU_WIDTH // LRU_HEADS
LRU_CONV_K = 4
LRU_C = 8.0
EVEN_IN = 3 * CONV_WIDTH + 2 * LRU_WIDTH
EVEN_MIX = CONV_WIDTH + LRU_WIDTH
MLA_HEADS = 16
QK_NOPE = 64
QK_ROPE = 32
QK_HEAD = QK_NOPE + QK_ROPE
V_HEAD = 64
Q_LORA = 384
KV_LORA = 256
ODD_IN = Q_LORA + KV_LORA + QK_ROPE
ROPE_BASE = 10000.0
ATTN_BLOCK = 128
D_FF = 2816
FFN_CONV_K = 3
N_EVEN = (DEPTH + 1) // 2
N_ODD = DEPTH // 2

kernel_name = "hybrid_conv_rglru_mla_convffn"


def rms_norm(x, g):
    xf = x.astype(jnp.float32)
    y = xf * lax.rsqrt(jnp.mean(xf * xf, axis=-1, keepdims=True) + EPS)
    return (y * g.astype(jnp.float32)).astype(x.dtype)


def causal_dwconv(x, w):
    k_width = w.shape[0]
    t_len = x.shape[1]
    xp = jnp.pad(x, ((0, 0), (k_width - 1, 0), (0, 0)))
    y = xp[:, 0:t_len] * w[0]
    for k in range(1, k_width):
        y = y + xp[:, k:k + t_len] * w[k]
    return y


def rope_tables(t_len):
    pos = jnp.arange(t_len, dtype=jnp.float32)
    inv_freq = ROPE_BASE ** (-jnp.arange(0, QK_ROPE, 2, dtype=jnp.float32) / QK_ROPE)
    ang = pos[:, None] * inv_freq[None, :]
    return jnp.cos(ang), jnp.sin(ang)


def apply_rope(x, cos, sin):
    xf = x.astype(jnp.float32)
    x1, x2 = jnp.split(xf, 2, axis=-1)
    out = jnp.concatenate([x1 * cos - x2 * sin, x2 * cos + x1 * sin], axis=-1)
    return out.astype(x.dtype)


def rg_lru(xc, r_w, r_b, i_w, i_b, lam):
    b, t, _ = xc.shape
    xh = xc.reshape(b, t, LRU_HEADS, LRU_HEAD_DIM)
    r = jax.nn.sigmoid(jnp.einsum('bthi,hij->bthj', xh, r_w).reshape(b, t, LRU_WIDTH) + r_b)
    i = jax.nn.sigmoid(jnp.einsum('bthi,hij->bthj', xh, i_w).reshape(b, t, LRU_WIDTH) + i_b)
    log_a = -LRU_C * r.astype(jnp.float32) * jax.nn.softplus(-lam.astype(jnp.float32))
    a = jnp.exp(log_a)
    mult = jnp.sqrt(-jnp.expm1(2.0 * log_a))
    u = mult * (i * xc).astype(jnp.float32)

    def combine(left, right):
        a1, b1 = left
        a2, b2 = right
        return a1 * a2, a2 * b1 + b2

    _, h = lax.associative_scan(combine, (a, u), axis=1)
    return h.astype(xc.dtype)


def even_layer(x, norm, w_in, conv_a, conv_b, conv_b_bias, r_w, r_b, i_w, i_b, lam, w_out):
    h = rms_norm(x, norm)
    u = h @ w_in
    gb, gc, xa, xb, gate = jnp.split(
        u, [CONV_WIDTH, 2 * CONV_WIDTH, 3 * CONV_WIDTH, 3 * CONV_WIDTH + LRU_WIDTH], axis=-1)
    y_a = gb * causal_dwconv(gc * xa, conv_a)
    xc = causal_dwconv(xb, conv_b) + conv_b_bias
    y_b = jax.nn.gelu(gate) * rg_lru(xc, r_w, r_b, i_w, i_b, lam)
    return x + jnp.concatenate([y_a, y_b], axis=-1) @ w_out


def causal_block_attention(q, k, v):
    b, t, nh, dq = q.shape
    nb = -(-t // ATTN_BLOCK)
    tp = nb * ATTN_BLOCK
    pad = ((0, 0), (0, tp - t), (0, 0), (0, 0))
    q, k, v = jnp.pad(q, pad), jnp.pad(k, pad), jnp.pad(v, pad)
    qb = q.reshape(b, nb, ATTN_BLOCK, nh, dq).transpose(1, 0, 2, 3, 4)
    key_pos = jnp.arange(tp)
    scale = QK_HEAD ** -0.5
    neg = jnp.finfo(jnp.float32).min

    def one_block(args):
        q_blk, blk = args
        s = jnp.einsum('bqhd,bkhd->bhqk', q_blk, k).astype(jnp.float32) * scale
        q_pos = blk * ATTN_BLOCK + jnp.arange(ATTN_BLOCK)
        mask = key_pos[None, :] <= q_pos[:, None]
        s = jnp.where(mask[None, None], s, neg)
        p = jax.nn.softmax(s, axis=-1).astype(v.dtype)
        return jnp.einsum('bhqk,bkhd->bqhd', p, v)

    out = lax.map(one_block, (qb, jnp.arange(nb)))
    out = out.transpose(1, 0, 2, 3, 4).reshape(b, tp, nh, V_HEAD)
    return out[:, :t]


def odd_layer(x, cos, sin, norm, w_in, q_norm, kv_norm, w_uq, w_ukv, w_out):
    b, t, _ = x.shape
    h = rms_norm(x, norm)
    u = h @ w_in
    cq, ckv, k_r = jnp.split(u, [Q_LORA, Q_LORA + KV_LORA], axis=-1)
    q = (rms_norm(cq, q_norm) @ w_uq).reshape(b, t, MLA_HEADS, QK_HEAD)
    q_nope, q_rope = jnp.split(q, [QK_NOPE], axis=-1)
    q_rope = apply_rope(q_rope, cos[:, None, :], sin[:, None, :])
    kv = (rms_norm(ckv, kv_norm) @ w_ukv).reshape(b, t, MLA_HEADS, QK_NOPE + V_HEAD)
    k_nope, v = jnp.split(kv, [QK_NOPE], axis=-1)
    k_rope = apply_rope(k_r, cos, sin)
    k_rope = jnp.broadcast_to(k_rope[:, :, None, :], (b, t, MLA_HEADS, QK_ROPE))
    qf = jnp.concatenate([q_nope, q_rope], axis=-1)
    kf = jnp.concatenate([k_nope, k_rope], axis=-1)
    o = causal_block_attention(qf, kf, v).reshape(b, t, MLA_HEADS * V_HEAD)
    return x + o @ w_out


def ffn_layer(x, norm, w_up, conv_w, conv_b, w_down):
    h = rms_norm(x, norm)
    u = causal_dwconv(h @ w_up, conv_w) + conv_b
    a, g = jnp.split(u, 2, axis=-1)
    return x + (jax.nn.silu(a) * g) @ w_down


def _fwd_setup_inputs(seed: int = 0) -> dict:
    key = jax.random.key(seed)
    ks = iter(jax.random.split(key, 40))

    def nrm(shape, scale):
        return jax.random.normal(next(ks), shape, jnp.float32) * scale

    def gain(shape):
        return 1.0 + nrm(shape, 0.01)

    u = jax.random.uniform(next(ks), (N_EVEN, LRU_WIDTH), jnp.float32, 0.9, 0.999)
    a_base = u ** (1.0 / LRU_C)
    lam = jnp.log(a_base) - jnp.log1p(-a_base)
    return {
        "x": nrm((BATCH, SEQ, D_MODEL), 1.0),
        "meta_tokens": nrm((N_META, D_MODEL), 1.0),
        "ev_norm": gain((N_EVEN, D_MODEL)),
        "ev_w_in": nrm((N_EVEN, D_MODEL, EVEN_IN), D_MODEL ** -0.5),
        "ev_conv_a": nrm((N_EVEN, SHORT_CONV_K, CONV_WIDTH), SHORT_CONV_K ** -0.5),
        "ev_conv_b": nrm((N_EVEN, LRU_CONV_K, LRU_WIDTH), LRU_CONV_K ** -0.5),
        "ev_conv_b_bias": nrm((N_EVEN, LRU_WIDTH), 0.02),
        "ev_gate_r_w": nrm((N_EVEN, LRU_HEADS, LRU_HEAD_DIM, LRU_HEAD_DIM), LRU_HEAD_DIM ** -0.5),
        "ev_gate_r_b": nrm((N_EVEN, LRU_WIDTH), 0.02),
        "ev_gate_i_w": nrm((N_EVEN, LRU_HEADS, LRU_HEAD_DIM, LRU_HEAD_DIM), LRU_HEAD_DIM ** -0.5),
        "ev_gate_i_b": nrm((N_EVEN, LRU_WIDTH), 0.02),
        "ev_lru_lambda": lam,
        "ev_w_out": nrm((N_EVEN, EVEN_MIX, D_MODEL), EVEN_MIX ** -0.5),
        "od_norm": gain((N_ODD, D_MODEL)),
        "od_w_in": nrm((N_ODD, D_MODEL, ODD_IN), D_MODEL ** -0.5),
        "od_q_norm": gain((N_ODD, Q_LORA)),
        "od_kv_norm": gain((N_ODD, KV_LORA)),
        "od_w_uq": nrm((N_ODD, Q_LORA, MLA_HEADS * QK_HEAD), Q_LORA ** -0.5),
        "od_w_ukv": nrm((N_ODD, KV_LORA, MLA_HEADS * (QK_NOPE + V_HEAD)), KV_LORA ** -0.5),
        "od_w_out": nrm((N_ODD, MLA_HEADS * V_HEAD, D_MODEL), (MLA_HEADS * V_HEAD) ** -0.5),
        "ffn_norm": gain((DEPTH, D_MODEL)),
        "ffn_w_up": nrm((DEPTH, D_MODEL, 2 * D_FF), D_MODEL ** -0.5),
        "ffn_conv_w": nrm((DEPTH, FFN_CONV_K, 2 * D_FF), FFN_CONV_K ** -0.5),
        "ffn_conv_b": nrm((DEPTH, 2 * D_FF), 0.02),
        "ffn_w_down": nrm((DEPTH, D_FF, D_MODEL), D_FF ** -0.5),
        "final_norm": gain((D_MODEL,)),
    }


def _fwd_reference(x, meta_tokens, ev_norm, ev_w_in, ev_conv_a, ev_conv_b, ev_conv_b_bias,
              ev_gate_r_w, ev_gate_r_b, ev_gate_i_w, ev_gate_i_b, ev_lru_lambda, ev_w_out,
              od_norm, od_w_in, od_q_norm, od_kv_norm, od_w_uq, od_w_ukv, od_w_out,
              ffn_norm, ffn_w_up, ffn_conv_w, ffn_conv_b, ffn_w_down, final_norm):
    b = x.shape[0]
    meta = jnp.broadcast_to(meta_tokens[None].astype(x.dtype), (b, N_META, D_MODEL))
    h = jnp.concatenate([meta, x], axis=1)
    cos, sin = rope_tables(h.shape[1])
    for layer in range(DEPTH):
        j = layer // 2
        if layer % 2 == 0:
            h = even_layer(h, ev_norm[j], ev_w_in[j], ev_conv_a[j], ev_conv_b[j], ev_conv_b_bias[j],
                           ev_gate_r_w[j], ev_gate_r_b[j], ev_gate_i_w[j], ev_gate_i_b[j],
                           ev_lru_lambda[j], ev_w_out[j])
        else:
            h = odd_layer(h, cos, sin, od_norm[j], od_w_in[j], od_q_norm[j], od_kv_norm[j],
                          od_w_uq[j], od_w_ukv[j], od_w_out[j])
        h = ffn_layer(h, ffn_norm[layer], ffn_w_up[layer], ffn_conv_w[layer], ffn_conv_b[layer],
                      ffn_w_down[layer])
    h = rms_norm(h, final_norm)
    return h[:, N_META:]


import jax as _jax
import jax.numpy as _jnp

TWIN_FORMAT = 'train_step'
FWD_PARAMS = ['x', 'meta_tokens', 'ev_norm', 'ev_w_in', 'ev_conv_a', 'ev_conv_b', 'ev_conv_b_bias', 'ev_gate_r_w', 'ev_gate_r_b', 'ev_gate_i_w', 'ev_gate_i_b', 'ev_lru_lambda', 'ev_w_out', 'od_norm', 'od_w_in', 'od_q_norm', 'od_kv_norm', 'od_w_uq', 'od_w_ukv', 'od_w_out', 'ffn_norm', 'ffn_w_up', 'ffn_conv_w', 'ffn_conv_b', 'ffn_w_down', 'final_norm']
TWIN_WEIGHTS = ['meta_tokens', 'ev_norm', 'ev_w_in', 'ev_conv_a', 'ev_conv_b', 'ev_conv_b_bias', 'ev_gate_r_w', 'ev_gate_r_b', 'ev_gate_i_w', 'ev_gate_i_b', 'ev_lru_lambda', 'ev_w_out', 'od_norm', 'od_w_in', 'od_q_norm', 'od_kv_norm', 'od_w_uq', 'od_w_ukv', 'od_w_out', 'ffn_norm', 'ffn_w_up', 'ffn_conv_w', 'ffn_conv_b', 'ffn_w_down', 'final_norm']
TWIN_DIFF_INPUT = 'x'
TWIN_INPUTS = ['x', 'meta_tokens', 'ev_norm', 'ev_w_in', 'ev_conv_a', 'ev_conv_b', 'ev_conv_b_bias', 'ev_gate_r_w', 'ev_gate_r_b', 'ev_gate_i_w', 'ev_gate_i_b', 'ev_lru_lambda', 'ev_w_out', 'od_norm', 'od_w_in', 'od_q_norm', 'od_kv_norm', 'od_w_uq', 'od_w_ukv', 'od_w_out', 'ffn_norm', 'ffn_w_up', 'ffn_conv_w', 'ffn_conv_b', 'ffn_w_down', 'final_norm', 'loss_target', 'm_meta_tokens', 'm_ev_norm', 'm_ev_w_in', 'm_ev_conv_a', 'm_ev_conv_b', 'm_ev_conv_b_bias', 'm_ev_gate_r_w', 'm_ev_gate_r_b', 'm_ev_gate_i_w', 'm_ev_gate_i_b', 'm_ev_lru_lambda', 'm_ev_w_out', 'm_od_norm', 'm_od_w_in', 'm_od_q_norm', 'm_od_kv_norm', 'm_od_w_uq', 'm_od_w_ukv', 'm_od_w_out', 'm_ffn_norm', 'm_ffn_w_up', 'm_ffn_conv_w', 'm_ffn_conv_b', 'm_ffn_w_down', 'm_final_norm', 'v_meta_tokens', 'v_ev_norm', 'v_ev_w_in', 'v_ev_conv_a', 'v_ev_conv_b', 'v_ev_conv_b_bias', 'v_ev_gate_r_w', 'v_ev_gate_r_b', 'v_ev_gate_i_w', 'v_ev_gate_i_b', 'v_ev_lru_lambda', 'v_ev_w_out', 'v_od_norm', 'v_od_w_in', 'v_od_q_norm', 'v_od_kv_norm', 'v_od_w_uq', 'v_od_w_ukv', 'v_od_w_out', 'v_ffn_norm', 'v_ffn_w_up', 'v_ffn_conv_w', 'v_ffn_conv_b', 'v_ffn_w_down', 'v_final_norm']
TWIN_OUTPUTS = ['loss', 'grad_x', 'grad_meta_tokens', 'grad_ev_norm', 'grad_ev_w_in', 'grad_ev_conv_a', 'grad_ev_conv_b', 'grad_ev_conv_b_bias', 'grad_ev_gate_r_w', 'grad_ev_gate_r_b', 'grad_ev_gate_i_w', 'grad_ev_gate_i_b', 'grad_ev_lru_lambda', 'grad_ev_w_out', 'grad_od_norm', 'grad_od_w_in', 'grad_od_q_norm', 'grad_od_kv_norm', 'grad_od_w_uq', 'grad_od_w_ukv', 'grad_od_w_out', 'grad_ffn_norm', 'grad_ffn_w_up', 'grad_ffn_conv_w', 'grad_ffn_conv_b', 'grad_ffn_w_down', 'grad_final_norm', 'delta_meta_tokens', 'delta_ev_norm', 'delta_ev_w_in', 'delta_ev_conv_a', 'delta_ev_conv_b', 'delta_ev_conv_b_bias', 'delta_ev_gate_r_w', 'delta_ev_gate_r_b', 'delta_ev_gate_i_w', 'delta_ev_gate_i_b', 'delta_ev_lru_lambda', 'delta_ev_w_out', 'delta_od_norm', 'delta_od_w_in', 'delta_od_q_norm', 'delta_od_kv_norm', 'delta_od_w_uq', 'delta_od_w_ukv', 'delta_od_w_out', 'delta_ffn_norm', 'delta_ffn_w_up', 'delta_ffn_conv_w', 'delta_ffn_conv_b', 'delta_ffn_w_down', 'delta_final_norm', 'new_m_meta_tokens', 'new_m_ev_norm', 'new_m_ev_w_in', 'new_m_ev_conv_a', 'new_m_ev_conv_b', 'new_m_ev_conv_b_bias', 'new_m_ev_gate_r_w', 'new_m_ev_gate_r_b', 'new_m_ev_gate_i_w', 'new_m_ev_gate_i_b', 'new_m_ev_lru_lambda', 'new_m_ev_w_out', 'new_m_od_norm', 'new_m_od_w_in', 'new_m_od_q_norm', 'new_m_od_kv_norm', 'new_m_od_w_uq', 'new_m_od_w_ukv', 'new_m_od_w_out', 'new_m_ffn_norm', 'new_m_ffn_w_up', 'new_m_ffn_conv_w', 'new_m_ffn_conv_b', 'new_m_ffn_w_down', 'new_m_final_norm', 'new_v_meta_tokens', 'new_v_ev_norm', 'new_v_ev_w_in', 'new_v_ev_conv_a', 'new_v_ev_conv_b', 'new_v_ev_conv_b_bias', 'new_v_ev_gate_r_w', 'new_v_ev_gate_r_b', 'new_v_ev_gate_i_w', 'new_v_ev_gate_i_b', 'new_v_ev_lru_lambda', 'new_v_ev_w_out', 'new_v_od_norm', 'new_v_od_w_in', 'new_v_od_q_norm', 'new_v_od_kv_norm', 'new_v_od_w_uq', 'new_v_od_w_ukv', 'new_v_od_w_out', 'new_v_ffn_norm', 'new_v_ffn_w_up', 'new_v_ffn_conv_w', 'new_v_ffn_conv_b', 'new_v_ffn_w_down', 'new_v_final_norm']
TWIN_LEAF_KINDS = {'loss': 'loss', 'grad_x': 'grad_x', 'grad_meta_tokens': 'grad_w', 'grad_ev_norm': 'grad_w', 'grad_ev_w_in': 'grad_w', 'grad_ev_conv_a': 'grad_w', 'grad_ev_conv_b': 'grad_w', 'grad_ev_conv_b_bias': 'grad_w', 'grad_ev_gate_r_w': 'grad_w', 'grad_ev_gate_r_b': 'grad_w', 'grad_ev_gate_i_w': 'grad_w', 'grad_ev_gate_i_b': 'grad_w', 'grad_ev_lru_lambda': 'grad_w', 'grad_ev_w_out': 'grad_w', 'grad_od_norm': 'grad_w', 'grad_od_w_in': 'grad_w', 'grad_od_q_norm': 'grad_w', 'grad_od_kv_norm': 'grad_w', 'grad_od_w_uq': 'grad_w', 'grad_od_w_ukv': 'grad_w', 'grad_od_w_out': 'grad_w', 'grad_ffn_norm': 'grad_w', 'grad_ffn_w_up': 'grad_w', 'grad_ffn_conv_w': 'grad_w', 'grad_ffn_conv_b': 'grad_w', 'grad_ffn_w_down': 'grad_w', 'grad_final_norm': 'grad_w', 'delta_meta_tokens': 'delta_w', 'delta_ev_norm': 'delta_w', 'delta_ev_w_in': 'delta_w', 'delta_ev_conv_a': 'delta_w', 'delta_ev_conv_b': 'delta_w', 'delta_ev_conv_b_bias': 'delta_w', 'delta_ev_gate_r_w': 'delta_w', 'delta_ev_gate_r_b': 'delta_w', 'delta_ev_gate_i_w': 'delta_w', 'delta_ev_gate_i_b': 'delta_w', 'delta_ev_lru_lambda': 'delta_w', 'delta_ev_w_out': 'delta_w', 'delta_od_norm': 'delta_w', 'delta_od_w_in': 'delta_w', 'delta_od_q_norm': 'delta_w', 'delta_od_kv_norm': 'delta_w', 'delta_od_w_uq': 'delta_w', 'delta_od_w_ukv': 'delta_w', 'delta_od_w_out': 'delta_w', 'delta_ffn_norm': 'delta_w', 'delta_ffn_w_up': 'delta_w', 'delta_ffn_conv_w': 'delta_w', 'delta_ffn_conv_b': 'delta_w', 'delta_ffn_w_down': 'delta_w', 'delta_final_norm': 'delta_w', 'new_m_meta_tokens': 'new_m', 'new_m_ev_norm': 'new_m', 'new_m_ev_w_in': 'new_m', 'new_m_ev_conv_a': 'new_m', 'new_m_ev_conv_b': 'new_m', 'new_m_ev_conv_b_bias': 'new_m', 'new_m_ev_gate_r_w': 'new_m', 'new_m_ev_gate_r_b': 'new_m', 'new_m_ev_gate_i_w': 'new_m', 'new_m_ev_gate_i_b': 'new_m', 'new_m_ev_lru_lambda': 'new_m', 'new_m_ev_w_out': 'new_m', 'new_m_od_norm': 'new_m', 'new_m_od_w_in': 'new_m', 'new_m_od_q_norm': 'new_m', 'new_m_od_kv_norm': 'new_m', 'new_m_od_w_uq': 'new_m', 'new_m_od_w_ukv': 'new_m', 'new_m_od_w_out': 'new_m', 'new_m_ffn_norm': 'new_m', 'new_m_ffn_w_up': 'new_m', 'new_m_ffn_conv_w': 'new_m', 'new_m_ffn_conv_b': 'new_m', 'new_m_ffn_w_down': 'new_m', 'new_m_final_norm': 'new_m', 'new_v_meta_tokens': 'new_v', 'new_v_ev_norm': 'new_v', 'new_v_ev_w_in': 'new_v', 'new_v_ev_conv_a': 'new_v', 'new_v_ev_conv_b': 'new_v', 'new_v_ev_conv_b_bias': 'new_v', 'new_v_ev_gate_r_w': 'new_v', 'new_v_ev_gate_r_b': 'new_v', 'new_v_ev_gate_i_w': 'new_v', 'new_v_ev_gate_i_b': 'new_v', 'new_v_ev_lru_lambda': 'new_v', 'new_v_ev_w_out': 'new_v', 'new_v_od_norm': 'new_v', 'new_v_od_w_in': 'new_v', 'new_v_od_q_norm': 'new_v', 'new_v_od_kv_norm': 'new_v', 'new_v_od_w_uq': 'new_v', 'new_v_od_w_ukv': 'new_v', 'new_v_od_w_out': 'new_v', 'new_v_ffn_norm': 'new_v', 'new_v_ffn_w_up': 'new_v', 'new_v_ffn_conv_w': 'new_v', 'new_v_ffn_conv_b': 'new_v', 'new_v_ffn_w_down': 'new_v', 'new_v_final_norm': 'new_v'}


def _forward(args):
    return _fwd_reference(*[args[k] for k in FWD_PARAMS])


def _output_shape():
    out = _jax.eval_shape(lambda: _forward(_fwd_setup_inputs(0)))
    return out.shape, out.dtype

N_MICROBATCH = 1
ADAM_LR = 0.001
ADAM_B1 = 0.9
ADAM_B2 = 0.999
ADAM_EPS = 1e-08
ADAM_WD = 0.01
ADAM_STEP = 10
PER_EXAMPLE_BATCH_AXIS = {'x': 0, 'loss_target': 0}
SHARED_INPUTS = []
_WEIGHT_DTYPES = {'meta_tokens': _jnp.float32, 'ev_norm': _jnp.float32, 'ev_w_in': _jnp.float32, 'ev_conv_a': _jnp.float32, 'ev_conv_b': _jnp.float32, 'ev_conv_b_bias': _jnp.float32, 'ev_gate_r_w': _jnp.float32, 'ev_gate_r_b': _jnp.float32, 'ev_gate_i_w': _jnp.float32, 'ev_gate_i_b': _jnp.float32, 'ev_lru_lambda': _jnp.float32, 'ev_w_out': _jnp.float32, 'od_norm': _jnp.float32, 'od_w_in': _jnp.float32, 'od_q_norm': _jnp.float32, 'od_kv_norm': _jnp.float32, 'od_w_uq': _jnp.float32, 'od_w_ukv': _jnp.float32, 'od_w_out': _jnp.float32, 'ffn_norm': _jnp.float32, 'ffn_w_up': _jnp.float32, 'ffn_conv_w': _jnp.float32, 'ffn_conv_b': _jnp.float32, 'ffn_w_down': _jnp.float32, 'final_norm': _jnp.float32}
MOMENT_SCALE = {'meta_tokens': 1.914530e-02, 'ev_norm': 3.325331e-01, 'ev_w_in': 2.033267e-01, 'ev_conv_a': 2.523596e-01, 'ev_conv_b': 1.364468e-01, 'ev_conv_b_bias': 1.925094e+00, 'ev_gate_r_w': 6.268623e-02, 'ev_gate_r_b': 3.797367e-02, 'ev_gate_i_w': 1.111682e-01, 'ev_gate_i_b': 3.916153e-02, 'ev_lru_lambda': 7.032467e-02, 'ev_w_out': 1.907023e-01, 'od_norm': 6.374564e-02, 'od_w_in': 7.689575e-02, 'od_q_norm': 5.538528e-02, 'od_kv_norm': 1.051871e-01, 'od_w_uq': 2.716988e-02, 'od_w_ukv': 3.653900e-02, 'od_w_out': 4.278910e-02, 'ffn_norm': 1.707361e-01, 'ffn_w_up': 7.239882e-02, 'ffn_conv_w': 7.315721e-02, 'ffn_conv_b': 7.314345e-02, 'ffn_w_down': 1.179027e-01, 'final_norm': 6.384703e+01}


def _to_microbatches(a, axis):
    t = _jnp.moveaxis(a, axis, 0)
    t = t.reshape((N_MICROBATCH, t.shape[0] // N_MICROBATCH) + t.shape[1:])
    return _jnp.moveaxis(t, 1, axis + 1)


def setup_inputs(seed: int = 0) -> dict:
    inp = _fwd_setup_inputs(seed)
    key = _jax.random.fold_in(_jax.random.key(seed), 7919)
    shape, _ = _output_shape()
    out = dict(inp)
    out["loss_target"] = _jax.random.normal(_jax.random.fold_in(key, 0), shape, _jnp.float32)
    for i, name in enumerate(TWIN_WEIGHTS):
        w = inp[name].astype(_jnp.float32)
        if MOMENT_SCALE is None:
            s = _jnp.sqrt(_jnp.mean(_jnp.square(w)) + 1e-30)
        else:
            s = MOMENT_SCALE[name]
        km, kv = _jax.random.split(_jax.random.fold_in(key, i + 1))
        out[name] = w
        out["m_" + name] = s * _jax.random.normal(km, w.shape, _jnp.float32)
        out["v_" + name] = (s * s) * _jax.random.uniform(kv, w.shape, _jnp.float32, 0.5, 1.5)
    if N_MICROBATCH > 1:
        for name, axis in PER_EXAMPLE_BATCH_AXIS.items():
            out[name] = _to_microbatches(out[name], axis)
    return {'x': out['x'], 'meta_tokens': out['meta_tokens'], 'ev_norm': out['ev_norm'], 'ev_w_in': out['ev_w_in'], 'ev_conv_a': out['ev_conv_a'], 'ev_conv_b': out['ev_conv_b'], 'ev_conv_b_bias': out['ev_conv_b_bias'], 'ev_gate_r_w': out['ev_gate_r_w'], 'ev_gate_r_b': out['ev_gate_r_b'], 'ev_gate_i_w': out['ev_gate_i_w'], 'ev_gate_i_b': out['ev_gate_i_b'], 'ev_lru_lambda': out['ev_lru_lambda'], 'ev_w_out': out['ev_w_out'], 'od_norm': out['od_norm'], 'od_w_in': out['od_w_in'], 'od_q_norm': out['od_q_norm'], 'od_kv_norm': out['od_kv_norm'], 'od_w_uq': out['od_w_uq'], 'od_w_ukv': out['od_w_ukv'], 'od_w_out': out['od_w_out'], 'ffn_norm': out['ffn_norm'], 'ffn_w_up': out['ffn_w_up'], 'ffn_conv_w': out['ffn_conv_w'], 'ffn_conv_b': out['ffn_conv_b'], 'ffn_w_down': out['ffn_w_down'], 'final_norm': out['final_norm'], 'loss_target': out['loss_target'], 'm_meta_tokens': out['m_meta_tokens'], 'm_ev_norm': out['m_ev_norm'], 'm_ev_w_in': out['m_ev_w_in'], 'm_ev_conv_a': out['m_ev_conv_a'], 'm_ev_conv_b': out['m_ev_conv_b'], 'm_ev_conv_b_bias': out['m_ev_conv_b_bias'], 'm_ev_gate_r_w': out['m_ev_gate_r_w'], 'm_ev_gate_r_b': out['m_ev_gate_r_b'], 'm_ev_gate_i_w': out['m_ev_gate_i_w'], 'm_ev_gate_i_b': out['m_ev_gate_i_b'], 'm_ev_lru_lambda': out['m_ev_lru_lambda'], 'm_ev_w_out': out['m_ev_w_out'], 'm_od_norm': out['m_od_norm'], 'm_od_w_in': out['m_od_w_in'], 'm_od_q_norm': out['m_od_q_norm'], 'm_od_kv_norm': out['m_od_kv_norm'], 'm_od_w_uq': out['m_od_w_uq'], 'm_od_w_ukv': out['m_od_w_ukv'], 'm_od_w_out': out['m_od_w_out'], 'm_ffn_norm': out['m_ffn_norm'], 'm_ffn_w_up': out['m_ffn_w_up'], 'm_ffn_conv_w': out['m_ffn_conv_w'], 'm_ffn_conv_b': out['m_ffn_conv_b'], 'm_ffn_w_down': out['m_ffn_w_down'], 'm_final_norm': out['m_final_norm'], 'v_meta_tokens': out['v_meta_tokens'], 'v_ev_norm': out['v_ev_norm'], 'v_ev_w_in': out['v_ev_w_in'], 'v_ev_conv_a': out['v_ev_conv_a'], 'v_ev_conv_b': out['v_ev_conv_b'], 'v_ev_conv_b_bias': out['v_ev_conv_b_bias'], 'v_ev_gate_r_w': out['v_ev_gate_r_w'], 'v_ev_gate_r_b': out['v_ev_gate_r_b'], 'v_ev_gate_i_w': out['v_ev_gate_i_w'], 'v_ev_gate_i_b': out['v_ev_gate_i_b'], 'v_ev_lru_lambda': out['v_ev_lru_lambda'], 'v_ev_w_out': out['v_ev_w_out'], 'v_od_norm': out['v_od_norm'], 'v_od_w_in': out['v_od_w_in'], 'v_od_q_norm': out['v_od_q_norm'], 'v_od_kv_norm': out['v_od_kv_norm'], 'v_od_w_uq': out['v_od_w_uq'], 'v_od_w_ukv': out['v_od_w_ukv'], 'v_od_w_out': out['v_od_w_out'], 'v_ffn_norm': out['v_ffn_norm'], 'v_ffn_w_up': out['v_ffn_w_up'], 'v_ffn_conv_w': out['v_ffn_conv_w'], 'v_ffn_conv_b': out['v_ffn_conv_b'], 'v_ffn_w_down': out['v_ffn_w_down'], 'v_final_norm': out['v_final_norm']}


def _loss(weights, diff, rest, loss_target):
    with _jax.named_scope("forward"):
        args = {**rest, TWIN_DIFF_INPUT: diff, **{k: w.astype(_WEIGHT_DTYPES[k]) for k, w in weights.items()}}
        y = _forward(args)
    with _jax.named_scope("loss_head"):
        err = _jnp.square(y.astype(_jnp.float32) - loss_target)
        return 0.5 * _jnp.sum(_jnp.mean(err, axis=-1)) if err.ndim else 0.5 * err


def _adamw(w, g, m, v):
    m = ADAM_B1 * m + (1.0 - ADAM_B1) * g
    v = ADAM_B2 * v + (1.0 - ADAM_B2) * _jnp.square(g)
    m_hat = m / (1.0 - ADAM_B1 ** ADAM_STEP)
    v_hat = v / (1.0 - ADAM_B2 ** ADAM_STEP)
    delta = -ADAM_LR * (m_hat / (_jnp.sqrt(v_hat) + ADAM_EPS) + ADAM_WD * w)
    return delta, m, v


def reference(x, meta_tokens, ev_norm, ev_w_in, ev_conv_a, ev_conv_b, ev_conv_b_bias, ev_gate_r_w, ev_gate_r_b, ev_gate_i_w, ev_gate_i_b, ev_lru_lambda, ev_w_out, od_norm, od_w_in, od_q_norm, od_kv_norm, od_w_uq, od_w_ukv, od_w_out, ffn_norm, ffn_w_up, ffn_conv_w, ffn_conv_b, ffn_w_down, final_norm, loss_target, m_meta_tokens, m_ev_norm, m_ev_w_in, m_ev_conv_a, m_ev_conv_b, m_ev_conv_b_bias, m_ev_gate_r_w, m_ev_gate_r_b, m_ev_gate_i_w, m_ev_gate_i_b, m_ev_lru_lambda, m_ev_w_out, m_od_norm, m_od_w_in, m_od_q_norm, m_od_kv_norm, m_od_w_uq, m_od_w_ukv, m_od_w_out, m_ffn_norm, m_ffn_w_up, m_ffn_conv_w, m_ffn_conv_b, m_ffn_w_down, m_final_norm, v_meta_tokens, v_ev_norm, v_ev_w_in, v_ev_conv_a, v_ev_conv_b, v_ev_conv_b_bias, v_ev_gate_r_w, v_ev_gate_r_b, v_ev_gate_i_w, v_ev_gate_i_b, v_ev_lru_lambda, v_ev_w_out, v_od_norm, v_od_w_in, v_od_q_norm, v_od_kv_norm, v_od_w_uq, v_od_w_ukv, v_od_w_out, v_ffn_norm, v_ffn_w_up, v_ffn_conv_w, v_ffn_conv_b, v_ffn_w_down, v_final_norm):
    given = dict(x=x, meta_tokens=meta_tokens, ev_norm=ev_norm, ev_w_in=ev_w_in, ev_conv_a=ev_conv_a, ev_conv_b=ev_conv_b, ev_conv_b_bias=ev_conv_b_bias, ev_gate_r_w=ev_gate_r_w, ev_gate_r_b=ev_gate_r_b, ev_gate_i_w=ev_gate_i_w, ev_gate_i_b=ev_gate_i_b, ev_lru_lambda=ev_lru_lambda, ev_w_out=ev_w_out, od_norm=od_norm, od_w_in=od_w_in, od_q_norm=od_q_norm, od_kv_norm=od_kv_norm, od_w_uq=od_w_uq, od_w_ukv=od_w_ukv, od_w_out=od_w_out, ffn_norm=ffn_norm, ffn_w_up=ffn_w_up, ffn_conv_w=ffn_conv_w, ffn_conv_b=ffn_conv_b, ffn_w_down=ffn_w_down, final_norm=final_norm, loss_target=loss_target, m_meta_tokens=m_meta_tokens, m_ev_norm=m_ev_norm, m_ev_w_in=m_ev_w_in, m_ev_conv_a=m_ev_conv_a, m_ev_conv_b=m_ev_conv_b, m_ev_conv_b_bias=m_ev_conv_b_bias, m_ev_gate_r_w=m_ev_gate_r_w, m_ev_gate_r_b=m_ev_gate_r_b, m_ev_gate_i_w=m_ev_gate_i_w, m_ev_gate_i_b=m_ev_gate_i_b, m_ev_lru_lambda=m_ev_lru_lambda, m_ev_w_out=m_ev_w_out, m_od_norm=m_od_norm, m_od_w_in=m_od_w_in, m_od_q_norm=m_od_q_norm, m_od_kv_norm=m_od_kv_norm, m_od_w_uq=m_od_w_uq, m_od_w_ukv=m_od_w_ukv, m_od_w_out=m_od_w_out, m_ffn_norm=m_ffn_norm, m_ffn_w_up=m_ffn_w_up, m_ffn_conv_w=m_ffn_conv_w, m_ffn_conv_b=m_ffn_conv_b, m_ffn_w_down=m_ffn_w_down, m_final_norm=m_final_norm, v_meta_tokens=v_meta_tokens, v_ev_norm=v_ev_norm, v_ev_w_in=v_ev_w_in, v_ev_conv_a=v_ev_conv_a, v_ev_conv_b=v_ev_conv_b, v_ev_conv_b_bias=v_ev_conv_b_bias, v_ev_gate_r_w=v_ev_gate_r_w, v_ev_gate_r_b=v_ev_gate_r_b, v_ev_gate_i_w=v_ev_gate_i_w, v_ev_gate_i_b=v_ev_gate_i_b, v_ev_lru_lambda=v_ev_lru_lambda, v_ev_w_out=v_ev_w_out, v_od_norm=v_od_norm, v_od_w_in=v_od_w_in, v_od_q_norm=v_od_q_norm, v_od_kv_norm=v_od_kv_norm, v_od_w_uq=v_od_w_uq, v_od_w_ukv=v_od_w_ukv, v_od_w_out=v_od_w_out, v_ffn_norm=v_ffn_norm, v_ffn_w_up=v_ffn_w_up, v_ffn_conv_w=v_ffn_conv_w, v_ffn_conv_b=v_ffn_conv_b, v_ffn_w_down=v_ffn_w_down, v_final_norm=v_final_norm)
    weights = {n: given[n] for n in TWIN_WEIGHTS}
    shared = {n: given[n] for n in SHARED_INPUTS}
    per_example = {n: given[n] for n in ['x']}
    grad_fn = _jax.value_and_grad(_loss, argnums=(0, 1))

    def one_microbatch(ex, loss_target):
        ex = dict(ex)
        diff = ex.pop(TWIN_DIFF_INPUT)
        return grad_fn(weights, diff, {**shared, **ex}, loss_target)

    if N_MICROBATCH == 1:
        loss, (grad_w, grad_x) = one_microbatch(per_example, given["loss_target"])
    else:
        def body(carry, xs):
            loss_sum, grad_sum = carry
            l_k, (gw_k, gx_k) = one_microbatch(xs[0], xs[1])
            with _jax.named_scope("update"):
                return (loss_sum + l_k, _jax.tree.map(_jnp.add, grad_sum, gw_k)), gx_k

        init = (_jnp.zeros((), _jnp.float32), _jax.tree.map(_jnp.zeros_like, weights))
        (loss, grad_w), grad_x = _jax.lax.scan(body, init, (per_example, given["loss_target"]))
    with _jax.named_scope("update"):
        delta_w, new_m, new_v = {}, {}, {}
        for n in TWIN_WEIGHTS:
            delta_w[n], new_m[n], new_v[n] = _adamw(weights[n], grad_w[n], given["m_" + n], given["v_" + n])
    return (loss, grad_x, *[grad_w[n] for n in TWIN_WEIGHTS], *[delta_w[n] for n in TWIN_WEIGHTS],
            *[new_m[n] for n in TWIN_WEIGHTS], *[new_v[n] for n in TWIN_WEIGHTS])
```

```python
import functools
import math

import jax
import jax.numpy as jnp
from jax import lax
from jax.experimental import pallas as pl
from jax.experimental.pallas import tpu as pltpu

F32 = jnp.float32
BF16 = jnp.bfloat16

N_DEV = 8
N_META = 16
EPS = 1e-6
LRU_C = 8.0
MLA_HEADS = 16
QK_NOPE = 64
QK_ROPE = 32
QK_HEAD = QK_NOPE + QK_ROPE
V_HEAD = 64
HEAD_PAD = 128
Q_LORA = 384
KV_LORA = 256
ODD_IN = Q_LORA + KV_LORA + QK_ROPE
ODD_IN_PAD = 768
ROPE_BASE = 10000.0
LRU_WIDTH = 512
D_FF = 2816

ADAM_LR = 0.001
ADAM_B1 = 0.9
ADAM_B2 = 0.999
ADAM_EPS = 1e-08
ADAM_WD = 0.01
ADAM_STEP = 10

ROW_TILE = 384
SUBLANES = 8
LANES = 128
VMEM_LIMIT = 48 * 1024 * 1024
NEG = -1e30

PARAMS = (
    ("meta_tokens", 1), ("ev_norm", None), ("ev_w_in", 2), ("ev_conv_a", 2), ("ev_conv_b", 2),
    ("ev_conv_b_bias", None), ("ev_gate_r_w", None), ("ev_gate_r_b", None), ("ev_gate_i_w", None),
    ("ev_gate_i_b", None), ("ev_lru_lambda", None), ("ev_w_out", 1), ("od_norm", 1), ("od_w_in", 1),
    ("od_q_norm", 1), ("od_kv_norm", 1), ("od_w_uq", 2), ("od_w_ukv", 2), ("od_w_out", 1),
    ("ffn_norm", None), ("ffn_w_up", 2), ("ffn_conv_w", 2), ("ffn_conv_b", None), ("ffn_w_down", 1),
    ("final_norm", None),
)
BIG = ("ev_w_in", "ev_w_out", "od_w_in", "od_w_uq", "od_w_ukv", "od_w_out", "ffn_w_up", "ffn_w_down")


def _cparams(n_grid):
    return pltpu.CompilerParams(dimension_semantics=("arbitrary",) * n_grid, vmem_limit_bytes=VMEM_LIMIT)


def _pick(dim, target):
    if dim <= target:
        return dim
    best = None
    for t in range(LANES, target + 1, LANES):
        if dim % t == 0:
            best = t
    assert best is not None, (dim, target)
    return best


def _matmul(a, b, mode, out_dtype=F32, residual=None, name="mm"):
    if mode == "nn":
        (m, k), (k2, n) = a.shape, b.shape
    elif mode == "nt":
        (m, k), (n, k2) = a.shape, b.shape
    else:
        (k, m), (k2, n) = a.shape, b.shape
    assert k == k2, (a.shape, b.shape, mode)
    tm, tn, tk = _pick(m, 768), _pick(n, 1408), _pick(k, 1024)
    nk = k // tk
    if mode == "tn":
        a_spec = pl.BlockSpec((tk, tm), lambda i, j, kk: (kk, i))
        dims = (((0,), (0,)), ((), ()))
    else:
        a_spec = pl.BlockSpec((tm, tk), lambda i, j, kk: (i, kk))
        dims = (((1,), (1 if mode == "nt" else 0,)), ((), ()))
    if mode == "nt":
        b_spec = pl.BlockSpec((tn, tk), lambda i, j, kk: (j, kk))
    else:
        b_spec = pl.BlockSpec((tk, tn), lambda i, j, kk: (kk, j))
    o_spec = pl.BlockSpec((tm, tn), lambda i, j, kk: (i, j))
    has_res = residual is not None

    def body(*refs):
        if has_res:
            a_ref, b_ref, r_ref, o_ref, acc_ref = refs
        else:
            a_ref, b_ref, o_ref, acc_ref = refs
        kk = pl.program_id(2)

        @pl.when(kk == 0)
        def _():
            acc_ref[...] = jnp.zeros_like(acc_ref)

        acc_ref[...] += lax.dot_general(a_ref[...].astype(BF16), b_ref[...].astype(BF16), dims,
                                        preferred_element_type=F32)

        @pl.when(kk == nk - 1)
        def _():
            out = acc_ref[...]
            if has_res:
                out = out + r_ref[...].astype(F32)
            o_ref[...] = out.astype(o_ref.dtype)

    in_specs = [a_spec, b_spec] + ([o_spec] if has_res else [])
    args = (a, b) + ((residual,) if has_res else ())
    return pl.pallas_call(
        body, out_shape=jax.ShapeDtypeStruct((m, n), out_dtype), grid=(m // tm, n // tn, nk),
        in_specs=in_specs, out_specs=o_spec, scratch_shapes=[pltpu.VMEM((tm, tn), F32)],
        compiler_params=_cparams(3), name=name)(*args)


def _rms_fwd(x, g, name):
    m, c = x.shape
    tm = _pick(m, ROW_TILE)

    def body(x_ref, g_ref, o_ref):
        xf = x_ref[...].astype(F32)
        r = lax.rsqrt(jnp.mean(xf * xf, axis=-1, keepdims=True) + EPS)
        o_ref[...] = (xf * r * g_ref[...]).astype(o_ref.dtype)

    return pl.pallas_call(
        body, out_shape=jax.ShapeDtypeStruct((m, c), BF16), grid=(m // tm,),
        in_specs=[pl.BlockSpec((tm, c), lambda i: (i, 0)), pl.BlockSpec((1, c), lambda i: (0, 0))],
        out_specs=pl.BlockSpec((tm, c), lambda i: (i, 0)), compiler_params=_cparams(1), name=name)(x, g)


def _rms_bwd(x, g, dy, residual, name):
    m, c = x.shape
    tm = _pick(m, ROW_TILE)
    has_res = residual is not None

    def body(*refs):
        if has_res:
            x_ref, g_ref, dy_ref, r_ref, dx_ref, dg_ref = refs
        else:
            x_ref, g_ref, dy_ref, dx_ref, dg_ref = refs
        xf = x_ref[...].astype(F32)
        dyf = dy_ref[...].astype(F32)
        r = lax.rsqrt(jnp.mean(xf * xf, axis=-1, keepdims=True) + EPS)
        xn = xf * r
        dyg = dyf * g_ref[...]
        dx = r * (dyg - xn * jnp.mean(dyg * xn, axis=-1, keepdims=True))
        if has_res:
            dx = dx + r_ref[...]
        dx_ref[...] = dx

        @pl.when(pl.program_id(0) == 0)
        def _():
            dg_ref[...] = jnp.zeros_like(dg_ref)

        dg_ref[...] += jnp.sum(dyf * xn, axis=0, keepdims=True)

    row = pl.BlockSpec((tm, c), lambda i: (i, 0))
    vec = pl.BlockSpec((1, c), lambda i: (0, 0))
    in_specs = [row, vec, row] + ([row] if has_res else [])
    args = (x, g, dy) + ((residual,) if has_res else ())
    return pl.pallas_call(
        body, out_shape=(jax.ShapeDtypeStruct((m, c), F32), jax.ShapeDtypeStruct((1, c), F32)), grid=(m // tm,),
        in_specs=in_specs, out_specs=(row, vec), compiler_params=_cparams(1), name=name)(*args)


def _chan_call(name, fn, m, tp, tc, ncol, row_ins=(), prev_ins=(), next_ins=(), chan_ins=(), row_outs=(),
               red_outs=()):
    tm = _pick(tp, ROW_TILE)
    tps = tp // tm
    nrow = m // tm
    h8 = tm // SUBLANES
    last8 = m // SUBLANES - 1
    n_in = len(row_ins) + len(prev_ins) + len(next_ins) + len(chan_ins)
    n_r, n_p, n_n = len(row_ins), len(prev_ins), len(next_ins)

    def body(*refs):
        i = pl.program_id(1)
        pos = lax.rem(i, tps)
        at_start = pos == 0
        at_end = pos == tps - 1
        rows = [r[...].astype(F32) for r in refs[:n_r]]
        prevs = [jnp.where(at_start, 0.0, r[...].astype(F32)) for r in refs[n_r:n_r + n_p]]
        nexts = [jnp.where(at_end, 0.0, r[...].astype(F32)) for r in refs[n_r + n_p:n_r + n_p + n_n]]
        chans = [r[...] for r in refs[n_r + n_p + n_n:n_in]]
        out_refs = refs[n_in:n_in + len(row_outs)]
        red_refs = refs[n_in + len(row_outs):]
        row_vals, red_vals = fn(rows, prevs, nexts, chans)
        for ref, val in zip(out_refs, row_vals):
            ref[...] = val.astype(ref.dtype)
        if red_refs:
            @pl.when(i == 0)
            def _():
                for ref in red_refs:
                    ref[...] = jnp.zeros_like(ref)

            for ref, val in zip(red_refs, red_vals):
                ref[...] += val

    in_specs, args = [], []
    for arr, off in row_ins:
        in_specs.append(pl.BlockSpec((tm, tc), lambda j, i, off=off: (i, j + off)))
        args.append(arr)
    for arr, off in prev_ins:
        in_specs.append(pl.BlockSpec((SUBLANES, tc), lambda j, i, off=off: (jnp.maximum(i * h8 - 1, 0), j + off)))
        args.append(arr)
    for arr, off in next_ins:
        in_specs.append(pl.BlockSpec((SUBLANES, tc), lambda j, i, off=off: (jnp.minimum((i + 1) * h8, last8), j + off)))
        args.append(arr)
    for arr, off in chan_ins:
        in_specs.append(pl.BlockSpec((arr.shape[0], tc), lambda j, i, off=off: (0, j + off)))
        args.append(arr)
    out_shape, out_specs = [], []
    for (dt,) in row_outs:
        out_shape.append(jax.ShapeDtypeStruct((m, ncol * tc), dt))
        out_specs.append(pl.BlockSpec((tm, tc), lambda j, i: (i, j)))
    for (k,) in red_outs:
        out_shape.append(jax.ShapeDtypeStruct((k, ncol * tc), F32))
        out_specs.append(pl.BlockSpec((k, tc), lambda j, i: (0, j)))
    return pl.pallas_call(
        body, out_shape=tuple(out_shape), grid=(ncol, nrow), in_specs=in_specs, out_specs=tuple(out_specs),
        compiler_params=_cparams(2), name=name)(*args)


def _shift_down(x, prev8, s):
    if s == 0:
        return x
    xs = pltpu.roll(x, s, 0)
    ps = pltpu.roll(prev8, s, 0)
    rid = lax.broadcasted_iota(jnp.int32, prev8.shape, 0)
    head = jnp.where(rid < s, ps, xs[:SUBLANES])
    return jnp.concatenate([head, xs[SUBLANES:]], axis=0)


def _shift_up(x, next8, s):
    if s == 0:
        return x
    tm = x.shape[0]
    xs = pltpu.roll(x, tm - s, 0)
    ns = pltpu.roll(next8, SUBLANES - s, 0)
    rid = lax.broadcasted_iota(jnp.int32, next8.shape, 0)
    tail = jnp.where(rid >= SUBLANES - s, ns, xs[tm - SUBLANES:])
    return jnp.concatenate([xs[:tm - SUBLANES], tail], axis=0)


def _conv_fwd(x, prev8, w):
    kw = w.shape[0]
    y = w[kw - 1:kw, :] * x
    for k in range(kw - 1):
        y = y + w[k:k + 1, :] * _shift_down(x, prev8, kw - 1 - k)
    return y


def _conv_dw(dy, x, prev8, kw):
    rid = lax.broadcasted_iota(jnp.int32, prev8.shape, 0)
    out = jnp.zeros(prev8.shape, F32)
    for k in range(kw):
        row = jnp.sum(dy * _shift_down(x, prev8, kw - 1 - k), axis=0, keepdims=True)
        out = out + jnp.where(rid == k, row, 0.0)
    return out


def _conv_dx(dy, next8, w):
    kw = w.shape[0]
    dx = w[kw - 1:kw, :] * dy
    for k in range(kw - 1):
        dx = dx + w[k:k + 1, :] * _shift_up(dy, next8, kw - 1 - k)
    return dx


def _sigmoid(x):
    return 1.0 / (1.0 + jnp.exp(-x))


def _expm1(x):
    series = x * (1.0 + x * 0.5 * (1.0 + x * (1.0 / 3.0) * (1.0 + x * 0.25 * (1.0 + x * 0.2))))
    return jnp.where(jnp.abs(x) < 0.3, series, jnp.exp(x) - 1.0)


def _softplus_neg(lam):
    e = jnp.exp(-jnp.abs(lam))
    log1p = jnp.where(e < 1e-2, e * (1.0 - e * (0.5 - e * (1.0 / 3.0))), jnp.log(1.0 + e))
    return jnp.maximum(-lam, 0.0) + log1p


GELU_C = math.sqrt(2.0 / math.pi)


def _gelu(x):
    return 0.5 * x * (1.0 + jnp.tanh(GELU_C * (x + 0.044715 * x * x * x)))


def _gelu_grad(x):
    t = jnp.tanh(GELU_C * (x + 0.044715 * x * x * x))
    return 0.5 * (1.0 + t) + 0.5 * x * (1.0 - t * t) * GELU_C * (1.0 + 3.0 * 0.044715 * x * x)


def _ffn_fwd(x, p, m, tp):
    h = _rms_fwd(x, p["norm"], "ffn_norm")
    u = _matmul(h, p["w_up"], "nn", F32, name="ffn_up")
    tc = 256
    ncol = D_FF // tc

    def gate(rows, prevs, nexts, chans):
        ua, ug = rows
        wa, wg, ba, bg = chans
        a = _conv_fwd(ua, prevs[0], wa) + ba
        g = _conv_fwd(ug, prevs[1], wg) + bg
        return [a * _sigmoid(a) * g], []

    (z,) = _chan_call("ffn_gate", gate, m, tp, tc, ncol, row_ins=[(u, 0), (u, ncol)], prev_ins=[(u, 0), (u, ncol)],
                      chan_ins=[(p["conv_w"], 0), (p["conv_w"], ncol), (p["conv_b"], 0), (p["conv_b"], ncol)],
                      row_outs=[(BF16,)])
    out = _matmul(z, p["w_down"], "nn", F32, residual=x, name="ffn_down")
    return out, (x, h, u, z)


def _ffn_bwd(dout, p, saved, m, tp):
    x, h, u, z = saved
    tc = 256
    ncol = D_FF // tc
    dz = _matmul(dout, p["w_down"], "nt", F32, name="ffn_down_dx")
    d_w_down = _matmul(z, dout, "tn", F32, name="ffn_down_dw")

    def gate_bwd(rows, prevs, nexts, chans):
        ua, ug, dzv = rows
        wa, wg, ba, bg = chans
        a = _conv_fwd(ua, prevs[0], wa) + ba
        g = _conv_fwd(ug, prevs[1], wg) + bg
        sg = _sigmoid(a)
        da = dzv * g * (sg * (1.0 + a * (1.0 - sg)))
        dg = dzv * a * sg
        return ([da, dg],
                [_conv_dw(da, ua, prevs[0], 3), _conv_dw(dg, ug, prevs[1], 3),
                 jnp.sum(da, axis=0, keepdims=True), jnp.sum(dg, axis=0, keepdims=True)])

    da, dg, dcw_a, dcw_g, dcb_a, dcb_g = _chan_call(
        "ffn_gate_bwd", gate_bwd, m, tp, tc, ncol, row_ins=[(u, 0), (u, ncol), (dz, 0)], prev_ins=[(u, 0), (u, ncol)],
        chan_ins=[(p["conv_w"], 0), (p["conv_w"], ncol), (p["conv_b"], 0), (p["conv_b"], ncol)],
        row_outs=[(F32,), (F32,)], red_outs=[(SUBLANES,), (SUBLANES,), (1,), (1,)])

    def conv_dx(rows, prevs, nexts, chans):
        return [_conv_dx(rows[0], nexts[0], chans[0]), _conv_dx(rows[1], nexts[1], chans[1])], []

    dua, dug = _chan_call("ffn_conv_dx", conv_dx, m, tp, tc, ncol, row_ins=[(da, 0), (dg, 0)],
                          next_ins=[(da, 0), (dg, 0)], chan_ins=[(p["conv_w"], 0), (p["conv_w"], ncol)],
                          row_outs=[(BF16,), (BF16,)])
    d_w_up = jnp.concatenate([_matmul(h, dua, "tn", F32, name="ffn_up_dw_a"),
                              _matmul(h, dug, "tn", F32, name="ffn_up_dw_g")], axis=1)
    dh = _matmul(dua, p["w_up_a"], "nt", F32, name="ffn_up_dx_a")
    dh = _matmul(dug, p["w_up_g"], "nt", F32, residual=dh, name="ffn_up_dx_g")
    dx, d_norm = _rms_bwd(x, p["norm"], dh, dout, "ffn_norm_bwd")
    d_conv_w = jnp.concatenate([dcw_a[:3], dcw_g[:3]], axis=1)
    d_conv_b = jnp.concatenate([dcb_a, dcb_g], axis=1)
    return dx, dict(norm=d_norm, w_up=d_w_up, conv_w=d_conv_w, conv_b=d_conv_b, w_down=d_w_down)


def _to_scan(x, nb, tp):
    return x.reshape(nb, tp, LRU_WIDTH // LANES, LANES).transpose(1, 0, 2, 3).reshape(tp, -1, LANES)


def _from_scan(x, nb, tp):
    return x.reshape(tp, nb, LRU_WIDTH // LANES, LANES).transpose(1, 0, 2, 3).reshape(nb * tp, LRU_WIDTH)


def _scan_fwd(a, u):
    t_len, s, _ = a.shape
    tc = _pick(t_len, 640)
    blk = pl.BlockSpec((tc, s, LANES), lambda i: (i, 0, 0))

    def body(a_ref, u_ref, h_ref, carry):
        @pl.when(pl.program_id(0) == 0)
        def _():
            carry[...] = jnp.zeros_like(carry)

        def step(t, h):
            h = a_ref[t] * h + u_ref[t]
            h_ref[t] = h
            return h

        carry[...] = lax.fori_loop(0, tc, step, carry[...], unroll=8)

    return pl.pallas_call(
        body, out_shape=jax.ShapeDtypeStruct(a.shape, F32), grid=(t_len // tc,), in_specs=[blk, blk], out_specs=blk,
        scratch_shapes=[pltpu.VMEM((s, LANES), F32)], compiler_params=_cparams(1), name="lru_scan")(a, u)


def _scan_bwd(dh, a, h_prev):
    t_len, s, _ = a.shape
    tc = _pick(t_len, 640)
    nb = t_len // tc
    blk = pl.BlockSpec((tc, s, LANES), lambda i: (nb - 1 - i, 0, 0))

    def body(dh_ref, a_ref, hp_ref, du_ref, da_ref, carry):
        @pl.when(pl.program_id(0) == 0)
        def _():
            carry[...] = jnp.zeros_like(carry)

        def step(k, c):
            t = tc - 1 - k
            d = dh_ref[t] + c
            du_ref[t] = d
            da_ref[t] = d * hp_ref[t]
            return a_ref[t] * d

        carry[...] = lax.fori_loop(0, tc, step, carry[...], unroll=8)

    shp = jax.ShapeDtypeStruct(a.shape, F32)
    return pl.pallas_call(
        body, out_shape=(shp, shp), grid=(nb,), in_specs=[blk, blk, blk], out_specs=(blk, blk),
        scratch_shapes=[pltpu.VMEM((s, LANES), F32)], compiler_params=_cparams(1), name="lru_scan_bwd")(dh, a, h_prev)


def _lru_gates(xc, zr, zi, r_b, i_b, lam):
    r = _sigmoid(zr + r_b)
    ig = _sigmoid(zi + i_b)
    sp = _softplus_neg(lam)
    log_a = -LRU_C * r * sp
    a = jnp.exp(log_a)
    mult = jnp.sqrt(-_expm1(2.0 * log_a))
    return r, ig, sp, a, mult


def _even_fwd(x, p, m, tp, nb):
    c = LRU_WIDTH
    h = _rms_fwd(x, p["norm"], "ev_norm")
    u = _matmul(h, p["w_in"], "nn", F32, name="ev_in")

    def pre(rows, prevs, nexts, chans):
        gb, gc, xa, xb = rows
        wa, wb, bias = chans
        pa = gc * xa
        ya = gb * _conv_fwd(pa, prevs[0] * prevs[1], wa)
        xc = _conv_fwd(xb, prevs[2], wb) + bias
        return [ya, xc], []

    ya, xc = _chan_call("ev_pre", pre, m, tp, c, 1, row_ins=[(u, 0), (u, 1), (u, 2), (u, 3)],
                        prev_ins=[(u, 1), (u, 2), (u, 3)],
                        chan_ins=[(p["conv_a"], 0), (p["conv_b"], 0), (p["conv_b_bias"], 0)],
                        row_outs=[(BF16,), (F32,)])
    zr = _matmul(xc, p["gate_r"], "nn", F32, name="ev_gate_r")
    zi = _matmul(xc, p["gate_i"], "nn", F32, name="ev_gate_i")

    def lru_in(rows, prevs, nexts, chans):
        xcv, zrv, ziv = rows
        r, ig, sp, a, mult = _lru_gates(xcv, zrv, ziv, *chans)
        return [a, mult * (ig * xcv)], []

    a, uu = _chan_call("ev_lru_in", lru_in, m, tp, c, 1, row_ins=[(xc, 0), (zr, 0), (zi, 0)],
                       chan_ins=[(p["gate_r_b"], 0), (p["gate_i_b"], 0), (p["lam"], 0)],
                       row_outs=[(F32,), (F32,)])
    a_s = _to_scan(a, nb, tp)
    hs_s = _scan_fwd(a_s, _to_scan(uu, nb, tp))
    hs = _from_scan(hs_s, nb, tp)

    def post(rows, prevs, nexts, chans):
        gate, hv = rows
        return [_gelu(gate) * hv], []

    (yb,) = _chan_call("ev_post", post, m, tp, c, 1, row_ins=[(u, 4), (hs, 0)], row_outs=[(BF16,)])
    out = _matmul(ya, p["w_out_a"], "nn", F32, residual=x, name="ev_out_a")
    out = _matmul(yb, p["w_out_b"], "nn", F32, residual=out, name="ev_out_b")
    return out, (x, h, u, ya, xc, zr, zi, a_s, hs_s, hs, yb)


def _even_bwd(dout, p, saved, m, tp, nb):
    c = LRU_WIDTH
    x, h, u, ya, xc, zr, zi, a_s, hs_s, hs, yb = saved
    dy = _matmul(dout, p["w_out"], "nt", F32, name="ev_out_dx")
    d_w_out = jnp.concatenate([_matmul(ya, dout, "tn", F32, name="ev_out_dw_a"),
                               _matmul(yb, dout, "tn", F32, name="ev_out_dw_b")], axis=0)

    def post_bwd(rows, prevs, nexts, chans):
        dyb, gate, hv = rows
        return [dyb * hv * _gelu_grad(gate), dyb * _gelu(gate)], []

    dgate, dhs = _chan_call("ev_post_bwd", post_bwd, m, tp, c, 1, row_ins=[(dy, 1), (u, 4), (hs, 0)],
                            row_outs=[(F32,), (F32,)])
    h_prev = jnp.concatenate([jnp.zeros_like(hs_s[:1]), hs_s[:-1]], axis=0)
    du_s, da_s = _scan_bwd(_to_scan(dhs, nb, tp), a_s, h_prev)
    du = _from_scan(du_s, nb, tp)
    da = _from_scan(da_s, nb, tp)

    def lru_in_bwd(rows, prevs, nexts, chans):
        duv, dav, xcv, zrv, ziv = rows
        r, ig, sp, a, mult = _lru_gates(xcv, zrv, ziv, *chans)
        dxc = duv * mult * ig
        dig = duv * mult * xcv
        dmult = duv * ig * xcv
        dlog_a = dav * a - dmult * (a * a) / jnp.maximum(mult, 1e-30)
        dr = dlog_a * (-LRU_C * sp)
        dzr = dr * r * (1.0 - r)
        dzi = dig * ig * (1.0 - ig)
        dsp = jnp.sum(dlog_a * (-LRU_C * r), axis=0, keepdims=True)
        dlam = -dsp * _sigmoid(-chans[2])
        return ([dzr, dzi, dxc],
                [jnp.sum(dzr, axis=0, keepdims=True), jnp.sum(dzi, axis=0, keepdims=True), dlam])

    dzr, dzi, dxc, d_r_b, d_i_b, d_lam = _chan_call(
        "ev_lru_in_bwd", lru_in_bwd, m, tp, c, 1, row_ins=[(du, 0), (da, 0), (xc, 0), (zr, 0), (zi, 0)],
        chan_ins=[(p["gate_r_b"], 0), (p["gate_i_b"], 0), (p["lam"], 0)],
        row_outs=[(F32,), (F32,), (F32,)], red_outs=[(1,), (1,), (1,)])
    d_gate_r = _matmul(xc, dzr, "tn", F32, name="ev_gate_r_dw")
    d_gate_i = _matmul(xc, dzi, "tn", F32, name="ev_gate_i_dw")
    dxc = _matmul(dzr, p["gate_r"], "nt", F32, residual=dxc, name="ev_gate_r_dx")
    dxc = _matmul(dzi, p["gate_i"], "nt", F32, residual=dxc, name="ev_gate_i_dx")

    def conv_b_bwd(rows, prevs, nexts, chans):
        dxcv, xb = rows
        return ([_conv_dx(dxcv, nexts[0], chans[0])],
                [_conv_dw(dxcv, xb, prevs[0], 4), jnp.sum(dxcv, axis=0, keepdims=True)])

    dxb, d_conv_b, d_bias = _chan_call(
        "ev_conv_b_bwd", conv_b_bwd, m, tp, c, 1, row_ins=[(dxc, 0), (u, 3)], prev_ins=[(u, 3)], next_ins=[(dxc, 0)],
        chan_ins=[(p["conv_b"], 0)], row_outs=[(F32,)], red_outs=[(SUBLANES,), (1,)])

    def mix_a_bwd(rows, prevs, nexts, chans):
        dya, gb, gc, xa = rows
        (wa,) = chans
        pa = gc * xa
        pa_prev = prevs[0] * prevs[1]
        ca = _conv_fwd(pa, pa_prev, wa)
        dca = dya * gb
        dpa = _conv_dx(dca, nexts[0] * nexts[1], wa)
        return [dya * ca, dpa * xa, dpa * gc], [_conv_dw(dca, pa, pa_prev, 3)]

    dgb, dgc, dxa, d_conv_a = _chan_call(
        "ev_mix_a_bwd", mix_a_bwd, m, tp, c, 1, row_ins=[(dy, 0), (u, 0), (u, 1), (u, 2)],
        prev_ins=[(u, 1), (u, 2)], next_ins=[(dy, 0), (u, 0)], chan_ins=[(p["conv_a"], 0)],
        row_outs=[(F32,), (F32,), (F32,)], red_outs=[(SUBLANES,)])
    du_all = jnp.concatenate([dgb, dgc, dxa, dxb, dgate], axis=1)
    d_w_in = _matmul(h, du_all, "tn", F32, name="ev_in_dw")
    dh = _matmul(du_all, p["w_in"], "nt", F32, name="ev_in_dx")
    dx, d_norm = _rms_bwd(x, p["norm"], dh, dout, "ev_norm_bwd")
    return dx, dict(norm=d_norm, w_in=d_w_in, conv_a=d_conv_a[:3], conv_b=d_conv_b[:4], conv_b_bias=d_bias,
                    gate_r=d_gate_r, gate_r_b=d_r_b, gate_i=d_gate_i, gate_i_b=d_i_b, lam=d_lam, w_out=d_w_out)


def _rope_tables(tp):
    pos = jnp.arange(tp, dtype=F32)
    inv_freq = ROPE_BASE ** (-jnp.arange(0, QK_ROPE, 2, dtype=F32) / QK_ROPE)
    ang = pos[:, None] * inv_freq[None, :]
    cos, sin = jnp.cos(ang), jnp.sin(ang)
    half = QK_ROPE // 2
    one = jnp.ones((tp, QK_NOPE), F32)
    z64 = jnp.zeros((tp, QK_NOPE), F32)
    zh = jnp.zeros((tp, half), F32)
    zt = jnp.zeros((tp, HEAD_PAD - QK_HEAD), F32)
    c_tab = jnp.concatenate([one, cos, cos, zt], axis=1)
    s_lo = jnp.concatenate([z64, -sin, zh, zt], axis=1)
    s_hi = jnp.concatenate([z64, zh, sin, zt], axis=1)
    return c_tab, s_lo, s_hi


def _rope(v, c_tab, s_lo, s_hi):
    half = QK_ROPE // 2
    return v * c_tab + pltpu.roll(v, HEAD_PAD - half, 1) * s_lo + pltpu.roll(v, half, 1) * s_hi


def _rope_t(dv, c_tab, s_lo, s_hi):
    half = QK_ROPE // 2
    return dv * c_tab + pltpu.roll(dv * s_lo, half, 1) + pltpu.roll(dv * s_hi, HEAD_PAD - half, 1)


def _rope_call(name, fn, m, tp, ins, tables, out_dtype):
    tm = _pick(tp, ROW_TILE)
    tps = tp // tm
    n = len(ins)

    def body(*refs):
        vals = [r[...].astype(F32) for r in refs[:n]]
        tabs = [r[...] for r in refs[n:n + 3]]
        refs[n + 3][...] = fn(*vals, *tabs).astype(out_dtype)

    in_specs, args = [], []
    for arr, fixed_col in ins:
        if fixed_col is None:
            in_specs.append(pl.BlockSpec((tm, HEAD_PAD), lambda i, hh: (i, hh)))
        else:
            in_specs.append(pl.BlockSpec((tm, HEAD_PAD), lambda i, hh, fc=fixed_col: (i, fc)))
        args.append(arr)
    for tab in tables:
        in_specs.append(pl.BlockSpec((tm, HEAD_PAD), lambda i, hh: (lax.rem(i, tps), 0)))
        args.append(tab)
    return pl.pallas_call(
        body, out_shape=jax.ShapeDtypeStruct((m, MLA_HEADS * HEAD_PAD), out_dtype), grid=(m // tm, MLA_HEADS),
        in_specs=in_specs, out_specs=pl.BlockSpec((tm, HEAD_PAD), lambda i, hh: (i, hh)),
        compiler_params=_cparams(2), name=name)(*args)


def _rope_k_bwd(dk, tables, m, tp):
    tm = _pick(tp, ROW_TILE)
    tps = tp // tm

    def body(dk_ref, c_ref, lo_ref, hi_ref, o_ref):
        acc = dk_ref[:, 0:HEAD_PAD].astype(F32)
        for hh in range(1, MLA_HEADS):
            acc = acc + dk_ref[:, hh * HEAD_PAD:(hh + 1) * HEAD_PAD].astype(F32)
        d = pltpu.roll(_rope_t(acc, c_ref[...], lo_ref[...], hi_ref[...]), QK_NOPE, 1)
        lane = lax.broadcasted_iota(jnp.int32, d.shape, 1)
        o_ref[...] = jnp.where(lane < QK_ROPE, d, 0.0)

    tab = pl.BlockSpec((tm, HEAD_PAD), lambda i: (lax.rem(i, tps), 0))
    return pl.pallas_call(
        body, out_shape=jax.ShapeDtypeStruct((m, HEAD_PAD), F32), grid=(m // tm,),
        in_specs=[pl.BlockSpec((tm, MLA_HEADS * HEAD_PAD), lambda i: (i, 0)), tab, tab, tab],
        out_specs=pl.BlockSpec((tm, HEAD_PAD), lambda i: (i, 0)), compiler_params=_cparams(1),
        name="od_rope_k_bwd")(dk, *tables)


def _causal_mask(row0, col0, shape):
    rows = row0 + lax.broadcasted_iota(jnp.int32, shape, 0)
    cols = col0 + lax.broadcasted_iota(jnp.int32, shape, 1)
    return cols <= rows


NT = (((1,), (1,)), ((), ()))


def _flash_fwd(q, k, v, nb, tp):
    tq = _pick(tp, ROW_TILE)
    nq = tp // tq

    def body(q_ref, k_ref, v_ref, o_ref, lse_ref):
        i = pl.program_id(2)
        qb = q_ref[...]

        def step(j, carry):
            mx, l, acc = carry
            off = pl.multiple_of(j * tq, tq)
            kb = k_ref[pl.ds(off, tq), :]
            vb = v_ref[pl.ds(off, tq), :]
            s = lax.dot_general(qb, kb, NT, preferred_element_type=F32)
            s = jnp.where(_causal_mask(i * tq, j * tq, s.shape), s, NEG)
            m_new = jnp.maximum(mx, jnp.max(s, axis=1, keepdims=True))
            alpha = jnp.exp(mx - m_new)
            pr = jnp.exp(s - m_new)
            l = alpha * l + jnp.sum(pr, axis=1, keepdims=True)
            acc = alpha * acc + jnp.dot(pr.astype(BF16), vb, preferred_element_type=F32)
            return m_new, l, acc

        init = (jnp.full((tq, 1), NEG, F32), jnp.zeros((tq, 1), F32), jnp.zeros((tq, HEAD_PAD), F32))
        mx, l, acc = lax.fori_loop(0, i + 1, step, init)
        o_ref[...] = (acc / l).astype(o_ref.dtype)
        lse_ref[...] = jnp.broadcast_to(mx + jnp.log(l), (tq, HEAD_PAD))

    qspec = pl.BlockSpec((tq, HEAD_PAD), lambda b, hh, i: (b * nq + i, hh))
    kvspec = pl.BlockSpec((tp, HEAD_PAD), lambda b, hh, i: (b, hh))
    shp = (nb * tp, MLA_HEADS * HEAD_PAD)
    return pl.pallas_call(
        body, out_shape=(jax.ShapeDtypeStruct(shp, BF16), jax.ShapeDtypeStruct(shp, F32)),
        grid=(nb, MLA_HEADS, nq), in_specs=[qspec, kvspec, kvspec], out_specs=(qspec, qspec),
        compiler_params=_cparams(3), name="od_flash_fwd")(q, k, v)


def _flash_prep(o, do, lse_c, nb, tp):
    tq = _pick(tp, ROW_TILE)
    nq = tp // tq

    def body(o_ref, do_ref, lse_ref, dc_ref, lr_ref, dr_ref):
        delta = jnp.sum(o_ref[...].astype(F32) * do_ref[...].astype(F32), axis=1, keepdims=True)
        dc = jnp.broadcast_to(delta, (tq, HEAD_PAD))
        dc_ref[...] = dc
        lr_ref[...] = jnp.transpose(lse_ref[...])[0:SUBLANES, :]
        dr_ref[...] = jnp.transpose(dc)[0:SUBLANES, :]

    qspec = pl.BlockSpec((tq, HEAD_PAD), lambda b, hh, i: (b * nq + i, hh))
    rspec = pl.BlockSpec((None, None, SUBLANES, tq), lambda b, hh, i: (b * MLA_HEADS + hh, i, 0, 0))
    rshape = jax.ShapeDtypeStruct((nb * MLA_HEADS, nq, SUBLANES, tq), F32)
    return pl.pallas_call(
        body, out_shape=(jax.ShapeDtypeStruct((nb * tp, MLA_HEADS * HEAD_PAD), F32), rshape, rshape),
        grid=(nb, MLA_HEADS, nq), in_specs=[qspec, qspec, qspec], out_specs=(qspec, rspec, rspec),
        compiler_params=_cparams(3), name="od_flash_prep")(o, do, lse_c)


def _flash_dq(q, k, v, do, lse_c, delta_c, nb, tp):
    tq = _pick(tp, ROW_TILE)
    nq = tp // tq

    def body(q_ref, k_ref, v_ref, do_ref, lse_ref, dl_ref, dq_ref):
        i = pl.program_id(2)
        qb = q_ref[...]
        dob = do_ref[...]
        lse = lse_ref[:, 0:1]
        delta = dl_ref[:, 0:1]

        def step(j, dq):
            off = pl.multiple_of(j * tq, tq)
            kb = k_ref[pl.ds(off, tq), :]
            vb = v_ref[pl.ds(off, tq), :]
            s = lax.dot_general(qb, kb, NT, preferred_element_type=F32)
            pr = jnp.where(_causal_mask(i * tq, j * tq, s.shape), jnp.exp(s - lse), 0.0)
            dp = lax.dot_general(dob, vb, NT, preferred_element_type=F32)
            ds = pr * (dp - delta)
            return dq + jnp.dot(ds.astype(BF16), kb, preferred_element_type=F32)

        dq_ref[...] = lax.fori_loop(0, i + 1, step, jnp.zeros((tq, HEAD_PAD), F32))

    qspec = pl.BlockSpec((tq, HEAD_PAD), lambda b, hh, i: (b * nq + i, hh))
    kvspec = pl.BlockSpec((tp, HEAD_PAD), lambda b, hh, i: (b, hh))
    return pl.pallas_call(
        body, out_shape=jax.ShapeDtypeStruct((nb * tp, MLA_HEADS * HEAD_PAD), F32), grid=(nb, MLA_HEADS, nq),
        in_specs=[qspec, kvspec, kvspec, qspec, qspec, qspec], out_specs=qspec, compiler_params=_cparams(3),
        name="od_flash_dq")(q, k, v, do, lse_c, delta_c)


def _flash_dkv(q, k, v, do, lse_r, delta_r, nb, tp):
    tq = _pick(tp, ROW_TILE)
    nq = tp // tq

    def body(q_ref, k_ref, v_ref, do_ref, lse_ref, dl_ref, dk_ref, dv_ref):
        j = pl.program_id(2)
        kb = k_ref[...]
        vb = v_ref[...]

        def step(i, carry):
            dk, dv = carry
            off = pl.multiple_of(i * tq, tq)
            qb = q_ref[pl.ds(off, tq), :]
            dob = do_ref[pl.ds(off, tq), :]
            lse = lse_ref[i][0:1, :]
            delta = dl_ref[i][0:1, :]
            st = lax.dot_general(kb, qb, NT, preferred_element_type=F32)
            keys = j * tq + lax.broadcasted_iota(jnp.int32, st.shape, 0)
            queries = i * tq + lax.broadcasted_iota(jnp.int32, st.shape, 1)
            pt = jnp.where(keys <= queries, jnp.exp(st - lse), 0.0)
            dv = dv + jnp.dot(pt.astype(BF16), dob, preferred_element_type=F32)
            dpt = lax.dot_general(vb, dob, NT, preferred_element_type=F32)
            dst = pt * (dpt - delta)
            dk = dk + jnp.dot(dst.astype(BF16), qb, preferred_element_type=F32)
            return dk, dv

        zero = jnp.zeros((tq, HEAD_PAD), F32)
        dk, dv = lax.fori_loop(j, nq, step, (zero, zero))
        dk_ref[...] = dk
        dv_ref[...] = dv.astype(dv_ref.dtype)

    tspec = pl.BlockSpec((tq, HEAD_PAD), lambda b, hh, j: (b * nq + j, hh))
    fullspec = pl.BlockSpec((tp, HEAD_PAD), lambda b, hh, j: (b, hh))
    rspec = pl.BlockSpec((None, nq, SUBLANES, tq), lambda b, hh, j: (b * MLA_HEADS + hh, 0, 0, 0))
    shp = (nb * tp, MLA_HEADS * HEAD_PAD)
    return pl.pallas_call(
        body, out_shape=(jax.ShapeDtypeStruct(shp, F32), jax.ShapeDtypeStruct(shp, BF16)),
        grid=(nb, MLA_HEADS, nq), in_specs=[fullspec, tspec, tspec, fullspec, rspec, rspec],
        out_specs=(tspec, tspec), compiler_params=_cparams(3), name="od_flash_dkv")(q, k, v, do, lse_r, delta_r)


def _odd_fwd(x, p, tables, m, tp, nb):
    scale = QK_HEAD ** -0.5
    h = _rms_fwd(x, p["norm"], "od_norm")
    u = _matmul(h, p["w_in"], "nn", F32, name="od_in")
    cq = u[:, :Q_LORA]
    ckv = u[:, Q_LORA:Q_LORA + KV_LORA]
    cqn = _rms_fwd(cq, p["q_norm"], "od_q_norm")
    ckvn = _rms_fwd(ckv, p["kv_norm"], "od_kv_norm")
    q_raw = _matmul(cqn, p["w_uq"], "nn", F32, name="od_uq")
    k_raw = _matmul(ckvn, p["w_uk"], "nn", F32, name="od_uk")
    v = _matmul(ckvn, p["w_uv"], "nn", BF16, name="od_uv")
    q = _rope_call("od_rope_q", lambda qv, c, lo, hi: _rope(qv, c, lo, hi) * scale, m, tp, [(q_raw, None)], tables,
                   BF16)
    kr_col = (Q_LORA + KV_LORA) // HEAD_PAD
    k = _rope_call("od_rope_k", lambda kv, uv, c, lo, hi: kv + _rope(pltpu.roll(uv, QK_NOPE, 1), c, lo, hi), m, tp,
                   [(k_raw, None), (u, kr_col)], tables, BF16)
    o, lse_c = _flash_fwd(q, k, v, nb, tp)
    out = _matmul(o, p["w_out"], "nn", F32, residual=x, name="od_out")
    return out, (x, h, cq, ckv, cqn, ckvn, q, k, v, o, lse_c)


def _odd_bwd(dout, p, tables, saved, m, tp, nb):
    scale = QK_HEAD ** -0.5
    x, h, cq, ckv, cqn, ckvn, q, k, v, o, lse_c = saved
    do = _matmul(dout, p["w_out"], "nt", BF16, name="od_out_dx")
    d_w_out = _matmul(o, dout, "tn", F32, name="od_out_dw")
    delta_c, lse_r, delta_r = _flash_prep(o, do, lse_c, nb, tp)
    dq = _flash_dq(q, k, v, do, lse_c, delta_c, nb, tp)
    dk, dv = _flash_dkv(q, k, v, do, lse_r, delta_r, nb, tp)
    dq_raw = _rope_call("od_rope_q_bwd", lambda d, c, lo, hi: _rope_t(d, c, lo, hi) * scale, m, tp, [(dq, None)],
                        tables, BF16)
    dkr = _rope_k_bwd(dk, tables, m, tp)
    d_w_uq = _matmul(cqn, dq_raw, "tn", F32, name="od_uq_dw")
    d_w_uk = _matmul(ckvn, dk, "tn", F32, name="od_uk_dw")
    d_w_uv = _matmul(ckvn, dv, "tn", F32, name="od_uv_dw")
    dcqn = _matmul(dq_raw, p["w_uq"], "nt", F32, name="od_uq_dx")
    dckvn = _matmul(dk, p["w_uk"], "nt", F32, name="od_uk_dx")
    dckvn = _matmul(dv, p["w_uv"], "nt", F32, residual=dckvn, name="od_uv_dx")
    dcq, d_q_norm = _rms_bwd(cq, p["q_norm"], dcqn, None, "od_q_norm_bwd")
    dckv, d_kv_norm = _rms_bwd(ckv, p["kv_norm"], dckvn, None, "od_kv_norm_bwd")
    du = jnp.concatenate([dcq, dckv, dkr], axis=1)
    d_w_in = _matmul(h, du, "tn", F32, name="od_in_dw")
    dh = _matmul(du, p["w_in"], "nt", F32, name="od_in_dx")
    dx, d_norm = _rms_bwd(x, p["norm"], dh, dout, "od_norm_bwd")
    return dx, dict(norm=d_norm, w_in=d_w_in, q_norm=d_q_norm, kv_norm=d_kv_norm, w_uq=d_w_uq, w_uk=d_w_uk,
                    w_uv=d_w_uv, w_out=d_w_out)


def _loss_head(hf, g, target, tp, t_real):
    m, c = hf.shape
    tm = _pick(tp, ROW_TILE)
    tps = tp // tm

    def body(x_ref, g_ref, t_ref, dx_ref, dg_ref, loss_ref):
        i = pl.program_id(0)
        xf = x_ref[...]
        r = lax.rsqrt(jnp.mean(xf * xf, axis=-1, keepdims=True) + EPS)
        xn = xf * r
        t_pos = lax.rem(i, tps) * tm + lax.broadcasted_iota(jnp.int32, (tm, 1), 0)
        valid = jnp.logical_and(t_pos >= N_META, t_pos < t_real)
        err = jnp.where(valid, xn * g_ref[...] - t_ref[...], 0.0)
        dyf = err * (1.0 / c)
        dyg = dyf * g_ref[...]
        dx_ref[...] = r * (dyg - xn * jnp.mean(dyg * xn, axis=-1, keepdims=True))

        @pl.when(i == 0)
        def _():
            dg_ref[...] = jnp.zeros_like(dg_ref)
            loss_ref[...] = jnp.zeros_like(loss_ref)

        dg_ref[...] += jnp.sum(dyf * xn, axis=0, keepdims=True)
        loss_ref[...] += (0.5 / c) * jnp.sum(jnp.sum(err * err, axis=1, keepdims=True), axis=0, keepdims=True)

    row = pl.BlockSpec((tm, c), lambda i: (i, 0))
    vec = pl.BlockSpec((1, c), lambda i: (0, 0))
    return pl.pallas_call(
        body, out_shape=(jax.ShapeDtypeStruct((m, c), F32), jax.ShapeDtypeStruct((1, c), F32),
                         jax.ShapeDtypeStruct((1, 1), F32)),
        grid=(m // tm,), in_specs=[row, vec, row], out_specs=(row, vec, pl.BlockSpec((1, 1), lambda i: (0, 0))),
        compiler_params=_cparams(1), name="loss_head")(hf, g, target)


def _meta_grad(dh0, nb, tp):
    d = dh0.shape[1]

    def body(x_ref, o_ref):
        @pl.when(pl.program_id(0) == 0)
        def _():
            o_ref[...] = jnp.zeros_like(o_ref)

        o_ref[...] += x_ref[...]

    return pl.pallas_call(
        body, out_shape=jax.ShapeDtypeStruct((N_META, d), F32), grid=(nb,),
        in_specs=[pl.BlockSpec((N_META, d), lambda b: (b * (tp // N_META), 0))],
        out_specs=pl.BlockSpec((N_META, d), lambda b: (0, 0)), compiler_params=_cparams(1), name="meta_grad")(dh0)


def _mesh_pos():
    x, y, c = lax.axis_index("x"), lax.axis_index("y"), lax.axis_index("c")
    return x, y, c


def _peer(x, y, c, k):
    px = 1 - x if k & 4 else x
    py = 1 - y if k & 2 else y
    pc = 1 - c if k & 1 else c
    return (px, py, pc), 4 * px + 2 * py + pc


def _all_gather(big, small):
    def body(big_ref, small_ref, obig_ref, osmall_ref, send_sems, recv_sems, local_sems):
        x, y, c = _mesh_pos()
        me = 4 * x + 2 * y + c
        srcs = (big_ref, small_ref)
        outs = (obig_ref, osmall_ref)
        local = [pltpu.make_async_copy(srcs[a], outs[a].at[me], local_sems.at[a]) for a in range(2)]
        for cp in local:
            cp.start()
        sends = []
        for k in range(1, N_DEV):
            peer, _ = _peer(x, y, c, k)
            for a in range(2):
                sends.append(pltpu.make_async_remote_copy(
                    src_ref=srcs[a], dst_ref=outs[a].at[me], send_sem=send_sems.at[a, k - 1],
                    recv_sem=recv_sems.at[a, k - 1], device_id=peer, device_id_type=pl.DeviceIdType.MESH))
        for cp in sends:
            cp.start()
        for k in range(1, N_DEV):
            peer, peer_id = _peer(x, y, c, k)
            for a in range(2):
                pltpu.make_async_remote_copy(
                    src_ref=srcs[a], dst_ref=outs[a].at[peer_id], send_sem=send_sems.at[a, k - 1],
                    recv_sem=recv_sems.at[a, k - 1], device_id=peer, device_id_type=pl.DeviceIdType.MESH).wait_recv()
        for cp in sends:
            cp.wait_send()
        for cp in local:
            cp.wait()

    any_spec = pl.BlockSpec(memory_space=pl.ANY)
    return pl.pallas_call(
        body, out_shape=(jax.ShapeDtypeStruct((N_DEV,) + big.shape, big.dtype),
                         jax.ShapeDtypeStruct((N_DEV,) + small.shape, small.dtype)),
        in_specs=[any_spec, any_spec], out_specs=(any_spec, any_spec),
        scratch_shapes=[pltpu.SemaphoreType.DMA((2, N_DEV - 1)), pltpu.SemaphoreType.DMA((2, N_DEV - 1)),
                        pltpu.SemaphoreType.DMA((2,))],
        name="weight_all_gather")(big, small)


def _grad_exchange(g):
    def body(g_ref, o_ref, send_sems, recv_sems, local_sem):
        x, y, c = _mesh_pos()
        me = 4 * x + 2 * y + c
        local = pltpu.make_async_copy(g_ref.at[me], o_ref.at[me], local_sem)
        local.start()
        sends = []
        for k in range(1, N_DEV):
            peer, peer_id = _peer(x, y, c, k)
            sends.append(pltpu.make_async_remote_copy(
                src_ref=g_ref.at[peer_id], dst_ref=o_ref.at[me], send_sem=send_sems.at[k - 1],
                recv_sem=recv_sems.at[k - 1], device_id=peer, device_id_type=pl.DeviceIdType.MESH))
        for cp in sends:
            cp.start()
        for k in range(1, N_DEV):
            peer, peer_id = _peer(x, y, c, k)
            pltpu.make_async_remote_copy(
                src_ref=g_ref.at[me], dst_ref=o_ref.at[peer_id], send_sem=send_sems.at[k - 1],
                recv_sem=recv_sems.at[k - 1], device_id=peer, device_id_type=pl.DeviceIdType.MESH).wait_recv()
        for cp in sends:
            cp.wait_send()
        local.wait()

    any_spec = pl.BlockSpec(memory_space=pl.ANY)
    return pl.pallas_call(
        body, out_shape=jax.ShapeDtypeStruct(g.shape, g.dtype), in_specs=[any_spec], out_specs=any_spec,
        scratch_shapes=[pltpu.SemaphoreType.DMA((N_DEV - 1,)), pltpu.SemaphoreType.DMA((N_DEV - 1,)),
                        pltpu.SemaphoreType.DMA],
        name="grad_exchange")(g)


def _reduce_adamw(parts, w, mom, vel):
    _, r, c = parts.shape
    tr = _pick(r, 128)
    c1 = 1.0 - ADAM_B1 ** ADAM_STEP
    c2 = 1.0 - ADAM_B2 ** ADAM_STEP

    def body(p_ref, w_ref, m_ref, v_ref, g_ref, d_ref, mo_ref, vo_ref):
        g = p_ref[0]
        for s in range(1, N_DEV):
            g = g + p_ref[s]
        mn = ADAM_B1 * m_ref[...] + (1.0 - ADAM_B1) * g
        vn = ADAM_B2 * v_ref[...] + (1.0 - ADAM_B2) * (g * g)
        m_hat = mn / c1
        v_hat = vn / c2
        g_ref[...] = g
        d_ref[...] = -ADAM_LR * (m_hat / (jnp.sqrt(v_hat) + ADAM_EPS) + ADAM_WD * w_ref[...])
        mo_ref[...] = mn
        vo_ref[...] = vn

    blk = pl.BlockSpec((tr, c), lambda i: (i, 0))
    shp = jax.ShapeDtypeStruct((r, c), F32)
    return pl.pallas_call(
        body, out_shape=(shp, shp, shp, shp), grid=(r // tr,),
        in_specs=[pl.BlockSpec((N_DEV, tr, c), lambda i: (0, i, 0)), blk, blk, blk], out_specs=(blk, blk, blk, blk),
        compiler_params=_cparams(1), name="reduce_adamw")(parts, w, mom, vel)


def _pack_rows(pieces, width, row_multiple, dtype):
    flat = jnp.concatenate([p.astype(dtype).reshape(-1) for p in pieces])
    rows = -(-flat.shape[0] // (width * row_multiple)) * row_multiple
    return jnp.pad(flat, (0, rows * width - flat.shape[0])).reshape(rows, width)


def _unshard(gathered, axis):
    moved = jnp.moveaxis(gathered, 0, axis)
    shape = list(moved.shape)
    shape[axis:axis + 2] = [shape[axis] * shape[axis + 1]]
    return moved.reshape(shape)


def _to_slots(full, axis):
    shape = list(full.shape)
    shape[axis:axis + 1] = [N_DEV, shape[axis] // N_DEV]
    return jnp.moveaxis(full.reshape(shape), axis, 0).reshape(N_DEV, -1)


def _block_diag(w):
    hh, d, _ = w.shape
    eye = jnp.eye(hh, dtype=w.dtype)
    return (w[:, :, None, :] * eye[:, None, :, None]).reshape(hh * d, hh * d)


def _block_diag_t(full, hh):
    d = full.shape[0] // hh
    f4 = full.reshape(hh, d, hh, d)
    return jnp.stack([f4[i, :, i, :] for i in range(hh)], axis=0)


def _pad_heads(w, width):
    r = w.shape[0]
    w3 = w.reshape(r, MLA_HEADS, width)
    return jnp.pad(w3, ((0, 0), (0, 0), (0, HEAD_PAD - width))).reshape(r, MLA_HEADS * HEAD_PAD)


def _unpad_heads(w, width):
    r = w.shape[0]
    return w.reshape(r, MLA_HEADS, HEAD_PAD)[:, :, :width].reshape(r, MLA_HEADS * width)


def kernel(x, meta_tokens, ev_norm, ev_w_in, ev_conv_a, ev_conv_b, ev_conv_b_bias, ev_gate_r_w, ev_gate_r_b, ev_gate_i_w, ev_gate_i_b, ev_lru_lambda, ev_w_out, od_norm, od_w_in, od_q_norm, od_kv_norm, od_w_uq, od_w_ukv, od_w_out, ffn_norm, ffn_w_up, ffn_conv_w, ffn_conv_b, ffn_w_down, final_norm, loss_target, m_meta_tokens, m_ev_norm, m_ev_w_in, m_ev_conv_a, m_ev_conv_b, m_ev_conv_b_bias, m_ev_gate_r_w, m_ev_gate_r_b, m_ev_gate_i_w, m_ev_gate_i_b, m_ev_lru_lambda, m_ev_w_out, m_od_norm, m_od_w_in, m_od_q_norm, m_od_kv_norm, m_od_w_uq, m_od_w_ukv, m_od_w_out, m_ffn_norm, m_ffn_w_up, m_ffn_conv_w, m_ffn_conv_b, m_ffn_w_down, m_final_norm, v_meta_tokens, v_ev_norm, v_ev_w_in, v_ev_conv_a, v_ev_conv_b, v_ev_conv_b_bias, v_ev_gate_r_w, v_ev_gate_r_b, v_ev_gate_i_w, v_ev_gate_i_b, v_ev_lru_lambda, v_ev_w_out, v_od_norm, v_od_w_in, v_od_q_norm, v_od_kv_norm, v_od_w_uq, v_od_w_ukv, v_od_w_out, v_ffn_norm, v_ffn_w_up, v_ffn_conv_w, v_ffn_conv_b, v_ffn_w_down, v_final_norm):
    given = dict(locals())
    names = [n for n, _ in PARAMS]
    axis_of = dict(PARAMS)
    w_loc = {n: given[n] for n in names}
    m_loc = {n: given["m_" + n] for n in names}
    v_loc = {n: given["v_" + n] for n in names}
    sharded = [n for n in names if axis_of[n] is not None]
    replicated = [n for n in names if axis_of[n] is None]
    small = [n for n in sharded if n not in BIG]

    nb, seq, d = x.shape
    t_real = N_META + seq
    tp = -(-t_real // ROW_TILE) * ROW_TILE
    m = nb * tp

    big_pack = _pack_rows([w_loc[n] for n in BIG], 1024, 16, BF16)
    small_pack = _pack_rows([w_loc[n] for n in small], LANES, SUBLANES, F32)
    big_all, small_all = _all_gather(big_pack, small_pack)
    full = {n: w_loc[n] for n in replicated}
    for group, gathered in ((BIG, big_all), (small, small_all)):
        flat = gathered.reshape(N_DEV, -1)
        off = 0
        for n in group:
            shard = w_loc[n].shape
            size = math.prod(shard)
            full[n] = _unshard(flat[:, off:off + size].reshape((N_DEV,) + shard), axis_of[n])
            off += size

    tables = _rope_tables(tp)

    def even_params(j):
        w_out = full["ev_w_out"][j]
        return dict(norm=full["ev_norm"][j][None], w_in=full["ev_w_in"][j], conv_a=full["ev_conv_a"][j],
                    conv_b=full["ev_conv_b"][j], conv_b_bias=full["ev_conv_b_bias"][j][None],
                    gate_r=_block_diag(full["ev_gate_r_w"][j]).astype(BF16),
                    gate_i=_block_diag(full["ev_gate_i_w"][j]).astype(BF16),
                    gate_r_b=full["ev_gate_r_b"][j][None], gate_i_b=full["ev_gate_i_b"][j][None],
                    lam=full["ev_lru_lambda"][j][None], w_out=w_out, w_out_a=w_out[:LRU_WIDTH],
                    w_out_b=w_out[LRU_WIDTH:])

    def odd_params(j):
        w_ukv = full["od_w_ukv"][j].reshape(KV_LORA, MLA_HEADS, QK_NOPE + V_HEAD)
        w_uk = w_ukv[:, :, :QK_NOPE].reshape(KV_LORA, MLA_HEADS * QK_NOPE)
        w_uv = w_ukv[:, :, QK_NOPE:].reshape(KV_LORA, MLA_HEADS * V_HEAD)
        w_out = full["od_w_out"][j].reshape(MLA_HEADS, V_HEAD, d)
        w_out = jnp.pad(w_out, ((0, 0), (0, HEAD_PAD - V_HEAD), (0, 0))).reshape(MLA_HEADS * HEAD_PAD, d)
        return dict(norm=full["od_norm"][j][None], w_in=jnp.pad(full["od_w_in"][j], ((0, 0), (0, ODD_IN_PAD - ODD_IN))),
                    q_norm=full["od_q_norm"][j][None], kv_norm=full["od_kv_norm"][j][None],
                    w_uq=_pad_heads(full["od_w_uq"][j], QK_HEAD), w_uk=_pad_heads(w_uk, QK_NOPE),
                    w_uv=_pad_heads(w_uv, V_HEAD), w_out=w_out)

    def ffn_params(layer):
        w_up = full["ffn_w_up"][layer]
        return dict(norm=full["ffn_norm"][layer][None], w_up=w_up, w_up_a=w_up[:, :D_FF], w_up_g=w_up[:, D_FF:],
                    conv_w=full["ffn_conv_w"][layer], conv_b=full["ffn_conv_b"][layer][None],
                    w_down=full["ffn_w_down"][layer])

    meta = jnp.broadcast_to(full["meta_tokens"][None], (nb, N_META, d))
    h0 = jnp.concatenate([meta, x, jnp.zeros((nb, tp - t_real, d), F32)], axis=1).reshape(m, d)
    hcur = h0
    tape = []
    for layer in range(4):
        j = layer // 2
        if layer % 2 == 0:
            mp = even_params(j)
            hcur, saved = _even_fwd(hcur, mp, m, tp, nb)
        else:
            mp = odd_params(j)
            hcur, saved = _odd_fwd(hcur, mp, tables, m, tp, nb)
        fp = ffn_params(layer)
        hcur, fsaved = _ffn_fwd(hcur, fp, m, tp)
        tape.append((mp, saved, fp, fsaved))

    target = jnp.pad(loss_target, ((0, 0), (N_META, tp - t_real), (0, 0))).reshape(m, d)
    dh, d_final_norm, loss_part = _loss_head(hcur, full["final_norm"][None], target, tp, t_real)
    loss = lax.psum(loss_part[0, 0], ("x", "y", "c"))

    grads = {"final_norm": d_final_norm[0]}
    ev_g, od_g, ffn_g = [None, None], [None, None], [None] * 4
    for layer in reversed(range(4)):
        mp, saved, fp, fsaved = tape[layer]
        dh, ffn_g[layer] = _ffn_bwd(dh, fp, fsaved, m, tp)
        if layer % 2 == 0:
            dh, ev_g[layer // 2] = _even_bwd(dh, mp, saved, m, tp, nb)
        else:
            dh, od_g[layer // 2] = _odd_bwd(dh, mp, tables, saved, m, tp, nb)

    dh3 = dh.reshape(nb, tp, d)
    grad_x = dh3[:, N_META:t_real]
    grads["meta_tokens"] = _meta_grad(dh, nb, tp)

    def stack(lst, key, fn=lambda a: a):
        return jnp.stack([fn(g[key]) for g in lst], axis=0)

    grads["ev_norm"] = stack(ev_g, "norm", lambda a: a[0])
    grads["ev_w_in"] = stack(ev_g, "w_in")
    grads["ev_conv_a"] = stack(ev_g, "conv_a")
    grads["ev_conv_b"] = stack(ev_g, "conv_b")
    grads["ev_conv_b_bias"] = stack(ev_g, "conv_b_bias", lambda a: a[0])
    grads["ev_gate_r_w"] = stack(ev_g, "gate_r", lambda a: _block_diag_t(a, 8))
    grads["ev_gate_r_b"] = stack(ev_g, "gate_r_b", lambda a: a[0])
    grads["ev_gate_i_w"] = stack(ev_g, "gate_i", lambda a: _block_diag_t(a, 8))
    grads["ev_gate_i_b"] = stack(ev_g, "gate_i_b", lambda a: a[0])
    grads["ev_lru_lambda"] = stack(ev_g, "lam", lambda a: a[0])
    grads["ev_w_out"] = stack(ev_g, "w_out")
    grads["od_norm"] = stack(od_g, "norm", lambda a: a[0])
    grads["od_w_in"] = stack(od_g, "w_in", lambda a: a[:, :ODD_IN])
    grads["od_q_norm"] = stack(od_g, "q_norm", lambda a: a[0])
    grads["od_kv_norm"] = stack(od_g, "kv_norm", lambda a: a[0])
    grads["od_w_uq"] = stack(od_g, "w_uq", lambda a: _unpad_heads(a, QK_HEAD))

    def ukv(g):
        gk = g["w_uk"].reshape(KV_LORA, MLA_HEADS, HEAD_PAD)[:, :, :QK_NOPE]
        gv = g["w_uv"].reshape(KV_LORA, MLA_HEADS, HEAD_PAD)[:, :, :V_HEAD]
        return jnp.concatenate([gk, gv], axis=2).reshape(KV_LORA, MLA_HEADS * (QK_NOPE + V_HEAD))

    grads["od_w_ukv"] = jnp.stack([ukv(g) for g in od_g], axis=0)
    grads["od_w_out"] = stack(od_g, "w_out", lambda a: a.reshape(MLA_HEADS, HEAD_PAD, d)[:, :V_HEAD].reshape(-1, d))
    grads["ffn_norm"] = stack(ffn_g, "norm", lambda a: a[0])
    grads["ffn_w_up"] = stack(ffn_g, "w_up")
    grads["ffn_conv_w"] = stack(ffn_g, "conv_w")
    grads["ffn_conv_b"] = stack(ffn_g, "conv_b", lambda a: a[0])
    grads["ffn_w_down"] = stack(ffn_g, "w_down")

    order = sharded + replicated
    slot_parts = [_to_slots(grads[n], axis_of[n]) for n in sharded]
    slot_parts += [jnp.broadcast_to(grads[n].reshape(1, -1), (N_DEV, grads[n].size)) for n in replicated]
    g_flat = jnp.concatenate(slot_parts, axis=1)
    n_flat = g_flat.shape[1]
    rows = -(-n_flat // (1024 * 128)) * 128
    g_send = jnp.pad(g_flat, ((0, 0), (0, rows * 1024 - n_flat))).reshape(N_DEV, rows, 1024)
    parts = _grad_exchange(g_send)

    def flat_local(tree):
        flat = jnp.concatenate([tree[n].reshape(-1) for n in order])
        return jnp.pad(flat, (0, rows * 1024 - n_flat)).reshape(rows, 1024)

    g_red, delta, new_m, new_v = _reduce_adamw(parts, flat_local(w_loc), flat_local(m_loc), flat_local(v_loc))

    def unflat(arr):
        flat = arr.reshape(-1)
        out, off = {}, 0
        for n in order:
            size = w_loc[n].size
            out[n] = flat[off:off + size].reshape(w_loc[n].shape)
            off += size
        return out

    g_out, d_out, m_out, v_out = unflat(g_red), unflat(delta), unflat(new_m), unflat(new_v)
    return (loss, grad_x, *[g_out[n] for n in names], *[d_out[n] for n in names], *[m_out[n] for n in names],
            *[v_out[n] for n in names])
```

```python
import functools
import math

import jax
import jax.numpy as jnp
from jax import lax
from jax.experimental import pallas as pl
from jax.experimental.pallas import tpu as pltpu

F32 = jnp.float32
BF16 = jnp.bfloat16

N_DEV = 8
N_META = 16
EPS = 1e-6
LRU_C = 8.0
MLA_HEADS = 16
QK_NOPE = 64
QK_ROPE = 32
QK_HEAD = QK_NOPE + QK_ROPE
V_HEAD = 64
HEAD_PAD = 128
Q_LORA = 384
KV_LORA = 256
ODD_IN = Q_LORA + KV_LORA + QK_ROPE
ODD_IN_PAD = 768
ROPE_BASE = 10000.0
LRU_WIDTH = 512
D_FF = 2816

ADAM_LR = 0.001
ADAM_B1 = 0.9
ADAM_B2 = 0.999
ADAM_EPS = 1e-08
ADAM_WD = 0.01
ADAM_STEP = 10

ROW_TILE = 384
SUBLANES = 8
LANES = 128
VMEM_LIMIT = 48 * 1024 * 1024
NEG = -1e30

PARAMS = (
    ("meta_tokens", 1), ("ev_norm", None), ("ev_w_in", 2), ("ev_conv_a", 2), ("ev_conv_b", 2),
    ("ev_conv_b_bias", None), ("ev_gate_r_w", None), ("ev_gate_r_b", None), ("ev_gate_i_w", None),
    ("ev_gate_i_b", None), ("ev_lru_lambda", None), ("ev_w_out", 1), ("od_norm", 1), ("od_w_in", 1),
    ("od_q_norm", 1), ("od_kv_norm", 1), ("od_w_uq", 2), ("od_w_ukv", 2), ("od_w_out", 1),
    ("ffn_norm", None), ("ffn_w_up", 2), ("ffn_conv_w", 2), ("ffn_conv_b", None), ("ffn_w_down", 1),
    ("final_norm", None),
)
BIG = ("ev_w_in", "ev_w_out", "od_w_in", "od_w_uq", "od_w_ukv", "od_w_out", "ffn_w_up", "ffn_w_down")


def _cparams(n_grid):
    return pltpu.CompilerParams(dimension_semantics=("arbitrary",) * n_grid, vmem_limit_bytes=VMEM_LIMIT)


def _pick(dim, target):
    if dim <= target:
        return dim
    best = None
    for t in range(LANES, target + 1, LANES):
        if dim % t == 0:
            best = t
    assert best is not None, (dim, target)
    return best


MATMUL_VMEM_BUDGET = 30 * 1024 * 1024
HBM_BYTES_PER_US = 3.0e6
GRID_STEP_US = 0.35


def _tile_candidates(dim):
    return [t for t in range(LANES, dim + 1, LANES) if dim % t == 0] or [dim]


def _matmul_tiles(m, n, k, sa, sb, so, sr):
    best, best_cost = None, None
    for tm in _tile_candidates(m):
        for tn in _tile_candidates(n):
            for tk in _tile_candidates(k):
                vmem = 2 * (tm * tk * sa + tk * tn * sb) + tm * tn * (4 + 2 * so + 2 * sr)
                vmem += (tm * tk * 2 if sa > 2 else 0) + (tk * tn * 2 if sb > 2 else 0)
                if vmem > MATMUL_VMEM_BUDGET:
                    continue
                traffic = m * k * sa * (n // tn) + k * n * sb * (m // tm) + m * n * (so + sr)
                cost = traffic / HBM_BYTES_PER_US + (m // tm) * (n // tn) * (k // tk) * GRID_STEP_US
                if best_cost is None or cost < best_cost:
                    best, best_cost = (tm, tn, tk), cost
    assert best is not None, (m, n, k)
    return best


def _matmul(a, b, mode, out_dtype=F32, residual=None, name="mm"):
    if mode == "nn":
        (m, k), (k2, n) = a.shape, b.shape
    elif mode == "nt":
        (m, k), (n, k2) = a.shape, b.shape
    else:
        (k, m), (k2, n) = a.shape, b.shape
    assert k == k2, (a.shape, b.shape, mode)
    tm, tn, tk = _matmul_tiles(m, n, k, a.dtype.itemsize, b.dtype.itemsize, jnp.dtype(out_dtype).itemsize,
                               0 if residual is None else residual.dtype.itemsize)
    nk = k // tk
    if mode == "tn":
        a_spec = pl.BlockSpec((tk, tm), lambda i, j, kk: (kk, i))
        dims = (((0,), (0,)), ((), ()))
    else:
        a_spec = pl.BlockSpec((tm, tk), lambda i, j, kk: (i, kk))
        dims = (((1,), (1 if mode == "nt" else 0,)), ((), ()))
    if mode == "nt":
        b_spec = pl.BlockSpec((tn, tk), lambda i, j, kk: (j, kk))
    else:
        b_spec = pl.BlockSpec((tk, tn), lambda i, j, kk: (kk, j))
    o_spec = pl.BlockSpec((tm, tn), lambda i, j, kk: (i, j))
    has_res = residual is not None

    def body(*refs):
        if has_res:
            a_ref, b_ref, r_ref, o_ref, acc_ref = refs
        else:
            a_ref, b_ref, o_ref, acc_ref = refs
        kk = pl.program_id(2)

        @pl.when(kk == 0)
        def _():
            acc_ref[...] = jnp.zeros_like(acc_ref)

        acc_ref[...] += lax.dot_general(a_ref[...].astype(BF16), b_ref[...].astype(BF16), dims,
                                        preferred_element_type=F32)

        @pl.when(kk == nk - 1)
        def _():
            out = acc_ref[...]
            if has_res:
                out = out + r_ref[...].astype(F32)
            o_ref[...] = out.astype(o_ref.dtype)

    in_specs = [a_spec, b_spec] + ([o_spec] if has_res else [])
    args = (a, b) + ((residual,) if has_res else ())
    return pl.pallas_call(
        body, out_shape=jax.ShapeDtypeStruct((m, n), out_dtype), grid=(m // tm, n // tn, nk),
        in_specs=in_specs, out_specs=o_spec, scratch_shapes=[pltpu.VMEM((tm, tn), F32)],
        compiler_params=_cparams(3), name=name)(*args)


def _rms_fwd(x, g, name):
    m, c = x.shape
    tm = _pick(m, ROW_TILE)

    def body(x_ref, g_ref, o_ref):
        xf = x_ref[...].astype(F32)
        r = lax.rsqrt(jnp.mean(xf * xf, axis=-1, keepdims=True) + EPS)
        o_ref[...] = (xf * r * g_ref[...]).astype(o_ref.dtype)

    return pl.pallas_call(
        body, out_shape=jax.ShapeDtypeStruct((m, c), BF16), grid=(m // tm,),
        in_specs=[pl.BlockSpec((tm, c), lambda i: (i, 0)), pl.BlockSpec((1, c), lambda i: (0, 0))],
        out_specs=pl.BlockSpec((tm, c), lambda i: (i, 0)), compiler_params=_cparams(1), name=name)(x, g)


def _rms_bwd(x, g, dy, residual, name):
    m, c = x.shape
    tm = _pick(m, ROW_TILE)
    has_res = residual is not None

    def body(*refs):
        if has_res:
            x_ref, g_ref, dy_ref, r_ref, dx_ref, dg_ref = refs
        else:
            x_ref, g_ref, dy_ref, dx_ref, dg_ref = refs
        xf = x_ref[...].astype(F32)
        dyf = dy_ref[...].astype(F32)
        r = lax.rsqrt(jnp.mean(xf * xf, axis=-1, keepdims=True) + EPS)
        xn = xf * r
        dyg = dyf * g_ref[...]
        dx = r * (dyg - xn * jnp.mean(dyg * xn, axis=-1, keepdims=True))
        if has_res:
            dx = dx + r_ref[...]
        dx_ref[...] = dx

        @pl.when(pl.program_id(0) == 0)
        def _():
            dg_ref[...] = jnp.zeros_like(dg_ref)

        dg_ref[...] += jnp.sum(dyf * xn, axis=0, keepdims=True)

    row = pl.BlockSpec((tm, c), lambda i: (i, 0))
    vec = pl.BlockSpec((1, c), lambda i: (0, 0))
    in_specs = [row, vec, row] + ([row] if has_res else [])
    args = (x, g, dy) + ((residual,) if has_res else ())
    return pl.pallas_call(
        body, out_shape=(jax.ShapeDtypeStruct((m, c), F32), jax.ShapeDtypeStruct((1, c), F32)), grid=(m // tm,),
        in_specs=in_specs, out_specs=(row, vec), compiler_params=_cparams(1), name=name)(*args)


def _chan_call(name, fn, m, tp, tc, ncol, row_ins=(), prev_ins=(), next_ins=(), chan_ins=(), row_outs=(),
               red_outs=()):
    tm = _pick(tp, ROW_TILE)
    tps = tp // tm
    nrow = m // tm
    h8 = tm // SUBLANES
    last8 = m // SUBLANES - 1
    n_in = len(row_ins) + len(prev_ins) + len(next_ins) + len(chan_ins)
    n_r, n_p, n_n = len(row_ins), len(prev_ins), len(next_ins)

    def body(*refs):
        i = pl.program_id(1)
        pos = lax.rem(i, tps)
        at_start = pos == 0
        at_end = pos == tps - 1
        rows = [r[...].astype(F32) for r in refs[:n_r]]
        prevs = [jnp.where(at_start, 0.0, r[...].astype(F32)) for r in refs[n_r:n_r + n_p]]
        nexts = [jnp.where(at_end, 0.0, r[...].astype(F32)) for r in refs[n_r + n_p:n_r + n_p + n_n]]
        chans = [r[...] for r in refs[n_r + n_p + n_n:n_in]]
        out_refs = refs[n_in:n_in + len(row_outs)]
        red_refs = refs[n_in + len(row_outs):]
        row_vals, red_vals = fn(rows, prevs, nexts, chans)
        for ref, val in zip(out_refs, row_vals):
            ref[...] = val.astype(ref.dtype)
        if red_refs:
            @pl.when(i == 0)
            def _():
                for ref in red_refs:
                    ref[...] = jnp.zeros_like(ref)

            for ref, val in zip(red_refs, red_vals):
                ref[...] += val

    in_specs, args = [], []
    for arr, off in row_ins:
        in_specs.append(pl.BlockSpec((tm, tc), lambda j, i, off=off: (i, j + off)))
        args.append(arr)
    for arr, off in prev_ins:
        in_specs.append(pl.BlockSpec((SUBLANES, tc), lambda j, i, off=off: (jnp.maximum(i * h8 - 1, 0), j + off)))
        args.append(arr)
    for arr, off in next_ins:
        in_specs.append(pl.BlockSpec((SUBLANES, tc), lambda j, i, off=off: (jnp.minimum((i + 1) * h8, last8), j + off)))
        args.append(arr)
    for arr, off in chan_ins:
        in_specs.append(pl.BlockSpec((arr.shape[0], tc), lambda j, i, off=off: (0, j + off)))
        args.append(arr)
    out_shape, out_specs = [], []
    for (dt,) in row_outs:
        out_shape.append(jax.ShapeDtypeStruct((m, ncol * tc), dt))
        out_specs.append(pl.BlockSpec((tm, tc), lambda j, i: (i, j)))
    for (k,) in red_outs:
        out_shape.append(jax.ShapeDtypeStruct((k, ncol * tc), F32))
        out_specs.append(pl.BlockSpec((k, tc), lambda j, i: (0, j)))
    return pl.pallas_call(
        body, out_shape=tuple(out_shape), grid=(ncol, nrow), in_specs=in_specs, out_specs=tuple(out_specs),
        compiler_params=_cparams(2), name=name)(*args)


def _shift_down(x, prev8, s):
    if s == 0:
        return x
    xs = pltpu.roll(x, s, 0)
    ps = pltpu.roll(prev8, s, 0)
    rid = lax.broadcasted_iota(jnp.int32, prev8.shape, 0)
    head = jnp.where(rid < s, ps, xs[:SUBLANES])
    return jnp.concatenate([head, xs[SUBLANES:]], axis=0)


def _shift_up(x, next8, s):
    if s == 0:
        return x
    tm = x.shape[0]
    xs = pltpu.roll(x, tm - s, 0)
    ns = pltpu.roll(next8, SUBLANES - s, 0)
    rid = lax.broadcasted_iota(jnp.int32, next8.shape, 0)
    tail = jnp.where(rid >= SUBLANES - s, ns, xs[tm - SUBLANES:])
    return jnp.concatenate([xs[:tm - SUBLANES], tail], axis=0)


def _conv_fwd(x, prev8, w):
    kw = w.shape[0]
    y = w[kw - 1:kw, :] * x
    for k in range(kw - 1):
        y = y + w[k:k + 1, :] * _shift_down(x, prev8, kw - 1 - k)
    return y


def _conv_dw(dy, x, prev8, kw):
    rid = lax.broadcasted_iota(jnp.int32, prev8.shape, 0)
    out = jnp.zeros(prev8.shape, F32)
    for k in range(kw):
        row = jnp.sum(dy * _shift_down(x, prev8, kw - 1 - k), axis=0, keepdims=True)
        out = out + jnp.where(rid == k, row, 0.0)
    return out


def _conv_dx(dy, next8, w):
    kw = w.shape[0]
    dx = w[kw - 1:kw, :] * dy
    for k in range(kw - 1):
        dx = dx + w[k:k + 1, :] * _shift_up(dy, next8, kw - 1 - k)
    return dx


def _sigmoid(x):
    return 1.0 / (1.0 + jnp.exp(-x))


def _expm1(x):
    series = x * (1.0 + x * 0.5 * (1.0 + x * (1.0 / 3.0) * (1.0 + x * 0.25 * (1.0 + x * 0.2))))
    return jnp.where(jnp.abs(x) < 0.3, series, jnp.exp(x) - 1.0)


def _softplus_neg(lam):
    e = jnp.exp(-jnp.abs(lam))
    log1p = jnp.where(e < 1e-2, e * (1.0 - e * (0.5 - e * (1.0 / 3.0))), jnp.log(1.0 + e))
    return jnp.maximum(-lam, 0.0) + log1p


GELU_C = math.sqrt(2.0 / math.pi)


def _gelu(x):
    return 0.5 * x * (1.0 + jnp.tanh(GELU_C * (x + 0.044715 * x * x * x)))


def _gelu_grad(x):
    t = jnp.tanh(GELU_C * (x + 0.044715 * x * x * x))
    return 0.5 * (1.0 + t) + 0.5 * x * (1.0 - t * t) * GELU_C * (1.0 + 3.0 * 0.044715 * x * x)


def _ffn_fwd(x, p, m, tp):
    h = _rms_fwd(x, p["norm"], "ffn_norm")
    u = _matmul(h, p["w_up"], "nn", F32, name="ffn_up")
    tc = 256
    ncol = D_FF // tc

    def gate(rows, prevs, nexts, chans):
        ua, ug = rows
        wa, wg, ba, bg = chans
        a = _conv_fwd(ua, prevs[0], wa) + ba
        g = _conv_fwd(ug, prevs[1], wg) + bg
        return [a * _sigmoid(a) * g], []

    (z,) = _chan_call("ffn_gate", gate, m, tp, tc, ncol, row_ins=[(u, 0), (u, ncol)], prev_ins=[(u, 0), (u, ncol)],
                      chan_ins=[(p["conv_w"], 0), (p["conv_w"], ncol), (p["conv_b"], 0), (p["conv_b"], ncol)],
                      row_outs=[(BF16,)])
    out = _matmul(z, p["w_down"], "nn", F32, residual=x, name="ffn_down")
    return out, (x, h, u, z)


def _ffn_bwd(dout, p, saved, m, tp):
    x, h, u, z = saved
    tc = 256
    ncol = D_FF // tc
    dz = _matmul(dout, p["w_down"], "nt", F32, name="ffn_down_dx")
    d_w_down = _matmul(z, dout, "tn", F32, name="ffn_down_dw")

    def gate_bwd(rows, prevs, nexts, chans):
        ua, ug, dzv = rows
        wa, wg, ba, bg = chans
        a = _conv_fwd(ua, prevs[0], wa) + ba
        g = _conv_fwd(ug, prevs[1], wg) + bg
        sg = _sigmoid(a)
        da = dzv * g * (sg * (1.0 + a * (1.0 - sg)))
        dg = dzv * a * sg
        return ([da, dg],
                [_conv_dw(da, ua, prevs[0], 3), _conv_dw(dg, ug, prevs[1], 3),
                 jnp.sum(da, axis=0, keepdims=True), jnp.sum(dg, axis=0, keepdims=True)])

    da, dg, dcw_a, dcw_g, dcb_a, dcb_g = _chan_call(
        "ffn_gate_bwd", gate_bwd, m, tp, tc, ncol, row_ins=[(u, 0), (u, ncol), (dz, 0)], prev_ins=[(u, 0), (u, ncol)],
        chan_ins=[(p["conv_w"], 0), (p["conv_w"], ncol), (p["conv_b"], 0), (p["conv_b"], ncol)],
        row_outs=[(F32,), (F32,)], red_outs=[(SUBLANES,), (SUBLANES,), (1,), (1,)])

    def conv_dx(rows, prevs, nexts, chans):
        return [_conv_dx(rows[0], nexts[0], chans[0]), _conv_dx(rows[1], nexts[1], chans[1])], []

    dua, dug = _chan_call("ffn_conv_dx", conv_dx, m, tp, tc, ncol, row_ins=[(da, 0), (dg, 0)],
                          next_ins=[(da, 0), (dg, 0)], chan_ins=[(p["conv_w"], 0), (p["conv_w"], ncol)],
                          row_outs=[(BF16,), (BF16,)])
    d_w_up = jnp.concatenate([_matmul(h, dua, "tn", F32, name="ffn_up_dw_a"),
                              _matmul(h, dug, "tn", F32, name="ffn_up_dw_g")], axis=1)
    dh = _matmul(dua, p["w_up_a"], "nt", F32, name="ffn_up_dx_a")
    dh = _matmul(dug, p["w_up_g"], "nt", F32, residual=dh, name="ffn_up_dx_g")
    dx, d_norm = _rms_bwd(x, p["norm"], dh, dout, "ffn_norm_bwd")
    d_conv_w = jnp.concatenate([dcw_a[:3], dcw_g[:3]], axis=1)
    d_conv_b = jnp.concatenate([dcb_a, dcb_g], axis=1)
    return dx, dict(norm=d_norm, w_up=d_w_up, conv_w=d_conv_w, conv_b=d_conv_b, w_down=d_w_down)


def _to_scan(x, nb, tp):
    return x.reshape(nb, tp, LRU_WIDTH // LANES, LANES).transpose(1, 0, 2, 3).reshape(tp, -1, LANES)


def _from_scan(x, nb, tp):
    return x.reshape(tp, nb, LRU_WIDTH // LANES, LANES).transpose(1, 0, 2, 3).reshape(nb * tp, LRU_WIDTH)


def _scan_fwd(a, u):
    t_len, s, _ = a.shape
    tc = _pick(t_len, 640)
    blk = pl.BlockSpec((tc, s, LANES), lambda i: (i, 0, 0))

    def body(a_ref, u_ref, h_ref, carry):
        @pl.when(pl.program_id(0) == 0)
        def _():
            carry[...] = jnp.zeros_like(carry)

        def step(t, h):
            h = a_ref[t] * h + u_ref[t]
            h_ref[t] = h
            return h

        carry[...] = lax.fori_loop(0, tc, step, carry[...], unroll=8)

    return pl.pallas_call(
        body, out_shape=jax.ShapeDtypeStruct(a.shape, F32), grid=(t_len // tc,), in_specs=[blk, blk], out_specs=blk,
        scratch_shapes=[pltpu.VMEM((s, LANES), F32)], compiler_params=_cparams(1), name="lru_scan")(a, u)


def _scan_bwd(dh, a, h_prev):
    t_len, s, _ = a.shape
    tc = _pick(t_len, 640)
    nb = t_len // tc
    blk = pl.BlockSpec((tc, s, LANES), lambda i: (nb - 1 - i, 0, 0))

    def body(dh_ref, a_ref, hp_ref, du_ref, da_ref, carry):
        @pl.when(pl.program_id(0) == 0)
        def _():
            carry[...] = jnp.zeros_like(carry)

        def step(k, c):
            t = tc - 1 - k
            d = dh_ref[t] + c
            du_ref[t] = d
            da_ref[t] = d * hp_ref[t]
            return a_ref[t] * d

        carry[...] = lax.fori_loop(0, tc, step, carry[...], unroll=8)

    shp = jax.ShapeDtypeStruct(a.shape, F32)
    return pl.pallas_call(
        body, out_shape=(shp, shp), grid=(nb,), in_specs=[blk, blk, blk], out_specs=(blk, blk),
        scratch_shapes=[pltpu.VMEM((s, LANES), F32)], compiler_params=_cparams(1), name="lru_scan_bwd")(dh, a, h_prev)


def _lru_gates(xc, zr, zi, r_b, i_b, lam):
    r = _sigmoid(zr + r_b)
    ig = _sigmoid(zi + i_b)
    sp = _softplus_neg(lam)
    log_a = -LRU_C * r * sp
    a = jnp.exp(log_a)
    mult = jnp.sqrt(-_expm1(2.0 * log_a))
    return r, ig, sp, a, mult


def _even_fwd(x, p, m, tp, nb):
    c = LRU_WIDTH
    h = _rms_fwd(x, p["norm"], "ev_norm")
    u = _matmul(h, p["w_in"], "nn", F32, name="ev_in")

    def pre(rows, prevs, nexts, chans):
        gb, gc, xa, xb = rows
        wa, wb, bias = chans
        pa = gc * xa
        ya = gb * _conv_fwd(pa, prevs[0] * prevs[1], wa)
        xc = _conv_fwd(xb, prevs[2], wb) + bias
        return [ya, xc], []

    ya, xc = _chan_call("ev_pre", pre, m, tp, c, 1, row_ins=[(u, 0), (u, 1), (u, 2), (u, 3)],
                        prev_ins=[(u, 1), (u, 2), (u, 3)],
                        chan_ins=[(p["conv_a"], 0), (p["conv_b"], 0), (p["conv_b_bias"], 0)],
                        row_outs=[(BF16,), (F32,)])
    zr = _matmul(xc, p["gate_r"], "nn", F32, name="ev_gate_r")
    zi = _matmul(xc, p["gate_i"], "nn", F32, name="ev_gate_i")

    def lru_in(rows, prevs, nexts, chans):
        xcv, zrv, ziv = rows
        r, ig, sp, a, mult = _lru_gates(xcv, zrv, ziv, *chans)
        return [a, mult * (ig * xcv)], []

    a, uu = _chan_call("ev_lru_in", lru_in, m, tp, c, 1, row_ins=[(xc, 0), (zr, 0), (zi, 0)],
                       chan_ins=[(p["gate_r_b"], 0), (p["gate_i_b"], 0), (p["lam"], 0)],
                       row_outs=[(F32,), (F32,)])
    a_s = _to_scan(a, nb, tp)
    hs_s = _scan_fwd(a_s, _to_scan(uu, nb, tp))
    hs = _from_scan(hs_s, nb, tp)

    def post(rows, prevs, nexts, chans):
        gate, hv = rows
        return [_gelu(gate) * hv], []

    (yb,) = _chan_call("ev_post", post, m, tp, c, 1, row_ins=[(u, 4), (hs, 0)], row_outs=[(BF16,)])
    out = _matmul(ya, p["w_out_a"], "nn", F32, residual=x, name="ev_out_a")
    out = _matmul(yb, p["w_out_b"], "nn", F32, residual=out, name="ev_out_b")
    return out, (x, h, u, ya, xc, zr, zi, a_s, hs_s, hs, yb)


def _even_bwd(dout, p, saved, m, tp, nb):
    c = LRU_WIDTH
    x, h, u, ya, xc, zr, zi, a_s, hs_s, hs, yb = saved
    dy = _matmul(dout, p["w_out"], "nt", F32, name="ev_out_dx")
    d_w_out = jnp.concatenate([_matmul(ya, dout, "tn", F32, name="ev_out_dw_a"),
                               _matmul(yb, dout, "tn", F32, name="ev_out_dw_b")], axis=0)

    def post_bwd(rows, prevs, nexts, chans):
        dyb, gate, hv = rows
        return [dyb * hv * _gelu_grad(gate), dyb * _gelu(gate)], []

    dgate, dhs = _chan_call("ev_post_bwd", post_bwd, m, tp, c, 1, row_ins=[(dy, 1), (u, 4), (hs, 0)],
                            row_outs=[(F32,), (F32,)])
    h_prev = jnp.concatenate([jnp.zeros_like(hs_s[:1]), hs_s[:-1]], axis=0)
    du_s, da_s = _scan_bwd(_to_scan(dhs, nb, tp), a_s, h_prev)
    du = _from_scan(du_s, nb, tp)
    da = _from_scan(da_s, nb, tp)

    def lru_in_bwd(rows, prevs, nexts, chans):
        duv, dav, xcv, zrv, ziv = rows
        r, ig, sp, a, mult = _lru_gates(xcv, zrv, ziv, *chans)
        dxc = duv * mult * ig
        dig = duv * mult * xcv
        dmult = duv * ig * xcv
        dlog_a = dav * a - dmult * (a * a) / jnp.maximum(mult, 1e-30)
        dr = dlog_a * (-LRU_C * sp)
        dzr = dr * r * (1.0 - r)
        dzi = dig * ig * (1.0 - ig)
        dsp = jnp.sum(dlog_a * (-LRU_C * r), axis=0, keepdims=True)
        dlam = -dsp * _sigmoid(-chans[2])
        return ([dzr, dzi, dxc],
                [jnp.sum(dzr, axis=0, keepdims=True), jnp.sum(dzi, axis=0, keepdims=True), dlam])

    dzr, dzi, dxc, d_r_b, d_i_b, d_lam = _chan_call(
        "ev_lru_in_bwd", lru_in_bwd, m, tp, c, 1, row_ins=[(du, 0), (da, 0), (xc, 0), (zr, 0), (zi, 0)],
        chan_ins=[(p["gate_r_b"], 0), (p["gate_i_b"], 0), (p["lam"], 0)],
        row_outs=[(F32,), (F32,), (F32,)], red_outs=[(1,), (1,), (1,)])
    d_gate_r = _matmul(xc, dzr, "tn", F32, name="ev_gate_r_dw")
    d_gate_i = _matmul(xc, dzi, "tn", F32, name="ev_gate_i_dw")
    dxc = _matmul(dzr, p["gate_r"], "nt", F32, residual=dxc, name="ev_gate_r_dx")
    dxc = _matmul(dzi, p["gate_i"], "nt", F32, residual=dxc, name="ev_gate_i_dx")

    def conv_b_bwd(rows, prevs, nexts, chans):
        dxcv, xb = rows
        return ([_conv_dx(dxcv, nexts[0], chans[0])],
                [_conv_dw(dxcv, xb, prevs[0], 4), jnp.sum(dxcv, axis=0, keepdims=True)])

    dxb, d_conv_b, d_bias = _chan_call(
        "ev_conv_b_bwd", conv_b_bwd, m, tp, c, 1, row_ins=[(dxc, 0), (u, 3)], prev_ins=[(u, 3)], next_ins=[(dxc, 0)],
        chan_ins=[(p["conv_b"], 0)], row_outs=[(F32,)], red_outs=[(SUBLANES,), (1,)])

    def mix_a_bwd(rows, prevs, nexts, chans):
        dya, gb, gc, xa = rows
        (wa,) = chans
        pa = gc * xa
        pa_prev = prevs[0] * prevs[1]
        ca = _conv_fwd(pa, pa_prev, wa)
        dca = dya * gb
        dpa = _conv_dx(dca, nexts[0] * nexts[1], wa)
        return [dya * ca, dpa * xa, dpa * gc], [_conv_dw(dca, pa, pa_prev, 3)]

    dgb, dgc, dxa, d_conv_a = _chan_call(
        "ev_mix_a_bwd", mix_a_bwd, m, tp, c, 1, row_ins=[(dy, 0), (u, 0), (u, 1), (u, 2)],
        prev_ins=[(u, 1), (u, 2)], next_ins=[(dy, 0), (u, 0)], chan_ins=[(p["conv_a"], 0)],
        row_outs=[(F32,), (F32,), (F32,)], red_outs=[(SUBLANES,)])
    du_all = jnp.concatenate([dgb, dgc, dxa, dxb, dgate], axis=1)
    d_w_in = _matmul(h, du_all, "tn", F32, name="ev_in_dw")
    dh = _matmul(du_all, p["w_in"], "nt", F32, name="ev_in_dx")
    dx, d_norm = _rms_bwd(x, p["norm"], dh, dout, "ev_norm_bwd")
    return dx, dict(norm=d_norm, w_in=d_w_in, conv_a=d_conv_a[:3], conv_b=d_conv_b[:4], conv_b_bias=d_bias,
                    gate_r=d_gate_r, gate_r_b=d_r_b, gate_i=d_gate_i, gate_i_b=d_i_b, lam=d_lam, w_out=d_w_out)


def _rope_tables(tp):
    pos = jnp.arange(tp, dtype=F32)
    inv_freq = ROPE_BASE ** (-jnp.arange(0, QK_ROPE, 2, dtype=F32) / QK_ROPE)
    ang = pos[:, None] * inv_freq[None, :]
    cos, sin = jnp.cos(ang), jnp.sin(ang)
    half = QK_ROPE // 2
    one = jnp.ones((tp, QK_NOPE), F32)
    z64 = jnp.zeros((tp, QK_NOPE), F32)
    zh = jnp.zeros((tp, half), F32)
    zt = jnp.zeros((tp, HEAD_PAD - QK_HEAD), F32)
    c_tab = jnp.concatenate([one, cos, cos, zt], axis=1)
    s_lo = jnp.concatenate([z64, -sin, zh, zt], axis=1)
    s_hi = jnp.concatenate([z64, zh, sin, zt], axis=1)
    return c_tab, s_lo, s_hi


def _rope(v, c_tab, s_lo, s_hi):
    half = QK_ROPE // 2
    return v * c_tab + pltpu.roll(v, HEAD_PAD - half, 1) * s_lo + pltpu.roll(v, half, 1) * s_hi


def _rope_t(dv, c_tab, s_lo, s_hi):
    half = QK_ROPE // 2
    return dv * c_tab + pltpu.roll(dv * s_lo, half, 1) + pltpu.roll(dv * s_hi, HEAD_PAD - half, 1)


def _rope_call(name, fn, m, tp, ins, tables, out_dtype):
    tm = _pick(tp, ROW_TILE)
    tps = tp // tm
    n = len(ins)

    def body(*refs):
        vals = [r[...].astype(F32) for r in refs[:n]]
        tabs = [r[...] for r in refs[n:n + 3]]
        refs[n + 3][...] = fn(*vals, *tabs).astype(out_dtype)

    in_specs, args = [], []
    for arr, fixed_col in ins:
        if fixed_col is None:
            in_specs.append(pl.BlockSpec((tm, HEAD_PAD), lambda i, hh: (i, hh)))
        else:
            in_specs.append(pl.BlockSpec((tm, HEAD_PAD), lambda i, hh, fc=fixed_col: (i, fc)))
        args.append(arr)
    for tab in tables:
        in_specs.append(pl.BlockSpec((tm, HEAD_PAD), lambda i, hh: (lax.rem(i, tps), 0)))
        args.append(tab)
    return pl.pallas_call(
        body, out_shape=jax.ShapeDtypeStruct((m, MLA_HEADS * HEAD_PAD), out_dtype), grid=(m // tm, MLA_HEADS),
        in_specs=in_specs, out_specs=pl.BlockSpec((tm, HEAD_PAD), lambda i, hh: (i, hh)),
        compiler_params=_cparams(2), name=name)(*args)


def _rope_k_bwd(dk, tables, m, tp):
    tm = _pick(tp, ROW_TILE)
    tps = tp // tm

    def body(dk_ref, c_ref, lo_ref, hi_ref, o_ref):
        acc = dk_ref[:, 0:HEAD_PAD].astype(F32)
        for hh in range(1, MLA_HEADS):
            acc = acc + dk_ref[:, hh * HEAD_PAD:(hh + 1) * HEAD_PAD].astype(F32)
        d = pltpu.roll(_rope_t(acc, c_ref[...], lo_ref[...], hi_ref[...]), QK_NOPE, 1)
        lane = lax.broadcasted_iota(jnp.int32, d.shape, 1)
        o_ref[...] = jnp.where(lane < QK_ROPE, d, 0.0)

    tab = pl.BlockSpec((tm, HEAD_PAD), lambda i: (lax.rem(i, tps), 0))
    return pl.pallas_call(
        body, out_shape=jax.ShapeDtypeStruct((m, HEAD_PAD), F32), grid=(m // tm,),
        in_specs=[pl.BlockSpec((tm, MLA_HEADS * HEAD_PAD), lambda i: (i, 0)), tab, tab, tab],
        out_specs=pl.BlockSpec((tm, HEAD_PAD), lambda i: (i, 0)), compiler_params=_cparams(1),
        name="od_rope_k_bwd")(dk, *tables)


def _causal_mask(row0, col0, shape):
    rows = row0 + lax.broadcasted_iota(jnp.int32, shape, 0)
    cols = col0 + lax.broadcasted_iota(jnp.int32, shape, 1)
    return cols <= rows


NT = (((1,), (1,)), ((), ()))
TN = (((0,), (0,)), ((), ()))
HEADS_PER_STEP = 2
HEAD_STEPS = MLA_HEADS // HEADS_PER_STEP
STEP_LANES = HEADS_PER_STEP * HEAD_PAD


def _flash_fwd(q, k, v, nb, tp):
    tq = _pick(tp, ROW_TILE)
    nq = tp // tq

    def body(q_ref, k_ref, v_ref, o_ref, lse_ref):
        i = pl.program_id(2)
        qbs = [q_ref[:, hd * HEAD_PAD:(hd + 1) * HEAD_PAD] for hd in range(HEADS_PER_STEP)]

        def chunk(j, carry, masked):
            off = pl.multiple_of(j * tq, tq)
            out = []
            for hd in range(HEADS_PER_STEP):
                mx, l, acc = carry[hd]
                lanes = slice(hd * HEAD_PAD, (hd + 1) * HEAD_PAD)
                kb = k_ref[pl.ds(off, tq), lanes]
                vb = v_ref[pl.ds(off, tq), lanes]
                s = lax.dot_general(qbs[hd], kb, NT, preferred_element_type=F32)
                if masked:
                    s = jnp.where(_causal_mask(0, 0, s.shape), s, NEG)
                m_new = jnp.maximum(mx, jnp.max(s, axis=1, keepdims=True))
                alpha = jnp.exp(mx - m_new)
                pr = jnp.exp(s - m_new)
                l = alpha * l + jnp.sum(pr, axis=1, keepdims=True)
                acc = alpha * acc + jnp.dot(pr.astype(BF16), vb, preferred_element_type=F32)
                out.append((m_new, l, acc))
            return tuple(out)

        one = (jnp.full((tq, 1), NEG, F32), jnp.zeros((tq, 1), F32), jnp.zeros((tq, HEAD_PAD), F32))
        carry = lax.fori_loop(0, i, lambda j, c: chunk(j, c, False), (one,) * HEADS_PER_STEP)
        carry = chunk(i, carry, True)
        for hd in range(HEADS_PER_STEP):
            mx, l, acc = carry[hd]
            lanes = slice(hd * HEAD_PAD, (hd + 1) * HEAD_PAD)
            o_ref[:, lanes] = (acc / l).astype(o_ref.dtype)
            lse_ref[:, lanes] = jnp.broadcast_to(mx + jnp.log(l), (tq, HEAD_PAD))

    qspec = pl.BlockSpec((tq, STEP_LANES), lambda b, hh, i: (b * nq + i, hh))
    kvspec = pl.BlockSpec((tp, STEP_LANES), lambda b, hh, i: (b, hh))
    shp = (nb * tp, MLA_HEADS * HEAD_PAD)
    return pl.pallas_call(
        body, out_shape=(jax.ShapeDtypeStruct(shp, BF16), jax.ShapeDtypeStruct(shp, F32)),
        grid=(nb, HEAD_STEPS, nq), in_specs=[qspec, kvspec, kvspec], out_specs=(qspec, qspec),
        compiler_params=_cparams(3), name="od_flash_fwd")(q, k, v)


def _flash_prep(o, do, lse_c, nb, tp):
    tq = _pick(tp, ROW_TILE)
    nq = tp // tq

    def body(o_ref, do_ref, lse_ref, lr_ref, dr_ref):
        delta = jnp.sum(o_ref[...].astype(F32) * do_ref[...].astype(F32), axis=1, keepdims=True)
        lr_ref[...] = jnp.transpose(lse_ref[...])[0:SUBLANES, :]
        dr_ref[...] = jnp.transpose(jnp.broadcast_to(delta, (tq, HEAD_PAD)))[0:SUBLANES, :]

    qspec = pl.BlockSpec((tq, HEAD_PAD), lambda b, hh, i: (b * nq + i, hh))
    rspec = pl.BlockSpec((None, None, SUBLANES, tq), lambda b, hh, i: (b * MLA_HEADS + hh, i, 0, 0))
    rshape = jax.ShapeDtypeStruct((nb * MLA_HEADS, nq, SUBLANES, tq), F32)
    return pl.pallas_call(
        body, out_shape=(rshape, rshape), grid=(nb, MLA_HEADS, nq), in_specs=[qspec, qspec, qspec],
        out_specs=(rspec, rspec), compiler_params=_cparams(3), name="od_flash_prep")(o, do, lse_c)


def _flash_bwd(q, k, v, do, lse_r, delta_r, nb, tp):
    tq = _pick(tp, ROW_TILE)
    nq = tp // tq

    def body(q_ref, k_ref, v_ref, do_ref, lse_ref, dl_ref, dq_ref, dk_ref, dv_ref):
        j = pl.program_id(2)

        @pl.when(j == 0)
        def _():
            dq_ref[...] = jnp.zeros_like(dq_ref)

        kbs = [k_ref[:, hd * HEAD_PAD:(hd + 1) * HEAD_PAD] for hd in range(HEADS_PER_STEP)]
        vbs = [v_ref[:, hd * HEAD_PAD:(hd + 1) * HEAD_PAD] for hd in range(HEADS_PER_STEP)]

        def chunk(i, carry, masked):
            off = pl.multiple_of(i * tq, tq)
            out = []
            for hd in range(HEADS_PER_STEP):
                dk, dv = carry[hd]
                lanes = slice(hd * HEAD_PAD, (hd + 1) * HEAD_PAD)
                qb = q_ref[pl.ds(off, tq), lanes]
                dob = do_ref[pl.ds(off, tq), lanes]
                lse = lse_ref[hd, i][0:1, :]
                delta = dl_ref[hd, i][0:1, :]
                st = lax.dot_general(kbs[hd], qb, NT, preferred_element_type=F32)
                pt = jnp.exp(st - lse)
                if masked:
                    keys = lax.broadcasted_iota(jnp.int32, st.shape, 0)
                    queries = lax.broadcasted_iota(jnp.int32, st.shape, 1)
                    pt = jnp.where(keys <= queries, pt, 0.0)
                dv = dv + jnp.dot(pt.astype(BF16), dob, preferred_element_type=F32)
                dpt = lax.dot_general(vbs[hd], dob, NT, preferred_element_type=F32)
                dst = (pt * (dpt - delta)).astype(BF16)
                dk = dk + jnp.dot(dst, qb, preferred_element_type=F32)
                dq_ref[pl.ds(off, tq), lanes] += lax.dot_general(dst, kbs[hd], TN, preferred_element_type=F32)
                out.append((dk, dv))
            return tuple(out)

        zero = jnp.zeros((tq, HEAD_PAD), F32)
        carry = chunk(j, ((zero, zero),) * HEADS_PER_STEP, True)
        carry = lax.fori_loop(j + 1, nq, lambda i, c: chunk(i, c, False), carry)
        for hd in range(HEADS_PER_STEP):
            lanes = slice(hd * HEAD_PAD, (hd + 1) * HEAD_PAD)
            dk_ref[:, lanes] = carry[hd][0]
            dv_ref[:, lanes] = carry[hd][1].astype(dv_ref.dtype)

    tspec = pl.BlockSpec((tq, STEP_LANES), lambda b, hh, j: (b * nq + j, hh))
    fullspec = pl.BlockSpec((tp, STEP_LANES), lambda b, hh, j: (b, hh))
    rspec = pl.BlockSpec((HEADS_PER_STEP, nq, SUBLANES, tq), lambda b, hh, j: (b * HEAD_STEPS + hh, 0, 0, 0))
    shp = (nb * tp, MLA_HEADS * HEAD_PAD)
    return pl.pallas_call(
        body, out_shape=(jax.ShapeDtypeStruct(shp, F32), jax.ShapeDtypeStruct(shp, F32),
                         jax.ShapeDtypeStruct(shp, BF16)),
        grid=(nb, HEAD_STEPS, nq), in_specs=[fullspec, tspec, tspec, fullspec, rspec, rspec],
        out_specs=(fullspec, tspec, tspec), compiler_params=_cparams(3),
        name="od_flash_bwd")(q, k, v, do, lse_r, delta_r)


def _odd_fwd(x, p, tables, m, tp, nb):
    scale = QK_HEAD ** -0.5
    h = _rms_fwd(x, p["norm"], "od_norm")
    u = _matmul(h, p["w_in"], "nn", F32, name="od_in")
    cq = u[:, :Q_LORA]
    ckv = u[:, Q_LORA:Q_LORA + KV_LORA]
    cqn = _rms_fwd(cq, p["q_norm"], "od_q_norm")
    ckvn = _rms_fwd(ckv, p["kv_norm"], "od_kv_norm")
    q_raw = _matmul(cqn, p["w_uq"], "nn", F32, name="od_uq")
    k_raw = _matmul(ckvn, p["w_uk"], "nn", F32, name="od_uk")
    v = _matmul(ckvn, p["w_uv"], "nn", BF16, name="od_uv")
    q = _rope_call("od_rope_q", lambda qv, c, lo, hi: _rope(qv, c, lo, hi) * scale, m, tp, [(q_raw, None)], tables,
                   BF16)
    kr_col = (Q_LORA + KV_LORA) // HEAD_PAD
    k = _rope_call("od_rope_k", lambda kv, uv, c, lo, hi: kv + _rope(pltpu.roll(uv, QK_NOPE, 1), c, lo, hi), m, tp,
                   [(k_raw, None), (u, kr_col)], tables, BF16)
    o, lse_c = _flash_fwd(q, k, v, nb, tp)
    out = _matmul(o, p["w_out"], "nn", F32, residual=x, name="od_out")
    return out, (x, h, cq, ckv, cqn, ckvn, q, k, v, o, lse_c)


def _odd_bwd(dout, p, tables, saved, m, tp, nb):
    scale = QK_HEAD ** -0.5
    x, h, cq, ckv, cqn, ckvn, q, k, v, o, lse_c = saved
    do = _matmul(dout, p["w_out"], "nt", BF16, name="od_out_dx")
    d_w_out = _matmul(o, dout, "tn", F32, name="od_out_dw")
    lse_r, delta_r = _flash_prep(o, do, lse_c, nb, tp)
    dq, dk, dv = _flash_bwd(q, k, v, do, lse_r, delta_r, nb, tp)
    dq_raw = _rope_call("od_rope_q_bwd", lambda d, c, lo, hi: _rope_t(d, c, lo, hi) * scale, m, tp, [(dq, None)],
                        tables, BF16)
    dkr = _rope_k_bwd(dk, tables, m, tp)
    d_w_uq = _matmul(cqn, dq_raw, "tn", F32, name="od_uq_dw")
    d_w_uk = _matmul(ckvn, dk, "tn", F32, name="od_uk_dw")
    d_w_uv = _matmul(ckvn, dv, "tn", F32, name="od_uv_dw")
    dcqn = _matmul(dq_raw, p["w_uq"], "nt", F32, name="od_uq_dx")
    dckvn = _matmul(dk, p["w_uk"], "nt", F32, name="od_uk_dx")
    dckvn = _matmul(dv, p["w_uv"], "nt", F32, residual=dckvn, name="od_uv_dx")
    dcq, d_q_norm = _rms_bwd(cq, p["q_norm"], dcqn, None, "od_q_norm_bwd")
    dckv, d_kv_norm = _rms_bwd(ckv, p["kv_norm"], dckvn, None, "od_kv_norm_bwd")
    du = jnp.concatenate([dcq, dckv, dkr], axis=1)
    d_w_in = _matmul(h, du, "tn", F32, name="od_in_dw")
    dh = _matmul(du, p["w_in"], "nt", F32, name="od_in_dx")
    dx, d_norm = _rms_bwd(x, p["norm"], dh, dout, "od_norm_bwd")
    return dx, dict(norm=d_norm, w_in=d_w_in, q_norm=d_q_norm, kv_norm=d_kv_norm, w_uq=d_w_uq, w_uk=d_w_uk,
                    w_uv=d_w_uv, w_out=d_w_out)


def _loss_head(hf, g, target, tp, t_real):
    m, c = hf.shape
    tm = _pick(tp, ROW_TILE)
    tps = tp // tm

    def body(x_ref, g_ref, t_ref, dx_ref, dg_ref, loss_ref):
        i = pl.program_id(0)
        xf = x_ref[...]
        r = lax.rsqrt(jnp.mean(xf * xf, axis=-1, keepdims=True) + EPS)
        xn = xf * r
        t_pos = lax.rem(i, tps) * tm + lax.broadcasted_iota(jnp.int32, (tm, 1), 0)
        valid = jnp.logical_and(t_pos >= N_META, t_pos < t_real)
        err = jnp.where(valid, xn * g_ref[...] - t_ref[...], 0.0)
        dyf = err * (1.0 / c)
        dyg = dyf * g_ref[...]
        dx_ref[...] = r * (dyg - xn * jnp.mean(dyg * xn, axis=-1, keepdims=True))

        @pl.when(i == 0)
        def _():
            dg_ref[...] = jnp.zeros_like(dg_ref)
            loss_ref[...] = jnp.zeros_like(loss_ref)

        dg_ref[...] += jnp.sum(dyf * xn, axis=0, keepdims=True)
        loss_ref[...] += (0.5 / c) * jnp.sum(jnp.sum(err * err, axis=1, keepdims=True), axis=0, keepdims=True)

    row = pl.BlockSpec((tm, c), lambda i: (i, 0))
    vec = pl.BlockSpec((1, c), lambda i: (0, 0))
    return pl.pallas_call(
        body, out_shape=(jax.ShapeDtypeStruct((m, c), F32), jax.ShapeDtypeStruct((1, c), F32),
                         jax.ShapeDtypeStruct((1, 1), F32)),
        grid=(m // tm,), in_specs=[row, vec, row], out_specs=(row, vec, pl.BlockSpec((1, 1), lambda i: (0, 0))),
        compiler_params=_cparams(1), name="loss_head")(hf, g, target)


def _meta_grad(dh0, nb, tp):
    d = dh0.shape[1]

    def body(x_ref, o_ref):
        @pl.when(pl.program_id(0) == 0)
        def _():
            o_ref[...] = jnp.zeros_like(o_ref)

        o_ref[...] += x_ref[...]

    return pl.pallas_call(
        body, out_shape=jax.ShapeDtypeStruct((N_META, d), F32), grid=(nb,),
        in_specs=[pl.BlockSpec((N_META, d), lambda b: (b * (tp // N_META), 0))],
        out_specs=pl.BlockSpec((N_META, d), lambda b: (0, 0)), compiler_params=_cparams(1), name="meta_grad")(dh0)


def _mesh_pos():
    x, y, c = lax.axis_index("x"), lax.axis_index("y"), lax.axis_index("c")
    return x, y, c


def _peer(x, y, c, k):
    px = 1 - x if k & 4 else x
    py = 1 - y if k & 2 else y
    pc = 1 - c if k & 1 else c
    return (px, py, pc), 4 * px + 2 * py + pc


def _exchange(name, arrays, scatter):
    n = len(arrays)

    def body(*refs):
        srcs, outs = refs[:n], refs[n:2 * n]
        send_sems, recv_sems, local_sems = refs[2 * n:]
        x, y, c = _mesh_pos()
        me = 4 * x + 2 * y + c

        def src(a, to):
            return srcs[a].at[to] if scatter else srcs[a]

        local = [pltpu.make_async_copy(src(a, me), outs[a].at[me], local_sems.at[a]) for a in range(n)]
        for cp in local:
            cp.start()
        sends = []
        for k in range(1, N_DEV):
            peer, peer_id = _peer(x, y, c, k)
            for a in range(n):
                sends.append(pltpu.make_async_remote_copy(
                    src_ref=src(a, peer_id), dst_ref=outs[a].at[me], send_sem=send_sems.at[a, k - 1],
                    recv_sem=recv_sems.at[a, k - 1], device_id=peer, device_id_type=pl.DeviceIdType.MESH))
        for cp in sends:
            cp.start()
        for k in range(1, N_DEV):
            peer, peer_id = _peer(x, y, c, k)
            for a in range(n):
                pltpu.make_async_remote_copy(
                    src_ref=src(a, me), dst_ref=outs[a].at[peer_id], send_sem=send_sems.at[a, k - 1],
                    recv_sem=recv_sems.at[a, k - 1], device_id=peer, device_id_type=pl.DeviceIdType.MESH).wait_recv()
        for cp in sends:
            cp.wait_send()
        for cp in local:
            cp.wait()

    any_spec = pl.BlockSpec(memory_space=pl.ANY)
    out_shape = tuple(jax.ShapeDtypeStruct(a.shape if scatter else (N_DEV,) + a.shape, a.dtype) for a in arrays)
    return pl.pallas_call(
        body, out_shape=out_shape, in_specs=[any_spec] * n, out_specs=(any_spec,) * n,
        scratch_shapes=[pltpu.SemaphoreType.DMA((n, N_DEV - 1)), pltpu.SemaphoreType.DMA((n, N_DEV - 1)),
                        pltpu.SemaphoreType.DMA((n,))],
        name=name)(*arrays)


REDUCE_BLOCK_BYTES = 512 * 1024


def _reduce_rows(r, c):
    best = None
    for t in range(16, r + 1, 16):
        if r % t == 0 and t * c * 4 <= REDUCE_BLOCK_BYTES:
            best = t
    assert best is not None, (r, c)
    return best


def _reduce_adamw(parts, w, mom, vel):
    _, r, c = parts.shape
    tr = _reduce_rows(r, c)
    c1 = 1.0 - ADAM_B1 ** ADAM_STEP
    c2 = 1.0 - ADAM_B2 ** ADAM_STEP

    def body(p_ref, w_ref, m_ref, v_ref, g_ref, d_ref, mo_ref, vo_ref):
        g = p_ref[0].astype(F32)
        for s in range(1, N_DEV):
            g = g + p_ref[s].astype(F32)
        mn = ADAM_B1 * m_ref[...] + (1.0 - ADAM_B1) * g
        vn = ADAM_B2 * v_ref[...] + (1.0 - ADAM_B2) * (g * g)
        m_hat = mn / c1
        v_hat = vn / c2
        g_ref[...] = g
        d_ref[...] = -ADAM_LR * (m_hat / (jnp.sqrt(v_hat) + ADAM_EPS) + ADAM_WD * w_ref[...])
        mo_ref[...] = mn
        vo_ref[...] = vn

    blk = pl.BlockSpec((tr, c), lambda i: (i, 0))
    shp = jax.ShapeDtypeStruct((r, c), F32)
    return pl.pallas_call(
        body, out_shape=(shp, shp, shp, shp), grid=(r // tr,),
        in_specs=[pl.BlockSpec((N_DEV, tr, c), lambda i: (0, i, 0)), blk, blk, blk], out_specs=(blk, blk, blk, blk),
        compiler_params=_cparams(1), name="reduce_adamw")(parts, w, mom, vel)


def _pack_rows(pieces, width, row_multiple, dtype):
    flat = jnp.concatenate([p.astype(dtype).reshape(-1) for p in pieces])
    rows = -(-flat.shape[0] // (width * row_multiple)) * row_multiple
    return jnp.pad(flat, (0, rows * width - flat.shape[0])).reshape(rows, width)


def _unshard(gathered, axis):
    moved = jnp.moveaxis(gathered, 0, axis)
    shape = list(moved.shape)
    shape[axis:axis + 2] = [shape[axis] * shape[axis + 1]]
    return moved.reshape(shape)


def _to_slots(full, axis):
    shape = list(full.shape)
    shape[axis:axis + 1] = [N_DEV, shape[axis] // N_DEV]
    return jnp.moveaxis(full.reshape(shape), axis, 0)


def _block_diag(w):
    hh, d, _ = w.shape
    eye = jnp.eye(hh, dtype=w.dtype)
    return (w[:, :, None, :] * eye[:, None, :, None]).reshape(hh * d, hh * d)


def _block_diag_t(full, hh):
    d = full.shape[0] // hh
    f4 = full.reshape(hh, d, hh, d)
    return jnp.stack([f4[i, :, i, :] for i in range(hh)], axis=0)


def _pad_heads(w, width):
    r = w.shape[0]
    w3 = w.reshape(r, MLA_HEADS, width)
    return jnp.pad(w3, ((0, 0), (0, 0), (0, HEAD_PAD - width))).reshape(r, MLA_HEADS * HEAD_PAD)


def _unpad_heads(w, width):
    r = w.shape[0]
    return w.reshape(r, MLA_HEADS, HEAD_PAD)[:, :, :width].reshape(r, MLA_HEADS * width)


def kernel(x, meta_tokens, ev_norm, ev_w_in, ev_conv_a, ev_conv_b, ev_conv_b_bias, ev_gate_r_w, ev_gate_r_b, ev_gate_i_w, ev_gate_i_b, ev_lru_lambda, ev_w_out, od_norm, od_w_in, od_q_norm, od_kv_norm, od_w_uq, od_w_ukv, od_w_out, ffn_norm, ffn_w_up, ffn_conv_w, ffn_conv_b, ffn_w_down, final_norm, loss_target, m_meta_tokens, m_ev_norm, m_ev_w_in, m_ev_conv_a, m_ev_conv_b, m_ev_conv_b_bias, m_ev_gate_r_w, m_ev_gate_r_b, m_ev_gate_i_w, m_ev_gate_i_b, m_ev_lru_lambda, m_ev_w_out, m_od_norm, m_od_w_in, m_od_q_norm, m_od_kv_norm, m_od_w_uq, m_od_w_ukv, m_od_w_out, m_ffn_norm, m_ffn_w_up, m_ffn_conv_w, m_ffn_conv_b, m_ffn_w_down, m_final_norm, v_meta_tokens, v_ev_norm, v_ev_w_in, v_ev_conv_a, v_ev_conv_b, v_ev_conv_b_bias, v_ev_gate_r_w, v_ev_gate_r_b, v_ev_gate_i_w, v_ev_gate_i_b, v_ev_lru_lambda, v_ev_w_out, v_od_norm, v_od_w_in, v_od_q_norm, v_od_kv_norm, v_od_w_uq, v_od_w_ukv, v_od_w_out, v_ffn_norm, v_ffn_w_up, v_ffn_conv_w, v_ffn_conv_b, v_ffn_w_down, v_final_norm):
    given = dict(locals())
    names = [n for n, _ in PARAMS]
    axis_of = dict(PARAMS)
    w_loc = {n: given[n] for n in names}
    m_loc = {n: given["m_" + n] for n in names}
    v_loc = {n: given["v_" + n] for n in names}
    sharded = [n for n in names if axis_of[n] is not None]
    replicated = [n for n in names if axis_of[n] is None]
    small = [n for n in sharded if n not in BIG]

    nb, seq, d = x.shape
    t_real = N_META + seq
    tp = -(-t_real // ROW_TILE) * ROW_TILE
    m = nb * tp

    small_pack = _pack_rows([w_loc[n] for n in small], LANES, SUBLANES, F32)
    gathered = _exchange("weight_all_gather", [w_loc[n].astype(BF16) for n in BIG] + [small_pack], scatter=False)
    full = {n: w_loc[n] for n in replicated}
    for n, g in zip(BIG, gathered[:-1]):
        full[n] = _unshard(g, axis_of[n])
    flat = gathered[-1].reshape(N_DEV, -1)
    off = 0
    for n in small:
        shard = w_loc[n].shape
        size = math.prod(shard)
        full[n] = _unshard(flat[:, off:off + size].reshape((N_DEV,) + shard), axis_of[n])
        off += size

    tables = _rope_tables(tp)

    def even_params(j):
        w_out = full["ev_w_out"][j]
        return dict(norm=full["ev_norm"][j][None], w_in=full["ev_w_in"][j], conv_a=full["ev_conv_a"][j],
                    conv_b=full["ev_conv_b"][j], conv_b_bias=full["ev_conv_b_bias"][j][None],
                    gate_r=_block_diag(full["ev_gate_r_w"][j]).astype(BF16),
                    gate_i=_block_diag(full["ev_gate_i_w"][j]).astype(BF16),
                    gate_r_b=full["ev_gate_r_b"][j][None], gate_i_b=full["ev_gate_i_b"][j][None],
                    lam=full["ev_lru_lambda"][j][None], w_out=w_out, w_out_a=w_out[:LRU_WIDTH],
                    w_out_b=w_out[LRU_WIDTH:])

    def odd_params(j):
        w_ukv = full["od_w_ukv"][j].reshape(KV_LORA, MLA_HEADS, QK_NOPE + V_HEAD)
        w_uk = w_ukv[:, :, :QK_NOPE].reshape(KV_LORA, MLA_HEADS * QK_NOPE)
        w_uv = w_ukv[:, :, QK_NOPE:].reshape(KV_LORA, MLA_HEADS * V_HEAD)
        w_out = full["od_w_out"][j].reshape(MLA_HEADS, V_HEAD, d)
        w_out = jnp.pad(w_out, ((0, 0), (0, HEAD_PAD - V_HEAD), (0, 0))).reshape(MLA_HEADS * HEAD_PAD, d)
        return dict(norm=full["od_norm"][j][None], w_in=jnp.pad(full["od_w_in"][j], ((0, 0), (0, ODD_IN_PAD - ODD_IN))),
                    q_norm=full["od_q_norm"][j][None], kv_norm=full["od_kv_norm"][j][None],
                    w_uq=_pad_heads(full["od_w_uq"][j], QK_HEAD), w_uk=_pad_heads(w_uk, QK_NOPE),
                    w_uv=_pad_heads(w_uv, V_HEAD), w_out=w_out)

    def ffn_params(layer):
        w_up = full["ffn_w_up"][layer]
        return dict(norm=full["ffn_norm"][layer][None], w_up=w_up, w_up_a=w_up[:, :D_FF], w_up_g=w_up[:, D_FF:],
                    conv_w=full["ffn_conv_w"][layer], conv_b=full["ffn_conv_b"][layer][None],
                    w_down=full["ffn_w_down"][layer])

    meta = jnp.broadcast_to(full["meta_tokens"][None], (nb, N_META, d))
    h0 = jnp.concatenate([meta, x, jnp.zeros((nb, tp - t_real, d), F32)], axis=1).reshape(m, d)
    hcur = h0
    tape = []
    for layer in range(4):
        j = layer // 2
        if layer % 2 == 0:
            mp = even_params(j)
            hcur, saved = _even_fwd(hcur, mp, m, tp, nb)
        else:
            mp = odd_params(j)
            hcur, saved = _odd_fwd(hcur, mp, tables, m, tp, nb)
        fp = ffn_params(layer)
        hcur, fsaved = _ffn_fwd(hcur, fp, m, tp)
        tape.append((mp, saved, fp, fsaved))

    target = jnp.pad(loss_target, ((0, 0), (N_META, tp - t_real), (0, 0))).reshape(m, d)
    dh, d_final_norm, loss_part = _loss_head(hcur, full["final_norm"][None], target, tp, t_real)
    loss = lax.psum(loss_part[0, 0], ("x", "y", "c"))

    grads = {"final_norm": d_final_norm[0]}
    ev_g, od_g, ffn_g = [None, None], [None, None], [None] * 4
    for layer in reversed(range(4)):
        mp, saved, fp, fsaved = tape[layer]
        dh, ffn_g[layer] = _ffn_bwd(dh, fp, fsaved, m, tp)
        if layer % 2 == 0:
            dh, ev_g[layer // 2] = _even_bwd(dh, mp, saved, m, tp, nb)
        else:
            dh, od_g[layer // 2] = _odd_bwd(dh, mp, tables, saved, m, tp, nb)

    dh3 = dh.reshape(nb, tp, d)
    grad_x = dh3[:, N_META:t_real]
    grads["meta_tokens"] = _meta_grad(dh, nb, tp)

    def stack(lst, key, fn=lambda a: a):
        return jnp.stack([fn(g[key]) for g in lst], axis=0)

    grads["ev_norm"] = stack(ev_g, "norm", lambda a: a[0])
    grads["ev_w_in"] = stack(ev_g, "w_in")
    grads["ev_conv_a"] = stack(ev_g, "conv_a")
    grads["ev_conv_b"] = stack(ev_g, "conv_b")
    grads["ev_conv_b_bias"] = stack(ev_g, "conv_b_bias", lambda a: a[0])
    grads["ev_gate_r_w"] = stack(ev_g, "gate_r", lambda a: _block_diag_t(a, 8))
    grads["ev_gate_r_b"] = stack(ev_g, "gate_r_b", lambda a: a[0])
    grads["ev_gate_i_w"] = stack(ev_g, "gate_i", lambda a: _block_diag_t(a, 8))
    grads["ev_gate_i_b"] = stack(ev_g, "gate_i_b", lambda a: a[0])
    grads["ev_lru_lambda"] = stack(ev_g, "lam", lambda a: a[0])
    grads["ev_w_out"] = stack(ev_g, "w_out")
    grads["od_norm"] = stack(od_g, "norm", lambda a: a[0])
    grads["od_w_in"] = stack(od_g, "w_in", lambda a: a[:, :ODD_IN])
    grads["od_q_norm"] = stack(od_g, "q_norm", lambda a: a[0])
    grads["od_kv_norm"] = stack(od_g, "kv_norm", lambda a: a[0])
    grads["od_w_uq"] = stack(od_g, "w_uq", lambda a: _unpad_heads(a, QK_HEAD))

    def ukv(g):
        gk = g["w_uk"].reshape(KV_LORA, MLA_HEADS, HEAD_PAD)[:, :, :QK_NOPE]
        gv = g["w_uv"].reshape(KV_LORA, MLA_HEADS, HEAD_PAD)[:, :, :V_HEAD]
        return jnp.concatenate([gk, gv], axis=2).reshape(KV_LORA, MLA_HEADS * (QK_NOPE + V_HEAD))

    grads["od_w_ukv"] = jnp.stack([ukv(g) for g in od_g], axis=0)
    grads["od_w_out"] = stack(od_g, "w_out", lambda a: a.reshape(MLA_HEADS, HEAD_PAD, d)[:, :V_HEAD].reshape(-1, d))
    grads["ffn_norm"] = stack(ffn_g, "norm", lambda a: a[0])
    grads["ffn_w_up"] = stack(ffn_g, "w_up")
    grads["ffn_conv_w"] = stack(ffn_g, "conv_w")
    grads["ffn_conv_b"] = stack(ffn_g, "conv_b", lambda a: a[0])
    grads["ffn_w_down"] = stack(ffn_g, "w_down")

    order = small + replicated
    slot_parts = [_to_slots(grads[n], axis_of[n]).reshape(N_DEV, -1) for n in small]
    slot_parts += [jnp.broadcast_to(grads[n].reshape(1, -1), (N_DEV, grads[n].size)) for n in replicated]
    g_flat = jnp.concatenate(slot_parts, axis=1)
    n_flat = g_flat.shape[1]
    rows = -(-n_flat // (1024 * 128)) * 128
    g_small = jnp.pad(g_flat, ((0, 0), (0, rows * 1024 - n_flat))).reshape(N_DEV, rows, 1024)

    def rows_of(n):
        shard = w_loc[n].shape
        return (math.prod(shard[:-1]), shard[-1])

    g_big = [_to_slots(grads[n], axis_of[n]).astype(BF16).reshape((N_DEV,) + rows_of(n)) for n in BIG]
    parts = _exchange("grad_exchange", g_big + [g_small], scatter=True)

    g_out, d_out, m_out, v_out = {}, {}, {}, {}
    for n, part in zip(BIG, parts[:-1]):
        res = _reduce_adamw(part, *[t[n].reshape(rows_of(n)) for t in (w_loc, m_loc, v_loc)])
        for out, r in zip((g_out, d_out, m_out, v_out), res):
            out[n] = r.reshape(w_loc[n].shape)

    def flat_local(tree):
        flat = jnp.concatenate([tree[n].reshape(-1) for n in order])
        return jnp.pad(flat, (0, rows * 1024 - n_flat)).reshape(rows, 1024)

    res = _reduce_adamw(parts[-1], flat_local(w_loc), flat_local(m_loc), flat_local(v_loc))
    for out, r in zip((g_out, d_out, m_out, v_out), res):
        flat = r.reshape(-1)
        off = 0
        for n in order:
            size = w_loc[n].size
            out[n] = flat[off:off + size].reshape(w_loc[n].shape)
            off += size
    return (loss, grad_x, *[g_out[n] for n in names], *[d_out[n] for n in names], *[m_out[n] for n in names],
            *[v_out[n] for n in names])
```

```python
import functools
import math

import jax
import jax.numpy as jnp
from jax import lax
from jax.experimental import pallas as pl
from jax.experimental.pallas import tpu as pltpu

F32 = jnp.float32
BF16 = jnp.bfloat16

N_DEV = 8
N_META = 16
EPS = 1e-6
LRU_C = 8.0
MLA_HEADS = 16
QK_NOPE = 64
QK_ROPE = 32
QK_HEAD = QK_NOPE + QK_ROPE
V_HEAD = 64
HEAD_PAD = 128
Q_LORA = 384
KV_LORA = 256
ODD_IN = Q_LORA + KV_LORA + QK_ROPE
ODD_IN_PAD = 768
ROPE_BASE = 10000.0
LRU_WIDTH = 512
D_FF = 2816

ADAM_LR = 0.001
ADAM_B1 = 0.9
ADAM_B2 = 0.999
ADAM_EPS = 1e-08
ADAM_WD = 0.01
ADAM_STEP = 10

ROW_TILE = 384
SUBLANES = 8
LANES = 128
VMEM_LIMIT = 48 * 1024 * 1024
NEG = -1e30

PARAMS = (
    ("meta_tokens", 1), ("ev_norm", None), ("ev_w_in", 2), ("ev_conv_a", 2), ("ev_conv_b", 2),
    ("ev_conv_b_bias", None), ("ev_gate_r_w", None), ("ev_gate_r_b", None), ("ev_gate_i_w", None),
    ("ev_gate_i_b", None), ("ev_lru_lambda", None), ("ev_w_out", 1), ("od_norm", 1), ("od_w_in", 1),
    ("od_q_norm", 1), ("od_kv_norm", 1), ("od_w_uq", 2), ("od_w_ukv", 2), ("od_w_out", 1),
    ("ffn_norm", None), ("ffn_w_up", 2), ("ffn_conv_w", 2), ("ffn_conv_b", None), ("ffn_w_down", 1),
    ("final_norm", None),
)
BIG = ("ev_w_in", "ev_w_out", "od_w_in", "od_w_uq", "od_w_ukv", "od_w_out", "ffn_w_up", "ffn_w_down")


def _cparams(n_grid):
    return pltpu.CompilerParams(dimension_semantics=("arbitrary",) * n_grid, vmem_limit_bytes=VMEM_LIMIT)


def _pick(dim, target):
    if dim <= target:
        return dim
    best = None
    for t in range(LANES, target + 1, LANES):
        if dim % t == 0:
            best = t
    assert best is not None, (dim, target)
    return best


MATMUL_VMEM_BUDGET = 30 * 1024 * 1024
HBM_BYTES_PER_US = 3.0e6
GRID_STEP_US = 0.35


def _tile_candidates(dim):
    return [t for t in range(LANES, dim + 1, LANES) if dim % t == 0] or [dim]


def _matmul_tiles(m, n, k, sa, sb, so, sr):
    best, best_cost = None, None
    for tm in _tile_candidates(m):
        for tn in _tile_candidates(n):
            for tk in _tile_candidates(k):
                vmem = 2 * (tm * tk * sa + tk * tn * sb) + tm * tn * (4 + 2 * so + 2 * sr)
                vmem += (tm * tk * 2 if sa > 2 else 0) + (tk * tn * 2 if sb > 2 else 0)
                if vmem > MATMUL_VMEM_BUDGET:
                    continue
                traffic = m * k * sa * (n // tn) + k * n * sb * (m // tm) + m * n * (so + sr)
                cost = traffic / HBM_BYTES_PER_US + (m // tm) * (n // tn) * (k // tk) * GRID_STEP_US
                if best_cost is None or cost < best_cost:
                    best, best_cost = (tm, tn, tk), cost
    assert best is not None, (m, n, k)
    return best


def _matmul(a, b, mode, out_dtype=F32, residual=None, name="mm"):
    if mode == "nn":
        (m, k), (k2, n) = a.shape, b.shape
    elif mode == "nt":
        (m, k), (n, k2) = a.shape, b.shape
    else:
        (k, m), (k2, n) = a.shape, b.shape
    assert k == k2, (a.shape, b.shape, mode)
    tm, tn, tk = _matmul_tiles(m, n, k, a.dtype.itemsize, b.dtype.itemsize, jnp.dtype(out_dtype).itemsize,
                               0 if residual is None else residual.dtype.itemsize)
    nk = k // tk
    if mode == "tn":
        a_spec = pl.BlockSpec((tk, tm), lambda i, j, kk: (kk, i))
        dims = (((0,), (0,)), ((), ()))
    else:
        a_spec = pl.BlockSpec((tm, tk), lambda i, j, kk: (i, kk))
        dims = (((1,), (1 if mode == "nt" else 0,)), ((), ()))
    if mode == "nt":
        b_spec = pl.BlockSpec((tn, tk), lambda i, j, kk: (j, kk))
    else:
        b_spec = pl.BlockSpec((tk, tn), lambda i, j, kk: (kk, j))
    o_spec = pl.BlockSpec((tm, tn), lambda i, j, kk: (i, j))
    has_res = residual is not None

    def body(*refs):
        if has_res:
            a_ref, b_ref, r_ref, o_ref, acc_ref = refs
        else:
            a_ref, b_ref, o_ref, acc_ref = refs
        kk = pl.program_id(2)

        @pl.when(kk == 0)
        def _():
            acc_ref[...] = jnp.zeros_like(acc_ref)

        acc_ref[...] += lax.dot_general(a_ref[...].astype(BF16), b_ref[...].astype(BF16), dims,
                                        preferred_element_type=F32)

        @pl.when(kk == nk - 1)
        def _():
            out = acc_ref[...]
            if has_res:
                out = out + r_ref[...].astype(F32)
            o_ref[...] = out.astype(o_ref.dtype)

    in_specs = [a_spec, b_spec] + ([o_spec] if has_res else [])
    args = (a, b) + ((residual,) if has_res else ())
    return pl.pallas_call(
        body, out_shape=jax.ShapeDtypeStruct((m, n), out_dtype), grid=(m // tm, n // tn, nk),
        in_specs=in_specs, out_specs=o_spec, scratch_shapes=[pltpu.VMEM((tm, tn), F32)],
        compiler_params=_cparams(3), name=name)(*args)


def _rms_fwd(x, g, name):
    m, c = x.shape
    tm = _pick(m, ROW_TILE)

    def body(x_ref, g_ref, o_ref):
        xf = x_ref[...].astype(F32)
        r = lax.rsqrt(jnp.mean(xf * xf, axis=-1, keepdims=True) + EPS)
        o_ref[...] = (xf * r * g_ref[...]).astype(o_ref.dtype)

    return pl.pallas_call(
        body, out_shape=jax.ShapeDtypeStruct((m, c), BF16), grid=(m // tm,),
        in_specs=[pl.BlockSpec((tm, c), lambda i: (i, 0)), pl.BlockSpec((1, c), lambda i: (0, 0))],
        out_specs=pl.BlockSpec((tm, c), lambda i: (i, 0)), compiler_params=_cparams(1), name=name)(x, g)


def _rms_bwd(x, g, dy, residual, name):
    m, c = x.shape
    tm = _pick(m, ROW_TILE)
    has_res = residual is not None

    def body(*refs):
        if has_res:
            x_ref, g_ref, dy_ref, r_ref, dx_ref, dg_ref = refs
        else:
            x_ref, g_ref, dy_ref, dx_ref, dg_ref = refs
        xf = x_ref[...].astype(F32)
        dyf = dy_ref[...].astype(F32)
        r = lax.rsqrt(jnp.mean(xf * xf, axis=-1, keepdims=True) + EPS)
        xn = xf * r
        dyg = dyf * g_ref[...]
        dx = r * (dyg - xn * jnp.mean(dyg * xn, axis=-1, keepdims=True))
        if has_res:
            dx = dx + r_ref[...]
        dx_ref[...] = dx

        @pl.when(pl.program_id(0) == 0)
        def _():
            dg_ref[...] = jnp.zeros_like(dg_ref)

        dg_ref[...] += jnp.sum(dyf * xn, axis=0, keepdims=True)

    row = pl.BlockSpec((tm, c), lambda i: (i, 0))
    vec = pl.BlockSpec((1, c), lambda i: (0, 0))
    in_specs = [row, vec, row] + ([row] if has_res else [])
    args = (x, g, dy) + ((residual,) if has_res else ())
    return pl.pallas_call(
        body, out_shape=(jax.ShapeDtypeStruct((m, c), F32), jax.ShapeDtypeStruct((1, c), F32)), grid=(m // tm,),
        in_specs=in_specs, out_specs=(row, vec), compiler_params=_cparams(1), name=name)(*args)


def _chan_call(name, fn, m, tp, tc, ncol, row_ins=(), prev_ins=(), next_ins=(), chan_ins=(), row_outs=(),
               red_outs=(), row_split=1):
    tm = _pick(tp, ROW_TILE) // row_split
    tps = tp // tm
    nrow = m // tm
    h8 = tm // SUBLANES
    last8 = m // SUBLANES - 1
    n_in = len(row_ins) + len(prev_ins) + len(next_ins) + len(chan_ins)
    n_r, n_p, n_n = len(row_ins), len(prev_ins), len(next_ins)

    def body(*refs):
        i = pl.program_id(1)
        pos = lax.rem(i, tps)
        at_start = pos == 0
        at_end = pos == tps - 1
        rows = [r[...].astype(F32) for r in refs[:n_r]]
        prevs = [jnp.where(at_start, 0.0, r[...].astype(F32)) for r in refs[n_r:n_r + n_p]]
        nexts = [jnp.where(at_end, 0.0, r[...].astype(F32)) for r in refs[n_r + n_p:n_r + n_p + n_n]]
        chans = [r[...] for r in refs[n_r + n_p + n_n:n_in]]
        out_refs = refs[n_in:n_in + len(row_outs)]
        red_refs = refs[n_in + len(row_outs):]
        row_vals, red_vals = fn(rows, prevs, nexts, chans)
        for ref, val in zip(out_refs, row_vals):
            ref[...] = val.astype(ref.dtype)
        if red_refs:
            @pl.when(i == 0)
            def _():
                for ref in red_refs:
                    ref[...] = jnp.zeros_like(ref)

            for ref, val in zip(red_refs, red_vals):
                ref[...] += val

    in_specs, args = [], []
    for arr, off in row_ins:
        in_specs.append(pl.BlockSpec((tm, tc), lambda j, i, off=off: (i, j + off)))
        args.append(arr)
    for arr, off in prev_ins:
        in_specs.append(pl.BlockSpec((SUBLANES, tc), lambda j, i, off=off: (jnp.maximum(i * h8 - 1, 0), j + off)))
        args.append(arr)
    for arr, off in next_ins:
        in_specs.append(pl.BlockSpec((SUBLANES, tc), lambda j, i, off=off: (jnp.minimum((i + 1) * h8, last8), j + off)))
        args.append(arr)
    for arr, off in chan_ins:
        in_specs.append(pl.BlockSpec((arr.shape[0], tc), lambda j, i, off=off: (0, j + off)))
        args.append(arr)
    out_shape, out_specs = [], []
    for (dt,) in row_outs:
        out_shape.append(jax.ShapeDtypeStruct((m, ncol * tc), dt))
        out_specs.append(pl.BlockSpec((tm, tc), lambda j, i: (i, j)))
    for (k,) in red_outs:
        out_shape.append(jax.ShapeDtypeStruct((k, ncol * tc), F32))
        out_specs.append(pl.BlockSpec((k, tc), lambda j, i: (0, j)))
    return pl.pallas_call(
        body, out_shape=tuple(out_shape), grid=(ncol, nrow), in_specs=in_specs, out_specs=tuple(out_specs),
        compiler_params=_cparams(2), name=name)(*args)


def _shift_down(x, prev8, s):
    if s == 0:
        return x
    xs = pltpu.roll(x, s, 0)
    ps = pltpu.roll(prev8, s, 0)
    rid = lax.broadcasted_iota(jnp.int32, prev8.shape, 0)
    head = jnp.where(rid < s, ps, xs[:SUBLANES])
    return jnp.concatenate([head, xs[SUBLANES:]], axis=0)


def _shift_up(x, next8, s):
    if s == 0:
        return x
    tm = x.shape[0]
    xs = pltpu.roll(x, tm - s, 0)
    ns = pltpu.roll(next8, SUBLANES - s, 0)
    rid = lax.broadcasted_iota(jnp.int32, next8.shape, 0)
    tail = jnp.where(rid >= SUBLANES - s, ns, xs[tm - SUBLANES:])
    return jnp.concatenate([xs[:tm - SUBLANES], tail], axis=0)


def _conv_fwd(x, prev8, w):
    kw = w.shape[0]
    y = w[kw - 1:kw, :] * x
    for k in range(kw - 1):
        y = y + w[k:k + 1, :] * _shift_down(x, prev8, kw - 1 - k)
    return y


def _conv_dw(dy, x, prev8, kw):
    rid = lax.broadcasted_iota(jnp.int32, prev8.shape, 0)
    out = jnp.zeros(prev8.shape, F32)
    for k in range(kw):
        row = jnp.sum(dy * _shift_down(x, prev8, kw - 1 - k), axis=0, keepdims=True)
        out = out + jnp.where(rid == k, row, 0.0)
    return out


def _conv_dx(dy, next8, w):
    kw = w.shape[0]
    dx = w[kw - 1:kw, :] * dy
    for k in range(kw - 1):
        dx = dx + w[k:k + 1, :] * _shift_up(dy, next8, kw - 1 - k)
    return dx


def _sigmoid(x):
    return 1.0 / (1.0 + jnp.exp(-x))


def _expm1(x):
    series = x * (1.0 + x * 0.5 * (1.0 + x * (1.0 / 3.0) * (1.0 + x * 0.25 * (1.0 + x * 0.2))))
    return jnp.where(jnp.abs(x) < 0.3, series, jnp.exp(x) - 1.0)


def _softplus_neg(lam):
    e = jnp.exp(-jnp.abs(lam))
    log1p = jnp.where(e < 1e-2, e * (1.0 - e * (0.5 - e * (1.0 / 3.0))), jnp.log(1.0 + e))
    return jnp.maximum(-lam, 0.0) + log1p


GELU_C = math.sqrt(2.0 / math.pi)


def _gelu(x):
    return 0.5 * x * (1.0 + jnp.tanh(GELU_C * (x + 0.044715 * x * x * x)))


def _gelu_grad(x):
    t = jnp.tanh(GELU_C * (x + 0.044715 * x * x * x))
    return 0.5 * (1.0 + t) + 0.5 * x * (1.0 - t * t) * GELU_C * (1.0 + 3.0 * 0.044715 * x * x)


FFN_COL_TILE = 1408


def _ffn_fwd(x, p, m, tp):
    h = _rms_fwd(x, p["norm"], "ffn_norm")
    u = _matmul(h, p["w_up"], "nn", F32, name="ffn_up")
    tc = FFN_COL_TILE
    ncol = D_FF // tc

    def gate(rows, prevs, nexts, chans):
        ua, ug = rows
        wa, wg, ba, bg = chans
        a = _conv_fwd(ua, prevs[0], wa) + ba
        g = _conv_fwd(ug, prevs[1], wg) + bg
        return [a * _sigmoid(a) * g], []

    (z,) = _chan_call("ffn_gate", gate, m, tp, tc, ncol, row_ins=[(u, 0), (u, ncol)], prev_ins=[(u, 0), (u, ncol)],
                      chan_ins=[(p["conv_w"], 0), (p["conv_w"], ncol), (p["conv_b"], 0), (p["conv_b"], ncol)],
                      row_outs=[(BF16,)])
    out = _matmul(z, p["w_down"], "nn", F32, residual=x, name="ffn_down")
    return out, (x, h, u, z)


def _ffn_bwd(dout, p, saved, m, tp):
    x, h, u, z = saved
    tc = FFN_COL_TILE
    ncol = D_FF // tc
    dz = _matmul(dout, p["w_down"], "nt", F32, name="ffn_down_dx")
    d_w_down = _matmul(z, dout, "tn", F32, name="ffn_down_dw")

    def gate_bwd(rows, prevs, nexts, chans):
        ua, ug, dzv = rows
        wa, wg, ba, bg = chans
        a = _conv_fwd(ua, prevs[0], wa) + ba
        g = _conv_fwd(ug, prevs[1], wg) + bg
        sg = _sigmoid(a)
        da = dzv * g * (sg * (1.0 + a * (1.0 - sg)))
        dg = dzv * a * sg
        return ([da, dg],
                [_conv_dw(da, ua, prevs[0], 3), _conv_dw(dg, ug, prevs[1], 3),
                 jnp.sum(da, axis=0, keepdims=True), jnp.sum(dg, axis=0, keepdims=True)])

    da, dg, dcw_a, dcw_g, dcb_a, dcb_g = _chan_call(
        "ffn_gate_bwd", gate_bwd, m, tp, tc, ncol, row_ins=[(u, 0), (u, ncol), (dz, 0)], prev_ins=[(u, 0), (u, ncol)],
        chan_ins=[(p["conv_w"], 0), (p["conv_w"], ncol), (p["conv_b"], 0), (p["conv_b"], ncol)],
        row_outs=[(F32,), (F32,)], red_outs=[(SUBLANES,), (SUBLANES,), (1,), (1,)], row_split=2)

    def conv_dx(rows, prevs, nexts, chans):
        return [_conv_dx(rows[0], nexts[0], chans[0]), _conv_dx(rows[1], nexts[1], chans[1])], []

    dua, dug = _chan_call("ffn_conv_dx", conv_dx, m, tp, tc, ncol, row_ins=[(da, 0), (dg, 0)],
                          next_ins=[(da, 0), (dg, 0)], chan_ins=[(p["conv_w"], 0), (p["conv_w"], ncol)],
                          row_outs=[(BF16,), (BF16,)])
    d_w_up = jnp.concatenate([_matmul(h, dua, "tn", F32, name="ffn_up_dw_a"),
                              _matmul(h, dug, "tn", F32, name="ffn_up_dw_g")], axis=1)
    dh = _matmul(dua, p["w_up_a"], "nt", F32, name="ffn_up_dx_a")
    dh = _matmul(dug, p["w_up_g"], "nt", F32, residual=dh, name="ffn_up_dx_g")
    dx, d_norm = _rms_bwd(x, p["norm"], dh, dout, "ffn_norm_bwd")
    d_conv_w = jnp.concatenate([dcw_a[:3], dcw_g[:3]], axis=1)
    d_conv_b = jnp.concatenate([dcb_a, dcb_g], axis=1)
    return dx, dict(norm=d_norm, w_up=d_w_up, conv_w=d_conv_w, conv_b=d_conv_b, w_down=d_w_down)


def _to_scan(x, nb, tp):
    return x.reshape(nb, tp, LRU_WIDTH // LANES, LANES).transpose(1, 0, 2, 3).reshape(tp, -1, LANES)


def _from_scan(x, nb, tp):
    return x.reshape(tp, nb, LRU_WIDTH // LANES, LANES).transpose(1, 0, 2, 3).reshape(nb * tp, LRU_WIDTH)


def _scan_fwd(a, u):
    t_len, s, _ = a.shape
    tc = _pick(t_len, 640)
    blk = pl.BlockSpec((tc, s, LANES), lambda i: (i, 0, 0))

    def body(a_ref, u_ref, h_ref, carry):
        @pl.when(pl.program_id(0) == 0)
        def _():
            carry[...] = jnp.zeros_like(carry)

        def step(t, h):
            h = a_ref[t] * h + u_ref[t]
            h_ref[t] = h
            return h

        carry[...] = lax.fori_loop(0, tc, step, carry[...], unroll=8)

    return pl.pallas_call(
        body, out_shape=jax.ShapeDtypeStruct(a.shape, F32), grid=(t_len // tc,), in_specs=[blk, blk], out_specs=blk,
        scratch_shapes=[pltpu.VMEM((s, LANES), F32)], compiler_params=_cparams(1), name="lru_scan")(a, u)


def _scan_bwd(dh, a, h_prev):
    t_len, s, _ = a.shape
    tc = _pick(t_len, 640)
    nb = t_len // tc
    blk = pl.BlockSpec((tc, s, LANES), lambda i: (nb - 1 - i, 0, 0))

    def body(dh_ref, a_ref, hp_ref, du_ref, da_ref, carry):
        @pl.when(pl.program_id(0) == 0)
        def _():
            carry[...] = jnp.zeros_like(carry)

        def step(k, c):
            t = tc - 1 - k
            d = dh_ref[t] + c
            du_ref[t] = d
            da_ref[t] = d * hp_ref[t]
            return a_ref[t] * d

        carry[...] = lax.fori_loop(0, tc, step, carry[...], unroll=8)

    shp = jax.ShapeDtypeStruct(a.shape, F32)
    return pl.pallas_call(
        body, out_shape=(shp, shp), grid=(nb,), in_specs=[blk, blk, blk], out_specs=(blk, blk),
        scratch_shapes=[pltpu.VMEM((s, LANES), F32)], compiler_params=_cparams(1), name="lru_scan_bwd")(dh, a, h_prev)


def _lru_gates(xc, zr, zi, r_b, i_b, lam):
    r = _sigmoid(zr + r_b)
    ig = _sigmoid(zi + i_b)
    sp = _softplus_neg(lam)
    log_a = -LRU_C * r * sp
    a = jnp.exp(log_a)
    mult = jnp.sqrt(-_expm1(2.0 * log_a))
    return r, ig, sp, a, mult


def _even_fwd(x, p, m, tp, nb):
    c = LRU_WIDTH
    h = _rms_fwd(x, p["norm"], "ev_norm")
    u = _matmul(h, p["w_in"], "nn", F32, name="ev_in")

    def pre(rows, prevs, nexts, chans):
        gb, gc, xa, xb = rows
        wa, wb, bias = chans
        pa = gc * xa
        ya = gb * _conv_fwd(pa, prevs[0] * prevs[1], wa)
        xc = _conv_fwd(xb, prevs[2], wb) + bias
        return [ya, xc], []

    ya, xc = _chan_call("ev_pre", pre, m, tp, c, 1, row_ins=[(u, 0), (u, 1), (u, 2), (u, 3)],
                        prev_ins=[(u, 1), (u, 2), (u, 3)],
                        chan_ins=[(p["conv_a"], 0), (p["conv_b"], 0), (p["conv_b_bias"], 0)],
                        row_outs=[(BF16,), (F32,)])
    zr = _matmul(xc, p["gate_r"], "nn", F32, name="ev_gate_r")
    zi = _matmul(xc, p["gate_i"], "nn", F32, name="ev_gate_i")

    def lru_in(rows, prevs, nexts, chans):
        xcv, zrv, ziv = rows
        r, ig, sp, a, mult = _lru_gates(xcv, zrv, ziv, *chans)
        return [a, mult * (ig * xcv)], []

    a, uu = _chan_call("ev_lru_in", lru_in, m, tp, c, 1, row_ins=[(xc, 0), (zr, 0), (zi, 0)],
                       chan_ins=[(p["gate_r_b"], 0), (p["gate_i_b"], 0), (p["lam"], 0)],
                       row_outs=[(F32,), (F32,)])
    a_s = _to_scan(a, nb, tp)
    hs_s = _scan_fwd(a_s, _to_scan(uu, nb, tp))
    hs = _from_scan(hs_s, nb, tp)

    def post(rows, prevs, nexts, chans):
        gate, hv = rows
        return [_gelu(gate) * hv], []

    (yb,) = _chan_call("ev_post", post, m, tp, c, 1, row_ins=[(u, 4), (hs, 0)], row_outs=[(BF16,)])
    out = _matmul(ya, p["w_out_a"], "nn", F32, residual=x, name="ev_out_a")
    out = _matmul(yb, p["w_out_b"], "nn", F32, residual=out, name="ev_out_b")
    return out, (x, h, u, ya, xc, zr, zi, a_s, hs_s, hs, yb)


def _even_bwd(dout, p, saved, m, tp, nb):
    c = LRU_WIDTH
    x, h, u, ya, xc, zr, zi, a_s, hs_s, hs, yb = saved
    dy = _matmul(dout, p["w_out"], "nt", F32, name="ev_out_dx")
    d_w_out = jnp.concatenate([_matmul(ya, dout, "tn", F32, name="ev_out_dw_a"),
                               _matmul(yb, dout, "tn", F32, name="ev_out_dw_b")], axis=0)

    def post_bwd(rows, prevs, nexts, chans):
        dyb, gate, hv = rows
        return [dyb * hv * _gelu_grad(gate), dyb * _gelu(gate)], []

    dgate, dhs = _chan_call("ev_post_bwd", post_bwd, m, tp, c, 1, row_ins=[(dy, 1), (u, 4), (hs, 0)],
                            row_outs=[(F32,), (F32,)])
    h_prev = jnp.concatenate([jnp.zeros_like(hs_s[:1]), hs_s[:-1]], axis=0)
    du_s, da_s = _scan_bwd(_to_scan(dhs, nb, tp), a_s, h_prev)
    du = _from_scan(du_s, nb, tp)
    da = _from_scan(da_s, nb, tp)

    def lru_in_bwd(rows, prevs, nexts, chans):
        duv, dav, xcv, zrv, ziv = rows
        r, ig, sp, a, mult = _lru_gates(xcv, zrv, ziv, *chans)
        dxc = duv * mult * ig
        dig = duv * mult * xcv
        dmult = duv * ig * xcv
        dlog_a = dav * a - dmult * (a * a) / jnp.maximum(mult, 1e-30)
        dr = dlog_a * (-LRU_C * sp)
        dzr = dr * r * (1.0 - r)
        dzi = dig * ig * (1.0 - ig)
        dsp = jnp.sum(dlog_a * (-LRU_C * r), axis=0, keepdims=True)
        dlam = -dsp * _sigmoid(-chans[2])
        return ([dzr, dzi, dxc],
                [jnp.sum(dzr, axis=0, keepdims=True), jnp.sum(dzi, axis=0, keepdims=True), dlam])

    dzr, dzi, dxc, d_r_b, d_i_b, d_lam = _chan_call(
        "ev_lru_in_bwd", lru_in_bwd, m, tp, c, 1, row_ins=[(du, 0), (da, 0), (xc, 0), (zr, 0), (zi, 0)],
        chan_ins=[(p["gate_r_b"], 0), (p["gate_i_b"], 0), (p["lam"], 0)],
        row_outs=[(F32,), (F32,), (F32,)], red_outs=[(1,), (1,), (1,)])
    d_gate_r = _matmul(xc, dzr, "tn", F32, name="ev_gate_r_dw")
    d_gate_i = _matmul(xc, dzi, "tn", F32, name="ev_gate_i_dw")
    dxc = _matmul(dzr, p["gate_r"], "nt", F32, residual=dxc, name="ev_gate_r_dx")
    dxc = _matmul(dzi, p["gate_i"], "nt", F32, residual=dxc, name="ev_gate_i_dx")

    def conv_b_bwd(rows, prevs, nexts, chans):
        dxcv, xb = rows
        return ([_conv_dx(dxcv, nexts[0], chans[0])],
                [_conv_dw(dxcv, xb, prevs[0], 4), jnp.sum(dxcv, axis=0, keepdims=True)])

    dxb, d_conv_b, d_bias = _chan_call(
        "ev_conv_b_bwd", conv_b_bwd, m, tp, c, 1, row_ins=[(dxc, 0), (u, 3)], prev_ins=[(u, 3)], next_ins=[(dxc, 0)],
        chan_ins=[(p["conv_b"], 0)], row_outs=[(F32,)], red_outs=[(SUBLANES,), (1,)])

    def mix_a_bwd(rows, prevs, nexts, chans):
        dya, gb, gc, xa = rows
        (wa,) = chans
        pa = gc * xa
        pa_prev = prevs[0] * prevs[1]
        ca = _conv_fwd(pa, pa_prev, wa)
        dca = dya * gb
        dpa = _conv_dx(dca, nexts[0] * nexts[1], wa)
        return [dya * ca, dpa * xa, dpa * gc], [_conv_dw(dca, pa, pa_prev, 3)]

    dgb, dgc, dxa, d_conv_a = _chan_call(
        "ev_mix_a_bwd", mix_a_bwd, m, tp, c, 1, row_ins=[(dy, 0), (u, 0), (u, 1), (u, 2)],
        prev_ins=[(u, 1), (u, 2)], next_ins=[(dy, 0), (u, 0)], chan_ins=[(p["conv_a"], 0)],
        row_outs=[(F32,), (F32,), (F32,)], red_outs=[(SUBLANES,)])
    du_all = jnp.concatenate([dgb, dgc, dxa, dxb, dgate], axis=1)
    d_w_in = _matmul(h, du_all, "tn", F32, name="ev_in_dw")
    dh = _matmul(du_all, p["w_in"], "nt", F32, name="ev_in_dx")
    dx, d_norm = _rms_bwd(x, p["norm"], dh, dout, "ev_norm_bwd")
    return dx, dict(norm=d_norm, w_in=d_w_in, conv_a=d_conv_a[:3], conv_b=d_conv_b[:4], conv_b_bias=d_bias,
                    gate_r=d_gate_r, gate_r_b=d_r_b, gate_i=d_gate_i, gate_i_b=d_i_b, lam=d_lam, w_out=d_w_out)


def _rope_tables(tp):
    pos = jnp.arange(tp, dtype=F32)
    inv_freq = ROPE_BASE ** (-jnp.arange(0, QK_ROPE, 2, dtype=F32) / QK_ROPE)
    ang = pos[:, None] * inv_freq[None, :]
    cos, sin = jnp.cos(ang), jnp.sin(ang)
    half = QK_ROPE // 2
    one = jnp.ones((tp, QK_NOPE), F32)
    z64 = jnp.zeros((tp, QK_NOPE), F32)
    zh = jnp.zeros((tp, half), F32)
    zt = jnp.zeros((tp, HEAD_PAD - QK_HEAD), F32)
    c_tab = jnp.concatenate([one, cos, cos, zt], axis=1)
    s_lo = jnp.concatenate([z64, -sin, zh, zt], axis=1)
    s_hi = jnp.concatenate([z64, zh, sin, zt], axis=1)
    return c_tab, s_lo, s_hi


def _rope(v, c_tab, s_lo, s_hi):
    half = QK_ROPE // 2
    return v * c_tab + pltpu.roll(v, HEAD_PAD - half, 1) * s_lo + pltpu.roll(v, half, 1) * s_hi


def _rope_t(dv, c_tab, s_lo, s_hi):
    half = QK_ROPE // 2
    return dv * c_tab + pltpu.roll(dv * s_lo, half, 1) + pltpu.roll(dv * s_hi, HEAD_PAD - half, 1)


def _rope_call(name, fn, m, tp, ins, tables, out_dtype, shared_pre=None):
    tm = _pick(tp, ROW_TILE)
    tps = tp // tm
    n = len(ins)
    width = MLA_HEADS * HEAD_PAD

    def body(*refs):
        tabs = [r[...] for r in refs[n:n + 3]]
        shared = [None if fc is None else shared_pre(refs[a][...].astype(F32), *tabs) for a, (_, fc) in enumerate(ins)]
        for hh in range(MLA_HEADS):
            lanes = slice(hh * HEAD_PAD, (hh + 1) * HEAD_PAD)
            vals = [refs[a][:, lanes].astype(F32) if shared[a] is None else shared[a] for a in range(n)]
            refs[n + 3][:, lanes] = fn(*vals, *tabs).astype(out_dtype)

    in_specs, args = [], []
    for arr, fixed_col in ins:
        if fixed_col is None:
            in_specs.append(pl.BlockSpec((tm, width), lambda i: (i, 0)))
        else:
            in_specs.append(pl.BlockSpec((tm, HEAD_PAD), lambda i, fc=fixed_col: (i, fc)))
        args.append(arr)
    for tab in tables:
        in_specs.append(pl.BlockSpec((tm, HEAD_PAD), lambda i: (lax.rem(i, tps), 0)))
        args.append(tab)
    return pl.pallas_call(
        body, out_shape=jax.ShapeDtypeStruct((m, width), out_dtype), grid=(m // tm,),
        in_specs=in_specs, out_specs=pl.BlockSpec((tm, width), lambda i: (i, 0)),
        compiler_params=_cparams(1), name=name)(*args)


def _rope_k_bwd(dk, tables, m, tp):
    tm = _pick(tp, ROW_TILE)
    tps = tp // tm

    def body(dk_ref, c_ref, lo_ref, hi_ref, o_ref):
        acc = dk_ref[:, 0:HEAD_PAD].astype(F32)
        for hh in range(1, MLA_HEADS):
            acc = acc + dk_ref[:, hh * HEAD_PAD:(hh + 1) * HEAD_PAD].astype(F32)
        d = pltpu.roll(_rope_t(acc, c_ref[...], lo_ref[...], hi_ref[...]), QK_NOPE, 1)
        lane = lax.broadcasted_iota(jnp.int32, d.shape, 1)
        o_ref[...] = jnp.where(lane < QK_ROPE, d, 0.0)

    tab = pl.BlockSpec((tm, HEAD_PAD), lambda i: (lax.rem(i, tps), 0))
    return pl.pallas_call(
        body, out_shape=jax.ShapeDtypeStruct((m, HEAD_PAD), F32), grid=(m // tm,),
        in_specs=[pl.BlockSpec((tm, MLA_HEADS * HEAD_PAD), lambda i: (i, 0)), tab, tab, tab],
        out_specs=pl.BlockSpec((tm, HEAD_PAD), lambda i: (i, 0)), compiler_params=_cparams(1),
        name="od_rope_k_bwd")(dk, *tables)


def _causal_mask(row0, col0, shape):
    rows = row0 + lax.broadcasted_iota(jnp.int32, shape, 0)
    cols = col0 + lax.broadcasted_iota(jnp.int32, shape, 1)
    return cols <= rows


NT = (((1,), (1,)), ((), ()))
TN = (((0,), (0,)), ((), ()))
HEADS_PER_STEP = 2
HEAD_STEPS = MLA_HEADS // HEADS_PER_STEP
STEP_LANES = HEADS_PER_STEP * HEAD_PAD


def _flash_fwd(q, k, v, nb, tp):
    tq = _pick(tp, ROW_TILE)
    nq = tp // tq

    def body(q_ref, k_ref, v_ref, o_ref, lse_ref):
        i = pl.program_id(2)
        qbs = [q_ref[:, hd * HEAD_PAD:(hd + 1) * HEAD_PAD] for hd in range(HEADS_PER_STEP)]

        def chunk(j, carry, masked):
            off = pl.multiple_of(j * tq, tq)
            out = []
            for hd in range(HEADS_PER_STEP):
                mx, l, acc = carry[hd]
                lanes = slice(hd * HEAD_PAD, (hd + 1) * HEAD_PAD)
                kb = k_ref[pl.ds(off, tq), lanes]
                vb = v_ref[pl.ds(off, tq), lanes]
                s = lax.dot_general(qbs[hd], kb, NT, preferred_element_type=F32)
                if masked:
                    s = jnp.where(_causal_mask(0, 0, s.shape), s, NEG)
                m_new = jnp.maximum(mx, jnp.max(s, axis=1, keepdims=True))
                alpha = jnp.exp(mx - m_new)
                pr = jnp.exp(s - m_new)
                l = alpha * l + jnp.sum(pr, axis=1, keepdims=True)
                acc = alpha * acc + jnp.dot(pr.astype(BF16), vb, preferred_element_type=F32)
                out.append((m_new, l, acc))
            return tuple(out)

        one = (jnp.full((tq, 1), NEG, F32), jnp.zeros((tq, 1), F32), jnp.zeros((tq, HEAD_PAD), F32))
        carry = lax.fori_loop(0, i, lambda j, c: chunk(j, c, False), (one,) * HEADS_PER_STEP)
        carry = chunk(i, carry, True)
        for hd in range(HEADS_PER_STEP):
            mx, l, acc = carry[hd]
            lanes = slice(hd * HEAD_PAD, (hd + 1) * HEAD_PAD)
            o_ref[:, lanes] = (acc / l).astype(o_ref.dtype)
            lse_ref[:, lanes] = jnp.broadcast_to(mx + jnp.log(l), (tq, HEAD_PAD))

    qspec = pl.BlockSpec((tq, STEP_LANES), lambda b, hh, i: (b * nq + i, hh))
    kvspec = pl.BlockSpec((tp, STEP_LANES), lambda b, hh, i: (b, hh))
    shp = (nb * tp, MLA_HEADS * HEAD_PAD)
    return pl.pallas_call(
        body, out_shape=(jax.ShapeDtypeStruct(shp, BF16), jax.ShapeDtypeStruct(shp, F32)),
        grid=(nb, HEAD_STEPS, nq), in_specs=[qspec, kvspec, kvspec], out_specs=(qspec, qspec),
        compiler_params=_cparams(3), name="od_flash_fwd")(q, k, v)


def _flash_prep(o, do, lse_c, nb, tp):
    tq = _pick(tp, ROW_TILE)
    nq = tp // tq

    def body(o_ref, do_ref, lse_ref, lr_ref, dr_ref):
        for hh in range(MLA_HEADS):
            lanes = slice(hh * HEAD_PAD, (hh + 1) * HEAD_PAD)
            delta = jnp.sum(o_ref[:, lanes].astype(F32) * do_ref[:, lanes].astype(F32), axis=1, keepdims=True)
            lr_ref[hh] = jnp.transpose(lse_ref[:, lanes])[0:SUBLANES, :]
            dr_ref[hh] = jnp.transpose(jnp.broadcast_to(delta, (tq, HEAD_PAD)))[0:SUBLANES, :]

    qspec = pl.BlockSpec((tq, MLA_HEADS * HEAD_PAD), lambda b, i: (b * nq + i, 0))
    rspec = pl.BlockSpec((MLA_HEADS, None, SUBLANES, tq), lambda b, i: (b, i, 0, 0))
    rshape = jax.ShapeDtypeStruct((nb * MLA_HEADS, nq, SUBLANES, tq), F32)
    return pl.pallas_call(
        body, out_shape=(rshape, rshape), grid=(nb, nq), in_specs=[qspec, qspec, qspec],
        out_specs=(rspec, rspec), compiler_params=_cparams(2), name="od_flash_prep")(o, do, lse_c)


def _flash_bwd(q, k, v, do, lse_r, delta_r, nb, tp):
    tq = _pick(tp, ROW_TILE)
    nq = tp // tq

    def body(q_ref, k_ref, v_ref, do_ref, lse_ref, dl_ref, dq_ref, dk_ref, dv_ref):
        j = pl.program_id(2)

        @pl.when(j == 0)
        def _():
            dq_ref[...] = jnp.zeros_like(dq_ref)

        kbs = [k_ref[:, hd * HEAD_PAD:(hd + 1) * HEAD_PAD] for hd in range(HEADS_PER_STEP)]
        vbs = [v_ref[:, hd * HEAD_PAD:(hd + 1) * HEAD_PAD] for hd in range(HEADS_PER_STEP)]

        def chunk(i, carry, masked):
            off = pl.multiple_of(i * tq, tq)
            out = []
            for hd in range(HEADS_PER_STEP):
                dk, dv = carry[hd]
                lanes = slice(hd * HEAD_PAD, (hd + 1) * HEAD_PAD)
                qb = q_ref[pl.ds(off, tq), lanes]
                dob = do_ref[pl.ds(off, tq), lanes]
                lse = lse_ref[hd, i][0:1, :]
                delta = dl_ref[hd, i][0:1, :]
                st = lax.dot_general(kbs[hd], qb, NT, preferred_element_type=F32)
                pt = jnp.exp(st - lse)
                if masked:
                    keys = lax.broadcasted_iota(jnp.int32, st.shape, 0)
                    queries = lax.broadcasted_iota(jnp.int32, st.shape, 1)
                    pt = jnp.where(keys <= queries, pt, 0.0)
                dv = dv + jnp.dot(pt.astype(BF16), dob, preferred_element_type=F32)
                dpt = lax.dot_general(vbs[hd], dob, NT, preferred_element_type=F32)
                dst = (pt * (dpt - delta)).astype(BF16)
                dk = dk + jnp.dot(dst, qb, preferred_element_type=F32)
                dq_ref[pl.ds(off, tq), lanes] += lax.dot_general(dst, kbs[hd], TN, preferred_element_type=F32)
                out.append((dk, dv))
            return tuple(out)

        zero = jnp.zeros((tq, HEAD_PAD), F32)
        carry = chunk(j, ((zero, zero),) * HEADS_PER_STEP, True)
        carry = lax.fori_loop(j + 1, nq, lambda i, c: chunk(i, c, False), carry)
        for hd in range(HEADS_PER_STEP):
            lanes = slice(hd * HEAD_PAD, (hd + 1) * HEAD_PAD)
            dk_ref[:, lanes] = carry[hd][0]
            dv_ref[:, lanes] = carry[hd][1].astype(dv_ref.dtype)

    tspec = pl.BlockSpec((tq, STEP_LANES), lambda b, hh, j: (b * nq + j, hh))
    fullspec = pl.BlockSpec((tp, STEP_LANES), lambda b, hh, j: (b, hh))
    rspec = pl.BlockSpec((HEADS_PER_STEP, nq, SUBLANES, tq), lambda b, hh, j: (b * HEAD_STEPS + hh, 0, 0, 0))
    shp = (nb * tp, MLA_HEADS * HEAD_PAD)
    return pl.pallas_call(
        body, out_shape=(jax.ShapeDtypeStruct(shp, F32), jax.ShapeDtypeStruct(shp, F32),
                         jax.ShapeDtypeStruct(shp, BF16)),
        grid=(nb, HEAD_STEPS, nq), in_specs=[fullspec, tspec, tspec, fullspec, rspec, rspec],
        out_specs=(fullspec, tspec, tspec), compiler_params=_cparams(3),
        name="od_flash_bwd")(q, k, v, do, lse_r, delta_r)


def _odd_fwd(x, p, tables, m, tp, nb):
    scale = QK_HEAD ** -0.5
    h = _rms_fwd(x, p["norm"], "od_norm")
    u = _matmul(h, p["w_in"], "nn", F32, name="od_in")
    cq = u[:, :Q_LORA]
    ckv = u[:, Q_LORA:Q_LORA + KV_LORA]
    cqn = _rms_fwd(cq, p["q_norm"], "od_q_norm")
    ckvn = _rms_fwd(ckv, p["kv_norm"], "od_kv_norm")
    q_raw = _matmul(cqn, p["w_uq"], "nn", F32, name="od_uq")
    k_raw = _matmul(ckvn, p["w_uk"], "nn", F32, name="od_uk")
    v = _matmul(ckvn, p["w_uv"], "nn", BF16, name="od_uv")
    q = _rope_call("od_rope_q", lambda qv, c, lo, hi: _rope(qv, c, lo, hi) * scale, m, tp, [(q_raw, None)], tables,
                   BF16)
    kr_col = (Q_LORA + KV_LORA) // HEAD_PAD
    k = _rope_call("od_rope_k", lambda kv, kr, c, lo, hi: kv + kr, m, tp, [(k_raw, None), (u, kr_col)], tables, BF16,
                   shared_pre=lambda uv, c, lo, hi: _rope(pltpu.roll(uv, QK_NOPE, 1), c, lo, hi))
    o, lse_c = _flash_fwd(q, k, v, nb, tp)
    out = _matmul(o, p["w_out"], "nn", F32, residual=x, name="od_out")
    return out, (x, h, cq, ckv, cqn, ckvn, q, k, v, o, lse_c)


def _odd_bwd(dout, p, tables, saved, m, tp, nb):
    scale = QK_HEAD ** -0.5
    x, h, cq, ckv, cqn, ckvn, q, k, v, o, lse_c = saved
    do = _matmul(dout, p["w_out"], "nt", BF16, name="od_out_dx")
    d_w_out = _matmul(o, dout, "tn", F32, name="od_out_dw")
    lse_r, delta_r = _flash_prep(o, do, lse_c, nb, tp)
    dq, dk, dv = _flash_bwd(q, k, v, do, lse_r, delta_r, nb, tp)
    dq_raw = _rope_call("od_rope_q_bwd", lambda d, c, lo, hi: _rope_t(d, c, lo, hi) * scale, m, tp, [(dq, None)],
                        tables, BF16)
    dkr = _rope_k_bwd(dk, tables, m, tp)
    d_w_uq = _matmul(cqn, dq_raw, "tn", F32, name="od_uq_dw")
    d_w_uk = _matmul(ckvn, dk, "tn", F32, name="od_uk_dw")
    d_w_uv = _matmul(ckvn, dv, "tn", F32, name="od_uv_dw")
    dcqn = _matmul(dq_raw, p["w_uq"], "nt", F32, name="od_uq_dx")
    dckvn = _matmul(dk, p["w_uk"], "nt", F32, name="od_uk_dx")
    dckvn = _matmul(dv, p["w_uv"], "nt", F32, residual=dckvn, name="od_uv_dx")
    dcq, d_q_norm = _rms_bwd(cq, p["q_norm"], dcqn, None, "od_q_norm_bwd")
    dckv, d_kv_norm = _rms_bwd(ckv, p["kv_norm"], dckvn, None, "od_kv_norm_bwd")
    du = jnp.concatenate([dcq, dckv, dkr], axis=1)
    d_w_in = _matmul(h, du, "tn", F32, name="od_in_dw")
    dh = _matmul(du, p["w_in"], "nt", F32, name="od_in_dx")
    dx, d_norm = _rms_bwd(x, p["norm"], dh, dout, "od_norm_bwd")
    return dx, dict(norm=d_norm, w_in=d_w_in, q_norm=d_q_norm, kv_norm=d_kv_norm, w_uq=d_w_uq, w_uk=d_w_uk,
                    w_uv=d_w_uv, w_out=d_w_out)


def _loss_head(hf, g, target, tp, t_real):
    m, c = hf.shape
    tm = _pick(tp, ROW_TILE)
    tps = tp // tm

    def body(x_ref, g_ref, t_ref, dx_ref, dg_ref, loss_ref):
        i = pl.program_id(0)
        xf = x_ref[...]
        r = lax.rsqrt(jnp.mean(xf * xf, axis=-1, keepdims=True) + EPS)
        xn = xf * r
        t_pos = lax.rem(i, tps) * tm + lax.broadcasted_iota(jnp.int32, (tm, 1), 0)
        valid = jnp.logical_and(t_pos >= N_META, t_pos < t_real)
        err = jnp.where(valid, xn * g_ref[...] - t_ref[...], 0.0)
        dyf = err * (1.0 / c)
        dyg = dyf * g_ref[...]
        dx_ref[...] = r * (dyg - xn * jnp.mean(dyg * xn, axis=-1, keepdims=True))

        @pl.when(i == 0)
        def _():
            dg_ref[...] = jnp.zeros_like(dg_ref)
            loss_ref[...] = jnp.zeros_like(loss_ref)

        dg_ref[...] += jnp.sum(dyf * xn, axis=0, keepdims=True)
        loss_ref[...] += (0.5 / c) * jnp.sum(jnp.sum(err * err, axis=1, keepdims=True), axis=0, keepdims=True)

    row = pl.BlockSpec((tm, c), lambda i: (i, 0))
    vec = pl.BlockSpec((1, c), lambda i: (0, 0))
    return pl.pallas_call(
        body, out_shape=(jax.ShapeDtypeStruct((m, c), F32), jax.ShapeDtypeStruct((1, c), F32),
                         jax.ShapeDtypeStruct((1, 1), F32)),
        grid=(m // tm,), in_specs=[row, vec, row], out_specs=(row, vec, pl.BlockSpec((1, 1), lambda i: (0, 0))),
        compiler_params=_cparams(1), name="loss_head")(hf, g, target)


def _meta_grad(dh0, nb, tp):
    d = dh0.shape[1]

    def body(x_ref, o_ref):
        @pl.when(pl.program_id(0) == 0)
        def _():
            o_ref[...] = jnp.zeros_like(o_ref)

        o_ref[...] += x_ref[...]

    return pl.pallas_call(
        body, out_shape=jax.ShapeDtypeStruct((N_META, d), F32), grid=(nb,),
        in_specs=[pl.BlockSpec((N_META, d), lambda b: (b * (tp // N_META), 0))],
        out_specs=pl.BlockSpec((N_META, d), lambda b: (0, 0)), compiler_params=_cparams(1), name="meta_grad")(dh0)


def _mesh_pos():
    x, y, c = lax.axis_index("x"), lax.axis_index("y"), lax.axis_index("c")
    return x, y, c


def _peer(x, y, c, k):
    px = 1 - x if k & 4 else x
    py = 1 - y if k & 2 else y
    pc = 1 - c if k & 1 else c
    return (px, py, pc), 4 * px + 2 * py + pc


def _exchange(name, arrays, scatter):
    n = len(arrays)

    def body(*refs):
        srcs, outs = refs[:n], refs[n:2 * n]
        send_sems, recv_sems, local_sems = refs[2 * n:]
        x, y, c = _mesh_pos()
        me = 4 * x + 2 * y + c

        def src(a, to):
            return srcs[a].at[to] if scatter else srcs[a]

        local = [pltpu.make_async_copy(src(a, me), outs[a].at[me], local_sems.at[a]) for a in range(n)]
        for cp in local:
            cp.start()
        sends = []
        for k in range(1, N_DEV):
            peer, peer_id = _peer(x, y, c, k)
            for a in range(n):
                sends.append(pltpu.make_async_remote_copy(
                    src_ref=src(a, peer_id), dst_ref=outs[a].at[me], send_sem=send_sems.at[a, k - 1],
                    recv_sem=recv_sems.at[a, k - 1], device_id=peer, device_id_type=pl.DeviceIdType.MESH))
        for cp in sends:
            cp.start()
        for k in range(1, N_DEV):
            peer, peer_id = _peer(x, y, c, k)
            for a in range(n):
                pltpu.make_async_remote_copy(
                    src_ref=src(a, me), dst_ref=outs[a].at[peer_id], send_sem=send_sems.at[a, k - 1],
                    recv_sem=recv_sems.at[a, k - 1], device_id=peer, device_id_type=pl.DeviceIdType.MESH).wait_recv()
        for cp in sends:
            cp.wait_send()
        for cp in local:
            cp.wait()

    any_spec = pl.BlockSpec(memory_space=pl.ANY)
    out_shape = tuple(jax.ShapeDtypeStruct(a.shape if scatter else (N_DEV,) + a.shape, a.dtype) for a in arrays)
    return pl.pallas_call(
        body, out_shape=out_shape, in_specs=[any_spec] * n, out_specs=(any_spec,) * n,
        scratch_shapes=[pltpu.SemaphoreType.DMA((n, N_DEV - 1)), pltpu.SemaphoreType.DMA((n, N_DEV - 1)),
                        pltpu.SemaphoreType.DMA((n,))],
        name=name)(*arrays)


REDUCE_BLOCK_BYTES = 512 * 1024


def _reduce_rows(r, c):
    best = None
    for t in range(16, r + 1, 16):
        if r % t == 0 and t * c * 4 <= REDUCE_BLOCK_BYTES:
            best = t
    assert best is not None, (r, c)
    return best


def _reduce_adamw(parts, w, mom, vel):
    _, r, c = parts.shape
    tr = _reduce_rows(r, c)
    c1 = 1.0 - ADAM_B1 ** ADAM_STEP
    c2 = 1.0 - ADAM_B2 ** ADAM_STEP

    def body(p_ref, w_ref, m_ref, v_ref, g_ref, d_ref, mo_ref, vo_ref):
        g = p_ref[0].astype(F32)
        for s in range(1, N_DEV):
            g = g + p_ref[s].astype(F32)
        mn = ADAM_B1 * m_ref[...] + (1.0 - ADAM_B1) * g
        vn = ADAM_B2 * v_ref[...] + (1.0 - ADAM_B2) * (g * g)
        m_hat = mn / c1
        v_hat = vn / c2
        g_ref[...] = g
        d_ref[...] = -ADAM_LR * (m_hat / (jnp.sqrt(v_hat) + ADAM_EPS) + ADAM_WD * w_ref[...])
        mo_ref[...] = mn
        vo_ref[...] = vn

    blk = pl.BlockSpec((tr, c), lambda i: (i, 0))
    shp = jax.ShapeDtypeStruct((r, c), F32)
    return pl.pallas_call(
        body, out_shape=(shp, shp, shp, shp), grid=(r // tr,),
        in_specs=[pl.BlockSpec((N_DEV, tr, c), lambda i: (0, i, 0)), blk, blk, blk], out_specs=(blk, blk, blk, blk),
        compiler_params=_cparams(1), name="reduce_adamw")(parts, w, mom, vel)


def _pack_rows(pieces, width, row_multiple, dtype):
    flat = jnp.concatenate([p.astype(dtype).reshape(-1) for p in pieces])
    rows = -(-flat.shape[0] // (width * row_multiple)) * row_multiple
    return jnp.pad(flat, (0, rows * width - flat.shape[0])).reshape(rows, width)


def _unshard(gathered, axis):
    moved = jnp.moveaxis(gathered, 0, axis)
    shape = list(moved.shape)
    shape[axis:axis + 2] = [shape[axis] * shape[axis + 1]]
    return moved.reshape(shape)


def _to_slots(full, axis):
    shape = list(full.shape)
    shape[axis:axis + 1] = [N_DEV, shape[axis] // N_DEV]
    return jnp.moveaxis(full.reshape(shape), axis, 0)


def _block_diag(w):
    hh, d, _ = w.shape
    eye = jnp.eye(hh, dtype=w.dtype)
    return (w[:, :, None, :] * eye[:, None, :, None]).reshape(hh * d, hh * d)


def _block_diag_t(full, hh):
    d = full.shape[0] // hh
    f4 = full.reshape(hh, d, hh, d)
    return jnp.stack([f4[i, :, i, :] for i in range(hh)], axis=0)


def _pad_heads(w, width):
    r = w.shape[0]
    w3 = w.reshape(r, MLA_HEADS, width)
    return jnp.pad(w3, ((0, 0), (0, 0), (0, HEAD_PAD - width))).reshape(r, MLA_HEADS * HEAD_PAD)


def _unpad_heads(w, width):
    r = w.shape[0]
    return w.reshape(r, MLA_HEADS, HEAD_PAD)[:, :, :width].reshape(r, MLA_HEADS * width)


def kernel(x, meta_tokens, ev_norm, ev_w_in, ev_conv_a, ev_conv_b, ev_conv_b_bias, ev_gate_r_w, ev_gate_r_b, ev_gate_i_w, ev_gate_i_b, ev_lru_lambda, ev_w_out, od_norm, od_w_in, od_q_norm, od_kv_norm, od_w_uq, od_w_ukv, od_w_out, ffn_norm, ffn_w_up, ffn_conv_w, ffn_conv_b, ffn_w_down, final_norm, loss_target, m_meta_tokens, m_ev_norm, m_ev_w_in, m_ev_conv_a, m_ev_conv_b, m_ev_conv_b_bias, m_ev_gate_r_w, m_ev_gate_r_b, m_ev_gate_i_w, m_ev_gate_i_b, m_ev_lru_lambda, m_ev_w_out, m_od_norm, m_od_w_in, m_od_q_norm, m_od_kv_norm, m_od_w_uq, m_od_w_ukv, m_od_w_out, m_ffn_norm, m_ffn_w_up, m_ffn_conv_w, m_ffn_conv_b, m_ffn_w_down, m_final_norm, v_meta_tokens, v_ev_norm, v_ev_w_in, v_ev_conv_a, v_ev_conv_b, v_ev_conv_b_bias, v_ev_gate_r_w, v_ev_gate_r_b, v_ev_gate_i_w, v_ev_gate_i_b, v_ev_lru_lambda, v_ev_w_out, v_od_norm, v_od_w_in, v_od_q_norm, v_od_kv_norm, v_od_w_uq, v_od_w_ukv, v_od_w_out, v_ffn_norm, v_ffn_w_up, v_ffn_conv_w, v_ffn_conv_b, v_ffn_w_down, v_final_norm):
    given = dict(locals())
    names = [n for n, _ in PARAMS]
    axis_of = dict(PARAMS)
    w_loc = {n: given[n] for n in names}
    m_loc = {n: given["m_" + n] for n in names}
    v_loc = {n: given["v_" + n] for n in names}
    sharded = [n for n in names if axis_of[n] is not None]
    replicated = [n for n in names if axis_of[n] is None]
    small = [n for n in sharded if n not in BIG]

    nb, seq, d = x.shape
    t_real = N_META + seq
    tp = -(-t_real // ROW_TILE) * ROW_TILE
    m = nb * tp

    small_pack = _pack_rows([w_loc[n] for n in small], LANES, SUBLANES, F32)
    gathered = _exchange("weight_all_gather", [w_loc[n].astype(BF16) for n in BIG] + [small_pack], scatter=False)
    full = {n: w_loc[n] for n in replicated}
    for n, g in zip(BIG, gathered[:-1]):
        full[n] = _unshard(g, axis_of[n])
    flat = gathered[-1].reshape(N_DEV, -1)
    off = 0
    for n in small:
        shard = w_loc[n].shape
        size = math.prod(shard)
        full[n] = _unshard(flat[:, off:off + size].reshape((N_DEV,) + shard), axis_of[n])
        off += size

    tables = _rope_tables(tp)

    def even_params(j):
        w_out = full["ev_w_out"][j]
        return dict(norm=full["ev_norm"][j][None], w_in=full["ev_w_in"][j], conv_a=full["ev_conv_a"][j],
                    conv_b=full["ev_conv_b"][j], conv_b_bias=full["ev_conv_b_bias"][j][None],
                    gate_r=_block_diag(full["ev_gate_r_w"][j]).astype(BF16),
                    gate_i=_block_diag(full["ev_gate_i_w"][j]).astype(BF16),
                    gate_r_b=full["ev_gate_r_b"][j][None], gate_i_b=full["ev_gate_i_b"][j][None],
                    lam=full["ev_lru_lambda"][j][None], w_out=w_out, w_out_a=w_out[:LRU_WIDTH],
                    w_out_b=w_out[LRU_WIDTH:])

    def odd_params(j):
        w_ukv = full["od_w_ukv"][j].reshape(KV_LORA, MLA_HEADS, QK_NOPE + V_HEAD)
        w_uk = w_ukv[:, :, :QK_NOPE].reshape(KV_LORA, MLA_HEADS * QK_NOPE)
        w_uv = w_ukv[:, :, QK_NOPE:].reshape(KV_LORA, MLA_HEADS * V_HEAD)
        w_out = full["od_w_out"][j].reshape(MLA_HEADS, V_HEAD, d)
        w_out = jnp.pad(w_out, ((0, 0), (0, HEAD_PAD - V_HEAD), (0, 0))).reshape(MLA_HEADS * HEAD_PAD, d)
        return dict(norm=full["od_norm"][j][None], w_in=jnp.pad(full["od_w_in"][j], ((0, 0), (0, ODD_IN_PAD - ODD_IN))),
                    q_norm=full["od_q_norm"][j][None], kv_norm=full["od_kv_norm"][j][None],
                    w_uq=_pad_heads(full["od_w_uq"][j], QK_HEAD), w_uk=_pad_heads(w_uk, QK_NOPE),
                    w_uv=_pad_heads(w_uv, V_HEAD), w_out=w_out)

    def ffn_params(layer):
        w_up = full["ffn_w_up"][layer]
        return dict(norm=full["ffn_norm"][layer][None], w_up=w_up, w_up_a=w_up[:, :D_FF], w_up_g=w_up[:, D_FF:],
                    conv_w=full["ffn_conv_w"][layer], conv_b=full["ffn_conv_b"][layer][None],
                    w_down=full["ffn_w_down"][layer])

    meta = jnp.broadcast_to(full["meta_tokens"][None], (nb, N_META, d))
    h0 = jnp.concatenate([meta, x, jnp.zeros((nb, tp - t_real, d), F32)], axis=1).reshape(m, d)
    hcur = h0
    tape = []
    for layer in range(4):
        j = layer // 2
        if layer % 2 == 0:
            mp = even_params(j)
            hcur, saved = _even_fwd(hcur, mp, m, tp, nb)
        else:
            mp = odd_params(j)
            hcur, saved = _odd_fwd(hcur, mp, tables, m, tp, nb)
        fp = ffn_params(layer)
        hcur, fsaved = _ffn_fwd(hcur, fp, m, tp)
        tape.append((mp, saved, fp, fsaved))

    target = jnp.pad(loss_target, ((0, 0), (N_META, tp - t_real), (0, 0))).reshape(m, d)
    dh, d_final_norm, loss_part = _loss_head(hcur, full["final_norm"][None], target, tp, t_real)
    loss = lax.psum(loss_part[0, 0], ("x", "y", "c"))

    grads = {"final_norm": d_final_norm[0]}
    ev_g, od_g, ffn_g = [None, None], [None, None], [None] * 4
    for layer in reversed(range(4)):
        mp, saved, fp, fsaved = tape[layer]
        dh, ffn_g[layer] = _ffn_bwd(dh, fp, fsaved, m, tp)
        if layer % 2 == 0:
            dh, ev_g[layer // 2] = _even_bwd(dh, mp, saved, m, tp, nb)
        else:
            dh, od_g[layer // 2] = _odd_bwd(dh, mp, tables, saved, m, tp, nb)

    dh3 = dh.reshape(nb, tp, d)
    grad_x = dh3[:, N_META:t_real]
    grads["meta_tokens"] = _meta_grad(dh, nb, tp)

    def stack(lst, key, fn=lambda a: a):
        return jnp.stack([fn(g[key]) for g in lst], axis=0)

    grads["ev_norm"] = stack(ev_g, "norm", lambda a: a[0])
    grads["ev_w_in"] = stack(ev_g, "w_in")
    grads["ev_conv_a"] = stack(ev_g, "conv_a")
    grads["ev_conv_b"] = stack(ev_g, "conv_b")
    grads["ev_conv_b_bias"] = stack(ev_g, "conv_b_bias", lambda a: a[0])
    grads["ev_gate_r_w"] = stack(ev_g, "gate_r", lambda a: _block_diag_t(a, 8))
    grads["ev_gate_r_b"] = stack(ev_g, "gate_r_b", lambda a: a[0])
    grads["ev_gate_i_w"] = stack(ev_g, "gate_i", lambda a: _block_diag_t(a, 8))
    grads["ev_gate_i_b"] = stack(ev_g, "gate_i_b", lambda a: a[0])
    grads["ev_lru_lambda"] = stack(ev_g, "lam", lambda a: a[0])
    grads["ev_w_out"] = stack(ev_g, "w_out")
    grads["od_norm"] = stack(od_g, "norm", lambda a: a[0])
    grads["od_w_in"] = stack(od_g, "w_in", lambda a: a[:, :ODD_IN])
    grads["od_q_norm"] = stack(od_g, "q_norm", lambda a: a[0])
    grads["od_kv_norm"] = stack(od_g, "kv_norm", lambda a: a[0])
    grads["od_w_uq"] = stack(od_g, "w_uq", lambda a: _unpad_heads(a, QK_HEAD))

    def ukv(g):
        gk = g["w_uk"].reshape(KV_LORA, MLA_HEADS, HEAD_PAD)[:, :, :QK_NOPE]
        gv = g["w_uv"].reshape(KV_LORA, MLA_HEADS, HEAD_PAD)[:, :, :V_HEAD]
        return jnp.concatenate([gk, gv], axis=2).reshape(KV_LORA, MLA_HEADS * (QK_NOPE + V_HEAD))

    grads["od_w_ukv"] = jnp.stack([ukv(g) for g in od_g], axis=0)
    grads["od_w_out"] = stack(od_g, "w_out", lambda a: a.reshape(MLA_HEADS, HEAD_PAD, d)[:, :V_HEAD].reshape(-1, d))
    grads["ffn_norm"] = stack(ffn_g, "norm", lambda a: a[0])
    grads["ffn_w_up"] = stack(ffn_g, "w_up")
    grads["ffn_conv_w"] = stack(ffn_g, "conv_w")
    grads["ffn_conv_b"] = stack(ffn_g, "conv_b", lambda a: a[0])
    grads["ffn_w_down"] = stack(ffn_g, "w_down")

    order = small + replicated
    slot_parts = [_to_slots(grads[n], axis_of[n]).reshape(N_DEV, -1) for n in small]
    slot_parts += [jnp.broadcast_to(grads[n].reshape(1, -1), (N_DEV, grads[n].size)) for n in replicated]
    g_flat = jnp.concatenate(slot_parts, axis=1)
    n_flat = g_flat.shape[1]
    rows = -(-n_flat // (1024 * 128)) * 128
    g_small = jnp.pad(g_flat, ((0, 0), (0, rows * 1024 - n_flat))).reshape(N_DEV, rows, 1024)

    def rows_of(n):
        shard = w_loc[n].shape
        return (math.prod(shard[:-1]), shard[-1])

    g_big = [_to_slots(grads[n], axis_of[n]).astype(BF16).reshape((N_DEV,) + rows_of(n)) for n in BIG]
    parts = _exchange("grad_exchange", g_big + [g_small], scatter=True)

    g_out, d_out, m_out, v_out = {}, {}, {}, {}
    for n, part in zip(BIG, parts[:-1]):
        res = _reduce_adamw(part, *[t[n].reshape(rows_of(n)) for t in (w_loc, m_loc, v_loc)])
        for out, r in zip((g_out, d_out, m_out, v_out), res):
            out[n] = r.reshape(w_loc[n].shape)

    def flat_local(tree):
        flat = jnp.concatenate([tree[n].reshape(-1) for n in order])
        return jnp.pad(flat, (0, rows * 1024 - n_flat)).reshape(rows, 1024)

    res = _reduce_adamw(parts[-1], flat_local(w_loc), flat_local(m_loc), flat_local(v_loc))
    for out, r in zip((g_out, d_out, m_out, v_out), res):
        flat = r.reshape(-1)
        off = 0
        for n in order:
            size = w_loc[n].size
            out[n] = flat[off:off + size].reshape(w_loc[n].shape)
            off += size
    return (loss, grad_x, *[g_out[n] for n in names], *[d_out[n] for n in names], *[m_out[n] for n in names],
            *[v_out[n] for n in names])
```

```python
import functools
import math

import jax
import jax.numpy as jnp
from jax import lax
from jax.experimental import pallas as pl
from jax.experimental.pallas import tpu as pltpu

F32 = jnp.float32
BF16 = jnp.bfloat16

N_DEV = 8
N_META = 16
EPS = 1e-6
LRU_C = 8.0
MLA_HEADS = 16
QK_NOPE = 64
QK_ROPE = 32
QK_HEAD = QK_NOPE + QK_ROPE
V_HEAD = 64
HEAD_PAD = 128
Q_LORA = 384
KV_LORA = 256
ODD_IN = Q_LORA + KV_LORA + QK_ROPE
ODD_IN_PAD = 768
ROPE_BASE = 10000.0
LRU_WIDTH = 512
D_FF = 2816

ADAM_LR = 0.001
ADAM_B1 = 0.9
ADAM_B2 = 0.999
ADAM_EPS = 1e-08
ADAM_WD = 0.01
ADAM_STEP = 10

ROW_TILE = 384
SUBLANES = 8
LANES = 128
VMEM_LIMIT = 48 * 1024 * 1024
NEG = -1e30

PARAMS = (
    ("meta_tokens", 1), ("ev_norm", None), ("ev_w_in", 2), ("ev_conv_a", 2), ("ev_conv_b", 2),
    ("ev_conv_b_bias", None), ("ev_gate_r_w", None), ("ev_gate_r_b", None), ("ev_gate_i_w", None),
    ("ev_gate_i_b", None), ("ev_lru_lambda", None), ("ev_w_out", 1), ("od_norm", 1), ("od_w_in", 1),
    ("od_q_norm", 1), ("od_kv_norm", 1), ("od_w_uq", 2), ("od_w_ukv", 2), ("od_w_out", 1),
    ("ffn_norm", None), ("ffn_w_up", 2), ("ffn_conv_w", 2), ("ffn_conv_b", None), ("ffn_w_down", 1),
    ("final_norm", None),
)
BIG = ("ev_w_in", "ev_w_out", "od_w_in", "od_w_uq", "od_w_ukv", "od_w_out", "ffn_w_up", "ffn_w_down")


def _cparams(n_grid):
    return pltpu.CompilerParams(dimension_semantics=("arbitrary",) * n_grid, vmem_limit_bytes=VMEM_LIMIT)


def _pick(dim, target):
    if dim <= target:
        return dim
    best = None
    for t in range(LANES, target + 1, LANES):
        if dim % t == 0:
            best = t
    assert best is not None, (dim, target)
    return best


MATMUL_VMEM_BUDGET = 30 * 1024 * 1024
HBM_BYTES_PER_US = 3.0e6
GRID_STEP_US = 0.35


def _tile_candidates(dim):
    return [t for t in range(LANES, dim + 1, LANES) if dim % t == 0] or [dim]


def _matmul_tiles(m, n, k, sa, sb, so, sr):
    best, best_cost = None, None
    for tm in _tile_candidates(m):
        for tn in _tile_candidates(n):
            for tk in _tile_candidates(k):
                vmem = 2 * (tm * tk * sa + tk * tn * sb) + tm * tn * (4 + 2 * so + 2 * sr)
                vmem += (tm * tk * 2 if sa > 2 else 0) + (tk * tn * 2 if sb > 2 else 0)
                if vmem > MATMUL_VMEM_BUDGET:
                    continue
                traffic = m * k * sa * (n // tn) + k * n * sb * (m // tm) + m * n * (so + sr)
                cost = traffic / HBM_BYTES_PER_US + (m // tm) * (n // tn) * (k // tk) * GRID_STEP_US
                if best_cost is None or cost < best_cost:
                    best, best_cost = (tm, tn, tk), cost
    assert best is not None, (m, n, k)
    return best


def _matmul(a, b, mode, out_dtype=F32, residual=None, name="mm"):
    if mode == "nn":
        (m, k), (k2, n) = a.shape, b.shape
    elif mode == "nt":
        (m, k), (n, k2) = a.shape, b.shape
    else:
        (k, m), (k2, n) = a.shape, b.shape
    assert k == k2, (a.shape, b.shape, mode)
    tm, tn, tk = _matmul_tiles(m, n, k, a.dtype.itemsize, b.dtype.itemsize, jnp.dtype(out_dtype).itemsize,
                               0 if residual is None else residual.dtype.itemsize)
    nk = k // tk
    if mode == "tn":
        a_spec = pl.BlockSpec((tk, tm), lambda i, j, kk: (kk, i))
        dims = (((0,), (0,)), ((), ()))
    else:
        a_spec = pl.BlockSpec((tm, tk), lambda i, j, kk: (i, kk))
        dims = (((1,), (1 if mode == "nt" else 0,)), ((), ()))
    if mode == "nt":
        b_spec = pl.BlockSpec((tn, tk), lambda i, j, kk: (j, kk))
    else:
        b_spec = pl.BlockSpec((tk, tn), lambda i, j, kk: (kk, j))
    o_spec = pl.BlockSpec((tm, tn), lambda i, j, kk: (i, j))
    has_res = residual is not None

    def body(*refs):
        if has_res:
            a_ref, b_ref, r_ref, o_ref, acc_ref = refs
        else:
            a_ref, b_ref, o_ref, acc_ref = refs
        kk = pl.program_id(2)

        @pl.when(kk == 0)
        def _():
            acc_ref[...] = jnp.zeros_like(acc_ref)

        acc_ref[...] += lax.dot_general(a_ref[...].astype(BF16), b_ref[...].astype(BF16), dims,
                                        preferred_element_type=F32)

        @pl.when(kk == nk - 1)
        def _():
            out = acc_ref[...]
            if has_res:
                out = out + r_ref[...].astype(F32)
            o_ref[...] = out.astype(o_ref.dtype)

    in_specs = [a_spec, b_spec] + ([o_spec] if has_res else [])
    args = (a, b) + ((residual,) if has_res else ())
    return pl.pallas_call(
        body, out_shape=jax.ShapeDtypeStruct((m, n), out_dtype), grid=(m // tm, n // tn, nk),
        in_specs=in_specs, out_specs=o_spec, scratch_shapes=[pltpu.VMEM((tm, tn), F32)],
        compiler_params=_cparams(3), name=name)(*args)


def _rms_fwd(x, g, name):
    m, c = x.shape
    tm = _pick(m, ROW_TILE)

    def body(x_ref, g_ref, o_ref):
        xf = x_ref[...].astype(F32)
        r = lax.rsqrt(jnp.mean(xf * xf, axis=-1, keepdims=True) + EPS)
        o_ref[...] = (xf * r * g_ref[...]).astype(o_ref.dtype)

    return pl.pallas_call(
        body, out_shape=jax.ShapeDtypeStruct((m, c), BF16), grid=(m // tm,),
        in_specs=[pl.BlockSpec((tm, c), lambda i: (i, 0)), pl.BlockSpec((1, c), lambda i: (0, 0))],
        out_specs=pl.BlockSpec((tm, c), lambda i: (i, 0)), compiler_params=_cparams(1), name=name)(x, g)


def _rms_bwd(x, g, dy, residual, name):
    m, c = x.shape
    tm = _pick(m, ROW_TILE)
    has_res = residual is not None

    def body(*refs):
        if has_res:
            x_ref, g_ref, dy_ref, r_ref, dx_ref, dg_ref = refs
        else:
            x_ref, g_ref, dy_ref, dx_ref, dg_ref = refs
        xf = x_ref[...].astype(F32)
        dyf = dy_ref[...].astype(F32)
        r = lax.rsqrt(jnp.mean(xf * xf, axis=-1, keepdims=True) + EPS)
        xn = xf * r
        dyg = dyf * g_ref[...]
        dx = r * (dyg - xn * jnp.mean(dyg * xn, axis=-1, keepdims=True))
        if has_res:
            dx = dx + r_ref[...]
        dx_ref[...] = dx

        @pl.when(pl.program_id(0) == 0)
        def _():
            dg_ref[...] = jnp.zeros_like(dg_ref)

        dg_ref[...] += jnp.sum(dyf * xn, axis=0, keepdims=True)

    row = pl.BlockSpec((tm, c), lambda i: (i, 0))
    vec = pl.BlockSpec((1, c), lambda i: (0, 0))
    in_specs = [row, vec, row] + ([row] if has_res else [])
    args = (x, g, dy) + ((residual,) if has_res else ())
    return pl.pallas_call(
        body, out_shape=(jax.ShapeDtypeStruct((m, c), F32), jax.ShapeDtypeStruct((1, c), F32)), grid=(m // tm,),
        in_specs=in_specs, out_specs=(row, vec), compiler_params=_cparams(1), name=name)(*args)


def _chan_call(name, fn, m, tp, tc, ncol, row_ins=(), prev_ins=(), next_ins=(), chan_ins=(), row_outs=(),
               red_outs=(), row_split=1):
    tm = _pick(tp, ROW_TILE) // row_split
    tps = tp // tm
    nrow = m // tm
    h8 = tm // SUBLANES
    last8 = m // SUBLANES - 1
    n_in = len(row_ins) + len(prev_ins) + len(next_ins) + len(chan_ins)
    n_r, n_p, n_n = len(row_ins), len(prev_ins), len(next_ins)

    def body(*refs):
        i = pl.program_id(1)
        pos = lax.rem(i, tps)
        at_start = pos == 0
        at_end = pos == tps - 1
        rows = [r[...].astype(F32) for r in refs[:n_r]]
        prevs = [jnp.where(at_start, 0.0, r[...].astype(F32)) for r in refs[n_r:n_r + n_p]]
        nexts = [jnp.where(at_end, 0.0, r[...].astype(F32)) for r in refs[n_r + n_p:n_r + n_p + n_n]]
        chans = [r[...] for r in refs[n_r + n_p + n_n:n_in]]
        out_refs = refs[n_in:n_in + len(row_outs)]
        red_refs = refs[n_in + len(row_outs):]
        row_vals, red_vals = fn(rows, prevs, nexts, chans)
        for ref, val in zip(out_refs, row_vals):
            ref[...] = val.astype(ref.dtype)
        if red_refs:
            @pl.when(i == 0)
            def _():
                for ref in red_refs:
                    ref[...] = jnp.zeros_like(ref)

            for ref, val in zip(red_refs, red_vals):
                ref[...] += val

    in_specs, args = [], []
    for arr, off in row_ins:
        in_specs.append(pl.BlockSpec((tm, tc), lambda j, i, off=off: (i, j + off)))
        args.append(arr)
    for arr, off in prev_ins:
        in_specs.append(pl.BlockSpec((SUBLANES, tc), lambda j, i, off=off: (jnp.maximum(i * h8 - 1, 0), j + off)))
        args.append(arr)
    for arr, off in next_ins:
        in_specs.append(pl.BlockSpec((SUBLANES, tc), lambda j, i, off=off: (jnp.minimum((i + 1) * h8, last8), j + off)))
        args.append(arr)
    for arr, off in chan_ins:
        in_specs.append(pl.BlockSpec((arr.shape[0], tc), lambda j, i, off=off: (0, j + off)))
        args.append(arr)
    out_shape, out_specs = [], []
    for (dt,) in row_outs:
        out_shape.append(jax.ShapeDtypeStruct((m, ncol * tc), dt))
        out_specs.append(pl.BlockSpec((tm, tc), lambda j, i: (i, j)))
    for (k,) in red_outs:
        out_shape.append(jax.ShapeDtypeStruct((k, ncol * tc), F32))
        out_specs.append(pl.BlockSpec((k, tc), lambda j, i: (0, j)))
    return pl.pallas_call(
        body, out_shape=tuple(out_shape), grid=(ncol, nrow), in_specs=in_specs, out_specs=tuple(out_specs),
        compiler_params=_cparams(2), name=name)(*args)


def _shift_down(x, prev8, s):
    if s == 0:
        return x
    xs = pltpu.roll(x, s, 0)
    ps = pltpu.roll(prev8, s, 0)
    rid = lax.broadcasted_iota(jnp.int32, prev8.shape, 0)
    head = jnp.where(rid < s, ps, xs[:SUBLANES])
    return jnp.concatenate([head, xs[SUBLANES:]], axis=0)


def _shift_up(x, next8, s):
    if s == 0:
        return x
    tm = x.shape[0]
    xs = pltpu.roll(x, tm - s, 0)
    ns = pltpu.roll(next8, SUBLANES - s, 0)
    rid = lax.broadcasted_iota(jnp.int32, next8.shape, 0)
    tail = jnp.where(rid >= SUBLANES - s, ns, xs[tm - SUBLANES:])
    return jnp.concatenate([xs[:tm - SUBLANES], tail], axis=0)


def _conv_fwd(x, prev8, w):
    kw = w.shape[0]
    y = w[kw - 1:kw, :] * x
    for k in range(kw - 1):
        y = y + w[k:k + 1, :] * _shift_down(x, prev8, kw - 1 - k)
    return y


def _conv_dw(dy, x, prev8, kw):
    rid = lax.broadcasted_iota(jnp.int32, prev8.shape, 0)
    out = jnp.zeros(prev8.shape, F32)
    for k in range(kw):
        row = jnp.sum(dy * _shift_down(x, prev8, kw - 1 - k), axis=0, keepdims=True)
        out = out + jnp.where(rid == k, row, 0.0)
    return out


def _conv_dx(dy, next8, w):
    kw = w.shape[0]
    dx = w[kw - 1:kw, :] * dy
    for k in range(kw - 1):
        dx = dx + w[k:k + 1, :] * _shift_up(dy, next8, kw - 1 - k)
    return dx


def _sigmoid(x):
    return 1.0 / (1.0 + jnp.exp(-x))


def _expm1(x):
    series = x * (1.0 + x * 0.5 * (1.0 + x * (1.0 / 3.0) * (1.0 + x * 0.25 * (1.0 + x * 0.2))))
    return jnp.where(jnp.abs(x) < 0.3, series, jnp.exp(x) - 1.0)


def _softplus_neg(lam):
    e = jnp.exp(-jnp.abs(lam))
    log1p = jnp.where(e < 1e-2, e * (1.0 - e * (0.5 - e * (1.0 / 3.0))), jnp.log(1.0 + e))
    return jnp.maximum(-lam, 0.0) + log1p


GELU_C = math.sqrt(2.0 / math.pi)


def _gelu(x):
    return 0.5 * x * (1.0 + jnp.tanh(GELU_C * (x + 0.044715 * x * x * x)))


def _gelu_grad(x):
    t = jnp.tanh(GELU_C * (x + 0.044715 * x * x * x))
    return 0.5 * (1.0 + t) + 0.5 * x * (1.0 - t * t) * GELU_C * (1.0 + 3.0 * 0.044715 * x * x)


FFN_COL_TILE = 1408


def _ffn_fwd(x, p, m, tp):
    h = _rms_fwd(x, p["norm"], "ffn_norm")
    u = _matmul(h, p["w_up"], "nn", F32, name="ffn_up")
    tc = FFN_COL_TILE
    ncol = D_FF // tc

    def gate(rows, prevs, nexts, chans):
        ua, ug = rows
        wa, wg, ba, bg = chans
        a = _conv_fwd(ua, prevs[0], wa) + ba
        g = _conv_fwd(ug, prevs[1], wg) + bg
        return [a * _sigmoid(a) * g], []

    (z,) = _chan_call("ffn_gate", gate, m, tp, tc, ncol, row_ins=[(u, 0), (u, ncol)], prev_ins=[(u, 0), (u, ncol)],
                      chan_ins=[(p["conv_w"], 0), (p["conv_w"], ncol), (p["conv_b"], 0), (p["conv_b"], ncol)],
                      row_outs=[(BF16,)])
    out = _matmul(z, p["w_down"], "nn", F32, residual=x, name="ffn_down")
    return out, (x, h, u, z)


def _ffn_bwd(dout, p, saved, m, tp):
    x, h, u, z = saved
    tc = FFN_COL_TILE
    ncol = D_FF // tc
    dz = _matmul(dout, p["w_down"], "nt", F32, name="ffn_down_dx")
    d_w_down = _matmul(z, dout, "tn", F32, name="ffn_down_dw")

    def gate_bwd(rows, prevs, nexts, chans):
        ua, ug, dzv = rows
        wa, wg, ba, bg = chans
        a = _conv_fwd(ua, prevs[0], wa) + ba
        g = _conv_fwd(ug, prevs[1], wg) + bg
        sg = _sigmoid(a)
        da = dzv * g * (sg * (1.0 + a * (1.0 - sg)))
        dg = dzv * a * sg
        return ([da, dg],
                [_conv_dw(da, ua, prevs[0], 3), _conv_dw(dg, ug, prevs[1], 3),
                 jnp.sum(da, axis=0, keepdims=True), jnp.sum(dg, axis=0, keepdims=True)])

    da, dg, dcw_a, dcw_g, dcb_a, dcb_g = _chan_call(
        "ffn_gate_bwd", gate_bwd, m, tp, tc, ncol, row_ins=[(u, 0), (u, ncol), (dz, 0)], prev_ins=[(u, 0), (u, ncol)],
        chan_ins=[(p["conv_w"], 0), (p["conv_w"], ncol), (p["conv_b"], 0), (p["conv_b"], ncol)],
        row_outs=[(F32,), (F32,)], red_outs=[(SUBLANES,), (SUBLANES,), (1,), (1,)], row_split=2)

    def conv_dx(rows, prevs, nexts, chans):
        return [_conv_dx(rows[0], nexts[0], chans[0]), _conv_dx(rows[1], nexts[1], chans[1])], []

    dua, dug = _chan_call("ffn_conv_dx", conv_dx, m, tp, tc, ncol, row_ins=[(da, 0), (dg, 0)],
                          next_ins=[(da, 0), (dg, 0)], chan_ins=[(p["conv_w"], 0), (p["conv_w"], ncol)],
                          row_outs=[(BF16,), (BF16,)])
    d_w_up = jnp.concatenate([_matmul(h, dua, "tn", F32, name="ffn_up_dw_a"),
                              _matmul(h, dug, "tn", F32, name="ffn_up_dw_g")], axis=1)
    dh = _matmul(dua, p["w_up_a"], "nt", F32, name="ffn_up_dx_a")
    dh = _matmul(dug, p["w_up_g"], "nt", F32, residual=dh, name="ffn_up_dx_g")
    dx, d_norm = _rms_bwd(x, p["norm"], dh, dout, "ffn_norm_bwd")
    d_conv_w = jnp.concatenate([dcw_a[:3], dcw_g[:3]], axis=1)
    d_conv_b = jnp.concatenate([dcb_a, dcb_g], axis=1)
    return dx, dict(norm=d_norm, w_up=d_w_up, conv_w=d_conv_w, conv_b=d_conv_b, w_down=d_w_down)


def _to_scan(x, nb, tp):
    return x.reshape(nb, tp, LRU_WIDTH // LANES, LANES).transpose(1, 0, 2, 3).reshape(tp, -1, LANES)


def _from_scan(x, nb, tp):
    return x.reshape(tp, nb, LRU_WIDTH // LANES, LANES).transpose(1, 0, 2, 3).reshape(nb * tp, LRU_WIDTH)


def _scan_fwd(a, u):
    t_len, s, _ = a.shape
    tc = _pick(t_len, 640)
    blk = pl.BlockSpec((tc, s, LANES), lambda i: (i, 0, 0))

    def body(a_ref, u_ref, h_ref, carry):
        @pl.when(pl.program_id(0) == 0)
        def _():
            carry[...] = jnp.zeros_like(carry)

        def step(t, h):
            h = a_ref[t] * h + u_ref[t]
            h_ref[t] = h
            return h

        carry[...] = lax.fori_loop(0, tc, step, carry[...], unroll=8)

    return pl.pallas_call(
        body, out_shape=jax.ShapeDtypeStruct(a.shape, F32), grid=(t_len // tc,), in_specs=[blk, blk], out_specs=blk,
        scratch_shapes=[pltpu.VMEM((s, LANES), F32)], compiler_params=_cparams(1), name="lru_scan")(a, u)


def _scan_bwd(dh, a, h_prev):
    t_len, s, _ = a.shape
    tc = _pick(t_len, 640)
    nb = t_len // tc
    blk = pl.BlockSpec((tc, s, LANES), lambda i: (nb - 1 - i, 0, 0))

    def body(dh_ref, a_ref, hp_ref, du_ref, da_ref, carry):
        @pl.when(pl.program_id(0) == 0)
        def _():
            carry[...] = jnp.zeros_like(carry)

        def step(k, c):
            t = tc - 1 - k
            d = dh_ref[t] + c
            du_ref[t] = d
            da_ref[t] = d * hp_ref[t]
            return a_ref[t] * d

        carry[...] = lax.fori_loop(0, tc, step, carry[...], unroll=8)

    shp = jax.ShapeDtypeStruct(a.shape, F32)
    return pl.pallas_call(
        body, out_shape=(shp, shp), grid=(nb,), in_specs=[blk, blk, blk], out_specs=(blk, blk),
        scratch_shapes=[pltpu.VMEM((s, LANES), F32)], compiler_params=_cparams(1), name="lru_scan_bwd")(dh, a, h_prev)


def _lru_gates(xc, zr, zi, r_b, i_b, lam):
    r = _sigmoid(zr + r_b)
    ig = _sigmoid(zi + i_b)
    sp = _softplus_neg(lam)
    log_a = -LRU_C * r * sp
    a = jnp.exp(log_a)
    mult = jnp.sqrt(-_expm1(2.0 * log_a))
    return r, ig, sp, a, mult


def _even_fwd(x, p, m, tp, nb):
    c = LRU_WIDTH
    h = _rms_fwd(x, p["norm"], "ev_norm")
    u = _matmul(h, p["w_in"], "nn", F32, name="ev_in")

    def pre(rows, prevs, nexts, chans):
        gb, gc, xa, xb = rows
        wa, wb, bias = chans
        pa = gc * xa
        ya = gb * _conv_fwd(pa, prevs[0] * prevs[1], wa)
        xc = _conv_fwd(xb, prevs[2], wb) + bias
        return [ya, xc], []

    ya, xc = _chan_call("ev_pre", pre, m, tp, c, 1, row_ins=[(u, 0), (u, 1), (u, 2), (u, 3)],
                        prev_ins=[(u, 1), (u, 2), (u, 3)],
                        chan_ins=[(p["conv_a"], 0), (p["conv_b"], 0), (p["conv_b_bias"], 0)],
                        row_outs=[(BF16,), (F32,)])
    zr = _matmul(xc, p["gate_r"], "nn", F32, name="ev_gate_r")
    zi = _matmul(xc, p["gate_i"], "nn", F32, name="ev_gate_i")

    def lru_in(rows, prevs, nexts, chans):
        xcv, zrv, ziv = rows
        r, ig, sp, a, mult = _lru_gates(xcv, zrv, ziv, *chans)
        return [a, mult * (ig * xcv)], []

    a, uu = _chan_call("ev_lru_in", lru_in, m, tp, c, 1, row_ins=[(xc, 0), (zr, 0), (zi, 0)],
                       chan_ins=[(p["gate_r_b"], 0), (p["gate_i_b"], 0), (p["lam"], 0)],
                       row_outs=[(F32,), (F32,)])
    a_s = _to_scan(a, nb, tp)
    hs_s = _scan_fwd(a_s, _to_scan(uu, nb, tp))
    hs = _from_scan(hs_s, nb, tp)

    def post(rows, prevs, nexts, chans):
        gate, hv = rows
        return [_gelu(gate) * hv], []

    (yb,) = _chan_call("ev_post", post, m, tp, c, 1, row_ins=[(u, 4), (hs, 0)], row_outs=[(BF16,)])
    out = _matmul(ya, p["w_out_a"], "nn", F32, residual=x, name="ev_out_a")
    out = _matmul(yb, p["w_out_b"], "nn", F32, residual=out, name="ev_out_b")
    return out, (x, h, u, ya, xc, zr, zi, a_s, hs_s, hs, yb)


def _even_bwd(dout, p, saved, m, tp, nb):
    c = LRU_WIDTH
    x, h, u, ya, xc, zr, zi, a_s, hs_s, hs, yb = saved
    dy = _matmul(dout, p["w_out"], "nt", F32, name="ev_out_dx")
    d_w_out = jnp.concatenate([_matmul(ya, dout, "tn", F32, name="ev_out_dw_a"),
                               _matmul(yb, dout, "tn", F32, name="ev_out_dw_b")], axis=0)

    def post_bwd(rows, prevs, nexts, chans):
        dyb, gate, hv = rows
        return [dyb * hv * _gelu_grad(gate), dyb * _gelu(gate)], []

    dgate, dhs = _chan_call("ev_post_bwd", post_bwd, m, tp, c, 1, row_ins=[(dy, 1), (u, 4), (hs, 0)],
                            row_outs=[(F32,), (F32,)])
    h_prev = jnp.concatenate([jnp.zeros_like(hs_s[:1]), hs_s[:-1]], axis=0)
    du_s, da_s = _scan_bwd(_to_scan(dhs, nb, tp), a_s, h_prev)
    du = _from_scan(du_s, nb, tp)
    da = _from_scan(da_s, nb, tp)

    def lru_in_bwd(rows, prevs, nexts, chans):
        duv, dav, xcv, zrv, ziv = rows
        r, ig, sp, a, mult = _lru_gates(xcv, zrv, ziv, *chans)
        dxc = duv * mult * ig
        dig = duv * mult * xcv
        dmult = duv * ig * xcv
        dlog_a = dav * a - dmult * (a * a) / jnp.maximum(mult, 1e-30)
        dr = dlog_a * (-LRU_C * sp)
        dzr = dr * r * (1.0 - r)
        dzi = dig * ig * (1.0 - ig)
        dsp = jnp.sum(dlog_a * (-LRU_C * r), axis=0, keepdims=True)
        dlam = -dsp * _sigmoid(-chans[2])
        return ([dzr, dzi, dxc],
                [jnp.sum(dzr, axis=0, keepdims=True), jnp.sum(dzi, axis=0, keepdims=True), dlam])

    dzr, dzi, dxc, d_r_b, d_i_b, d_lam = _chan_call(
        "ev_lru_in_bwd", lru_in_bwd, m, tp, c, 1, row_ins=[(du, 0), (da, 0), (xc, 0), (zr, 0), (zi, 0)],
        chan_ins=[(p["gate_r_b"], 0), (p["gate_i_b"], 0), (p["lam"], 0)],
        row_outs=[(F32,), (F32,), (F32,)], red_outs=[(1,), (1,), (1,)])
    d_gate_r = _matmul(xc, dzr, "tn", F32, name="ev_gate_r_dw")
    d_gate_i = _matmul(xc, dzi, "tn", F32, name="ev_gate_i_dw")
    dxc = _matmul(dzr, p["gate_r"], "nt", F32, residual=dxc, name="ev_gate_r_dx")
    dxc = _matmul(dzi, p["gate_i"], "nt", F32, residual=dxc, name="ev_gate_i_dx")

    def conv_b_bwd(rows, prevs, nexts, chans):
        dxcv, xb = rows
        return ([_conv_dx(dxcv, nexts[0], chans[0])],
                [_conv_dw(dxcv, xb, prevs[0], 4), jnp.sum(dxcv, axis=0, keepdims=True)])

    dxb, d_conv_b, d_bias = _chan_call(
        "ev_conv_b_bwd", conv_b_bwd, m, tp, c, 1, row_ins=[(dxc, 0), (u, 3)], prev_ins=[(u, 3)], next_ins=[(dxc, 0)],
        chan_ins=[(p["conv_b"], 0)], row_outs=[(F32,)], red_outs=[(SUBLANES,), (1,)])

    def mix_a_bwd(rows, prevs, nexts, chans):
        dya, gb, gc, xa = rows
        (wa,) = chans
        pa = gc * xa
        pa_prev = prevs[0] * prevs[1]
        ca = _conv_fwd(pa, pa_prev, wa)
        dca = dya * gb
        dpa = _conv_dx(dca, nexts[0] * nexts[1], wa)
        return [dya * ca, dpa * xa, dpa * gc], [_conv_dw(dca, pa, pa_prev, 3)]

    dgb, dgc, dxa, d_conv_a = _chan_call(
        "ev_mix_a_bwd", mix_a_bwd, m, tp, c, 1, row_ins=[(dy, 0), (u, 0), (u, 1), (u, 2)],
        prev_ins=[(u, 1), (u, 2)], next_ins=[(dy, 0), (u, 0)], chan_ins=[(p["conv_a"], 0)],
        row_outs=[(F32,), (F32,), (F32,)], red_outs=[(SUBLANES,)])
    du_all = jnp.concatenate([dgb, dgc, dxa, dxb, dgate], axis=1)
    d_w_in = _matmul(h, du_all, "tn", F32, name="ev_in_dw")
    dh = _matmul(du_all, p["w_in"], "nt", F32, name="ev_in_dx")
    dx, d_norm = _rms_bwd(x, p["norm"], dh, dout, "ev_norm_bwd")
    return dx, dict(norm=d_norm, w_in=d_w_in, conv_a=d_conv_a[:3], conv_b=d_conv_b[:4], conv_b_bias=d_bias,
                    gate_r=d_gate_r, gate_r_b=d_r_b, gate_i=d_gate_i, gate_i_b=d_i_b, lam=d_lam, w_out=d_w_out)


def _rope_tables(tp):
    pos = jnp.arange(tp, dtype=F32)
    inv_freq = ROPE_BASE ** (-jnp.arange(0, QK_ROPE, 2, dtype=F32) / QK_ROPE)
    ang = pos[:, None] * inv_freq[None, :]
    cos, sin = jnp.cos(ang), jnp.sin(ang)
    half = QK_ROPE // 2
    one = jnp.ones((tp, QK_NOPE), F32)
    z64 = jnp.zeros((tp, QK_NOPE), F32)
    zh = jnp.zeros((tp, half), F32)
    zt = jnp.zeros((tp, HEAD_PAD - QK_HEAD), F32)
    c_tab = jnp.concatenate([one, cos, cos, zt], axis=1)
    s_lo = jnp.concatenate([z64, -sin, zh, zt], axis=1)
    s_hi = jnp.concatenate([z64, zh, sin, zt], axis=1)
    return c_tab, s_lo, s_hi


def _rope(v, c_tab, s_lo, s_hi):
    half = QK_ROPE // 2
    return v * c_tab + pltpu.roll(v, HEAD_PAD - half, 1) * s_lo + pltpu.roll(v, half, 1) * s_hi


def _rope_t(dv, c_tab, s_lo, s_hi):
    half = QK_ROPE // 2
    return dv * c_tab + pltpu.roll(dv * s_lo, half, 1) + pltpu.roll(dv * s_hi, HEAD_PAD - half, 1)


def _rope_call(name, fn, m, tp, ins, tables, out_dtype, shared_pre=None):
    tm = _pick(tp, ROW_TILE)
    tps = tp // tm
    n = len(ins)
    width = MLA_HEADS * HEAD_PAD

    def body(*refs):
        tabs = [r[...] for r in refs[n:n + 3]]
        shared = [None if fc is None else shared_pre(refs[a][...].astype(F32), *tabs) for a, (_, fc) in enumerate(ins)]
        for hh in range(MLA_HEADS):
            lanes = slice(hh * HEAD_PAD, (hh + 1) * HEAD_PAD)
            vals = [refs[a][:, lanes].astype(F32) if shared[a] is None else shared[a] for a in range(n)]
            refs[n + 3][:, lanes] = fn(*vals, *tabs).astype(out_dtype)

    in_specs, args = [], []
    for arr, fixed_col in ins:
        if fixed_col is None:
            in_specs.append(pl.BlockSpec((tm, width), lambda i: (i, 0)))
        else:
            in_specs.append(pl.BlockSpec((tm, HEAD_PAD), lambda i, fc=fixed_col: (i, fc)))
        args.append(arr)
    for tab in tables:
        in_specs.append(pl.BlockSpec((tm, HEAD_PAD), lambda i: (lax.rem(i, tps), 0)))
        args.append(tab)
    return pl.pallas_call(
        body, out_shape=jax.ShapeDtypeStruct((m, width), out_dtype), grid=(m // tm,),
        in_specs=in_specs, out_specs=pl.BlockSpec((tm, width), lambda i: (i, 0)),
        compiler_params=_cparams(1), name=name)(*args)


def _rope_k_bwd(dk, tables, m, tp):
    tm = _pick(tp, ROW_TILE)
    tps = tp // tm

    def body(dk_ref, c_ref, lo_ref, hi_ref, o_ref):
        acc = dk_ref[:, 0:HEAD_PAD].astype(F32)
        for hh in range(1, MLA_HEADS):
            acc = acc + dk_ref[:, hh * HEAD_PAD:(hh + 1) * HEAD_PAD].astype(F32)
        d = pltpu.roll(_rope_t(acc, c_ref[...], lo_ref[...], hi_ref[...]), QK_NOPE, 1)
        lane = lax.broadcasted_iota(jnp.int32, d.shape, 1)
        o_ref[...] = jnp.where(lane < QK_ROPE, d, 0.0)

    tab = pl.BlockSpec((tm, HEAD_PAD), lambda i: (lax.rem(i, tps), 0))
    return pl.pallas_call(
        body, out_shape=jax.ShapeDtypeStruct((m, HEAD_PAD), F32), grid=(m // tm,),
        in_specs=[pl.BlockSpec((tm, MLA_HEADS * HEAD_PAD), lambda i: (i, 0)), tab, tab, tab],
        out_specs=pl.BlockSpec((tm, HEAD_PAD), lambda i: (i, 0)), compiler_params=_cparams(1),
        name="od_rope_k_bwd")(dk, *tables)


def _causal_mask(row0, col0, shape):
    rows = row0 + lax.broadcasted_iota(jnp.int32, shape, 0)
    cols = col0 + lax.broadcasted_iota(jnp.int32, shape, 1)
    return cols <= rows


NT = (((1,), (1,)), ((), ()))
TN = (((0,), (0,)), ((), ()))
HEADS_PER_STEP = 2
HEAD_STEPS = MLA_HEADS // HEADS_PER_STEP
STEP_LANES = HEADS_PER_STEP * HEAD_PAD


def _flash_fwd(q, k, v, nb, tp):
    tq = _pick(tp, ROW_TILE)
    nq = tp // tq

    def body(q_ref, k_ref, v_ref, o_ref, lse_ref):
        i = pl.program_id(2)
        qbs = [q_ref[:, hd * HEAD_PAD:(hd + 1) * HEAD_PAD] for hd in range(HEADS_PER_STEP)]

        def chunk(j, carry, masked):
            off = pl.multiple_of(j * tq, tq)
            out = []
            for hd in range(HEADS_PER_STEP):
                mx, l, acc = carry[hd]
                lanes = slice(hd * HEAD_PAD, (hd + 1) * HEAD_PAD)
                kb = k_ref[pl.ds(off, tq), lanes]
                vb = v_ref[pl.ds(off, tq), lanes]
                s = lax.dot_general(qbs[hd], kb, NT, preferred_element_type=F32)
                if masked:
                    s = jnp.where(_causal_mask(0, 0, s.shape), s, NEG)
                m_new = jnp.maximum(mx, jnp.max(s, axis=1, keepdims=True))
                alpha = jnp.exp(mx - m_new)
                pr = jnp.exp(s - m_new)
                l = alpha * l + jnp.sum(pr, axis=1, keepdims=True)
                acc = alpha * acc + jnp.dot(pr.astype(BF16), vb, preferred_element_type=F32)
                out.append((m_new, l, acc))
            return tuple(out)

        one = (jnp.full((tq, 1), NEG, F32), jnp.zeros((tq, 1), F32), jnp.zeros((tq, HEAD_PAD), F32))
        carry = lax.fori_loop(0, i, lambda j, c: chunk(j, c, False), (one,) * HEADS_PER_STEP)
        carry = chunk(i, carry, True)
        for hd in range(HEADS_PER_STEP):
            mx, l, acc = carry[hd]
            lanes = slice(hd * HEAD_PAD, (hd + 1) * HEAD_PAD)
            o_ref[:, lanes] = (acc / l).astype(o_ref.dtype)
            lse_ref[:, lanes] = jnp.broadcast_to(mx + jnp.log(l), (tq, HEAD_PAD))

    qspec = pl.BlockSpec((tq, STEP_LANES), lambda b, hh, i: (b * nq + i, hh))
    kvspec = pl.BlockSpec((tp, STEP_LANES), lambda b, hh, i: (b, hh))
    shp = (nb * tp, MLA_HEADS * HEAD_PAD)
    return pl.pallas_call(
        body, out_shape=(jax.ShapeDtypeStruct(shp, BF16), jax.ShapeDtypeStruct(shp, F32)),
        grid=(nb, HEAD_STEPS, nq), in_specs=[qspec, kvspec, kvspec], out_specs=(qspec, qspec),
        compiler_params=_cparams(3), name="od_flash_fwd")(q, k, v)


def _flash_prep(o, do, lse_c, nb, tp):
    tq = _pick(tp, ROW_TILE)
    nq = tp // tq

    def body(o_ref, do_ref, lse_ref, lr_ref, dr_ref):
        for hh in range(MLA_HEADS):
            lanes = slice(hh * HEAD_PAD, (hh + 1) * HEAD_PAD)
            delta = jnp.sum(o_ref[:, lanes].astype(F32) * do_ref[:, lanes].astype(F32), axis=1, keepdims=True)
            lr_ref[hh] = jnp.transpose(lse_ref[:, lanes])[0:SUBLANES, :]
            dr_ref[hh] = jnp.transpose(jnp.broadcast_to(delta, (tq, HEAD_PAD)))[0:SUBLANES, :]

    qspec = pl.BlockSpec((tq, MLA_HEADS * HEAD_PAD), lambda b, i: (b * nq + i, 0))
    rspec = pl.BlockSpec((MLA_HEADS, None, SUBLANES, tq), lambda b, i: (b, i, 0, 0))
    rshape = jax.ShapeDtypeStruct((nb * MLA_HEADS, nq, SUBLANES, tq), F32)
    return pl.pallas_call(
        body, out_shape=(rshape, rshape), grid=(nb, nq), in_specs=[qspec, qspec, qspec],
        out_specs=(rspec, rspec), compiler_params=_cparams(2), name="od_flash_prep")(o, do, lse_c)


def _flash_bwd(q, k, v, do, lse_r, delta_r, nb, tp):
    tq = _pick(tp, ROW_TILE)
    nq = tp // tq

    def body(q_ref, k_ref, v_ref, do_ref, lse_ref, dl_ref, dq_ref, dk_ref, dv_ref):
        j = pl.program_id(2)

        @pl.when(j == 0)
        def _():
            dq_ref[...] = jnp.zeros_like(dq_ref)

        kbs = [k_ref[:, hd * HEAD_PAD:(hd + 1) * HEAD_PAD] for hd in range(HEADS_PER_STEP)]
        vbs = [v_ref[:, hd * HEAD_PAD:(hd + 1) * HEAD_PAD] for hd in range(HEADS_PER_STEP)]

        def chunk(i, carry, masked):
            off = pl.multiple_of(i * tq, tq)
            out = []
            for hd in range(HEADS_PER_STEP):
                dk, dv = carry[hd]
                lanes = slice(hd * HEAD_PAD, (hd + 1) * HEAD_PAD)
                qb = q_ref[pl.ds(off, tq), lanes]
                dob = do_ref[pl.ds(off, tq), lanes]
                lse = lse_ref[hd, i][0:1, :]
                delta = dl_ref[hd, i][0:1, :]
                st = lax.dot_general(kbs[hd], qb, NT, preferred_element_type=F32)
                pt = jnp.exp(st - lse)
                if masked:
                    keys = lax.broadcasted_iota(jnp.int32, st.shape, 0)
                    queries = lax.broadcasted_iota(jnp.int32, st.shape, 1)
                    pt = jnp.where(keys <= queries, pt, 0.0)
                dv = dv + jnp.dot(pt.astype(BF16), dob, preferred_element_type=F32)
                dpt = lax.dot_general(vbs[hd], dob, NT, preferred_element_type=F32)
                dst = (pt * (dpt - delta)).astype(BF16)
                dk = dk + jnp.dot(dst, qb, preferred_element_type=F32)
                dq_ref[pl.ds(off, tq), lanes] += lax.dot_general(dst, kbs[hd], TN, preferred_element_type=F32)
                out.append((dk, dv))
            return tuple(out)

        zero = jnp.zeros((tq, HEAD_PAD), F32)
        carry = chunk(j, ((zero, zero),) * HEADS_PER_STEP, True)
        carry = lax.fori_loop(j + 1, nq, lambda i, c: chunk(i, c, False), carry)
        for hd in range(HEADS_PER_STEP):
            lanes = slice(hd * HEAD_PAD, (hd + 1) * HEAD_PAD)
            dk_ref[:, lanes] = carry[hd][0]
            dv_ref[:, lanes] = carry[hd][1].astype(dv_ref.dtype)

    tspec = pl.BlockSpec((tq, STEP_LANES), lambda b, hh, j: (b * nq + j, hh))
    fullspec = pl.BlockSpec((tp, STEP_LANES), lambda b, hh, j: (b, hh))
    rspec = pl.BlockSpec((HEADS_PER_STEP, nq, SUBLANES, tq), lambda b, hh, j: (b * HEAD_STEPS + hh, 0, 0, 0))
    shp = (nb * tp, MLA_HEADS * HEAD_PAD)
    return pl.pallas_call(
        body, out_shape=(jax.ShapeDtypeStruct(shp, F32), jax.ShapeDtypeStruct(shp, F32),
                         jax.ShapeDtypeStruct(shp, BF16)),
        grid=(nb, HEAD_STEPS, nq), in_specs=[fullspec, tspec, tspec, fullspec, rspec, rspec],
        out_specs=(fullspec, tspec, tspec), compiler_params=_cparams(3),
        name="od_flash_bwd")(q, k, v, do, lse_r, delta_r)


def _odd_fwd(x, p, tables, m, tp, nb):
    scale = QK_HEAD ** -0.5
    h = _rms_fwd(x, p["norm"], "od_norm")
    u = _matmul(h, p["w_in"], "nn", F32, name="od_in")
    cq = u[:, :Q_LORA]
    ckv = u[:, Q_LORA:Q_LORA + KV_LORA]
    cqn = _rms_fwd(cq, p["q_norm"], "od_q_norm")
    ckvn = _rms_fwd(ckv, p["kv_norm"], "od_kv_norm")
    q_raw = _matmul(cqn, p["w_uq"], "nn", F32, name="od_uq")
    k_raw = _matmul(ckvn, p["w_uk"], "nn", F32, name="od_uk")
    v = _matmul(ckvn, p["w_uv"], "nn", BF16, name="od_uv")
    q = _rope_call("od_rope_q", lambda qv, c, lo, hi: _rope(qv, c, lo, hi) * scale, m, tp, [(q_raw, None)], tables,
                   BF16)
    kr_col = (Q_LORA + KV_LORA) // HEAD_PAD
    k = _rope_call("od_rope_k", lambda kv, kr, c, lo, hi: kv + kr, m, tp, [(k_raw, None), (u, kr_col)], tables, BF16,
                   shared_pre=lambda uv, c, lo, hi: _rope(pltpu.roll(uv, QK_NOPE, 1), c, lo, hi))
    o, lse_c = _flash_fwd(q, k, v, nb, tp)
    out = _matmul(o, p["w_out"], "nn", F32, residual=x, name="od_out")
    return out, (x, h, cq, ckv, cqn, ckvn, q, k, v, o, lse_c)


def _odd_bwd(dout, p, tables, saved, m, tp, nb):
    scale = QK_HEAD ** -0.5
    x, h, cq, ckv, cqn, ckvn, q, k, v, o, lse_c = saved
    do = _matmul(dout, p["w_out"], "nt", BF16, name="od_out_dx")
    d_w_out = _matmul(o, dout, "tn", F32, name="od_out_dw")
    lse_r, delta_r = _flash_prep(o, do, lse_c, nb, tp)
    dq, dk, dv = _flash_bwd(q, k, v, do, lse_r, delta_r, nb, tp)
    dq_raw = _rope_call("od_rope_q_bwd", lambda d, c, lo, hi: _rope_t(d, c, lo, hi) * scale, m, tp, [(dq, None)],
                        tables, BF16)
    dkr = _rope_k_bwd(dk, tables, m, tp)
    d_w_uq = _matmul(cqn, dq_raw, "tn", F32, name="od_uq_dw")
    d_w_uk = _matmul(ckvn, dk, "tn", F32, name="od_uk_dw")
    d_w_uv = _matmul(ckvn, dv, "tn", F32, name="od_uv_dw")
    dcqn = _matmul(dq_raw, p["w_uq"], "nt", F32, name="od_uq_dx")
    dckvn = _matmul(dk, p["w_uk"], "nt", F32, name="od_uk_dx")
    dckvn = _matmul(dv, p["w_uv"], "nt", F32, residual=dckvn, name="od_uv_dx")
    dcq, d_q_norm = _rms_bwd(cq, p["q_norm"], dcqn, None, "od_q_norm_bwd")
    dckv, d_kv_norm = _rms_bwd(ckv, p["kv_norm"], dckvn, None, "od_kv_norm_bwd")
    du = jnp.concatenate([dcq, dckv, dkr], axis=1)
    d_w_in = _matmul(h, du, "tn", F32, name="od_in_dw")
    dh = _matmul(du, p["w_in"], "nt", F32, name="od_in_dx")
    dx, d_norm = _rms_bwd(x, p["norm"], dh, dout, "od_norm_bwd")
    return dx, dict(norm=d_norm, w_in=d_w_in, q_norm=d_q_norm, kv_norm=d_kv_norm, w_uq=d_w_uq, w_uk=d_w_uk,
                    w_uv=d_w_uv, w_out=d_w_out)


def _loss_head(hf, g, target, tp, t_real):
    m, c = hf.shape
    tm = _pick(tp, ROW_TILE)
    tps = tp // tm

    def body(x_ref, g_ref, t_ref, dx_ref, dg_ref, loss_ref):
        i = pl.program_id(0)
        xf = x_ref[...]
        r = lax.rsqrt(jnp.mean(xf * xf, axis=-1, keepdims=True) + EPS)
        xn = xf * r
        t_pos = lax.rem(i, tps) * tm + lax.broadcasted_iota(jnp.int32, (tm, 1), 0)
        valid = jnp.logical_and(t_pos >= N_META, t_pos < t_real)
        err = jnp.where(valid, xn * g_ref[...] - t_ref[...], 0.0)
        dyf = err * (1.0 / c)
        dyg = dyf * g_ref[...]
        dx_ref[...] = r * (dyg - xn * jnp.mean(dyg * xn, axis=-1, keepdims=True))

        @pl.when(i == 0)
        def _():
            dg_ref[...] = jnp.zeros_like(dg_ref)
            loss_ref[...] = jnp.zeros_like(loss_ref)

        dg_ref[...] += jnp.sum(dyf * xn, axis=0, keepdims=True)
        loss_ref[...] += (0.5 / c) * jnp.sum(jnp.sum(err * err, axis=1, keepdims=True), axis=0, keepdims=True)

    row = pl.BlockSpec((tm, c), lambda i: (i, 0))
    vec = pl.BlockSpec((1, c), lambda i: (0, 0))
    return pl.pallas_call(
        body, out_shape=(jax.ShapeDtypeStruct((m, c), F32), jax.ShapeDtypeStruct((1, c), F32),
                         jax.ShapeDtypeStruct((1, 1), F32)),
        grid=(m // tm,), in_specs=[row, vec, row], out_specs=(row, vec, pl.BlockSpec((1, 1), lambda i: (0, 0))),
        compiler_params=_cparams(1), name="loss_head")(hf, g, target)


def _meta_grad(dh0, nb, tp):
    d = dh0.shape[1]

    def body(x_ref, o_ref):
        @pl.when(pl.program_id(0) == 0)
        def _():
            o_ref[...] = jnp.zeros_like(o_ref)

        o_ref[...] += x_ref[...]

    return pl.pallas_call(
        body, out_shape=jax.ShapeDtypeStruct((N_META, d), F32), grid=(nb,),
        in_specs=[pl.BlockSpec((N_META, d), lambda b: (b * (tp // N_META), 0))],
        out_specs=pl.BlockSpec((N_META, d), lambda b: (0, 0)), compiler_params=_cparams(1), name="meta_grad")(dh0)


def _mesh_pos():
    x, y, c = lax.axis_index("x"), lax.axis_index("y"), lax.axis_index("c")
    return x, y, c


N_CHIP = 4
MESH_ID = pl.DeviceIdType.MESH


def _peer_chip(x, y, k):
    px = 1 - x if k & 2 else x
    py = 1 - y if k & 1 else y
    return px, py


def _all_gather(arrays):
    n = len(arrays)

    def body(*refs):
        srcs, outs = refs[:n], refs[n:2 * n]
        send_sems, recv_sems, local_sems = refs[2 * n:]
        x, y, c = _mesh_pos()
        me = 4 * x + 2 * y + c
        sibling = (x, y, 1 - c)

        def copy(a, sem, src, block, to):
            return pltpu.make_async_remote_copy(
                src_ref=src, dst_ref=outs[a].at[block], send_sem=send_sems.at[a, sem], recv_sem=recv_sems.at[a, sem],
                device_id=to, device_id_type=MESH_ID)

        local = [pltpu.make_async_copy(srcs[a], outs[a].at[me], local_sems.at[a]) for a in range(n)]
        for cp in local:
            cp.start()
        sends = [copy(a, 0, srcs[a], me, sibling) for a in range(n)]
        for k in range(1, N_CHIP):
            px, py = _peer_chip(x, y, k)
            sends += [copy(a, k, srcs[a], me, (px, py, c)) for a in range(n)]
        for cp in sends:
            cp.start()
        for k in range(1, N_CHIP):
            px, py = _peer_chip(x, y, k)
            block = 4 * px + 2 * py + c
            for a in range(n):
                copy(a, k, srcs[a], block, sibling).wait_recv()
            passed = [copy(a, N_CHIP - 1 + k, outs[a].at[block], block, sibling) for a in range(n)]
            for cp in passed:
                cp.start()
            sends += passed
        for a in range(n):
            copy(a, 0, srcs[a], 4 * x + 2 * y + (1 - c), sibling).wait_recv()
        for k in range(1, N_CHIP):
            px, py = _peer_chip(x, y, k)
            for a in range(n):
                copy(a, N_CHIP - 1 + k, srcs[a], 4 * px + 2 * py + (1 - c), sibling).wait_recv()
        for cp in sends:
            cp.wait_send()
        for cp in local:
            cp.wait()

    any_spec = pl.BlockSpec(memory_space=pl.ANY)
    out_shape = tuple(jax.ShapeDtypeStruct((N_DEV,) + a.shape, a.dtype) for a in arrays)
    return pl.pallas_call(
        body, out_shape=out_shape, in_specs=[any_spec] * n, out_specs=(any_spec,) * n,
        scratch_shapes=[pltpu.SemaphoreType.DMA((n, N_DEV - 1)), pltpu.SemaphoreType.DMA((n, N_DEV - 1)),
                        pltpu.SemaphoreType.DMA((n,))],
        name="weight_all_gather")(*arrays)


def _pair_exchange(arrays):
    n = len(arrays)

    def body(*refs):
        srcs, outs = refs[:n], refs[n:2 * n]
        send_sems, recv_sems = refs[2 * n:]
        x, y, c = _mesh_pos()
        copies = [pltpu.make_async_remote_copy(
            src_ref=srcs[a], dst_ref=outs[a], send_sem=send_sems.at[a], recv_sem=recv_sems.at[a],
            device_id=(x, y, 1 - c), device_id_type=MESH_ID) for a in range(n)]
        for cp in copies:
            cp.start()
        for cp in copies:
            cp.wait()

    any_spec = pl.BlockSpec(memory_space=pl.ANY)
    return pl.pallas_call(
        body, out_shape=tuple(jax.ShapeDtypeStruct(a.shape, a.dtype) for a in arrays), in_specs=[any_spec] * n,
        out_specs=(any_spec,) * n, scratch_shapes=[pltpu.SemaphoreType.DMA((n,)), pltpu.SemaphoreType.DMA((n,))],
        name="grad_pair_exchange")(*arrays)


def _chip_exchange(arrays):
    n = len(arrays)

    def body(*refs):
        srcs, outs = refs[:n], refs[n:2 * n]
        send_sems, recv_sems, local_sems = refs[2 * n:]
        x, y, c = _mesh_pos()
        q = 2 * x + y
        local = [pltpu.make_async_copy(srcs[a].at[q], outs[a].at[q], local_sems.at[a]) for a in range(n)]
        for cp in local:
            cp.start()

        def copy(a, k, to_q, from_q, px, py):
            return pltpu.make_async_remote_copy(
                src_ref=srcs[a].at[to_q], dst_ref=outs[a].at[from_q], send_sem=send_sems.at[a, k - 1],
                recv_sem=recv_sems.at[a, k - 1], device_id=(px, py, c), device_id_type=MESH_ID)

        sends = []
        for k in range(1, N_CHIP):
            px, py = _peer_chip(x, y, k)
            sends += [copy(a, k, 2 * px + py, q, px, py) for a in range(n)]
        for cp in sends:
            cp.start()
        for k in range(1, N_CHIP):
            px, py = _peer_chip(x, y, k)
            for a in range(n):
                copy(a, k, q, 2 * px + py, px, py).wait_recv()
        for cp in sends:
            cp.wait_send()
        for cp in local:
            cp.wait()

    any_spec = pl.BlockSpec(memory_space=pl.ANY)
    return pl.pallas_call(
        body, out_shape=tuple(jax.ShapeDtypeStruct(a.shape, a.dtype) for a in arrays), in_specs=[any_spec] * n,
        out_specs=(any_spec,) * n,
        scratch_shapes=[pltpu.SemaphoreType.DMA((n, N_CHIP - 1)), pltpu.SemaphoreType.DMA((n, N_CHIP - 1)),
                        pltpu.SemaphoreType.DMA((n,))],
        name="grad_chip_exchange")(*arrays)


REDUCE_BLOCK_BYTES = 512 * 1024


def _pair_add(a, b):
    p, r, c = a.shape
    tr = _reduce_rows(r, c)

    def body(a_ref, b_ref, o_ref):
        o_ref[...] = (a_ref[...].astype(F32) + b_ref[...].astype(F32)).astype(o_ref.dtype)

    blk = pl.BlockSpec((None, tr, c), lambda s, i: (s, i, 0))
    return pl.pallas_call(
        body, out_shape=jax.ShapeDtypeStruct(a.shape, a.dtype), grid=(p, r // tr), in_specs=[blk, blk], out_specs=blk,
        compiler_params=_cparams(2), name="grad_pair_add")(a, b)


def _reduce_rows(r, c):
    best = None
    for t in range(16, r + 1, 16):
        if r % t == 0 and t * c * 4 <= REDUCE_BLOCK_BYTES:
            best = t
    assert best is not None, (r, c)
    return best


def _reduce_adamw(parts, w, mom, vel):
    n_parts, r, c = parts.shape
    tr = _reduce_rows(r, c)
    c1 = 1.0 - ADAM_B1 ** ADAM_STEP
    c2 = 1.0 - ADAM_B2 ** ADAM_STEP

    def body(p_ref, w_ref, m_ref, v_ref, g_ref, d_ref, mo_ref, vo_ref):
        g = p_ref[0].astype(F32)
        for s in range(1, n_parts):
            g = g + p_ref[s].astype(F32)
        mn = ADAM_B1 * m_ref[...] + (1.0 - ADAM_B1) * g
        vn = ADAM_B2 * v_ref[...] + (1.0 - ADAM_B2) * (g * g)
        m_hat = mn / c1
        v_hat = vn / c2
        g_ref[...] = g
        d_ref[...] = -ADAM_LR * (m_hat / (jnp.sqrt(v_hat) + ADAM_EPS) + ADAM_WD * w_ref[...])
        mo_ref[...] = mn
        vo_ref[...] = vn

    blk = pl.BlockSpec((tr, c), lambda i: (i, 0))
    shp = jax.ShapeDtypeStruct((r, c), F32)
    return pl.pallas_call(
        body, out_shape=(shp, shp, shp, shp), grid=(r // tr,),
        in_specs=[pl.BlockSpec((n_parts, tr, c), lambda i: (0, i, 0)), blk, blk, blk], out_specs=(blk, blk, blk, blk),
        compiler_params=_cparams(1), name="reduce_adamw")(parts, w, mom, vel)


def _pack_rows(pieces, width, row_multiple, dtype):
    flat = jnp.concatenate([p.astype(dtype).reshape(-1) for p in pieces])
    rows = -(-flat.shape[0] // (width * row_multiple)) * row_multiple
    return jnp.pad(flat, (0, rows * width - flat.shape[0])).reshape(rows, width)


def _unshard(gathered, axis):
    moved = jnp.moveaxis(gathered, 0, axis)
    shape = list(moved.shape)
    shape[axis:axis + 2] = [shape[axis] * shape[axis + 1]]
    return moved.reshape(shape)


def _to_slots(full, axis):
    shape = list(full.shape)
    shape[axis:axis + 1] = [N_DEV, shape[axis] // N_DEV]
    return jnp.moveaxis(full.reshape(shape), axis, 0)


def _block_diag(w):
    hh, d, _ = w.shape
    eye = jnp.eye(hh, dtype=w.dtype)
    return (w[:, :, None, :] * eye[:, None, :, None]).reshape(hh * d, hh * d)


def _block_diag_t(full, hh):
    d = full.shape[0] // hh
    f4 = full.reshape(hh, d, hh, d)
    return jnp.stack([f4[i, :, i, :] for i in range(hh)], axis=0)


def _pad_heads(w, width):
    r = w.shape[0]
    w3 = w.reshape(r, MLA_HEADS, width)
    return jnp.pad(w3, ((0, 0), (0, 0), (0, HEAD_PAD - width))).reshape(r, MLA_HEADS * HEAD_PAD)


def _unpad_heads(w, width):
    r = w.shape[0]
    return w.reshape(r, MLA_HEADS, HEAD_PAD)[:, :, :width].reshape(r, MLA_HEADS * width)


def kernel(x, meta_tokens, ev_norm, ev_w_in, ev_conv_a, ev_conv_b, ev_conv_b_bias, ev_gate_r_w, ev_gate_r_b, ev_gate_i_w, ev_gate_i_b, ev_lru_lambda, ev_w_out, od_norm, od_w_in, od_q_norm, od_kv_norm, od_w_uq, od_w_ukv, od_w_out, ffn_norm, ffn_w_up, ffn_conv_w, ffn_conv_b, ffn_w_down, final_norm, loss_target, m_meta_tokens, m_ev_norm, m_ev_w_in, m_ev_conv_a, m_ev_conv_b, m_ev_conv_b_bias, m_ev_gate_r_w, m_ev_gate_r_b, m_ev_gate_i_w, m_ev_gate_i_b, m_ev_lru_lambda, m_ev_w_out, m_od_norm, m_od_w_in, m_od_q_norm, m_od_kv_norm, m_od_w_uq, m_od_w_ukv, m_od_w_out, m_ffn_norm, m_ffn_w_up, m_ffn_conv_w, m_ffn_conv_b, m_ffn_w_down, m_final_norm, v_meta_tokens, v_ev_norm, v_ev_w_in, v_ev_conv_a, v_ev_conv_b, v_ev_conv_b_bias, v_ev_gate_r_w, v_ev_gate_r_b, v_ev_gate_i_w, v_ev_gate_i_b, v_ev_lru_lambda, v_ev_w_out, v_od_norm, v_od_w_in, v_od_q_norm, v_od_kv_norm, v_od_w_uq, v_od_w_ukv, v_od_w_out, v_ffn_norm, v_ffn_w_up, v_ffn_conv_w, v_ffn_conv_b, v_ffn_w_down, v_final_norm):
    given = dict(locals())
    names = [n for n, _ in PARAMS]
    axis_of = dict(PARAMS)
    w_loc = {n: given[n] for n in names}
    m_loc = {n: given["m_" + n] for n in names}
    v_loc = {n: given["v_" + n] for n in names}
    sharded = [n for n in names if axis_of[n] is not None]
    replicated = [n for n in names if axis_of[n] is None]
    small = [n for n in sharded if n not in BIG]

    nb, seq, d = x.shape
    t_real = N_META + seq
    tp = -(-t_real // ROW_TILE) * ROW_TILE
    m = nb * tp

    small_pack = _pack_rows([w_loc[n] for n in small], LANES, SUBLANES, F32)
    gathered = _all_gather([w_loc[n].astype(BF16) for n in BIG] + [small_pack])
    full = {n: w_loc[n] for n in replicated}
    for n, g in zip(BIG, gathered[:-1]):
        full[n] = _unshard(g, axis_of[n])
    flat = gathered[-1].reshape(N_DEV, -1)
    off = 0
    for n in small:
        shard = w_loc[n].shape
        size = math.prod(shard)
        full[n] = _unshard(flat[:, off:off + size].reshape((N_DEV,) + shard), axis_of[n])
        off += size

    tables = _rope_tables(tp)

    def even_params(j):
        w_out = full["ev_w_out"][j]
        return dict(norm=full["ev_norm"][j][None], w_in=full["ev_w_in"][j], conv_a=full["ev_conv_a"][j],
                    conv_b=full["ev_conv_b"][j], conv_b_bias=full["ev_conv_b_bias"][j][None],
                    gate_r=_block_diag(full["ev_gate_r_w"][j]).astype(BF16),
                    gate_i=_block_diag(full["ev_gate_i_w"][j]).astype(BF16),
                    gate_r_b=full["ev_gate_r_b"][j][None], gate_i_b=full["ev_gate_i_b"][j][None],
                    lam=full["ev_lru_lambda"][j][None], w_out=w_out, w_out_a=w_out[:LRU_WIDTH],
                    w_out_b=w_out[LRU_WIDTH:])

    def odd_params(j):
        w_ukv = full["od_w_ukv"][j].reshape(KV_LORA, MLA_HEADS, QK_NOPE + V_HEAD)
        w_uk = w_ukv[:, :, :QK_NOPE].reshape(KV_LORA, MLA_HEADS * QK_NOPE)
        w_uv = w_ukv[:, :, QK_NOPE:].reshape(KV_LORA, MLA_HEADS * V_HEAD)
        w_out = full["od_w_out"][j].reshape(MLA_HEADS, V_HEAD, d)
        w_out = jnp.pad(w_out, ((0, 0), (0, HEAD_PAD - V_HEAD), (0, 0))).reshape(MLA_HEADS * HEAD_PAD, d)
        return dict(norm=full["od_norm"][j][None], w_in=jnp.pad(full["od_w_in"][j], ((0, 0), (0, ODD_IN_PAD - ODD_IN))),
                    q_norm=full["od_q_norm"][j][None], kv_norm=full["od_kv_norm"][j][None],
                    w_uq=_pad_heads(full["od_w_uq"][j], QK_HEAD), w_uk=_pad_heads(w_uk, QK_NOPE),
                    w_uv=_pad_heads(w_uv, V_HEAD), w_out=w_out)

    def ffn_params(layer):
        w_up = full["ffn_w_up"][layer]
        return dict(norm=full["ffn_norm"][layer][None], w_up=w_up, w_up_a=w_up[:, :D_FF], w_up_g=w_up[:, D_FF:],
                    conv_w=full["ffn_conv_w"][layer], conv_b=full["ffn_conv_b"][layer][None],
                    w_down=full["ffn_w_down"][layer])

    meta = jnp.broadcast_to(full["meta_tokens"][None], (nb, N_META, d))
    h0 = jnp.concatenate([meta, x, jnp.zeros((nb, tp - t_real, d), F32)], axis=1).reshape(m, d)
    hcur = h0
    tape = []
    for layer in range(4):
        j = layer // 2
        if layer % 2 == 0:
            mp = even_params(j)
            hcur, saved = _even_fwd(hcur, mp, m, tp, nb)
        else:
            mp = odd_params(j)
            hcur, saved = _odd_fwd(hcur, mp, tables, m, tp, nb)
        fp = ffn_params(layer)
        hcur, fsaved = _ffn_fwd(hcur, fp, m, tp)
        tape.append((mp, saved, fp, fsaved))

    target = jnp.pad(loss_target, ((0, 0), (N_META, tp - t_real), (0, 0))).reshape(m, d)
    dh, d_final_norm, loss_part = _loss_head(hcur, full["final_norm"][None], target, tp, t_real)
    loss = lax.psum(loss_part[0, 0], ("x", "y", "c"))

    grads = {"final_norm": d_final_norm[0]}
    ev_g, od_g, ffn_g = [None, None], [None, None], [None] * 4
    for layer in reversed(range(4)):
        mp, saved, fp, fsaved = tape[layer]
        dh, ffn_g[layer] = _ffn_bwd(dh, fp, fsaved, m, tp)
        if layer % 2 == 0:
            dh, ev_g[layer // 2] = _even_bwd(dh, mp, saved, m, tp, nb)
        else:
            dh, od_g[layer // 2] = _odd_bwd(dh, mp, tables, saved, m, tp, nb)

    dh3 = dh.reshape(nb, tp, d)
    grad_x = dh3[:, N_META:t_real]
    grads["meta_tokens"] = _meta_grad(dh, nb, tp)

    def stack(lst, key, fn=lambda a: a):
        return jnp.stack([fn(g[key]) for g in lst], axis=0)

    grads["ev_norm"] = stack(ev_g, "norm", lambda a: a[0])
    grads["ev_w_in"] = stack(ev_g, "w_in")
    grads["ev_conv_a"] = stack(ev_g, "conv_a")
    grads["ev_conv_b"] = stack(ev_g, "conv_b")
    grads["ev_conv_b_bias"] = stack(ev_g, "conv_b_bias", lambda a: a[0])
    grads["ev_gate_r_w"] = stack(ev_g, "gate_r", lambda a: _block_diag_t(a, 8))
    grads["ev_gate_r_b"] = stack(ev_g, "gate_r_b", lambda a: a[0])
    grads["ev_gate_i_w"] = stack(ev_g, "gate_i", lambda a: _block_diag_t(a, 8))
    grads["ev_gate_i_b"] = stack(ev_g, "gate_i_b", lambda a: a[0])
    grads["ev_lru_lambda"] = stack(ev_g, "lam", lambda a: a[0])
    grads["ev_w_out"] = stack(ev_g, "w_out")
    grads["od_norm"] = stack(od_g, "norm", lambda a: a[0])
    grads["od_w_in"] = stack(od_g, "w_in", lambda a: a[:, :ODD_IN])
    grads["od_q_norm"] = stack(od_g, "q_norm", lambda a: a[0])
    grads["od_kv_norm"] = stack(od_g, "kv_norm", lambda a: a[0])
    grads["od_w_uq"] = stack(od_g, "w_uq", lambda a: _unpad_heads(a, QK_HEAD))

    def ukv(g):
        gk = g["w_uk"].reshape(KV_LORA, MLA_HEADS, HEAD_PAD)[:, :, :QK_NOPE]
        gv = g["w_uv"].reshape(KV_LORA, MLA_HEADS, HEAD_PAD)[:, :, :V_HEAD]
        return jnp.concatenate([gk, gv], axis=2).reshape(KV_LORA, MLA_HEADS * (QK_NOPE + V_HEAD))

    grads["od_w_ukv"] = jnp.stack([ukv(g) for g in od_g], axis=0)
    grads["od_w_out"] = stack(od_g, "w_out", lambda a: a.reshape(MLA_HEADS, HEAD_PAD, d)[:, :V_HEAD].reshape(-1, d))
    grads["ffn_norm"] = stack(ffn_g, "norm", lambda a: a[0])
    grads["ffn_w_up"] = stack(ffn_g, "w_up")
    grads["ffn_conv_w"] = stack(ffn_g, "conv_w")
    grads["ffn_conv_b"] = stack(ffn_g, "conv_b", lambda a: a[0])
    grads["ffn_w_down"] = stack(ffn_g, "w_down")

    order = small + replicated
    slot_parts = [_to_slots(grads[n], axis_of[n]).reshape(N_DEV, -1) for n in small]
    slot_parts += [jnp.broadcast_to(grads[n].reshape(1, -1), (N_DEV, grads[n].size)) for n in replicated]
    g_flat = jnp.concatenate(slot_parts, axis=1)
    n_flat = g_flat.shape[1]
    rows = -(-n_flat // (1024 * 128)) * 128
    g_small = jnp.pad(g_flat, ((0, 0), (0, rows * 1024 - n_flat))).reshape(N_DEV, rows, 1024)

    def rows_of(n):
        shard = w_loc[n].shape
        return (math.prod(shard[:-1]), shard[-1])

    g_big = [_to_slots(grads[n], axis_of[n]).astype(BF16).reshape((N_DEV,) + rows_of(n)) for n in BIG]
    core = lax.axis_index("c")
    by_core = [jnp.swapaxes(g.reshape((N_CHIP, 2) + g.shape[1:]), 0, 1) for g in g_big + [g_small]]
    mine = [lax.dynamic_index_in_dim(g, core, 0, keepdims=False) for g in by_core]
    theirs = [lax.dynamic_index_in_dim(g, 1 - core, 0, keepdims=False) for g in by_core]
    from_sibling = _pair_exchange(theirs)
    parts = _chip_exchange([_pair_add(a, b) for a, b in zip(mine, from_sibling)])

    g_out, d_out, m_out, v_out = {}, {}, {}, {}
    for n, part in zip(BIG, parts[:-1]):
        res = _reduce_adamw(part, *[t[n].reshape(rows_of(n)) for t in (w_loc, m_loc, v_loc)])
        for out, r in zip((g_out, d_out, m_out, v_out), res):
            out[n] = r.reshape(w_loc[n].shape)

    def flat_local(tree):
        flat = jnp.concatenate([tree[n].reshape(-1) for n in order])
        return jnp.pad(flat, (0, rows * 1024 - n_flat)).reshape(rows, 1024)

    res = _reduce_adamw(parts[-1], flat_local(w_loc), flat_local(m_loc), flat_local(v_loc))
    for out, r in zip((g_out, d_out, m_out, v_out), res):
        flat = r.reshape(-1)
        off = 0
        for n in order:
            size = w_loc[n].size
            out[n] = flat[off:off + size].reshape(w_loc[n].shape)
            off += size
    return (loss, grad_x, *[g_out[n] for n in names], *[d_out[n] for n in names], *[m_out[n] for n in names],
            *[v_out[n] for n in names])
```

```python
import functools
import math

import jax
import jax.numpy as jnp
from jax import lax
from jax.experimental import pallas as pl
from jax.experimental.pallas import tpu as pltpu

F32 = jnp.float32
BF16 = jnp.bfloat16

N_DEV = 8
N_META = 16
EPS = 1e-6
LRU_C = 8.0
MLA_HEADS = 16
QK_NOPE = 64
QK_ROPE = 32
QK_HEAD = QK_NOPE + QK_ROPE
V_HEAD = 64
HEAD_PAD = 128
Q_LORA = 384
KV_LORA = 256
ODD_IN = Q_LORA + KV_LORA + QK_ROPE
ODD_IN_PAD = 768
ROPE_BASE = 10000.0
LRU_WIDTH = 512
D_FF = 2816

ADAM_LR = 0.001
ADAM_B1 = 0.9
ADAM_B2 = 0.999
ADAM_EPS = 1e-08
ADAM_WD = 0.01
ADAM_STEP = 10

ROW_TILE = 384
SUBLANES = 8
HALO_ROWS = 16
LANES = 128
VMEM_LIMIT = 48 * 1024 * 1024
NEG = -1e30

PARAMS = (
    ("meta_tokens", 1), ("ev_norm", None), ("ev_w_in", 2), ("ev_conv_a", 2), ("ev_conv_b", 2),
    ("ev_conv_b_bias", None), ("ev_gate_r_w", None), ("ev_gate_r_b", None), ("ev_gate_i_w", None),
    ("ev_gate_i_b", None), ("ev_lru_lambda", None), ("ev_w_out", 1), ("od_norm", 1), ("od_w_in", 1),
    ("od_q_norm", 1), ("od_kv_norm", 1), ("od_w_uq", 2), ("od_w_ukv", 2), ("od_w_out", 1),
    ("ffn_norm", None), ("ffn_w_up", 2), ("ffn_conv_w", 2), ("ffn_conv_b", None), ("ffn_w_down", 1),
    ("final_norm", None),
)
BIG = ("ev_w_in", "ev_w_out", "od_w_in", "od_w_uq", "od_w_ukv", "od_w_out", "ffn_w_up", "ffn_w_down")


def _cparams(n_grid):
    return pltpu.CompilerParams(dimension_semantics=("arbitrary",) * n_grid, vmem_limit_bytes=VMEM_LIMIT)


def _pick(dim, target):
    if dim <= target:
        return dim
    best = None
    for t in range(LANES, target + 1, LANES):
        if dim % t == 0:
            best = t
    assert best is not None, (dim, target)
    return best


MATMUL_VMEM_BUDGET = 30 * 1024 * 1024
HBM_BYTES_PER_US = 3.0e6
MXU_FLOPS_PER_US = 9.0e8
ACC_BYTES_PER_US = 7.6e6
GRID_STEP_US = 0.35


def _tile_candidates(dim):
    return [t for t in range(LANES, dim + 1, LANES) if dim % t == 0] or [dim]


def _matmul_tiles(m, n, k, sa, sb, so, sr):
    best, best_cost = None, None
    for tm in _tile_candidates(m):
        for tn in _tile_candidates(n):
            for tk in _tile_candidates(k):
                nk = k // tk
                vmem = 2 * (tm * tk * sa + tk * tn * sb) + tm * tn * ((4 if nk > 1 else 0) + 2 * so + 2 * sr)
                vmem += (tm * tk * 2 if sa > 2 else 0) + (tk * tn * 2 if sb > 2 else 0) + tm * tn * 4
                if vmem > MATMUL_VMEM_BUDGET:
                    continue
                steps = (m // tm) * (n // tn) * nk
                traffic = m * k * sa * (n // tn) + k * n * sb * (m // tm) + m * n * (so + sr)
                acc_us = steps * tm * tn * 4 / ACC_BYTES_PER_US if nk > 1 else 0.0
                cost = max(traffic / HBM_BYTES_PER_US, 2.0 * m * n * k / MXU_FLOPS_PER_US + acc_us) + steps * GRID_STEP_US
                if best_cost is None or cost < best_cost:
                    best, best_cost = (tm, tn, tk), cost
    assert best is not None, (m, n, k)
    return best


def _matmul(a, b, mode, out_dtype=F32, residual=None, name="mm"):
    if mode == "nn":
        (m, k), (k2, n) = a.shape, b.shape
    elif mode == "nt":
        (m, k), (n, k2) = a.shape, b.shape
    else:
        (k, m), (k2, n) = a.shape, b.shape
    assert k == k2, (a.shape, b.shape, mode)
    tm, tn, tk = _matmul_tiles(m, n, k, a.dtype.itemsize, b.dtype.itemsize, jnp.dtype(out_dtype).itemsize,
                               0 if residual is None else residual.dtype.itemsize)
    nk = k // tk
    if mode == "tn":
        a_spec = pl.BlockSpec((tk, tm), lambda i, j, kk: (kk, i))
        dims = (((0,), (0,)), ((), ()))
    else:
        a_spec = pl.BlockSpec((tm, tk), lambda i, j, kk: (i, kk))
        dims = (((1,), (1 if mode == "nt" else 0,)), ((), ()))
    if mode == "nt":
        b_spec = pl.BlockSpec((tn, tk), lambda i, j, kk: (j, kk))
    else:
        b_spec = pl.BlockSpec((tk, tn), lambda i, j, kk: (kk, j))
    o_spec = pl.BlockSpec((tm, tn), lambda i, j, kk: (i, j))
    has_res = residual is not None

    def body(*refs):
        a_ref, b_ref = refs[:2]
        r_ref = refs[2] if has_res else None
        o_ref = refs[3] if has_res else refs[2]
        part = lax.dot_general(a_ref[...].astype(BF16), b_ref[...].astype(BF16), dims, preferred_element_type=F32)

        def finish(out):
            if has_res:
                out = out + r_ref[...].astype(F32)
            o_ref[...] = out.astype(o_ref.dtype)

        if nk == 1:
            finish(part)
            return
        acc_ref = refs[-1]
        kk = pl.program_id(2)

        @pl.when(kk == 0)
        def _():
            acc_ref[...] = part

        @pl.when(kk > 0)
        def _():
            acc_ref[...] += part

        @pl.when(kk == nk - 1)
        def _():
            finish(acc_ref[...])

    in_specs = [a_spec, b_spec] + ([o_spec] if has_res else [])
    args = (a, b) + ((residual,) if has_res else ())
    return pl.pallas_call(
        body, out_shape=jax.ShapeDtypeStruct((m, n), out_dtype), grid=(m // tm, n // tn, nk),
        in_specs=in_specs, out_specs=o_spec, scratch_shapes=[pltpu.VMEM((tm, tn), F32)] if nk > 1 else [],
        compiler_params=_cparams(3), name=name)(*args)


def _rms_fwd(x, g, name):
    m, c = x.shape
    tm = _pick(m, ROW_TILE)

    def body(x_ref, g_ref, o_ref):
        xf = x_ref[...].astype(F32)
        r = lax.rsqrt(jnp.mean(xf * xf, axis=-1, keepdims=True) + EPS)
        o_ref[...] = (xf * r * g_ref[...]).astype(o_ref.dtype)

    return pl.pallas_call(
        body, out_shape=jax.ShapeDtypeStruct((m, c), BF16), grid=(m // tm,),
        in_specs=[pl.BlockSpec((tm, c), lambda i: (i, 0)), pl.BlockSpec((1, c), lambda i: (0, 0))],
        out_specs=pl.BlockSpec((tm, c), lambda i: (i, 0)), compiler_params=_cparams(1), name=name)(x, g)


def _rms_bwd(x, g, dy, residual, name):
    m, c = x.shape
    tm = _pick(m, ROW_TILE)
    has_res = residual is not None

    def body(*refs):
        if has_res:
            x_ref, g_ref, dy_ref, r_ref, dx_ref, dg_ref = refs
        else:
            x_ref, g_ref, dy_ref, dx_ref, dg_ref = refs
        xf = x_ref[...].astype(F32)
        dyf = dy_ref[...].astype(F32)
        r = lax.rsqrt(jnp.mean(xf * xf, axis=-1, keepdims=True) + EPS)
        xn = xf * r
        dyg = dyf * g_ref[...]
        dx = r * (dyg - xn * jnp.mean(dyg * xn, axis=-1, keepdims=True))
        if has_res:
            dx = dx + r_ref[...]
        dx_ref[...] = dx

        @pl.when(pl.program_id(0) == 0)
        def _():
            dg_ref[...] = jnp.zeros_like(dg_ref)

        dg_ref[...] += jnp.sum(dyf * xn, axis=0, keepdims=True)

    row = pl.BlockSpec((tm, c), lambda i: (i, 0))
    vec = pl.BlockSpec((1, c), lambda i: (0, 0))
    in_specs = [row, vec, row] + ([row] if has_res else [])
    args = (x, g, dy) + ((residual,) if has_res else ())
    return pl.pallas_call(
        body, out_shape=(jax.ShapeDtypeStruct((m, c), F32), jax.ShapeDtypeStruct((1, c), F32)), grid=(m // tm,),
        in_specs=in_specs, out_specs=(row, vec), compiler_params=_cparams(1), name=name)(*args)


def _chan_call(name, fn, m, tp, tc, ncol, row_ins=(), prev_ins=(), next_ins=(), chan_ins=(), row_outs=(),
               red_outs=(), row_split=1):
    tm = _pick(tp, ROW_TILE) // row_split
    tps = tp // tm
    nrow = m // tm
    halo_blocks = tm // HALO_ROWS
    last_halo = m // HALO_ROWS - 1
    n_in = len(row_ins) + len(prev_ins) + len(next_ins) + len(chan_ins)
    n_r, n_p, n_n = len(row_ins), len(prev_ins), len(next_ins)

    def body(*refs):
        i = pl.program_id(1)
        pos = lax.rem(i, tps)
        at_start = pos == 0
        at_end = pos == tps - 1
        rows = [r[...].astype(F32) for r in refs[:n_r]]
        prevs = [jnp.where(at_start, 0.0, r[...].astype(F32)[SUBLANES:]) for r in refs[n_r:n_r + n_p]]
        nexts = [jnp.where(at_end, 0.0, r[...].astype(F32)[:SUBLANES]) for r in refs[n_r + n_p:n_r + n_p + n_n]]
        chans = [r[...] for r in refs[n_r + n_p + n_n:n_in]]
        out_refs = refs[n_in:n_in + len(row_outs)]
        red_refs = refs[n_in + len(row_outs):]
        row_vals, red_vals = fn(rows, prevs, nexts, chans)
        for ref, val in zip(out_refs, row_vals):
            ref[...] = val.astype(ref.dtype)
        if red_refs:
            @pl.when(i == 0)
            def _():
                for ref in red_refs:
                    ref[...] = jnp.zeros_like(ref)

            for ref, val in zip(red_refs, red_vals):
                ref[...] += val

    in_specs, args = [], []
    for arr, off in row_ins:
        in_specs.append(pl.BlockSpec((tm, tc), lambda j, i, off=off: (i, j + off)))
        args.append(arr)
    for arr, off in prev_ins:
        in_specs.append(pl.BlockSpec((HALO_ROWS, tc),
                                     lambda j, i, off=off: (jnp.maximum(i * halo_blocks - 1, 0), j + off)))
        args.append(arr)
    for arr, off in next_ins:
        in_specs.append(pl.BlockSpec((HALO_ROWS, tc),
                                     lambda j, i, off=off: (jnp.minimum((i + 1) * halo_blocks, last_halo), j + off)))
        args.append(arr)
    for arr, off in chan_ins:
        in_specs.append(pl.BlockSpec((arr.shape[0], tc), lambda j, i, off=off: (0, j + off)))
        args.append(arr)
    out_shape, out_specs = [], []
    for (dt,) in row_outs:
        out_shape.append(jax.ShapeDtypeStruct((m, ncol * tc), dt))
        out_specs.append(pl.BlockSpec((tm, tc), lambda j, i: (i, j)))
    for (k,) in red_outs:
        out_shape.append(jax.ShapeDtypeStruct((k, ncol * tc), F32))
        out_specs.append(pl.BlockSpec((k, tc), lambda j, i: (0, j)))
    return pl.pallas_call(
        body, out_shape=tuple(out_shape), grid=(ncol, nrow), in_specs=in_specs, out_specs=tuple(out_specs),
        compiler_params=_cparams(2), name=name)(*args)


def _shift_down(x, prev8, s):
    if s == 0:
        return x
    xs = pltpu.roll(x, s, 0)
    ps = pltpu.roll(prev8, s, 0)
    rid = lax.broadcasted_iota(jnp.int32, prev8.shape, 0)
    head = jnp.where(rid < s, ps, xs[:SUBLANES])
    return jnp.concatenate([head, xs[SUBLANES:]], axis=0)


def _shift_up(x, next8, s):
    if s == 0:
        return x
    tm = x.shape[0]
    xs = pltpu.roll(x, tm - s, 0)
    ns = pltpu.roll(next8, SUBLANES - s, 0)
    rid = lax.broadcasted_iota(jnp.int32, next8.shape, 0)
    tail = jnp.where(rid >= SUBLANES - s, ns, xs[tm - SUBLANES:])
    return jnp.concatenate([xs[:tm - SUBLANES], tail], axis=0)


def _taps(x, prev8, kw):
    return [_shift_down(x, prev8, kw - 1 - k) for k in range(kw)]


def _conv_taps(taps, w):
    y = w[0:1, :] * taps[0]
    for k in range(1, len(taps)):
        y = y + w[k:k + 1, :] * taps[k]
    return y


def _conv_dw_taps(dy, taps):
    shape = (SUBLANES, dy.shape[1])
    rid = lax.broadcasted_iota(jnp.int32, shape, 0)
    out = jnp.zeros(shape, F32)
    for k, tap in enumerate(taps):
        out = out + jnp.where(rid == k, jnp.sum(dy * tap, axis=0, keepdims=True), 0.0)
    return out


def _conv_fwd(x, prev8, w):
    return _conv_taps(_taps(x, prev8, w.shape[0]), w)


def _conv_dw(dy, x, prev8, kw):
    return _conv_dw_taps(dy, _taps(x, prev8, kw))


def _conv_dx(dy, next8, w):
    kw = w.shape[0]
    dx = w[kw - 1:kw, :] * dy
    for k in range(kw - 1):
        dx = dx + w[k:k + 1, :] * _shift_up(dy, next8, kw - 1 - k)
    return dx


def _sigmoid(x):
    return 1.0 / (1.0 + jnp.exp(-x))


def _expm1(x):
    series = x * (1.0 + x * 0.5 * (1.0 + x * (1.0 / 3.0) * (1.0 + x * 0.25 * (1.0 + x * 0.2))))
    return jnp.where(jnp.abs(x) < 0.3, series, jnp.exp(x) - 1.0)


def _softplus_neg(lam):
    e = jnp.exp(-jnp.abs(lam))
    log1p = jnp.where(e < 1e-2, e * (1.0 - e * (0.5 - e * (1.0 / 3.0))), jnp.log(1.0 + e))
    return jnp.maximum(-lam, 0.0) + log1p


GELU_C = math.sqrt(2.0 / math.pi)


def _gelu(x):
    return 0.5 * x * (1.0 + jnp.tanh(GELU_C * (x + 0.044715 * x * x * x)))


def _gelu_grad(x):
    t = jnp.tanh(GELU_C * (x + 0.044715 * x * x * x))
    return 0.5 * (1.0 + t) + 0.5 * x * (1.0 - t * t) * GELU_C * (1.0 + 3.0 * 0.044715 * x * x)


FFN_COL_TILE = 1408


def _ffn_fwd(x, p, m, tp):
    h = _rms_fwd(x, p["norm"], "ffn_norm")
    u = _matmul(h, p["w_up"], "nn", BF16, name="ffn_up")
    tc = FFN_COL_TILE
    ncol = D_FF // tc

    def gate(rows, prevs, nexts, chans):
        ua, ug = rows
        wa, wg, ba, bg = chans
        a = _conv_fwd(ua, prevs[0], wa) + ba
        g = _conv_fwd(ug, prevs[1], wg) + bg
        return [a * _sigmoid(a) * g], []

    (z,) = _chan_call("ffn_gate", gate, m, tp, tc, ncol, row_ins=[(u, 0), (u, ncol)], prev_ins=[(u, 0), (u, ncol)],
                      chan_ins=[(p["conv_w"], 0), (p["conv_w"], ncol), (p["conv_b"], 0), (p["conv_b"], ncol)],
                      row_outs=[(BF16,)])
    out = _matmul(z, p["w_down"], "nn", F32, residual=x, name="ffn_down")
    return out, (x, h, u, z)


def _ffn_bwd(dout, p, saved, m, tp):
    x, h, u, z = saved
    tc = FFN_COL_TILE
    ncol = D_FF // tc
    dz = _matmul(dout, p["w_down"], "nt", F32, name="ffn_down_dx")
    d_w_down = _matmul(z, dout, "tn", F32, name="ffn_down_dw")

    def gate_bwd(rows, prevs, nexts, chans):
        ua, ug, dzv = rows
        wa, wg, ba, bg = chans
        taps_a = _taps(ua, prevs[0], 3)
        taps_g = _taps(ug, prevs[1], 3)
        a = _conv_taps(taps_a, wa) + ba
        g = _conv_taps(taps_g, wg) + bg
        sg = _sigmoid(a)
        da = dzv * g * (sg * (1.0 + a * (1.0 - sg)))
        dg = dzv * a * sg
        return ([da, dg],
                [_conv_dw_taps(da, taps_a), _conv_dw_taps(dg, taps_g),
                 jnp.sum(da, axis=0, keepdims=True), jnp.sum(dg, axis=0, keepdims=True)])

    da, dg, dcw_a, dcw_g, dcb_a, dcb_g = _chan_call(
        "ffn_gate_bwd", gate_bwd, m, tp, tc, ncol, row_ins=[(u, 0), (u, ncol), (dz, 0)], prev_ins=[(u, 0), (u, ncol)],
        chan_ins=[(p["conv_w"], 0), (p["conv_w"], ncol), (p["conv_b"], 0), (p["conv_b"], ncol)],
        row_outs=[(F32,), (F32,)], red_outs=[(SUBLANES,), (SUBLANES,), (1,), (1,)], row_split=2)

    def conv_dx(rows, prevs, nexts, chans):
        return [_conv_dx(rows[0], nexts[0], chans[0]), _conv_dx(rows[1], nexts[1], chans[1])], []

    dua, dug = _chan_call("ffn_conv_dx", conv_dx, m, tp, tc, ncol, row_ins=[(da, 0), (dg, 0)],
                          next_ins=[(da, 0), (dg, 0)], chan_ins=[(p["conv_w"], 0), (p["conv_w"], ncol)],
                          row_outs=[(BF16,), (BF16,)])
    d_w_up = jnp.concatenate([_matmul(h, dua, "tn", F32, name="ffn_up_dw_a"),
                              _matmul(h, dug, "tn", F32, name="ffn_up_dw_g")], axis=1)
    dh = _matmul(dua, p["w_up_a"], "nt", F32, name="ffn_up_dx_a")
    dh = _matmul(dug, p["w_up_g"], "nt", F32, residual=dh, name="ffn_up_dx_g")
    dx, d_norm = _rms_bwd(x, p["norm"], dh, dout, "ffn_norm_bwd")
    d_conv_w = jnp.concatenate([dcw_a[:3], dcw_g[:3]], axis=1)
    d_conv_b = jnp.concatenate([dcb_a, dcb_g], axis=1)
    return dx, dict(norm=d_norm, w_up=d_w_up, conv_w=d_conv_w, conv_b=d_conv_b, w_down=d_w_down)


def _to_scan(x, nb, tp):
    return x.reshape(nb, tp, LRU_WIDTH // LANES, LANES).transpose(1, 0, 2, 3).reshape(tp, -1, LANES)


def _from_scan(x, nb, tp):
    return x.reshape(tp, nb, LRU_WIDTH // LANES, LANES).transpose(1, 0, 2, 3).reshape(nb * tp, LRU_WIDTH)


def _scan_fwd(a, u):
    t_len, s, _ = a.shape
    tc = _pick(t_len, 640)
    blk = pl.BlockSpec((tc, s, LANES), lambda i: (i, 0, 0))

    def body(a_ref, u_ref, h_ref, carry):
        @pl.when(pl.program_id(0) == 0)
        def _():
            carry[...] = jnp.zeros_like(carry)

        def step(t, h):
            h = a_ref[t] * h + u_ref[t]
            h_ref[t] = h
            return h

        carry[...] = lax.fori_loop(0, tc, step, carry[...], unroll=8)

    return pl.pallas_call(
        body, out_shape=jax.ShapeDtypeStruct(a.shape, F32), grid=(t_len // tc,), in_specs=[blk, blk], out_specs=blk,
        scratch_shapes=[pltpu.VMEM((s, LANES), F32)], compiler_params=_cparams(1), name="lru_scan")(a, u)


def _scan_bwd(dh, a, h_prev):
    t_len, s, _ = a.shape
    tc = _pick(t_len, 640)
    nb = t_len // tc
    blk = pl.BlockSpec((tc, s, LANES), lambda i: (nb - 1 - i, 0, 0))

    def body(dh_ref, a_ref, hp_ref, du_ref, da_ref, carry):
        @pl.when(pl.program_id(0) == 0)
        def _():
            carry[...] = jnp.zeros_like(carry)

        def step(k, c):
            t = tc - 1 - k
            d = dh_ref[t] + c
            du_ref[t] = d
            da_ref[t] = d * hp_ref[t]
            return a_ref[t] * d

        carry[...] = lax.fori_loop(0, tc, step, carry[...], unroll=8)

    shp = jax.ShapeDtypeStruct(a.shape, F32)
    return pl.pallas_call(
        body, out_shape=(shp, shp), grid=(nb,), in_specs=[blk, blk, blk], out_specs=(blk, blk),
        scratch_shapes=[pltpu.VMEM((s, LANES), F32)], compiler_params=_cparams(1), name="lru_scan_bwd")(dh, a, h_prev)


def _lru_gates(xc, zr, zi, r_b, i_b, lam):
    r = _sigmoid(zr + r_b)
    ig = _sigmoid(zi + i_b)
    sp = _softplus_neg(lam)
    log_a = -LRU_C * r * sp
    a = jnp.exp(log_a)
    mult = jnp.sqrt(-_expm1(2.0 * log_a))
    return r, ig, sp, a, mult


def _even_fwd(x, p, m, tp, nb):
    c = LRU_WIDTH
    h = _rms_fwd(x, p["norm"], "ev_norm")
    u = _matmul(h, p["w_in"], "nn", F32, name="ev_in")

    def pre(rows, prevs, nexts, chans):
        gb, gc, xa, xb = rows
        wa, wb, bias = chans
        pa = gc * xa
        ya = gb * _conv_fwd(pa, prevs[0] * prevs[1], wa)
        xc = _conv_fwd(xb, prevs[2], wb) + bias
        return [ya, xc], []

    ya, xc = _chan_call("ev_pre", pre, m, tp, c, 1, row_ins=[(u, 0), (u, 1), (u, 2), (u, 3)],
                        prev_ins=[(u, 1), (u, 2), (u, 3)],
                        chan_ins=[(p["conv_a"], 0), (p["conv_b"], 0), (p["conv_b_bias"], 0)],
                        row_outs=[(BF16,), (F32,)])
    zr = _matmul(xc, p["gate_r"], "nn", F32, name="ev_gate_r")
    zi = _matmul(xc, p["gate_i"], "nn", F32, name="ev_gate_i")

    def lru_in(rows, prevs, nexts, chans):
        xcv, zrv, ziv = rows
        r, ig, sp, a, mult = _lru_gates(xcv, zrv, ziv, *chans)
        return [a, mult * (ig * xcv)], []

    a, uu = _chan_call("ev_lru_in", lru_in, m, tp, c, 1, row_ins=[(xc, 0), (zr, 0), (zi, 0)],
                       chan_ins=[(p["gate_r_b"], 0), (p["gate_i_b"], 0), (p["lam"], 0)],
                       row_outs=[(F32,), (F32,)])
    a_s = _to_scan(a, nb, tp)
    hs_s = _scan_fwd(a_s, _to_scan(uu, nb, tp))
    hs = _from_scan(hs_s, nb, tp)

    def post(rows, prevs, nexts, chans):
        gate, hv = rows
        return [_gelu(gate) * hv], []

    (yb,) = _chan_call("ev_post", post, m, tp, c, 1, row_ins=[(u, 4), (hs, 0)], row_outs=[(BF16,)])
    out = _matmul(ya, p["w_out_a"], "nn", F32, residual=x, name="ev_out_a")
    out = _matmul(yb, p["w_out_b"], "nn", F32, residual=out, name="ev_out_b")
    return out, (x, h, u, ya, xc, zr, zi, a_s, hs_s, hs, yb)


def _even_bwd(dout, p, saved, m, tp, nb):
    c = LRU_WIDTH
    x, h, u, ya, xc, zr, zi, a_s, hs_s, hs, yb = saved
    dy = _matmul(dout, p["w_out"], "nt", F32, name="ev_out_dx")
    d_w_out = jnp.concatenate([_matmul(ya, dout, "tn", F32, name="ev_out_dw_a"),
                               _matmul(yb, dout, "tn", F32, name="ev_out_dw_b")], axis=0)

    def post_bwd(rows, prevs, nexts, chans):
        dyb, gate, hv = rows
        return [dyb * hv * _gelu_grad(gate), dyb * _gelu(gate)], []

    dgate, dhs = _chan_call("ev_post_bwd", post_bwd, m, tp, c, 1, row_ins=[(dy, 1), (u, 4), (hs, 0)],
                            row_outs=[(F32,), (F32,)])
    h_prev = jnp.concatenate([jnp.zeros_like(hs_s[:1]), hs_s[:-1]], axis=0)
    du_s, da_s = _scan_bwd(_to_scan(dhs, nb, tp), a_s, h_prev)
    du = _from_scan(du_s, nb, tp)
    da = _from_scan(da_s, nb, tp)

    def lru_in_bwd(rows, prevs, nexts, chans):
        duv, dav, xcv, zrv, ziv = rows
        r, ig, sp, a, mult = _lru_gates(xcv, zrv, ziv, *chans)
        dxc = duv * mult * ig
        dig = duv * mult * xcv
        dmult = duv * ig * xcv
        dlog_a = dav * a - dmult * (a * a) / jnp.maximum(mult, 1e-30)
        dr = dlog_a * (-LRU_C * sp)
        dzr = dr * r * (1.0 - r)
        dzi = dig * ig * (1.0 - ig)
        dsp = jnp.sum(dlog_a * (-LRU_C * r), axis=0, keepdims=True)
        dlam = -dsp * _sigmoid(-chans[2])
        return ([dzr, dzi, dxc],
                [jnp.sum(dzr, axis=0, keepdims=True), jnp.sum(dzi, axis=0, keepdims=True), dlam])

    dzr, dzi, dxc, d_r_b, d_i_b, d_lam = _chan_call(
        "ev_lru_in_bwd", lru_in_bwd, m, tp, c, 1, row_ins=[(du, 0), (da, 0), (xc, 0), (zr, 0), (zi, 0)],
        chan_ins=[(p["gate_r_b"], 0), (p["gate_i_b"], 0), (p["lam"], 0)],
        row_outs=[(F32,), (F32,), (F32,)], red_outs=[(1,), (1,), (1,)])
    d_gate_r = _matmul(xc, dzr, "tn", F32, name="ev_gate_r_dw")
    d_gate_i = _matmul(xc, dzi, "tn", F32, name="ev_gate_i_dw")
    dxc = _matmul(dzr, p["gate_r"], "nt", F32, residual=dxc, name="ev_gate_r_dx")
    dxc = _matmul(dzi, p["gate_i"], "nt", F32, residual=dxc, name="ev_gate_i_dx")

    def conv_b_bwd(rows, prevs, nexts, chans):
        dxcv, xb = rows
        return ([_conv_dx(dxcv, nexts[0], chans[0])],
                [_conv_dw(dxcv, xb, prevs[0], 4), jnp.sum(dxcv, axis=0, keepdims=True)])

    dxb, d_conv_b, d_bias = _chan_call(
        "ev_conv_b_bwd", conv_b_bwd, m, tp, c, 1, row_ins=[(dxc, 0), (u, 3)], prev_ins=[(u, 3)], next_ins=[(dxc, 0)],
        chan_ins=[(p["conv_b"], 0)], row_outs=[(F32,)], red_outs=[(SUBLANES,), (1,)])

    def mix_a_bwd(rows, prevs, nexts, chans):
        dya, gb, gc, xa = rows
        (wa,) = chans
        taps = _taps(gc * xa, prevs[0] * prevs[1], 3)
        ca = _conv_taps(taps, wa)
        dca = dya * gb
        dpa = _conv_dx(dca, nexts[0] * nexts[1], wa)
        return [dya * ca, dpa * xa, dpa * gc], [_conv_dw_taps(dca, taps)]

    dgb, dgc, dxa, d_conv_a = _chan_call(
        "ev_mix_a_bwd", mix_a_bwd, m, tp, c, 1, row_ins=[(dy, 0), (u, 0), (u, 1), (u, 2)],
        prev_ins=[(u, 1), (u, 2)], next_ins=[(dy, 0), (u, 0)], chan_ins=[(p["conv_a"], 0)],
        row_outs=[(F32,), (F32,), (F32,)], red_outs=[(SUBLANES,)])
    du_all = jnp.concatenate([dgb, dgc, dxa, dxb, dgate], axis=1)
    d_w_in = _matmul(h, du_all, "tn", F32, name="ev_in_dw")
    dh = _matmul(du_all, p["w_in"], "nt", F32, name="ev_in_dx")
    dx, d_norm = _rms_bwd(x, p["norm"], dh, dout, "ev_norm_bwd")
    return dx, dict(norm=d_norm, w_in=d_w_in, conv_a=d_conv_a[:3], conv_b=d_conv_b[:4], conv_b_bias=d_bias,
                    gate_r=d_gate_r, gate_r_b=d_r_b, gate_i=d_gate_i, gate_i_b=d_i_b, lam=d_lam, w_out=d_w_out)


def _rope_tables(tp):
    pos = jnp.arange(tp, dtype=F32)
    inv_freq = ROPE_BASE ** (-jnp.arange(0, QK_ROPE, 2, dtype=F32) / QK_ROPE)
    ang = pos[:, None] * inv_freq[None, :]
    cos, sin = jnp.cos(ang), jnp.sin(ang)
    half = QK_ROPE // 2
    one = jnp.ones((tp, QK_NOPE), F32)
    z64 = jnp.zeros((tp, QK_NOPE), F32)
    zh = jnp.zeros((tp, half), F32)
    zt = jnp.zeros((tp, HEAD_PAD - QK_HEAD), F32)
    c_tab = jnp.concatenate([one, cos, cos, zt], axis=1)
    s_lo = jnp.concatenate([z64, -sin, zh, zt], axis=1)
    s_hi = jnp.concatenate([z64, zh, sin, zt], axis=1)
    return c_tab, s_lo, s_hi


def _rope(v, c_tab, s_lo, s_hi):
    half = QK_ROPE // 2
    return v * c_tab + pltpu.roll(v, HEAD_PAD - half, 1) * s_lo + pltpu.roll(v, half, 1) * s_hi


def _rope_t(dv, c_tab, s_lo, s_hi):
    half = QK_ROPE // 2
    return dv * c_tab + pltpu.roll(dv * s_lo, half, 1) + pltpu.roll(dv * s_hi, HEAD_PAD - half, 1)


def _rope_call(name, fn, m, tp, ins, tables, out_dtype, shared_pre=None):
    tm = _pick(tp, ROW_TILE)
    tps = tp // tm
    n = len(ins)
    width = MLA_HEADS * HEAD_PAD

    def body(*refs):
        tabs = [r[...] for r in refs[n:n + 3]]
        shared = [None if fc is None else shared_pre(refs[a][...].astype(F32), *tabs) for a, (_, fc) in enumerate(ins)]
        for hh in range(MLA_HEADS):
            lanes = slice(hh * HEAD_PAD, (hh + 1) * HEAD_PAD)
            vals = [refs[a][:, lanes].astype(F32) if shared[a] is None else shared[a] for a in range(n)]
            refs[n + 3][:, lanes] = fn(*vals, *tabs).astype(out_dtype)

    in_specs, args = [], []
    for arr, fixed_col in ins:
        if fixed_col is None:
            in_specs.append(pl.BlockSpec((tm, width), lambda i: (i, 0)))
        else:
            in_specs.append(pl.BlockSpec((tm, HEAD_PAD), lambda i, fc=fixed_col: (i, fc)))
        args.append(arr)
    for tab in tables:
        in_specs.append(pl.BlockSpec((tm, HEAD_PAD), lambda i: (lax.rem(i, tps), 0)))
        args.append(tab)
    return pl.pallas_call(
        body, out_shape=jax.ShapeDtypeStruct((m, width), out_dtype), grid=(m // tm,),
        in_specs=in_specs, out_specs=pl.BlockSpec((tm, width), lambda i: (i, 0)),
        compiler_params=_cparams(1), name=name)(*args)


def _rope_k_bwd(dk, tables, m, tp):
    tm = _pick(tp, ROW_TILE)
    tps = tp // tm

    def body(dk_ref, c_ref, lo_ref, hi_ref, o_ref):
        acc = dk_ref[:, 0:HEAD_PAD].astype(F32)
        for hh in range(1, MLA_HEADS):
            acc = acc + dk_ref[:, hh * HEAD_PAD:(hh + 1) * HEAD_PAD].astype(F32)
        d = pltpu.roll(_rope_t(acc, c_ref[...], lo_ref[...], hi_ref[...]), QK_NOPE, 1)
        lane = lax.broadcasted_iota(jnp.int32, d.shape, 1)
        o_ref[...] = jnp.where(lane < QK_ROPE, d, 0.0)

    tab = pl.BlockSpec((tm, HEAD_PAD), lambda i: (lax.rem(i, tps), 0))
    return pl.pallas_call(
        body, out_shape=jax.ShapeDtypeStruct((m, HEAD_PAD), F32), grid=(m // tm,),
        in_specs=[pl.BlockSpec((tm, MLA_HEADS * HEAD_PAD), lambda i: (i, 0)), tab, tab, tab],
        out_specs=pl.BlockSpec((tm, HEAD_PAD), lambda i: (i, 0)), compiler_params=_cparams(1),
        name="od_rope_k_bwd")(dk, *tables)


def _causal_mask(row0, col0, shape):
    rows = row0 + lax.broadcasted_iota(jnp.int32, shape, 0)
    cols = col0 + lax.broadcasted_iota(jnp.int32, shape, 1)
    return cols <= rows


NT = (((1,), (1,)), ((), ()))
TN = (((0,), (0,)), ((), ()))
HEADS_PER_STEP = 2
HEAD_STEPS = MLA_HEADS // HEADS_PER_STEP
STEP_LANES = HEADS_PER_STEP * HEAD_PAD


def _flash_fwd(q, k, v, nb, tp):
    tq = _pick(tp, ROW_TILE)
    nq = tp // tq

    def body(q_ref, k_ref, v_ref, o_ref, lse_ref):
        i = pl.program_id(2)
        qbs = [q_ref[:, hd * HEAD_PAD:(hd + 1) * HEAD_PAD] for hd in range(HEADS_PER_STEP)]

        def chunk(j, carry, masked, width=1):
            off = pl.multiple_of(j * tq, tq)
            out = []
            for hd in range(HEADS_PER_STEP):
                mx, l, acc = carry[hd]
                lanes = slice(hd * HEAD_PAD, (hd + 1) * HEAD_PAD)
                kb = k_ref[pl.ds(off, width * tq), lanes]
                vb = v_ref[pl.ds(off, width * tq), lanes]
                s = lax.dot_general(qbs[hd], kb, NT, preferred_element_type=F32)
                if masked:
                    s = jnp.where(_causal_mask(0, 0, s.shape), s, NEG)
                m_new = jnp.maximum(mx, jnp.max(s, axis=1, keepdims=True))
                alpha = jnp.exp(mx - m_new)
                pr = jnp.exp(s - m_new)
                l = alpha * l + jnp.sum(pr, axis=1, keepdims=True)
                acc = alpha * acc + jnp.dot(pr.astype(BF16), vb, preferred_element_type=F32)
                out.append((m_new, l, acc))
            return tuple(out)

        one = (jnp.full((tq, 1), NEG, F32), jnp.zeros((tq, 1), F32), jnp.zeros((tq, HEAD_PAD), F32))
        carry = lax.fori_loop(0, i // 2, lambda jj, c: chunk(2 * jj, c, False, 2), (one,) * HEADS_PER_STEP)
        carry = lax.fori_loop(0, lax.rem(i, 2), lambda _, c: chunk(i - 1, c, False), carry)
        carry = chunk(i, carry, True)
        for hd in range(HEADS_PER_STEP):
            mx, l, acc = carry[hd]
            lanes = slice(hd * HEAD_PAD, (hd + 1) * HEAD_PAD)
            o_ref[:, lanes] = (acc / l).astype(o_ref.dtype)
            lse_ref[:, lanes] = jnp.broadcast_to(mx + jnp.log(l), (tq, HEAD_PAD))

    qspec = pl.BlockSpec((tq, STEP_LANES), lambda b, hh, i: (b * nq + i, hh))
    kvspec = pl.BlockSpec((tp, STEP_LANES), lambda b, hh, i: (b, hh))
    shp = (nb * tp, MLA_HEADS * HEAD_PAD)
    return pl.pallas_call(
        body, out_shape=(jax.ShapeDtypeStruct(shp, BF16), jax.ShapeDtypeStruct(shp, F32)),
        grid=(nb, HEAD_STEPS, nq), in_specs=[qspec, kvspec, kvspec], out_specs=(qspec, qspec),
        compiler_params=_cparams(3), name="od_flash_fwd")(q, k, v)


def _flash_prep(o, do, lse_c, nb, tp):
    tq = _pick(tp, ROW_TILE)
    nq = tp // tq

    def body(o_ref, do_ref, lse_ref, lr_ref, dr_ref):
        for hh in range(MLA_HEADS):
            lanes = slice(hh * HEAD_PAD, (hh + 1) * HEAD_PAD)
            delta = jnp.sum(o_ref[:, lanes].astype(F32) * do_ref[:, lanes].astype(F32), axis=1, keepdims=True)
            lr_ref[hh] = jnp.transpose(lse_ref[:, lanes])[0:SUBLANES, :]
            dr_ref[hh] = jnp.transpose(jnp.broadcast_to(delta, (tq, HEAD_PAD)))[0:SUBLANES, :]

    qspec = pl.BlockSpec((tq, MLA_HEADS * HEAD_PAD), lambda b, i: (b * nq + i, 0))
    rspec = pl.BlockSpec((MLA_HEADS, None, SUBLANES, tq), lambda b, i: (b, i, 0, 0))
    rshape = jax.ShapeDtypeStruct((nb * MLA_HEADS, nq, SUBLANES, tq), F32)
    return pl.pallas_call(
        body, out_shape=(rshape, rshape), grid=(nb, nq), in_specs=[qspec, qspec, qspec],
        out_specs=(rspec, rspec), compiler_params=_cparams(2), name="od_flash_prep")(o, do, lse_c)


def _flash_bwd(q, k, v, do, lse_r, delta_r, nb, tp):
    tq = _pick(tp, ROW_TILE)
    nq = tp // tq

    def body(q_ref, k_ref, v_ref, do_ref, lse_ref, dl_ref, dq_ref, dk_ref, dv_ref):
        j = pl.program_id(2)

        @pl.when(j == 0)
        def _():
            dq_ref[...] = jnp.zeros_like(dq_ref)

        kbs = [k_ref[:, hd * HEAD_PAD:(hd + 1) * HEAD_PAD] for hd in range(HEADS_PER_STEP)]
        vbs = [v_ref[:, hd * HEAD_PAD:(hd + 1) * HEAD_PAD] for hd in range(HEADS_PER_STEP)]

        def chunk(i, carry, masked):
            off = pl.multiple_of(i * tq, tq)
            out = []
            for hd in range(HEADS_PER_STEP):
                dk, dv = carry[hd]
                lanes = slice(hd * HEAD_PAD, (hd + 1) * HEAD_PAD)
                qb = q_ref[pl.ds(off, tq), lanes]
                dob = do_ref[pl.ds(off, tq), lanes]
                lse = lse_ref[hd, i][0:1, :]
                delta = dl_ref[hd, i][0:1, :]
                st = lax.dot_general(kbs[hd], qb, NT, preferred_element_type=F32)
                pt = jnp.exp(st - lse)
                if masked:
                    keys = lax.broadcasted_iota(jnp.int32, st.shape, 0)
                    queries = lax.broadcasted_iota(jnp.int32, st.shape, 1)
                    pt = jnp.where(keys <= queries, pt, 0.0)
                dv = dv + jnp.dot(pt.astype(BF16), dob, preferred_element_type=F32)
                dpt = lax.dot_general(vbs[hd], dob, NT, preferred_element_type=F32)
                dst = (pt * (dpt - delta)).astype(BF16)
                dk = dk + jnp.dot(dst, qb, preferred_element_type=F32)
                dq_ref[pl.ds(off, tq), lanes] += lax.dot_general(dst, kbs[hd], TN, preferred_element_type=F32)
                out.append((dk, dv))
            return tuple(out)

        zero = jnp.zeros((tq, HEAD_PAD), F32)
        carry = chunk(j, ((zero, zero),) * HEADS_PER_STEP, True)
        carry = lax.fori_loop(j + 1, nq, lambda i, c: chunk(i, c, False), carry)
        for hd in range(HEADS_PER_STEP):
            lanes = slice(hd * HEAD_PAD, (hd + 1) * HEAD_PAD)
            dk_ref[:, lanes] = carry[hd][0]
            dv_ref[:, lanes] = carry[hd][1].astype(dv_ref.dtype)

    tspec = pl.BlockSpec((tq, STEP_LANES), lambda b, hh, j: (b * nq + j, hh))
    fullspec = pl.BlockSpec((tp, STEP_LANES), lambda b, hh, j: (b, hh))
    rspec = pl.BlockSpec((HEADS_PER_STEP, nq, SUBLANES, tq), lambda b, hh, j: (b * HEAD_STEPS + hh, 0, 0, 0))
    shp = (nb * tp, MLA_HEADS * HEAD_PAD)
    return pl.pallas_call(
        body, out_shape=(jax.ShapeDtypeStruct(shp, F32), jax.ShapeDtypeStruct(shp, F32),
                         jax.ShapeDtypeStruct(shp, BF16)),
        grid=(nb, HEAD_STEPS, nq), in_specs=[fullspec, tspec, tspec, fullspec, rspec, rspec],
        out_specs=(fullspec, tspec, tspec), compiler_params=_cparams(3),
        name="od_flash_bwd")(q, k, v, do, lse_r, delta_r)


def _odd_fwd(x, p, tables, m, tp, nb):
    scale = QK_HEAD ** -0.5
    h = _rms_fwd(x, p["norm"], "od_norm")
    u = _matmul(h, p["w_in"], "nn", F32, name="od_in")
    cq = u[:, :Q_LORA]
    ckv = u[:, Q_LORA:Q_LORA + KV_LORA]
    cqn = _rms_fwd(cq, p["q_norm"], "od_q_norm")
    ckvn = _rms_fwd(ckv, p["kv_norm"], "od_kv_norm")
    q_raw = _matmul(cqn, p["w_uq"], "nn", F32, name="od_uq")
    k_raw = _matmul(ckvn, p["w_uk"], "nn", F32, name="od_uk")
    v = _matmul(ckvn, p["w_uv"], "nn", BF16, name="od_uv")
    q = _rope_call("od_rope_q", lambda qv, c, lo, hi: _rope(qv, c, lo, hi) * scale, m, tp, [(q_raw, None)], tables,
                   BF16)
    kr_col = (Q_LORA + KV_LORA) // HEAD_PAD
    k = _rope_call("od_rope_k", lambda kv, kr, c, lo, hi: kv + kr, m, tp, [(k_raw, None), (u, kr_col)], tables, BF16,
                   shared_pre=lambda uv, c, lo, hi: _rope(pltpu.roll(uv, QK_NOPE, 1), c, lo, hi))
    o, lse_c = _flash_fwd(q, k, v, nb, tp)
    out = _matmul(o, p["w_out"], "nn", F32, residual=x, name="od_out")
    return out, (x, h, cq, ckv, cqn, ckvn, q, k, v, o, lse_c)


def _odd_bwd(dout, p, tables, saved, m, tp, nb):
    scale = QK_HEAD ** -0.5
    x, h, cq, ckv, cqn, ckvn, q, k, v, o, lse_c = saved
    do = _matmul(dout, p["w_out"], "nt", BF16, name="od_out_dx")
    d_w_out = _matmul(o, dout, "tn", F32, name="od_out_dw")
    lse_r, delta_r = _flash_prep(o, do, lse_c, nb, tp)
    dq, dk, dv = _flash_bwd(q, k, v, do, lse_r, delta_r, nb, tp)
    dq_raw = _rope_call("od_rope_q_bwd", lambda d, c, lo, hi: _rope_t(d, c, lo, hi) * scale, m, tp, [(dq, None)],
                        tables, BF16)
    dkr = _rope_k_bwd(dk, tables, m, tp)
    d_w_uq = _matmul(cqn, dq_raw, "tn", F32, name="od_uq_dw")
    d_w_uk = _matmul(ckvn, dk, "tn", F32, name="od_uk_dw")
    d_w_uv = _matmul(ckvn, dv, "tn", F32, name="od_uv_dw")
    dcqn = _matmul(dq_raw, p["w_uq"], "nt", F32, name="od_uq_dx")
    dckvn = _matmul(dk, p["w_uk"], "nt", F32, name="od_uk_dx")
    dckvn = _matmul(dv, p["w_uv"], "nt", F32, residual=dckvn, name="od_uv_dx")
    dcq, d_q_norm = _rms_bwd(cq, p["q_norm"], dcqn, None, "od_q_norm_bwd")
    dckv, d_kv_norm = _rms_bwd(ckv, p["kv_norm"], dckvn, None, "od_kv_norm_bwd")
    du = jnp.concatenate([dcq, dckv, dkr], axis=1)
    d_w_in = _matmul(h, du, "tn", F32, name="od_in_dw")
    dh = _matmul(du, p["w_in"], "nt", F32, name="od_in_dx")
    dx, d_norm = _rms_bwd(x, p["norm"], dh, dout, "od_norm_bwd")
    return dx, dict(norm=d_norm, w_in=d_w_in, q_norm=d_q_norm, kv_norm=d_kv_norm, w_uq=d_w_uq, w_uk=d_w_uk,
                    w_uv=d_w_uv, w_out=d_w_out)


def _loss_head(hf, g, target, tp, t_real):
    m, c = hf.shape
    tm = _pick(tp, ROW_TILE)
    tps = tp // tm

    def body(x_ref, g_ref, t_ref, dx_ref, dg_ref, loss_ref):
        i = pl.program_id(0)
        xf = x_ref[...]
        r = lax.rsqrt(jnp.mean(xf * xf, axis=-1, keepdims=True) + EPS)
        xn = xf * r
        t_pos = lax.rem(i, tps) * tm + lax.broadcasted_iota(jnp.int32, (tm, 1), 0)
        valid = jnp.logical_and(t_pos >= N_META, t_pos < t_real)
        err = jnp.where(valid, xn * g_ref[...] - t_ref[...], 0.0)
        dyf = err * (1.0 / c)
        dyg = dyf * g_ref[...]
        dx_ref[...] = r * (dyg - xn * jnp.mean(dyg * xn, axis=-1, keepdims=True))

        @pl.when(i == 0)
        def _():
            dg_ref[...] = jnp.zeros_like(dg_ref)
            loss_ref[...] = jnp.zeros_like(loss_ref)

        dg_ref[...] += jnp.sum(dyf * xn, axis=0, keepdims=True)
        loss_ref[...] += (0.5 / c) * jnp.sum(jnp.sum(err * err, axis=1, keepdims=True), axis=0, keepdims=True)

    row = pl.BlockSpec((tm, c), lambda i: (i, 0))
    vec = pl.BlockSpec((1, c), lambda i: (0, 0))
    return pl.pallas_call(
        body, out_shape=(jax.ShapeDtypeStruct((m, c), F32), jax.ShapeDtypeStruct((1, c), F32),
                         jax.ShapeDtypeStruct((1, 1), F32)),
        grid=(m // tm,), in_specs=[row, vec, row], out_specs=(row, vec, pl.BlockSpec((1, 1), lambda i: (0, 0))),
        compiler_params=_cparams(1), name="loss_head")(hf, g, target)


def _meta_grad(dh0, nb, tp):
    d = dh0.shape[1]

    def body(x_ref, o_ref):
        @pl.when(pl.program_id(0) == 0)
        def _():
            o_ref[...] = jnp.zeros_like(o_ref)

        o_ref[...] += x_ref[...]

    return pl.pallas_call(
        body, out_shape=jax.ShapeDtypeStruct((N_META, d), F32), grid=(nb,),
        in_specs=[pl.BlockSpec((N_META, d), lambda b: (b * (tp // N_META), 0))],
        out_specs=pl.BlockSpec((N_META, d), lambda b: (0, 0)), compiler_params=_cparams(1), name="meta_grad")(dh0)


def _mesh_pos():
    x, y, c = lax.axis_index("x"), lax.axis_index("y"), lax.axis_index("c")
    return x, y, c


N_CHIP = 4
MESH_ID = pl.DeviceIdType.MESH


def _peer_chip(x, y, k):
    px = 1 - x if k & 2 else x
    py = 1 - y if k & 1 else y
    return px, py


def _all_gather(arrays):
    n = len(arrays)

    def body(*refs):
        srcs, outs = refs[:n], refs[n:2 * n]
        send_sems, recv_sems, local_sems = refs[2 * n:]
        x, y, c = _mesh_pos()
        me = 4 * x + 2 * y + c
        sibling = (x, y, 1 - c)

        def copy(a, sem, src, block, to):
            return pltpu.make_async_remote_copy(
                src_ref=src, dst_ref=outs[a].at[block], send_sem=send_sems.at[a, sem], recv_sem=recv_sems.at[a, sem],
                device_id=to, device_id_type=MESH_ID)

        local = [pltpu.make_async_copy(srcs[a], outs[a].at[me], local_sems.at[a]) for a in range(n)]
        for cp in local:
            cp.start()
        sends = [copy(a, 0, srcs[a], me, sibling) for a in range(n)]
        for k in range(1, N_CHIP):
            px, py = _peer_chip(x, y, k)
            sends += [copy(a, k, srcs[a], me, (px, py, c)) for a in range(n)]
        for cp in sends:
            cp.start()
        for k in range(1, N_CHIP):
            px, py = _peer_chip(x, y, k)
            block = 4 * px + 2 * py + c
            for a in range(n):
                copy(a, k, srcs[a], block, sibling).wait_recv()
            passed = [copy(a, N_CHIP - 1 + k, outs[a].at[block], block, sibling) for a in range(n)]
            for cp in passed:
                cp.start()
            sends += passed
        for a in range(n):
            copy(a, 0, srcs[a], 4 * x + 2 * y + (1 - c), sibling).wait_recv()
        for k in range(1, N_CHIP):
            px, py = _peer_chip(x, y, k)
            for a in range(n):
                copy(a, N_CHIP - 1 + k, srcs[a], 4 * px + 2 * py + (1 - c), sibling).wait_recv()
        for cp in sends:
            cp.wait_send()
        for cp in local:
            cp.wait()

    any_spec = pl.BlockSpec(memory_space=pl.ANY)
    out_shape = tuple(jax.ShapeDtypeStruct((N_DEV,) + a.shape, a.dtype) for a in arrays)
    return pl.pallas_call(
        body, out_shape=out_shape, in_specs=[any_spec] * n, out_specs=(any_spec,) * n,
        scratch_shapes=[pltpu.SemaphoreType.DMA((n, N_DEV - 1)), pltpu.SemaphoreType.DMA((n, N_DEV - 1)),
                        pltpu.SemaphoreType.DMA((n,))],
        name="weight_all_gather")(*arrays)


def _pair_exchange(arrays):
    n = len(arrays)

    def body(*refs):
        srcs, outs = refs[:n], refs[n:2 * n]
        send_sems, recv_sems = refs[2 * n:]
        x, y, c = _mesh_pos()
        copies = [pltpu.make_async_remote_copy(
            src_ref=srcs[a], dst_ref=outs[a], send_sem=send_sems.at[a], recv_sem=recv_sems.at[a],
            device_id=(x, y, 1 - c), device_id_type=MESH_ID) for a in range(n)]
        for cp in copies:
            cp.start()
        for cp in copies:
            cp.wait()

    any_spec = pl.BlockSpec(memory_space=pl.ANY)
    return pl.pallas_call(
        body, out_shape=tuple(jax.ShapeDtypeStruct(a.shape, a.dtype) for a in arrays), in_specs=[any_spec] * n,
        out_specs=(any_spec,) * n, scratch_shapes=[pltpu.SemaphoreType.DMA((n,)), pltpu.SemaphoreType.DMA((n,))],
        name="grad_pair_exchange")(*arrays)


def _chip_exchange(arrays):
    n = len(arrays)

    def body(*refs):
        srcs, outs = refs[:n], refs[n:2 * n]
        send_sems, recv_sems, local_sems = refs[2 * n:]
        x, y, c = _mesh_pos()
        q = 2 * x + y
        local = [pltpu.make_async_copy(srcs[a].at[q], outs[a].at[q], local_sems.at[a]) for a in range(n)]
        for cp in local:
            cp.start()

        def copy(a, k, to_q, from_q, px, py):
            return pltpu.make_async_remote_copy(
                src_ref=srcs[a].at[to_q], dst_ref=outs[a].at[from_q], send_sem=send_sems.at[a, k - 1],
                recv_sem=recv_sems.at[a, k - 1], device_id=(px, py, c), device_id_type=MESH_ID)

        sends = []
        for k in range(1, N_CHIP):
            px, py = _peer_chip(x, y, k)
            sends += [copy(a, k, 2 * px + py, q, px, py) for a in range(n)]
        for cp in sends:
            cp.start()
        for k in range(1, N_CHIP):
            px, py = _peer_chip(x, y, k)
            for a in range(n):
                copy(a, k, q, 2 * px + py, px, py).wait_recv()
        for cp in sends:
            cp.wait_send()
        for cp in local:
            cp.wait()

    any_spec = pl.BlockSpec(memory_space=pl.ANY)
    return pl.pallas_call(
        body, out_shape=tuple(jax.ShapeDtypeStruct(a.shape, a.dtype) for a in arrays), in_specs=[any_spec] * n,
        out_specs=(any_spec,) * n,
        scratch_shapes=[pltpu.SemaphoreType.DMA((n, N_CHIP - 1)), pltpu.SemaphoreType.DMA((n, N_CHIP - 1)),
                        pltpu.SemaphoreType.DMA((n,))],
        name="grad_chip_exchange")(*arrays)


REDUCE_BLOCK_BYTES = 512 * 1024


def _pair_add(a, b):
    p, r, c = a.shape
    tr = _reduce_rows(r, c)

    def body(a_ref, b_ref, o_ref):
        o_ref[...] = (a_ref[...].astype(F32) + b_ref[...].astype(F32)).astype(o_ref.dtype)

    blk = pl.BlockSpec((None, tr, c), lambda s, i: (s, i, 0))
    return pl.pallas_call(
        body, out_shape=jax.ShapeDtypeStruct(a.shape, a.dtype), grid=(p, r // tr), in_specs=[blk, blk], out_specs=blk,
        compiler_params=_cparams(2), name="grad_pair_add")(a, b)


def _reduce_rows(r, c):
    best = None
    for t in range(16, r + 1, 16):
        if r % t == 0 and t * c * 4 <= REDUCE_BLOCK_BYTES:
            best = t
    assert best is not None, (r, c)
    return best


def _reduce_adamw(parts, w, mom, vel):
    n_parts, r, c = parts.shape
    tr = _reduce_rows(r, c)
    c1 = 1.0 - ADAM_B1 ** ADAM_STEP
    c2 = 1.0 - ADAM_B2 ** ADAM_STEP

    def body(p_ref, w_ref, m_ref, v_ref, g_ref, d_ref, mo_ref, vo_ref):
        g = p_ref[0].astype(F32)
        for s in range(1, n_parts):
            g = g + p_ref[s].astype(F32)
        mn = ADAM_B1 * m_ref[...] + (1.0 - ADAM_B1) * g
        vn = ADAM_B2 * v_ref[...] + (1.0 - ADAM_B2) * (g * g)
        m_hat = mn / c1
        v_hat = vn / c2
        g_ref[...] = g
        d_ref[...] = -ADAM_LR * (m_hat / (jnp.sqrt(v_hat) + ADAM_EPS) + ADAM_WD * w_ref[...])
        mo_ref[...] = mn
        vo_ref[...] = vn

    blk = pl.BlockSpec((tr, c), lambda i: (i, 0))
    shp = jax.ShapeDtypeStruct((r, c), F32)
    return pl.pallas_call(
        body, out_shape=(shp, shp, shp, shp), grid=(r // tr,),
        in_specs=[pl.BlockSpec((n_parts, tr, c), lambda i: (0, i, 0)), blk, blk, blk], out_specs=(blk, blk, blk, blk),
        compiler_params=_cparams(1), name="reduce_adamw")(parts, w, mom, vel)


def _pack_rows(pieces, width, row_multiple, dtype):
    flat = jnp.concatenate([p.astype(dtype).reshape(-1) for p in pieces])
    rows = -(-flat.shape[0] // (width * row_multiple)) * row_multiple
    return jnp.pad(flat, (0, rows * width - flat.shape[0])).reshape(rows, width)


def _unshard(gathered, axis):
    moved = jnp.moveaxis(gathered, 0, axis)
    shape = list(moved.shape)
    shape[axis:axis + 2] = [shape[axis] * shape[axis + 1]]
    return moved.reshape(shape)


def _to_slots(full, axis):
    shape = list(full.shape)
    shape[axis:axis + 1] = [N_DEV, shape[axis] // N_DEV]
    return jnp.moveaxis(full.reshape(shape), axis, 0)


def _block_diag(w):
    hh, d, _ = w.shape
    eye = jnp.eye(hh, dtype=w.dtype)
    return (w[:, :, None, :] * eye[:, None, :, None]).reshape(hh * d, hh * d)


def _block_diag_t(full, hh):
    d = full.shape[0] // hh
    f4 = full.reshape(hh, d, hh, d)
    return jnp.stack([f4[i, :, i, :] for i in range(hh)], axis=0)


def _pad_heads(w, width):
    r = w.shape[0]
    w3 = w.reshape(r, MLA_HEADS, width)
    return jnp.pad(w3, ((0, 0), (0, 0), (0, HEAD_PAD - width))).reshape(r, MLA_HEADS * HEAD_PAD)


def _unpad_heads(w, width):
    r = w.shape[0]
    return w.reshape(r, MLA_HEADS, HEAD_PAD)[:, :, :width].reshape(r, MLA_HEADS * width)


def kernel(x, meta_tokens, ev_norm, ev_w_in, ev_conv_a, ev_conv_b, ev_conv_b_bias, ev_gate_r_w, ev_gate_r_b, ev_gate_i_w, ev_gate_i_b, ev_lru_lambda, ev_w_out, od_norm, od_w_in, od_q_norm, od_kv_norm, od_w_uq, od_w_ukv, od_w_out, ffn_norm, ffn_w_up, ffn_conv_w, ffn_conv_b, ffn_w_down, final_norm, loss_target, m_meta_tokens, m_ev_norm, m_ev_w_in, m_ev_conv_a, m_ev_conv_b, m_ev_conv_b_bias, m_ev_gate_r_w, m_ev_gate_r_b, m_ev_gate_i_w, m_ev_gate_i_b, m_ev_lru_lambda, m_ev_w_out, m_od_norm, m_od_w_in, m_od_q_norm, m_od_kv_norm, m_od_w_uq, m_od_w_ukv, m_od_w_out, m_ffn_norm, m_ffn_w_up, m_ffn_conv_w, m_ffn_conv_b, m_ffn_w_down, m_final_norm, v_meta_tokens, v_ev_norm, v_ev_w_in, v_ev_conv_a, v_ev_conv_b, v_ev_conv_b_bias, v_ev_gate_r_w, v_ev_gate_r_b, v_ev_gate_i_w, v_ev_gate_i_b, v_ev_lru_lambda, v_ev_w_out, v_od_norm, v_od_w_in, v_od_q_norm, v_od_kv_norm, v_od_w_uq, v_od_w_ukv, v_od_w_out, v_ffn_norm, v_ffn_w_up, v_ffn_conv_w, v_ffn_conv_b, v_ffn_w_down, v_final_norm):
    given = dict(locals())
    names = [n for n, _ in PARAMS]
    axis_of = dict(PARAMS)
    w_loc = {n: given[n] for n in names}
    m_loc = {n: given["m_" + n] for n in names}
    v_loc = {n: given["v_" + n] for n in names}
    sharded = [n for n in names if axis_of[n] is not None]
    replicated = [n for n in names if axis_of[n] is None]
    small = [n for n in sharded if n not in BIG]

    nb, seq, d = x.shape
    t_real = N_META + seq
    tp = -(-t_real // ROW_TILE) * ROW_TILE
    m = nb * tp

    small_pack = _pack_rows([w_loc[n] for n in small], LANES, SUBLANES, F32)
    gathered = _all_gather([w_loc[n].astype(BF16) for n in BIG] + [small_pack])
    full = {n: w_loc[n] for n in replicated}
    for n, g in zip(BIG, gathered[:-1]):
        full[n] = _unshard(g, axis_of[n])
    flat = gathered[-1].reshape(N_DEV, -1)
    off = 0
    for n in small:
        shard = w_loc[n].shape
        size = math.prod(shard)
        full[n] = _unshard(flat[:, off:off + size].reshape((N_DEV,) + shard), axis_of[n])
        off += size

    tables = _rope_tables(tp)

    def even_params(j):
        w_out = full["ev_w_out"][j]
        return dict(norm=full["ev_norm"][j][None], w_in=full["ev_w_in"][j], conv_a=full["ev_conv_a"][j],
                    conv_b=full["ev_conv_b"][j], conv_b_bias=full["ev_conv_b_bias"][j][None],
                    gate_r=_block_diag(full["ev_gate_r_w"][j]).astype(BF16),
                    gate_i=_block_diag(full["ev_gate_i_w"][j]).astype(BF16),
                    gate_r_b=full["ev_gate_r_b"][j][None], gate_i_b=full["ev_gate_i_b"][j][None],
                    lam=full["ev_lru_lambda"][j][None], w_out=w_out, w_out_a=w_out[:LRU_WIDTH],
                    w_out_b=w_out[LRU_WIDTH:])

    def odd_params(j):
        w_ukv = full["od_w_ukv"][j].reshape(KV_LORA, MLA_HEADS, QK_NOPE + V_HEAD)
        w_uk = w_ukv[:, :, :QK_NOPE].reshape(KV_LORA, MLA_HEADS * QK_NOPE)
        w_uv = w_ukv[:, :, QK_NOPE:].reshape(KV_LORA, MLA_HEADS * V_HEAD)
        w_out = full["od_w_out"][j].reshape(MLA_HEADS, V_HEAD, d)
        w_out = jnp.pad(w_out, ((0, 0), (0, HEAD_PAD - V_HEAD), (0, 0))).reshape(MLA_HEADS * HEAD_PAD, d)
        return dict(norm=full["od_norm"][j][None], w_in=jnp.pad(full["od_w_in"][j], ((0, 0), (0, ODD_IN_PAD - ODD_IN))),
                    q_norm=full["od_q_norm"][j][None], kv_norm=full["od_kv_norm"][j][None],
                    w_uq=_pad_heads(full["od_w_uq"][j], QK_HEAD), w_uk=_pad_heads(w_uk, QK_NOPE),
                    w_uv=_pad_heads(w_uv, V_HEAD), w_out=w_out)

    def ffn_params(layer):
        w_up = full["ffn_w_up"][layer]
        return dict(norm=full["ffn_norm"][layer][None], w_up=w_up, w_up_a=w_up[:, :D_FF], w_up_g=w_up[:, D_FF:],
                    conv_w=full["ffn_conv_w"][layer], conv_b=full["ffn_conv_b"][layer][None],
                    w_down=full["ffn_w_down"][layer])

    meta = jnp.broadcast_to(full["meta_tokens"][None], (nb, N_META, d))
    h0 = jnp.concatenate([meta, x, jnp.zeros((nb, tp - t_real, d), F32)], axis=1).reshape(m, d)
    hcur = h0
    tape = []
    for layer in range(4):
        j = layer // 2
        if layer % 2 == 0:
            mp = even_params(j)
            hcur, saved = _even_fwd(hcur, mp, m, tp, nb)
        else:
            mp = odd_params(j)
            hcur, saved = _odd_fwd(hcur, mp, tables, m, tp, nb)
        fp = ffn_params(layer)
        hcur, fsaved = _ffn_fwd(hcur, fp, m, tp)
        tape.append((mp, saved, fp, fsaved))

    target = jnp.pad(loss_target, ((0, 0), (N_META, tp - t_real), (0, 0))).reshape(m, d)
    dh, d_final_norm, loss_part = _loss_head(hcur, full["final_norm"][None], target, tp, t_real)
    loss = lax.psum(loss_part[0, 0], ("x", "y", "c"))

    grads = {"final_norm": d_final_norm[0]}
    ev_g, od_g, ffn_g = [None, None], [None, None], [None] * 4
    for layer in reversed(range(4)):
        mp, saved, fp, fsaved = tape[layer]
        dh, ffn_g[layer] = _ffn_bwd(dh, fp, fsaved, m, tp)
        if layer % 2 == 0:
            dh, ev_g[layer // 2] = _even_bwd(dh, mp, saved, m, tp, nb)
        else:
            dh, od_g[layer // 2] = _odd_bwd(dh, mp, tables, saved, m, tp, nb)

    dh3 = dh.reshape(nb, tp, d)
    grad_x = dh3[:, N_META:t_real]
    grads["meta_tokens"] = _meta_grad(dh, nb, tp)

    def stack(lst, key, fn=lambda a: a):
        return jnp.stack([fn(g[key]) for g in lst], axis=0)

    grads["ev_norm"] = stack(ev_g, "norm", lambda a: a[0])
    grads["ev_w_in"] = stack(ev_g, "w_in")
    grads["ev_conv_a"] = stack(ev_g, "conv_a")
    grads["ev_conv_b"] = stack(ev_g, "conv_b")
    grads["ev_conv_b_bias"] = stack(ev_g, "conv_b_bias", lambda a: a[0])
    grads["ev_gate_r_w"] = stack(ev_g, "gate_r", lambda a: _block_diag_t(a, 8))
    grads["ev_gate_r_b"] = stack(ev_g, "gate_r_b", lambda a: a[0])
    grads["ev_gate_i_w"] = stack(ev_g, "gate_i", lambda a: _block_diag_t(a, 8))
    grads["ev_gate_i_b"] = stack(ev_g, "gate_i_b", lambda a: a[0])
    grads["ev_lru_lambda"] = stack(ev_g, "lam", lambda a: a[0])
    grads["ev_w_out"] = stack(ev_g, "w_out")
    grads["od_norm"] = stack(od_g, "norm", lambda a: a[0])
    grads["od_w_in"] = stack(od_g, "w_in", lambda a: a[:, :ODD_IN])
    grads["od_q_norm"] = stack(od_g, "q_norm", lambda a: a[0])
    grads["od_kv_norm"] = stack(od_g, "kv_norm", lambda a: a[0])
    grads["od_w_uq"] = stack(od_g, "w_uq", lambda a: _unpad_heads(a, QK_HEAD))

    def ukv(g):
        gk = g["w_uk"].reshape(KV_LORA, MLA_HEADS, HEAD_PAD)[:, :, :QK_NOPE]
        gv = g["w_uv"].reshape(KV_LORA, MLA_HEADS, HEAD_PAD)[:, :, :V_HEAD]
        return jnp.concatenate([gk, gv], axis=2).reshape(KV_LORA, MLA_HEADS * (QK_NOPE + V_HEAD))

    grads["od_w_ukv"] = jnp.stack([ukv(g) for g in od_g], axis=0)
    grads["od_w_out"] = stack(od_g, "w_out", lambda a: a.reshape(MLA_HEADS, HEAD_PAD, d)[:, :V_HEAD].reshape(-1, d))
    grads["ffn_norm"] = stack(ffn_g, "norm", lambda a: a[0])
    grads["ffn_w_up"] = stack(ffn_g, "w_up")
    grads["ffn_conv_w"] = stack(ffn_g, "conv_w")
    grads["ffn_conv_b"] = stack(ffn_g, "conv_b", lambda a: a[0])
    grads["ffn_w_down"] = stack(ffn_g, "w_down")

    order = small + replicated
    slot_parts = [_to_slots(grads[n], axis_of[n]).reshape(N_DEV, -1) for n in small]
    slot_parts += [jnp.broadcast_to(grads[n].reshape(1, -1), (N_DEV, grads[n].size)) for n in replicated]
    g_flat = jnp.concatenate(slot_parts, axis=1)
    n_flat = g_flat.shape[1]
    rows = -(-n_flat // (1024 * 128)) * 128
    g_small = jnp.pad(g_flat, ((0, 0), (0, rows * 1024 - n_flat))).reshape(N_DEV, rows, 1024)

    def rows_of(n):
        shard = w_loc[n].shape
        return (math.prod(shard[:-1]), shard[-1])

    g_big = [_to_slots(grads[n], axis_of[n]).astype(BF16).reshape((N_DEV,) + rows_of(n)) for n in BIG]
    core = lax.axis_index("c")
    by_core = [jnp.swapaxes(g.reshape((N_CHIP, 2) + g.shape[1:]), 0, 1) for g in g_big + [g_small]]
    mine = [lax.dynamic_index_in_dim(g, core, 0, keepdims=False) for g in by_core]
    theirs = [lax.dynamic_index_in_dim(g, 1 - core, 0, keepdims=False) for g in by_core]
    from_sibling = _pair_exchange(theirs)
    parts = _chip_exchange([_pair_add(a, b) for a, b in zip(mine, from_sibling)])

    g_out, d_out, m_out, v_out = {}, {}, {}, {}
    for n, part in zip(BIG, parts[:-1]):
        res = _reduce_adamw(part, *[t[n].reshape(rows_of(n)) for t in (w_loc, m_loc, v_loc)])
        for out, r in zip((g_out, d_out, m_out, v_out), res):
            out[n] = r.reshape(w_loc[n].shape)

    def flat_local(tree):
        flat = jnp.concatenate([tree[n].reshape(-1) for n in order])
        return jnp.pad(flat, (0, rows * 1024 - n_flat)).reshape(rows, 1024)

    res = _reduce_adamw(parts[-1], flat_local(w_loc), flat_local(m_loc), flat_local(v_loc))
    for out, r in zip((g_out, d_out, m_out, v_out), res):
        flat = r.reshape(-1)
        off = 0
        for n in order:
            size = w_loc[n].size
            out[n] = flat[off:off + size].reshape(w_loc[n].shape)
            off += size
    return (loss, grad_x, *[g_out[n] for n in names], *[d_out[n] for n in names], *[m_out[n] for n in names],
            *[v_out[n] for n in names])
```

```python
import functools
import math

import jax
import jax.numpy as jnp
from jax import lax
from jax.experimental import pallas as pl
from jax.experimental.pallas import tpu as pltpu

F32 = jnp.float32
BF16 = jnp.bfloat16

N_DEV = 8
N_META = 16
EPS = 1e-6
LRU_C = 8.0
MLA_HEADS = 16
QK_NOPE = 64
QK_ROPE = 32
QK_HEAD = QK_NOPE + QK_ROPE
V_HEAD = 64
HEAD_PAD = 128
Q_LORA = 384
KV_LORA = 256
ODD_IN = Q_LORA + KV_LORA + QK_ROPE
ODD_IN_PAD = 768
ROPE_BASE = 10000.0
LRU_WIDTH = 512
D_FF = 2816

ADAM_LR = 0.001
ADAM_B1 = 0.9
ADAM_B2 = 0.999
ADAM_EPS = 1e-08
ADAM_WD = 0.01
ADAM_STEP = 10

ROW_TILE = 384
SUBLANES = 8
HALO_ROWS = 16
LANES = 128
VMEM_LIMIT = 48 * 1024 * 1024
NEG = -1e30

PARAMS = (
    ("meta_tokens", 1), ("ev_norm", None), ("ev_w_in", 2), ("ev_conv_a", 2), ("ev_conv_b", 2),
    ("ev_conv_b_bias", None), ("ev_gate_r_w", None), ("ev_gate_r_b", None), ("ev_gate_i_w", None),
    ("ev_gate_i_b", None), ("ev_lru_lambda", None), ("ev_w_out", 1), ("od_norm", 1), ("od_w_in", 1),
    ("od_q_norm", 1), ("od_kv_norm", 1), ("od_w_uq", 2), ("od_w_ukv", 2), ("od_w_out", 1),
    ("ffn_norm", None), ("ffn_w_up", 2), ("ffn_conv_w", 2), ("ffn_conv_b", None), ("ffn_w_down", 1),
    ("final_norm", None),
)
BIG = ("ev_w_in", "ev_w_out", "od_w_in", "od_w_uq", "od_w_ukv", "od_w_out", "ffn_w_up", "ffn_w_down")


def _cparams(n_grid):
    return pltpu.CompilerParams(dimension_semantics=("arbitrary",) * n_grid, vmem_limit_bytes=VMEM_LIMIT)


def _pick(dim, target):
    if dim <= target:
        return dim
    best = None
    for t in range(LANES, target + 1, LANES):
        if dim % t == 0:
            best = t
    assert best is not None, (dim, target)
    return best


MATMUL_VMEM_BUDGET = 38 * 1024 * 1024
HBM_BYTES_PER_US = 3.0e6
MXU_FLOPS_PER_US = 9.0e8
ACC_BYTES_PER_US = 7.6e6
GRID_STEP_US = 0.35


def _tile_candidates(dim):
    return [t for t in range(LANES, dim + 1, LANES) if dim % t == 0] or [dim]


def _matmul_tiles(m, n, k, sa, sb, so, sr, transposed_lhs):
    best, best_cost = None, None
    for tm in _tile_candidates(m):
        for tn in _tile_candidates(n):
            for tk in _tile_candidates(k):
                nk = k // tk
                vmem = 2 * (tm * tk * sa + tk * tn * sb) + tm * tn * ((4 if nk > 1 else 0) + 2 * so + 2 * sr)
                vmem += (tm * tk * 2 if sa > 2 else 0) + (tk * tn * 2 if sb > 2 else 0) + tm * tn * 4
                if vmem > MATMUL_VMEM_BUDGET:
                    continue
                steps = (m // tm) * (n // tn) * nk
                traffic = m * k * sa * (n // tn) + k * n * sb * (m // tm) + m * n * (so + sr)
                acc_us = steps * tm * tn * 4 / ACC_BYTES_PER_US if nk > 1 else 0.0
                busy_us = 0.0 if transposed_lhs else 2.0 * m * n * k / MXU_FLOPS_PER_US + acc_us
                cost = max(traffic / HBM_BYTES_PER_US, busy_us) + steps * GRID_STEP_US
                if best_cost is None or cost < best_cost:
                    best, best_cost = (tm, tn, tk), cost
    assert best is not None, (m, n, k)
    return best


def _matmul(a, b, mode, out_dtype=F32, residual=None, name="mm"):
    if mode == "nn":
        (m, k), (k2, n) = a.shape, b.shape
    elif mode == "nt":
        (m, k), (n, k2) = a.shape, b.shape
    else:
        (k, m), (k2, n) = a.shape, b.shape
    assert k == k2, (a.shape, b.shape, mode)
    tm, tn, tk = _matmul_tiles(m, n, k, a.dtype.itemsize, b.dtype.itemsize, jnp.dtype(out_dtype).itemsize,
                               0 if residual is None else residual.dtype.itemsize, mode == "tn")
    nk = k // tk
    if mode == "tn":
        a_spec = pl.BlockSpec((tk, tm), lambda i, j, kk: (kk, i))
        dims = (((0,), (0,)), ((), ()))
    else:
        a_spec = pl.BlockSpec((tm, tk), lambda i, j, kk: (i, kk))
        dims = (((1,), (1 if mode == "nt" else 0,)), ((), ()))
    if mode == "nt":
        b_spec = pl.BlockSpec((tn, tk), lambda i, j, kk: (j, kk))
    else:
        b_spec = pl.BlockSpec((tk, tn), lambda i, j, kk: (kk, j))
    o_spec = pl.BlockSpec((tm, tn), lambda i, j, kk: (i, j))
    has_res = residual is not None

    def body(*refs):
        a_ref, b_ref = refs[:2]
        r_ref = refs[2] if has_res else None
        o_ref = refs[3] if has_res else refs[2]
        part = lax.dot_general(a_ref[...].astype(BF16), b_ref[...].astype(BF16), dims, preferred_element_type=F32)

        def finish(out):
            if has_res:
                out = out + r_ref[...].astype(F32)
            o_ref[...] = out.astype(o_ref.dtype)

        if nk == 1:
            finish(part)
            return
        acc_ref = refs[-1]
        kk = pl.program_id(2)

        @pl.when(kk == 0)
        def _():
            acc_ref[...] = part

        @pl.when(kk > 0)
        def _():
            acc_ref[...] += part

        @pl.when(kk == nk - 1)
        def _():
            finish(acc_ref[...])

    in_specs = [a_spec, b_spec] + ([o_spec] if has_res else [])
    args = (a, b) + ((residual,) if has_res else ())
    return pl.pallas_call(
        body, out_shape=jax.ShapeDtypeStruct((m, n), out_dtype), grid=(m // tm, n // tn, nk),
        in_specs=in_specs, out_specs=o_spec, scratch_shapes=[pltpu.VMEM((tm, tn), F32)] if nk > 1 else [],
        compiler_params=_cparams(3), name=name)(*args)


def _rms_fwd(x, g, name):
    m, c = x.shape
    tm = _pick(m, ROW_TILE)

    def body(x_ref, g_ref, o_ref):
        xf = x_ref[...].astype(F32)
        r = lax.rsqrt(jnp.mean(xf * xf, axis=-1, keepdims=True) + EPS)
        o_ref[...] = (xf * r * g_ref[...]).astype(o_ref.dtype)

    return pl.pallas_call(
        body, out_shape=jax.ShapeDtypeStruct((m, c), BF16), grid=(m // tm,),
        in_specs=[pl.BlockSpec((tm, c), lambda i: (i, 0)), pl.BlockSpec((1, c), lambda i: (0, 0))],
        out_specs=pl.BlockSpec((tm, c), lambda i: (i, 0)), compiler_params=_cparams(1), name=name)(x, g)


def _rms_bwd(x, g, dy, residual, name):
    m, c = x.shape
    tm = _pick(m, ROW_TILE)
    has_res = residual is not None

    def body(*refs):
        if has_res:
            x_ref, g_ref, dy_ref, r_ref, dx_ref, dg_ref = refs
        else:
            x_ref, g_ref, dy_ref, dx_ref, dg_ref = refs
        xf = x_ref[...].astype(F32)
        dyf = dy_ref[...].astype(F32)
        r = lax.rsqrt(jnp.mean(xf * xf, axis=-1, keepdims=True) + EPS)
        xn = xf * r
        dyg = dyf * g_ref[...]
        dx = r * (dyg - xn * jnp.mean(dyg * xn, axis=-1, keepdims=True))
        if has_res:
            dx = dx + r_ref[...]
        dx_ref[...] = dx

        @pl.when(pl.program_id(0) == 0)
        def _():
            dg_ref[...] = jnp.zeros_like(dg_ref)

        dg_ref[...] += jnp.sum(dyf * xn, axis=0, keepdims=True)

    row = pl.BlockSpec((tm, c), lambda i: (i, 0))
    vec = pl.BlockSpec((1, c), lambda i: (0, 0))
    in_specs = [row, vec, row] + ([row] if has_res else [])
    args = (x, g, dy) + ((residual,) if has_res else ())
    return pl.pallas_call(
        body, out_shape=(jax.ShapeDtypeStruct((m, c), F32), jax.ShapeDtypeStruct((1, c), F32)), grid=(m // tm,),
        in_specs=in_specs, out_specs=(row, vec), compiler_params=_cparams(1), name=name)(*args)


def _chan_call(name, fn, m, tp, tc, ncol, row_ins=(), prev_ins=(), next_ins=(), chan_ins=(), row_outs=(),
               red_outs=(), row_split=1):
    tm = _pick(tp, ROW_TILE) // row_split
    tps = tp // tm
    nrow = m // tm
    halo_blocks = tm // HALO_ROWS
    last_halo = m // HALO_ROWS - 1
    n_in = len(row_ins) + len(prev_ins) + len(next_ins) + len(chan_ins)
    n_r, n_p, n_n = len(row_ins), len(prev_ins), len(next_ins)

    def body(*refs):
        i = pl.program_id(1)
        pos = lax.rem(i, tps)
        at_start = pos == 0
        at_end = pos == tps - 1
        rows = [r[...].astype(F32) for r in refs[:n_r]]
        prevs = [jnp.where(at_start, 0.0, r[...].astype(F32)[SUBLANES:]) for r in refs[n_r:n_r + n_p]]
        nexts = [jnp.where(at_end, 0.0, r[...].astype(F32)[:SUBLANES]) for r in refs[n_r + n_p:n_r + n_p + n_n]]
        chans = [r[...] for r in refs[n_r + n_p + n_n:n_in]]
        out_refs = refs[n_in:n_in + len(row_outs)]
        red_refs = refs[n_in + len(row_outs):]
        row_vals, red_vals = fn(rows, prevs, nexts, chans)
        for ref, val in zip(out_refs, row_vals):
            ref[...] = val.astype(ref.dtype)
        if red_refs:
            @pl.when(i == 0)
            def _():
                for ref in red_refs:
                    ref[...] = jnp.zeros_like(ref)

            for ref, val in zip(red_refs, red_vals):
                ref[...] += val

    in_specs, args = [], []
    for arr, off in row_ins:
        in_specs.append(pl.BlockSpec((tm, tc), lambda j, i, off=off: (i, j + off)))
        args.append(arr)
    for arr, off in prev_ins:
        in_specs.append(pl.BlockSpec((HALO_ROWS, tc),
                                     lambda j, i, off=off: (jnp.maximum(i * halo_blocks - 1, 0), j + off)))
        args.append(arr)
    for arr, off in next_ins:
        in_specs.append(pl.BlockSpec((HALO_ROWS, tc),
                                     lambda j, i, off=off: (jnp.minimum((i + 1) * halo_blocks, last_halo), j + off)))
        args.append(arr)
    for arr, off in chan_ins:
        in_specs.append(pl.BlockSpec((arr.shape[0], tc), lambda j, i, off=off: (0, j + off)))
        args.append(arr)
    out_shape, out_specs = [], []
    for (dt,) in row_outs:
        out_shape.append(jax.ShapeDtypeStruct((m, ncol * tc), dt))
        out_specs.append(pl.BlockSpec((tm, tc), lambda j, i: (i, j)))
    for (k,) in red_outs:
        out_shape.append(jax.ShapeDtypeStruct((k, ncol * tc), F32))
        out_specs.append(pl.BlockSpec((k, tc), lambda j, i: (0, j)))
    return pl.pallas_call(
        body, out_shape=tuple(out_shape), grid=(ncol, nrow), in_specs=in_specs, out_specs=tuple(out_specs),
        compiler_params=_cparams(2), name=name)(*args)


def _shift_down(x, prev8, s):
    if s == 0:
        return x
    xs = pltpu.roll(x, s, 0)
    ps = pltpu.roll(prev8, s, 0)
    rid = lax.broadcasted_iota(jnp.int32, prev8.shape, 0)
    head = jnp.where(rid < s, ps, xs[:SUBLANES])
    return jnp.concatenate([head, xs[SUBLANES:]], axis=0)


def _shift_up(x, next8, s):
    if s == 0:
        return x
    tm = x.shape[0]
    xs = pltpu.roll(x, tm - s, 0)
    ns = pltpu.roll(next8, SUBLANES - s, 0)
    rid = lax.broadcasted_iota(jnp.int32, next8.shape, 0)
    tail = jnp.where(rid >= SUBLANES - s, ns, xs[tm - SUBLANES:])
    return jnp.concatenate([xs[:tm - SUBLANES], tail], axis=0)


def _taps(x, prev8, kw):
    return [_shift_down(x, prev8, kw - 1 - k) for k in range(kw)]


def _conv_taps(taps, w):
    y = w[0:1, :] * taps[0]
    for k in range(1, len(taps)):
        y = y + w[k:k + 1, :] * taps[k]
    return y


def _conv_dw_taps(dy, taps):
    shape = (SUBLANES, dy.shape[1])
    rid = lax.broadcasted_iota(jnp.int32, shape, 0)
    out = jnp.zeros(shape, F32)
    for k, tap in enumerate(taps):
        out = out + jnp.where(rid == k, jnp.sum(dy * tap, axis=0, keepdims=True), 0.0)
    return out


def _conv_fwd(x, prev8, w):
    return _conv_taps(_taps(x, prev8, w.shape[0]), w)


def _conv_dw(dy, x, prev8, kw):
    return _conv_dw_taps(dy, _taps(x, prev8, kw))


def _conv_dx(dy, next8, w):
    kw = w.shape[0]
    dx = w[kw - 1:kw, :] * dy
    for k in range(kw - 1):
        dx = dx + w[k:k + 1, :] * _shift_up(dy, next8, kw - 1 - k)
    return dx


def _sigmoid(x):
    return 1.0 / (1.0 + jnp.exp(-x))


def _expm1(x):
    series = x * (1.0 + x * 0.5 * (1.0 + x * (1.0 / 3.0) * (1.0 + x * 0.25 * (1.0 + x * 0.2))))
    return jnp.where(jnp.abs(x) < 0.3, series, jnp.exp(x) - 1.0)


def _softplus_neg(lam):
    e = jnp.exp(-jnp.abs(lam))
    log1p = jnp.where(e < 1e-2, e * (1.0 - e * (0.5 - e * (1.0 / 3.0))), jnp.log(1.0 + e))
    return jnp.maximum(-lam, 0.0) + log1p


GELU_C = math.sqrt(2.0 / math.pi)


def _gelu(x):
    return 0.5 * x * (1.0 + jnp.tanh(GELU_C * (x + 0.044715 * x * x * x)))


def _gelu_grad(x):
    t = jnp.tanh(GELU_C * (x + 0.044715 * x * x * x))
    return 0.5 * (1.0 + t) + 0.5 * x * (1.0 - t * t) * GELU_C * (1.0 + 3.0 * 0.044715 * x * x)


FFN_COL_TILE = 1408


def _ffn_fwd(x, p, m, tp):
    h = _rms_fwd(x, p["norm"], "ffn_norm")
    u = _matmul(h, p["w_up"], "nn", BF16, name="ffn_up")
    tc = FFN_COL_TILE
    ncol = D_FF // tc

    def gate(rows, prevs, nexts, chans):
        ua, ug = rows
        wa, wg, ba, bg = chans
        a = _conv_fwd(ua, prevs[0], wa) + ba
        g = _conv_fwd(ug, prevs[1], wg) + bg
        return [a * _sigmoid(a) * g, a, g], []

    z, a_act, g_act = _chan_call(
        "ffn_gate", gate, m, tp, tc, ncol, row_ins=[(u, 0), (u, ncol)], prev_ins=[(u, 0), (u, ncol)],
        chan_ins=[(p["conv_w"], 0), (p["conv_w"], ncol), (p["conv_b"], 0), (p["conv_b"], ncol)],
        row_outs=[(BF16,), (BF16,), (BF16,)])
    out = _matmul(z, p["w_down"], "nn", F32, residual=x, name="ffn_down")
    return out, (x, h, u, z, a_act, g_act)


def _ffn_bwd(dout, p, saved, m, tp):
    x, h, u, z, a_act, g_act = saved
    tc = FFN_COL_TILE
    ncol = D_FF // tc
    dz = _matmul(dout, p["w_down"], "nt", F32, name="ffn_down_dx")
    d_w_down = _matmul(z, dout, "tn", F32, name="ffn_down_dw")

    def gate_bwd(rows, prevs, nexts, chans):
        ua, ug, dzv, a, g = rows
        sg = _sigmoid(a)
        da = dzv * g * (sg * (1.0 + a * (1.0 - sg)))
        dg = dzv * a * sg
        return ([da, dg],
                [_conv_dw(da, ua, prevs[0], 3), _conv_dw(dg, ug, prevs[1], 3),
                 jnp.sum(da, axis=0, keepdims=True), jnp.sum(dg, axis=0, keepdims=True)])

    da, dg, dcw_a, dcw_g, dcb_a, dcb_g = _chan_call(
        "ffn_gate_bwd", gate_bwd, m, tp, tc, ncol,
        row_ins=[(u, 0), (u, ncol), (dz, 0), (a_act, 0), (g_act, 0)], prev_ins=[(u, 0), (u, ncol)],
        row_outs=[(F32,), (F32,)], red_outs=[(SUBLANES,), (SUBLANES,), (1,), (1,)], row_split=2)

    def conv_dx(rows, prevs, nexts, chans):
        return [_conv_dx(rows[0], nexts[0], chans[0]), _conv_dx(rows[1], nexts[1], chans[1])], []

    dua, dug = _chan_call("ffn_conv_dx", conv_dx, m, tp, tc, ncol, row_ins=[(da, 0), (dg, 0)],
                          next_ins=[(da, 0), (dg, 0)], chan_ins=[(p["conv_w"], 0), (p["conv_w"], ncol)],
                          row_outs=[(BF16,), (BF16,)])
    d_w_up = jnp.concatenate([_matmul(h, dua, "tn", F32, name="ffn_up_dw_a"),
                              _matmul(h, dug, "tn", F32, name="ffn_up_dw_g")], axis=1)
    dh = _matmul(dua, p["w_up_a"], "nt", F32, name="ffn_up_dx_a")
    dh = _matmul(dug, p["w_up_g"], "nt", F32, residual=dh, name="ffn_up_dx_g")
    dx, d_norm = _rms_bwd(x, p["norm"], dh, dout, "ffn_norm_bwd")
    d_conv_w = jnp.concatenate([dcw_a[:3], dcw_g[:3]], axis=1)
    d_conv_b = jnp.concatenate([dcb_a, dcb_g], axis=1)
    return dx, dict(norm=d_norm, w_up=d_w_up, conv_w=d_conv_w, conv_b=d_conv_b, w_down=d_w_down)


def _to_scan(x, nb, tp):
    return x.reshape(nb, tp, LRU_WIDTH // LANES, LANES).transpose(1, 0, 2, 3).reshape(tp, -1, LANES)


def _from_scan(x, nb, tp):
    return x.reshape(tp, nb, LRU_WIDTH // LANES, LANES).transpose(1, 0, 2, 3).reshape(nb * tp, LRU_WIDTH)


def _scan_fwd(a, u):
    t_len, s, _ = a.shape
    tc = _pick(t_len, 640)
    blk = pl.BlockSpec((tc, s, LANES), lambda i: (i, 0, 0))

    def body(a_ref, u_ref, h_ref, carry):
        @pl.when(pl.program_id(0) == 0)
        def _():
            carry[...] = jnp.zeros_like(carry)

        def step(t, h):
            h = a_ref[t] * h + u_ref[t]
            h_ref[t] = h
            return h

        carry[...] = lax.fori_loop(0, tc, step, carry[...], unroll=8)

    return pl.pallas_call(
        body, out_shape=jax.ShapeDtypeStruct(a.shape, F32), grid=(t_len // tc,), in_specs=[blk, blk], out_specs=blk,
        scratch_shapes=[pltpu.VMEM((s, LANES), F32)], compiler_params=_cparams(1), name="lru_scan")(a, u)


def _scan_bwd(dh, a, h_prev):
    t_len, s, _ = a.shape
    tc = _pick(t_len, 640)
    nb = t_len // tc
    blk = pl.BlockSpec((tc, s, LANES), lambda i: (nb - 1 - i, 0, 0))

    def body(dh_ref, a_ref, hp_ref, du_ref, da_ref, carry):
        @pl.when(pl.program_id(0) == 0)
        def _():
            carry[...] = jnp.zeros_like(carry)

        def step(k, c):
            t = tc - 1 - k
            d = dh_ref[t] + c
            du_ref[t] = d
            da_ref[t] = d * hp_ref[t]
            return a_ref[t] * d

        carry[...] = lax.fori_loop(0, tc, step, carry[...], unroll=8)

    shp = jax.ShapeDtypeStruct(a.shape, F32)
    return pl.pallas_call(
        body, out_shape=(shp, shp), grid=(nb,), in_specs=[blk, blk, blk], out_specs=(blk, blk),
        scratch_shapes=[pltpu.VMEM((s, LANES), F32)], compiler_params=_cparams(1), name="lru_scan_bwd")(dh, a, h_prev)


def _lru_gates(xc, zr, zi, r_b, i_b, lam):
    r = _sigmoid(zr + r_b)
    ig = _sigmoid(zi + i_b)
    sp = _softplus_neg(lam)
    log_a = -LRU_C * r * sp
    a = jnp.exp(log_a)
    mult = jnp.sqrt(-_expm1(2.0 * log_a))
    return r, ig, sp, a, mult


def _even_fwd(x, p, m, tp, nb):
    c = LRU_WIDTH
    h = _rms_fwd(x, p["norm"], "ev_norm")
    u = _matmul(h, p["w_in"], "nn", F32, name="ev_in")

    def pre(rows, prevs, nexts, chans):
        gb, gc, xa, xb = rows
        wa, wb, bias = chans
        pa = gc * xa
        ya = gb * _conv_fwd(pa, prevs[0] * prevs[1], wa)
        xc = _conv_fwd(xb, prevs[2], wb) + bias
        return [ya, xc], []

    ya, xc = _chan_call("ev_pre", pre, m, tp, c, 1, row_ins=[(u, 0), (u, 1), (u, 2), (u, 3)],
                        prev_ins=[(u, 1), (u, 2), (u, 3)],
                        chan_ins=[(p["conv_a"], 0), (p["conv_b"], 0), (p["conv_b_bias"], 0)],
                        row_outs=[(BF16,), (F32,)])
    zr = _matmul(xc, p["gate_r"], "nn", F32, name="ev_gate_r")
    zi = _matmul(xc, p["gate_i"], "nn", F32, name="ev_gate_i")

    def lru_in(rows, prevs, nexts, chans):
        xcv, zrv, ziv = rows
        r, ig, sp, a, mult = _lru_gates(xcv, zrv, ziv, *chans)
        return [a, mult * (ig * xcv)], []

    a, uu = _chan_call("ev_lru_in", lru_in, m, tp, c, 1, row_ins=[(xc, 0), (zr, 0), (zi, 0)],
                       chan_ins=[(p["gate_r_b"], 0), (p["gate_i_b"], 0), (p["lam"], 0)],
                       row_outs=[(F32,), (F32,)])
    a_s = _to_scan(a, nb, tp)
    hs_s = _scan_fwd(a_s, _to_scan(uu, nb, tp))
    hs = _from_scan(hs_s, nb, tp)

    def post(rows, prevs, nexts, chans):
        gate, hv = rows
        return [_gelu(gate) * hv], []

    (yb,) = _chan_call("ev_post", post, m, tp, c, 1, row_ins=[(u, 4), (hs, 0)], row_outs=[(BF16,)])
    out = _matmul(ya, p["w_out_a"], "nn", F32, residual=x, name="ev_out_a")
    out = _matmul(yb, p["w_out_b"], "nn", F32, residual=out, name="ev_out_b")
    return out, (x, h, u, ya, xc, zr, zi, a_s, hs_s, hs, yb)


def _even_bwd(dout, p, saved, m, tp, nb):
    c = LRU_WIDTH
    x, h, u, ya, xc, zr, zi, a_s, hs_s, hs, yb = saved
    dy = _matmul(dout, p["w_out"], "nt", F32, name="ev_out_dx")
    d_w_out = jnp.concatenate([_matmul(ya, dout, "tn", F32, name="ev_out_dw_a"),
                               _matmul(yb, dout, "tn", F32, name="ev_out_dw_b")], axis=0)

    def post_bwd(rows, prevs, nexts, chans):
        dyb, gate, hv = rows
        return [dyb * hv * _gelu_grad(gate), dyb * _gelu(gate)], []

    dgate, dhs = _chan_call("ev_post_bwd", post_bwd, m, tp, c, 1, row_ins=[(dy, 1), (u, 4), (hs, 0)],
                            row_outs=[(F32,), (F32,)])
    h_prev = jnp.concatenate([jnp.zeros_like(hs_s[:1]), hs_s[:-1]], axis=0)
    du_s, da_s = _scan_bwd(_to_scan(dhs, nb, tp), a_s, h_prev)
    du = _from_scan(du_s, nb, tp)
    da = _from_scan(da_s, nb, tp)

    def lru_in_bwd(rows, prevs, nexts, chans):
        duv, dav, xcv, zrv, ziv = rows
        r, ig, sp, a, mult = _lru_gates(xcv, zrv, ziv, *chans)
        dxc = duv * mult * ig
        dig = duv * mult * xcv
        dmult = duv * ig * xcv
        dlog_a = dav * a - dmult * (a * a) / jnp.maximum(mult, 1e-30)
        dr = dlog_a * (-LRU_C * sp)
        dzr = dr * r * (1.0 - r)
        dzi = dig * ig * (1.0 - ig)
        dsp = jnp.sum(dlog_a * (-LRU_C * r), axis=0, keepdims=True)
        dlam = -dsp * _sigmoid(-chans[2])
        return ([dzr, dzi, dxc],
                [jnp.sum(dzr, axis=0, keepdims=True), jnp.sum(dzi, axis=0, keepdims=True), dlam])

    dzr, dzi, dxc, d_r_b, d_i_b, d_lam = _chan_call(
        "ev_lru_in_bwd", lru_in_bwd, m, tp, c, 1, row_ins=[(du, 0), (da, 0), (xc, 0), (zr, 0), (zi, 0)],
        chan_ins=[(p["gate_r_b"], 0), (p["gate_i_b"], 0), (p["lam"], 0)],
        row_outs=[(F32,), (F32,), (F32,)], red_outs=[(1,), (1,), (1,)])
    d_gate_r = _matmul(xc, dzr, "tn", F32, name="ev_gate_r_dw")
    d_gate_i = _matmul(xc, dzi, "tn", F32, name="ev_gate_i_dw")
    dxc = _matmul(dzr, p["gate_r"], "nt", F32, residual=dxc, name="ev_gate_r_dx")
    dxc = _matmul(dzi, p["gate_i"], "nt", F32, residual=dxc, name="ev_gate_i_dx")

    def conv_b_bwd(rows, prevs, nexts, chans):
        dxcv, xb = rows
        return ([_conv_dx(dxcv, nexts[0], chans[0])],
                [_conv_dw(dxcv, xb, prevs[0], 4), jnp.sum(dxcv, axis=0, keepdims=True)])

    dxb, d_conv_b, d_bias = _chan_call(
        "ev_conv_b_bwd", conv_b_bwd, m, tp, c, 1, row_ins=[(dxc, 0), (u, 3)], prev_ins=[(u, 3)], next_ins=[(dxc, 0)],
        chan_ins=[(p["conv_b"], 0)], row_outs=[(F32,)], red_outs=[(SUBLANES,), (1,)])

    def mix_a_bwd(rows, prevs, nexts, chans):
        dya, gb, gc, xa = rows
        (wa,) = chans
        taps = _taps(gc * xa, prevs[0] * prevs[1], 3)
        ca = _conv_taps(taps, wa)
        dca = dya * gb
        dpa = _conv_dx(dca, nexts[0] * nexts[1], wa)
        return [dya * ca, dpa * xa, dpa * gc], [_conv_dw_taps(dca, taps)]

    dgb, dgc, dxa, d_conv_a = _chan_call(
        "ev_mix_a_bwd", mix_a_bwd, m, tp, c, 1, row_ins=[(dy, 0), (u, 0), (u, 1), (u, 2)],
        prev_ins=[(u, 1), (u, 2)], next_ins=[(dy, 0), (u, 0)], chan_ins=[(p["conv_a"], 0)],
        row_outs=[(F32,), (F32,), (F32,)], red_outs=[(SUBLANES,)])
    du_all = jnp.concatenate([dgb, dgc, dxa, dxb, dgate], axis=1)
    d_w_in = _matmul(h, du_all, "tn", F32, name="ev_in_dw")
    dh = _matmul(du_all, p["w_in"], "nt", F32, name="ev_in_dx")
    dx, d_norm = _rms_bwd(x, p["norm"], dh, dout, "ev_norm_bwd")
    return dx, dict(norm=d_norm, w_in=d_w_in, conv_a=d_conv_a[:3], conv_b=d_conv_b[:4], conv_b_bias=d_bias,
                    gate_r=d_gate_r, gate_r_b=d_r_b, gate_i=d_gate_i, gate_i_b=d_i_b, lam=d_lam, w_out=d_w_out)


def _rope_tables(tp):
    pos = jnp.arange(tp, dtype=F32)
    inv_freq = ROPE_BASE ** (-jnp.arange(0, QK_ROPE, 2, dtype=F32) / QK_ROPE)
    ang = pos[:, None] * inv_freq[None, :]
    cos, sin = jnp.cos(ang), jnp.sin(ang)
    half = QK_ROPE // 2
    one = jnp.ones((tp, QK_NOPE), F32)
    z64 = jnp.zeros((tp, QK_NOPE), F32)
    zh = jnp.zeros((tp, half), F32)
    zt = jnp.zeros((tp, HEAD_PAD - QK_HEAD), F32)
    c_tab = jnp.concatenate([one, cos, cos, zt], axis=1)
    s_lo = jnp.concatenate([z64, -sin, zh, zt], axis=1)
    s_hi = jnp.concatenate([z64, zh, sin, zt], axis=1)
    return c_tab, s_lo, s_hi


def _rope(v, c_tab, s_lo, s_hi):
    half = QK_ROPE // 2
    return v * c_tab + pltpu.roll(v, HEAD_PAD - half, 1) * s_lo + pltpu.roll(v, half, 1) * s_hi


def _rope_t(dv, c_tab, s_lo, s_hi):
    half = QK_ROPE // 2
    return dv * c_tab + pltpu.roll(dv * s_lo, half, 1) + pltpu.roll(dv * s_hi, HEAD_PAD - half, 1)


def _rope_call(name, fn, m, tp, ins, tables, out_dtype, shared_pre=None):
    tm = _pick(tp, ROW_TILE)
    tps = tp // tm
    n = len(ins)
    width = MLA_HEADS * HEAD_PAD

    def body(*refs):
        tabs = [r[...] for r in refs[n:n + 3]]
        shared = [None if fc is None else shared_pre(refs[a][...].astype(F32), *tabs) for a, (_, fc) in enumerate(ins)]
        for hh in range(MLA_HEADS):
            lanes = slice(hh * HEAD_PAD, (hh + 1) * HEAD_PAD)
            vals = [refs[a][:, lanes].astype(F32) if shared[a] is None else shared[a] for a in range(n)]
            refs[n + 3][:, lanes] = fn(*vals, *tabs).astype(out_dtype)

    in_specs, args = [], []
    for arr, fixed_col in ins:
        if fixed_col is None:
            in_specs.append(pl.BlockSpec((tm, width), lambda i: (i, 0)))
        else:
            in_specs.append(pl.BlockSpec((tm, HEAD_PAD), lambda i, fc=fixed_col: (i, fc)))
        args.append(arr)
    for tab in tables:
        in_specs.append(pl.BlockSpec((tm, HEAD_PAD), lambda i: (lax.rem(i, tps), 0)))
        args.append(tab)
    return pl.pallas_call(
        body, out_shape=jax.ShapeDtypeStruct((m, width), out_dtype), grid=(m // tm,),
        in_specs=in_specs, out_specs=pl.BlockSpec((tm, width), lambda i: (i, 0)),
        compiler_params=_cparams(1), name=name)(*args)


def _rope_k_bwd(dk, tables, m, tp):
    tm = _pick(tp, ROW_TILE)
    tps = tp // tm

    def body(dk_ref, c_ref, lo_ref, hi_ref, o_ref):
        acc = dk_ref[:, 0:HEAD_PAD].astype(F32)
        for hh in range(1, MLA_HEADS):
            acc = acc + dk_ref[:, hh * HEAD_PAD:(hh + 1) * HEAD_PAD].astype(F32)
        d = pltpu.roll(_rope_t(acc, c_ref[...], lo_ref[...], hi_ref[...]), QK_NOPE, 1)
        lane = lax.broadcasted_iota(jnp.int32, d.shape, 1)
        o_ref[...] = jnp.where(lane < QK_ROPE, d, 0.0)

    tab = pl.BlockSpec((tm, HEAD_PAD), lambda i: (lax.rem(i, tps), 0))
    return pl.pallas_call(
        body, out_shape=jax.ShapeDtypeStruct((m, HEAD_PAD), F32), grid=(m // tm,),
        in_specs=[pl.BlockSpec((tm, MLA_HEADS * HEAD_PAD), lambda i: (i, 0)), tab, tab, tab],
        out_specs=pl.BlockSpec((tm, HEAD_PAD), lambda i: (i, 0)), compiler_params=_cparams(1),
        name="od_rope_k_bwd")(dk, *tables)


def _causal_mask(row0, col0, shape):
    rows = row0 + lax.broadcasted_iota(jnp.int32, shape, 0)
    cols = col0 + lax.broadcasted_iota(jnp.int32, shape, 1)
    return cols <= rows


NT = (((1,), (1,)), ((), ()))
TN = (((0,), (0,)), ((), ()))
HEADS_PER_STEP = 2
HEAD_STEPS = MLA_HEADS // HEADS_PER_STEP
STEP_LANES = HEADS_PER_STEP * HEAD_PAD


def _flash_fwd(q, k, v, nb, tp):
    tq = _pick(tp, ROW_TILE)
    nq = tp // tq

    def body(q_ref, k_ref, v_ref, o_ref, lse_ref):
        i = pl.program_id(2)
        qbs = [q_ref[:, hd * HEAD_PAD:(hd + 1) * HEAD_PAD] for hd in range(HEADS_PER_STEP)]

        def chunk(j, carry, masked, width=1):
            off = pl.multiple_of(j * tq, tq)
            out = []
            for hd in range(HEADS_PER_STEP):
                mx, l, acc = carry[hd]
                lanes = slice(hd * HEAD_PAD, (hd + 1) * HEAD_PAD)
                kb = k_ref[pl.ds(off, width * tq), lanes]
                vb = v_ref[pl.ds(off, width * tq), lanes]
                s = lax.dot_general(qbs[hd], kb, NT, preferred_element_type=F32)
                if masked:
                    s = jnp.where(_causal_mask(0, 0, s.shape), s, NEG)
                m_new = jnp.maximum(mx, jnp.max(s, axis=1, keepdims=True))
                alpha = jnp.exp(mx - m_new)
                pr = jnp.exp(s - m_new)
                l = alpha * l + jnp.sum(pr, axis=1, keepdims=True)
                acc = alpha * acc + jnp.dot(pr.astype(BF16), vb, preferred_element_type=F32)
                out.append((m_new, l, acc))
            return tuple(out)

        one = (jnp.full((tq, 1), NEG, F32), jnp.zeros((tq, 1), F32), jnp.zeros((tq, HEAD_PAD), F32))
        quads = i // 4
        carry = lax.fori_loop(0, quads, lambda jj, c: chunk(4 * jj, c, False, 4), (one,) * HEADS_PER_STEP)
        carry = lax.fori_loop(0, lax.rem(i, 4) // 2, lambda _, c: chunk(4 * quads, c, False, 2), carry)
        carry = lax.fori_loop(0, lax.rem(i, 2), lambda _, c: chunk(i - 1, c, False), carry)
        carry = chunk(i, carry, True)
        for hd in range(HEADS_PER_STEP):
            mx, l, acc = carry[hd]
            lanes = slice(hd * HEAD_PAD, (hd + 1) * HEAD_PAD)
            o_ref[:, lanes] = (acc / l).astype(o_ref.dtype)
            lse_ref[:, lanes] = jnp.broadcast_to(mx + jnp.log(l), (tq, HEAD_PAD))

    qspec = pl.BlockSpec((tq, STEP_LANES), lambda b, hh, i: (b * nq + i, hh))
    kvspec = pl.BlockSpec((tp, STEP_LANES), lambda b, hh, i: (b, hh))
    shp = (nb * tp, MLA_HEADS * HEAD_PAD)
    return pl.pallas_call(
        body, out_shape=(jax.ShapeDtypeStruct(shp, BF16), jax.ShapeDtypeStruct(shp, F32)),
        grid=(nb, HEAD_STEPS, nq), in_specs=[qspec, kvspec, kvspec], out_specs=(qspec, qspec),
        compiler_params=_cparams(3), name="od_flash_fwd")(q, k, v)


def _flash_prep(o, do, lse_c, nb, tp):
    tq = _pick(tp, ROW_TILE)
    nq = tp // tq

    def body(o_ref, do_ref, lse_ref, lr_ref, dr_ref):
        for hh in range(MLA_HEADS):
            lanes = slice(hh * HEAD_PAD, (hh + 1) * HEAD_PAD)
            delta = jnp.sum(o_ref[:, lanes].astype(F32) * do_ref[:, lanes].astype(F32), axis=1, keepdims=True)
            lr_ref[hh] = jnp.transpose(lse_ref[:, lanes])[0:SUBLANES, :]
            dr_ref[hh] = jnp.transpose(jnp.broadcast_to(delta, (tq, HEAD_PAD)))[0:SUBLANES, :]

    qspec = pl.BlockSpec((tq, MLA_HEADS * HEAD_PAD), lambda b, i: (b * nq + i, 0))
    rspec = pl.BlockSpec((MLA_HEADS, None, SUBLANES, tq), lambda b, i: (b, i, 0, 0))
    rshape = jax.ShapeDtypeStruct((nb * MLA_HEADS, nq, SUBLANES, tq), F32)
    return pl.pallas_call(
        body, out_shape=(rshape, rshape), grid=(nb, nq), in_specs=[qspec, qspec, qspec],
        out_specs=(rspec, rspec), compiler_params=_cparams(2), name="od_flash_prep")(o, do, lse_c)


def _flash_bwd(q, k, v, do, lse_r, delta_r, nb, tp):
    tq = _pick(tp, ROW_TILE)
    nq = tp // tq

    def body(q_ref, k_ref, v_ref, do_ref, lse_ref, dl_ref, dq_ref, dk_ref, dv_ref):
        j = pl.program_id(2)

        @pl.when(j == 0)
        def _():
            dq_ref[...] = jnp.zeros_like(dq_ref)

        kbs = [k_ref[:, hd * HEAD_PAD:(hd + 1) * HEAD_PAD] for hd in range(HEADS_PER_STEP)]
        vbs = [v_ref[:, hd * HEAD_PAD:(hd + 1) * HEAD_PAD] for hd in range(HEADS_PER_STEP)]

        def chunk(i, carry, masked):
            off = pl.multiple_of(i * tq, tq)
            out = []
            for hd in range(HEADS_PER_STEP):
                dk, dv = carry[hd]
                lanes = slice(hd * HEAD_PAD, (hd + 1) * HEAD_PAD)
                qb = q_ref[pl.ds(off, tq), lanes]
                dob = do_ref[pl.ds(off, tq), lanes]
                lse = lse_ref[hd, i][0:1, :]
                delta = dl_ref[hd, i][0:1, :]
                st = lax.dot_general(kbs[hd], qb, NT, preferred_element_type=F32)
                pt = jnp.exp(st - lse)
                if masked:
                    keys = lax.broadcasted_iota(jnp.int32, st.shape, 0)
                    queries = lax.broadcasted_iota(jnp.int32, st.shape, 1)
                    pt = jnp.where(keys <= queries, pt, 0.0)
                dv = dv + jnp.dot(pt.astype(BF16), dob, preferred_element_type=F32)
                dpt = lax.dot_general(vbs[hd], dob, NT, preferred_element_type=F32)
                dst = (pt * (dpt - delta)).astype(BF16)
                dk = dk + jnp.dot(dst, qb, preferred_element_type=F32)
                dq_ref[pl.ds(off, tq), lanes] += lax.dot_general(dst, kbs[hd], TN, preferred_element_type=F32)
                out.append((dk, dv))
            return tuple(out)

        zero = jnp.zeros((tq, HEAD_PAD), F32)
        carry = chunk(j, ((zero, zero),) * HEADS_PER_STEP, True)
        carry = lax.fori_loop(j + 1, nq, lambda i, c: chunk(i, c, False), carry)
        for hd in range(HEADS_PER_STEP):
            lanes = slice(hd * HEAD_PAD, (hd + 1) * HEAD_PAD)
            dk_ref[:, lanes] = carry[hd][0]
            dv_ref[:, lanes] = carry[hd][1].astype(dv_ref.dtype)

    tspec = pl.BlockSpec((tq, STEP_LANES), lambda b, hh, j: (b * nq + j, hh))
    fullspec = pl.BlockSpec((tp, STEP_LANES), lambda b, hh, j: (b, hh))
    rspec = pl.BlockSpec((HEADS_PER_STEP, nq, SUBLANES, tq), lambda b, hh, j: (b * HEAD_STEPS + hh, 0, 0, 0))
    shp = (nb * tp, MLA_HEADS * HEAD_PAD)
    return pl.pallas_call(
        body, out_shape=(jax.ShapeDtypeStruct(shp, F32), jax.ShapeDtypeStruct(shp, F32),
                         jax.ShapeDtypeStruct(shp, BF16)),
        grid=(nb, HEAD_STEPS, nq), in_specs=[fullspec, tspec, tspec, fullspec, rspec, rspec],
        out_specs=(fullspec, tspec, tspec), compiler_params=_cparams(3),
        name="od_flash_bwd")(q, k, v, do, lse_r, delta_r)


def _odd_fwd(x, p, tables, m, tp, nb):
    scale = QK_HEAD ** -0.5
    h = _rms_fwd(x, p["norm"], "od_norm")
    u = _matmul(h, p["w_in"], "nn", F32, name="od_in")
    cq = u[:, :Q_LORA]
    ckv = u[:, Q_LORA:Q_LORA + KV_LORA]
    cqn = _rms_fwd(cq, p["q_norm"], "od_q_norm")
    ckvn = _rms_fwd(ckv, p["kv_norm"], "od_kv_norm")
    q_raw = _matmul(cqn, p["w_uq"], "nn", F32, name="od_uq")
    k_raw = _matmul(ckvn, p["w_uk"], "nn", F32, name="od_uk")
    v = _matmul(ckvn, p["w_uv"], "nn", BF16, name="od_uv")
    q = _rope_call("od_rope_q", lambda qv, c, lo, hi: _rope(qv, c, lo, hi) * scale, m, tp, [(q_raw, None)], tables,
                   BF16)
    kr_col = (Q_LORA + KV_LORA) // HEAD_PAD
    k = _rope_call("od_rope_k", lambda kv, kr, c, lo, hi: kv + kr, m, tp, [(k_raw, None), (u, kr_col)], tables, BF16,
                   shared_pre=lambda uv, c, lo, hi: _rope(pltpu.roll(uv, QK_NOPE, 1), c, lo, hi))
    o, lse_c = _flash_fwd(q, k, v, nb, tp)
    out = _matmul(o, p["w_out"], "nn", F32, residual=x, name="od_out")
    return out, (x, h, cq, ckv, cqn, ckvn, q, k, v, o, lse_c)


def _odd_bwd(dout, p, tables, saved, m, tp, nb):
    scale = QK_HEAD ** -0.5
    x, h, cq, ckv, cqn, ckvn, q, k, v, o, lse_c = saved
    do = _matmul(dout, p["w_out"], "nt", BF16, name="od_out_dx")
    d_w_out = _matmul(o, dout, "tn", F32, name="od_out_dw")
    lse_r, delta_r = _flash_prep(o, do, lse_c, nb, tp)
    dq, dk, dv = _flash_bwd(q, k, v, do, lse_r, delta_r, nb, tp)
    dq_raw = _rope_call("od_rope_q_bwd", lambda d, c, lo, hi: _rope_t(d, c, lo, hi) * scale, m, tp, [(dq, None)],
                        tables, BF16)
    dkr = _rope_k_bwd(dk, tables, m, tp)
    d_w_uq = _matmul(cqn, dq_raw, "tn", F32, name="od_uq_dw")
    d_w_uk = _matmul(ckvn, dk, "tn", F32, name="od_uk_dw")
    d_w_uv = _matmul(ckvn, dv, "tn", F32, name="od_uv_dw")
    dcqn = _matmul(dq_raw, p["w_uq"], "nt", F32, name="od_uq_dx")
    dckvn = _matmul(dk, p["w_uk"], "nt", F32, name="od_uk_dx")
    dckvn = _matmul(dv, p["w_uv"], "nt", F32, residual=dckvn, name="od_uv_dx")
    dcq, d_q_norm = _rms_bwd(cq, p["q_norm"], dcqn, None, "od_q_norm_bwd")
    dckv, d_kv_norm = _rms_bwd(ckv, p["kv_norm"], dckvn, None, "od_kv_norm_bwd")
    du = jnp.concatenate([dcq, dckv, dkr], axis=1)
    d_w_in = _matmul(h, du, "tn", F32, name="od_in_dw")
    dh = _matmul(du, p["w_in"], "nt", F32, name="od_in_dx")
    dx, d_norm = _rms_bwd(x, p["norm"], dh, dout, "od_norm_bwd")
    return dx, dict(norm=d_norm, w_in=d_w_in, q_norm=d_q_norm, kv_norm=d_kv_norm, w_uq=d_w_uq, w_uk=d_w_uk,
                    w_uv=d_w_uv, w_out=d_w_out)


def _loss_head(hf, g, target, tp, t_real):
    m, c = hf.shape
    tm = _pick(tp, ROW_TILE)
    tps = tp // tm

    def body(x_ref, g_ref, t_ref, dx_ref, dg_ref, loss_ref):
        i = pl.program_id(0)
        xf = x_ref[...]
        r = lax.rsqrt(jnp.mean(xf * xf, axis=-1, keepdims=True) + EPS)
        xn = xf * r
        t_pos = lax.rem(i, tps) * tm + lax.broadcasted_iota(jnp.int32, (tm, 1), 0)
        valid = jnp.logical_and(t_pos >= N_META, t_pos < t_real)
        err = jnp.where(valid, xn * g_ref[...] - t_ref[...], 0.0)
        dyf = err * (1.0 / c)
        dyg = dyf * g_ref[...]
        dx_ref[...] = r * (dyg - xn * jnp.mean(dyg * xn, axis=-1, keepdims=True))

        @pl.when(i == 0)
        def _():
            dg_ref[...] = jnp.zeros_like(dg_ref)
            loss_ref[...] = jnp.zeros_like(loss_ref)

        dg_ref[...] += jnp.sum(dyf * xn, axis=0, keepdims=True)
        loss_ref[...] += (0.5 / c) * jnp.sum(jnp.sum(err * err, axis=1, keepdims=True), axis=0, keepdims=True)

    row = pl.BlockSpec((tm, c), lambda i: (i, 0))
    vec = pl.BlockSpec((1, c), lambda i: (0, 0))
    return pl.pallas_call(
        body, out_shape=(jax.ShapeDtypeStruct((m, c), F32), jax.ShapeDtypeStruct((1, c), F32),
                         jax.ShapeDtypeStruct((1, 1), F32)),
        grid=(m // tm,), in_specs=[row, vec, row], out_specs=(row, vec, pl.BlockSpec((1, 1), lambda i: (0, 0))),
        compiler_params=_cparams(1), name="loss_head")(hf, g, target)


def _meta_grad(dh0, nb, tp):
    d = dh0.shape[1]

    def body(x_ref, o_ref):
        @pl.when(pl.program_id(0) == 0)
        def _():
            o_ref[...] = jnp.zeros_like(o_ref)

        o_ref[...] += x_ref[...]

    return pl.pallas_call(
        body, out_shape=jax.ShapeDtypeStruct((N_META, d), F32), grid=(nb,),
        in_specs=[pl.BlockSpec((N_META, d), lambda b: (b * (tp // N_META), 0))],
        out_specs=pl.BlockSpec((N_META, d), lambda b: (0, 0)), compiler_params=_cparams(1), name="meta_grad")(dh0)


def _mesh_pos():
    x, y, c = lax.axis_index("x"), lax.axis_index("y"), lax.axis_index("c")
    return x, y, c


N_CHIP = 4
MESH_ID = pl.DeviceIdType.MESH


def _peer_chip(x, y, k):
    px = 1 - x if k & 2 else x
    py = 1 - y if k & 1 else y
    return px, py


def _all_gather(arrays):
    n = len(arrays)

    def body(*refs):
        srcs, outs = refs[:n], refs[n:2 * n]
        send_sems, recv_sems, local_sems = refs[2 * n:]
        x, y, c = _mesh_pos()
        me = 4 * x + 2 * y + c
        sibling = (x, y, 1 - c)

        def copy(a, sem, src, block, to):
            return pltpu.make_async_remote_copy(
                src_ref=src, dst_ref=outs[a].at[block], send_sem=send_sems.at[a, sem], recv_sem=recv_sems.at[a, sem],
                device_id=to, device_id_type=MESH_ID)

        local = [pltpu.make_async_copy(srcs[a], outs[a].at[me], local_sems.at[a]) for a in range(n)]
        for cp in local:
            cp.start()
        sends = [copy(a, 0, srcs[a], me, sibling) for a in range(n)]
        for k in range(1, N_CHIP):
            px, py = _peer_chip(x, y, k)
            sends += [copy(a, k, srcs[a], me, (px, py, c)) for a in range(n)]
        for cp in sends:
            cp.start()
        for k in range(1, N_CHIP):
            px, py = _peer_chip(x, y, k)
            block = 4 * px + 2 * py + c
            for a in range(n):
                copy(a, k, srcs[a], block, sibling).wait_recv()
            passed = [copy(a, N_CHIP - 1 + k, outs[a].at[block], block, sibling) for a in range(n)]
            for cp in passed:
                cp.start()
            sends += passed
        for a in range(n):
            copy(a, 0, srcs[a], 4 * x + 2 * y + (1 - c), sibling).wait_recv()
        for k in range(1, N_CHIP):
            px, py = _peer_chip(x, y, k)
            for a in range(n):
                copy(a, N_CHIP - 1 + k, srcs[a], 4 * px + 2 * py + (1 - c), sibling).wait_recv()
        for cp in sends:
            cp.wait_send()
        for cp in local:
            cp.wait()

    any_spec = pl.BlockSpec(memory_space=pl.ANY)
    out_shape = tuple(jax.ShapeDtypeStruct((N_DEV,) + a.shape, a.dtype) for a in arrays)
    return pl.pallas_call(
        body, out_shape=out_shape, in_specs=[any_spec] * n, out_specs=(any_spec,) * n,
        scratch_shapes=[pltpu.SemaphoreType.DMA((n, N_DEV - 1)), pltpu.SemaphoreType.DMA((n, N_DEV - 1)),
                        pltpu.SemaphoreType.DMA((n,))],
        name="weight_all_gather")(*arrays)


def _pair_exchange(arrays):
    n = len(arrays)

    def body(*refs):
        srcs, outs = refs[:n], refs[n:2 * n]
        send_sems, recv_sems = refs[2 * n:]
        x, y, c = _mesh_pos()
        copies = [pltpu.make_async_remote_copy(
            src_ref=srcs[a], dst_ref=outs[a], send_sem=send_sems.at[a], recv_sem=recv_sems.at[a],
            device_id=(x, y, 1 - c), device_id_type=MESH_ID) for a in range(n)]
        for cp in copies:
            cp.start()
        for cp in copies:
            cp.wait()

    any_spec = pl.BlockSpec(memory_space=pl.ANY)
    return pl.pallas_call(
        body, out_shape=tuple(jax.ShapeDtypeStruct(a.shape, a.dtype) for a in arrays), in_specs=[any_spec] * n,
        out_specs=(any_spec,) * n, scratch_shapes=[pltpu.SemaphoreType.DMA((n,)), pltpu.SemaphoreType.DMA((n,))],
        name="grad_pair_exchange")(*arrays)


def _chip_exchange(arrays):
    n = len(arrays)

    def body(*refs):
        srcs, outs = refs[:n], refs[n:2 * n]
        send_sems, recv_sems, local_sems = refs[2 * n:]
        x, y, c = _mesh_pos()
        q = 2 * x + y
        local = [pltpu.make_async_copy(srcs[a].at[q], outs[a].at[q], local_sems.at[a]) for a in range(n)]
        for cp in local:
            cp.start()

        def copy(a, k, to_q, from_q, px, py):
            return pltpu.make_async_remote_copy(
                src_ref=srcs[a].at[to_q], dst_ref=outs[a].at[from_q], send_sem=send_sems.at[a, k - 1],
                recv_sem=recv_sems.at[a, k - 1], device_id=(px, py, c), device_id_type=MESH_ID)

        sends = []
        for k in range(1, N_CHIP):
            px, py = _peer_chip(x, y, k)
            sends += [copy(a, k, 2 * px + py, q, px, py) for a in range(n)]
        for cp in sends:
            cp.start()
        for k in range(1, N_CHIP):
            px, py = _peer_chip(x, y, k)
            for a in range(n):
                copy(a, k, q, 2 * px + py, px, py).wait_recv()
        for cp in sends:
            cp.wait_send()
        for cp in local:
            cp.wait()

    any_spec = pl.BlockSpec(memory_space=pl.ANY)
    return pl.pallas_call(
        body, out_shape=tuple(jax.ShapeDtypeStruct(a.shape, a.dtype) for a in arrays), in_specs=[any_spec] * n,
        out_specs=(any_spec,) * n,
        scratch_shapes=[pltpu.SemaphoreType.DMA((n, N_CHIP - 1)), pltpu.SemaphoreType.DMA((n, N_CHIP - 1)),
                        pltpu.SemaphoreType.DMA((n,))],
        name="grad_chip_exchange")(*arrays)


REDUCE_BLOCK_BYTES = 512 * 1024


def _pair_add(a, b):
    p, r, c = a.shape
    tr = _reduce_rows(r, c)

    def body(a_ref, b_ref, o_ref):
        o_ref[...] = (a_ref[...].astype(F32) + b_ref[...].astype(F32)).astype(o_ref.dtype)

    blk = pl.BlockSpec((None, tr, c), lambda s, i: (s, i, 0))
    return pl.pallas_call(
        body, out_shape=jax.ShapeDtypeStruct(a.shape, a.dtype), grid=(p, r // tr), in_specs=[blk, blk], out_specs=blk,
        compiler_params=_cparams(2), name="grad_pair_add")(a, b)


def _reduce_rows(r, c):
    best = None
    for t in range(16, r + 1, 16):
        if r % t == 0 and t * c * 4 <= REDUCE_BLOCK_BYTES:
            best = t
    assert best is not None, (r, c)
    return best


def _reduce_adamw(parts, w, mom, vel):
    n_parts, r, c = parts.shape
    tr = _reduce_rows(r, c)
    c1 = 1.0 - ADAM_B1 ** ADAM_STEP
    c2 = 1.0 - ADAM_B2 ** ADAM_STEP

    def body(p_ref, w_ref, m_ref, v_ref, g_ref, d_ref, mo_ref, vo_ref):
        g = p_ref[0].astype(F32)
        for s in range(1, n_parts):
            g = g + p_ref[s].astype(F32)
        mn = ADAM_B1 * m_ref[...] + (1.0 - ADAM_B1) * g
        vn = ADAM_B2 * v_ref[...] + (1.0 - ADAM_B2) * (g * g)
        m_hat = mn / c1
        v_hat = vn / c2
        g_ref[...] = g
        d_ref[...] = -ADAM_LR * (m_hat / (jnp.sqrt(v_hat) + ADAM_EPS) + ADAM_WD * w_ref[...])
        mo_ref[...] = mn
        vo_ref[...] = vn

    blk = pl.BlockSpec((tr, c), lambda i: (i, 0))
    shp = jax.ShapeDtypeStruct((r, c), F32)
    return pl.pallas_call(
        body, out_shape=(shp, shp, shp, shp), grid=(r // tr,),
        in_specs=[pl.BlockSpec((n_parts, tr, c), lambda i: (0, i, 0)), blk, blk, blk], out_specs=(blk, blk, blk, blk),
        compiler_params=_cparams(1), name="reduce_adamw")(parts, w, mom, vel)


def _pack_rows(pieces, width, row_multiple, dtype):
    flat = jnp.concatenate([p.astype(dtype).reshape(-1) for p in pieces])
    rows = -(-flat.shape[0] // (width * row_multiple)) * row_multiple
    return jnp.pad(flat, (0, rows * width - flat.shape[0])).reshape(rows, width)


def _unshard(gathered, axis):
    moved = jnp.moveaxis(gathered, 0, axis)
    shape = list(moved.shape)
    shape[axis:axis + 2] = [shape[axis] * shape[axis + 1]]
    return moved.reshape(shape)


def _to_slots(full, axis):
    shape = list(full.shape)
    shape[axis:axis + 1] = [N_DEV, shape[axis] // N_DEV]
    return jnp.moveaxis(full.reshape(shape), axis, 0)


def _core_slots(full, axis, core):
    shape = list(full.shape)
    shape[axis:axis + 1] = [N_CHIP, 2, shape[axis] // N_DEV]
    picked = lax.dynamic_index_in_dim(full.reshape(shape), core, axis + 1, keepdims=False)
    return jnp.moveaxis(picked, axis, 0)


def _block_diag(w):
    hh, d, _ = w.shape
    eye = jnp.eye(hh, dtype=w.dtype)
    return (w[:, :, None, :] * eye[:, None, :, None]).reshape(hh * d, hh * d)


def _block_diag_t(full, hh):
    d = full.shape[0] // hh
    f4 = full.reshape(hh, d, hh, d)
    return jnp.stack([f4[i, :, i, :] for i in range(hh)], axis=0)


def _pad_heads(w, width):
    r = w.shape[0]
    w3 = w.reshape(r, MLA_HEADS, width)
    return jnp.pad(w3, ((0, 0), (0, 0), (0, HEAD_PAD - width))).reshape(r, MLA_HEADS * HEAD_PAD)


def _unpad_heads(w, width):
    r = w.shape[0]
    return w.reshape(r, MLA_HEADS, HEAD_PAD)[:, :, :width].reshape(r, MLA_HEADS * width)


def kernel(x, meta_tokens, ev_norm, ev_w_in, ev_conv_a, ev_conv_b, ev_conv_b_bias, ev_gate_r_w, ev_gate_r_b, ev_gate_i_w, ev_gate_i_b, ev_lru_lambda, ev_w_out, od_norm, od_w_in, od_q_norm, od_kv_norm, od_w_uq, od_w_ukv, od_w_out, ffn_norm, ffn_w_up, ffn_conv_w, ffn_conv_b, ffn_w_down, final_norm, loss_target, m_meta_tokens, m_ev_norm, m_ev_w_in, m_ev_conv_a, m_ev_conv_b, m_ev_conv_b_bias, m_ev_gate_r_w, m_ev_gate_r_b, m_ev_gate_i_w, m_ev_gate_i_b, m_ev_lru_lambda, m_ev_w_out, m_od_norm, m_od_w_in, m_od_q_norm, m_od_kv_norm, m_od_w_uq, m_od_w_ukv, m_od_w_out, m_ffn_norm, m_ffn_w_up, m_ffn_conv_w, m_ffn_conv_b, m_ffn_w_down, m_final_norm, v_meta_tokens, v_ev_norm, v_ev_w_in, v_ev_conv_a, v_ev_conv_b, v_ev_conv_b_bias, v_ev_gate_r_w, v_ev_gate_r_b, v_ev_gate_i_w, v_ev_gate_i_b, v_ev_lru_lambda, v_ev_w_out, v_od_norm, v_od_w_in, v_od_q_norm, v_od_kv_norm, v_od_w_uq, v_od_w_ukv, v_od_w_out, v_ffn_norm, v_ffn_w_up, v_ffn_conv_w, v_ffn_conv_b, v_ffn_w_down, v_final_norm):
    given = dict(locals())
    names = [n for n, _ in PARAMS]
    axis_of = dict(PARAMS)
    w_loc = {n: given[n] for n in names}
    m_loc = {n: given["m_" + n] for n in names}
    v_loc = {n: given["v_" + n] for n in names}
    sharded = [n for n in names if axis_of[n] is not None]
    replicated = [n for n in names if axis_of[n] is None]
    small = [n for n in sharded if n not in BIG]

    nb, seq, d = x.shape
    t_real = N_META + seq
    tp = -(-t_real // ROW_TILE) * ROW_TILE
    m = nb * tp

    small_pack = _pack_rows([w_loc[n] for n in small], LANES, SUBLANES, F32)
    gathered = _all_gather([w_loc[n].astype(BF16) for n in BIG] + [small_pack])
    full = {n: w_loc[n] for n in replicated}
    for n, g in zip(BIG, gathered[:-1]):
        full[n] = _unshard(g, axis_of[n])
    flat = gathered[-1].reshape(N_DEV, -1)
    off = 0
    for n in small:
        shard = w_loc[n].shape
        size = math.prod(shard)
        full[n] = _unshard(flat[:, off:off + size].reshape((N_DEV,) + shard), axis_of[n])
        off += size

    tables = _rope_tables(tp)

    def even_params(j):
        w_out = full["ev_w_out"][j]
        return dict(norm=full["ev_norm"][j][None], w_in=full["ev_w_in"][j], conv_a=full["ev_conv_a"][j],
                    conv_b=full["ev_conv_b"][j], conv_b_bias=full["ev_conv_b_bias"][j][None],
                    gate_r=_block_diag(full["ev_gate_r_w"][j]).astype(BF16),
                    gate_i=_block_diag(full["ev_gate_i_w"][j]).astype(BF16),
                    gate_r_b=full["ev_gate_r_b"][j][None], gate_i_b=full["ev_gate_i_b"][j][None],
                    lam=full["ev_lru_lambda"][j][None], w_out=w_out, w_out_a=w_out[:LRU_WIDTH],
                    w_out_b=w_out[LRU_WIDTH:])

    def odd_params(j):
        w_ukv = full["od_w_ukv"][j].reshape(KV_LORA, MLA_HEADS, QK_NOPE + V_HEAD)
        w_uk = w_ukv[:, :, :QK_NOPE].reshape(KV_LORA, MLA_HEADS * QK_NOPE)
        w_uv = w_ukv[:, :, QK_NOPE:].reshape(KV_LORA, MLA_HEADS * V_HEAD)
        w_out = full["od_w_out"][j].reshape(MLA_HEADS, V_HEAD, d)
        w_out = jnp.pad(w_out, ((0, 0), (0, HEAD_PAD - V_HEAD), (0, 0))).reshape(MLA_HEADS * HEAD_PAD, d)
        return dict(norm=full["od_norm"][j][None], w_in=jnp.pad(full["od_w_in"][j], ((0, 0), (0, ODD_IN_PAD - ODD_IN))),
                    q_norm=full["od_q_norm"][j][None], kv_norm=full["od_kv_norm"][j][None],
                    w_uq=_pad_heads(full["od_w_uq"][j], QK_HEAD), w_uk=_pad_heads(w_uk, QK_NOPE),
                    w_uv=_pad_heads(w_uv, V_HEAD), w_out=w_out)

    def ffn_params(layer):
        w_up = full["ffn_w_up"][layer]
        return dict(norm=full["ffn_norm"][layer][None], w_up=w_up, w_up_a=w_up[:, :D_FF], w_up_g=w_up[:, D_FF:],
                    conv_w=full["ffn_conv_w"][layer], conv_b=full["ffn_conv_b"][layer][None],
                    w_down=full["ffn_w_down"][layer])

    meta = jnp.broadcast_to(full["meta_tokens"][None], (nb, N_META, d))
    h0 = jnp.concatenate([meta, x, jnp.zeros((nb, tp - t_real, d), F32)], axis=1).reshape(m, d)
    hcur = h0
    tape = []
    for layer in range(4):
        j = layer // 2
        if layer % 2 == 0:
            mp = even_params(j)
            hcur, saved = _even_fwd(hcur, mp, m, tp, nb)
        else:
            mp = odd_params(j)
            hcur, saved = _odd_fwd(hcur, mp, tables, m, tp, nb)
        fp = ffn_params(layer)
        hcur, fsaved = _ffn_fwd(hcur, fp, m, tp)
        tape.append((mp, saved, fp, fsaved))

    target = jnp.pad(loss_target, ((0, 0), (N_META, tp - t_real), (0, 0))).reshape(m, d)
    dh, d_final_norm, loss_part = _loss_head(hcur, full["final_norm"][None], target, tp, t_real)
    loss = lax.psum(loss_part[0, 0], ("x", "y", "c"))

    grads = {"final_norm": d_final_norm[0]}
    ev_g, od_g, ffn_g = [None, None], [None, None], [None] * 4
    for layer in reversed(range(4)):
        mp, saved, fp, fsaved = tape[layer]
        dh, ffn_g[layer] = _ffn_bwd(dh, fp, fsaved, m, tp)
        if layer % 2 == 0:
            dh, ev_g[layer // 2] = _even_bwd(dh, mp, saved, m, tp, nb)
        else:
            dh, od_g[layer // 2] = _odd_bwd(dh, mp, tables, saved, m, tp, nb)

    dh3 = dh.reshape(nb, tp, d)
    grad_x = dh3[:, N_META:t_real]
    grads["meta_tokens"] = _meta_grad(dh, nb, tp)

    def stack(lst, key, fn=lambda a: a):
        return jnp.stack([fn(g[key]) for g in lst], axis=0)

    grads["ev_norm"] = stack(ev_g, "norm", lambda a: a[0])
    grads["ev_w_in"] = stack(ev_g, "w_in")
    grads["ev_conv_a"] = stack(ev_g, "conv_a")
    grads["ev_conv_b"] = stack(ev_g, "conv_b")
    grads["ev_conv_b_bias"] = stack(ev_g, "conv_b_bias", lambda a: a[0])
    grads["ev_gate_r_w"] = stack(ev_g, "gate_r", lambda a: _block_diag_t(a, 8))
    grads["ev_gate_r_b"] = stack(ev_g, "gate_r_b", lambda a: a[0])
    grads["ev_gate_i_w"] = stack(ev_g, "gate_i", lambda a: _block_diag_t(a, 8))
    grads["ev_gate_i_b"] = stack(ev_g, "gate_i_b", lambda a: a[0])
    grads["ev_lru_lambda"] = stack(ev_g, "lam", lambda a: a[0])
    grads["ev_w_out"] = stack(ev_g, "w_out")
    grads["od_norm"] = stack(od_g, "norm", lambda a: a[0])
    grads["od_w_in"] = stack(od_g, "w_in", lambda a: a[:, :ODD_IN])
    grads["od_q_norm"] = stack(od_g, "q_norm", lambda a: a[0])
    grads["od_kv_norm"] = stack(od_g, "kv_norm", lambda a: a[0])
    grads["od_w_uq"] = stack(od_g, "w_uq", lambda a: _unpad_heads(a, QK_HEAD))

    def ukv(g):
        gk = g["w_uk"].reshape(KV_LORA, MLA_HEADS, HEAD_PAD)[:, :, :QK_NOPE]
        gv = g["w_uv"].reshape(KV_LORA, MLA_HEADS, HEAD_PAD)[:, :, :V_HEAD]
        return jnp.concatenate([gk, gv], axis=2).reshape(KV_LORA, MLA_HEADS * (QK_NOPE + V_HEAD))

    grads["od_w_ukv"] = jnp.stack([ukv(g) for g in od_g], axis=0)
    grads["od_w_out"] = stack(od_g, "w_out", lambda a: a.reshape(MLA_HEADS, HEAD_PAD, d)[:, :V_HEAD].reshape(-1, d))
    grads["ffn_norm"] = stack(ffn_g, "norm", lambda a: a[0])
    grads["ffn_w_up"] = stack(ffn_g, "w_up")
    grads["ffn_conv_w"] = stack(ffn_g, "conv_w")
    grads["ffn_conv_b"] = stack(ffn_g, "conv_b", lambda a: a[0])
    grads["ffn_w_down"] = stack(ffn_g, "w_down")

    order = small + replicated
    slot_parts = [_to_slots(grads[n], axis_of[n]).reshape(N_DEV, -1) for n in small]
    slot_parts += [jnp.broadcast_to(grads[n].reshape(1, -1), (N_DEV, grads[n].size)) for n in replicated]
    g_flat = jnp.concatenate(slot_parts, axis=1)
    n_flat = g_flat.shape[1]
    rows = -(-n_flat // (1024 * 128)) * 128
    g_small = jnp.pad(g_flat, ((0, 0), (0, rows * 1024 - n_flat))).reshape(N_DEV, rows, 1024)

    def rows_of(n):
        shard = w_loc[n].shape
        return (math.prod(shard[:-1]), shard[-1])

    core = lax.axis_index("c")

    def core_slots(n, which):
        return _core_slots(grads[n], axis_of[n], which).astype(BF16).reshape((N_CHIP,) + rows_of(n))

    small_by_core = jnp.swapaxes(g_small.reshape((N_CHIP, 2) + g_small.shape[1:]), 0, 1)
    mine = [core_slots(n, core) for n in BIG] + [lax.dynamic_index_in_dim(small_by_core, core, 0, keepdims=False)]
    theirs = [core_slots(n, 1 - core) for n in BIG]
    theirs.append(lax.dynamic_index_in_dim(small_by_core, 1 - core, 0, keepdims=False))
    from_sibling = _pair_exchange(theirs)
    parts = _chip_exchange([_pair_add(a, b) for a, b in zip(mine, from_sibling)])

    g_out, d_out, m_out, v_out = {}, {}, {}, {}
    for n, part in zip(BIG, parts[:-1]):
        res = _reduce_adamw(part, *[t[n].reshape(rows_of(n)) for t in (w_loc, m_loc, v_loc)])
        for out, r in zip((g_out, d_out, m_out, v_out), res):
            out[n] = r.reshape(w_loc[n].shape)

    def flat_local(tree):
        flat = jnp.concatenate([tree[n].reshape(-1) for n in order])
        return jnp.pad(flat, (0, rows * 1024 - n_flat)).reshape(rows, 1024)

    res = _reduce_adamw(parts[-1], flat_local(w_loc), flat_local(m_loc), flat_local(v_loc))
    for out, r in zip((g_out, d_out, m_out, v_out), res):
        flat = r.reshape(-1)
        off = 0
        for n in order:
            size = w_loc[n].size
            out[n] = flat[off:off + size].reshape(w_loc[n].shape)
            off += size
    return (loss, grad_x, *[g_out[n] for n in names], *[d_out[n] for n in names], *[m_out[n] for n in names],
            *[v_out[n] for n in names])
```

```python
import functools
import math

import jax
import jax.numpy as jnp
from jax import lax
from jax.experimental import pallas as pl
from jax.experimental.pallas import tpu as pltpu

F32 = jnp.float32
BF16 = jnp.bfloat16

N_DEV = 8
N_META = 16
EPS = 1e-6
LRU_C = 8.0
MLA_HEADS = 16
QK_NOPE = 64
QK_ROPE = 32
QK_HEAD = QK_NOPE + QK_ROPE
V_HEAD = 64
HEAD_PAD = 128
Q_LORA = 384
KV_LORA = 256
ODD_IN = Q_LORA + KV_LORA + QK_ROPE
ODD_IN_PAD = 768
ROPE_BASE = 10000.0
LRU_WIDTH = 512
D_FF = 2816

ADAM_LR = 0.001
ADAM_B1 = 0.9
ADAM_B2 = 0.999
ADAM_EPS = 1e-08
ADAM_WD = 0.01
ADAM_STEP = 10

ROW_TILE = 384
SUBLANES = 8
HALO_ROWS = 16
LANES = 128
VMEM_LIMIT = 48 * 1024 * 1024
NEG = -1e30

PARAMS = (
    ("meta_tokens", 1), ("ev_norm", None), ("ev_w_in", 2), ("ev_conv_a", 2), ("ev_conv_b", 2),
    ("ev_conv_b_bias", None), ("ev_gate_r_w", None), ("ev_gate_r_b", None), ("ev_gate_i_w", None),
    ("ev_gate_i_b", None), ("ev_lru_lambda", None), ("ev_w_out", 1), ("od_norm", 1), ("od_w_in", 1),
    ("od_q_norm", 1), ("od_kv_norm", 1), ("od_w_uq", 2), ("od_w_ukv", 2), ("od_w_out", 1),
    ("ffn_norm", None), ("ffn_w_up", 2), ("ffn_conv_w", 2), ("ffn_conv_b", None), ("ffn_w_down", 1),
    ("final_norm", None),
)
BIG = ("ev_w_in", "ev_w_out", "od_w_in", "od_w_uq", "od_w_ukv", "od_w_out", "ffn_w_up", "ffn_w_down")


def _cparams(n_grid):
    return pltpu.CompilerParams(dimension_semantics=("arbitrary",) * n_grid, vmem_limit_bytes=VMEM_LIMIT)


def _pick(dim, target):
    if dim <= target:
        return dim
    best = None
    for t in range(LANES, target + 1, LANES):
        if dim % t == 0:
            best = t
    assert best is not None, (dim, target)
    return best


MATMUL_VMEM_BUDGET = 38 * 1024 * 1024
HBM_BYTES_PER_US = 3.0e6
MXU_FLOPS_PER_US = 9.0e8
ACC_BYTES_PER_US = 7.6e6
GRID_STEP_US = 0.35


def _tile_candidates(dim):
    return [t for t in range(LANES, dim + 1, LANES) if dim % t == 0] or [dim]


def _matmul_tiles(m, n, k, sa, sb, so, sr, transposed_lhs):
    best, best_cost = None, None
    for tm in _tile_candidates(m):
        for tn in _tile_candidates(n):
            for tk in _tile_candidates(k):
                nk = k // tk
                vmem = 2 * (tm * tk * sa + tk * tn * sb) + tm * tn * ((4 if nk > 1 else 0) + 2 * so + 2 * sr)
                vmem += (tm * tk * 2 if sa > 2 else 0) + (tk * tn * 2 if sb > 2 else 0) + tm * tn * 4
                if vmem > MATMUL_VMEM_BUDGET:
                    continue
                steps = (m // tm) * (n // tn) * nk
                traffic = m * k * sa * (n // tn) + k * n * sb * (m // tm) + m * n * (so + sr)
                acc_us = steps * tm * tn * 4 / ACC_BYTES_PER_US if nk > 1 else 0.0
                busy_us = 0.0 if transposed_lhs else 2.0 * m * n * k / MXU_FLOPS_PER_US + acc_us
                cost = max(traffic / HBM_BYTES_PER_US, busy_us) + steps * GRID_STEP_US
                if best_cost is None or cost < best_cost:
                    best, best_cost = (tm, tn, tk), cost
    assert best is not None, (m, n, k)
    return best


def _matmul(a, b, mode, out_dtype=F32, residual=None, name="mm", b_col_off=0):
    if mode == "nn":
        (m, k), (k2, n) = a.shape, b.shape
    elif mode == "nt":
        (m, k), n = a.shape, b.shape[0]
        k2 = k if b_col_off or b.shape[1] > k else b.shape[1]
    else:
        (k, m), (k2, n) = a.shape, b.shape
    assert k == k2, (a.shape, b.shape, mode)
    tm, tn, tk = _matmul_tiles(m, n, k, a.dtype.itemsize, b.dtype.itemsize, jnp.dtype(out_dtype).itemsize,
                               0 if residual is None else residual.dtype.itemsize, mode == "tn")
    nk = k // tk
    if mode == "tn":
        a_spec = pl.BlockSpec((tk, tm), lambda i, j, kk: (kk, i))
        dims = (((0,), (0,)), ((), ()))
    else:
        a_spec = pl.BlockSpec((tm, tk), lambda i, j, kk: (i, kk))
        dims = (((1,), (1 if mode == "nt" else 0,)), ((), ()))
    if mode == "nt":
        assert b_col_off % tk == 0, (b_col_off, tk)
        b_spec = pl.BlockSpec((tn, tk), lambda i, j, kk: (j, kk + b_col_off // tk))
    else:
        b_spec = pl.BlockSpec((tk, tn), lambda i, j, kk: (kk, j))
    o_spec = pl.BlockSpec((tm, tn), lambda i, j, kk: (i, j))
    has_res = residual is not None

    def body(*refs):
        a_ref, b_ref = refs[:2]
        r_ref = refs[2] if has_res else None
        o_ref = refs[3] if has_res else refs[2]
        part = lax.dot_general(a_ref[...].astype(BF16), b_ref[...].astype(BF16), dims, preferred_element_type=F32)

        def finish(out):
            if has_res:
                out = out + r_ref[...].astype(F32)
            o_ref[...] = out.astype(o_ref.dtype)

        if nk == 1:
            finish(part)
            return
        acc_ref = refs[-1]
        kk = pl.program_id(2)

        @pl.when(kk == 0)
        def _():
            acc_ref[...] = part

        @pl.when(kk > 0)
        def _():
            acc_ref[...] += part

        @pl.when(kk == nk - 1)
        def _():
            finish(acc_ref[...])

    in_specs = [a_spec, b_spec] + ([o_spec] if has_res else [])
    args = (a, b) + ((residual,) if has_res else ())
    return pl.pallas_call(
        body, out_shape=jax.ShapeDtypeStruct((m, n), out_dtype), grid=(m // tm, n // tn, nk),
        in_specs=in_specs, out_specs=o_spec, scratch_shapes=[pltpu.VMEM((tm, tn), F32)] if nk > 1 else [],
        compiler_params=_cparams(3), name=name)(*args)


def _rms_fwd(x, g, name):
    m, c = x.shape
    tm = _pick(m, ROW_TILE)

    def body(x_ref, g_ref, o_ref):
        xf = x_ref[...].astype(F32)
        r = lax.rsqrt(jnp.mean(xf * xf, axis=-1, keepdims=True) + EPS)
        o_ref[...] = (xf * r * g_ref[...]).astype(o_ref.dtype)

    return pl.pallas_call(
        body, out_shape=jax.ShapeDtypeStruct((m, c), BF16), grid=(m // tm,),
        in_specs=[pl.BlockSpec((tm, c), lambda i: (i, 0)), pl.BlockSpec((1, c), lambda i: (0, 0))],
        out_specs=pl.BlockSpec((tm, c), lambda i: (i, 0)), compiler_params=_cparams(1), name=name)(x, g)


def _rms_bwd(x, g, dy, residual, name):
    m, c = x.shape
    tm = _pick(m, ROW_TILE)
    has_res = residual is not None

    def body(*refs):
        if has_res:
            x_ref, g_ref, dy_ref, r_ref, dx_ref, dg_ref = refs
        else:
            x_ref, g_ref, dy_ref, dx_ref, dg_ref = refs
        xf = x_ref[...].astype(F32)
        dyf = dy_ref[...].astype(F32)
        r = lax.rsqrt(jnp.mean(xf * xf, axis=-1, keepdims=True) + EPS)
        xn = xf * r
        dyg = dyf * g_ref[...]
        dx = r * (dyg - xn * jnp.mean(dyg * xn, axis=-1, keepdims=True))
        if has_res:
            dx = dx + r_ref[...]
        dx_ref[...] = dx

        @pl.when(pl.program_id(0) == 0)
        def _():
            dg_ref[...] = jnp.zeros_like(dg_ref)

        dg_ref[...] += jnp.sum(dyf * xn, axis=0, keepdims=True)

    row = pl.BlockSpec((tm, c), lambda i: (i, 0))
    vec = pl.BlockSpec((1, c), lambda i: (0, 0))
    in_specs = [row, vec, row] + ([row] if has_res else [])
    args = (x, g, dy) + ((residual,) if has_res else ())
    return pl.pallas_call(
        body, out_shape=(jax.ShapeDtypeStruct((m, c), F32), jax.ShapeDtypeStruct((1, c), F32)), grid=(m // tm,),
        in_specs=in_specs, out_specs=(row, vec), compiler_params=_cparams(1), name=name)(*args)


def _chan_call(name, fn, m, tp, tc, ncol, row_ins=(), prev_ins=(), next_ins=(), chan_ins=(), row_outs=(),
               red_outs=(), row_split=1):
    tm = _pick(tp, ROW_TILE) // row_split
    tps = tp // tm
    nrow = m // tm
    halo_blocks = tm // HALO_ROWS
    last_halo = m // HALO_ROWS - 1
    n_in = len(row_ins) + len(prev_ins) + len(next_ins) + len(chan_ins)
    n_r, n_p, n_n = len(row_ins), len(prev_ins), len(next_ins)

    def body(*refs):
        i = pl.program_id(1)
        pos = lax.rem(i, tps)
        at_start = pos == 0
        at_end = pos == tps - 1
        rows = [r[...].astype(F32) for r in refs[:n_r]]
        prevs = [jnp.where(at_start, 0.0, r[...].astype(F32)[SUBLANES:]) for r in refs[n_r:n_r + n_p]]
        nexts = [jnp.where(at_end, 0.0, r[...].astype(F32)[:SUBLANES]) for r in refs[n_r + n_p:n_r + n_p + n_n]]
        chans = [r[...] for r in refs[n_r + n_p + n_n:n_in]]
        out_refs = refs[n_in:n_in + len(row_outs)]
        red_refs = refs[n_in + len(row_outs):]
        row_vals, red_vals = fn(rows, prevs, nexts, chans)
        for ref, val in zip(out_refs, row_vals):
            ref[...] = val.astype(ref.dtype)
        if red_refs:
            @pl.when(i == 0)
            def _():
                for ref in red_refs:
                    ref[...] = jnp.zeros_like(ref)

            for ref, val in zip(red_refs, red_vals):
                ref[...] += val

    in_specs, args = [], []
    for arr, off in row_ins:
        in_specs.append(pl.BlockSpec((tm, tc), lambda j, i, off=off: (i, j + off)))
        args.append(arr)
    for arr, off in prev_ins:
        in_specs.append(pl.BlockSpec((HALO_ROWS, tc),
                                     lambda j, i, off=off: (jnp.maximum(i * halo_blocks - 1, 0), j + off)))
        args.append(arr)
    for arr, off in next_ins:
        in_specs.append(pl.BlockSpec((HALO_ROWS, tc),
                                     lambda j, i, off=off: (jnp.minimum((i + 1) * halo_blocks, last_halo), j + off)))
        args.append(arr)
    for arr, off in chan_ins:
        in_specs.append(pl.BlockSpec((arr.shape[0], tc), lambda j, i, off=off: (0, j + off)))
        args.append(arr)
    out_shape, out_specs = [], []
    for (dt,) in row_outs:
        out_shape.append(jax.ShapeDtypeStruct((m, ncol * tc), dt))
        out_specs.append(pl.BlockSpec((tm, tc), lambda j, i: (i, j)))
    for (k,) in red_outs:
        out_shape.append(jax.ShapeDtypeStruct((k, ncol * tc), F32))
        out_specs.append(pl.BlockSpec((k, tc), lambda j, i: (0, j)))
    return pl.pallas_call(
        body, out_shape=tuple(out_shape), grid=(ncol, nrow), in_specs=in_specs, out_specs=tuple(out_specs),
        compiler_params=_cparams(2), name=name)(*args)


def _shift_down(x, prev8, s):
    if s == 0:
        return x
    tm, tc = x.shape
    groups = tm // SUBLANES
    xr = pltpu.roll(x.reshape(groups, SUBLANES, tc), s, 1)
    before = jnp.concatenate([pltpu.roll(prev8, s, 0)[None], xr[:-1]], axis=0)
    rid = lax.broadcasted_iota(jnp.int32, xr.shape, 1)
    return jnp.where(rid < s, before, xr).reshape(tm, tc)


def _shift_up(x, next8, s):
    if s == 0:
        return x
    tm, tc = x.shape
    groups = tm // SUBLANES
    xr = pltpu.roll(x.reshape(groups, SUBLANES, tc), SUBLANES - s, 1)
    after = jnp.concatenate([xr[1:], pltpu.roll(next8, SUBLANES - s, 0)[None]], axis=0)
    rid = lax.broadcasted_iota(jnp.int32, xr.shape, 1)
    return jnp.where(rid >= SUBLANES - s, after, xr).reshape(tm, tc)


def _taps(x, prev8, kw):
    return [_shift_down(x, prev8, kw - 1 - k) for k in range(kw)]


def _conv_taps(taps, w):
    y = w[0:1, :] * taps[0]
    for k in range(1, len(taps)):
        y = y + w[k:k + 1, :] * taps[k]
    return y


def _conv_dw_taps(dy, taps):
    shape = (SUBLANES, dy.shape[1])
    rid = lax.broadcasted_iota(jnp.int32, shape, 0)
    out = jnp.zeros(shape, F32)
    for k, tap in enumerate(taps):
        out = out + jnp.where(rid == k, jnp.sum(dy * tap, axis=0, keepdims=True), 0.0)
    return out


def _conv_fwd(x, prev8, w):
    return _conv_taps(_taps(x, prev8, w.shape[0]), w)


def _conv_dw(dy, x, prev8, kw):
    return _conv_dw_taps(dy, _taps(x, prev8, kw))


def _conv_dx(dy, next8, w):
    kw = w.shape[0]
    dx = w[kw - 1:kw, :] * dy
    for k in range(kw - 1):
        dx = dx + w[k:k + 1, :] * _shift_up(dy, next8, kw - 1 - k)
    return dx


def _sigmoid(x):
    return 1.0 / (1.0 + jnp.exp(-x))


def _expm1(x):
    series = x * (1.0 + x * 0.5 * (1.0 + x * (1.0 / 3.0) * (1.0 + x * 0.25 * (1.0 + x * 0.2))))
    return jnp.where(jnp.abs(x) < 0.3, series, jnp.exp(x) - 1.0)


def _softplus_neg(lam):
    e = jnp.exp(-jnp.abs(lam))
    log1p = jnp.where(e < 1e-2, e * (1.0 - e * (0.5 - e * (1.0 / 3.0))), jnp.log(1.0 + e))
    return jnp.maximum(-lam, 0.0) + log1p


GELU_C = math.sqrt(2.0 / math.pi)


def _gelu(x):
    return 0.5 * x * (1.0 + jnp.tanh(GELU_C * (x + 0.044715 * x * x * x)))


def _gelu_grad(x):
    t = jnp.tanh(GELU_C * (x + 0.044715 * x * x * x))
    return 0.5 * (1.0 + t) + 0.5 * x * (1.0 - t * t) * GELU_C * (1.0 + 3.0 * 0.044715 * x * x)


FFN_COL_TILE = 1408


def _ffn_fwd(x, p, m, tp):
    h = _rms_fwd(x, p["norm"], "ffn_norm")
    u = _matmul(h, p["w_up"], "nn", BF16, name="ffn_up")
    tc = FFN_COL_TILE
    ncol = D_FF // tc

    def gate(rows, prevs, nexts, chans):
        ua, ug = rows
        wa, wg, ba, bg = chans
        a = _conv_fwd(ua, prevs[0], wa) + ba
        g = _conv_fwd(ug, prevs[1], wg) + bg
        return [a * _sigmoid(a) * g, a, g], []

    z, a_act, g_act = _chan_call(
        "ffn_gate", gate, m, tp, tc, ncol, row_ins=[(u, 0), (u, ncol)], prev_ins=[(u, 0), (u, ncol)],
        chan_ins=[(p["conv_w"], 0), (p["conv_w"], ncol), (p["conv_b"], 0), (p["conv_b"], ncol)],
        row_outs=[(BF16,), (BF16,), (BF16,)])
    out = _matmul(z, p["w_down"], "nn", F32, residual=x, name="ffn_down")
    return out, (x, h, u, z, a_act, g_act)


def _ffn_bwd(dout, p, saved, m, tp):
    x, h, u, z, a_act, g_act = saved
    tc = FFN_COL_TILE
    ncol = D_FF // tc
    dz = _matmul(dout, p["w_down"], "nt", F32, name="ffn_down_dx")
    d_w_down = _matmul(z, dout, "tn", F32, name="ffn_down_dw")

    def act_bwd(a, g, dzv):
        sg = _sigmoid(a)
        return dzv * g * (sg * (1.0 + a * (1.0 - sg))), dzv * a * sg

    def gate_bwd(rows, prevs, nexts, chans):
        ua, ug, dzv, a, g = rows
        da, dg = act_bwd(a, g, dzv)
        da_next, dg_next = act_bwd(nexts[1], nexts[2], nexts[0])
        ups_a = [_shift_up(da, da_next, 2 - k) for k in range(3)]
        ups_g = [_shift_up(dg, dg_next, 2 - k) for k in range(3)]
        return ([_conv_taps(ups_a, chans[0]), _conv_taps(ups_g, chans[1])],
                [_conv_dw_taps(ua, ups_a), _conv_dw_taps(ug, ups_g),
                 jnp.sum(da, axis=0, keepdims=True), jnp.sum(dg, axis=0, keepdims=True)])

    dua, dug, dcw_a, dcw_g, dcb_a, dcb_g = _chan_call(
        "ffn_gate_bwd", gate_bwd, m, tp, tc, ncol,
        row_ins=[(u, 0), (u, ncol), (dz, 0), (a_act, 0), (g_act, 0)], next_ins=[(dz, 0), (a_act, 0), (g_act, 0)],
        chan_ins=[(p["conv_w"], 0), (p["conv_w"], ncol)],
        row_outs=[(BF16,), (BF16,)], red_outs=[(SUBLANES,), (SUBLANES,), (1,), (1,)], row_split=2)
    d_w_up = jnp.concatenate([_matmul(h, dua, "tn", F32, name="ffn_up_dw_a"),
                              _matmul(h, dug, "tn", F32, name="ffn_up_dw_g")], axis=1)
    dh = _matmul(dua, p["w_up"], "nt", F32, name="ffn_up_dx_a")
    dh = _matmul(dug, p["w_up"], "nt", F32, residual=dh, name="ffn_up_dx_g", b_col_off=D_FF)
    dx, d_norm = _rms_bwd(x, p["norm"], dh, dout, "ffn_norm_bwd")
    d_conv_w = jnp.concatenate([dcw_a[:3], dcw_g[:3]], axis=1)
    d_conv_b = jnp.concatenate([dcb_a, dcb_g], axis=1)
    return dx, dict(norm=d_norm, w_up=d_w_up, conv_w=d_conv_w, conv_b=d_conv_b, w_down=d_w_down)


def _to_scan(x, nb, tp):
    return x.reshape(nb, tp, LRU_WIDTH // LANES, LANES).transpose(1, 0, 2, 3).reshape(tp, -1, LANES)


def _from_scan(x, nb, tp):
    return x.reshape(tp, nb, LRU_WIDTH // LANES, LANES).transpose(1, 0, 2, 3).reshape(nb * tp, LRU_WIDTH)


def _scan_fwd(a, u):
    t_len, s, _ = a.shape
    tc = _pick(t_len, 640)
    blk = pl.BlockSpec((tc, s, LANES), lambda i: (i, 0, 0))

    def body(a_ref, u_ref, h_ref, carry):
        @pl.when(pl.program_id(0) == 0)
        def _():
            carry[...] = jnp.zeros_like(carry)

        def step(t, h):
            h = a_ref[t] * h + u_ref[t]
            h_ref[t] = h
            return h

        carry[...] = lax.fori_loop(0, tc, step, carry[...], unroll=8)

    return pl.pallas_call(
        body, out_shape=jax.ShapeDtypeStruct(a.shape, F32), grid=(t_len // tc,), in_specs=[blk, blk], out_specs=blk,
        scratch_shapes=[pltpu.VMEM((s, LANES), F32)], compiler_params=_cparams(1), name="lru_scan")(a, u)


def _scan_bwd(dh, a, h_prev):
    t_len, s, _ = a.shape
    tc = _pick(t_len, 640)
    nb = t_len // tc
    blk = pl.BlockSpec((tc, s, LANES), lambda i: (nb - 1 - i, 0, 0))

    def body(dh_ref, a_ref, hp_ref, du_ref, da_ref, carry):
        @pl.when(pl.program_id(0) == 0)
        def _():
            carry[...] = jnp.zeros_like(carry)

        def step(k, c):
            t = tc - 1 - k
            d = dh_ref[t] + c
            du_ref[t] = d
            da_ref[t] = d * hp_ref[t]
            return a_ref[t] * d

        carry[...] = lax.fori_loop(0, tc, step, carry[...], unroll=8)

    shp = jax.ShapeDtypeStruct(a.shape, F32)
    return pl.pallas_call(
        body, out_shape=(shp, shp), grid=(nb,), in_specs=[blk, blk, blk], out_specs=(blk, blk),
        scratch_shapes=[pltpu.VMEM((s, LANES), F32)], compiler_params=_cparams(1), name="lru_scan_bwd")(dh, a, h_prev)


def _lru_gates(xc, zr, zi, r_b, i_b, lam):
    r = _sigmoid(zr + r_b)
    ig = _sigmoid(zi + i_b)
    sp = _softplus_neg(lam)
    log_a = -LRU_C * r * sp
    a = jnp.exp(log_a)
    mult = jnp.sqrt(-_expm1(2.0 * log_a))
    return r, ig, sp, a, mult


def _even_fwd(x, p, m, tp, nb):
    c = LRU_WIDTH
    h = _rms_fwd(x, p["norm"], "ev_norm")
    u = _matmul(h, p["w_in"], "nn", F32, name="ev_in")

    def pre(rows, prevs, nexts, chans):
        gb, gc, xa, xb = rows
        wa, wb, bias = chans
        pa = gc * xa
        ya = gb * _conv_fwd(pa, prevs[0] * prevs[1], wa)
        xc = _conv_fwd(xb, prevs[2], wb) + bias
        return [ya, xc], []

    ya, xc = _chan_call("ev_pre", pre, m, tp, c, 1, row_ins=[(u, 0), (u, 1), (u, 2), (u, 3)],
                        prev_ins=[(u, 1), (u, 2), (u, 3)],
                        chan_ins=[(p["conv_a"], 0), (p["conv_b"], 0), (p["conv_b_bias"], 0)],
                        row_outs=[(BF16,), (F32,)])
    zr = _matmul(xc, p["gate_r"], "nn", F32, name="ev_gate_r")
    zi = _matmul(xc, p["gate_i"], "nn", F32, name="ev_gate_i")

    def lru_in(rows, prevs, nexts, chans):
        xcv, zrv, ziv = rows
        r, ig, sp, a, mult = _lru_gates(xcv, zrv, ziv, *chans)
        return [a, mult * (ig * xcv)], []

    a, uu = _chan_call("ev_lru_in", lru_in, m, tp, c, 1, row_ins=[(xc, 0), (zr, 0), (zi, 0)],
                       chan_ins=[(p["gate_r_b"], 0), (p["gate_i_b"], 0), (p["lam"], 0)],
                       row_outs=[(F32,), (F32,)])
    a_s = _to_scan(a, nb, tp)
    hs_s = _scan_fwd(a_s, _to_scan(uu, nb, tp))
    hs = _from_scan(hs_s, nb, tp)

    def post(rows, prevs, nexts, chans):
        gate, hv = rows
        return [_gelu(gate) * hv], []

    (yb,) = _chan_call("ev_post", post, m, tp, c, 1, row_ins=[(u, 4), (hs, 0)], row_outs=[(BF16,)])
    out = _matmul(ya, p["w_out_a"], "nn", F32, residual=x, name="ev_out_a")
    out = _matmul(yb, p["w_out_b"], "nn", F32, residual=out, name="ev_out_b")
    return out, (x, h, u, ya, xc, zr, zi, a_s, hs_s, hs, yb)


def _even_bwd(dout, p, saved, m, tp, nb):
    c = LRU_WIDTH
    x, h, u, ya, xc, zr, zi, a_s, hs_s, hs, yb = saved
    dy = _matmul(dout, p["w_out"], "nt", F32, name="ev_out_dx")
    d_w_out = jnp.concatenate([_matmul(ya, dout, "tn", F32, name="ev_out_dw_a"),
                               _matmul(yb, dout, "tn", F32, name="ev_out_dw_b")], axis=0)

    def post_bwd(rows, prevs, nexts, chans):
        dyb, gate, hv = rows
        return [dyb * hv * _gelu_grad(gate), dyb * _gelu(gate)], []

    dgate, dhs = _chan_call("ev_post_bwd", post_bwd, m, tp, c, 1, row_ins=[(dy, 1), (u, 4), (hs, 0)],
                            row_outs=[(F32,), (F32,)])
    h_prev = jnp.concatenate([jnp.zeros_like(hs_s[:1]), hs_s[:-1]], axis=0)
    du_s, da_s = _scan_bwd(_to_scan(dhs, nb, tp), a_s, h_prev)
    du = _from_scan(du_s, nb, tp)
    da = _from_scan(da_s, nb, tp)

    def lru_in_bwd(rows, prevs, nexts, chans):
        duv, dav, xcv, zrv, ziv = rows
        r, ig, sp, a, mult = _lru_gates(xcv, zrv, ziv, *chans)
        dxc = duv * mult * ig
        dig = duv * mult * xcv
        dmult = duv * ig * xcv
        dlog_a = dav * a - dmult * (a * a) / jnp.maximum(mult, 1e-30)
        dr = dlog_a * (-LRU_C * sp)
        dzr = dr * r * (1.0 - r)
        dzi = dig * ig * (1.0 - ig)
        dsp = jnp.sum(dlog_a * (-LRU_C * r), axis=0, keepdims=True)
        dlam = -dsp * _sigmoid(-chans[2])
        return ([dzr, dzi, dxc],
                [jnp.sum(dzr, axis=0, keepdims=True), jnp.sum(dzi, axis=0, keepdims=True), dlam])

    dzr, dzi, dxc, d_r_b, d_i_b, d_lam = _chan_call(
        "ev_lru_in_bwd", lru_in_bwd, m, tp, c, 1, row_ins=[(du, 0), (da, 0), (xc, 0), (zr, 0), (zi, 0)],
        chan_ins=[(p["gate_r_b"], 0), (p["gate_i_b"], 0), (p["lam"], 0)],
        row_outs=[(F32,), (F32,), (F32,)], red_outs=[(1,), (1,), (1,)])
    d_gate_r = _matmul(xc, dzr, "tn", F32, name="ev_gate_r_dw")
    d_gate_i = _matmul(xc, dzi, "tn", F32, name="ev_gate_i_dw")
    dxc = _matmul(dzr, p["gate_r"], "nt", F32, residual=dxc, name="ev_gate_r_dx")
    dxc = _matmul(dzi, p["gate_i"], "nt", F32, residual=dxc, name="ev_gate_i_dx")

    def conv_b_bwd(rows, prevs, nexts, chans):
        dxcv, xb = rows
        return ([_conv_dx(dxcv, nexts[0], chans[0])],
                [_conv_dw(dxcv, xb, prevs[0], 4), jnp.sum(dxcv, axis=0, keepdims=True)])

    dxb, d_conv_b, d_bias = _chan_call(
        "ev_conv_b_bwd", conv_b_bwd, m, tp, c, 1, row_ins=[(dxc, 0), (u, 3)], prev_ins=[(u, 3)], next_ins=[(dxc, 0)],
        chan_ins=[(p["conv_b"], 0)], row_outs=[(F32,)], red_outs=[(SUBLANES,), (1,)])

    def mix_a_bwd(rows, prevs, nexts, chans):
        dya, gb, gc, xa = rows
        (wa,) = chans
        taps = _taps(gc * xa, prevs[0] * prevs[1], 3)
        ca = _conv_taps(taps, wa)
        dca = dya * gb
        dpa = _conv_dx(dca, nexts[0] * nexts[1], wa)
        return [dya * ca, dpa * xa, dpa * gc], [_conv_dw_taps(dca, taps)]

    dgb, dgc, dxa, d_conv_a = _chan_call(
        "ev_mix_a_bwd", mix_a_bwd, m, tp, c, 1, row_ins=[(dy, 0), (u, 0), (u, 1), (u, 2)],
        prev_ins=[(u, 1), (u, 2)], next_ins=[(dy, 0), (u, 0)], chan_ins=[(p["conv_a"], 0)],
        row_outs=[(F32,), (F32,), (F32,)], red_outs=[(SUBLANES,)])
    du_all = jnp.concatenate([dgb, dgc, dxa, dxb, dgate], axis=1)
    d_w_in = _matmul(h, du_all, "tn", F32, name="ev_in_dw")
    dh = _matmul(du_all, p["w_in"], "nt", F32, name="ev_in_dx")
    dx, d_norm = _rms_bwd(x, p["norm"], dh, dout, "ev_norm_bwd")
    return dx, dict(norm=d_norm, w_in=d_w_in, conv_a=d_conv_a[:3], conv_b=d_conv_b[:4], conv_b_bias=d_bias,
                    gate_r=d_gate_r, gate_r_b=d_r_b, gate_i=d_gate_i, gate_i_b=d_i_b, lam=d_lam, w_out=d_w_out)


def _rope_tables(tp):
    pos = jnp.arange(tp, dtype=F32)
    inv_freq = ROPE_BASE ** (-jnp.arange(0, QK_ROPE, 2, dtype=F32) / QK_ROPE)
    ang = pos[:, None] * inv_freq[None, :]
    cos, sin = jnp.cos(ang), jnp.sin(ang)
    half = QK_ROPE // 2
    one = jnp.ones((tp, QK_NOPE), F32)
    z64 = jnp.zeros((tp, QK_NOPE), F32)
    zh = jnp.zeros((tp, half), F32)
    zt = jnp.zeros((tp, HEAD_PAD - QK_HEAD), F32)
    c_tab = jnp.concatenate([one, cos, cos, zt], axis=1)
    s_lo = jnp.concatenate([z64, -sin, zh, zt], axis=1)
    s_hi = jnp.concatenate([z64, zh, sin, zt], axis=1)
    return c_tab, s_lo, s_hi


def _rope(v, c_tab, s_lo, s_hi):
    half = QK_ROPE // 2
    return v * c_tab + pltpu.roll(v, HEAD_PAD - half, 1) * s_lo + pltpu.roll(v, half, 1) * s_hi


def _rope_t(dv, c_tab, s_lo, s_hi):
    half = QK_ROPE // 2
    return dv * c_tab + pltpu.roll(dv * s_lo, half, 1) + pltpu.roll(dv * s_hi, HEAD_PAD - half, 1)


def _rope_call(name, fn, m, tp, ins, tables, out_dtype, shared_pre=None):
    tm = _pick(tp, ROW_TILE)
    tps = tp // tm
    n = len(ins)
    width = MLA_HEADS * HEAD_PAD

    def body(*refs):
        tabs = [r[...] for r in refs[n:n + 3]]
        shared = [None if fc is None else shared_pre(refs[a][...].astype(F32), *tabs) for a, (_, fc) in enumerate(ins)]
        for hh in range(MLA_HEADS):
            lanes = slice(hh * HEAD_PAD, (hh + 1) * HEAD_PAD)
            vals = [refs[a][:, lanes].astype(F32) if shared[a] is None else shared[a] for a in range(n)]
            refs[n + 3][:, lanes] = fn(*vals, *tabs).astype(out_dtype)

    in_specs, args = [], []
    for arr, fixed_col in ins:
        if fixed_col is None:
            in_specs.append(pl.BlockSpec((tm, width), lambda i: (i, 0)))
        else:
            in_specs.append(pl.BlockSpec((tm, HEAD_PAD), lambda i, fc=fixed_col: (i, fc)))
        args.append(arr)
    for tab in tables:
        in_specs.append(pl.BlockSpec((tm, HEAD_PAD), lambda i: (lax.rem(i, tps), 0)))
        args.append(tab)
    return pl.pallas_call(
        body, out_shape=jax.ShapeDtypeStruct((m, width), out_dtype), grid=(m // tm,),
        in_specs=in_specs, out_specs=pl.BlockSpec((tm, width), lambda i: (i, 0)),
        compiler_params=_cparams(1), name=name)(*args)


def _rope_k_bwd(dk, tables, m, tp):
    tm = _pick(tp, ROW_TILE)
    tps = tp // tm

    def body(dk_ref, c_ref, lo_ref, hi_ref, o_ref):
        acc = dk_ref[:, 0:HEAD_PAD].astype(F32)
        for hh in range(1, MLA_HEADS):
            acc = acc + dk_ref[:, hh * HEAD_PAD:(hh + 1) * HEAD_PAD].astype(F32)
        d = pltpu.roll(_rope_t(acc, c_ref[...], lo_ref[...], hi_ref[...]), QK_NOPE, 1)
        lane = lax.broadcasted_iota(jnp.int32, d.shape, 1)
        o_ref[...] = jnp.where(lane < QK_ROPE, d, 0.0)

    tab = pl.BlockSpec((tm, HEAD_PAD), lambda i: (lax.rem(i, tps), 0))
    return pl.pallas_call(
        body, out_shape=jax.ShapeDtypeStruct((m, HEAD_PAD), F32), grid=(m // tm,),
        in_specs=[pl.BlockSpec((tm, MLA_HEADS * HEAD_PAD), lambda i: (i, 0)), tab, tab, tab],
        out_specs=pl.BlockSpec((tm, HEAD_PAD), lambda i: (i, 0)), compiler_params=_cparams(1),
        name="od_rope_k_bwd")(dk, *tables)


def _causal_mask(row0, col0, shape):
    rows = row0 + lax.broadcasted_iota(jnp.int32, shape, 0)
    cols = col0 + lax.broadcasted_iota(jnp.int32, shape, 1)
    return cols <= rows


NT = (((1,), (1,)), ((), ()))
TN = (((0,), (0,)), ((), ()))
HEADS_PER_STEP = 2
HEAD_STEPS = MLA_HEADS // HEADS_PER_STEP
STEP_LANES = HEADS_PER_STEP * HEAD_PAD


def _flash_fwd(q, k, v, nb, tp):
    tq = _pick(tp, ROW_TILE)
    nq = tp // tq

    def body(q_ref, k_ref, v_ref, o_ref, lse_ref):
        i = pl.program_id(2)
        qbs = [q_ref[:, hd * HEAD_PAD:(hd + 1) * HEAD_PAD] for hd in range(HEADS_PER_STEP)]

        def chunk(j, carry, masked, width=1):
            off = pl.multiple_of(j * tq, tq)
            out = []
            for hd in range(HEADS_PER_STEP):
                mx, l, acc = carry[hd]
                lanes = slice(hd * HEAD_PAD, (hd + 1) * HEAD_PAD)
                kb = k_ref[pl.ds(off, width * tq), lanes]
                vb = v_ref[pl.ds(off, width * tq), lanes]
                s = lax.dot_general(qbs[hd], kb, NT, preferred_element_type=F32)
                if masked:
                    s = jnp.where(_causal_mask(0, 0, s.shape), s, NEG)
                m_new = jnp.maximum(mx, jnp.max(s, axis=1, keepdims=True))
                alpha = jnp.exp(mx - m_new)
                pr = jnp.exp(s - m_new)
                l = alpha * l + jnp.sum(pr, axis=1, keepdims=True)
                acc = alpha * acc + jnp.dot(pr.astype(BF16), vb, preferred_element_type=F32)
                out.append((m_new, l, acc))
            return tuple(out)

        one = (jnp.full((tq, 1), NEG, F32), jnp.zeros((tq, 1), F32), jnp.zeros((tq, HEAD_PAD), F32))
        quads = i // 4
        carry = lax.fori_loop(0, quads, lambda jj, c: chunk(4 * jj, c, False, 4), (one,) * HEADS_PER_STEP)
        carry = lax.fori_loop(0, lax.rem(i, 4) // 2, lambda _, c: chunk(4 * quads, c, False, 2), carry)
        carry = lax.fori_loop(0, lax.rem(i, 2), lambda _, c: chunk(i - 1, c, False), carry)
        carry = chunk(i, carry, True)
        for hd in range(HEADS_PER_STEP):
            mx, l, acc = carry[hd]
            lanes = slice(hd * HEAD_PAD, (hd + 1) * HEAD_PAD)
            o_ref[:, lanes] = (acc / l).astype(o_ref.dtype)
            lse_ref[:, lanes] = jnp.broadcast_to(mx + jnp.log(l), (tq, HEAD_PAD))

    qspec = pl.BlockSpec((tq, STEP_LANES), lambda b, hh, i: (b * nq + i, hh))
    kvspec = pl.BlockSpec((tp, STEP_LANES), lambda b, hh, i: (b, hh))
    shp = (nb * tp, MLA_HEADS * HEAD_PAD)
    return pl.pallas_call(
        body, out_shape=(jax.ShapeDtypeStruct(shp, BF16), jax.ShapeDtypeStruct(shp, F32)),
        grid=(nb, HEAD_STEPS, nq), in_specs=[qspec, kvspec, kvspec], out_specs=(qspec, qspec),
        compiler_params=_cparams(3), name="od_flash_fwd")(q, k, v)


def _flash_prep(o, do, lse_c, nb, tp):
    tq = _pick(tp, ROW_TILE)
    nq = tp // tq

    def body(o_ref, do_ref, lse_ref, lr_ref, dr_ref):
        for hh in range(MLA_HEADS):
            lanes = slice(hh * HEAD_PAD, (hh + 1) * HEAD_PAD)
            delta = jnp.sum(o_ref[:, lanes].astype(F32) * do_ref[:, lanes].astype(F32), axis=1, keepdims=True)
            lr_ref[hh] = jnp.transpose(lse_ref[:, lanes])[0:SUBLANES, :]
            dr_ref[hh] = jnp.transpose(jnp.broadcast_to(delta, (tq, HEAD_PAD)))[0:SUBLANES, :]

    qspec = pl.BlockSpec((tq, MLA_HEADS * HEAD_PAD), lambda b, i: (b * nq + i, 0))
    rspec = pl.BlockSpec((MLA_HEADS, None, SUBLANES, tq), lambda b, i: (b, i, 0, 0))
    rshape = jax.ShapeDtypeStruct((nb * MLA_HEADS, nq, SUBLANES, tq), F32)
    return pl.pallas_call(
        body, out_shape=(rshape, rshape), grid=(nb, nq), in_specs=[qspec, qspec, qspec],
        out_specs=(rspec, rspec), compiler_params=_cparams(2), name="od_flash_prep")(o, do, lse_c)


def _flash_bwd(q, k, v, do, lse_r, delta_r, nb, tp):
    tq = _pick(tp, ROW_TILE)
    nq = tp // tq

    def body(q_ref, k_ref, v_ref, do_ref, lse_ref, dl_ref, dq_ref, dk_ref, dv_ref):
        j = pl.program_id(2)

        @pl.when(j == 0)
        def _():
            dq_ref[...] = jnp.zeros_like(dq_ref)

        kbs = [k_ref[:, hd * HEAD_PAD:(hd + 1) * HEAD_PAD] for hd in range(HEADS_PER_STEP)]
        vbs = [v_ref[:, hd * HEAD_PAD:(hd + 1) * HEAD_PAD] for hd in range(HEADS_PER_STEP)]

        def chunk(i, carry, masked):
            off = pl.multiple_of(i * tq, tq)
            out = []
            for hd in range(HEADS_PER_STEP):
                dk, dv = carry[hd]
                lanes = slice(hd * HEAD_PAD, (hd + 1) * HEAD_PAD)
                qb = q_ref[pl.ds(off, tq), lanes]
                dob = do_ref[pl.ds(off, tq), lanes]
                lse = lse_ref[hd, i][0:1, :]
                delta = dl_ref[hd, i][0:1, :]
                st = lax.dot_general(kbs[hd], qb, NT, preferred_element_type=F32)
                pt = jnp.exp(st - lse)
                if masked:
                    keys = lax.broadcasted_iota(jnp.int32, st.shape, 0)
                    queries = lax.broadcasted_iota(jnp.int32, st.shape, 1)
                    pt = jnp.where(keys <= queries, pt, 0.0)
                dv = dv + jnp.dot(pt.astype(BF16), dob, preferred_element_type=F32)
                dpt = lax.dot_general(vbs[hd], dob, NT, preferred_element_type=F32)
                dst = (pt * (dpt - delta)).astype(BF16)
                dk = dk + jnp.dot(dst, qb, preferred_element_type=F32)
                dq_ref[pl.ds(off, tq), lanes] += lax.dot_general(dst, kbs[hd], TN, preferred_element_type=F32)
                out.append((dk, dv))
            return tuple(out)

        zero = jnp.zeros((tq, HEAD_PAD), F32)
        carry = chunk(j, ((zero, zero),) * HEADS_PER_STEP, True)
        carry = lax.fori_loop(j + 1, nq, lambda i, c: chunk(i, c, False), carry)
        for hd in range(HEADS_PER_STEP):
            lanes = slice(hd * HEAD_PAD, (hd + 1) * HEAD_PAD)
            dk_ref[:, lanes] = carry[hd][0]
            dv_ref[:, lanes] = carry[hd][1].astype(dv_ref.dtype)

    tspec = pl.BlockSpec((tq, STEP_LANES), lambda b, hh, j: (b * nq + j, hh))
    fullspec = pl.BlockSpec((tp, STEP_LANES), lambda b, hh, j: (b, hh))
    rspec = pl.BlockSpec((HEADS_PER_STEP, nq, SUBLANES, tq), lambda b, hh, j: (b * HEAD_STEPS + hh, 0, 0, 0))
    shp = (nb * tp, MLA_HEADS * HEAD_PAD)
    return pl.pallas_call(
        body, out_shape=(jax.ShapeDtypeStruct(shp, F32), jax.ShapeDtypeStruct(shp, F32),
                         jax.ShapeDtypeStruct(shp, BF16)),
        grid=(nb, HEAD_STEPS, nq), in_specs=[fullspec, tspec, tspec, fullspec, rspec, rspec],
        out_specs=(fullspec, tspec, tspec), compiler_params=_cparams(3),
        name="od_flash_bwd")(q, k, v, do, lse_r, delta_r)


def _odd_fwd(x, p, tables, m, tp, nb):
    scale = QK_HEAD ** -0.5
    h = _rms_fwd(x, p["norm"], "od_norm")
    u = _matmul(h, p["w_in"], "nn", F32, name="od_in")
    cq = u[:, :Q_LORA]
    ckv = u[:, Q_LORA:Q_LORA + KV_LORA]
    cqn = _rms_fwd(cq, p["q_norm"], "od_q_norm")
    ckvn = _rms_fwd(ckv, p["kv_norm"], "od_kv_norm")
    q_raw = _matmul(cqn, p["w_uq"], "nn", F32, name="od_uq")
    k_raw = _matmul(ckvn, p["w_uk"], "nn", F32, name="od_uk")
    v = _matmul(ckvn, p["w_uv"], "nn", BF16, name="od_uv")
    q = _rope_call("od_rope_q", lambda qv, c, lo, hi: _rope(qv, c, lo, hi) * scale, m, tp, [(q_raw, None)], tables,
                   BF16)
    kr_col = (Q_LORA + KV_LORA) // HEAD_PAD
    k = _rope_call("od_rope_k", lambda kv, kr, c, lo, hi: kv + kr, m, tp, [(k_raw, None), (u, kr_col)], tables, BF16,
                   shared_pre=lambda uv, c, lo, hi: _rope(pltpu.roll(uv, QK_NOPE, 1), c, lo, hi))
    o, lse_c = _flash_fwd(q, k, v, nb, tp)
    out = _matmul(o, p["w_out"], "nn", F32, residual=x, name="od_out")
    return out, (x, h, cq, ckv, cqn, ckvn, q, k, v, o, lse_c)


def _odd_bwd(dout, p, tables, saved, m, tp, nb):
    scale = QK_HEAD ** -0.5
    x, h, cq, ckv, cqn, ckvn, q, k, v, o, lse_c = saved
    do = _matmul(dout, p["w_out"], "nt", BF16, name="od_out_dx")
    d_w_out = _matmul(o, dout, "tn", F32, name="od_out_dw")
    lse_r, delta_r = _flash_prep(o, do, lse_c, nb, tp)
    dq, dk, dv = _flash_bwd(q, k, v, do, lse_r, delta_r, nb, tp)
    dq_raw = _rope_call("od_rope_q_bwd", lambda d, c, lo, hi: _rope_t(d, c, lo, hi) * scale, m, tp, [(dq, None)],
                        tables, BF16)
    dkr = _rope_k_bwd(dk, tables, m, tp)
    d_w_uq = _matmul(cqn, dq_raw, "tn", F32, name="od_uq_dw")
    d_w_uk = _matmul(ckvn, dk, "tn", F32, name="od_uk_dw")
    d_w_uv = _matmul(ckvn, dv, "tn", F32, name="od_uv_dw")
    dcqn = _matmul(dq_raw, p["w_uq"], "nt", F32, name="od_uq_dx")
    dckvn = _matmul(dk, p["w_uk"], "nt", F32, name="od_uk_dx")
    dckvn = _matmul(dv, p["w_uv"], "nt", F32, residual=dckvn, name="od_uv_dx")
    dcq, d_q_norm = _rms_bwd(cq, p["q_norm"], dcqn, None, "od_q_norm_bwd")
    dckv, d_kv_norm = _rms_bwd(ckv, p["kv_norm"], dckvn, None, "od_kv_norm_bwd")
    du = jnp.concatenate([dcq, dckv, dkr], axis=1)
    d_w_in = _matmul(h, du, "tn", F32, name="od_in_dw")
    dh = _matmul(du, p["w_in"], "nt", F32, name="od_in_dx")
    dx, d_norm = _rms_bwd(x, p["norm"], dh, dout, "od_norm_bwd")
    return dx, dict(norm=d_norm, w_in=d_w_in, q_norm=d_q_norm, kv_norm=d_kv_norm, w_uq=d_w_uq, w_uk=d_w_uk,
                    w_uv=d_w_uv, w_out=d_w_out)


def _loss_head(hf, g, target, tp, t_real):
    m, c = hf.shape
    tm = _pick(tp, ROW_TILE)
    tps = tp // tm

    def body(x_ref, g_ref, t_ref, dx_ref, dg_ref, loss_ref):
        i = pl.program_id(0)
        xf = x_ref[...]
        r = lax.rsqrt(jnp.mean(xf * xf, axis=-1, keepdims=True) + EPS)
        xn = xf * r
        t_pos = lax.rem(i, tps) * tm + lax.broadcasted_iota(jnp.int32, (tm, 1), 0)
        valid = jnp.logical_and(t_pos >= N_META, t_pos < t_real)
        err = jnp.where(valid, xn * g_ref[...] - t_ref[...], 0.0)
        dyf = err * (1.0 / c)
        dyg = dyf * g_ref[...]
        dx_ref[...] = r * (dyg - xn * jnp.mean(dyg * xn, axis=-1, keepdims=True))

        @pl.when(i == 0)
        def _():
            dg_ref[...] = jnp.zeros_like(dg_ref)
            loss_ref[...] = jnp.zeros_like(loss_ref)

        dg_ref[...] += jnp.sum(dyf * xn, axis=0, keepdims=True)
        loss_ref[...] += (0.5 / c) * jnp.sum(jnp.sum(err * err, axis=1, keepdims=True), axis=0, keepdims=True)

    row = pl.BlockSpec((tm, c), lambda i: (i, 0))
    vec = pl.BlockSpec((1, c), lambda i: (0, 0))
    return pl.pallas_call(
        body, out_shape=(jax.ShapeDtypeStruct((m, c), F32), jax.ShapeDtypeStruct((1, c), F32),
                         jax.ShapeDtypeStruct((1, 1), F32)),
        grid=(m // tm,), in_specs=[row, vec, row], out_specs=(row, vec, pl.BlockSpec((1, 1), lambda i: (0, 0))),
        compiler_params=_cparams(1), name="loss_head")(hf, g, target)


def _meta_grad(dh0, nb, tp):
    d = dh0.shape[1]

    def body(x_ref, o_ref):
        @pl.when(pl.program_id(0) == 0)
        def _():
            o_ref[...] = jnp.zeros_like(o_ref)

        o_ref[...] += x_ref[...]

    return pl.pallas_call(
        body, out_shape=jax.ShapeDtypeStruct((N_META, d), F32), grid=(nb,),
        in_specs=[pl.BlockSpec((N_META, d), lambda b: (b * (tp // N_META), 0))],
        out_specs=pl.BlockSpec((N_META, d), lambda b: (0, 0)), compiler_params=_cparams(1), name="meta_grad")(dh0)


def _mesh_pos():
    x, y, c = lax.axis_index("x"), lax.axis_index("y"), lax.axis_index("c")
    return x, y, c


N_CHIP = 4
MESH_ID = pl.DeviceIdType.MESH


def _peer_chip(x, y, k):
    px = 1 - x if k & 2 else x
    py = 1 - y if k & 1 else y
    return px, py


def _all_gather(arrays):
    n = len(arrays)

    def body(*refs):
        srcs, outs = refs[:n], refs[n:2 * n]
        send_sems, recv_sems, local_sems = refs[2 * n:]
        x, y, c = _mesh_pos()
        me = 4 * x + 2 * y + c
        sibling = (x, y, 1 - c)

        def copy(a, sem, src, block, to):
            return pltpu.make_async_remote_copy(
                src_ref=src, dst_ref=outs[a].at[block], send_sem=send_sems.at[a, sem], recv_sem=recv_sems.at[a, sem],
                device_id=to, device_id_type=MESH_ID)

        local = [pltpu.make_async_copy(srcs[a], outs[a].at[me], local_sems.at[a]) for a in range(n)]
        for cp in local:
            cp.start()
        sends = [copy(a, 0, srcs[a], me, sibling) for a in range(n)]
        for k in range(1, N_CHIP):
            px, py = _peer_chip(x, y, k)
            sends += [copy(a, k, srcs[a], me, (px, py, c)) for a in range(n)]
        for cp in sends:
            cp.start()
        for k in range(1, N_CHIP):
            px, py = _peer_chip(x, y, k)
            block = 4 * px + 2 * py + c
            for a in range(n):
                copy(a, k, srcs[a], block, sibling).wait_recv()
            passed = [copy(a, N_CHIP - 1 + k, outs[a].at[block], block, sibling) for a in range(n)]
            for cp in passed:
                cp.start()
            sends += passed
        for a in range(n):
            copy(a, 0, srcs[a], 4 * x + 2 * y + (1 - c), sibling).wait_recv()
        for k in range(1, N_CHIP):
            px, py = _peer_chip(x, y, k)
            for a in range(n):
                copy(a, N_CHIP - 1 + k, srcs[a], 4 * px + 2 * py + (1 - c), sibling).wait_recv()
        for cp in sends:
            cp.wait_send()
        for cp in local:
            cp.wait()

    any_spec = pl.BlockSpec(memory_space=pl.ANY)
    out_shape = tuple(jax.ShapeDtypeStruct((N_DEV,) + a.shape, a.dtype) for a in arrays)
    return pl.pallas_call(
        body, out_shape=out_shape, in_specs=[any_spec] * n, out_specs=(any_spec,) * n,
        scratch_shapes=[pltpu.SemaphoreType.DMA((n, N_DEV - 1)), pltpu.SemaphoreType.DMA((n, N_DEV - 1)),
                        pltpu.SemaphoreType.DMA((n,))],
        name="weight_all_gather")(*arrays)


def _pair_exchange(arrays):
    n = len(arrays)

    def body(*refs):
        srcs, outs = refs[:n], refs[n:2 * n]
        send_sems, recv_sems = refs[2 * n:]
        x, y, c = _mesh_pos()
        copies = [pltpu.make_async_remote_copy(
            src_ref=srcs[a], dst_ref=outs[a], send_sem=send_sems.at[a], recv_sem=recv_sems.at[a],
            device_id=(x, y, 1 - c), device_id_type=MESH_ID) for a in range(n)]
        for cp in copies:
            cp.start()
        for cp in copies:
            cp.wait()

    any_spec = pl.BlockSpec(memory_space=pl.ANY)
    return pl.pallas_call(
        body, out_shape=tuple(jax.ShapeDtypeStruct(a.shape, a.dtype) for a in arrays), in_specs=[any_spec] * n,
        out_specs=(any_spec,) * n, scratch_shapes=[pltpu.SemaphoreType.DMA((n,)), pltpu.SemaphoreType.DMA((n,))],
        name="grad_pair_exchange")(*arrays)


def _chip_exchange(arrays):
    n = len(arrays)

    def body(*refs):
        srcs, outs = refs[:n], refs[n:2 * n]
        send_sems, recv_sems, local_sems = refs[2 * n:]
        x, y, c = _mesh_pos()
        q = 2 * x + y
        local = [pltpu.make_async_copy(srcs[a].at[q], outs[a].at[q], local_sems.at[a]) for a in range(n)]
        for cp in local:
            cp.start()

        def copy(a, k, to_q, from_q, px, py):
            return pltpu.make_async_remote_copy(
                src_ref=srcs[a].at[to_q], dst_ref=outs[a].at[from_q], send_sem=send_sems.at[a, k - 1],
                recv_sem=recv_sems.at[a, k - 1], device_id=(px, py, c), device_id_type=MESH_ID)

        sends = []
        for k in range(1, N_CHIP):
            px, py = _peer_chip(x, y, k)
            sends += [copy(a, k, 2 * px + py, q, px, py) for a in range(n)]
        for cp in sends:
            cp.start()
        for k in range(1, N_CHIP):
            px, py = _peer_chip(x, y, k)
            for a in range(n):
                copy(a, k, q, 2 * px + py, px, py).wait_recv()
        for cp in sends:
            cp.wait_send()
        for cp in local:
            cp.wait()

    any_spec = pl.BlockSpec(memory_space=pl.ANY)
    return pl.pallas_call(
        body, out_shape=tuple(jax.ShapeDtypeStruct(a.shape, a.dtype) for a in arrays), in_specs=[any_spec] * n,
        out_specs=(any_spec,) * n,
        scratch_shapes=[pltpu.SemaphoreType.DMA((n, N_CHIP - 1)), pltpu.SemaphoreType.DMA((n, N_CHIP - 1)),
                        pltpu.SemaphoreType.DMA((n,))],
        name="grad_chip_exchange")(*arrays)


REDUCE_BLOCK_BYTES = 512 * 1024


def _pair_add(a, b):
    p, r, c = a.shape
    tr = _reduce_rows(r, c)

    def body(a_ref, b_ref, o_ref):
        o_ref[...] = (a_ref[...].astype(F32) + b_ref[...].astype(F32)).astype(o_ref.dtype)

    blk = pl.BlockSpec((None, tr, c), lambda s, i: (s, i, 0))
    return pl.pallas_call(
        body, out_shape=jax.ShapeDtypeStruct(a.shape, a.dtype), grid=(p, r // tr), in_specs=[blk, blk], out_specs=blk,
        compiler_params=_cparams(2), name="grad_pair_add")(a, b)


def _reduce_rows(r, c):
    best = None
    for t in range(16, r + 1, 16):
        if r % t == 0 and t * c * 4 <= REDUCE_BLOCK_BYTES:
            best = t
    assert best is not None, (r, c)
    return best


def _reduce_adamw(parts, w, mom, vel):
    n_parts, r, c = parts.shape
    tr = _reduce_rows(r, c)
    c1 = 1.0 - ADAM_B1 ** ADAM_STEP
    c2 = 1.0 - ADAM_B2 ** ADAM_STEP

    def body(p_ref, w_ref, m_ref, v_ref, g_ref, d_ref, mo_ref, vo_ref):
        g = p_ref[0].astype(F32)
        for s in range(1, n_parts):
            g = g + p_ref[s].astype(F32)
        mn = ADAM_B1 * m_ref[...] + (1.0 - ADAM_B1) * g
        vn = ADAM_B2 * v_ref[...] + (1.0 - ADAM_B2) * (g * g)
        m_hat = mn / c1
        v_hat = vn / c2
        g_ref[...] = g
        d_ref[...] = -ADAM_LR * (m_hat / (jnp.sqrt(v_hat) + ADAM_EPS) + ADAM_WD * w_ref[...])
        mo_ref[...] = mn
        vo_ref[...] = vn

    blk = pl.BlockSpec((tr, c), lambda i: (i, 0))
    shp = jax.ShapeDtypeStruct((r, c), F32)
    return pl.pallas_call(
        body, out_shape=(shp, shp, shp, shp), grid=(r // tr,),
        in_specs=[pl.BlockSpec((n_parts, tr, c), lambda i: (0, i, 0)), blk, blk, blk], out_specs=(blk, blk, blk, blk),
        compiler_params=_cparams(1), name="reduce_adamw")(parts, w, mom, vel)


def _pack_rows(pieces, width, row_multiple, dtype):
    flat = jnp.concatenate([p.astype(dtype).reshape(-1) for p in pieces])
    rows = -(-flat.shape[0] // (width * row_multiple)) * row_multiple
    return jnp.pad(flat, (0, rows * width - flat.shape[0])).reshape(rows, width)


def _unshard(gathered, axis):
    moved = jnp.moveaxis(gathered, 0, axis)
    shape = list(moved.shape)
    shape[axis:axis + 2] = [shape[axis] * shape[axis + 1]]
    return moved.reshape(shape)


def _to_slots(full, axis):
    shape = list(full.shape)
    shape[axis:axis + 1] = [N_DEV, shape[axis] // N_DEV]
    return jnp.moveaxis(full.reshape(shape), axis, 0)


def _core_slots(full, axis, core):
    shape = list(full.shape)
    shape[axis:axis + 1] = [N_CHIP, 2, shape[axis] // N_DEV]
    picked = lax.dynamic_index_in_dim(full.reshape(shape), core, axis + 1, keepdims=False)
    return jnp.moveaxis(picked, axis, 0)


def _block_diag(w):
    hh, d, _ = w.shape
    eye = jnp.eye(hh, dtype=w.dtype)
    return (w[:, :, None, :] * eye[:, None, :, None]).reshape(hh * d, hh * d)


def _block_diag_t(full, hh):
    d = full.shape[0] // hh
    f4 = full.reshape(hh, d, hh, d)
    return jnp.stack([f4[i, :, i, :] for i in range(hh)], axis=0)


def _pad_heads(w, width):
    r = w.shape[0]
    w3 = w.reshape(r, MLA_HEADS, width)
    return jnp.pad(w3, ((0, 0), (0, 0), (0, HEAD_PAD - width))).reshape(r, MLA_HEADS * HEAD_PAD)


def _unpad_heads(w, width):
    r = w.shape[0]
    return w.reshape(r, MLA_HEADS, HEAD_PAD)[:, :, :width].reshape(r, MLA_HEADS * width)


def kernel(x, meta_tokens, ev_norm, ev_w_in, ev_conv_a, ev_conv_b, ev_conv_b_bias, ev_gate_r_w, ev_gate_r_b, ev_gate_i_w, ev_gate_i_b, ev_lru_lambda, ev_w_out, od_norm, od_w_in, od_q_norm, od_kv_norm, od_w_uq, od_w_ukv, od_w_out, ffn_norm, ffn_w_up, ffn_conv_w, ffn_conv_b, ffn_w_down, final_norm, loss_target, m_meta_tokens, m_ev_norm, m_ev_w_in, m_ev_conv_a, m_ev_conv_b, m_ev_conv_b_bias, m_ev_gate_r_w, m_ev_gate_r_b, m_ev_gate_i_w, m_ev_gate_i_b, m_ev_lru_lambda, m_ev_w_out, m_od_norm, m_od_w_in, m_od_q_norm, m_od_kv_norm, m_od_w_uq, m_od_w_ukv, m_od_w_out, m_ffn_norm, m_ffn_w_up, m_ffn_conv_w, m_ffn_conv_b, m_ffn_w_down, m_final_norm, v_meta_tokens, v_ev_norm, v_ev_w_in, v_ev_conv_a, v_ev_conv_b, v_ev_conv_b_bias, v_ev_gate_r_w, v_ev_gate_r_b, v_ev_gate_i_w, v_ev_gate_i_b, v_ev_lru_lambda, v_ev_w_out, v_od_norm, v_od_w_in, v_od_q_norm, v_od_kv_norm, v_od_w_uq, v_od_w_ukv, v_od_w_out, v_ffn_norm, v_ffn_w_up, v_ffn_conv_w, v_ffn_conv_b, v_ffn_w_down, v_final_norm):
    given = dict(locals())
    names = [n for n, _ in PARAMS]
    axis_of = dict(PARAMS)
    w_loc = {n: given[n] for n in names}
    m_loc = {n: given["m_" + n] for n in names}
    v_loc = {n: given["v_" + n] for n in names}
    sharded = [n for n in names if axis_of[n] is not None]
    replicated = [n for n in names if axis_of[n] is None]
    small = [n for n in sharded if n not in BIG]

    nb, seq, d = x.shape
    t_real = N_META + seq
    tp = -(-t_real // ROW_TILE) * ROW_TILE
    m = nb * tp

    small_pack = _pack_rows([w_loc[n] for n in small], LANES, SUBLANES, F32)
    gathered = _all_gather([w_loc[n].astype(BF16) for n in BIG] + [small_pack])
    full = {n: w_loc[n] for n in replicated}
    for n, g in zip(BIG, gathered[:-1]):
        full[n] = _unshard(g, axis_of[n])
    flat = gathered[-1].reshape(N_DEV, -1)
    off = 0
    for n in small:
        shard = w_loc[n].shape
        size = math.prod(shard)
        full[n] = _unshard(flat[:, off:off + size].reshape((N_DEV,) + shard), axis_of[n])
        off += size

    tables = _rope_tables(tp)

    def even_params(j):
        w_out = full["ev_w_out"][j]
        return dict(norm=full["ev_norm"][j][None], w_in=full["ev_w_in"][j], conv_a=full["ev_conv_a"][j],
                    conv_b=full["ev_conv_b"][j], conv_b_bias=full["ev_conv_b_bias"][j][None],
                    gate_r=_block_diag(full["ev_gate_r_w"][j]).astype(BF16),
                    gate_i=_block_diag(full["ev_gate_i_w"][j]).astype(BF16),
                    gate_r_b=full["ev_gate_r_b"][j][None], gate_i_b=full["ev_gate_i_b"][j][None],
                    lam=full["ev_lru_lambda"][j][None], w_out=w_out, w_out_a=w_out[:LRU_WIDTH],
                    w_out_b=w_out[LRU_WIDTH:])

    def odd_params(j):
        w_ukv = full["od_w_ukv"][j].reshape(KV_LORA, MLA_HEADS, QK_NOPE + V_HEAD)
        w_uk = w_ukv[:, :, :QK_NOPE].reshape(KV_LORA, MLA_HEADS * QK_NOPE)
        w_uv = w_ukv[:, :, QK_NOPE:].reshape(KV_LORA, MLA_HEADS * V_HEAD)
        w_out = full["od_w_out"][j].reshape(MLA_HEADS, V_HEAD, d)
        w_out = jnp.pad(w_out, ((0, 0), (0, HEAD_PAD - V_HEAD), (0, 0))).reshape(MLA_HEADS * HEAD_PAD, d)
        return dict(norm=full["od_norm"][j][None], w_in=jnp.pad(full["od_w_in"][j], ((0, 0), (0, ODD_IN_PAD - ODD_IN))),
                    q_norm=full["od_q_norm"][j][None], kv_norm=full["od_kv_norm"][j][None],
                    w_uq=_pad_heads(full["od_w_uq"][j], QK_HEAD), w_uk=_pad_heads(w_uk, QK_NOPE),
                    w_uv=_pad_heads(w_uv, V_HEAD), w_out=w_out)

    def ffn_params(layer):
        w_up = full["ffn_w_up"][layer]
        return dict(norm=full["ffn_norm"][layer][None], w_up=w_up, conv_w=full["ffn_conv_w"][layer],
                    conv_b=full["ffn_conv_b"][layer][None], w_down=full["ffn_w_down"][layer])

    meta = jnp.broadcast_to(full["meta_tokens"][None], (nb, N_META, d))
    h0 = jnp.concatenate([meta, x, jnp.zeros((nb, tp - t_real, d), F32)], axis=1).reshape(m, d)
    hcur = h0
    tape = []
    for layer in range(4):
        j = layer // 2
        if layer % 2 == 0:
            mp = even_params(j)
            hcur, saved = _even_fwd(hcur, mp, m, tp, nb)
        else:
            mp = odd_params(j)
            hcur, saved = _odd_fwd(hcur, mp, tables, m, tp, nb)
        fp = ffn_params(layer)
        hcur, fsaved = _ffn_fwd(hcur, fp, m, tp)
        tape.append((mp, saved, fp, fsaved))

    target = jnp.pad(loss_target, ((0, 0), (N_META, tp - t_real), (0, 0))).reshape(m, d)
    dh, d_final_norm, loss_part = _loss_head(hcur, full["final_norm"][None], target, tp, t_real)
    loss = lax.psum(loss_part[0, 0], ("x", "y", "c"))

    grads = {"final_norm": d_final_norm[0]}
    ev_g, od_g, ffn_g = [None, None], [None, None], [None] * 4
    for layer in reversed(range(4)):
        mp, saved, fp, fsaved = tape[layer]
        dh, ffn_g[layer] = _ffn_bwd(dh, fp, fsaved, m, tp)
        if layer % 2 == 0:
            dh, ev_g[layer // 2] = _even_bwd(dh, mp, saved, m, tp, nb)
        else:
            dh, od_g[layer // 2] = _odd_bwd(dh, mp, tables, saved, m, tp, nb)

    dh3 = dh.reshape(nb, tp, d)
    grad_x = dh3[:, N_META:t_real]
    grads["meta_tokens"] = _meta_grad(dh, nb, tp)

    def stack(lst, key, fn=lambda a: a):
        return jnp.stack([fn(g[key]) for g in lst], axis=0)

    grads["ev_norm"] = stack(ev_g, "norm", lambda a: a[0])
    grads["ev_w_in"] = stack(ev_g, "w_in")
    grads["ev_conv_a"] = stack(ev_g, "conv_a")
    grads["ev_conv_b"] = stack(ev_g, "conv_b")
    grads["ev_conv_b_bias"] = stack(ev_g, "conv_b_bias", lambda a: a[0])
    grads["ev_gate_r_w"] = stack(ev_g, "gate_r", lambda a: _block_diag_t(a, 8))
    grads["ev_gate_r_b"] = stack(ev_g, "gate_r_b", lambda a: a[0])
    grads["ev_gate_i_w"] = stack(ev_g, "gate_i", lambda a: _block_diag_t(a, 8))
    grads["ev_gate_i_b"] = stack(ev_g, "gate_i_b", lambda a: a[0])
    grads["ev_lru_lambda"] = stack(ev_g, "lam", lambda a: a[0])
    grads["ev_w_out"] = stack(ev_g, "w_out")
    grads["od_norm"] = stack(od_g, "norm", lambda a: a[0])
    grads["od_w_in"] = stack(od_g, "w_in", lambda a: a[:, :ODD_IN])
    grads["od_q_norm"] = stack(od_g, "q_norm", lambda a: a[0])
    grads["od_kv_norm"] = stack(od_g, "kv_norm", lambda a: a[0])
    grads["od_w_uq"] = stack(od_g, "w_uq", lambda a: _unpad_heads(a, QK_HEAD))

    def ukv(g):
        gk = g["w_uk"].reshape(KV_LORA, MLA_HEADS, HEAD_PAD)[:, :, :QK_NOPE]
        gv = g["w_uv"].reshape(KV_LORA, MLA_HEADS, HEAD_PAD)[:, :, :V_HEAD]
        return jnp.concatenate([gk, gv], axis=2).reshape(KV_LORA, MLA_HEADS * (QK_NOPE + V_HEAD))

    grads["od_w_ukv"] = jnp.stack([ukv(g) for g in od_g], axis=0)
    grads["od_w_out"] = stack(od_g, "w_out", lambda a: a.reshape(MLA_HEADS, HEAD_PAD, d)[:, :V_HEAD].reshape(-1, d))
    grads["ffn_norm"] = stack(ffn_g, "norm", lambda a: a[0])
    grads["ffn_w_up"] = stack(ffn_g, "w_up")
    grads["ffn_conv_w"] = stack(ffn_g, "conv_w")
    grads["ffn_conv_b"] = stack(ffn_g, "conv_b", lambda a: a[0])
    grads["ffn_w_down"] = stack(ffn_g, "w_down")

    order = small + replicated
    slot_parts = [_to_slots(grads[n], axis_of[n]).reshape(N_DEV, -1) for n in small]
    slot_parts += [jnp.broadcast_to(grads[n].reshape(1, -1), (N_DEV, grads[n].size)) for n in replicated]
    g_flat = jnp.concatenate(slot_parts, axis=1)
    n_flat = g_flat.shape[1]
    rows = -(-n_flat // (1024 * 128)) * 128
    g_small = jnp.pad(g_flat, ((0, 0), (0, rows * 1024 - n_flat))).reshape(N_DEV, rows, 1024)

    def rows_of(n):
        shard = w_loc[n].shape
        return (math.prod(shard[:-1]), shard[-1])

    core = lax.axis_index("c")

    def core_slots(n, which):
        return _core_slots(grads[n], axis_of[n], which).astype(BF16).reshape((N_CHIP,) + rows_of(n))

    small_by_core = jnp.swapaxes(g_small.reshape((N_CHIP, 2) + g_small.shape[1:]), 0, 1)
    mine = [core_slots(n, core) for n in BIG] + [lax.dynamic_index_in_dim(small_by_core, core, 0, keepdims=False)]
    theirs = [core_slots(n, 1 - core) for n in BIG]
    theirs.append(lax.dynamic_index_in_dim(small_by_core, 1 - core, 0, keepdims=False))
    from_sibling = _pair_exchange(theirs)
    parts = _chip_exchange([_pair_add(a, b) for a, b in zip(mine, from_sibling)])

    g_out, d_out, m_out, v_out = {}, {}, {}, {}
    for n, part in zip(BIG, parts[:-1]):
        res = _reduce_adamw(part, *[t[n].reshape(rows_of(n)) for t in (w_loc, m_loc, v_loc)])
        for out, r in zip((g_out, d_out, m_out, v_out), res):
            out[n] = r.reshape(w_loc[n].shape)

    def flat_local(tree):
        flat = jnp.concatenate([tree[n].reshape(-1) for n in order])
        return jnp.pad(flat, (0, rows * 1024 - n_flat)).reshape(rows, 1024)

    res = _reduce_adamw(parts[-1], flat_local(w_loc), flat_local(m_loc), flat_local(v_loc))
    for out, r in zip((g_out, d_out, m_out, v_out), res):
        flat = r.reshape(-1)
        off = 0
        for n in order:
            size = w_loc[n].size
            out[n] = flat[off:off + size].reshape(w_loc[n].shape)
            off += size
    return (loss, grad_x, *[g_out[n] for n in names], *[d_out[n] for n in names], *[m_out[n] for n in names],
            *[v_out[n] for n in names])
```

```python
import functools
import math

import jax
import jax.numpy as jnp
from jax import lax
from jax.experimental import pallas as pl
from jax.experimental.pallas import tpu as pltpu

F32 = jnp.float32
BF16 = jnp.bfloat16

N_DEV = 8
N_META = 16
EPS = 1e-6
LRU_C = 8.0
MLA_HEADS = 16
QK_NOPE = 64
QK_ROPE = 32
QK_HEAD = QK_NOPE + QK_ROPE
V_HEAD = 64
HEAD_PAD = 128
Q_LORA = 384
KV_LORA = 256
ODD_IN = Q_LORA + KV_LORA + QK_ROPE
ODD_IN_PAD = 768
ROPE_BASE = 10000.0
LRU_WIDTH = 512
D_FF = 2816

ADAM_LR = 0.001
ADAM_B1 = 0.9
ADAM_B2 = 0.999
ADAM_EPS = 1e-08
ADAM_WD = 0.01
ADAM_STEP = 10

ROW_TILE = 384
SUBLANES = 8
HALO_ROWS = 16
LANES = 128
VMEM_LIMIT = 48 * 1024 * 1024
NEG = -1e30

PARAMS = (
    ("meta_tokens", 1), ("ev_norm", None), ("ev_w_in", 2), ("ev_conv_a", 2), ("ev_conv_b", 2),
    ("ev_conv_b_bias", None), ("ev_gate_r_w", None), ("ev_gate_r_b", None), ("ev_gate_i_w", None),
    ("ev_gate_i_b", None), ("ev_lru_lambda", None), ("ev_w_out", 1), ("od_norm", 1), ("od_w_in", 1),
    ("od_q_norm", 1), ("od_kv_norm", 1), ("od_w_uq", 2), ("od_w_ukv", 2), ("od_w_out", 1),
    ("ffn_norm", None), ("ffn_w_up", 2), ("ffn_conv_w", 2), ("ffn_conv_b", None), ("ffn_w_down", 1),
    ("final_norm", None),
)
BIG = ("ev_w_in", "ev_w_out", "od_w_in", "od_w_uq", "od_w_ukv", "od_w_out", "ffn_w_up", "ffn_w_down")


def _cparams(n_grid):
    return pltpu.CompilerParams(dimension_semantics=("arbitrary",) * n_grid, vmem_limit_bytes=VMEM_LIMIT)


def _pick(dim, target):
    if dim <= target:
        return dim
    best = None
    for t in range(LANES, target + 1, LANES):
        if dim % t == 0:
            best = t
    assert best is not None, (dim, target)
    return best


MATMUL_VMEM_BUDGET = 38 * 1024 * 1024
HBM_BYTES_PER_US = 3.0e6
MXU_FLOPS_PER_US = 9.0e8
ACC_BYTES_PER_US = 7.6e6
GRID_STEP_US = 0.35


def _tile_candidates(dim):
    return [t for t in range(LANES, dim + 1, LANES) if dim % t == 0] or [dim]


def _matmul_tiles(m, n, k, sa, sb, so, sr, transposed_lhs):
    best, best_cost = None, None
    for tm in _tile_candidates(m):
        for tn in _tile_candidates(n):
            for tk in _tile_candidates(k):
                nk = k // tk
                vmem = 2 * (tm * tk * sa + tk * tn * sb) + tm * tn * ((4 if nk > 1 else 0) + 2 * so + 2 * sr)
                vmem += (tm * tk * 2 if sa > 2 else 0) + (tk * tn * 2 if sb > 2 else 0) + tm * tn * 4
                if vmem > MATMUL_VMEM_BUDGET:
                    continue
                steps = (m // tm) * (n // tn) * nk
                traffic = m * k * sa * (n // tn) + k * n * sb * (m // tm) + m * n * (so + sr)
                acc_us = steps * tm * tn * 4 / ACC_BYTES_PER_US if nk > 1 else 0.0
                busy_us = 0.0 if transposed_lhs else 2.0 * m * n * k / MXU_FLOPS_PER_US + acc_us
                cost = max(traffic / HBM_BYTES_PER_US, busy_us) + steps * GRID_STEP_US
                if best_cost is None or cost < best_cost:
                    best, best_cost = (tm, tn, tk), cost
    assert best is not None, (m, n, k)
    return best


def _matmul(a, b, mode, out_dtype=F32, residual=None, name="mm", b_col_off=0):
    if mode == "nn":
        (m, k), (k2, n) = a.shape, b.shape
    elif mode == "nt":
        (m, k), n = a.shape, b.shape[0]
        k2 = k if b_col_off or b.shape[1] > k else b.shape[1]
    else:
        (k, m), (k2, n) = a.shape, b.shape
    assert k == k2, (a.shape, b.shape, mode)
    tm, tn, tk = _matmul_tiles(m, n, k, a.dtype.itemsize, b.dtype.itemsize, jnp.dtype(out_dtype).itemsize,
                               0 if residual is None else residual.dtype.itemsize, mode == "tn")
    nk = k // tk
    if mode == "tn":
        a_spec = pl.BlockSpec((tk, tm), lambda i, j, kk: (kk, i))
        dims = (((0,), (0,)), ((), ()))
    else:
        a_spec = pl.BlockSpec((tm, tk), lambda i, j, kk: (i, kk))
        dims = (((1,), (1 if mode == "nt" else 0,)), ((), ()))
    if mode == "nt":
        assert b_col_off % tk == 0, (b_col_off, tk)
        b_spec = pl.BlockSpec((tn, tk), lambda i, j, kk: (j, kk + b_col_off // tk))
    else:
        b_spec = pl.BlockSpec((tk, tn), lambda i, j, kk: (kk, j))
    o_spec = pl.BlockSpec((tm, tn), lambda i, j, kk: (i, j))
    has_res = residual is not None

    def body(*refs):
        a_ref, b_ref = refs[:2]
        r_ref = refs[2] if has_res else None
        o_ref = refs[3] if has_res else refs[2]
        part = lax.dot_general(a_ref[...].astype(BF16), b_ref[...].astype(BF16), dims, preferred_element_type=F32)

        def finish(out):
            if has_res:
                out = out + r_ref[...].astype(F32)
            o_ref[...] = out.astype(o_ref.dtype)

        if nk == 1:
            finish(part)
            return
        acc_ref = refs[-1]
        kk = pl.program_id(2)

        @pl.when(kk == 0)
        def _():
            acc_ref[...] = part

        @pl.when(kk > 0)
        def _():
            acc_ref[...] += part

        @pl.when(kk == nk - 1)
        def _():
            finish(acc_ref[...])

    in_specs = [a_spec, b_spec] + ([o_spec] if has_res else [])
    args = (a, b) + ((residual,) if has_res else ())
    return pl.pallas_call(
        body, out_shape=jax.ShapeDtypeStruct((m, n), out_dtype), grid=(m // tm, n // tn, nk),
        in_specs=in_specs, out_specs=o_spec, scratch_shapes=[pltpu.VMEM((tm, tn), F32)] if nk > 1 else [],
        compiler_params=_cparams(3), name=name)(*args)


def _rms_fwd(x, g, name):
    m, c = x.shape
    tm = _pick(m, ROW_TILE)

    def body(x_ref, g_ref, o_ref):
        xf = x_ref[...].astype(F32)
        r = lax.rsqrt(jnp.mean(xf * xf, axis=-1, keepdims=True) + EPS)
        o_ref[...] = (xf * r * g_ref[...]).astype(o_ref.dtype)

    return pl.pallas_call(
        body, out_shape=jax.ShapeDtypeStruct((m, c), BF16), grid=(m // tm,),
        in_specs=[pl.BlockSpec((tm, c), lambda i: (i, 0)), pl.BlockSpec((1, c), lambda i: (0, 0))],
        out_specs=pl.BlockSpec((tm, c), lambda i: (i, 0)), compiler_params=_cparams(1), name=name)(x, g)


def _rms_bwd(x, g, dy, residual, name):
    m, c = x.shape
    tm = _pick(m, ROW_TILE)
    has_res = residual is not None

    def body(*refs):
        if has_res:
            x_ref, g_ref, dy_ref, r_ref, dx_ref, dg_ref = refs
        else:
            x_ref, g_ref, dy_ref, dx_ref, dg_ref = refs
        xf = x_ref[...].astype(F32)
        dyf = dy_ref[...].astype(F32)
        r = lax.rsqrt(jnp.mean(xf * xf, axis=-1, keepdims=True) + EPS)
        xn = xf * r
        dyg = dyf * g_ref[...]
        dx = r * (dyg - xn * jnp.mean(dyg * xn, axis=-1, keepdims=True))
        if has_res:
            dx = dx + r_ref[...]
        dx_ref[...] = dx

        @pl.when(pl.program_id(0) == 0)
        def _():
            dg_ref[...] = jnp.zeros_like(dg_ref)

        dg_ref[...] += jnp.sum(dyf * xn, axis=0, keepdims=True)

    row = pl.BlockSpec((tm, c), lambda i: (i, 0))
    vec = pl.BlockSpec((1, c), lambda i: (0, 0))
    in_specs = [row, vec, row] + ([row] if has_res else [])
    args = (x, g, dy) + ((residual,) if has_res else ())
    return pl.pallas_call(
        body, out_shape=(jax.ShapeDtypeStruct((m, c), F32), jax.ShapeDtypeStruct((1, c), F32)), grid=(m // tm,),
        in_specs=in_specs, out_specs=(row, vec), compiler_params=_cparams(1), name=name)(*args)


def _chan_call(name, fn, m, tp, tc, ncol, row_ins=(), prev_ins=(), next_ins=(), chan_ins=(), row_outs=(),
               red_outs=(), row_split=1):
    tm = _pick(tp, ROW_TILE) // row_split
    tps = tp // tm
    nrow = m // tm
    halo_blocks = tm // HALO_ROWS
    last_halo = m // HALO_ROWS - 1
    n_in = len(row_ins) + len(prev_ins) + len(next_ins) + len(chan_ins)
    n_r, n_p, n_n = len(row_ins), len(prev_ins), len(next_ins)

    def body(*refs):
        i = pl.program_id(1)
        pos = lax.rem(i, tps)
        at_start = pos == 0
        at_end = pos == tps - 1
        rows = [r[...].astype(F32) for r in refs[:n_r]]
        prevs = [jnp.where(at_start, 0.0, r[...].astype(F32)[SUBLANES:]) for r in refs[n_r:n_r + n_p]]
        nexts = [jnp.where(at_end, 0.0, r[...].astype(F32)[:SUBLANES]) for r in refs[n_r + n_p:n_r + n_p + n_n]]
        chans = [r[...] for r in refs[n_r + n_p + n_n:n_in]]
        out_refs = refs[n_in:n_in + len(row_outs)]
        red_refs = refs[n_in + len(row_outs):]
        row_vals, red_vals = fn(rows, prevs, nexts, chans)
        for ref, val in zip(out_refs, row_vals):
            ref[...] = val.astype(ref.dtype)
        if red_refs:
            @pl.when(i == 0)
            def _():
                for ref in red_refs:
                    ref[...] = jnp.zeros_like(ref)

            for ref, val in zip(red_refs, red_vals):
                ref[...] += val

    in_specs, args = [], []
    for arr, off in row_ins:
        in_specs.append(pl.BlockSpec((tm, tc), lambda j, i, off=off: (i, j + off)))
        args.append(arr)
    for arr, off in prev_ins:
        in_specs.append(pl.BlockSpec((HALO_ROWS, tc),
                                     lambda j, i, off=off: (jnp.maximum(i * halo_blocks - 1, 0), j + off)))
        args.append(arr)
    for arr, off in next_ins:
        in_specs.append(pl.BlockSpec((HALO_ROWS, tc),
                                     lambda j, i, off=off: (jnp.minimum((i + 1) * halo_blocks, last_halo), j + off)))
        args.append(arr)
    for arr, off in chan_ins:
        in_specs.append(pl.BlockSpec((arr.shape[0], tc), lambda j, i, off=off: (0, j + off)))
        args.append(arr)
    out_shape, out_specs = [], []
    for (dt,) in row_outs:
        out_shape.append(jax.ShapeDtypeStruct((m, ncol * tc), dt))
        out_specs.append(pl.BlockSpec((tm, tc), lambda j, i: (i, j)))
    for (k,) in red_outs:
        out_shape.append(jax.ShapeDtypeStruct((k, ncol * tc), F32))
        out_specs.append(pl.BlockSpec((k, tc), lambda j, i: (0, j)))
    return pl.pallas_call(
        body, out_shape=tuple(out_shape), grid=(ncol, nrow), in_specs=in_specs, out_specs=tuple(out_specs),
        compiler_params=_cparams(2), name=name)(*args)


def _shift_down(x, prev8, s):
    if s == 0:
        return x
    tm, tc = x.shape
    groups = tm // SUBLANES
    xr = pltpu.roll(x.reshape(groups, SUBLANES, tc), s, 1)
    before = jnp.concatenate([pltpu.roll(prev8, s, 0)[None], xr[:-1]], axis=0)
    rid = lax.broadcasted_iota(jnp.int32, xr.shape, 1)
    return jnp.where(rid < s, before, xr).reshape(tm, tc)


def _shift_up(x, next8, s):
    if s == 0:
        return x
    tm, tc = x.shape
    groups = tm // SUBLANES
    xr = pltpu.roll(x.reshape(groups, SUBLANES, tc), SUBLANES - s, 1)
    after = jnp.concatenate([xr[1:], pltpu.roll(next8, SUBLANES - s, 0)[None]], axis=0)
    rid = lax.broadcasted_iota(jnp.int32, xr.shape, 1)
    return jnp.where(rid >= SUBLANES - s, after, xr).reshape(tm, tc)


def _taps(x, prev8, kw):
    return [_shift_down(x, prev8, kw - 1 - k) for k in range(kw)]


def _conv_taps(taps, w):
    y = w[0:1, :] * taps[0]
    for k in range(1, len(taps)):
        y = y + w[k:k + 1, :] * taps[k]
    return y


def _conv_dw_taps(dy, taps):
    shape = (SUBLANES, dy.shape[1])
    rid = lax.broadcasted_iota(jnp.int32, shape, 0)
    out = jnp.zeros(shape, F32)
    for k, tap in enumerate(taps):
        out = out + jnp.where(rid == k, jnp.sum(dy * tap, axis=0, keepdims=True), 0.0)
    return out


def _conv_fwd(x, prev8, w):
    return _conv_taps(_taps(x, prev8, w.shape[0]), w)


def _conv_dw(dy, x, prev8, kw):
    return _conv_dw_taps(dy, _taps(x, prev8, kw))


def _conv_dx(dy, next8, w):
    kw = w.shape[0]
    dx = w[kw - 1:kw, :] * dy
    for k in range(kw - 1):
        dx = dx + w[k:k + 1, :] * _shift_up(dy, next8, kw - 1 - k)
    return dx


def _sigmoid(x):
    return 1.0 / (1.0 + jnp.exp(-x))


def _expm1(x):
    series = x * (1.0 + x * 0.5 * (1.0 + x * (1.0 / 3.0) * (1.0 + x * 0.25 * (1.0 + x * 0.2))))
    return jnp.where(jnp.abs(x) < 0.3, series, jnp.exp(x) - 1.0)


def _softplus_neg(lam):
    e = jnp.exp(-jnp.abs(lam))
    log1p = jnp.where(e < 1e-2, e * (1.0 - e * (0.5 - e * (1.0 / 3.0))), jnp.log(1.0 + e))
    return jnp.maximum(-lam, 0.0) + log1p


GELU_C = math.sqrt(2.0 / math.pi)


def _gelu(x):
    return 0.5 * x * (1.0 + jnp.tanh(GELU_C * (x + 0.044715 * x * x * x)))


def _gelu_grad(x):
    t = jnp.tanh(GELU_C * (x + 0.044715 * x * x * x))
    return 0.5 * (1.0 + t) + 0.5 * x * (1.0 - t * t) * GELU_C * (1.0 + 3.0 * 0.044715 * x * x)


FFN_COL_TILE = 1408


def _ffn_fwd(x, p, m, tp):
    h = _rms_fwd(x, p["norm"], "ffn_norm")
    u = _matmul(h, p["w_up"], "nn", BF16, name="ffn_up")
    tc = FFN_COL_TILE
    ncol = D_FF // tc

    def gate(rows, prevs, nexts, chans):
        ua, ug = rows
        wa, wg, ba, bg = chans
        a = _conv_fwd(ua, prevs[0], wa) + ba
        g = _conv_fwd(ug, prevs[1], wg) + bg
        return [a * _sigmoid(a) * g, a, g], []

    z, a_act, g_act = _chan_call(
        "ffn_gate", gate, m, tp, tc, ncol, row_ins=[(u, 0), (u, ncol)], prev_ins=[(u, 0), (u, ncol)],
        chan_ins=[(p["conv_w"], 0), (p["conv_w"], ncol), (p["conv_b"], 0), (p["conv_b"], ncol)],
        row_outs=[(BF16,), (BF16,), (BF16,)])
    out = _matmul(z, p["w_down"], "nn", F32, residual=x, name="ffn_down")
    return out, (x, h, u, z, a_act, g_act)


def _ffn_bwd(dout, p, saved, m, tp):
    x, h, u, z, a_act, g_act = saved
    tc = FFN_COL_TILE
    ncol = D_FF // tc
    dz = _matmul(dout, p["w_down"], "nt", F32, name="ffn_down_dx")
    d_w_down = _matmul(z, dout, "tn", F32, name="ffn_down_dw")

    def act_bwd(a, g, dzv):
        sg = _sigmoid(a)
        return dzv * g * (sg * (1.0 + a * (1.0 - sg))), dzv * a * sg

    def gate_bwd(rows, prevs, nexts, chans):
        ua, ug, dzv, a, g = rows
        da, dg = act_bwd(a, g, dzv)
        da_next, dg_next = act_bwd(nexts[1], nexts[2], nexts[0])
        ups_a = [_shift_up(da, da_next, 2 - k) for k in range(3)]
        ups_g = [_shift_up(dg, dg_next, 2 - k) for k in range(3)]
        return ([_conv_taps(ups_a, chans[0]), _conv_taps(ups_g, chans[1])],
                [_conv_dw_taps(ua, ups_a), _conv_dw_taps(ug, ups_g),
                 jnp.sum(da, axis=0, keepdims=True), jnp.sum(dg, axis=0, keepdims=True)])

    dua, dug, dcw_a, dcw_g, dcb_a, dcb_g = _chan_call(
        "ffn_gate_bwd", gate_bwd, m, tp, tc, ncol,
        row_ins=[(u, 0), (u, ncol), (dz, 0), (a_act, 0), (g_act, 0)], next_ins=[(dz, 0), (a_act, 0), (g_act, 0)],
        chan_ins=[(p["conv_w"], 0), (p["conv_w"], ncol)],
        row_outs=[(BF16,), (BF16,)], red_outs=[(SUBLANES,), (SUBLANES,), (1,), (1,)], row_split=2)
    d_w_up = jnp.concatenate([_matmul(h, dua, "tn", F32, name="ffn_up_dw_a"),
                              _matmul(h, dug, "tn", F32, name="ffn_up_dw_g")], axis=1)
    dh = _matmul(dua, p["w_up"], "nt", F32, name="ffn_up_dx_a")
    dh = _matmul(dug, p["w_up"], "nt", F32, residual=dh, name="ffn_up_dx_g", b_col_off=D_FF)
    dx, d_norm = _rms_bwd(x, p["norm"], dh, dout, "ffn_norm_bwd")
    d_conv_w = jnp.concatenate([dcw_a[:3], dcw_g[:3]], axis=1)
    d_conv_b = jnp.concatenate([dcb_a, dcb_g], axis=1)
    return dx, dict(norm=d_norm, w_up=d_w_up, conv_w=d_conv_w, conv_b=d_conv_b, w_down=d_w_down)


def _to_scan(x, nb, tp):
    return x.reshape(nb, tp, LRU_WIDTH // LANES, LANES).transpose(1, 0, 2, 3).reshape(tp, -1, LANES)


def _from_scan(x, nb, tp):
    return x.reshape(tp, nb, LRU_WIDTH // LANES, LANES).transpose(1, 0, 2, 3).reshape(nb * tp, LRU_WIDTH)


def _scan_fwd(a, u):
    t_len, s, _ = a.shape
    tc = _pick(t_len, 640)
    blk = pl.BlockSpec((tc, s, LANES), lambda i: (i, 0, 0))

    def body(a_ref, u_ref, h_ref, carry):
        @pl.when(pl.program_id(0) == 0)
        def _():
            carry[...] = jnp.zeros_like(carry)

        def step(t, h):
            h = a_ref[t] * h + u_ref[t]
            h_ref[t] = h
            return h

        carry[...] = lax.fori_loop(0, tc, step, carry[...], unroll=8)

    return pl.pallas_call(
        body, out_shape=jax.ShapeDtypeStruct(a.shape, F32), grid=(t_len // tc,), in_specs=[blk, blk], out_specs=blk,
        scratch_shapes=[pltpu.VMEM((s, LANES), F32)], compiler_params=_cparams(1), name="lru_scan")(a, u)


def _scan_bwd(dh, a, h_prev):
    t_len, s, _ = a.shape
    tc = _pick(t_len, 640)
    nb = t_len // tc
    blk = pl.BlockSpec((tc, s, LANES), lambda i: (nb - 1 - i, 0, 0))

    def body(dh_ref, a_ref, hp_ref, du_ref, da_ref, carry):
        @pl.when(pl.program_id(0) == 0)
        def _():
            carry[...] = jnp.zeros_like(carry)

        def step(k, c):
            t = tc - 1 - k
            d = dh_ref[t] + c
            du_ref[t] = d
            da_ref[t] = d * hp_ref[t]
            return a_ref[t] * d

        carry[...] = lax.fori_loop(0, tc, step, carry[...], unroll=8)

    shp = jax.ShapeDtypeStruct(a.shape, F32)
    return pl.pallas_call(
        body, out_shape=(shp, shp), grid=(nb,), in_specs=[blk, blk, blk], out_specs=(blk, blk),
        scratch_shapes=[pltpu.VMEM((s, LANES), F32)], compiler_params=_cparams(1), name="lru_scan_bwd")(dh, a, h_prev)


def _lru_gates(xc, zr, zi, r_b, i_b, lam):
    r = _sigmoid(zr + r_b)
    ig = _sigmoid(zi + i_b)
    sp = _softplus_neg(lam)
    log_a = -LRU_C * r * sp
    a = jnp.exp(log_a)
    mult = jnp.sqrt(-_expm1(2.0 * log_a))
    return r, ig, sp, a, mult


def _even_fwd(x, p, m, tp, nb):
    c = LRU_WIDTH
    h = _rms_fwd(x, p["norm"], "ev_norm")
    u = _matmul(h, p["w_in"], "nn", F32, name="ev_in")

    def pre(rows, prevs, nexts, chans):
        gb, gc, xa, xb = rows
        wa, wb, bias = chans
        pa = gc * xa
        ya = gb * _conv_fwd(pa, prevs[0] * prevs[1], wa)
        xc = _conv_fwd(xb, prevs[2], wb) + bias
        return [ya, xc], []

    ya, xc = _chan_call("ev_pre", pre, m, tp, c, 1, row_ins=[(u, 0), (u, 1), (u, 2), (u, 3)],
                        prev_ins=[(u, 1), (u, 2), (u, 3)],
                        chan_ins=[(p["conv_a"], 0), (p["conv_b"], 0), (p["conv_b_bias"], 0)],
                        row_outs=[(BF16,), (F32,)])
    zr = _matmul(xc, p["gate_r"], "nn", F32, name="ev_gate_r")
    zi = _matmul(xc, p["gate_i"], "nn", F32, name="ev_gate_i")

    def lru_in(rows, prevs, nexts, chans):
        xcv, zrv, ziv = rows
        r, ig, sp, a, mult = _lru_gates(xcv, zrv, ziv, *chans)
        return [a, mult * (ig * xcv)], []

    a, uu = _chan_call("ev_lru_in", lru_in, m, tp, c, 1, row_ins=[(xc, 0), (zr, 0), (zi, 0)],
                       chan_ins=[(p["gate_r_b"], 0), (p["gate_i_b"], 0), (p["lam"], 0)],
                       row_outs=[(F32,), (F32,)])
    a_s = _to_scan(a, nb, tp)
    hs_s = _scan_fwd(a_s, _to_scan(uu, nb, tp))
    hs = _from_scan(hs_s, nb, tp)

    def post(rows, prevs, nexts, chans):
        gate, hv = rows
        return [_gelu(gate) * hv], []

    (yb,) = _chan_call("ev_post", post, m, tp, c, 1, row_ins=[(u, 4), (hs, 0)], row_outs=[(BF16,)])
    out = _matmul(ya, p["w_out_a"], "nn", F32, residual=x, name="ev_out_a")
    out = _matmul(yb, p["w_out_b"], "nn", F32, residual=out, name="ev_out_b")
    return out, (x, h, u, ya, xc, zr, zi, a_s, hs_s, hs, yb)


def _even_bwd(dout, p, saved, m, tp, nb):
    c = LRU_WIDTH
    x, h, u, ya, xc, zr, zi, a_s, hs_s, hs, yb = saved
    dy = _matmul(dout, p["w_out"], "nt", F32, name="ev_out_dx")
    d_w_out = jnp.concatenate([_matmul(ya, dout, "tn", F32, name="ev_out_dw_a"),
                               _matmul(yb, dout, "tn", F32, name="ev_out_dw_b")], axis=0)

    def post_bwd(rows, prevs, nexts, chans):
        dyb, gate, hv = rows
        return [dyb * hv * _gelu_grad(gate), dyb * _gelu(gate)], []

    dgate, dhs = _chan_call("ev_post_bwd", post_bwd, m, tp, c, 1, row_ins=[(dy, 1), (u, 4), (hs, 0)],
                            row_outs=[(F32,), (F32,)])
    h_prev = jnp.concatenate([jnp.zeros_like(hs_s[:1]), hs_s[:-1]], axis=0)
    du_s, da_s = _scan_bwd(_to_scan(dhs, nb, tp), a_s, h_prev)
    du = _from_scan(du_s, nb, tp)
    da = _from_scan(da_s, nb, tp)

    def lru_in_bwd(rows, prevs, nexts, chans):
        duv, dav, xcv, zrv, ziv = rows
        r, ig, sp, a, mult = _lru_gates(xcv, zrv, ziv, *chans)
        dxc = duv * mult * ig
        dig = duv * mult * xcv
        dmult = duv * ig * xcv
        dlog_a = dav * a - dmult * (a * a) / jnp.maximum(mult, 1e-30)
        dr = dlog_a * (-LRU_C * sp)
        dzr = dr * r * (1.0 - r)
        dzi = dig * ig * (1.0 - ig)
        dsp = jnp.sum(dlog_a * (-LRU_C * r), axis=0, keepdims=True)
        dlam = -dsp * _sigmoid(-chans[2])
        return ([dzr, dzi, dxc],
                [jnp.sum(dzr, axis=0, keepdims=True), jnp.sum(dzi, axis=0, keepdims=True), dlam])

    dzr, dzi, dxc, d_r_b, d_i_b, d_lam = _chan_call(
        "ev_lru_in_bwd", lru_in_bwd, m, tp, c, 1, row_ins=[(du, 0), (da, 0), (xc, 0), (zr, 0), (zi, 0)],
        chan_ins=[(p["gate_r_b"], 0), (p["gate_i_b"], 0), (p["lam"], 0)],
        row_outs=[(F32,), (F32,), (F32,)], red_outs=[(1,), (1,), (1,)])
    d_gate_r = _matmul(xc, dzr, "tn", F32, name="ev_gate_r_dw")
    d_gate_i = _matmul(xc, dzi, "tn", F32, name="ev_gate_i_dw")
    dxc = _matmul(dzr, p["gate_r"], "nt", F32, residual=dxc, name="ev_gate_r_dx")
    dxc = _matmul(dzi, p["gate_i"], "nt", F32, residual=dxc, name="ev_gate_i_dx")

    def conv_b_bwd(rows, prevs, nexts, chans):
        dxcv, xb = rows
        return ([_conv_dx(dxcv, nexts[0], chans[0])],
                [_conv_dw(dxcv, xb, prevs[0], 4), jnp.sum(dxcv, axis=0, keepdims=True)])

    dxb, d_conv_b, d_bias = _chan_call(
        "ev_conv_b_bwd", conv_b_bwd, m, tp, c, 1, row_ins=[(dxc, 0), (u, 3)], prev_ins=[(u, 3)], next_ins=[(dxc, 0)],
        chan_ins=[(p["conv_b"], 0)], row_outs=[(F32,)], red_outs=[(SUBLANES,), (1,)])

    def mix_a_bwd(rows, prevs, nexts, chans):
        dya, gb, gc, xa = rows
        (wa,) = chans
        taps = _taps(gc * xa, prevs[0] * prevs[1], 3)
        ca = _conv_taps(taps, wa)
        dca = dya * gb
        dpa = _conv_dx(dca, nexts[0] * nexts[1], wa)
        return [dya * ca, dpa * xa, dpa * gc], [_conv_dw_taps(dca, taps)]

    dgb, dgc, dxa, d_conv_a = _chan_call(
        "ev_mix_a_bwd", mix_a_bwd, m, tp, c, 1, row_ins=[(dy, 0), (u, 0), (u, 1), (u, 2)],
        prev_ins=[(u, 1), (u, 2)], next_ins=[(dy, 0), (u, 0)], chan_ins=[(p["conv_a"], 0)],
        row_outs=[(F32,), (F32,), (F32,)], red_outs=[(SUBLANES,)])
    du_all = jnp.concatenate([dgb, dgc, dxa, dxb, dgate], axis=1)
    d_w_in = _matmul(h, du_all, "tn", F32, name="ev_in_dw")
    dh = _matmul(du_all, p["w_in"], "nt", F32, name="ev_in_dx")
    dx, d_norm = _rms_bwd(x, p["norm"], dh, dout, "ev_norm_bwd")
    return dx, dict(norm=d_norm, w_in=d_w_in, conv_a=d_conv_a[:3], conv_b=d_conv_b[:4], conv_b_bias=d_bias,
                    gate_r=d_gate_r, gate_r_b=d_r_b, gate_i=d_gate_i, gate_i_b=d_i_b, lam=d_lam, w_out=d_w_out)


def _rope_tables(tp):
    pos = jnp.arange(tp, dtype=F32)
    inv_freq = ROPE_BASE ** (-jnp.arange(0, QK_ROPE, 2, dtype=F32) / QK_ROPE)
    ang = pos[:, None] * inv_freq[None, :]
    cos, sin = jnp.cos(ang), jnp.sin(ang)
    half = QK_ROPE // 2
    one = jnp.ones((tp, QK_NOPE), F32)
    z64 = jnp.zeros((tp, QK_NOPE), F32)
    zh = jnp.zeros((tp, half), F32)
    zt = jnp.zeros((tp, HEAD_PAD - QK_HEAD), F32)
    c_tab = jnp.concatenate([one, cos, cos, zt], axis=1)
    s_lo = jnp.concatenate([z64, -sin, zh, zt], axis=1)
    s_hi = jnp.concatenate([z64, zh, sin, zt], axis=1)
    return c_tab, s_lo, s_hi


def _rope(v, c_tab, s_lo, s_hi):
    half = QK_ROPE // 2
    return v * c_tab + pltpu.roll(v, HEAD_PAD - half, 1) * s_lo + pltpu.roll(v, half, 1) * s_hi


def _rope_t(dv, c_tab, s_lo, s_hi):
    half = QK_ROPE // 2
    return dv * c_tab + pltpu.roll(dv * s_lo, half, 1) + pltpu.roll(dv * s_hi, HEAD_PAD - half, 1)


def _rope_call(name, fn, m, tp, ins, tables, out_dtype, shared_pre=None):
    tm = _pick(tp, ROW_TILE)
    tps = tp // tm
    n = len(ins)
    width = MLA_HEADS * HEAD_PAD

    def body(*refs):
        tabs = [r[...] for r in refs[n:n + 3]]
        shared = [None if fc is None else shared_pre(refs[a][...].astype(F32), *tabs) for a, (_, fc) in enumerate(ins)]
        for hh in range(MLA_HEADS):
            lanes = slice(hh * HEAD_PAD, (hh + 1) * HEAD_PAD)
            vals = [refs[a][:, lanes].astype(F32) if shared[a] is None else shared[a] for a in range(n)]
            refs[n + 3][:, lanes] = fn(*vals, *tabs).astype(out_dtype)

    in_specs, args = [], []
    for arr, fixed_col in ins:
        if fixed_col is None:
            in_specs.append(pl.BlockSpec((tm, width), lambda i: (i, 0)))
        else:
            in_specs.append(pl.BlockSpec((tm, HEAD_PAD), lambda i, fc=fixed_col: (i, fc)))
        args.append(arr)
    for tab in tables:
        in_specs.append(pl.BlockSpec((tm, HEAD_PAD), lambda i: (lax.rem(i, tps), 0)))
        args.append(tab)
    return pl.pallas_call(
        body, out_shape=jax.ShapeDtypeStruct((m, width), out_dtype), grid=(m // tm,),
        in_specs=in_specs, out_specs=pl.BlockSpec((tm, width), lambda i: (i, 0)),
        compiler_params=_cparams(1), name=name)(*args)


def _rope_k_bwd(dk, tables, m, tp):
    tm = _pick(tp, ROW_TILE)
    tps = tp // tm

    def body(dk_ref, c_ref, lo_ref, hi_ref, o_ref):
        acc = dk_ref[:, 0:HEAD_PAD].astype(F32)
        for hh in range(1, MLA_HEADS):
            acc = acc + dk_ref[:, hh * HEAD_PAD:(hh + 1) * HEAD_PAD].astype(F32)
        d = pltpu.roll(_rope_t(acc, c_ref[...], lo_ref[...], hi_ref[...]), QK_NOPE, 1)
        lane = lax.broadcasted_iota(jnp.int32, d.shape, 1)
        o_ref[...] = jnp.where(lane < QK_ROPE, d, 0.0)

    tab = pl.BlockSpec((tm, HEAD_PAD), lambda i: (lax.rem(i, tps), 0))
    return pl.pallas_call(
        body, out_shape=jax.ShapeDtypeStruct((m, HEAD_PAD), F32), grid=(m // tm,),
        in_specs=[pl.BlockSpec((tm, MLA_HEADS * HEAD_PAD), lambda i: (i, 0)), tab, tab, tab],
        out_specs=pl.BlockSpec((tm, HEAD_PAD), lambda i: (i, 0)), compiler_params=_cparams(1),
        name="od_rope_k_bwd")(dk, *tables)


def _causal_mask(row0, col0, shape):
    rows = row0 + lax.broadcasted_iota(jnp.int32, shape, 0)
    cols = col0 + lax.broadcasted_iota(jnp.int32, shape, 1)
    return cols <= rows


NT = (((1,), (1,)), ((), ()))
TN = (((0,), (0,)), ((), ()))
HEADS_PER_STEP = 2
HEAD_STEPS = MLA_HEADS // HEADS_PER_STEP
STEP_LANES = HEADS_PER_STEP * HEAD_PAD


def _flash_fwd(q, k, v, nb, tp):
    tq = _pick(tp, ROW_TILE)
    nq = tp // tq

    def body(q_ref, k_ref, v_ref, o_ref, lse_ref):
        i = pl.program_id(2)
        qbs = [q_ref[:, hd * HEAD_PAD:(hd + 1) * HEAD_PAD] for hd in range(HEADS_PER_STEP)]

        def chunk(j, carry, masked, width=1):
            off = pl.multiple_of(j * tq, tq)
            out = []
            for hd in range(HEADS_PER_STEP):
                mx, l, acc = carry[hd]
                lanes = slice(hd * HEAD_PAD, (hd + 1) * HEAD_PAD)
                kb = k_ref[pl.ds(off, width * tq), lanes]
                vb = v_ref[pl.ds(off, width * tq), lanes]
                s = lax.dot_general(qbs[hd], kb, NT, preferred_element_type=F32)
                if masked:
                    s = jnp.where(_causal_mask(0, 0, s.shape), s, NEG)
                m_new = jnp.maximum(mx, jnp.max(s, axis=1, keepdims=True))
                alpha = jnp.exp(mx - m_new)
                pr = jnp.exp(s - m_new)
                l = alpha * l + jnp.sum(pr, axis=1, keepdims=True)
                acc = alpha * acc + jnp.dot(pr.astype(BF16), vb, preferred_element_type=F32)
                out.append((m_new, l, acc))
            return tuple(out)

        one = (jnp.full((tq, 1), NEG, F32), jnp.zeros((tq, 1), F32), jnp.zeros((tq, HEAD_PAD), F32))
        quads = i // 4
        carry = lax.fori_loop(0, quads, lambda jj, c: chunk(4 * jj, c, False, 4), (one,) * HEADS_PER_STEP)
        carry = lax.fori_loop(0, lax.rem(i, 4) // 2, lambda _, c: chunk(4 * quads, c, False, 2), carry)
        carry = lax.fori_loop(0, lax.rem(i, 2), lambda _, c: chunk(i - 1, c, False), carry)
        carry = chunk(i, carry, True)
        for hd in range(HEADS_PER_STEP):
            mx, l, acc = carry[hd]
            lanes = slice(hd * HEAD_PAD, (hd + 1) * HEAD_PAD)
            o_ref[:, lanes] = (acc / l).astype(o_ref.dtype)
            lse_ref[:, lanes] = jnp.broadcast_to(mx + jnp.log(l), (tq, HEAD_PAD))

    qspec = pl.BlockSpec((tq, STEP_LANES), lambda b, hh, i: (b * nq + i, hh))
    kvspec = pl.BlockSpec((tp, STEP_LANES), lambda b, hh, i: (b, hh))
    shp = (nb * tp, MLA_HEADS * HEAD_PAD)
    return pl.pallas_call(
        body, out_shape=(jax.ShapeDtypeStruct(shp, BF16), jax.ShapeDtypeStruct(shp, F32)),
        grid=(nb, HEAD_STEPS, nq), in_specs=[qspec, kvspec, kvspec], out_specs=(qspec, qspec),
        compiler_params=_cparams(3), name="od_flash_fwd")(q, k, v)


def _flash_prep(o, do, lse_c, nb, tp):
    tq = _pick(tp, ROW_TILE)
    nq = tp // tq

    def body(o_ref, do_ref, lse_ref, lr_ref, dr_ref):
        for hh in range(MLA_HEADS):
            lanes = slice(hh * HEAD_PAD, (hh + 1) * HEAD_PAD)
            delta = jnp.sum(o_ref[:, lanes].astype(F32) * do_ref[:, lanes].astype(F32), axis=1, keepdims=True)
            lr_ref[hh] = jnp.transpose(lse_ref[:, lanes])[0:SUBLANES, :]
            dr_ref[hh] = jnp.transpose(jnp.broadcast_to(delta, (tq, HEAD_PAD)))[0:SUBLANES, :]

    qspec = pl.BlockSpec((tq, MLA_HEADS * HEAD_PAD), lambda b, i: (b * nq + i, 0))
    rspec = pl.BlockSpec((MLA_HEADS, None, SUBLANES, tq), lambda b, i: (b, i, 0, 0))
    rshape = jax.ShapeDtypeStruct((nb * MLA_HEADS, nq, SUBLANES, tq), F32)
    return pl.pallas_call(
        body, out_shape=(rshape, rshape), grid=(nb, nq), in_specs=[qspec, qspec, qspec],
        out_specs=(rspec, rspec), compiler_params=_cparams(2), name="od_flash_prep")(o, do, lse_c)


def _flash_bwd(q, k, v, do, lse_r, delta_r, nb, tp):
    tq = _pick(tp, ROW_TILE)
    nq = tp // tq

    def body(q_ref, k_ref, v_ref, do_ref, lse_ref, dl_ref, dq_ref, dk_ref, dv_ref):
        j = pl.program_id(2)

        @pl.when(j == 0)
        def _():
            dq_ref[...] = jnp.zeros_like(dq_ref)

        kbs = [k_ref[:, hd * HEAD_PAD:(hd + 1) * HEAD_PAD] for hd in range(HEADS_PER_STEP)]
        vbs = [v_ref[:, hd * HEAD_PAD:(hd + 1) * HEAD_PAD] for hd in range(HEADS_PER_STEP)]

        def chunk(i, carry, masked, width=1):
            off = pl.multiple_of(i * tq, tq)
            out = []
            for hd in range(HEADS_PER_STEP):
                dk, dv = carry[hd]
                lanes = slice(hd * HEAD_PAD, (hd + 1) * HEAD_PAD)
                qb = q_ref[pl.ds(off, width * tq), lanes]
                dob = do_ref[pl.ds(off, width * tq), lanes]
                lse = jnp.concatenate([lse_ref[hd, i + w][0:1, :] for w in range(width)], axis=1)
                delta = jnp.concatenate([dl_ref[hd, i + w][0:1, :] for w in range(width)], axis=1)
                st = lax.dot_general(kbs[hd], qb, NT, preferred_element_type=F32)
                pt = jnp.exp(st - lse)
                if masked:
                    keys = lax.broadcasted_iota(jnp.int32, st.shape, 0)
                    queries = lax.broadcasted_iota(jnp.int32, st.shape, 1)
                    pt = jnp.where(keys <= queries, pt, 0.0)
                dv = dv + jnp.dot(pt.astype(BF16), dob, preferred_element_type=F32)
                dpt = lax.dot_general(vbs[hd], dob, NT, preferred_element_type=F32)
                dst = (pt * (dpt - delta)).astype(BF16)
                dk = dk + jnp.dot(dst, qb, preferred_element_type=F32)
                dq_ref[pl.ds(off, width * tq), lanes] += lax.dot_general(dst, kbs[hd], TN,
                                                                         preferred_element_type=F32)
                out.append((dk, dv))
            return tuple(out)

        zero = jnp.zeros((tq, HEAD_PAD), F32)
        rest = nq - 1 - j
        carry = chunk(j, ((zero, zero),) * HEADS_PER_STEP, True)
        carry = lax.fori_loop(0, rest // 2, lambda pp, c: chunk(j + 1 + 2 * pp, c, False, 2), carry)
        carry = lax.fori_loop(0, lax.rem(rest, 2), lambda _, c: chunk(nq - 1, c, False), carry)
        for hd in range(HEADS_PER_STEP):
            lanes = slice(hd * HEAD_PAD, (hd + 1) * HEAD_PAD)
            dk_ref[:, lanes] = carry[hd][0]
            dv_ref[:, lanes] = carry[hd][1].astype(dv_ref.dtype)

    tspec = pl.BlockSpec((tq, STEP_LANES), lambda b, hh, j: (b * nq + j, hh))
    fullspec = pl.BlockSpec((tp, STEP_LANES), lambda b, hh, j: (b, hh))
    rspec = pl.BlockSpec((HEADS_PER_STEP, nq, SUBLANES, tq), lambda b, hh, j: (b * HEAD_STEPS + hh, 0, 0, 0))
    shp = (nb * tp, MLA_HEADS * HEAD_PAD)
    return pl.pallas_call(
        body, out_shape=(jax.ShapeDtypeStruct(shp, F32), jax.ShapeDtypeStruct(shp, F32),
                         jax.ShapeDtypeStruct(shp, BF16)),
        grid=(nb, HEAD_STEPS, nq), in_specs=[fullspec, tspec, tspec, fullspec, rspec, rspec],
        out_specs=(fullspec, tspec, tspec), compiler_params=_cparams(3),
        name="od_flash_bwd")(q, k, v, do, lse_r, delta_r)


def _odd_fwd(x, p, tables, m, tp, nb):
    scale = QK_HEAD ** -0.5
    h = _rms_fwd(x, p["norm"], "od_norm")
    u = _matmul(h, p["w_in"], "nn", F32, name="od_in")
    cq = u[:, :Q_LORA]
    ckv = u[:, Q_LORA:Q_LORA + KV_LORA]
    cqn = _rms_fwd(cq, p["q_norm"], "od_q_norm")
    ckvn = _rms_fwd(ckv, p["kv_norm"], "od_kv_norm")
    q_raw = _matmul(cqn, p["w_uq"], "nn", F32, name="od_uq")
    k_raw = _matmul(ckvn, p["w_uk"], "nn", F32, name="od_uk")
    v = _matmul(ckvn, p["w_uv"], "nn", BF16, name="od_uv")
    q = _rope_call("od_rope_q", lambda qv, c, lo, hi: _rope(qv, c, lo, hi) * scale, m, tp, [(q_raw, None)], tables,
                   BF16)
    kr_col = (Q_LORA + KV_LORA) // HEAD_PAD
    k = _rope_call("od_rope_k", lambda kv, kr, c, lo, hi: kv + kr, m, tp, [(k_raw, None), (u, kr_col)], tables, BF16,
                   shared_pre=lambda uv, c, lo, hi: _rope(pltpu.roll(uv, QK_NOPE, 1), c, lo, hi))
    o, lse_c = _flash_fwd(q, k, v, nb, tp)
    out = _matmul(o, p["w_out"], "nn", F32, residual=x, name="od_out")
    return out, (x, h, cq, ckv, cqn, ckvn, q, k, v, o, lse_c)


def _odd_bwd(dout, p, tables, saved, m, tp, nb):
    scale = QK_HEAD ** -0.5
    x, h, cq, ckv, cqn, ckvn, q, k, v, o, lse_c = saved
    do = _matmul(dout, p["w_out"], "nt", BF16, name="od_out_dx")
    d_w_out = _matmul(o, dout, "tn", F32, name="od_out_dw")
    lse_r, delta_r = _flash_prep(o, do, lse_c, nb, tp)
    dq, dk, dv = _flash_bwd(q, k, v, do, lse_r, delta_r, nb, tp)
    dq_raw = _rope_call("od_rope_q_bwd", lambda d, c, lo, hi: _rope_t(d, c, lo, hi) * scale, m, tp, [(dq, None)],
                        tables, BF16)
    dkr = _rope_k_bwd(dk, tables, m, tp)
    d_w_uq = _matmul(cqn, dq_raw, "tn", F32, name="od_uq_dw")
    d_w_uk = _matmul(ckvn, dk, "tn", F32, name="od_uk_dw")
    d_w_uv = _matmul(ckvn, dv, "tn", F32, name="od_uv_dw")
    dcqn = _matmul(dq_raw, p["w_uq"], "nt", F32, name="od_uq_dx")
    dckvn = _matmul(dk, p["w_uk"], "nt", F32, name="od_uk_dx")
    dckvn = _matmul(dv, p["w_uv"], "nt", F32, residual=dckvn, name="od_uv_dx")
    dcq, d_q_norm = _rms_bwd(cq, p["q_norm"], dcqn, None, "od_q_norm_bwd")
    dckv, d_kv_norm = _rms_bwd(ckv, p["kv_norm"], dckvn, None, "od_kv_norm_bwd")
    du = jnp.concatenate([dcq, dckv, dkr], axis=1)
    d_w_in = _matmul(h, du, "tn", F32, name="od_in_dw")
    dh = _matmul(du, p["w_in"], "nt", F32, name="od_in_dx")
    dx, d_norm = _rms_bwd(x, p["norm"], dh, dout, "od_norm_bwd")
    return dx, dict(norm=d_norm, w_in=d_w_in, q_norm=d_q_norm, kv_norm=d_kv_norm, w_uq=d_w_uq, w_uk=d_w_uk,
                    w_uv=d_w_uv, w_out=d_w_out)


def _loss_head(hf, g, target, tp, t_real):
    m, c = hf.shape
    tm = _pick(tp, ROW_TILE)
    tps = tp // tm

    def body(x_ref, g_ref, t_ref, dx_ref, dg_ref, loss_ref):
        i = pl.program_id(0)
        xf = x_ref[...]
        r = lax.rsqrt(jnp.mean(xf * xf, axis=-1, keepdims=True) + EPS)
        xn = xf * r
        t_pos = lax.rem(i, tps) * tm + lax.broadcasted_iota(jnp.int32, (tm, 1), 0)
        valid = jnp.logical_and(t_pos >= N_META, t_pos < t_real)
        err = jnp.where(valid, xn * g_ref[...] - t_ref[...], 0.0)
        dyf = err * (1.0 / c)
        dyg = dyf * g_ref[...]
        dx_ref[...] = r * (dyg - xn * jnp.mean(dyg * xn, axis=-1, keepdims=True))

        @pl.when(i == 0)
        def _():
            dg_ref[...] = jnp.zeros_like(dg_ref)
            loss_ref[...] = jnp.zeros_like(loss_ref)

        dg_ref[...] += jnp.sum(dyf * xn, axis=0, keepdims=True)
        loss_ref[...] += (0.5 / c) * jnp.sum(jnp.sum(err * err, axis=1, keepdims=True), axis=0, keepdims=True)

    row = pl.BlockSpec((tm, c), lambda i: (i, 0))
    vec = pl.BlockSpec((1, c), lambda i: (0, 0))
    return pl.pallas_call(
        body, out_shape=(jax.ShapeDtypeStruct((m, c), F32), jax.ShapeDtypeStruct((1, c), F32),
                         jax.ShapeDtypeStruct((1, 1), F32)),
        grid=(m // tm,), in_specs=[row, vec, row], out_specs=(row, vec, pl.BlockSpec((1, 1), lambda i: (0, 0))),
        compiler_params=_cparams(1), name="loss_head")(hf, g, target)


def _meta_grad(dh0, nb, tp):
    d = dh0.shape[1]

    def body(x_ref, o_ref):
        @pl.when(pl.program_id(0) == 0)
        def _():
            o_ref[...] = jnp.zeros_like(o_ref)

        o_ref[...] += x_ref[...]

    return pl.pallas_call(
        body, out_shape=jax.ShapeDtypeStruct((N_META, d), F32), grid=(nb,),
        in_specs=[pl.BlockSpec((N_META, d), lambda b: (b * (tp // N_META), 0))],
        out_specs=pl.BlockSpec((N_META, d), lambda b: (0, 0)), compiler_params=_cparams(1), name="meta_grad")(dh0)


def _mesh_pos():
    x, y, c = lax.axis_index("x"), lax.axis_index("y"), lax.axis_index("c")
    return x, y, c


N_CHIP = 4
MESH_ID = pl.DeviceIdType.MESH


def _peer_chip(x, y, k):
    px = 1 - x if k & 2 else x
    py = 1 - y if k & 1 else y
    return px, py


def _all_gather(arrays):
    n = len(arrays)

    def body(*refs):
        srcs, outs = refs[:n], refs[n:2 * n]
        send_sems, recv_sems, local_sems = refs[2 * n:]
        x, y, c = _mesh_pos()
        me = 4 * x + 2 * y + c
        sibling = (x, y, 1 - c)

        def copy(a, sem, src, block, to):
            return pltpu.make_async_remote_copy(
                src_ref=src, dst_ref=outs[a].at[block], send_sem=send_sems.at[a, sem], recv_sem=recv_sems.at[a, sem],
                device_id=to, device_id_type=MESH_ID)

        local = [pltpu.make_async_copy(srcs[a], outs[a].at[me], local_sems.at[a]) for a in range(n)]
        for cp in local:
            cp.start()
        sends = [copy(a, 0, srcs[a], me, sibling) for a in range(n)]
        for k in range(1, N_CHIP):
            px, py = _peer_chip(x, y, k)
            sends += [copy(a, k, srcs[a], me, (px, py, c)) for a in range(n)]
        for cp in sends:
            cp.start()
        for k in range(1, N_CHIP):
            px, py = _peer_chip(x, y, k)
            block = 4 * px + 2 * py + c
            for a in range(n):
                copy(a, k, srcs[a], block, sibling).wait_recv()
            passed = [copy(a, N_CHIP - 1 + k, outs[a].at[block], block, sibling) for a in range(n)]
            for cp in passed:
                cp.start()
            sends += passed
        for a in range(n):
            copy(a, 0, srcs[a], 4 * x + 2 * y + (1 - c), sibling).wait_recv()
        for k in range(1, N_CHIP):
            px, py = _peer_chip(x, y, k)
            for a in range(n):
                copy(a, N_CHIP - 1 + k, srcs[a], 4 * px + 2 * py + (1 - c), sibling).wait_recv()
        for cp in sends:
            cp.wait_send()
        for cp in local:
            cp.wait()

    any_spec = pl.BlockSpec(memory_space=pl.ANY)
    out_shape = tuple(jax.ShapeDtypeStruct((N_DEV,) + a.shape, a.dtype) for a in arrays)
    return pl.pallas_call(
        body, out_shape=out_shape, in_specs=[any_spec] * n, out_specs=(any_spec,) * n,
        scratch_shapes=[pltpu.SemaphoreType.DMA((n, N_DEV - 1)), pltpu.SemaphoreType.DMA((n, N_DEV - 1)),
                        pltpu.SemaphoreType.DMA((n,))],
        name="weight_all_gather")(*arrays)


def _pair_exchange(arrays):
    n = len(arrays)

    def body(*refs):
        srcs, outs = refs[:n], refs[n:2 * n]
        send_sems, recv_sems = refs[2 * n:]
        x, y, c = _mesh_pos()
        copies = [pltpu.make_async_remote_copy(
            src_ref=srcs[a], dst_ref=outs[a], send_sem=send_sems.at[a], recv_sem=recv_sems.at[a],
            device_id=(x, y, 1 - c), device_id_type=MESH_ID) for a in range(n)]
        for cp in copies:
            cp.start()
        for cp in copies:
            cp.wait()

    any_spec = pl.BlockSpec(memory_space=pl.ANY)
    return pl.pallas_call(
        body, out_shape=tuple(jax.ShapeDtypeStruct(a.shape, a.dtype) for a in arrays), in_specs=[any_spec] * n,
        out_specs=(any_spec,) * n, scratch_shapes=[pltpu.SemaphoreType.DMA((n,)), pltpu.SemaphoreType.DMA((n,))],
        name="grad_pair_exchange")(*arrays)


def _chip_exchange(arrays):
    n = len(arrays)

    def body(*refs):
        srcs, outs = refs[:n], refs[n:2 * n]
        send_sems, recv_sems, local_sems = refs[2 * n:]
        x, y, c = _mesh_pos()
        q = 2 * x + y
        local = [pltpu.make_async_copy(srcs[a].at[q], outs[a].at[q], local_sems.at[a]) for a in range(n)]
        for cp in local:
            cp.start()

        def copy(a, k, to_q, from_q, px, py):
            return pltpu.make_async_remote_copy(
                src_ref=srcs[a].at[to_q], dst_ref=outs[a].at[from_q], send_sem=send_sems.at[a, k - 1],
                recv_sem=recv_sems.at[a, k - 1], device_id=(px, py, c), device_id_type=MESH_ID)

        sends = []
        for k in range(1, N_CHIP):
            px, py = _peer_chip(x, y, k)
            sends += [copy(a, k, 2 * px + py, q, px, py) for a in range(n)]
        for cp in sends:
            cp.start()
        for k in range(1, N_CHIP):
            px, py = _peer_chip(x, y, k)
            for a in range(n):
                copy(a, k, q, 2 * px + py, px, py).wait_recv()
        for cp in sends:
            cp.wait_send()
        for cp in local:
            cp.wait()

    any_spec = pl.BlockSpec(memory_space=pl.ANY)
    return pl.pallas_call(
        body, out_shape=tuple(jax.ShapeDtypeStruct(a.shape, a.dtype) for a in arrays), in_specs=[any_spec] * n,
        out_specs=(any_spec,) * n,
        scratch_shapes=[pltpu.SemaphoreType.DMA((n, N_CHIP - 1)), pltpu.SemaphoreType.DMA((n, N_CHIP - 1)),
                        pltpu.SemaphoreType.DMA((n,))],
        name="grad_chip_exchange")(*arrays)


REDUCE_BLOCK_BYTES = 512 * 1024


def _pair_add(a, b):
    p, r, c = a.shape
    tr = _reduce_rows(r, c)

    def body(a_ref, b_ref, o_ref):
        o_ref[...] = (a_ref[...].astype(F32) + b_ref[...].astype(F32)).astype(o_ref.dtype)

    blk = pl.BlockSpec((None, tr, c), lambda s, i: (s, i, 0))
    return pl.pallas_call(
        body, out_shape=jax.ShapeDtypeStruct(a.shape, a.dtype), grid=(p, r // tr), in_specs=[blk, blk], out_specs=blk,
        compiler_params=_cparams(2), name="grad_pair_add")(a, b)


def _reduce_rows(r, c):
    best = None
    for t in range(16, r + 1, 16):
        if r % t == 0 and t * c * 4 <= REDUCE_BLOCK_BYTES:
            best = t
    assert best is not None, (r, c)
    return best


def _reduce_adamw(parts, w, mom, vel):
    n_parts, r, c = parts.shape
    tr = _reduce_rows(r, c)
    c1 = 1.0 - ADAM_B1 ** ADAM_STEP
    c2 = 1.0 - ADAM_B2 ** ADAM_STEP

    def body(p_ref, w_ref, m_ref, v_ref, g_ref, d_ref, mo_ref, vo_ref):
        g = p_ref[0].astype(F32)
        for s in range(1, n_parts):
            g = g + p_ref[s].astype(F32)
        mn = ADAM_B1 * m_ref[...] + (1.0 - ADAM_B1) * g
        vn = ADAM_B2 * v_ref[...] + (1.0 - ADAM_B2) * (g * g)
        m_hat = mn / c1
        v_hat = vn / c2
        g_ref[...] = g
        d_ref[...] = -ADAM_LR * (m_hat / (jnp.sqrt(v_hat) + ADAM_EPS) + ADAM_WD * w_ref[...])
        mo_ref[...] = mn
        vo_ref[...] = vn

    blk = pl.BlockSpec((tr, c), lambda i: (i, 0))
    shp = jax.ShapeDtypeStruct((r, c), F32)
    return pl.pallas_call(
        body, out_shape=(shp, shp, shp, shp), grid=(r // tr,),
        in_specs=[pl.BlockSpec((n_parts, tr, c), lambda i: (0, i, 0)), blk, blk, blk], out_specs=(blk, blk, blk, blk),
        compiler_params=_cparams(1), name="reduce_adamw")(parts, w, mom, vel)


def _pack_rows(pieces, width, row_multiple, dtype):
    flat = jnp.concatenate([p.astype(dtype).reshape(-1) for p in pieces])
    rows = -(-flat.shape[0] // (width * row_multiple)) * row_multiple
    return jnp.pad(flat, (0, rows * width - flat.shape[0])).reshape(rows, width)


def _unshard(gathered, axis):
    moved = jnp.moveaxis(gathered, 0, axis)
    shape = list(moved.shape)
    shape[axis:axis + 2] = [shape[axis] * shape[axis + 1]]
    return moved.reshape(shape)


def _to_slots(full, axis):
    shape = list(full.shape)
    shape[axis:axis + 1] = [N_DEV, shape[axis] // N_DEV]
    return jnp.moveaxis(full.reshape(shape), axis, 0)


def _core_slots(full, axis, core):
    shape = list(full.shape)
    shape[axis:axis + 1] = [N_CHIP, 2, shape[axis] // N_DEV]
    picked = lax.dynamic_index_in_dim(full.reshape(shape), core, axis + 1, keepdims=False)
    return jnp.moveaxis(picked, axis, 0)


def _block_diag(w):
    hh, d, _ = w.shape
    eye = jnp.eye(hh, dtype=w.dtype)
    return (w[:, :, None, :] * eye[:, None, :, None]).reshape(hh * d, hh * d)


def _block_diag_t(full, hh):
    d = full.shape[0] // hh
    f4 = full.reshape(hh, d, hh, d)
    return jnp.stack([f4[i, :, i, :] for i in range(hh)], axis=0)


def _pad_heads(w, width):
    r = w.shape[0]
    w3 = w.reshape(r, MLA_HEADS, width)
    return jnp.pad(w3, ((0, 0), (0, 0), (0, HEAD_PAD - width))).reshape(r, MLA_HEADS * HEAD_PAD)


def _unpad_heads(w, width):
    r = w.shape[0]
    return w.reshape(r, MLA_HEADS, HEAD_PAD)[:, :, :width].reshape(r, MLA_HEADS * width)


def kernel(x, meta_tokens, ev_norm, ev_w_in, ev_conv_a, ev_conv_b, ev_conv_b_bias, ev_gate_r_w, ev_gate_r_b, ev_gate_i_w, ev_gate_i_b, ev_lru_lambda, ev_w_out, od_norm, od_w_in, od_q_norm, od_kv_norm, od_w_uq, od_w_ukv, od_w_out, ffn_norm, ffn_w_up, ffn_conv_w, ffn_conv_b, ffn_w_down, final_norm, loss_target, m_meta_tokens, m_ev_norm, m_ev_w_in, m_ev_conv_a, m_ev_conv_b, m_ev_conv_b_bias, m_ev_gate_r_w, m_ev_gate_r_b, m_ev_gate_i_w, m_ev_gate_i_b, m_ev_lru_lambda, m_ev_w_out, m_od_norm, m_od_w_in, m_od_q_norm, m_od_kv_norm, m_od_w_uq, m_od_w_ukv, m_od_w_out, m_ffn_norm, m_ffn_w_up, m_ffn_conv_w, m_ffn_conv_b, m_ffn_w_down, m_final_norm, v_meta_tokens, v_ev_norm, v_ev_w_in, v_ev_conv_a, v_ev_conv_b, v_ev_conv_b_bias, v_ev_gate_r_w, v_ev_gate_r_b, v_ev_gate_i_w, v_ev_gate_i_b, v_ev_lru_lambda, v_ev_w_out, v_od_norm, v_od_w_in, v_od_q_norm, v_od_kv_norm, v_od_w_uq, v_od_w_ukv, v_od_w_out, v_ffn_norm, v_ffn_w_up, v_ffn_conv_w, v_ffn_conv_b, v_ffn_w_down, v_final_norm):
    given = dict(locals())
    names = [n for n, _ in PARAMS]
    axis_of = dict(PARAMS)
    w_loc = {n: given[n] for n in names}
    m_loc = {n: given["m_" + n] for n in names}
    v_loc = {n: given["v_" + n] for n in names}
    sharded = [n for n in names if axis_of[n] is not None]
    replicated = [n for n in names if axis_of[n] is None]
    small = [n for n in sharded if n not in BIG]

    nb, seq, d = x.shape
    t_real = N_META + seq
    tp = -(-t_real // ROW_TILE) * ROW_TILE
    m = nb * tp

    small_pack = _pack_rows([w_loc[n] for n in small], LANES, SUBLANES, F32)
    gathered = _all_gather([w_loc[n].astype(BF16) for n in BIG] + [small_pack])
    full = {n: w_loc[n] for n in replicated}
    for n, g in zip(BIG, gathered[:-1]):
        full[n] = _unshard(g, axis_of[n])
    flat = gathered[-1].reshape(N_DEV, -1)
    off = 0
    for n in small:
        shard = w_loc[n].shape
        size = math.prod(shard)
        full[n] = _unshard(flat[:, off:off + size].reshape((N_DEV,) + shard), axis_of[n])
        off += size

    tables = _rope_tables(tp)

    def even_params(j):
        w_out = full["ev_w_out"][j]
        return dict(norm=full["ev_norm"][j][None], w_in=full["ev_w_in"][j], conv_a=full["ev_conv_a"][j],
                    conv_b=full["ev_conv_b"][j], conv_b_bias=full["ev_conv_b_bias"][j][None],
                    gate_r=_block_diag(full["ev_gate_r_w"][j]).astype(BF16),
                    gate_i=_block_diag(full["ev_gate_i_w"][j]).astype(BF16),
                    gate_r_b=full["ev_gate_r_b"][j][None], gate_i_b=full["ev_gate_i_b"][j][None],
                    lam=full["ev_lru_lambda"][j][None], w_out=w_out, w_out_a=w_out[:LRU_WIDTH],
                    w_out_b=w_out[LRU_WIDTH:])

    def odd_params(j):
        w_ukv = full["od_w_ukv"][j].reshape(KV_LORA, MLA_HEADS, QK_NOPE + V_HEAD)
        w_uk = w_ukv[:, :, :QK_NOPE].reshape(KV_LORA, MLA_HEADS * QK_NOPE)
        w_uv = w_ukv[:, :, QK_NOPE:].reshape(KV_LORA, MLA_HEADS * V_HEAD)
        w_out = full["od_w_out"][j].reshape(MLA_HEADS, V_HEAD, d)
        w_out = jnp.pad(w_out, ((0, 0), (0, HEAD_PAD - V_HEAD), (0, 0))).reshape(MLA_HEADS * HEAD_PAD, d)
        return dict(norm=full["od_norm"][j][None], w_in=jnp.pad(full["od_w_in"][j], ((0, 0), (0, ODD_IN_PAD - ODD_IN))),
                    q_norm=full["od_q_norm"][j][None], kv_norm=full["od_kv_norm"][j][None],
                    w_uq=_pad_heads(full["od_w_uq"][j], QK_HEAD), w_uk=_pad_heads(w_uk, QK_NOPE),
                    w_uv=_pad_heads(w_uv, V_HEAD), w_out=w_out)

    def ffn_params(layer):
        w_up = full["ffn_w_up"][layer]
        return dict(norm=full["ffn_norm"][layer][None], w_up=w_up, conv_w=full["ffn_conv_w"][layer],
                    conv_b=full["ffn_conv_b"][layer][None], w_down=full["ffn_w_down"][layer])

    meta = jnp.broadcast_to(full["meta_tokens"][None], (nb, N_META, d))
    h0 = jnp.concatenate([meta, x, jnp.zeros((nb, tp - t_real, d), F32)], axis=1).reshape(m, d)
    hcur = h0
    tape = []
    for layer in range(4):
        j = layer // 2
        if layer % 2 == 0:
            mp = even_params(j)
            hcur, saved = _even_fwd(hcur, mp, m, tp, nb)
        else:
            mp = odd_params(j)
            hcur, saved = _odd_fwd(hcur, mp, tables, m, tp, nb)
        fp = ffn_params(layer)
        hcur, fsaved = _ffn_fwd(hcur, fp, m, tp)
        tape.append((mp, saved, fp, fsaved))

    target = jnp.pad(loss_target, ((0, 0), (N_META, tp - t_real), (0, 0))).reshape(m, d)
    dh, d_final_norm, loss_part = _loss_head(hcur, full["final_norm"][None], target, tp, t_real)

    grads = {"final_norm": d_final_norm[0]}
    ev_g, od_g, ffn_g = [None, None], [None, None], [None] * 4
    for layer in reversed(range(4)):
        mp, saved, fp, fsaved = tape[layer]
        dh, ffn_g[layer] = _ffn_bwd(dh, fp, fsaved, m, tp)
        if layer % 2 == 0:
            dh, ev_g[layer // 2] = _even_bwd(dh, mp, saved, m, tp, nb)
        else:
            dh, od_g[layer // 2] = _odd_bwd(dh, mp, tables, saved, m, tp, nb)

    dh3 = dh.reshape(nb, tp, d)
    grad_x = dh3[:, N_META:t_real]
    grads["meta_tokens"] = _meta_grad(dh, nb, tp)

    def stack(lst, key, fn=lambda a: a):
        return jnp.stack([fn(g[key]) for g in lst], axis=0)

    grads["ev_norm"] = stack(ev_g, "norm", lambda a: a[0])
    grads["ev_w_in"] = stack(ev_g, "w_in")
    grads["ev_conv_a"] = stack(ev_g, "conv_a")
    grads["ev_conv_b"] = stack(ev_g, "conv_b")
    grads["ev_conv_b_bias"] = stack(ev_g, "conv_b_bias", lambda a: a[0])
    grads["ev_gate_r_w"] = stack(ev_g, "gate_r", lambda a: _block_diag_t(a, 8))
    grads["ev_gate_r_b"] = stack(ev_g, "gate_r_b", lambda a: a[0])
    grads["ev_gate_i_w"] = stack(ev_g, "gate_i", lambda a: _block_diag_t(a, 8))
    grads["ev_gate_i_b"] = stack(ev_g, "gate_i_b", lambda a: a[0])
    grads["ev_lru_lambda"] = stack(ev_g, "lam", lambda a: a[0])
    grads["ev_w_out"] = stack(ev_g, "w_out")
    grads["od_norm"] = stack(od_g, "norm", lambda a: a[0])
    grads["od_w_in"] = stack(od_g, "w_in", lambda a: a[:, :ODD_IN])
    grads["od_q_norm"] = stack(od_g, "q_norm", lambda a: a[0])
    grads["od_kv_norm"] = stack(od_g, "kv_norm", lambda a: a[0])
    grads["od_w_uq"] = stack(od_g, "w_uq", lambda a: _unpad_heads(a, QK_HEAD))

    def ukv(g):
        gk = g["w_uk"].reshape(KV_LORA, MLA_HEADS, HEAD_PAD)[:, :, :QK_NOPE]
        gv = g["w_uv"].reshape(KV_LORA, MLA_HEADS, HEAD_PAD)[:, :, :V_HEAD]
        return jnp.concatenate([gk, gv], axis=2).reshape(KV_LORA, MLA_HEADS * (QK_NOPE + V_HEAD))

    grads["od_w_ukv"] = jnp.stack([ukv(g) for g in od_g], axis=0)
    grads["od_w_out"] = stack(od_g, "w_out", lambda a: a.reshape(MLA_HEADS, HEAD_PAD, d)[:, :V_HEAD].reshape(-1, d))
    grads["ffn_norm"] = stack(ffn_g, "norm", lambda a: a[0])
    grads["ffn_w_up"] = stack(ffn_g, "w_up")
    grads["ffn_conv_w"] = stack(ffn_g, "conv_w")
    grads["ffn_conv_b"] = stack(ffn_g, "conv_b", lambda a: a[0])
    grads["ffn_w_down"] = stack(ffn_g, "w_down")

    order = small + replicated
    slot_parts = [_to_slots(grads[n], axis_of[n]).reshape(N_DEV, -1) for n in small]
    slot_parts += [jnp.broadcast_to(grads[n].reshape(1, -1), (N_DEV, grads[n].size)) for n in replicated]
    slot_parts.append(jnp.broadcast_to(loss_part, (N_DEV, 1)))
    g_flat = jnp.concatenate(slot_parts, axis=1)
    n_flat = g_flat.shape[1]
    rows = -(-n_flat // (1024 * 128)) * 128
    g_small = jnp.pad(g_flat, ((0, 0), (0, rows * 1024 - n_flat))).reshape(N_DEV, rows, 1024)

    def rows_of(n):
        shard = w_loc[n].shape
        return (math.prod(shard[:-1]), shard[-1])

    core = lax.axis_index("c")

    def core_slots(n, which):
        return _core_slots(grads[n], axis_of[n], which).astype(BF16).reshape((N_CHIP,) + rows_of(n))

    small_by_core = jnp.swapaxes(g_small.reshape((N_CHIP, 2) + g_small.shape[1:]), 0, 1)
    mine = [core_slots(n, core) for n in BIG] + [lax.dynamic_index_in_dim(small_by_core, core, 0, keepdims=False)]
    theirs = [core_slots(n, 1 - core) for n in BIG]
    theirs.append(lax.dynamic_index_in_dim(small_by_core, 1 - core, 0, keepdims=False))
    from_sibling = _pair_exchange(theirs)
    parts = _chip_exchange([_pair_add(a, b) for a, b in zip(mine, from_sibling)])

    g_out, d_out, m_out, v_out = {}, {}, {}, {}
    for n, part in zip(BIG, parts[:-1]):
        res = _reduce_adamw(part, *[t[n].reshape(rows_of(n)) for t in (w_loc, m_loc, v_loc)])
        for out, r in zip((g_out, d_out, m_out, v_out), res):
            out[n] = r.reshape(w_loc[n].shape)

    def flat_local(tree):
        flat = jnp.concatenate([tree[n].reshape(-1) for n in order])
        return jnp.pad(flat, (0, rows * 1024 - flat.shape[0])).reshape(rows, 1024)

    res = _reduce_adamw(parts[-1], flat_local(w_loc), flat_local(m_loc), flat_local(v_loc))
    loss = res[0].reshape(-1)[n_flat - 1]
    for out, r in zip((g_out, d_out, m_out, v_out), res):
        flat = r.reshape(-1)
        off = 0
        for n in order:
            size = w_loc[n].size
            out[n] = flat[off:off + size].reshape(w_loc[n].shape)
            off += size
    return (loss, grad_x, *[g_out[n] for n in names], *[d_out[n] for n in names], *[m_out[n] for n in names],
            *[v_out[n] for n in names])
```

```python
import functools
import math

import jax
import jax.numpy as jnp
from jax import lax
from jax.experimental import pallas as pl
from jax.experimental.pallas import tpu as pltpu

F32 = jnp.float32
BF16 = jnp.bfloat16

N_DEV = 8
N_META = 16
EPS = 1e-6
LRU_C = 8.0
MLA_HEADS = 16
QK_NOPE = 64
QK_ROPE = 32
QK_HEAD = QK_NOPE + QK_ROPE
V_HEAD = 64
HEAD_PAD = 128
Q_LORA = 384
KV_LORA = 256
ODD_IN = Q_LORA + KV_LORA + QK_ROPE
ODD_IN_PAD = 768
ROPE_BASE = 10000.0
LRU_WIDTH = 512
D_FF = 2816

ADAM_LR = 0.001
ADAM_B1 = 0.9
ADAM_B2 = 0.999
ADAM_EPS = 1e-08
ADAM_WD = 0.01
ADAM_STEP = 10

ROW_TILE = 384
SUBLANES = 8
HALO_ROWS = 16
LANES = 128
VMEM_LIMIT = 48 * 1024 * 1024
NEG = -1e30

PARAMS = (
    ("meta_tokens", 1), ("ev_norm", None), ("ev_w_in", 2), ("ev_conv_a", 2), ("ev_conv_b", 2),
    ("ev_conv_b_bias", None), ("ev_gate_r_w", None), ("ev_gate_r_b", None), ("ev_gate_i_w", None),
    ("ev_gate_i_b", None), ("ev_lru_lambda", None), ("ev_w_out", 1), ("od_norm", 1), ("od_w_in", 1),
    ("od_q_norm", 1), ("od_kv_norm", 1), ("od_w_uq", 2), ("od_w_ukv", 2), ("od_w_out", 1),
    ("ffn_norm", None), ("ffn_w_up", 2), ("ffn_conv_w", 2), ("ffn_conv_b", None), ("ffn_w_down", 1),
    ("final_norm", None),
)
BIG = ("ev_w_in", "ev_w_out", "od_w_in", "od_w_uq", "od_w_ukv", "od_w_out", "ffn_w_up", "ffn_w_down")


def _cparams(n_grid):
    return pltpu.CompilerParams(dimension_semantics=("arbitrary",) * n_grid, vmem_limit_bytes=VMEM_LIMIT)


def _pick(dim, target):
    if dim <= target:
        return dim
    best = None
    for t in range(LANES, target + 1, LANES):
        if dim % t == 0:
            best = t
    assert best is not None, (dim, target)
    return best


MATMUL_VMEM_BUDGET = 38 * 1024 * 1024
HBM_BYTES_PER_US = 3.0e6
MXU_FLOPS_PER_US = 9.0e8
MXU_TILE = 256
ACC_BYTES_PER_US = 7.6e6
GRID_STEP_US = 0.35


def _tile_candidates(dim):
    return [t for t in range(LANES, dim + 1, LANES) if dim % t == 0] or [dim]


def _matmul_tiles(m, n, k, sa, sb, so, sr, transposed_lhs):
    best, best_cost = None, None
    for tm in _tile_candidates(m):
        for tn in _tile_candidates(n):
            for tk in _tile_candidates(k):
                nk = k // tk
                vmem = 2 * (tm * tk * sa + tk * tn * sb) + tm * tn * ((4 if nk > 1 else 0) + 2 * so + 2 * sr)
                vmem += (tm * tk * 2 if sa > 2 else 0) + (tk * tn * 2 if sb > 2 else 0) + tm * tn * 4
                if vmem > MATMUL_VMEM_BUDGET:
                    continue
                steps = (m // tm) * (n // tn) * nk
                traffic = m * k * sa * (n // tn) + k * n * sb * (m // tm) + m * n * (so + sr)
                acc_us = steps * tm * tn * 4 / ACC_BYTES_PER_US if nk > 1 else 0.0
                padded = (-(-tk // MXU_TILE) * MXU_TILE / tk) * (-(-tn // MXU_TILE) * MXU_TILE / tn)
                mxu_us = 2.0 * m * n * k / MXU_FLOPS_PER_US
                if transposed_lhs:
                    cost = traffic / HBM_BYTES_PER_US + mxu_us * (padded - 1.0) + steps * GRID_STEP_US
                else:
                    cost = max(traffic / HBM_BYTES_PER_US, mxu_us * padded + acc_us) + steps * GRID_STEP_US
                if best_cost is None or cost < best_cost:
                    best, best_cost = (tm, tn, tk), cost
    assert best is not None, (m, n, k)
    return best


def _matmul(a, b, mode, out_dtype=F32, residual=None, name="mm", b_col_off=0):
    if mode == "nn":
        (m, k), (k2, n) = a.shape, b.shape
    elif mode == "nt":
        (m, k), n = a.shape, b.shape[0]
        k2 = k if b_col_off or b.shape[1] > k else b.shape[1]
    else:
        (k, m), (k2, n) = a.shape, b.shape
    assert k == k2, (a.shape, b.shape, mode)
    tm, tn, tk = _matmul_tiles(m, n, k, a.dtype.itemsize, b.dtype.itemsize, jnp.dtype(out_dtype).itemsize,
                               0 if residual is None else residual.dtype.itemsize, mode == "tn")
    nk = k // tk
    if mode == "tn":
        a_spec = pl.BlockSpec((tk, tm), lambda i, j, kk: (kk, i))
        dims = (((0,), (0,)), ((), ()))
    else:
        a_spec = pl.BlockSpec((tm, tk), lambda i, j, kk: (i, kk))
        dims = (((1,), (1 if mode == "nt" else 0,)), ((), ()))
    if mode == "nt":
        assert b_col_off % tk == 0, (b_col_off, tk)
        b_spec = pl.BlockSpec((tn, tk), lambda i, j, kk: (j, kk + b_col_off // tk))
    else:
        b_spec = pl.BlockSpec((tk, tn), lambda i, j, kk: (kk, j))
    o_spec = pl.BlockSpec((tm, tn), lambda i, j, kk: (i, j))
    has_res = residual is not None

    def body(*refs):
        a_ref, b_ref = refs[:2]
        r_ref = refs[2] if has_res else None
        o_ref = refs[3] if has_res else refs[2]
        part = lax.dot_general(a_ref[...].astype(BF16), b_ref[...].astype(BF16), dims, preferred_element_type=F32)

        def finish(out):
            if has_res:
                out = out + r_ref[...].astype(F32)
            o_ref[...] = out.astype(o_ref.dtype)

        if nk == 1:
            finish(part)
            return
        acc_ref = refs[-1]
        kk = pl.program_id(2)

        @pl.when(kk == 0)
        def _():
            acc_ref[...] = part

        @pl.when(kk > 0)
        def _():
            acc_ref[...] += part

        @pl.when(kk == nk - 1)
        def _():
            finish(acc_ref[...])

    in_specs = [a_spec, b_spec] + ([o_spec] if has_res else [])
    args = (a, b) + ((residual,) if has_res else ())
    return pl.pallas_call(
        body, out_shape=jax.ShapeDtypeStruct((m, n), out_dtype), grid=(m // tm, n // tn, nk),
        in_specs=in_specs, out_specs=o_spec, scratch_shapes=[pltpu.VMEM((tm, tn), F32)] if nk > 1 else [],
        compiler_params=_cparams(3), name=name)(*args)


def _rms_fwd(x, g, name):
    m, c = x.shape
    tm = _pick(m, ROW_TILE)

    def body(x_ref, g_ref, o_ref):
        xf = x_ref[...].astype(F32)
        r = lax.rsqrt(jnp.mean(xf * xf, axis=-1, keepdims=True) + EPS)
        o_ref[...] = (xf * r * g_ref[...]).astype(o_ref.dtype)

    return pl.pallas_call(
        body, out_shape=jax.ShapeDtypeStruct((m, c), BF16), grid=(m // tm,),
        in_specs=[pl.BlockSpec((tm, c), lambda i: (i, 0)), pl.BlockSpec((1, c), lambda i: (0, 0))],
        out_specs=pl.BlockSpec((tm, c), lambda i: (i, 0)), compiler_params=_cparams(1), name=name)(x, g)


def _rms_bwd(x, g, dy, residual, name):
    m, c = x.shape
    tm = _pick(m, ROW_TILE)
    has_res = residual is not None

    def body(*refs):
        if has_res:
            x_ref, g_ref, dy_ref, r_ref, dx_ref, dg_ref = refs
        else:
            x_ref, g_ref, dy_ref, dx_ref, dg_ref = refs
        xf = x_ref[...].astype(F32)
        dyf = dy_ref[...].astype(F32)
        r = lax.rsqrt(jnp.mean(xf * xf, axis=-1, keepdims=True) + EPS)
        xn = xf * r
        dyg = dyf * g_ref[...]
        dx = r * (dyg - xn * jnp.mean(dyg * xn, axis=-1, keepdims=True))
        if has_res:
            dx = dx + r_ref[...]
        dx_ref[...] = dx

        @pl.when(pl.program_id(0) == 0)
        def _():
            dg_ref[...] = jnp.zeros_like(dg_ref)

        dg_ref[...] += jnp.sum(dyf * xn, axis=0, keepdims=True)

    row = pl.BlockSpec((tm, c), lambda i: (i, 0))
    vec = pl.BlockSpec((1, c), lambda i: (0, 0))
    in_specs = [row, vec, row] + ([row] if has_res else [])
    args = (x, g, dy) + ((residual,) if has_res else ())
    return pl.pallas_call(
        body, out_shape=(jax.ShapeDtypeStruct((m, c), F32), jax.ShapeDtypeStruct((1, c), F32)), grid=(m // tm,),
        in_specs=in_specs, out_specs=(row, vec), compiler_params=_cparams(1), name=name)(*args)


def _chan_call(name, fn, m, tp, tc, ncol, row_ins=(), prev_ins=(), next_ins=(), chan_ins=(), row_outs=(),
               red_outs=(), row_split=1):
    tm = _pick(tp, ROW_TILE) // row_split
    tps = tp // tm
    nrow = m // tm
    halo_blocks = tm // HALO_ROWS
    last_halo = m // HALO_ROWS - 1
    n_in = len(row_ins) + len(prev_ins) + len(next_ins) + len(chan_ins)
    n_r, n_p, n_n = len(row_ins), len(prev_ins), len(next_ins)

    def body(*refs):
        i = pl.program_id(1)
        pos = lax.rem(i, tps)
        at_start = pos == 0
        at_end = pos == tps - 1
        rows = [r[...].astype(F32) for r in refs[:n_r]]
        prevs = [jnp.where(at_start, 0.0, r[...].astype(F32)[SUBLANES:]) for r in refs[n_r:n_r + n_p]]
        nexts = [jnp.where(at_end, 0.0, r[...].astype(F32)[:SUBLANES]) for r in refs[n_r + n_p:n_r + n_p + n_n]]
        chans = [r[...] for r in refs[n_r + n_p + n_n:n_in]]
        out_refs = refs[n_in:n_in + len(row_outs)]
        red_refs = refs[n_in + len(row_outs):]
        row_vals, red_vals = fn(rows, prevs, nexts, chans)
        for ref, val in zip(out_refs, row_vals):
            ref[...] = val.astype(ref.dtype)
        if red_refs:
            @pl.when(i == 0)
            def _():
                for ref in red_refs:
                    ref[...] = jnp.zeros_like(ref)

            for ref, val in zip(red_refs, red_vals):
                ref[...] += val

    in_specs, args = [], []
    for arr, off in row_ins:
        in_specs.append(pl.BlockSpec((tm, tc), lambda j, i, off=off: (i, j + off)))
        args.append(arr)
    for arr, off in prev_ins:
        in_specs.append(pl.BlockSpec((HALO_ROWS, tc),
                                     lambda j, i, off=off: (jnp.maximum(i * halo_blocks - 1, 0), j + off)))
        args.append(arr)
    for arr, off in next_ins:
        in_specs.append(pl.BlockSpec((HALO_ROWS, tc),
                                     lambda j, i, off=off: (jnp.minimum((i + 1) * halo_blocks, last_halo), j + off)))
        args.append(arr)
    for arr, off in chan_ins:
        in_specs.append(pl.BlockSpec((arr.shape[0], tc), lambda j, i, off=off: (0, j + off)))
        args.append(arr)
    out_shape, out_specs = [], []
    for (dt,) in row_outs:
        out_shape.append(jax.ShapeDtypeStruct((m, ncol * tc), dt))
        out_specs.append(pl.BlockSpec((tm, tc), lambda j, i: (i, j)))
    for (k,) in red_outs:
        out_shape.append(jax.ShapeDtypeStruct((k, ncol * tc), F32))
        out_specs.append(pl.BlockSpec((k, tc), lambda j, i: (0, j)))
    return pl.pallas_call(
        body, out_shape=tuple(out_shape), grid=(ncol, nrow), in_specs=in_specs, out_specs=tuple(out_specs),
        compiler_params=_cparams(2), name=name)(*args)


def _shift_down(x, prev8, s):
    if s == 0:
        return x
    tm, tc = x.shape
    groups = tm // SUBLANES
    xr = pltpu.roll(x.reshape(groups, SUBLANES, tc), s, 1)
    before = jnp.concatenate([pltpu.roll(prev8, s, 0)[None], xr[:-1]], axis=0)
    rid = lax.broadcasted_iota(jnp.int32, xr.shape, 1)
    return jnp.where(rid < s, before, xr).reshape(tm, tc)


def _shift_up(x, next8, s):
    if s == 0:
        return x
    tm, tc = x.shape
    groups = tm // SUBLANES
    xr = pltpu.roll(x.reshape(groups, SUBLANES, tc), SUBLANES - s, 1)
    after = jnp.concatenate([xr[1:], pltpu.roll(next8, SUBLANES - s, 0)[None]], axis=0)
    rid = lax.broadcasted_iota(jnp.int32, xr.shape, 1)
    return jnp.where(rid >= SUBLANES - s, after, xr).reshape(tm, tc)


def _taps(x, prev8, kw):
    return [_shift_down(x, prev8, kw - 1 - k) for k in range(kw)]


def _conv_taps(taps, w):
    y = w[0:1, :] * taps[0]
    for k in range(1, len(taps)):
        y = y + w[k:k + 1, :] * taps[k]
    return y


def _conv_dw_taps(dy, taps):
    shape = (SUBLANES, dy.shape[1])
    rid = lax.broadcasted_iota(jnp.int32, shape, 0)
    out = jnp.zeros(shape, F32)
    for k, tap in enumerate(taps):
        out = out + jnp.where(rid == k, jnp.sum(dy * tap, axis=0, keepdims=True), 0.0)
    return out


def _conv_fwd(x, prev8, w):
    return _conv_taps(_taps(x, prev8, w.shape[0]), w)


def _conv_dw(dy, x, prev8, kw):
    return _conv_dw_taps(dy, _taps(x, prev8, kw))


def _conv_dx(dy, next8, w):
    kw = w.shape[0]
    dx = w[kw - 1:kw, :] * dy
    for k in range(kw - 1):
        dx = dx + w[k:k + 1, :] * _shift_up(dy, next8, kw - 1 - k)
    return dx


def _sigmoid(x):
    return 1.0 / (1.0 + jnp.exp(-x))


def _expm1(x):
    series = x * (1.0 + x * 0.5 * (1.0 + x * (1.0 / 3.0) * (1.0 + x * 0.25 * (1.0 + x * 0.2))))
    return jnp.where(jnp.abs(x) < 0.3, series, jnp.exp(x) - 1.0)


def _softplus_neg(lam):
    e = jnp.exp(-jnp.abs(lam))
    log1p = jnp.where(e < 1e-2, e * (1.0 - e * (0.5 - e * (1.0 / 3.0))), jnp.log(1.0 + e))
    return jnp.maximum(-lam, 0.0) + log1p


GELU_C = math.sqrt(2.0 / math.pi)


def _gelu(x):
    return 0.5 * x * (1.0 + jnp.tanh(GELU_C * (x + 0.044715 * x * x * x)))


def _gelu_grad(x):
    t = jnp.tanh(GELU_C * (x + 0.044715 * x * x * x))
    return 0.5 * (1.0 + t) + 0.5 * x * (1.0 - t * t) * GELU_C * (1.0 + 3.0 * 0.044715 * x * x)


FFN_COL_TILE = 1408


def _ffn_fwd(x, p, m, tp):
    h = _rms_fwd(x, p["norm"], "ffn_norm")
    u = _matmul(h, p["w_up"], "nn", BF16, name="ffn_up")
    tc = FFN_COL_TILE
    ncol = D_FF // tc

    def gate(rows, prevs, nexts, chans):
        ua, ug = rows
        wa, wg, ba, bg = chans
        a = _conv_fwd(ua, prevs[0], wa) + ba
        g = _conv_fwd(ug, prevs[1], wg) + bg
        return [a * _sigmoid(a) * g, a, g], []

    z, a_act, g_act = _chan_call(
        "ffn_gate", gate, m, tp, tc, ncol, row_ins=[(u, 0), (u, ncol)], prev_ins=[(u, 0), (u, ncol)],
        chan_ins=[(p["conv_w"], 0), (p["conv_w"], ncol), (p["conv_b"], 0), (p["conv_b"], ncol)],
        row_outs=[(BF16,), (BF16,), (BF16,)])
    out = _matmul(z, p["w_down"], "nn", F32, residual=x, name="ffn_down")
    return out, (x, h, u, z, a_act, g_act)


def _ffn_bwd(dout, p, saved, m, tp):
    x, h, u, z, a_act, g_act = saved
    tc = FFN_COL_TILE
    ncol = D_FF // tc
    dz = _matmul(dout, p["w_down"], "nt", F32, name="ffn_down_dx")
    d_w_down = _matmul(z, dout, "tn", F32, name="ffn_down_dw")

    def act_bwd(a, g, dzv):
        sg = _sigmoid(a)
        return dzv * g * (sg * (1.0 + a * (1.0 - sg))), dzv * a * sg

    def gate_bwd(rows, prevs, nexts, chans):
        ua, ug, dzv, a, g = rows
        da, dg = act_bwd(a, g, dzv)
        da_next, dg_next = act_bwd(nexts[1], nexts[2], nexts[0])
        ups_a = [_shift_up(da, da_next, 2 - k) for k in range(3)]
        ups_g = [_shift_up(dg, dg_next, 2 - k) for k in range(3)]
        return ([_conv_taps(ups_a, chans[0]), _conv_taps(ups_g, chans[1])],
                [_conv_dw_taps(ua, ups_a), _conv_dw_taps(ug, ups_g),
                 jnp.sum(da, axis=0, keepdims=True), jnp.sum(dg, axis=0, keepdims=True)])

    dua, dug, dcw_a, dcw_g, dcb_a, dcb_g = _chan_call(
        "ffn_gate_bwd", gate_bwd, m, tp, tc, ncol,
        row_ins=[(u, 0), (u, ncol), (dz, 0), (a_act, 0), (g_act, 0)], next_ins=[(dz, 0), (a_act, 0), (g_act, 0)],
        chan_ins=[(p["conv_w"], 0), (p["conv_w"], ncol)],
        row_outs=[(BF16,), (BF16,)], red_outs=[(SUBLANES,), (SUBLANES,), (1,), (1,)], row_split=2)
    d_w_up = jnp.concatenate([_matmul(h, dua, "tn", F32, name="ffn_up_dw_a"),
                              _matmul(h, dug, "tn", F32, name="ffn_up_dw_g")], axis=1)
    dh = _matmul(dua, p["w_up"], "nt", F32, name="ffn_up_dx_a")
    dh = _matmul(dug, p["w_up"], "nt", F32, residual=dh, name="ffn_up_dx_g", b_col_off=D_FF)
    dx, d_norm = _rms_bwd(x, p["norm"], dh, dout, "ffn_norm_bwd")
    d_conv_w = jnp.concatenate([dcw_a[:3], dcw_g[:3]], axis=1)
    d_conv_b = jnp.concatenate([dcb_a, dcb_g], axis=1)
    return dx, dict(norm=d_norm, w_up=d_w_up, conv_w=d_conv_w, conv_b=d_conv_b, w_down=d_w_down)


def _to_scan(x, nb, tp):
    return x.reshape(nb, tp, LRU_WIDTH // LANES, LANES).transpose(1, 0, 2, 3).reshape(tp, -1, LANES)


def _from_scan(x, nb, tp):
    return x.reshape(tp, nb, LRU_WIDTH // LANES, LANES).transpose(1, 0, 2, 3).reshape(nb * tp, LRU_WIDTH)


def _scan_fwd(a, u):
    t_len, s, _ = a.shape
    tc = _pick(t_len, 640)
    blk = pl.BlockSpec((tc, s, LANES), lambda i: (i, 0, 0))

    def body(a_ref, u_ref, h_ref, carry):
        @pl.when(pl.program_id(0) == 0)
        def _():
            carry[...] = jnp.zeros_like(carry)

        def step(t, h):
            h = a_ref[t] * h + u_ref[t]
            h_ref[t] = h
            return h

        carry[...] = lax.fori_loop(0, tc, step, carry[...], unroll=8)

    return pl.pallas_call(
        body, out_shape=jax.ShapeDtypeStruct(a.shape, F32), grid=(t_len // tc,), in_specs=[blk, blk], out_specs=blk,
        scratch_shapes=[pltpu.VMEM((s, LANES), F32)], compiler_params=_cparams(1), name="lru_scan")(a, u)


def _scan_bwd(dh, a, h_prev):
    t_len, s, _ = a.shape
    tc = _pick(t_len, 640)
    nb = t_len // tc
    blk = pl.BlockSpec((tc, s, LANES), lambda i: (nb - 1 - i, 0, 0))

    def body(dh_ref, a_ref, hp_ref, du_ref, da_ref, carry):
        @pl.when(pl.program_id(0) == 0)
        def _():
            carry[...] = jnp.zeros_like(carry)

        def step(k, c):
            t = tc - 1 - k
            d = dh_ref[t] + c
            du_ref[t] = d
            da_ref[t] = d * hp_ref[t]
            return a_ref[t] * d

        carry[...] = lax.fori_loop(0, tc, step, carry[...], unroll=8)

    shp = jax.ShapeDtypeStruct(a.shape, F32)
    return pl.pallas_call(
        body, out_shape=(shp, shp), grid=(nb,), in_specs=[blk, blk, blk], out_specs=(blk, blk),
        scratch_shapes=[pltpu.VMEM((s, LANES), F32)], compiler_params=_cparams(1), name="lru_scan_bwd")(dh, a, h_prev)


def _lru_gates(xc, zr, zi, r_b, i_b, lam):
    r = _sigmoid(zr + r_b)
    ig = _sigmoid(zi + i_b)
    sp = _softplus_neg(lam)
    log_a = -LRU_C * r * sp
    a = jnp.exp(log_a)
    mult = jnp.sqrt(-_expm1(2.0 * log_a))
    return r, ig, sp, a, mult


def _even_fwd(x, p, m, tp, nb):
    c = LRU_WIDTH
    h = _rms_fwd(x, p["norm"], "ev_norm")
    u = _matmul(h, p["w_in"], "nn", F32, name="ev_in")

    def pre(rows, prevs, nexts, chans):
        gb, gc, xa, xb = rows
        wa, wb, bias = chans
        pa = gc * xa
        ya = gb * _conv_fwd(pa, prevs[0] * prevs[1], wa)
        xc = _conv_fwd(xb, prevs[2], wb) + bias
        return [ya, xc], []

    ya, xc = _chan_call("ev_pre", pre, m, tp, c, 1, row_ins=[(u, 0), (u, 1), (u, 2), (u, 3)],
                        prev_ins=[(u, 1), (u, 2), (u, 3)],
                        chan_ins=[(p["conv_a"], 0), (p["conv_b"], 0), (p["conv_b_bias"], 0)],
                        row_outs=[(BF16,), (F32,)])
    zr = _matmul(xc, p["gate_r"], "nn", F32, name="ev_gate_r")
    zi = _matmul(xc, p["gate_i"], "nn", F32, name="ev_gate_i")

    def lru_in(rows, prevs, nexts, chans):
        xcv, zrv, ziv = rows
        r, ig, sp, a, mult = _lru_gates(xcv, zrv, ziv, *chans)
        return [a, mult * (ig * xcv)], []

    a, uu = _chan_call("ev_lru_in", lru_in, m, tp, c, 1, row_ins=[(xc, 0), (zr, 0), (zi, 0)],
                       chan_ins=[(p["gate_r_b"], 0), (p["gate_i_b"], 0), (p["lam"], 0)],
                       row_outs=[(F32,), (F32,)])
    a_s = _to_scan(a, nb, tp)
    hs_s = _scan_fwd(a_s, _to_scan(uu, nb, tp))
    hs = _from_scan(hs_s, nb, tp)

    def post(rows, prevs, nexts, chans):
        gate, hv = rows
        return [_gelu(gate) * hv], []

    (yb,) = _chan_call("ev_post", post, m, tp, c, 1, row_ins=[(u, 4), (hs, 0)], row_outs=[(BF16,)])
    out = _matmul(ya, p["w_out_a"], "nn", F32, residual=x, name="ev_out_a")
    out = _matmul(yb, p["w_out_b"], "nn", F32, residual=out, name="ev_out_b")
    return out, (x, h, u, ya, xc, zr, zi, a_s, hs_s, hs, yb)


def _even_bwd(dout, p, saved, m, tp, nb):
    c = LRU_WIDTH
    x, h, u, ya, xc, zr, zi, a_s, hs_s, hs, yb = saved
    dy = _matmul(dout, p["w_out"], "nt", F32, name="ev_out_dx")
    d_w_out = jnp.concatenate([_matmul(ya, dout, "tn", F32, name="ev_out_dw_a"),
                               _matmul(yb, dout, "tn", F32, name="ev_out_dw_b")], axis=0)

    def post_bwd(rows, prevs, nexts, chans):
        dyb, gate, hv = rows
        return [dyb * hv * _gelu_grad(gate), dyb * _gelu(gate)], []

    dgate, dhs = _chan_call("ev_post_bwd", post_bwd, m, tp, c, 1, row_ins=[(dy, 1), (u, 4), (hs, 0)],
                            row_outs=[(F32,), (F32,)])
    h_prev = jnp.concatenate([jnp.zeros_like(hs_s[:1]), hs_s[:-1]], axis=0)
    du_s, da_s = _scan_bwd(_to_scan(dhs, nb, tp), a_s, h_prev)
    du = _from_scan(du_s, nb, tp)
    da = _from_scan(da_s, nb, tp)

    def lru_in_bwd(rows, prevs, nexts, chans):
        duv, dav, xcv, zrv, ziv = rows
        r, ig, sp, a, mult = _lru_gates(xcv, zrv, ziv, *chans)
        dxc = duv * mult * ig
        dig = duv * mult * xcv
        dmult = duv * ig * xcv
        dlog_a = dav * a - dmult * (a * a) / jnp.maximum(mult, 1e-30)
        dr = dlog_a * (-LRU_C * sp)
        dzr = dr * r * (1.0 - r)
        dzi = dig * ig * (1.0 - ig)
        dsp = jnp.sum(dlog_a * (-LRU_C * r), axis=0, keepdims=True)
        dlam = -dsp * _sigmoid(-chans[2])
        return ([dzr, dzi, dxc],
                [jnp.sum(dzr, axis=0, keepdims=True), jnp.sum(dzi, axis=0, keepdims=True), dlam])

    dzr, dzi, dxc, d_r_b, d_i_b, d_lam = _chan_call(
        "ev_lru_in_bwd", lru_in_bwd, m, tp, c, 1, row_ins=[(du, 0), (da, 0), (xc, 0), (zr, 0), (zi, 0)],
        chan_ins=[(p["gate_r_b"], 0), (p["gate_i_b"], 0), (p["lam"], 0)],
        row_outs=[(F32,), (F32,), (F32,)], red_outs=[(1,), (1,), (1,)])
    d_gate_r = _matmul(xc, dzr, "tn", F32, name="ev_gate_r_dw")
    d_gate_i = _matmul(xc, dzi, "tn", F32, name="ev_gate_i_dw")
    dxc = _matmul(dzr, p["gate_r"], "nt", F32, residual=dxc, name="ev_gate_r_dx")
    dxc = _matmul(dzi, p["gate_i"], "nt", F32, residual=dxc, name="ev_gate_i_dx")

    def conv_b_bwd(rows, prevs, nexts, chans):
        dxcv, xb = rows
        return ([_conv_dx(dxcv, nexts[0], chans[0])],
                [_conv_dw(dxcv, xb, prevs[0], 4), jnp.sum(dxcv, axis=0, keepdims=True)])

    dxb, d_conv_b, d_bias = _chan_call(
        "ev_conv_b_bwd", conv_b_bwd, m, tp, c, 1, row_ins=[(dxc, 0), (u, 3)], prev_ins=[(u, 3)], next_ins=[(dxc, 0)],
        chan_ins=[(p["conv_b"], 0)], row_outs=[(F32,)], red_outs=[(SUBLANES,), (1,)])

    def mix_a_bwd(rows, prevs, nexts, chans):
        dya, gb, gc, xa = rows
        (wa,) = chans
        taps = _taps(gc * xa, prevs[0] * prevs[1], 3)
        ca = _conv_taps(taps, wa)
        dca = dya * gb
        dpa = _conv_dx(dca, nexts[0] * nexts[1], wa)
        return [dya * ca, dpa * xa, dpa * gc], [_conv_dw_taps(dca, taps)]

    dgb, dgc, dxa, d_conv_a = _chan_call(
        "ev_mix_a_bwd", mix_a_bwd, m, tp, c, 1, row_ins=[(dy, 0), (u, 0), (u, 1), (u, 2)],
        prev_ins=[(u, 1), (u, 2)], next_ins=[(dy, 0), (u, 0)], chan_ins=[(p["conv_a"], 0)],
        row_outs=[(F32,), (F32,), (F32,)], red_outs=[(SUBLANES,)])
    du_all = jnp.concatenate([dgb, dgc, dxa, dxb, dgate], axis=1)
    d_w_in = _matmul(h, du_all, "tn", F32, name="ev_in_dw")
    dh = _matmul(du_all, p["w_in"], "nt", F32, name="ev_in_dx")
    dx, d_norm = _rms_bwd(x, p["norm"], dh, dout, "ev_norm_bwd")
    return dx, dict(norm=d_norm, w_in=d_w_in, conv_a=d_conv_a[:3], conv_b=d_conv_b[:4], conv_b_bias=d_bias,
                    gate_r=d_gate_r, gate_r_b=d_r_b, gate_i=d_gate_i, gate_i_b=d_i_b, lam=d_lam, w_out=d_w_out)


def _rope_tables(tp):
    pos = jnp.arange(tp, dtype=F32)
    inv_freq = ROPE_BASE ** (-jnp.arange(0, QK_ROPE, 2, dtype=F32) / QK_ROPE)
    ang = pos[:, None] * inv_freq[None, :]
    cos, sin = jnp.cos(ang), jnp.sin(ang)
    half = QK_ROPE // 2
    one = jnp.ones((tp, QK_NOPE), F32)
    z64 = jnp.zeros((tp, QK_NOPE), F32)
    zh = jnp.zeros((tp, half), F32)
    zt = jnp.zeros((tp, HEAD_PAD - QK_HEAD), F32)
    c_tab = jnp.concatenate([one, cos, cos, zt], axis=1)
    s_lo = jnp.concatenate([z64, -sin, zh, zt], axis=1)
    s_hi = jnp.concatenate([z64, zh, sin, zt], axis=1)
    return c_tab, s_lo, s_hi


def _rope(v, c_tab, s_lo, s_hi):
    half = QK_ROPE // 2
    return v * c_tab + pltpu.roll(v, HEAD_PAD - half, 1) * s_lo + pltpu.roll(v, half, 1) * s_hi


def _rope_t(dv, c_tab, s_lo, s_hi):
    half = QK_ROPE // 2
    return dv * c_tab + pltpu.roll(dv * s_lo, half, 1) + pltpu.roll(dv * s_hi, HEAD_PAD - half, 1)


def _rope_call(name, fn, m, tp, ins, tables, out_dtype, shared_pre=None):
    tm = _pick(tp, ROW_TILE)
    tps = tp // tm
    n = len(ins)
    width = MLA_HEADS * HEAD_PAD

    def body(*refs):
        tabs = [r[...] for r in refs[n:n + 3]]
        shared = [None if fc is None else shared_pre(refs[a][...].astype(F32), *tabs) for a, (_, fc) in enumerate(ins)]
        for hh in range(MLA_HEADS):
            lanes = slice(hh * HEAD_PAD, (hh + 1) * HEAD_PAD)
            vals = [refs[a][:, lanes].astype(F32) if shared[a] is None else shared[a] for a in range(n)]
            refs[n + 3][:, lanes] = fn(*vals, *tabs).astype(out_dtype)

    in_specs, args = [], []
    for arr, fixed_col in ins:
        if fixed_col is None:
            in_specs.append(pl.BlockSpec((tm, width), lambda i: (i, 0)))
        else:
            in_specs.append(pl.BlockSpec((tm, HEAD_PAD), lambda i, fc=fixed_col: (i, fc)))
        args.append(arr)
    for tab in tables:
        in_specs.append(pl.BlockSpec((tm, HEAD_PAD), lambda i: (lax.rem(i, tps), 0)))
        args.append(tab)
    return pl.pallas_call(
        body, out_shape=jax.ShapeDtypeStruct((m, width), out_dtype), grid=(m // tm,),
        in_specs=in_specs, out_specs=pl.BlockSpec((tm, width), lambda i: (i, 0)),
        compiler_params=_cparams(1), name=name)(*args)


def _rope_k_bwd(dk, tables, m, tp):
    tm = _pick(tp, ROW_TILE)
    tps = tp // tm

    def body(dk_ref, c_ref, lo_ref, hi_ref, o_ref):
        acc = dk_ref[:, 0:HEAD_PAD].astype(F32)
        for hh in range(1, MLA_HEADS):
            acc = acc + dk_ref[:, hh * HEAD_PAD:(hh + 1) * HEAD_PAD].astype(F32)
        d = pltpu.roll(_rope_t(acc, c_ref[...], lo_ref[...], hi_ref[...]), QK_NOPE, 1)
        lane = lax.broadcasted_iota(jnp.int32, d.shape, 1)
        o_ref[...] = jnp.where(lane < QK_ROPE, d, 0.0)

    tab = pl.BlockSpec((tm, HEAD_PAD), lambda i: (lax.rem(i, tps), 0))
    return pl.pallas_call(
        body, out_shape=jax.ShapeDtypeStruct((m, HEAD_PAD), F32), grid=(m // tm,),
        in_specs=[pl.BlockSpec((tm, MLA_HEADS * HEAD_PAD), lambda i: (i, 0)), tab, tab, tab],
        out_specs=pl.BlockSpec((tm, HEAD_PAD), lambda i: (i, 0)), compiler_params=_cparams(1),
        name="od_rope_k_bwd")(dk, *tables)


def _causal_mask(row0, col0, shape):
    rows = row0 + lax.broadcasted_iota(jnp.int32, shape, 0)
    cols = col0 + lax.broadcasted_iota(jnp.int32, shape, 1)
    return cols <= rows


NT = (((1,), (1,)), ((), ()))
TN = (((0,), (0,)), ((), ()))
HEADS_PER_STEP = 2
HEAD_STEPS = MLA_HEADS // HEADS_PER_STEP
STEP_LANES = HEADS_PER_STEP * HEAD_PAD


def _flash_fwd(q, k, v, nb, tp):
    tq = _pick(tp, ROW_TILE)
    nq = tp // tq

    def body(q_ref, k_ref, v_ref, o_ref, lse_ref):
        i = pl.program_id(2)
        qbs = [q_ref[:, hd * HEAD_PAD:(hd + 1) * HEAD_PAD] for hd in range(HEADS_PER_STEP)]

        def chunk(j, carry, masked, width=1):
            off = pl.multiple_of(j * tq, tq)
            out = []
            for hd in range(HEADS_PER_STEP):
                mx, l, acc = carry[hd]
                lanes = slice(hd * HEAD_PAD, (hd + 1) * HEAD_PAD)
                kb = k_ref[pl.ds(off, width * tq), lanes]
                vb = v_ref[pl.ds(off, width * tq), lanes]
                s = lax.dot_general(qbs[hd], kb, NT, preferred_element_type=F32)
                if masked:
                    s = jnp.where(_causal_mask((width - 1) * tq, 0, s.shape), s, NEG)
                m_new = jnp.maximum(mx, jnp.max(s, axis=1, keepdims=True))
                alpha = jnp.exp(mx - m_new)
                pr = jnp.exp(s - m_new)
                l = alpha * l + jnp.sum(pr, axis=1, keepdims=True)
                acc = alpha * acc + jnp.dot(pr.astype(BF16), vb, preferred_element_type=F32)
                out.append((m_new, l, acc))
            return tuple(out)

        one = (jnp.full((tq, 1), NEG, F32), jnp.zeros((tq, 1), F32), jnp.zeros((tq, HEAD_PAD), F32))
        wide = jnp.minimum(i, 1)
        before = i - wide
        quads = before // 4
        carry = lax.fori_loop(0, quads, lambda jj, c: chunk(4 * jj, c, False, 4), (one,) * HEADS_PER_STEP)
        carry = lax.fori_loop(0, lax.rem(before, 4) // 2, lambda _, c: chunk(4 * quads, c, False, 2), carry)
        carry = lax.fori_loop(0, lax.rem(before, 2), lambda _, c: chunk(before - 1, c, False), carry)
        carry = lax.fori_loop(0, wide, lambda _, c: chunk(i - 1, c, True, 2), carry)
        carry = lax.fori_loop(0, 1 - wide, lambda _, c: chunk(i, c, True), carry)
        for hd in range(HEADS_PER_STEP):
            mx, l, acc = carry[hd]
            lanes = slice(hd * HEAD_PAD, (hd + 1) * HEAD_PAD)
            o_ref[:, lanes] = (acc / l).astype(o_ref.dtype)
            lse_ref[:, lanes] = jnp.broadcast_to(mx + jnp.log(l), (tq, HEAD_PAD))

    qspec = pl.BlockSpec((tq, STEP_LANES), lambda b, hh, i: (b * nq + i, hh))
    kvspec = pl.BlockSpec((tp, STEP_LANES), lambda b, hh, i: (b, hh))
    shp = (nb * tp, MLA_HEADS * HEAD_PAD)
    return pl.pallas_call(
        body, out_shape=(jax.ShapeDtypeStruct(shp, BF16), jax.ShapeDtypeStruct(shp, F32)),
        grid=(nb, HEAD_STEPS, nq), in_specs=[qspec, kvspec, kvspec], out_specs=(qspec, qspec),
        compiler_params=_cparams(3), name="od_flash_fwd")(q, k, v)


def _flash_prep(o, do, lse_c, nb, tp):
    tq = _pick(tp, ROW_TILE)
    nq = tp // tq

    def body(o_ref, do_ref, lse_ref, lr_ref, dr_ref):
        for hh in range(MLA_HEADS):
            lanes = slice(hh * HEAD_PAD, (hh + 1) * HEAD_PAD)
            delta = jnp.sum(o_ref[:, lanes].astype(F32) * do_ref[:, lanes].astype(F32), axis=1, keepdims=True)
            lr_ref[hh] = jnp.transpose(lse_ref[:, lanes])[0:SUBLANES, :]
            dr_ref[hh] = jnp.transpose(jnp.broadcast_to(delta, (tq, HEAD_PAD)))[0:SUBLANES, :]

    qspec = pl.BlockSpec((tq, MLA_HEADS * HEAD_PAD), lambda b, i: (b * nq + i, 0))
    rspec = pl.BlockSpec((MLA_HEADS, None, SUBLANES, tq), lambda b, i: (b, i, 0, 0))
    rshape = jax.ShapeDtypeStruct((nb * MLA_HEADS, nq, SUBLANES, tq), F32)
    return pl.pallas_call(
        body, out_shape=(rshape, rshape), grid=(nb, nq), in_specs=[qspec, qspec, qspec],
        out_specs=(rspec, rspec), compiler_params=_cparams(2), name="od_flash_prep")(o, do, lse_c)


def _flash_bwd(q, k, v, do, lse_r, delta_r, nb, tp):
    tq = _pick(tp, ROW_TILE)
    nq = tp // tq

    def body(q_ref, k_ref, v_ref, do_ref, lse_ref, dl_ref, dq_ref, dk_ref, dv_ref):
        j = pl.program_id(2)

        @pl.when(j == 0)
        def _():
            dq_ref[...] = jnp.zeros_like(dq_ref)

        kbs = [k_ref[:, hd * HEAD_PAD:(hd + 1) * HEAD_PAD] for hd in range(HEADS_PER_STEP)]
        vbs = [v_ref[:, hd * HEAD_PAD:(hd + 1) * HEAD_PAD] for hd in range(HEADS_PER_STEP)]

        def chunk(i, carry, masked, width=1):
            off = pl.multiple_of(i * tq, tq)
            out = []
            for hd in range(HEADS_PER_STEP):
                dk, dv = carry[hd]
                lanes = slice(hd * HEAD_PAD, (hd + 1) * HEAD_PAD)
                qb = q_ref[pl.ds(off, width * tq), lanes]
                dob = do_ref[pl.ds(off, width * tq), lanes]
                lse = jnp.concatenate([lse_ref[hd, i + w][0:1, :] for w in range(width)], axis=1)
                delta = jnp.concatenate([dl_ref[hd, i + w][0:1, :] for w in range(width)], axis=1)
                st = lax.dot_general(kbs[hd], qb, NT, preferred_element_type=F32)
                pt = jnp.exp(st - lse)
                if masked:
                    keys = lax.broadcasted_iota(jnp.int32, st.shape, 0)
                    queries = lax.broadcasted_iota(jnp.int32, st.shape, 1)
                    pt = jnp.where(keys <= queries, pt, 0.0)
                dv = dv + jnp.dot(pt.astype(BF16), dob, preferred_element_type=F32)
                dpt = lax.dot_general(vbs[hd], dob, NT, preferred_element_type=F32)
                dst = (pt * (dpt - delta)).astype(BF16)
                dk = dk + jnp.dot(dst, qb, preferred_element_type=F32)
                dq_ref[pl.ds(off, width * tq), lanes] += lax.dot_general(dst, kbs[hd], TN,
                                                                         preferred_element_type=F32)
                out.append((dk, dv))
            return tuple(out)

        zero = jnp.zeros((tq, HEAD_PAD), F32)
        wide = jnp.minimum(nq - 1 - j, 1)
        rest = nq - 1 - j - wide
        carry = ((zero, zero),) * HEADS_PER_STEP
        carry = lax.fori_loop(0, wide, lambda _, c: chunk(j, c, True, 2), carry)
        carry = lax.fori_loop(0, 1 - wide, lambda _, c: chunk(j, c, True), carry)
        carry = lax.fori_loop(0, rest // 2, lambda pp, c: chunk(j + 1 + wide + 2 * pp, c, False, 2), carry)
        carry = lax.fori_loop(0, lax.rem(rest, 2), lambda _, c: chunk(nq - 1, c, False), carry)
        for hd in range(HEADS_PER_STEP):
            lanes = slice(hd * HEAD_PAD, (hd + 1) * HEAD_PAD)
            dk_ref[:, lanes] = carry[hd][0]
            dv_ref[:, lanes] = carry[hd][1].astype(dv_ref.dtype)

    tspec = pl.BlockSpec((tq, STEP_LANES), lambda b, hh, j: (b * nq + j, hh))
    fullspec = pl.BlockSpec((tp, STEP_LANES), lambda b, hh, j: (b, hh))
    rspec = pl.BlockSpec((HEADS_PER_STEP, nq, SUBLANES, tq), lambda b, hh, j: (b * HEAD_STEPS + hh, 0, 0, 0))
    shp = (nb * tp, MLA_HEADS * HEAD_PAD)
    return pl.pallas_call(
        body, out_shape=(jax.ShapeDtypeStruct(shp, F32), jax.ShapeDtypeStruct(shp, F32),
                         jax.ShapeDtypeStruct(shp, BF16)),
        grid=(nb, HEAD_STEPS, nq), in_specs=[fullspec, tspec, tspec, fullspec, rspec, rspec],
        out_specs=(fullspec, tspec, tspec), compiler_params=_cparams(3),
        name="od_flash_bwd")(q, k, v, do, lse_r, delta_r)


def _odd_fwd(x, p, tables, m, tp, nb):
    scale = QK_HEAD ** -0.5
    h = _rms_fwd(x, p["norm"], "od_norm")
    u = _matmul(h, p["w_in"], "nn", F32, name="od_in")
    cq = u[:, :Q_LORA]
    ckv = u[:, Q_LORA:Q_LORA + KV_LORA]
    cqn = _rms_fwd(cq, p["q_norm"], "od_q_norm")
    ckvn = _rms_fwd(ckv, p["kv_norm"], "od_kv_norm")
    q_raw = _matmul(cqn, p["w_uq"], "nn", F32, name="od_uq")
    k_raw = _matmul(ckvn, p["w_uk"], "nn", F32, name="od_uk")
    v = _matmul(ckvn, p["w_uv"], "nn", BF16, name="od_uv")
    q = _rope_call("od_rope_q", lambda qv, c, lo, hi: _rope(qv, c, lo, hi) * scale, m, tp, [(q_raw, None)], tables,
                   BF16)
    kr_col = (Q_LORA + KV_LORA) // HEAD_PAD
    k = _rope_call("od_rope_k", lambda kv, kr, c, lo, hi: kv + kr, m, tp, [(k_raw, None), (u, kr_col)], tables, BF16,
                   shared_pre=lambda uv, c, lo, hi: _rope(pltpu.roll(uv, QK_NOPE, 1), c, lo, hi))
    o, lse_c = _flash_fwd(q, k, v, nb, tp)
    out = _matmul(o, p["w_out"], "nn", F32, residual=x, name="od_out")
    return out, (x, h, cq, ckv, cqn, ckvn, q, k, v, o, lse_c)


def _odd_bwd(dout, p, tables, saved, m, tp, nb):
    scale = QK_HEAD ** -0.5
    x, h, cq, ckv, cqn, ckvn, q, k, v, o, lse_c = saved
    do = _matmul(dout, p["w_out"], "nt", BF16, name="od_out_dx")
    d_w_out = _matmul(o, dout, "tn", F32, name="od_out_dw")
    lse_r, delta_r = _flash_prep(o, do, lse_c, nb, tp)
    dq, dk, dv = _flash_bwd(q, k, v, do, lse_r, delta_r, nb, tp)
    dq_raw = _rope_call("od_rope_q_bwd", lambda d, c, lo, hi: _rope_t(d, c, lo, hi) * scale, m, tp, [(dq, None)],
                        tables, BF16)
    dkr = _rope_k_bwd(dk, tables, m, tp)
    d_w_uq = _matmul(cqn, dq_raw, "tn", F32, name="od_uq_dw")
    d_w_uk = _matmul(ckvn, dk, "tn", F32, name="od_uk_dw")
    d_w_uv = _matmul(ckvn, dv, "tn", F32, name="od_uv_dw")
    dcqn = _matmul(dq_raw, p["w_uq"], "nt", F32, name="od_uq_dx")
    dckvn = _matmul(dk, p["w_uk"], "nt", F32, name="od_uk_dx")
    dckvn = _matmul(dv, p["w_uv"], "nt", F32, residual=dckvn, name="od_uv_dx")
    dcq, d_q_norm = _rms_bwd(cq, p["q_norm"], dcqn, None, "od_q_norm_bwd")
    dckv, d_kv_norm = _rms_bwd(ckv, p["kv_norm"], dckvn, None, "od_kv_norm_bwd")
    du = jnp.concatenate([dcq, dckv, dkr], axis=1)
    d_w_in = _matmul(h, du, "tn", F32, name="od_in_dw")
    dh = _matmul(du, p["w_in"], "nt", F32, name="od_in_dx")
    dx, d_norm = _rms_bwd(x, p["norm"], dh, dout, "od_norm_bwd")
    return dx, dict(norm=d_norm, w_in=d_w_in, q_norm=d_q_norm, kv_norm=d_kv_norm, w_uq=d_w_uq, w_uk=d_w_uk,
                    w_uv=d_w_uv, w_out=d_w_out)


def _loss_head(hf, g, target, tp, t_real):
    m, c = hf.shape
    tm = _pick(tp, ROW_TILE)
    tps = tp // tm

    def body(x_ref, g_ref, t_ref, dx_ref, dg_ref, loss_ref):
        i = pl.program_id(0)
        xf = x_ref[...]
        r = lax.rsqrt(jnp.mean(xf * xf, axis=-1, keepdims=True) + EPS)
        xn = xf * r
        t_pos = lax.rem(i, tps) * tm + lax.broadcasted_iota(jnp.int32, (tm, 1), 0)
        valid = jnp.logical_and(t_pos >= N_META, t_pos < t_real)
        err = jnp.where(valid, xn * g_ref[...] - t_ref[...], 0.0)
        dyf = err * (1.0 / c)
        dyg = dyf * g_ref[...]
        dx_ref[...] = r * (dyg - xn * jnp.mean(dyg * xn, axis=-1, keepdims=True))

        @pl.when(i == 0)
        def _():
            dg_ref[...] = jnp.zeros_like(dg_ref)
            loss_ref[...] = jnp.zeros_like(loss_ref)

        dg_ref[...] += jnp.sum(dyf * xn, axis=0, keepdims=True)
        loss_ref[...] += (0.5 / c) * jnp.sum(jnp.sum(err * err, axis=1, keepdims=True), axis=0, keepdims=True)

    row = pl.BlockSpec((tm, c), lambda i: (i, 0))
    vec = pl.BlockSpec((1, c), lambda i: (0, 0))
    return pl.pallas_call(
        body, out_shape=(jax.ShapeDtypeStruct((m, c), F32), jax.ShapeDtypeStruct((1, c), F32),
                         jax.ShapeDtypeStruct((1, 1), F32)),
        grid=(m // tm,), in_specs=[row, vec, row], out_specs=(row, vec, pl.BlockSpec((1, 1), lambda i: (0, 0))),
        compiler_params=_cparams(1), name="loss_head")(hf, g, target)


def _meta_grad(dh0, nb, tp):
    d = dh0.shape[1]

    def body(x_ref, o_ref):
        @pl.when(pl.program_id(0) == 0)
        def _():
            o_ref[...] = jnp.zeros_like(o_ref)

        o_ref[...] += x_ref[...]

    return pl.pallas_call(
        body, out_shape=jax.ShapeDtypeStruct((N_META, d), F32), grid=(nb,),
        in_specs=[pl.BlockSpec((N_META, d), lambda b: (b * (tp // N_META), 0))],
        out_specs=pl.BlockSpec((N_META, d), lambda b: (0, 0)), compiler_params=_cparams(1), name="meta_grad")(dh0)


def _mesh_pos():
    x, y, c = lax.axis_index("x"), lax.axis_index("y"), lax.axis_index("c")
    return x, y, c


N_CHIP = 4
MESH_ID = pl.DeviceIdType.MESH


def _peer_chip(x, y, k):
    px = 1 - x if k & 2 else x
    py = 1 - y if k & 1 else y
    return px, py


def _all_gather(arrays):
    n = len(arrays)

    def body(*refs):
        srcs, outs = refs[:n], refs[n:2 * n]
        send_sems, recv_sems, local_sems = refs[2 * n:]
        x, y, c = _mesh_pos()
        me = 4 * x + 2 * y + c
        sibling = (x, y, 1 - c)

        def copy(a, sem, src, block, to):
            return pltpu.make_async_remote_copy(
                src_ref=src, dst_ref=outs[a].at[block], send_sem=send_sems.at[a, sem], recv_sem=recv_sems.at[a, sem],
                device_id=to, device_id_type=MESH_ID)

        local = [pltpu.make_async_copy(srcs[a], outs[a].at[me], local_sems.at[a]) for a in range(n)]
        for cp in local:
            cp.start()
        sends = [copy(a, 0, srcs[a], me, sibling) for a in range(n)]
        for k in range(1, N_CHIP):
            px, py = _peer_chip(x, y, k)
            sends += [copy(a, k, srcs[a], me, (px, py, c)) for a in range(n)]
        for cp in sends:
            cp.start()
        for k in range(1, N_CHIP):
            px, py = _peer_chip(x, y, k)
            block = 4 * px + 2 * py + c
            for a in range(n):
                copy(a, k, srcs[a], block, sibling).wait_recv()
            passed = [copy(a, N_CHIP - 1 + k, outs[a].at[block], block, sibling) for a in range(n)]
            for cp in passed:
                cp.start()
            sends += passed
        for a in range(n):
            copy(a, 0, srcs[a], 4 * x + 2 * y + (1 - c), sibling).wait_recv()
        for k in range(1, N_CHIP):
            px, py = _peer_chip(x, y, k)
            for a in range(n):
                copy(a, N_CHIP - 1 + k, srcs[a], 4 * px + 2 * py + (1 - c), sibling).wait_recv()
        for cp in sends:
            cp.wait_send()
        for cp in local:
            cp.wait()

    any_spec = pl.BlockSpec(memory_space=pl.ANY)
    out_shape = tuple(jax.ShapeDtypeStruct((N_DEV,) + a.shape, a.dtype) for a in arrays)
    return pl.pallas_call(
        body, out_shape=out_shape, in_specs=[any_spec] * n, out_specs=(any_spec,) * n,
        scratch_shapes=[pltpu.SemaphoreType.DMA((n, N_DEV - 1)), pltpu.SemaphoreType.DMA((n, N_DEV - 1)),
                        pltpu.SemaphoreType.DMA((n,))],
        name="weight_all_gather")(*arrays)


def _pair_exchange(arrays):
    n = len(arrays)

    def body(*refs):
        srcs, outs = refs[:n], refs[n:2 * n]
        send_sems, recv_sems = refs[2 * n:]
        x, y, c = _mesh_pos()
        copies = [pltpu.make_async_remote_copy(
            src_ref=srcs[a], dst_ref=outs[a], send_sem=send_sems.at[a], recv_sem=recv_sems.at[a],
            device_id=(x, y, 1 - c), device_id_type=MESH_ID) for a in range(n)]
        for cp in copies:
            cp.start()
        for cp in copies:
            cp.wait()

    any_spec = pl.BlockSpec(memory_space=pl.ANY)
    return pl.pallas_call(
        body, out_shape=tuple(jax.ShapeDtypeStruct(a.shape, a.dtype) for a in arrays), in_specs=[any_spec] * n,
        out_specs=(any_spec,) * n, scratch_shapes=[pltpu.SemaphoreType.DMA((n,)), pltpu.SemaphoreType.DMA((n,))],
        name="grad_pair_exchange")(*arrays)


def _chip_exchange(arrays):
    n = len(arrays)

    def body(*refs):
        srcs, outs = refs[:n], refs[n:2 * n]
        send_sems, recv_sems, local_sems = refs[2 * n:]
        x, y, c = _mesh_pos()
        q = 2 * x + y
        local = [pltpu.make_async_copy(srcs[a].at[q], outs[a].at[q], local_sems.at[a]) for a in range(n)]
        for cp in local:
            cp.start()

        def copy(a, k, to_q, from_q, px, py):
            return pltpu.make_async_remote_copy(
                src_ref=srcs[a].at[to_q], dst_ref=outs[a].at[from_q], send_sem=send_sems.at[a, k - 1],
                recv_sem=recv_sems.at[a, k - 1], device_id=(px, py, c), device_id_type=MESH_ID)

        sends = []
        for k in range(1, N_CHIP):
            px, py = _peer_chip(x, y, k)
            sends += [copy(a, k, 2 * px + py, q, px, py) for a in range(n)]
        for cp in sends:
            cp.start()
        for k in range(1, N_CHIP):
            px, py = _peer_chip(x, y, k)
            for a in range(n):
                copy(a, k, q, 2 * px + py, px, py).wait_recv()
        for cp in sends:
            cp.wait_send()
        for cp in local:
            cp.wait()

    any_spec = pl.BlockSpec(memory_space=pl.ANY)
    return pl.pallas_call(
        body, out_shape=tuple(jax.ShapeDtypeStruct(a.shape, a.dtype) for a in arrays), in_specs=[any_spec] * n,
        out_specs=(any_spec,) * n,
        scratch_shapes=[pltpu.SemaphoreType.DMA((n, N_CHIP - 1)), pltpu.SemaphoreType.DMA((n, N_CHIP - 1)),
                        pltpu.SemaphoreType.DMA((n,))],
        name="grad_chip_exchange")(*arrays)


REDUCE_BLOCK_BYTES = 512 * 1024


def _pair_add(a, b):
    p, r, c = a.shape
    tr = _reduce_rows(r, c)

    def body(a_ref, b_ref, o_ref):
        o_ref[...] = (a_ref[...].astype(F32) + b_ref[...].astype(F32)).astype(o_ref.dtype)

    blk = pl.BlockSpec((None, tr, c), lambda s, i: (s, i, 0))
    return pl.pallas_call(
        body, out_shape=jax.ShapeDtypeStruct(a.shape, a.dtype), grid=(p, r // tr), in_specs=[blk, blk], out_specs=blk,
        compiler_params=_cparams(2), name="grad_pair_add")(a, b)


def _reduce_rows(r, c):
    best = None
    for t in range(16, r + 1, 16):
        if r % t == 0 and t * c * 4 <= REDUCE_BLOCK_BYTES:
            best = t
    assert best is not None, (r, c)
    return best


def _reduce_adamw(parts, w, mom, vel):
    n_parts, r, c = parts.shape
    tr = _reduce_rows(r, c)
    c1 = 1.0 - ADAM_B1 ** ADAM_STEP
    c2 = 1.0 - ADAM_B2 ** ADAM_STEP

    def body(p_ref, w_ref, m_ref, v_ref, g_ref, d_ref, mo_ref, vo_ref):
        g = p_ref[0].astype(F32)
        for s in range(1, n_parts):
            g = g + p_ref[s].astype(F32)
        mn = ADAM_B1 * m_ref[...] + (1.0 - ADAM_B1) * g
        vn = ADAM_B2 * v_ref[...] + (1.0 - ADAM_B2) * (g * g)
        m_hat = mn / c1
        v_hat = vn / c2
        g_ref[...] = g
        d_ref[...] = -ADAM_LR * (m_hat / (jnp.sqrt(v_hat) + ADAM_EPS) + ADAM_WD * w_ref[...])
        mo_ref[...] = mn
        vo_ref[...] = vn

    blk = pl.BlockSpec((tr, c), lambda i: (i, 0))
    shp = jax.ShapeDtypeStruct((r, c), F32)
    return pl.pallas_call(
        body, out_shape=(shp, shp, shp, shp), grid=(r // tr,),
        in_specs=[pl.BlockSpec((n_parts, tr, c), lambda i: (0, i, 0)), blk, blk, blk], out_specs=(blk, blk, blk, blk),
        compiler_params=_cparams(1), name="reduce_adamw")(parts, w, mom, vel)


def _pack_rows(pieces, width, row_multiple, dtype):
    flat = jnp.concatenate([p.astype(dtype).reshape(-1) for p in pieces])
    rows = -(-flat.shape[0] // (width * row_multiple)) * row_multiple
    return jnp.pad(flat, (0, rows * width - flat.shape[0])).reshape(rows, width)


def _unshard(gathered, axis):
    moved = jnp.moveaxis(gathered, 0, axis)
    shape = list(moved.shape)
    shape[axis:axis + 2] = [shape[axis] * shape[axis + 1]]
    return moved.reshape(shape)


def _to_slots(full, axis):
    shape = list(full.shape)
    shape[axis:axis + 1] = [N_DEV, shape[axis] // N_DEV]
    return jnp.moveaxis(full.reshape(shape), axis, 0)


def _core_slots(full, axis, core):
    shape = list(full.shape)
    shape[axis:axis + 1] = [N_CHIP, 2, shape[axis] // N_DEV]
    picked = lax.dynamic_index_in_dim(full.reshape(shape), core, axis + 1, keepdims=False)
    return jnp.moveaxis(picked, axis, 0)


def _block_diag(w):
    hh, d, _ = w.shape
    eye = jnp.eye(hh, dtype=w.dtype)
    return (w[:, :, None, :] * eye[:, None, :, None]).reshape(hh * d, hh * d)


def _block_diag_t(full, hh):
    d = full.shape[0] // hh
    f4 = full.reshape(hh, d, hh, d)
    return jnp.stack([f4[i, :, i, :] for i in range(hh)], axis=0)


def _pad_heads(w, width):
    r = w.shape[0]
    w3 = w.reshape(r, MLA_HEADS, width)
    return jnp.pad(w3, ((0, 0), (0, 0), (0, HEAD_PAD - width))).reshape(r, MLA_HEADS * HEAD_PAD)


def _unpad_heads(w, width):
    r = w.shape[0]
    return w.reshape(r, MLA_HEADS, HEAD_PAD)[:, :, :width].reshape(r, MLA_HEADS * width)


def kernel(x, meta_tokens, ev_norm, ev_w_in, ev_conv_a, ev_conv_b, ev_conv_b_bias, ev_gate_r_w, ev_gate_r_b, ev_gate_i_w, ev_gate_i_b, ev_lru_lambda, ev_w_out, od_norm, od_w_in, od_q_norm, od_kv_norm, od_w_uq, od_w_ukv, od_w_out, ffn_norm, ffn_w_up, ffn_conv_w, ffn_conv_b, ffn_w_down, final_norm, loss_target, m_meta_tokens, m_ev_norm, m_ev_w_in, m_ev_conv_a, m_ev_conv_b, m_ev_conv_b_bias, m_ev_gate_r_w, m_ev_gate_r_b, m_ev_gate_i_w, m_ev_gate_i_b, m_ev_lru_lambda, m_ev_w_out, m_od_norm, m_od_w_in, m_od_q_norm, m_od_kv_norm, m_od_w_uq, m_od_w_ukv, m_od_w_out, m_ffn_norm, m_ffn_w_up, m_ffn_conv_w, m_ffn_conv_b, m_ffn_w_down, m_final_norm, v_meta_tokens, v_ev_norm, v_ev_w_in, v_ev_conv_a, v_ev_conv_b, v_ev_conv_b_bias, v_ev_gate_r_w, v_ev_gate_r_b, v_ev_gate_i_w, v_ev_gate_i_b, v_ev_lru_lambda, v_ev_w_out, v_od_norm, v_od_w_in, v_od_q_norm, v_od_kv_norm, v_od_w_uq, v_od_w_ukv, v_od_w_out, v_ffn_norm, v_ffn_w_up, v_ffn_conv_w, v_ffn_conv_b, v_ffn_w_down, v_final_norm):
    given = dict(locals())
    names = [n for n, _ in PARAMS]
    axis_of = dict(PARAMS)
    w_loc = {n: given[n] for n in names}
    m_loc = {n: given["m_" + n] for n in names}
    v_loc = {n: given["v_" + n] for n in names}
    sharded = [n for n in names if axis_of[n] is not None]
    replicated = [n for n in names if axis_of[n] is None]
    small = [n for n in sharded if n not in BIG]

    nb, seq, d = x.shape
    t_real = N_META + seq
    tp = -(-t_real // ROW_TILE) * ROW_TILE
    m = nb * tp

    small_pack = _pack_rows([w_loc[n] for n in small], LANES, SUBLANES, F32)
    gathered = _all_gather([w_loc[n].astype(BF16) for n in BIG] + [small_pack])
    full = {n: w_loc[n] for n in replicated}
    for n, g in zip(BIG, gathered[:-1]):
        full[n] = _unshard(g, axis_of[n])
    flat = gathered[-1].reshape(N_DEV, -1)
    off = 0
    for n in small:
        shard = w_loc[n].shape
        size = math.prod(shard)
        full[n] = _unshard(flat[:, off:off + size].reshape((N_DEV,) + shard), axis_of[n])
        off += size

    tables = _rope_tables(tp)

    def even_params(j):
        w_out = full["ev_w_out"][j]
        return dict(norm=full["ev_norm"][j][None], w_in=full["ev_w_in"][j], conv_a=full["ev_conv_a"][j],
                    conv_b=full["ev_conv_b"][j], conv_b_bias=full["ev_conv_b_bias"][j][None],
                    gate_r=_block_diag(full["ev_gate_r_w"][j]).astype(BF16),
                    gate_i=_block_diag(full["ev_gate_i_w"][j]).astype(BF16),
                    gate_r_b=full["ev_gate_r_b"][j][None], gate_i_b=full["ev_gate_i_b"][j][None],
                    lam=full["ev_lru_lambda"][j][None], w_out=w_out, w_out_a=w_out[:LRU_WIDTH],
                    w_out_b=w_out[LRU_WIDTH:])

    def odd_params(j):
        w_ukv = full["od_w_ukv"][j].reshape(KV_LORA, MLA_HEADS, QK_NOPE + V_HEAD)
        w_uk = w_ukv[:, :, :QK_NOPE].reshape(KV_LORA, MLA_HEADS * QK_NOPE)
        w_uv = w_ukv[:, :, QK_NOPE:].reshape(KV_LORA, MLA_HEADS * V_HEAD)
        w_out = full["od_w_out"][j].reshape(MLA_HEADS, V_HEAD, d)
        w_out = jnp.pad(w_out, ((0, 0), (0, HEAD_PAD - V_HEAD), (0, 0))).reshape(MLA_HEADS * HEAD_PAD, d)
        return dict(norm=full["od_norm"][j][None], w_in=jnp.pad(full["od_w_in"][j], ((0, 0), (0, ODD_IN_PAD - ODD_IN))),
                    q_norm=full["od_q_norm"][j][None], kv_norm=full["od_kv_norm"][j][None],
                    w_uq=_pad_heads(full["od_w_uq"][j], QK_HEAD), w_uk=_pad_heads(w_uk, QK_NOPE),
                    w_uv=_pad_heads(w_uv, V_HEAD), w_out=w_out)

    def ffn_params(layer):
        w_up = full["ffn_w_up"][layer]
        return dict(norm=full["ffn_norm"][layer][None], w_up=w_up, conv_w=full["ffn_conv_w"][layer],
                    conv_b=full["ffn_conv_b"][layer][None], w_down=full["ffn_w_down"][layer])

    meta = jnp.broadcast_to(full["meta_tokens"][None], (nb, N_META, d))
    h0 = jnp.concatenate([meta, x, jnp.zeros((nb, tp - t_real, d), F32)], axis=1).reshape(m, d)
    hcur = h0
    tape = []
    for layer in range(4):
        j = layer // 2
        if layer % 2 == 0:
            mp = even_params(j)
            hcur, saved = _even_fwd(hcur, mp, m, tp, nb)
        else:
            mp = odd_params(j)
            hcur, saved = _odd_fwd(hcur, mp, tables, m, tp, nb)
        fp = ffn_params(layer)
        hcur, fsaved = _ffn_fwd(hcur, fp, m, tp)
        tape.append((mp, saved, fp, fsaved))

    target = jnp.pad(loss_target, ((0, 0), (N_META, tp - t_real), (0, 0))).reshape(m, d)
    dh, d_final_norm, loss_part = _loss_head(hcur, full["final_norm"][None], target, tp, t_real)

    grads = {"final_norm": d_final_norm[0]}
    ev_g, od_g, ffn_g = [None, None], [None, None], [None] * 4
    for layer in reversed(range(4)):
        mp, saved, fp, fsaved = tape[layer]
        dh, ffn_g[layer] = _ffn_bwd(dh, fp, fsaved, m, tp)
        if layer % 2 == 0:
            dh, ev_g[layer // 2] = _even_bwd(dh, mp, saved, m, tp, nb)
        else:
            dh, od_g[layer // 2] = _odd_bwd(dh, mp, tables, saved, m, tp, nb)

    dh3 = dh.reshape(nb, tp, d)
    grad_x = dh3[:, N_META:t_real]
    grads["meta_tokens"] = _meta_grad(dh, nb, tp)

    def stack(lst, key, fn=lambda a: a):
        return jnp.stack([fn(g[key]) for g in lst], axis=0)

    grads["ev_norm"] = stack(ev_g, "norm", lambda a: a[0])
    grads["ev_w_in"] = stack(ev_g, "w_in")
    grads["ev_conv_a"] = stack(ev_g, "conv_a")
    grads["ev_conv_b"] = stack(ev_g, "conv_b")
    grads["ev_conv_b_bias"] = stack(ev_g, "conv_b_bias", lambda a: a[0])
    grads["ev_gate_r_w"] = stack(ev_g, "gate_r", lambda a: _block_diag_t(a, 8))
    grads["ev_gate_r_b"] = stack(ev_g, "gate_r_b", lambda a: a[0])
    grads["ev_gate_i_w"] = stack(ev_g, "gate_i", lambda a: _block_diag_t(a, 8))
    grads["ev_gate_i_b"] = stack(ev_g, "gate_i_b", lambda a: a[0])
    grads["ev_lru_lambda"] = stack(ev_g, "lam", lambda a: a[0])
    grads["ev_w_out"] = stack(ev_g, "w_out")
    grads["od_norm"] = stack(od_g, "norm", lambda a: a[0])
    grads["od_w_in"] = stack(od_g, "w_in", lambda a: a[:, :ODD_IN])
    grads["od_q_norm"] = stack(od_g, "q_norm", lambda a: a[0])
    grads["od_kv_norm"] = stack(od_g, "kv_norm", lambda a: a[0])
    grads["od_w_uq"] = stack(od_g, "w_uq", lambda a: _unpad_heads(a, QK_HEAD))

    def ukv(g):
        gk = g["w_uk"].reshape(KV_LORA, MLA_HEADS, HEAD_PAD)[:, :, :QK_NOPE]
        gv = g["w_uv"].reshape(KV_LORA, MLA_HEADS, HEAD_PAD)[:, :, :V_HEAD]
        return jnp.concatenate([gk, gv], axis=2).reshape(KV_LORA, MLA_HEADS * (QK_NOPE + V_HEAD))

    grads["od_w_ukv"] = jnp.stack([ukv(g) for g in od_g], axis=0)
    grads["od_w_out"] = stack(od_g, "w_out", lambda a: a.reshape(MLA_HEADS, HEAD_PAD, d)[:, :V_HEAD].reshape(-1, d))
    grads["ffn_norm"] = stack(ffn_g, "norm", lambda a: a[0])
    grads["ffn_w_up"] = stack(ffn_g, "w_up")
    grads["ffn_conv_w"] = stack(ffn_g, "conv_w")
    grads["ffn_conv_b"] = stack(ffn_g, "conv_b", lambda a: a[0])
    grads["ffn_w_down"] = stack(ffn_g, "w_down")

    order = small + replicated
    slot_parts = [_to_slots(grads[n], axis_of[n]).reshape(N_DEV, -1) for n in small]
    slot_parts += [jnp.broadcast_to(grads[n].reshape(1, -1), (N_DEV, grads[n].size)) for n in replicated]
    slot_parts.append(jnp.broadcast_to(loss_part, (N_DEV, 1)))
    g_flat = jnp.concatenate(slot_parts, axis=1)
    n_flat = g_flat.shape[1]
    rows = -(-n_flat // (1024 * 128)) * 128
    g_small = jnp.pad(g_flat, ((0, 0), (0, rows * 1024 - n_flat))).reshape(N_DEV, rows, 1024)

    def rows_of(n):
        shard = w_loc[n].shape
        return (math.prod(shard[:-1]), shard[-1])

    core = lax.axis_index("c")

    def core_slots(n, which):
        return _core_slots(grads[n], axis_of[n], which).astype(BF16).reshape((N_CHIP,) + rows_of(n))

    small_by_core = jnp.swapaxes(g_small.reshape((N_CHIP, 2) + g_small.shape[1:]), 0, 1)
    mine = [core_slots(n, core) for n in BIG] + [lax.dynamic_index_in_dim(small_by_core, core, 0, keepdims=False)]
    theirs = [core_slots(n, 1 - core) for n in BIG]
    theirs.append(lax.dynamic_index_in_dim(small_by_core, 1 - core, 0, keepdims=False))
    from_sibling = _pair_exchange(theirs)
    parts = _chip_exchange([_pair_add(a, b) for a, b in zip(mine, from_sibling)])

    g_out, d_out, m_out, v_out = {}, {}, {}, {}
    for n, part in zip(BIG, parts[:-1]):
        res = _reduce_adamw(part, *[t[n].reshape(rows_of(n)) for t in (w_loc, m_loc, v_loc)])
        for out, r in zip((g_out, d_out, m_out, v_out), res):
            out[n] = r.reshape(w_loc[n].shape)

    def flat_local(tree):
        flat = jnp.concatenate([tree[n].reshape(-1) for n in order])
        return jnp.pad(flat, (0, rows * 1024 - flat.shape[0])).reshape(rows, 1024)

    res = _reduce_adamw(parts[-1], flat_local(w_loc), flat_local(m_loc), flat_local(v_loc))
    loss = res[0].reshape(-1)[n_flat - 1]
    for out, r in zip((g_out, d_out, m_out, v_out), res):
        flat = r.reshape(-1)
        off = 0
        for n in order:
            size = w_loc[n].size
            out[n] = flat[off:off + size].reshape(w_loc[n].shape)
            off += size
    return (loss, grad_x, *[g_out[n] for n in names], *[d_out[n] for n in names], *[m_out[n] for n in names],
            *[v_out[n] for n in names])
```

```python
import math

import jax
import jax.numpy as jnp
from jax import lax
from jax.experimental import pallas as pl
from jax.experimental.pallas import tpu as pltpu

F32 = jnp.float32
BF16 = jnp.bfloat16

N_DEV = 8
N_META = 16
EPS = 1e-6
LRU_C = 8.0
MLA_HEADS = 16
QK_NOPE = 64
QK_ROPE = 32
QK_HEAD = QK_NOPE + QK_ROPE
V_HEAD = 64
HEAD_PAD = 128
Q_LORA = 384
KV_LORA = 256
ODD_IN = Q_LORA + KV_LORA + QK_ROPE
ODD_IN_PAD = 768
ROPE_BASE = 10000.0
LRU_WIDTH = 512
D_FF = 2816

ADAM_LR = 0.001
ADAM_B1 = 0.9
ADAM_B2 = 0.999
ADAM_EPS = 1e-08
ADAM_WD = 0.01
ADAM_STEP = 10

ROW_TILE = 384
SUBLANES = 8
HALO_ROWS = 16
LANES = 128
VMEM_LIMIT = 48 * 1024 * 1024
NEG = -1e30

PARAMS = (
    ("meta_tokens", 1), ("ev_norm", None), ("ev_w_in", 2), ("ev_conv_a", 2), ("ev_conv_b", 2),
    ("ev_conv_b_bias", None), ("ev_gate_r_w", None), ("ev_gate_r_b", None), ("ev_gate_i_w", None),
    ("ev_gate_i_b", None), ("ev_lru_lambda", None), ("ev_w_out", 1), ("od_norm", 1), ("od_w_in", 1),
    ("od_q_norm", 1), ("od_kv_norm", 1), ("od_w_uq", 2), ("od_w_ukv", 2), ("od_w_out", 1),
    ("ffn_norm", None), ("ffn_w_up", 2), ("ffn_conv_w", 2), ("ffn_conv_b", None), ("ffn_w_down", 1),
    ("final_norm", None),
)
BIG = ("ev_w_in", "ev_w_out", "od_w_in", "od_w_uq", "od_w_ukv", "od_w_out", "ffn_w_up", "ffn_w_down")


def _cparams(n_grid):
    return pltpu.CompilerParams(dimension_semantics=("arbitrary",) * n_grid, vmem_limit_bytes=VMEM_LIMIT)


def _pick(dim, target):
    if dim <= target:
        return dim
    best = None
    for t in range(LANES, target + 1, LANES):
        if dim % t == 0:
            best = t
    assert best is not None, (dim, target)
    return best


MATMUL_VMEM_BUDGET = 38 * 1024 * 1024
HBM_BYTES_PER_US = 3.0e6
MXU_FLOPS_PER_US = 9.0e8
ACC_BYTES_PER_US = 7.6e6
GRID_STEP_US = 0.35


def _tile_candidates(dim):
    return [t for t in range(LANES, dim + 1, LANES) if dim % t == 0] or [dim]


def _matmul_tiles(m, n, k, sa, sb, so, sr, transposed_lhs):
    best, best_cost = None, None
    for tm in _tile_candidates(m):
        for tn in _tile_candidates(n):
            for tk in _tile_candidates(k):
                nk = k // tk
                vmem = 2 * (tm * tk * sa + tk * tn * sb) + tm * tn * ((4 if nk > 1 else 0) + 2 * so + 2 * sr)
                vmem += (tm * tk * 2 if sa > 2 else 0) + (tk * tn * 2 if sb > 2 else 0) + tm * tn * 4
                if vmem > MATMUL_VMEM_BUDGET:
                    continue
                steps = (m // tm) * (n // tn) * nk
                traffic = m * k * sa * (n // tn) + k * n * sb * (m // tm) + m * n * (so + sr)
                acc_us = steps * tm * tn * 4 / ACC_BYTES_PER_US if nk > 1 else 0.0
                busy_us = 0.0 if transposed_lhs else 2.0 * m * n * k / MXU_FLOPS_PER_US + acc_us
                cost = max(traffic / HBM_BYTES_PER_US, busy_us) + steps * GRID_STEP_US
                if best_cost is None or cost < best_cost:
                    best, best_cost = (tm, tn, tk), cost
    assert best is not None, (m, n, k)
    return best


def _matmul(a, b, mode, out_dtype=F32, residual=None, name="mm", b_col_off=0):
    if mode == "nn":
        (m, k), (k2, n) = a.shape, b.shape
    elif mode == "nt":
        (m, k), n = a.shape, b.shape[0]
        k2 = k if b_col_off or b.shape[1] > k else b.shape[1]
    else:
        (k, m), (k2, n) = a.shape, b.shape
    assert k == k2, (a.shape, b.shape, mode)
    tm, tn, tk = _matmul_tiles(m, n, k, a.dtype.itemsize, b.dtype.itemsize, jnp.dtype(out_dtype).itemsize,
                               0 if residual is None else residual.dtype.itemsize, mode == "tn")
    nk = k // tk
    if mode == "tn":
        a_spec = pl.BlockSpec((tk, tm), lambda i, j, kk: (kk, i))
        dims = (((0,), (0,)), ((), ()))
    else:
        a_spec = pl.BlockSpec((tm, tk), lambda i, j, kk: (i, kk))
        dims = (((1,), (1 if mode == "nt" else 0,)), ((), ()))
    if mode == "nt":
        assert b_col_off % tk == 0, (b_col_off, tk)
        b_spec = pl.BlockSpec((tn, tk), lambda i, j, kk: (j, kk + b_col_off // tk))
    else:
        b_spec = pl.BlockSpec((tk, tn), lambda i, j, kk: (kk, j))
    o_spec = pl.BlockSpec((tm, tn), lambda i, j, kk: (i, j))
    has_res = residual is not None

    def body(*refs):
        a_ref, b_ref = refs[:2]
        r_ref = refs[2] if has_res else None
        o_ref = refs[3] if has_res else refs[2]
        part = lax.dot_general(a_ref[...].astype(BF16), b_ref[...].astype(BF16), dims, preferred_element_type=F32)

        def finish(out):
            if has_res:
                out = out + r_ref[...].astype(F32)
            o_ref[...] = out.astype(o_ref.dtype)

        if nk == 1:
            finish(part)
            return
        acc_ref = refs[-1]
        kk = pl.program_id(2)

        @pl.when(kk == 0)
        def _():
            acc_ref[...] = part

        @pl.when(kk > 0)
        def _():
            acc_ref[...] += part

        @pl.when(kk == nk - 1)
        def _():
            finish(acc_ref[...])

    in_specs = [a_spec, b_spec] + ([o_spec] if has_res else [])
    args = (a, b) + ((residual,) if has_res else ())
    return pl.pallas_call(
        body, out_shape=jax.ShapeDtypeStruct((m, n), out_dtype), grid=(m // tm, n // tn, nk),
        in_specs=in_specs, out_specs=o_spec, scratch_shapes=[pltpu.VMEM((tm, tn), F32)] if nk > 1 else [],
        compiler_params=_cparams(3), name=name)(*args)


def _rms_fwd(x, g, name):
    m, c = x.shape
    tm = _pick(m, ROW_TILE)

    def body(x_ref, g_ref, o_ref):
        xf = x_ref[...].astype(F32)
        r = lax.rsqrt(jnp.mean(xf * xf, axis=-1, keepdims=True) + EPS)
        o_ref[...] = (xf * r * g_ref[...]).astype(o_ref.dtype)

    return pl.pallas_call(
        body, out_shape=jax.ShapeDtypeStruct((m, c), BF16), grid=(m // tm,),
        in_specs=[pl.BlockSpec((tm, c), lambda i: (i, 0)), pl.BlockSpec((1, c), lambda i: (0, 0))],
        out_specs=pl.BlockSpec((tm, c), lambda i: (i, 0)), compiler_params=_cparams(1), name=name)(x, g)


def _rms_bwd(x, g, dy, residual, name):
    m, c = x.shape
    tm = _pick(m, ROW_TILE)
    has_res = residual is not None

    def body(*refs):
        if has_res:
            x_ref, g_ref, dy_ref, r_ref, dx_ref, dg_ref = refs
        else:
            x_ref, g_ref, dy_ref, dx_ref, dg_ref = refs
        xf = x_ref[...].astype(F32)
        dyf = dy_ref[...].astype(F32)
        r = lax.rsqrt(jnp.mean(xf * xf, axis=-1, keepdims=True) + EPS)
        xn = xf * r
        dyg = dyf * g_ref[...]
        dx = r * (dyg - xn * jnp.mean(dyg * xn, axis=-1, keepdims=True))
        if has_res:
            dx = dx + r_ref[...]
        dx_ref[...] = dx

        @pl.when(pl.program_id(0) == 0)
        def _():
            dg_ref[...] = jnp.zeros_like(dg_ref)

        dg_ref[...] += jnp.sum(dyf * xn, axis=0, keepdims=True)

    row = pl.BlockSpec((tm, c), lambda i: (i, 0))
    vec = pl.BlockSpec((1, c), lambda i: (0, 0))
    in_specs = [row, vec, row] + ([row] if has_res else [])
    args = (x, g, dy) + ((residual,) if has_res else ())
    return pl.pallas_call(
        body, out_shape=(jax.ShapeDtypeStruct((m, c), F32), jax.ShapeDtypeStruct((1, c), F32)), grid=(m // tm,),
        in_specs=in_specs, out_specs=(row, vec), compiler_params=_cparams(1), name=name)(*args)


def _chan_call(name, fn, m, tp, tc, ncol, row_ins=(), prev_ins=(), next_ins=(), chan_ins=(), row_outs=(),
               red_outs=(), row_split=1):
    tm = _pick(tp, ROW_TILE) // row_split
    tps = tp // tm
    nrow = m // tm
    halo_blocks = tm // HALO_ROWS
    last_halo = m // HALO_ROWS - 1
    n_in = len(row_ins) + len(prev_ins) + len(next_ins) + len(chan_ins)
    n_r, n_p, n_n = len(row_ins), len(prev_ins), len(next_ins)

    def body(*refs):
        i = pl.program_id(1)
        pos = lax.rem(i, tps)
        at_start = pos == 0
        at_end = pos == tps - 1
        rows = [r[...].astype(F32) for r in refs[:n_r]]
        prevs = [jnp.where(at_start, 0.0, r[...].astype(F32)[SUBLANES:]) for r in refs[n_r:n_r + n_p]]
        nexts = [jnp.where(at_end, 0.0, r[...].astype(F32)[:SUBLANES]) for r in refs[n_r + n_p:n_r + n_p + n_n]]
        chans = [r[...] for r in refs[n_r + n_p + n_n:n_in]]
        out_refs = refs[n_in:n_in + len(row_outs)]
        red_refs = refs[n_in + len(row_outs):]
        row_vals, red_vals = fn(rows, prevs, nexts, chans)
        for ref, val in zip(out_refs, row_vals):
            ref[...] = val.astype(ref.dtype)
        if red_refs:
            @pl.when(i == 0)
            def _():
                for ref in red_refs:
                    ref[...] = jnp.zeros_like(ref)

            for ref, val in zip(red_refs, red_vals):
                ref[...] += val

    in_specs, args = [], []
    for arr, off in row_ins:
        in_specs.append(pl.BlockSpec((tm, tc), lambda j, i, off=off: (i, j + off)))
        args.append(arr)
    for arr, off in prev_ins:
        in_specs.append(pl.BlockSpec((HALO_ROWS, tc),
                                     lambda j, i, off=off: (jnp.maximum(i * halo_blocks - 1, 0), j + off)))
        args.append(arr)
    for arr, off in next_ins:
        in_specs.append(pl.BlockSpec((HALO_ROWS, tc),
                                     lambda j, i, off=off: (jnp.minimum((i + 1) * halo_blocks, last_halo), j + off)))
        args.append(arr)
    for arr, off in chan_ins:
        in_specs.append(pl.BlockSpec((arr.shape[0], tc), lambda j, i, off=off: (0, j + off)))
        args.append(arr)
    out_shape, out_specs = [], []
    for (dt,) in row_outs:
        out_shape.append(jax.ShapeDtypeStruct((m, ncol * tc), dt))
        out_specs.append(pl.BlockSpec((tm, tc), lambda j, i: (i, j)))
    for (k,) in red_outs:
        out_shape.append(jax.ShapeDtypeStruct((k, ncol * tc), F32))
        out_specs.append(pl.BlockSpec((k, tc), lambda j, i: (0, j)))
    return pl.pallas_call(
        body, out_shape=tuple(out_shape), grid=(ncol, nrow), in_specs=in_specs, out_specs=tuple(out_specs),
        compiler_params=_cparams(2), name=name)(*args)


def _shift_down(x, prev8, s):
    if s == 0:
        return x
    tm, tc = x.shape
    groups = tm // SUBLANES
    xr = pltpu.roll(x.reshape(groups, SUBLANES, tc), s, 1)
    before = jnp.concatenate([pltpu.roll(prev8, s, 0)[None], xr[:-1]], axis=0)
    rid = lax.broadcasted_iota(jnp.int32, xr.shape, 1)
    return jnp.where(rid < s, before, xr).reshape(tm, tc)


def _shift_up(x, next8, s):
    if s == 0:
        return x
    tm, tc = x.shape
    groups = tm // SUBLANES
    xr = pltpu.roll(x.reshape(groups, SUBLANES, tc), SUBLANES - s, 1)
    after = jnp.concatenate([xr[1:], pltpu.roll(next8, SUBLANES - s, 0)[None]], axis=0)
    rid = lax.broadcasted_iota(jnp.int32, xr.shape, 1)
    return jnp.where(rid >= SUBLANES - s, after, xr).reshape(tm, tc)


def _taps(x, prev8, kw):
    return [_shift_down(x, prev8, kw - 1 - k) for k in range(kw)]


def _conv_taps(taps, w):
    y = w[0:1, :] * taps[0]
    for k in range(1, len(taps)):
        y = y + w[k:k + 1, :] * taps[k]
    return y


def _conv_dw_taps(dy, taps):
    shape = (SUBLANES, dy.shape[1])
    rid = lax.broadcasted_iota(jnp.int32, shape, 0)
    out = jnp.zeros(shape, F32)
    for k, tap in enumerate(taps):
        out = out + jnp.where(rid == k, jnp.sum(dy * tap, axis=0, keepdims=True), 0.0)
    return out


def _conv_fwd(x, prev8, w):
    return _conv_taps(_taps(x, prev8, w.shape[0]), w)


def _conv_dw(dy, x, prev8, kw):
    return _conv_dw_taps(dy, _taps(x, prev8, kw))


def _conv_dx(dy, next8, w):
    kw = w.shape[0]
    dx = w[kw - 1:kw, :] * dy
    for k in range(kw - 1):
        dx = dx + w[k:k + 1, :] * _shift_up(dy, next8, kw - 1 - k)
    return dx


def _sigmoid(x):
    return 1.0 / (1.0 + jnp.exp(-x))


def _sigmoid_tanh(x):
    return 0.5 + 0.5 * jnp.tanh(0.5 * x)


def _expm1(x):
    series = x * (1.0 + x * 0.5 * (1.0 + x * (1.0 / 3.0) * (1.0 + x * 0.25 * (1.0 + x * 0.2))))
    return jnp.where(jnp.abs(x) < 0.3, series, jnp.exp(x) - 1.0)


def _softplus_neg(lam):
    e = jnp.exp(-jnp.abs(lam))
    log1p = jnp.where(e < 1e-2, e * (1.0 - e * (0.5 - e * (1.0 / 3.0))), jnp.log(1.0 + e))
    return jnp.maximum(-lam, 0.0) + log1p


GELU_C = math.sqrt(2.0 / math.pi)


def _gelu(x):
    return 0.5 * x * (1.0 + jnp.tanh(GELU_C * (x + 0.044715 * x * x * x)))


def _gelu_grad(x):
    t = jnp.tanh(GELU_C * (x + 0.044715 * x * x * x))
    return 0.5 * (1.0 + t) + 0.5 * x * (1.0 - t * t) * GELU_C * (1.0 + 3.0 * 0.044715 * x * x)


FFN_COL_TILE = 1408


def _ffn_fwd(x, p, m, tp):
    h = _rms_fwd(x, p["norm"], "ffn_norm")
    u = _matmul(h, p["w_up"], "nn", BF16, name="ffn_up")
    tc = FFN_COL_TILE
    ncol = D_FF // tc

    def gate(rows, prevs, nexts, chans):
        ua, ug = rows
        wa, wg, ba, bg = chans
        a = _conv_fwd(ua, prevs[0], wa) + ba
        g = _conv_fwd(ug, prevs[1], wg) + bg
        return [a * _sigmoid_tanh(a) * g, a, g], []

    z, a_act, g_act = _chan_call(
        "ffn_gate", gate, m, tp, tc, ncol, row_ins=[(u, 0), (u, ncol)], prev_ins=[(u, 0), (u, ncol)],
        chan_ins=[(p["conv_w"], 0), (p["conv_w"], ncol), (p["conv_b"], 0), (p["conv_b"], ncol)],
        row_outs=[(BF16,), (BF16,), (BF16,)])
    out = _matmul(z, p["w_down"], "nn", F32, residual=x, name="ffn_down")
    return out, (x, h, u, z, a_act, g_act)


def _ffn_bwd(dout, p, saved, m, tp):
    x, h, u, z, a_act, g_act = saved
    tc = FFN_COL_TILE
    ncol = D_FF // tc
    dz = _matmul(dout, p["w_down"], "nt", F32, name="ffn_down_dx")
    d_w_down = _matmul(z, dout, "tn", F32, name="ffn_down_dw")

    def act_bwd(a, g, dzv):
        sg = _sigmoid_tanh(a)
        return dzv * g * (sg * (1.0 + a * (1.0 - sg))), dzv * a * sg

    def gate_bwd(rows, prevs, nexts, chans):
        ua, ug, dzv, a, g = rows
        da, dg = act_bwd(a, g, dzv)
        da_next, dg_next = act_bwd(nexts[1], nexts[2], nexts[0])
        ups_a = [_shift_up(da, da_next, 2 - k) for k in range(3)]
        ups_g = [_shift_up(dg, dg_next, 2 - k) for k in range(3)]
        return ([_conv_taps(ups_a, chans[0]), _conv_taps(ups_g, chans[1])],
                [_conv_dw_taps(ua, ups_a), _conv_dw_taps(ug, ups_g),
                 jnp.sum(da, axis=0, keepdims=True), jnp.sum(dg, axis=0, keepdims=True)])

    dua, dug, dcw_a, dcw_g, dcb_a, dcb_g = _chan_call(
        "ffn_gate_bwd", gate_bwd, m, tp, tc, ncol,
        row_ins=[(u, 0), (u, ncol), (dz, 0), (a_act, 0), (g_act, 0)], next_ins=[(dz, 0), (a_act, 0), (g_act, 0)],
        chan_ins=[(p["conv_w"], 0), (p["conv_w"], ncol)],
        row_outs=[(BF16,), (BF16,)], red_outs=[(SUBLANES,), (SUBLANES,), (1,), (1,)], row_split=2)
    d_w_up = jnp.concatenate([_matmul(h, dua, "tn", F32, name="ffn_up_dw_a"),
                              _matmul(h, dug, "tn", F32, name="ffn_up_dw_g")], axis=1)
    dh = _matmul(dua, p["w_up"], "nt", F32, name="ffn_up_dx_a")
    dh = _matmul(dug, p["w_up"], "nt", F32, residual=dh, name="ffn_up_dx_g", b_col_off=D_FF)
    dx, d_norm = _rms_bwd(x, p["norm"], dh, dout, "ffn_norm_bwd")
    d_conv_w = jnp.concatenate([dcw_a[:3], dcw_g[:3]], axis=1)
    d_conv_b = jnp.concatenate([dcb_a, dcb_g], axis=1)
    return dx, dict(norm=d_norm, w_up=d_w_up, conv_w=d_conv_w, conv_b=d_conv_b, w_down=d_w_down)


def _to_scan(x, nb, tp):
    return x.reshape(nb, tp, LRU_WIDTH // LANES, LANES).transpose(1, 0, 2, 3).reshape(tp, -1, LANES)


def _from_scan(x, nb, tp):
    return x.reshape(tp, nb, LRU_WIDTH // LANES, LANES).transpose(1, 0, 2, 3).reshape(nb * tp, LRU_WIDTH)


def _scan_fwd(a, u):
    t_len, s, _ = a.shape
    tc = _pick(t_len, 640)
    blk = pl.BlockSpec((tc, s, LANES), lambda i: (i, 0, 0))

    def body(a_ref, u_ref, h_ref, carry):
        @pl.when(pl.program_id(0) == 0)
        def _():
            carry[...] = jnp.zeros_like(carry)

        def step(t, h):
            h = a_ref[t] * h + u_ref[t]
            h_ref[t] = h
            return h

        carry[...] = lax.fori_loop(0, tc, step, carry[...], unroll=8)

    return pl.pallas_call(
        body, out_shape=jax.ShapeDtypeStruct(a.shape, F32), grid=(t_len // tc,), in_specs=[blk, blk], out_specs=blk,
        scratch_shapes=[pltpu.VMEM((s, LANES), F32)], compiler_params=_cparams(1), name="lru_scan")(a, u)


def _scan_bwd(dh, a, h_prev):
    t_len, s, _ = a.shape
    tc = _pick(t_len, 640)
    nb = t_len // tc
    blk = pl.BlockSpec((tc, s, LANES), lambda i: (nb - 1 - i, 0, 0))

    def body(dh_ref, a_ref, hp_ref, du_ref, da_ref, carry):
        @pl.when(pl.program_id(0) == 0)
        def _():
            carry[...] = jnp.zeros_like(carry)

        def step(k, c):
            t = tc - 1 - k
            d = dh_ref[t] + c
            du_ref[t] = d
            da_ref[t] = d * hp_ref[t]
            return a_ref[t] * d

        carry[...] = lax.fori_loop(0, tc, step, carry[...], unroll=8)

    shp = jax.ShapeDtypeStruct(a.shape, F32)
    return pl.pallas_call(
        body, out_shape=(shp, shp), grid=(nb,), in_specs=[blk, blk, blk], out_specs=(blk, blk),
        scratch_shapes=[pltpu.VMEM((s, LANES), F32)], compiler_params=_cparams(1), name="lru_scan_bwd")(dh, a, h_prev)


def _lru_gates(xc, zr, zi, r_b, i_b, lam):
    r = _sigmoid(zr + r_b)
    ig = _sigmoid(zi + i_b)
    sp = _softplus_neg(lam)
    log_a = -LRU_C * r * sp
    a = jnp.exp(log_a)
    mult = jnp.sqrt(-_expm1(2.0 * log_a))
    return r, ig, sp, a, mult


def _even_fwd(x, p, m, tp, nb):
    c = LRU_WIDTH
    h = _rms_fwd(x, p["norm"], "ev_norm")
    u = _matmul(h, p["w_in"], "nn", F32, name="ev_in")

    def pre(rows, prevs, nexts, chans):
        gb, gc, xa, xb = rows
        wa, wb, bias = chans
        pa = gc * xa
        ya = gb * _conv_fwd(pa, prevs[0] * prevs[1], wa)
        xc = _conv_fwd(xb, prevs[2], wb) + bias
        return [ya, xc], []

    ya, xc = _chan_call("ev_pre", pre, m, tp, c, 1, row_ins=[(u, 0), (u, 1), (u, 2), (u, 3)],
                        prev_ins=[(u, 1), (u, 2), (u, 3)],
                        chan_ins=[(p["conv_a"], 0), (p["conv_b"], 0), (p["conv_b_bias"], 0)],
                        row_outs=[(BF16,), (F32,)])
    zr = _matmul(xc, p["gate_r"], "nn", F32, name="ev_gate_r")
    zi = _matmul(xc, p["gate_i"], "nn", F32, name="ev_gate_i")

    def lru_in(rows, prevs, nexts, chans):
        xcv, zrv, ziv = rows
        r, ig, sp, a, mult = _lru_gates(xcv, zrv, ziv, *chans)
        return [a, mult * (ig * xcv)], []

    a, uu = _chan_call("ev_lru_in", lru_in, m, tp, c, 1, row_ins=[(xc, 0), (zr, 0), (zi, 0)],
                       chan_ins=[(p["gate_r_b"], 0), (p["gate_i_b"], 0), (p["lam"], 0)],
                       row_outs=[(F32,), (F32,)])
    a_s = _to_scan(a, nb, tp)
    hs_s = _scan_fwd(a_s, _to_scan(uu, nb, tp))
    hs = _from_scan(hs_s, nb, tp)

    def post(rows, prevs, nexts, chans):
        gate, hv = rows
        return [_gelu(gate) * hv], []

    (yb,) = _chan_call("ev_post", post, m, tp, c, 1, row_ins=[(u, 4), (hs, 0)], row_outs=[(BF16,)])
    out = _matmul(ya, p["w_out_a"], "nn", F32, residual=x, name="ev_out_a")
    out = _matmul(yb, p["w_out_b"], "nn", F32, residual=out, name="ev_out_b")
    return out, (x, h, u, ya, xc, zr, zi, a_s, hs_s, hs, yb)


def _even_bwd(dout, p, saved, m, tp, nb):
    c = LRU_WIDTH
    x, h, u, ya, xc, zr, zi, a_s, hs_s, hs, yb = saved
    dy = _matmul(dout, p["w_out"], "nt", F32, name="ev_out_dx")
    d_w_out = jnp.concatenate([_matmul(ya, dout, "tn", F32, name="ev_out_dw_a"),
                               _matmul(yb, dout, "tn", F32, name="ev_out_dw_b")], axis=0)

    def post_bwd(rows, prevs, nexts, chans):
        dyb, gate, hv = rows
        return [dyb * hv * _gelu_grad(gate), dyb * _gelu(gate)], []

    dgate, dhs = _chan_call("ev_post_bwd", post_bwd, m, tp, c, 1, row_ins=[(dy, 1), (u, 4), (hs, 0)],
                            row_outs=[(F32,), (F32,)])
    h_prev = jnp.concatenate([jnp.zeros_like(hs_s[:1]), hs_s[:-1]], axis=0)
    du_s, da_s = _scan_bwd(_to_scan(dhs, nb, tp), a_s, h_prev)
    du = _from_scan(du_s, nb, tp)
    da = _from_scan(da_s, nb, tp)

    def lru_in_bwd(rows, prevs, nexts, chans):
        duv, dav, xcv, zrv, ziv = rows
        r, ig, sp, a, mult = _lru_gates(xcv, zrv, ziv, *chans)
        dxc = duv * mult * ig
        dig = duv * mult * xcv
        dmult = duv * ig * xcv
        dlog_a = dav * a - dmult * (a * a) / jnp.maximum(mult, 1e-30)
        dr = dlog_a * (-LRU_C * sp)
        dzr = dr * r * (1.0 - r)
        dzi = dig * ig * (1.0 - ig)
        dsp = jnp.sum(dlog_a * (-LRU_C * r), axis=0, keepdims=True)
        dlam = -dsp * _sigmoid(-chans[2])
        return ([dzr, dzi, dxc],
                [jnp.sum(dzr, axis=0, keepdims=True), jnp.sum(dzi, axis=0, keepdims=True), dlam])

    dzr, dzi, dxc, d_r_b, d_i_b, d_lam = _chan_call(
        "ev_lru_in_bwd", lru_in_bwd, m, tp, c, 1, row_ins=[(du, 0), (da, 0), (xc, 0), (zr, 0), (zi, 0)],
        chan_ins=[(p["gate_r_b"], 0), (p["gate_i_b"], 0), (p["lam"], 0)],
        row_outs=[(F32,), (F32,), (F32,)], red_outs=[(1,), (1,), (1,)])
    d_gate_r = _matmul(xc, dzr, "tn", F32, name="ev_gate_r_dw")
    d_gate_i = _matmul(xc, dzi, "tn", F32, name="ev_gate_i_dw")
    dxc = _matmul(dzr, p["gate_r"], "nt", F32, residual=dxc, name="ev_gate_r_dx")
    dxc = _matmul(dzi, p["gate_i"], "nt", F32, residual=dxc, name="ev_gate_i_dx")

    def conv_b_bwd(rows, prevs, nexts, chans):
        dxcv, xb = rows
        return ([_conv_dx(dxcv, nexts[0], chans[0])],
                [_conv_dw(dxcv, xb, prevs[0], 4), jnp.sum(dxcv, axis=0, keepdims=True)])

    dxb, d_conv_b, d_bias = _chan_call(
        "ev_conv_b_bwd", conv_b_bwd, m, tp, c, 1, row_ins=[(dxc, 0), (u, 3)], prev_ins=[(u, 3)], next_ins=[(dxc, 0)],
        chan_ins=[(p["conv_b"], 0)], row_outs=[(F32,)], red_outs=[(SUBLANES,), (1,)])

    def mix_a_bwd(rows, prevs, nexts, chans):
        dya, gb, gc, xa = rows
        (wa,) = chans
        taps = _taps(gc * xa, prevs[0] * prevs[1], 3)
        ca = _conv_taps(taps, wa)
        dca = dya * gb
        dpa = _conv_dx(dca, nexts[0] * nexts[1], wa)
        return [dya * ca, dpa * xa, dpa * gc], [_conv_dw_taps(dca, taps)]

    dgb, dgc, dxa, d_conv_a = _chan_call(
        "ev_mix_a_bwd", mix_a_bwd, m, tp, c, 1, row_ins=[(dy, 0), (u, 0), (u, 1), (u, 2)],
        prev_ins=[(u, 1), (u, 2)], next_ins=[(dy, 0), (u, 0)], chan_ins=[(p["conv_a"], 0)],
        row_outs=[(F32,), (F32,), (F32,)], red_outs=[(SUBLANES,)])
    du_all = jnp.concatenate([dgb, dgc, dxa, dxb, dgate], axis=1)
    d_w_in = _matmul(h, du_all, "tn", F32, name="ev_in_dw")
    dh = _matmul(du_all, p["w_in"], "nt", F32, name="ev_in_dx")
    dx, d_norm = _rms_bwd(x, p["norm"], dh, dout, "ev_norm_bwd")
    return dx, dict(norm=d_norm, w_in=d_w_in, conv_a=d_conv_a[:3], conv_b=d_conv_b[:4], conv_b_bias=d_bias,
                    gate_r=d_gate_r, gate_r_b=d_r_b, gate_i=d_gate_i, gate_i_b=d_i_b, lam=d_lam, w_out=d_w_out)


def _rope_tables(tp):
    pos = jnp.arange(tp, dtype=F32)
    inv_freq = ROPE_BASE ** (-jnp.arange(0, QK_ROPE, 2, dtype=F32) / QK_ROPE)
    ang = pos[:, None] * inv_freq[None, :]
    cos, sin = jnp.cos(ang), jnp.sin(ang)
    half = QK_ROPE // 2
    one = jnp.ones((tp, QK_NOPE), F32)
    z64 = jnp.zeros((tp, QK_NOPE), F32)
    zh = jnp.zeros((tp, half), F32)
    zt = jnp.zeros((tp, HEAD_PAD - QK_HEAD), F32)
    c_tab = jnp.concatenate([one, cos, cos, zt], axis=1)
    s_lo = jnp.concatenate([z64, -sin, zh, zt], axis=1)
    s_hi = jnp.concatenate([z64, zh, sin, zt], axis=1)
    return c_tab, s_lo, s_hi


def _rope(v, c_tab, s_lo, s_hi):
    half = QK_ROPE // 2
    return v * c_tab + pltpu.roll(v, HEAD_PAD - half, 1) * s_lo + pltpu.roll(v, half, 1) * s_hi


def _rope_t(dv, c_tab, s_lo, s_hi):
    half = QK_ROPE // 2
    return dv * c_tab + pltpu.roll(dv * s_lo, half, 1) + pltpu.roll(dv * s_hi, HEAD_PAD - half, 1)


def _rope_call(name, fn, m, tp, ins, tables, out_dtype, shared_pre=None):
    tm = _pick(tp, ROW_TILE)
    tps = tp // tm
    n = len(ins)
    width = MLA_HEADS * HEAD_PAD

    def body(*refs):
        tabs = [r[...] for r in refs[n:n + 3]]
        shared = [None if fc is None else shared_pre(refs[a][...].astype(F32), *tabs) for a, (_, fc) in enumerate(ins)]
        for hh in range(MLA_HEADS):
            lanes = slice(hh * HEAD_PAD, (hh + 1) * HEAD_PAD)
            vals = [refs[a][:, lanes].astype(F32) if shared[a] is None else shared[a] for a in range(n)]
            refs[n + 3][:, lanes] = fn(*vals, *tabs).astype(out_dtype)

    in_specs, args = [], []
    for arr, fixed_col in ins:
        if fixed_col is None:
            in_specs.append(pl.BlockSpec((tm, width), lambda i: (i, 0)))
        else:
            in_specs.append(pl.BlockSpec((tm, HEAD_PAD), lambda i, fc=fixed_col: (i, fc)))
        args.append(arr)
    for tab in tables:
        in_specs.append(pl.BlockSpec((tm, HEAD_PAD), lambda i: (lax.rem(i, tps), 0)))
        args.append(tab)
    return pl.pallas_call(
        body, out_shape=jax.ShapeDtypeStruct((m, width), out_dtype), grid=(m // tm,),
        in_specs=in_specs, out_specs=pl.BlockSpec((tm, width), lambda i: (i, 0)),
        compiler_params=_cparams(1), name=name)(*args)


def _rope_k_bwd(dk, tables, m, tp):
    tm = _pick(tp, ROW_TILE)
    tps = tp // tm

    def body(dk_ref, c_ref, lo_ref, hi_ref, o_ref):
        acc = dk_ref[:, 0:HEAD_PAD].astype(F32)
        for hh in range(1, MLA_HEADS):
            acc = acc + dk_ref[:, hh * HEAD_PAD:(hh + 1) * HEAD_PAD].astype(F32)
        d = pltpu.roll(_rope_t(acc, c_ref[...], lo_ref[...], hi_ref[...]), QK_NOPE, 1)
        lane = lax.broadcasted_iota(jnp.int32, d.shape, 1)
        o_ref[...] = jnp.where(lane < QK_ROPE, d, 0.0)

    tab = pl.BlockSpec((tm, HEAD_PAD), lambda i: (lax.rem(i, tps), 0))
    return pl.pallas_call(
        body, out_shape=jax.ShapeDtypeStruct((m, HEAD_PAD), F32), grid=(m // tm,),
        in_specs=[pl.BlockSpec((tm, MLA_HEADS * HEAD_PAD), lambda i: (i, 0)), tab, tab, tab],
        out_specs=pl.BlockSpec((tm, HEAD_PAD), lambda i: (i, 0)), compiler_params=_cparams(1),
        name="od_rope_k_bwd")(dk, *tables)


def _causal_mask(row0, col0, shape):
    rows = row0 + lax.broadcasted_iota(jnp.int32, shape, 0)
    cols = col0 + lax.broadcasted_iota(jnp.int32, shape, 1)
    return cols <= rows


NT = (((1,), (1,)), ((), ()))
TN = (((0,), (0,)), ((), ()))
HEADS_PER_STEP = 2
HEAD_STEPS = MLA_HEADS // HEADS_PER_STEP
STEP_LANES = HEADS_PER_STEP * HEAD_PAD


def _flash_fwd(q, k, v, nb, tp):
    tq = _pick(tp, ROW_TILE)
    nq = tp // tq

    def body(q_ref, k_ref, v_ref, o_ref, lse_ref):
        i = pl.program_id(2)
        qbs = [q_ref[:, hd * HEAD_PAD:(hd + 1) * HEAD_PAD] for hd in range(HEADS_PER_STEP)]

        def chunk(j, carry, masked, width=1):
            off = pl.multiple_of(j * tq, tq)
            out = []
            for hd in range(HEADS_PER_STEP):
                mx, l, acc = carry[hd]
                lanes = slice(hd * HEAD_PAD, (hd + 1) * HEAD_PAD)
                kb = k_ref[pl.ds(off, width * tq), lanes]
                vb = v_ref[pl.ds(off, width * tq), lanes]
                s = lax.dot_general(qbs[hd], kb, NT, preferred_element_type=F32)
                if masked:
                    s = jnp.where(_causal_mask(0, 0, s.shape), s, NEG)
                m_new = jnp.maximum(mx, jnp.max(s, axis=1, keepdims=True))
                alpha = jnp.exp(mx - m_new)
                pr = jnp.exp(s - m_new)
                l = alpha * l + jnp.sum(pr, axis=1, keepdims=True)
                acc = alpha * acc + jnp.dot(pr.astype(BF16), vb, preferred_element_type=F32)
                out.append((m_new, l, acc))
            return tuple(out)

        one = (jnp.full((tq, 1), NEG, F32), jnp.zeros((tq, 1), F32), jnp.zeros((tq, HEAD_PAD), F32))
        quads = i // 4
        carry = lax.fori_loop(0, quads, lambda jj, c: chunk(4 * jj, c, False, 4), (one,) * HEADS_PER_STEP)
        carry = lax.fori_loop(0, lax.rem(i, 4) // 2, lambda _, c: chunk(4 * quads, c, False, 2), carry)
        carry = lax.fori_loop(0, lax.rem(i, 2), lambda _, c: chunk(i - 1, c, False), carry)
        carry = chunk(i, carry, True)
        for hd in range(HEADS_PER_STEP):
            mx, l, acc = carry[hd]
            lanes = slice(hd * HEAD_PAD, (hd + 1) * HEAD_PAD)
            o_ref[:, lanes] = (acc / l).astype(o_ref.dtype)
            lse_ref[:, lanes] = jnp.broadcast_to(mx + jnp.log(l), (tq, HEAD_PAD))

    qspec = pl.BlockSpec((tq, STEP_LANES), lambda b, hh, i: (b * nq + i, hh))
    kvspec = pl.BlockSpec((tp, STEP_LANES), lambda b, hh, i: (b, hh))
    shp = (nb * tp, MLA_HEADS * HEAD_PAD)
    return pl.pallas_call(
        body, out_shape=(jax.ShapeDtypeStruct(shp, BF16), jax.ShapeDtypeStruct(shp, F32)),
        grid=(nb, HEAD_STEPS, nq), in_specs=[qspec, kvspec, kvspec], out_specs=(qspec, qspec),
        compiler_params=_cparams(3), name="od_flash_fwd")(q, k, v)


def _flash_prep(o, do, lse_c, nb, tp):
    tq = _pick(tp, ROW_TILE)
    nq = tp // tq

    def body(o_ref, do_ref, lse_ref, lr_ref, dr_ref):
        for hh in range(MLA_HEADS):
            lanes = slice(hh * HEAD_PAD, (hh + 1) * HEAD_PAD)
            delta = jnp.sum(o_ref[:, lanes].astype(F32) * do_ref[:, lanes].astype(F32), axis=1, keepdims=True)
            lr_ref[hh] = jnp.transpose(lse_ref[:, lanes])[0:SUBLANES, :]
            dr_ref[hh] = jnp.transpose(jnp.broadcast_to(delta, (tq, HEAD_PAD)))[0:SUBLANES, :]

    qspec = pl.BlockSpec((tq, MLA_HEADS * HEAD_PAD), lambda b, i: (b * nq + i, 0))
    rspec = pl.BlockSpec((MLA_HEADS, None, SUBLANES, tq), lambda b, i: (b, i, 0, 0))
    rshape = jax.ShapeDtypeStruct((nb * MLA_HEADS, nq, SUBLANES, tq), F32)
    return pl.pallas_call(
        body, out_shape=(rshape, rshape), grid=(nb, nq), in_specs=[qspec, qspec, qspec],
        out_specs=(rspec, rspec), compiler_params=_cparams(2), name="od_flash_prep")(o, do, lse_c)


def _flash_bwd(q, k, v, do, lse_r, delta_r, nb, tp):
    tq = _pick(tp, ROW_TILE)
    nq = tp // tq

    def body(q_ref, k_ref, v_ref, do_ref, lse_ref, dl_ref, dq_ref, dk_ref, dv_ref):
        j = pl.program_id(2)

        @pl.when(j == 0)
        def _():
            dq_ref[...] = jnp.zeros_like(dq_ref)

        kbs = [k_ref[:, hd * HEAD_PAD:(hd + 1) * HEAD_PAD] for hd in range(HEADS_PER_STEP)]
        vbs = [v_ref[:, hd * HEAD_PAD:(hd + 1) * HEAD_PAD] for hd in range(HEADS_PER_STEP)]

        def chunk(i, carry, masked, width=1):
            off = pl.multiple_of(i * tq, tq)
            out = []
            for hd in range(HEADS_PER_STEP):
                dk, dv = carry[hd]
                lanes = slice(hd * HEAD_PAD, (hd + 1) * HEAD_PAD)
                qb = q_ref[pl.ds(off, width * tq), lanes]
                dob = do_ref[pl.ds(off, width * tq), lanes]
                lse = jnp.concatenate([lse_ref[hd, i + w][0:1, :] for w in range(width)], axis=1)
                delta = jnp.concatenate([dl_ref[hd, i + w][0:1, :] for w in range(width)], axis=1)
                st = lax.dot_general(kbs[hd], qb, NT, preferred_element_type=F32)
                pt = jnp.exp(st - lse)
                if masked:
                    keys = lax.broadcasted_iota(jnp.int32, st.shape, 0)
                    queries = lax.broadcasted_iota(jnp.int32, st.shape, 1)
                    pt = jnp.where(keys <= queries, pt, 0.0)
                dv = dv + jnp.dot(pt.astype(BF16), dob, preferred_element_type=F32)
                dpt = lax.dot_general(vbs[hd], dob, NT, preferred_element_type=F32)
                dst = (pt * (dpt - delta)).astype(BF16)
                dk = dk + jnp.dot(dst, qb, preferred_element_type=F32)
                dq_ref[pl.ds(off, width * tq), lanes] += lax.dot_general(dst, kbs[hd], TN,
                                                                         preferred_element_type=F32)
                out.append((dk, dv))
            return tuple(out)

        zero = jnp.zeros((tq, HEAD_PAD), F32)
        wide = jnp.minimum(nq - 1 - j, 1)
        rest = nq - 1 - j - wide
        carry = ((zero, zero),) * HEADS_PER_STEP
        carry = lax.fori_loop(0, wide, lambda _, c: chunk(j, c, True, 2), carry)
        carry = lax.fori_loop(0, 1 - wide, lambda _, c: chunk(j, c, True), carry)
        carry = lax.fori_loop(0, rest // 2, lambda pp, c: chunk(j + 1 + wide + 2 * pp, c, False, 2), carry)
        carry = lax.fori_loop(0, lax.rem(rest, 2), lambda _, c: chunk(nq - 1, c, False), carry)
        for hd in range(HEADS_PER_STEP):
            lanes = slice(hd * HEAD_PAD, (hd + 1) * HEAD_PAD)
            dk_ref[:, lanes] = carry[hd][0]
            dv_ref[:, lanes] = carry[hd][1].astype(dv_ref.dtype)

    tspec = pl.BlockSpec((tq, STEP_LANES), lambda b, hh, j: (b * nq + j, hh))
    fullspec = pl.BlockSpec((tp, STEP_LANES), lambda b, hh, j: (b, hh))
    rspec = pl.BlockSpec((HEADS_PER_STEP, nq, SUBLANES, tq), lambda b, hh, j: (b * HEAD_STEPS + hh, 0, 0, 0))
    shp = (nb * tp, MLA_HEADS * HEAD_PAD)
    return pl.pallas_call(
        body, out_shape=(jax.ShapeDtypeStruct(shp, F32), jax.ShapeDtypeStruct(shp, F32),
                         jax.ShapeDtypeStruct(shp, BF16)),
        grid=(nb, HEAD_STEPS, nq), in_specs=[fullspec, tspec, tspec, fullspec, rspec, rspec],
        out_specs=(fullspec, tspec, tspec), compiler_params=_cparams(3),
        name="od_flash_bwd")(q, k, v, do, lse_r, delta_r)


def _odd_fwd(x, p, tables, m, tp, nb):
    scale = QK_HEAD ** -0.5
    h = _rms_fwd(x, p["norm"], "od_norm")
    u = _matmul(h, p["w_in"], "nn", F32, name="od_in")
    cq = u[:, :Q_LORA]
    ckv = u[:, Q_LORA:Q_LORA + KV_LORA]
    cqn = _rms_fwd(cq, p["q_norm"], "od_q_norm")
    ckvn = _rms_fwd(ckv, p["kv_norm"], "od_kv_norm")
    q_raw = _matmul(cqn, p["w_uq"], "nn", F32, name="od_uq")
    k_raw = _matmul(ckvn, p["w_uk"], "nn", F32, name="od_uk")
    v = _matmul(ckvn, p["w_uv"], "nn", BF16, name="od_uv")
    q = _rope_call("od_rope_q", lambda qv, c, lo, hi: _rope(qv, c, lo, hi) * scale, m, tp, [(q_raw, None)], tables,
                   BF16)
    kr_col = (Q_LORA + KV_LORA) // HEAD_PAD
    k = _rope_call("od_rope_k", lambda kv, kr, c, lo, hi: kv + kr, m, tp, [(k_raw, None), (u, kr_col)], tables, BF16,
                   shared_pre=lambda uv, c, lo, hi: _rope(pltpu.roll(uv, QK_NOPE, 1), c, lo, hi))
    o, lse_c = _flash_fwd(q, k, v, nb, tp)
    out = _matmul(o, p["w_out"], "nn", F32, residual=x, name="od_out")
    return out, (x, h, cq, ckv, cqn, ckvn, q, k, v, o, lse_c)


def _odd_bwd(dout, p, tables, saved, m, tp, nb):
    scale = QK_HEAD ** -0.5
    x, h, cq, ckv, cqn, ckvn, q, k, v, o, lse_c = saved
    do = _matmul(dout, p["w_out"], "nt", BF16, name="od_out_dx")
    d_w_out = _matmul(o, dout, "tn", F32, name="od_out_dw")
    lse_r, delta_r = _flash_prep(o, do, lse_c, nb, tp)
    dq, dk, dv = _flash_bwd(q, k, v, do, lse_r, delta_r, nb, tp)
    dq_raw = _rope_call("od_rope_q_bwd", lambda d, c, lo, hi: _rope_t(d, c, lo, hi) * scale, m, tp, [(dq, None)],
                        tables, BF16)
    dkr = _rope_k_bwd(dk, tables, m, tp)
    d_w_uq = _matmul(cqn, dq_raw, "tn", F32, name="od_uq_dw")
    d_w_uk = _matmul(ckvn, dk, "tn", F32, name="od_uk_dw")
    d_w_uv = _matmul(ckvn, dv, "tn", F32, name="od_uv_dw")
    dcqn = _matmul(dq_raw, p["w_uq"], "nt", F32, name="od_uq_dx")
    dckvn = _matmul(dk, p["w_uk"], "nt", F32, name="od_uk_dx")
    dckvn = _matmul(dv, p["w_uv"], "nt", F32, residual=dckvn, name="od_uv_dx")
    dcq, d_q_norm = _rms_bwd(cq, p["q_norm"], dcqn, None, "od_q_norm_bwd")
    dckv, d_kv_norm = _rms_bwd(ckv, p["kv_norm"], dckvn, None, "od_kv_norm_bwd")
    du = jnp.concatenate([dcq, dckv, dkr], axis=1)
    d_w_in = _matmul(h, du, "tn", F32, name="od_in_dw")
    dh = _matmul(du, p["w_in"], "nt", F32, name="od_in_dx")
    dx, d_norm = _rms_bwd(x, p["norm"], dh, dout, "od_norm_bwd")
    return dx, dict(norm=d_norm, w_in=d_w_in, q_norm=d_q_norm, kv_norm=d_kv_norm, w_uq=d_w_uq, w_uk=d_w_uk,
                    w_uv=d_w_uv, w_out=d_w_out)


def _loss_head(hf, g, target, tp, t_real):
    m, c = hf.shape
    tm = _pick(tp, ROW_TILE)
    tps = tp // tm

    def body(x_ref, g_ref, t_ref, dx_ref, dg_ref, loss_ref):
        i = pl.program_id(0)
        xf = x_ref[...]
        r = lax.rsqrt(jnp.mean(xf * xf, axis=-1, keepdims=True) + EPS)
        xn = xf * r
        t_pos = lax.rem(i, tps) * tm + lax.broadcasted_iota(jnp.int32, (tm, 1), 0)
        valid = jnp.logical_and(t_pos >= N_META, t_pos < t_real)
        err = jnp.where(valid, xn * g_ref[...] - t_ref[...], 0.0)
        dyf = err * (1.0 / c)
        dyg = dyf * g_ref[...]
        dx_ref[...] = r * (dyg - xn * jnp.mean(dyg * xn, axis=-1, keepdims=True))

        @pl.when(i == 0)
        def _():
            dg_ref[...] = jnp.zeros_like(dg_ref)
            loss_ref[...] = jnp.zeros_like(loss_ref)

        dg_ref[...] += jnp.sum(dyf * xn, axis=0, keepdims=True)
        loss_ref[...] += (0.5 / c) * jnp.sum(jnp.sum(err * err, axis=1, keepdims=True), axis=0, keepdims=True)

    row = pl.BlockSpec((tm, c), lambda i: (i, 0))
    vec = pl.BlockSpec((1, c), lambda i: (0, 0))
    return pl.pallas_call(
        body, out_shape=(jax.ShapeDtypeStruct((m, c), F32), jax.ShapeDtypeStruct((1, c), F32),
                         jax.ShapeDtypeStruct((1, 1), F32)),
        grid=(m // tm,), in_specs=[row, vec, row], out_specs=(row, vec, pl.BlockSpec((1, 1), lambda i: (0, 0))),
        compiler_params=_cparams(1), name="loss_head")(hf, g, target)


def _meta_grad(dh0, nb, tp):
    d = dh0.shape[1]

    def body(x_ref, o_ref):
        @pl.when(pl.program_id(0) == 0)
        def _():
            o_ref[...] = jnp.zeros_like(o_ref)

        o_ref[...] += x_ref[...]

    return pl.pallas_call(
        body, out_shape=jax.ShapeDtypeStruct((N_META, d), F32), grid=(nb,),
        in_specs=[pl.BlockSpec((N_META, d), lambda b: (b * (tp // N_META), 0))],
        out_specs=pl.BlockSpec((N_META, d), lambda b: (0, 0)), compiler_params=_cparams(1), name="meta_grad")(dh0)


def _mesh_pos():
    x, y, c = lax.axis_index("x"), lax.axis_index("y"), lax.axis_index("c")
    return x, y, c


N_CHIP = 4
MESH_ID = pl.DeviceIdType.MESH


def _peer_chip(x, y, k):
    px = 1 - x if k & 2 else x
    py = 1 - y if k & 1 else y
    return px, py


def _all_gather(arrays):
    n = len(arrays)

    def body(*refs):
        srcs, outs = refs[:n], refs[n:2 * n]
        send_sems, recv_sems, local_sems = refs[2 * n:]
        x, y, c = _mesh_pos()
        me = 4 * x + 2 * y + c
        sibling = (x, y, 1 - c)

        def copy(a, sem, src, block, to):
            return pltpu.make_async_remote_copy(
                src_ref=src, dst_ref=outs[a].at[block], send_sem=send_sems.at[a, sem], recv_sem=recv_sems.at[a, sem],
                device_id=to, device_id_type=MESH_ID)

        local = [pltpu.make_async_copy(srcs[a], outs[a].at[me], local_sems.at[a]) for a in range(n)]
        for cp in local:
            cp.start()
        sends = [copy(a, 0, srcs[a], me, sibling) for a in range(n)]
        for k in range(1, N_CHIP):
            px, py = _peer_chip(x, y, k)
            sends += [copy(a, k, srcs[a], me, (px, py, c)) for a in range(n)]
        for cp in sends:
            cp.start()
        for k in range(1, N_CHIP):
            px, py = _peer_chip(x, y, k)
            block = 4 * px + 2 * py + c
            for a in range(n):
                copy(a, k, srcs[a], block, sibling).wait_recv()
            passed = [copy(a, N_CHIP - 1 + k, outs[a].at[block], block, sibling) for a in range(n)]
            for cp in passed:
                cp.start()
            sends += passed
        for a in range(n):
            copy(a, 0, srcs[a], 4 * x + 2 * y + (1 - c), sibling).wait_recv()
        for k in range(1, N_CHIP):
            px, py = _peer_chip(x, y, k)
            for a in range(n):
                copy(a, N_CHIP - 1 + k, srcs[a], 4 * px + 2 * py + (1 - c), sibling).wait_recv()
        for cp in sends:
            cp.wait_send()
        for cp in local:
            cp.wait()

    any_spec = pl.BlockSpec(memory_space=pl.ANY)
    out_shape = tuple(jax.ShapeDtypeStruct((N_DEV,) + a.shape, a.dtype) for a in arrays)
    return pl.pallas_call(
        body, out_shape=out_shape, in_specs=[any_spec] * n, out_specs=(any_spec,) * n,
        scratch_shapes=[pltpu.SemaphoreType.DMA((n, N_DEV - 1)), pltpu.SemaphoreType.DMA((n, N_DEV - 1)),
                        pltpu.SemaphoreType.DMA((n,))],
        name="weight_all_gather")(*arrays)


def _pair_exchange(arrays):
    n = len(arrays)

    def body(*refs):
        srcs, outs = refs[:n], refs[n:2 * n]
        send_sems, recv_sems = refs[2 * n:]
        x, y, c = _mesh_pos()
        copies = [pltpu.make_async_remote_copy(
            src_ref=srcs[a], dst_ref=outs[a], send_sem=send_sems.at[a], recv_sem=recv_sems.at[a],
            device_id=(x, y, 1 - c), device_id_type=MESH_ID) for a in range(n)]
        for cp in copies:
            cp.start()
        for cp in copies:
            cp.wait()

    any_spec = pl.BlockSpec(memory_space=pl.ANY)
    return pl.pallas_call(
        body, out_shape=tuple(jax.ShapeDtypeStruct(a.shape, a.dtype) for a in arrays), in_specs=[any_spec] * n,
        out_specs=(any_spec,) * n, scratch_shapes=[pltpu.SemaphoreType.DMA((n,)), pltpu.SemaphoreType.DMA((n,))],
        name="grad_pair_exchange")(*arrays)


def _chip_exchange(arrays):
    n = len(arrays)

    def body(*refs):
        srcs, outs = refs[:n], refs[n:2 * n]
        send_sems, recv_sems, local_sems = refs[2 * n:]
        x, y, c = _mesh_pos()
        q = 2 * x + y
        local = [pltpu.make_async_copy(srcs[a].at[q], outs[a].at[q], local_sems.at[a]) for a in range(n)]
        for cp in local:
            cp.start()

        def copy(a, k, to_q, from_q, px, py):
            return pltpu.make_async_remote_copy(
                src_ref=srcs[a].at[to_q], dst_ref=outs[a].at[from_q], send_sem=send_sems.at[a, k - 1],
                recv_sem=recv_sems.at[a, k - 1], device_id=(px, py, c), device_id_type=MESH_ID)

        sends = []
        for k in range(1, N_CHIP):
            px, py = _peer_chip(x, y, k)
            sends += [copy(a, k, 2 * px + py, q, px, py) for a in range(n)]
        for cp in sends:
            cp.start()
        for k in range(1, N_CHIP):
            px, py = _peer_chip(x, y, k)
            for a in range(n):
                copy(a, k, q, 2 * px + py, px, py).wait_recv()
        for cp in sends:
            cp.wait_send()
        for cp in local:
            cp.wait()

    any_spec = pl.BlockSpec(memory_space=pl.ANY)
    return pl.pallas_call(
        body, out_shape=tuple(jax.ShapeDtypeStruct(a.shape, a.dtype) for a in arrays), in_specs=[any_spec] * n,
        out_specs=(any_spec,) * n,
        scratch_shapes=[pltpu.SemaphoreType.DMA((n, N_CHIP - 1)), pltpu.SemaphoreType.DMA((n, N_CHIP - 1)),
                        pltpu.SemaphoreType.DMA((n,))],
        name="grad_chip_exchange")(*arrays)


REDUCE_BLOCK_BYTES = 512 * 1024


def _pair_add(a, b):
    p, r, c = a.shape
    tr = _reduce_rows(r, c)

    def body(a_ref, b_ref, o_ref):
        o_ref[...] = (a_ref[...].astype(F32) + b_ref[...].astype(F32)).astype(o_ref.dtype)

    blk = pl.BlockSpec((None, tr, c), lambda s, i: (s, i, 0))
    return pl.pallas_call(
        body, out_shape=jax.ShapeDtypeStruct(a.shape, a.dtype), grid=(p, r // tr), in_specs=[blk, blk], out_specs=blk,
        compiler_params=_cparams(2), name="grad_pair_add")(a, b)


def _reduce_rows(r, c):
    best = None
    for t in range(16, r + 1, 16):
        if r % t == 0 and t * c * 4 <= REDUCE_BLOCK_BYTES:
            best = t
    assert best is not None, (r, c)
    return best


def _reduce_adamw(parts, w, mom, vel):
    n_parts, r, c = parts.shape
    tr = _reduce_rows(r, c)
    c1 = 1.0 - ADAM_B1 ** ADAM_STEP
    c2 = 1.0 - ADAM_B2 ** ADAM_STEP

    def body(p_ref, w_ref, m_ref, v_ref, g_ref, d_ref, mo_ref, vo_ref):
        g = p_ref[0].astype(F32)
        for s in range(1, n_parts):
            g = g + p_ref[s].astype(F32)
        mn = ADAM_B1 * m_ref[...] + (1.0 - ADAM_B1) * g
        vn = ADAM_B2 * v_ref[...] + (1.0 - ADAM_B2) * (g * g)
        m_hat = mn / c1
        v_hat = vn / c2
        g_ref[...] = g
        d_ref[...] = -ADAM_LR * (m_hat / (jnp.sqrt(v_hat) + ADAM_EPS) + ADAM_WD * w_ref[...])
        mo_ref[...] = mn
        vo_ref[...] = vn

    blk = pl.BlockSpec((tr, c), lambda i: (i, 0))
    shp = jax.ShapeDtypeStruct((r, c), F32)
    return pl.pallas_call(
        body, out_shape=(shp, shp, shp, shp), grid=(r // tr,),
        in_specs=[pl.BlockSpec((n_parts, tr, c), lambda i: (0, i, 0)), blk, blk, blk], out_specs=(blk, blk, blk, blk),
        compiler_params=_cparams(1), name="reduce_adamw")(parts, w, mom, vel)


def _pack_rows(pieces, width, row_multiple, dtype):
    flat = jnp.concatenate([p.astype(dtype).reshape(-1) for p in pieces])
    rows = -(-flat.shape[0] // (width * row_multiple)) * row_multiple
    return jnp.pad(flat, (0, rows * width - flat.shape[0])).reshape(rows, width)


def _unshard(gathered, axis):
    moved = jnp.moveaxis(gathered, 0, axis)
    shape = list(moved.shape)
    shape[axis:axis + 2] = [shape[axis] * shape[axis + 1]]
    return moved.reshape(shape)


def _to_slots(full, axis):
    shape = list(full.shape)
    shape[axis:axis + 1] = [N_DEV, shape[axis] // N_DEV]
    return jnp.moveaxis(full.reshape(shape), axis, 0)


def _core_slots(full, axis, core):
    shape = list(full.shape)
    shape[axis:axis + 1] = [N_CHIP, 2, shape[axis] // N_DEV]
    picked = lax.dynamic_index_in_dim(full.reshape(shape), core, axis + 1, keepdims=False)
    return jnp.moveaxis(picked, axis, 0)


def _block_diag(w):
    hh, d, _ = w.shape
    eye = jnp.eye(hh, dtype=w.dtype)
    return (w[:, :, None, :] * eye[:, None, :, None]).reshape(hh * d, hh * d)


def _block_diag_t(full, hh):
    d = full.shape[0] // hh
    f4 = full.reshape(hh, d, hh, d)
    return jnp.stack([f4[i, :, i, :] for i in range(hh)], axis=0)


def _pad_heads(w, width):
    r = w.shape[0]
    w3 = w.reshape(r, MLA_HEADS, width)
    return jnp.pad(w3, ((0, 0), (0, 0), (0, HEAD_PAD - width))).reshape(r, MLA_HEADS * HEAD_PAD)


def _unpad_heads(w, width):
    r = w.shape[0]
    return w.reshape(r, MLA_HEADS, HEAD_PAD)[:, :, :width].reshape(r, MLA_HEADS * width)


def kernel(x, meta_tokens, ev_norm, ev_w_in, ev_conv_a, ev_conv_b, ev_conv_b_bias, ev_gate_r_w, ev_gate_r_b, ev_gate_i_w, ev_gate_i_b, ev_lru_lambda, ev_w_out, od_norm, od_w_in, od_q_norm, od_kv_norm, od_w_uq, od_w_ukv, od_w_out, ffn_norm, ffn_w_up, ffn_conv_w, ffn_conv_b, ffn_w_down, final_norm, loss_target, m_meta_tokens, m_ev_norm, m_ev_w_in, m_ev_conv_a, m_ev_conv_b, m_ev_conv_b_bias, m_ev_gate_r_w, m_ev_gate_r_b, m_ev_gate_i_w, m_ev_gate_i_b, m_ev_lru_lambda, m_ev_w_out, m_od_norm, m_od_w_in, m_od_q_norm, m_od_kv_norm, m_od_w_uq, m_od_w_ukv, m_od_w_out, m_ffn_norm, m_ffn_w_up, m_ffn_conv_w, m_ffn_conv_b, m_ffn_w_down, m_final_norm, v_meta_tokens, v_ev_norm, v_ev_w_in, v_ev_conv_a, v_ev_conv_b, v_ev_conv_b_bias, v_ev_gate_r_w, v_ev_gate_r_b, v_ev_gate_i_w, v_ev_gate_i_b, v_ev_lru_lambda, v_ev_w_out, v_od_norm, v_od_w_in, v_od_q_norm, v_od_kv_norm, v_od_w_uq, v_od_w_ukv, v_od_w_out, v_ffn_norm, v_ffn_w_up, v_ffn_conv_w, v_ffn_conv_b, v_ffn_w_down, v_final_norm):
    given = dict(locals())
    names = [n for n, _ in PARAMS]
    axis_of = dict(PARAMS)
    w_loc = {n: given[n] for n in names}
    m_loc = {n: given["m_" + n] for n in names}
    v_loc = {n: given["v_" + n] for n in names}
    sharded = [n for n in names if axis_of[n] is not None]
    replicated = [n for n in names if axis_of[n] is None]
    small = [n for n in sharded if n not in BIG]

    nb, seq, d = x.shape
    t_real = N_META + seq
    tp = -(-t_real // ROW_TILE) * ROW_TILE
    m = nb * tp

    small_pack = _pack_rows([w_loc[n] for n in small], LANES, SUBLANES, F32)
    gathered = _all_gather([w_loc[n].astype(BF16) for n in BIG] + [small_pack])
    full = {n: w_loc[n] for n in replicated}
    for n, g in zip(BIG, gathered[:-1]):
        full[n] = _unshard(g, axis_of[n])
    flat = gathered[-1].reshape(N_DEV, -1)
    off = 0
    for n in small:
        shard = w_loc[n].shape
        size = math.prod(shard)
        full[n] = _unshard(flat[:, off:off + size].reshape((N_DEV,) + shard), axis_of[n])
        off += size

    tables = _rope_tables(tp)

    def even_params(j):
        w_out = full["ev_w_out"][j]
        return dict(norm=full["ev_norm"][j][None], w_in=full["ev_w_in"][j], conv_a=full["ev_conv_a"][j],
                    conv_b=full["ev_conv_b"][j], conv_b_bias=full["ev_conv_b_bias"][j][None],
                    gate_r=_block_diag(full["ev_gate_r_w"][j]).astype(BF16),
                    gate_i=_block_diag(full["ev_gate_i_w"][j]).astype(BF16),
                    gate_r_b=full["ev_gate_r_b"][j][None], gate_i_b=full["ev_gate_i_b"][j][None],
                    lam=full["ev_lru_lambda"][j][None], w_out=w_out, w_out_a=w_out[:LRU_WIDTH],
                    w_out_b=w_out[LRU_WIDTH:])

    def odd_params(j):
        w_ukv = full["od_w_ukv"][j].reshape(KV_LORA, MLA_HEADS, QK_NOPE + V_HEAD)
        w_uk = w_ukv[:, :, :QK_NOPE].reshape(KV_LORA, MLA_HEADS * QK_NOPE)
        w_uv = w_ukv[:, :, QK_NOPE:].reshape(KV_LORA, MLA_HEADS * V_HEAD)
        w_out = full["od_w_out"][j].reshape(MLA_HEADS, V_HEAD, d)
        w_out = jnp.pad(w_out, ((0, 0), (0, HEAD_PAD - V_HEAD), (0, 0))).reshape(MLA_HEADS * HEAD_PAD, d)
        return dict(norm=full["od_norm"][j][None], w_in=jnp.pad(full["od_w_in"][j], ((0, 0), (0, ODD_IN_PAD - ODD_IN))),
                    q_norm=full["od_q_norm"][j][None], kv_norm=full["od_kv_norm"][j][None],
                    w_uq=_pad_heads(full["od_w_uq"][j], QK_HEAD), w_uk=_pad_heads(w_uk, QK_NOPE),
                    w_uv=_pad_heads(w_uv, V_HEAD), w_out=w_out)

    def ffn_params(layer):
        w_up = full["ffn_w_up"][layer]
        return dict(norm=full["ffn_norm"][layer][None], w_up=w_up, conv_w=full["ffn_conv_w"][layer],
                    conv_b=full["ffn_conv_b"][layer][None], w_down=full["ffn_w_down"][layer])

    meta = jnp.broadcast_to(full["meta_tokens"][None], (nb, N_META, d))
    h0 = jnp.concatenate([meta, x, jnp.zeros((nb, tp - t_real, d), F32)], axis=1).reshape(m, d)
    hcur = h0
    tape = []
    for layer in range(4):
        j = layer // 2
        if layer % 2 == 0:
            mp = even_params(j)
            hcur, saved = _even_fwd(hcur, mp, m, tp, nb)
        else:
            mp = odd_params(j)
            hcur, saved = _odd_fwd(hcur, mp, tables, m, tp, nb)
        fp = ffn_params(layer)
        hcur, fsaved = _ffn_fwd(hcur, fp, m, tp)
        tape.append((mp, saved, fp, fsaved))

    target = jnp.pad(loss_target, ((0, 0), (N_META, tp - t_real), (0, 0))).reshape(m, d)
    dh, d_final_norm, loss_part = _loss_head(hcur, full["final_norm"][None], target, tp, t_real)

    grads = {"final_norm": d_final_norm[0]}
    ev_g, od_g, ffn_g = [None, None], [None, None], [None] * 4
    for layer in reversed(range(4)):
        mp, saved, fp, fsaved = tape[layer]
        dh, ffn_g[layer] = _ffn_bwd(dh, fp, fsaved, m, tp)
        if layer % 2 == 0:
            dh, ev_g[layer // 2] = _even_bwd(dh, mp, saved, m, tp, nb)
        else:
            dh, od_g[layer // 2] = _odd_bwd(dh, mp, tables, saved, m, tp, nb)

    dh3 = dh.reshape(nb, tp, d)
    grad_x = dh3[:, N_META:t_real]
    grads["meta_tokens"] = _meta_grad(dh, nb, tp)

    def stack(lst, key, fn=lambda a: a):
        return jnp.stack([fn(g[key]) for g in lst], axis=0)

    grads["ev_norm"] = stack(ev_g, "norm", lambda a: a[0])
    grads["ev_w_in"] = stack(ev_g, "w_in")
    grads["ev_conv_a"] = stack(ev_g, "conv_a")
    grads["ev_conv_b"] = stack(ev_g, "conv_b")
    grads["ev_conv_b_bias"] = stack(ev_g, "conv_b_bias", lambda a: a[0])
    grads["ev_gate_r_w"] = stack(ev_g, "gate_r", lambda a: _block_diag_t(a, 8))
    grads["ev_gate_r_b"] = stack(ev_g, "gate_r_b", lambda a: a[0])
    grads["ev_gate_i_w"] = stack(ev_g, "gate_i", lambda a: _block_diag_t(a, 8))
    grads["ev_gate_i_b"] = stack(ev_g, "gate_i_b", lambda a: a[0])
    grads["ev_lru_lambda"] = stack(ev_g, "lam", lambda a: a[0])
    grads["ev_w_out"] = stack(ev_g, "w_out")
    grads["od_norm"] = stack(od_g, "norm", lambda a: a[0])
    grads["od_w_in"] = stack(od_g, "w_in", lambda a: a[:, :ODD_IN])
    grads["od_q_norm"] = stack(od_g, "q_norm", lambda a: a[0])
    grads["od_kv_norm"] = stack(od_g, "kv_norm", lambda a: a[0])
    grads["od_w_uq"] = stack(od_g, "w_uq", lambda a: _unpad_heads(a, QK_HEAD))

    def ukv(g):
        gk = g["w_uk"].reshape(KV_LORA, MLA_HEADS, HEAD_PAD)[:, :, :QK_NOPE]
        gv = g["w_uv"].reshape(KV_LORA, MLA_HEADS, HEAD_PAD)[:, :, :V_HEAD]
        return jnp.concatenate([gk, gv], axis=2).reshape(KV_LORA, MLA_HEADS * (QK_NOPE + V_HEAD))

    grads["od_w_ukv"] = jnp.stack([ukv(g) for g in od_g], axis=0)
    grads["od_w_out"] = stack(od_g, "w_out", lambda a: a.reshape(MLA_HEADS, HEAD_PAD, d)[:, :V_HEAD].reshape(-1, d))
    grads["ffn_norm"] = stack(ffn_g, "norm", lambda a: a[0])
    grads["ffn_w_up"] = stack(ffn_g, "w_up")
    grads["ffn_conv_w"] = stack(ffn_g, "conv_w")
    grads["ffn_conv_b"] = stack(ffn_g, "conv_b", lambda a: a[0])
    grads["ffn_w_down"] = stack(ffn_g, "w_down")

    order = small + replicated
    slot_parts = [_to_slots(grads[n], axis_of[n]).reshape(N_DEV, -1) for n in small]
    slot_parts += [jnp.broadcast_to(grads[n].reshape(1, -1), (N_DEV, grads[n].size)) for n in replicated]
    slot_parts.append(jnp.broadcast_to(loss_part, (N_DEV, 1)))
    g_flat = jnp.concatenate(slot_parts, axis=1)
    n_flat = g_flat.shape[1]
    rows = -(-n_flat // (1024 * 128)) * 128
    g_small = jnp.pad(g_flat, ((0, 0), (0, rows * 1024 - n_flat))).reshape(N_DEV, rows, 1024)

    def rows_of(n):
        shard = w_loc[n].shape
        return (math.prod(shard[:-1]), shard[-1])

    core = lax.axis_index("c")

    def core_slots(n, which):
        return _core_slots(grads[n], axis_of[n], which).astype(BF16).reshape((N_CHIP,) + rows_of(n))

    small_by_core = jnp.swapaxes(g_small.reshape((N_CHIP, 2) + g_small.shape[1:]), 0, 1)
    mine = [core_slots(n, core) for n in BIG] + [lax.dynamic_index_in_dim(small_by_core, core, 0, keepdims=False)]
    theirs = [core_slots(n, 1 - core) for n in BIG]
    theirs.append(lax.dynamic_index_in_dim(small_by_core, 1 - core, 0, keepdims=False))
    from_sibling = _pair_exchange(theirs)
    parts = _chip_exchange([_pair_add(a, b) for a, b in zip(mine, from_sibling)])

    g_out, d_out, m_out, v_out = {}, {}, {}, {}
    for n, part in zip(BIG, parts[:-1]):
        res = _reduce_adamw(part, *[t[n].reshape(rows_of(n)) for t in (w_loc, m_loc, v_loc)])
        for out, r in zip((g_out, d_out, m_out, v_out), res):
            out[n] = r.reshape(w_loc[n].shape)

    def flat_local(tree):
        flat = jnp.concatenate([tree[n].reshape(-1) for n in order])
        return jnp.pad(flat, (0, rows * 1024 - flat.shape[0])).reshape(rows, 1024)

    res = _reduce_adamw(parts[-1], flat_local(w_loc), flat_local(m_loc), flat_local(v_loc))
    loss = res[0].reshape(-1)[n_flat - 1]
    for out, r in zip((g_out, d_out, m_out, v_out), res):
        flat = r.reshape(-1)
        off = 0
        for n in order:
            size = w_loc[n].size
            out[n] = flat[off:off + size].reshape(w_loc[n].shape)
            off += size
    return (loss, grad_x, *[g_out[n] for n in names], *[d_out[n] for n in names], *[m_out[n] for n in names],
            *[v_out[n] for n in names])
```

```python
import math

import jax
import jax.numpy as jnp
from jax import lax
from jax.experimental import pallas as pl
from jax.experimental.pallas import tpu as pltpu

F32 = jnp.float32
BF16 = jnp.bfloat16

N_DEV = 8
N_META = 16
EPS = 1e-6
LRU_C = 8.0
MLA_HEADS = 16
QK_NOPE = 64
QK_ROPE = 32
QK_HEAD = QK_NOPE + QK_ROPE
V_HEAD = 64
HEAD_PAD = 128
Q_LORA = 384
KV_LORA = 256
ODD_IN = Q_LORA + KV_LORA + QK_ROPE
ODD_IN_PAD = 768
ROPE_BASE = 10000.0
LRU_WIDTH = 512
D_FF = 2816

ADAM_LR = 0.001
ADAM_B1 = 0.9
ADAM_B2 = 0.999
ADAM_EPS = 1e-08
ADAM_WD = 0.01
ADAM_STEP = 10

ROW_TILE = 384
SUBLANES = 8
HALO_ROWS = 16
LANES = 128
VMEM_LIMIT = 48 * 1024 * 1024
NEG = -1e30

PARAMS = (
    ("meta_tokens", 1), ("ev_norm", None), ("ev_w_in", 2), ("ev_conv_a", 2), ("ev_conv_b", 2),
    ("ev_conv_b_bias", None), ("ev_gate_r_w", None), ("ev_gate_r_b", None), ("ev_gate_i_w", None),
    ("ev_gate_i_b", None), ("ev_lru_lambda", None), ("ev_w_out", 1), ("od_norm", 1), ("od_w_in", 1),
    ("od_q_norm", 1), ("od_kv_norm", 1), ("od_w_uq", 2), ("od_w_ukv", 2), ("od_w_out", 1),
    ("ffn_norm", None), ("ffn_w_up", 2), ("ffn_conv_w", 2), ("ffn_conv_b", None), ("ffn_w_down", 1),
    ("final_norm", None),
)
BIG = ("ev_w_in", "ev_w_out", "od_w_in", "od_w_uq", "od_w_ukv", "od_w_out", "ffn_w_up", "ffn_w_down")


def _cparams(n_grid):
    return pltpu.CompilerParams(dimension_semantics=("arbitrary",) * n_grid, vmem_limit_bytes=VMEM_LIMIT)


def _pick(dim, target):
    if dim <= target:
        return dim
    best = None
    for t in range(LANES, target + 1, LANES):
        if dim % t == 0:
            best = t
    assert best is not None, (dim, target)
    return best


MATMUL_VMEM_BUDGET = 38 * 1024 * 1024
HBM_BYTES_PER_US = 3.0e6
MXU_FLOPS_PER_US = 9.0e8
ACC_BYTES_PER_US = 7.6e6
GRID_STEP_US = 0.35


def _tile_candidates(dim):
    return [t for t in range(LANES, dim + 1, LANES) if dim % t == 0] or [dim]


def _matmul_tiles(m, n, k, sa, sb, so, sr, transposed_lhs):
    best, best_cost = None, None
    for tm in _tile_candidates(m):
        for tn in _tile_candidates(n):
            for tk in _tile_candidates(k):
                nk = k // tk
                vmem = 2 * (tm * tk * sa + tk * tn * sb) + tm * tn * ((4 if nk > 1 else 0) + 2 * so + 2 * sr)
                vmem += (tm * tk * 2 if sa > 2 else 0) + (tk * tn * 2 if sb > 2 else 0) + tm * tn * 4
                if vmem > MATMUL_VMEM_BUDGET:
                    continue
                steps = (m // tm) * (n // tn) * nk
                traffic = m * k * sa * (n // tn) + k * n * sb * (m // tm) + m * n * (so + sr)
                acc_us = steps * tm * tn * 4 / ACC_BYTES_PER_US if nk > 1 else 0.0
                busy_us = 0.0 if transposed_lhs else 2.0 * m * n * k / MXU_FLOPS_PER_US + acc_us
                cost = max(traffic / HBM_BYTES_PER_US, busy_us) + steps * GRID_STEP_US
                if best_cost is None or cost < best_cost:
                    best, best_cost = (tm, tn, tk), cost
    assert best is not None, (m, n, k)
    return best


def _matmul(a, b, mode, out_dtype=F32, residual=None, name="mm", b_col_off=0):
    if mode == "nn":
        (m, k), (k2, n) = a.shape, b.shape
    elif mode == "nt":
        (m, k), n = a.shape, b.shape[0]
        k2 = k if b_col_off or b.shape[1] > k else b.shape[1]
    else:
        (k, m), (k2, n) = a.shape, b.shape
    assert k == k2, (a.shape, b.shape, mode)
    tm, tn, tk = _matmul_tiles(m, n, k, a.dtype.itemsize, b.dtype.itemsize, jnp.dtype(out_dtype).itemsize,
                               0 if residual is None else residual.dtype.itemsize, mode == "tn")
    nk = k // tk
    if mode == "tn":
        a_spec = pl.BlockSpec((tk, tm), lambda i, j, kk: (kk, i))
        dims = (((0,), (0,)), ((), ()))
    else:
        a_spec = pl.BlockSpec((tm, tk), lambda i, j, kk: (i, kk))
        dims = (((1,), (1 if mode == "nt" else 0,)), ((), ()))
    if mode == "nt":
        assert b_col_off % tk == 0, (b_col_off, tk)
        b_spec = pl.BlockSpec((tn, tk), lambda i, j, kk: (j, kk + b_col_off // tk))
    else:
        b_spec = pl.BlockSpec((tk, tn), lambda i, j, kk: (kk, j))
    o_spec = pl.BlockSpec((tm, tn), lambda i, j, kk: (i, j))
    has_res = residual is not None

    def body(*refs):
        a_ref, b_ref = refs[:2]
        r_ref = refs[2] if has_res else None
        o_ref = refs[3] if has_res else refs[2]
        part = lax.dot_general(a_ref[...].astype(BF16), b_ref[...].astype(BF16), dims, preferred_element_type=F32)

        def finish(out):
            if has_res:
                out = out + r_ref[...].astype(F32)
            o_ref[...] = out.astype(o_ref.dtype)

        if nk == 1:
            finish(part)
            return
        acc_ref = refs[-1]
        kk = pl.program_id(2)

        @pl.when(kk == 0)
        def _():
            acc_ref[...] = part

        @pl.when(kk > 0)
        def _():
            acc_ref[...] += part

        @pl.when(kk == nk - 1)
        def _():
            finish(acc_ref[...])

    in_specs = [a_spec, b_spec] + ([o_spec] if has_res else [])
    args = (a, b) + ((residual,) if has_res else ())
    return pl.pallas_call(
        body, out_shape=jax.ShapeDtypeStruct((m, n), out_dtype), grid=(m // tm, n // tn, nk),
        in_specs=in_specs, out_specs=o_spec, scratch_shapes=[pltpu.VMEM((tm, tn), F32)] if nk > 1 else [],
        compiler_params=_cparams(3), name=name)(*args)


def _rms_fwd(x, g, name):
    m, c = x.shape
    tm = _pick(m, ROW_TILE)

    def body(x_ref, g_ref, o_ref):
        xf = x_ref[...].astype(F32)
        r = lax.rsqrt(jnp.mean(xf * xf, axis=-1, keepdims=True) + EPS)
        o_ref[...] = (xf * r * g_ref[...]).astype(o_ref.dtype)

    return pl.pallas_call(
        body, out_shape=jax.ShapeDtypeStruct((m, c), BF16), grid=(m // tm,),
        in_specs=[pl.BlockSpec((tm, c), lambda i: (i, 0)), pl.BlockSpec((1, c), lambda i: (0, 0))],
        out_specs=pl.BlockSpec((tm, c), lambda i: (i, 0)), compiler_params=_cparams(1), name=name)(x, g)


def _rms_bwd(x, g, dy, residual, name):
    m, c = x.shape
    tm = _pick(m, ROW_TILE)
    has_res = residual is not None

    def body(*refs):
        if has_res:
            x_ref, g_ref, dy_ref, r_ref, dx_ref, dg_ref = refs
        else:
            x_ref, g_ref, dy_ref, dx_ref, dg_ref = refs
        xf = x_ref[...].astype(F32)
        dyf = dy_ref[...].astype(F32)
        r = lax.rsqrt(jnp.mean(xf * xf, axis=-1, keepdims=True) + EPS)
        xn = xf * r
        dyg = dyf * g_ref[...]
        dx = r * (dyg - xn * jnp.mean(dyg * xn, axis=-1, keepdims=True))
        if has_res:
            dx = dx + r_ref[...]
        dx_ref[...] = dx

        @pl.when(pl.program_id(0) == 0)
        def _():
            dg_ref[...] = jnp.zeros_like(dg_ref)

        dg_ref[...] += jnp.sum(dyf * xn, axis=0, keepdims=True)

    row = pl.BlockSpec((tm, c), lambda i: (i, 0))
    vec = pl.BlockSpec((1, c), lambda i: (0, 0))
    in_specs = [row, vec, row] + ([row] if has_res else [])
    args = (x, g, dy) + ((residual,) if has_res else ())
    return pl.pallas_call(
        body, out_shape=(jax.ShapeDtypeStruct((m, c), F32), jax.ShapeDtypeStruct((1, c), F32)), grid=(m // tm,),
        in_specs=in_specs, out_specs=(row, vec), compiler_params=_cparams(1), name=name)(*args)


def _chan_call(name, fn, m, tp, tc, ncol, row_ins=(), prev_ins=(), next_ins=(), chan_ins=(), row_outs=(),
               red_outs=(), row_split=1):
    tm = _pick(tp, ROW_TILE) // row_split
    tps = tp // tm
    nrow = m // tm
    halo_blocks = tm // HALO_ROWS
    last_halo = m // HALO_ROWS - 1
    n_in = len(row_ins) + len(prev_ins) + len(next_ins) + len(chan_ins)
    n_r, n_p, n_n = len(row_ins), len(prev_ins), len(next_ins)

    def body(*refs):
        i = pl.program_id(1)
        pos = lax.rem(i, tps)
        at_start = pos == 0
        at_end = pos == tps - 1
        rows = [r[...].astype(F32) for r in refs[:n_r]]
        prevs = [jnp.where(at_start, 0.0, r[...].astype(F32)[SUBLANES:]) for r in refs[n_r:n_r + n_p]]
        nexts = [jnp.where(at_end, 0.0, r[...].astype(F32)[:SUBLANES]) for r in refs[n_r + n_p:n_r + n_p + n_n]]
        chans = [r[...] for r in refs[n_r + n_p + n_n:n_in]]
        out_refs = refs[n_in:n_in + len(row_outs)]
        red_refs = refs[n_in + len(row_outs):]
        row_vals, red_vals = fn(rows, prevs, nexts, chans)
        for ref, val in zip(out_refs, row_vals):
            ref[...] = val.astype(ref.dtype)
        if red_refs:
            @pl.when(i == 0)
            def _():
                for ref in red_refs:
                    ref[...] = jnp.zeros_like(ref)

            for ref, val in zip(red_refs, red_vals):
                ref[...] += val

    in_specs, args = [], []
    for arr, off in row_ins:
        in_specs.append(pl.BlockSpec((tm, tc), lambda j, i, off=off: (i, j + off)))
        args.append(arr)
    for arr, off in prev_ins:
        in_specs.append(pl.BlockSpec((HALO_ROWS, tc),
                                     lambda j, i, off=off: (jnp.maximum(i * halo_blocks - 1, 0), j + off)))
        args.append(arr)
    for arr, off in next_ins:
        in_specs.append(pl.BlockSpec((HALO_ROWS, tc),
                                     lambda j, i, off=off: (jnp.minimum((i + 1) * halo_blocks, last_halo), j + off)))
        args.append(arr)
    for arr, off in chan_ins:
        in_specs.append(pl.BlockSpec((arr.shape[0], tc), lambda j, i, off=off: (0, j + off)))
        args.append(arr)
    out_shape, out_specs = [], []
    for (dt,) in row_outs:
        out_shape.append(jax.ShapeDtypeStruct((m, ncol * tc), dt))
        out_specs.append(pl.BlockSpec((tm, tc), lambda j, i: (i, j)))
    for (k,) in red_outs:
        out_shape.append(jax.ShapeDtypeStruct((k, ncol * tc), F32))
        out_specs.append(pl.BlockSpec((k, tc), lambda j, i: (0, j)))
    return pl.pallas_call(
        body, out_shape=tuple(out_shape), grid=(ncol, nrow), in_specs=in_specs, out_specs=tuple(out_specs),
        compiler_params=_cparams(2), name=name)(*args)


def _shift_down(x, prev8, s):
    if s == 0:
        return x
    tm, tc = x.shape
    groups = tm // SUBLANES
    xr = pltpu.roll(x.reshape(groups, SUBLANES, tc), s, 1)
    before = jnp.concatenate([pltpu.roll(prev8, s, 0)[None], xr[:-1]], axis=0)
    rid = lax.broadcasted_iota(jnp.int32, xr.shape, 1)
    return jnp.where(rid < s, before, xr).reshape(tm, tc)


def _shift_up(x, next8, s):
    if s == 0:
        return x
    tm, tc = x.shape
    groups = tm // SUBLANES
    xr = pltpu.roll(x.reshape(groups, SUBLANES, tc), SUBLANES - s, 1)
    after = jnp.concatenate([xr[1:], pltpu.roll(next8, SUBLANES - s, 0)[None]], axis=0)
    rid = lax.broadcasted_iota(jnp.int32, xr.shape, 1)
    return jnp.where(rid >= SUBLANES - s, after, xr).reshape(tm, tc)


def _taps(x, prev8, kw):
    return [_shift_down(x, prev8, kw - 1 - k) for k in range(kw)]


def _conv_taps(taps, w):
    y = w[0:1, :] * taps[0]
    for k in range(1, len(taps)):
        y = y + w[k:k + 1, :] * taps[k]
    return y


def _conv_dw_taps(dy, taps):
    shape = (SUBLANES, dy.shape[1])
    rid = lax.broadcasted_iota(jnp.int32, shape, 0)
    out = jnp.zeros(shape, F32)
    for k, tap in enumerate(taps):
        out = out + jnp.where(rid == k, jnp.sum(dy * tap, axis=0, keepdims=True), 0.0)
    return out


def _conv_fwd(x, prev8, w):
    return _conv_taps(_taps(x, prev8, w.shape[0]), w)


def _conv_dw(dy, x, prev8, kw):
    return _conv_dw_taps(dy, _taps(x, prev8, kw))


def _conv_dx(dy, next8, w):
    kw = w.shape[0]
    dx = w[kw - 1:kw, :] * dy
    for k in range(kw - 1):
        dx = dx + w[k:k + 1, :] * _shift_up(dy, next8, kw - 1 - k)
    return dx


def _sigmoid(x):
    return 1.0 / (1.0 + jnp.exp(-x))


def _sigmoid_tanh(x):
    return 0.5 + 0.5 * jnp.tanh(0.5 * x)


def _expm1(x):
    series = x * (1.0 + x * 0.5 * (1.0 + x * (1.0 / 3.0) * (1.0 + x * 0.25 * (1.0 + x * 0.2))))
    return jnp.where(jnp.abs(x) < 0.3, series, jnp.exp(x) - 1.0)


def _softplus_neg(lam):
    e = jnp.exp(-jnp.abs(lam))
    log1p = jnp.where(e < 1e-2, e * (1.0 - e * (0.5 - e * (1.0 / 3.0))), jnp.log(1.0 + e))
    return jnp.maximum(-lam, 0.0) + log1p


GELU_C = math.sqrt(2.0 / math.pi)


def _gelu(x):
    return 0.5 * x * (1.0 + jnp.tanh(GELU_C * (x + 0.044715 * x * x * x)))


def _gelu_grad(x):
    t = jnp.tanh(GELU_C * (x + 0.044715 * x * x * x))
    return 0.5 * (1.0 + t) + 0.5 * x * (1.0 - t * t) * GELU_C * (1.0 + 3.0 * 0.044715 * x * x)


FFN_COL_TILE = 1408


def _ffn_fwd(x, p, m, tp):
    h = _rms_fwd(x, p["norm"], "ffn_norm")
    u = _matmul(h, p["w_up"], "nn", BF16, name="ffn_up")
    tc = FFN_COL_TILE
    ncol = D_FF // tc

    def gate(rows, prevs, nexts, chans):
        ua, ug = rows
        wa, wg, ba, bg = chans
        a = _conv_fwd(ua, prevs[0], wa) + ba
        g = _conv_fwd(ug, prevs[1], wg) + bg
        return [a * _sigmoid_tanh(a) * g, a, g], []

    z, a_act, g_act = _chan_call(
        "ffn_gate", gate, m, tp, tc, ncol, row_ins=[(u, 0), (u, ncol)], prev_ins=[(u, 0), (u, ncol)],
        chan_ins=[(p["conv_w"], 0), (p["conv_w"], ncol), (p["conv_b"], 0), (p["conv_b"], ncol)],
        row_outs=[(BF16,), (BF16,), (BF16,)])
    out = _matmul(z, p["w_down"], "nn", F32, residual=x, name="ffn_down")
    return out, (x, h, u, z, a_act, g_act)


def _ffn_bwd(dout, p, saved, m, tp):
    x, h, u, z, a_act, g_act = saved
    tc = FFN_COL_TILE
    ncol = D_FF // tc
    dz = _matmul(dout, p["w_down"], "nt", F32, name="ffn_down_dx")
    d_w_down = _matmul(z, dout, "tn", F32, name="ffn_down_dw")

    def act_bwd(a, g, dzv):
        sg = _sigmoid_tanh(a)
        return dzv * g * (sg * (1.0 + a * (1.0 - sg))), dzv * a * sg

    def gate_bwd(rows, prevs, nexts, chans):
        ua, ug, dzv, a, g = rows
        da, dg = act_bwd(a, g, dzv)
        da_next, dg_next = act_bwd(nexts[1], nexts[2], nexts[0])
        ups_a = [_shift_up(da, da_next, 2 - k) for k in range(3)]
        ups_g = [_shift_up(dg, dg_next, 2 - k) for k in range(3)]
        return ([_conv_taps(ups_a, chans[0]), _conv_taps(ups_g, chans[1])],
                [_conv_dw_taps(ua, ups_a), _conv_dw_taps(ug, ups_g),
                 jnp.sum(da, axis=0, keepdims=True), jnp.sum(dg, axis=0, keepdims=True)])

    dua, dug, dcw_a, dcw_g, dcb_a, dcb_g = _chan_call(
        "ffn_gate_bwd", gate_bwd, m, tp, tc, ncol,
        row_ins=[(u, 0), (u, ncol), (dz, 0), (a_act, 0), (g_act, 0)], next_ins=[(dz, 0), (a_act, 0), (g_act, 0)],
        chan_ins=[(p["conv_w"], 0), (p["conv_w"], ncol)],
        row_outs=[(BF16,), (BF16,)], red_outs=[(SUBLANES,), (SUBLANES,), (1,), (1,)], row_split=2)
    d_w_up = jnp.concatenate([_matmul(h, dua, "tn", F32, name="ffn_up_dw_a"),
                              _matmul(h, dug, "tn", F32, name="ffn_up_dw_g")], axis=1)
    dh = _matmul(dua, p["w_up"], "nt", F32, name="ffn_up_dx_a")
    dh = _matmul(dug, p["w_up"], "nt", F32, residual=dh, name="ffn_up_dx_g", b_col_off=D_FF)
    dx, d_norm = _rms_bwd(x, p["norm"], dh, dout, "ffn_norm_bwd")
    d_conv_w = jnp.concatenate([dcw_a[:3], dcw_g[:3]], axis=1)
    d_conv_b = jnp.concatenate([dcb_a, dcb_g], axis=1)
    return dx, dict(norm=d_norm, w_up=d_w_up, conv_w=d_conv_w, conv_b=d_conv_b, w_down=d_w_down)


def _to_scan(x, nb, tp):
    return x.reshape(nb, tp, LRU_WIDTH // LANES, LANES).transpose(1, 0, 2, 3).reshape(tp, -1, LANES)


def _from_scan(x, nb, tp):
    return x.reshape(tp, nb, LRU_WIDTH // LANES, LANES).transpose(1, 0, 2, 3).reshape(nb * tp, LRU_WIDTH)


def _scan_fwd(a, u):
    t_len, s, _ = a.shape
    tc = _pick(t_len, 640)
    blk = pl.BlockSpec((tc, s, LANES), lambda i: (i, 0, 0))

    def body(a_ref, u_ref, h_ref, carry):
        @pl.when(pl.program_id(0) == 0)
        def _():
            carry[...] = jnp.zeros_like(carry)

        def step(t, h):
            h = a_ref[t] * h + u_ref[t]
            h_ref[t] = h
            return h

        carry[...] = lax.fori_loop(0, tc, step, carry[...], unroll=8)

    return pl.pallas_call(
        body, out_shape=jax.ShapeDtypeStruct(a.shape, F32), grid=(t_len // tc,), in_specs=[blk, blk], out_specs=blk,
        scratch_shapes=[pltpu.VMEM((s, LANES), F32)], compiler_params=_cparams(1), name="lru_scan")(a, u)


def _scan_bwd(dh, a, h_prev):
    t_len, s, _ = a.shape
    tc = _pick(t_len, 640)
    nb = t_len // tc
    blk = pl.BlockSpec((tc, s, LANES), lambda i: (nb - 1 - i, 0, 0))

    def body(dh_ref, a_ref, hp_ref, du_ref, da_ref, carry):
        @pl.when(pl.program_id(0) == 0)
        def _():
            carry[...] = jnp.zeros_like(carry)

        def step(k, c):
            t = tc - 1 - k
            d = dh_ref[t] + c
            du_ref[t] = d
            da_ref[t] = d * hp_ref[t]
            return a_ref[t] * d

        carry[...] = lax.fori_loop(0, tc, step, carry[...], unroll=8)

    shp = jax.ShapeDtypeStruct(a.shape, F32)
    return pl.pallas_call(
        body, out_shape=(shp, shp), grid=(nb,), in_specs=[blk, blk, blk], out_specs=(blk, blk),
        scratch_shapes=[pltpu.VMEM((s, LANES), F32)], compiler_params=_cparams(1), name="lru_scan_bwd")(dh, a, h_prev)


def _lru_gates(xc, zr, zi, r_b, i_b, lam):
    r = _sigmoid(zr + r_b)
    ig = _sigmoid(zi + i_b)
    sp = _softplus_neg(lam)
    log_a = -LRU_C * r * sp
    a = jnp.exp(log_a)
    mult = jnp.sqrt(-_expm1(2.0 * log_a))
    return r, ig, sp, a, mult


def _even_fwd(x, p, m, tp, nb):
    c = LRU_WIDTH
    h = _rms_fwd(x, p["norm"], "ev_norm")
    u = _matmul(h, p["w_in"], "nn", F32, name="ev_in")

    def pre(rows, prevs, nexts, chans):
        gb, gc, xa, xb = rows
        wa, wb, bias = chans
        pa = gc * xa
        ya = gb * _conv_fwd(pa, prevs[0] * prevs[1], wa)
        xc = _conv_fwd(xb, prevs[2], wb) + bias
        return [ya, xc], []

    ya, xc = _chan_call("ev_pre", pre, m, tp, c, 1, row_ins=[(u, 0), (u, 1), (u, 2), (u, 3)],
                        prev_ins=[(u, 1), (u, 2), (u, 3)],
                        chan_ins=[(p["conv_a"], 0), (p["conv_b"], 0), (p["conv_b_bias"], 0)],
                        row_outs=[(BF16,), (F32,)])
    zr = _matmul(xc, p["gate_r"], "nn", F32, name="ev_gate_r")
    zi = _matmul(xc, p["gate_i"], "nn", F32, name="ev_gate_i")

    def lru_in(rows, prevs, nexts, chans):
        xcv, zrv, ziv = rows
        r, ig, sp, a, mult = _lru_gates(xcv, zrv, ziv, *chans)
        return [a, mult * (ig * xcv)], []

    a, uu = _chan_call("ev_lru_in", lru_in, m, tp, c, 1, row_ins=[(xc, 0), (zr, 0), (zi, 0)],
                       chan_ins=[(p["gate_r_b"], 0), (p["gate_i_b"], 0), (p["lam"], 0)],
                       row_outs=[(F32,), (F32,)])
    a_s = _to_scan(a, nb, tp)
    hs_s = _scan_fwd(a_s, _to_scan(uu, nb, tp))
    hs = _from_scan(hs_s, nb, tp)

    def post(rows, prevs, nexts, chans):
        gate, hv = rows
        return [_gelu(gate) * hv], []

    (yb,) = _chan_call("ev_post", post, m, tp, c, 1, row_ins=[(u, 4), (hs, 0)], row_outs=[(BF16,)])
    out = _matmul(ya, p["w_out_a"], "nn", F32, residual=x, name="ev_out_a")
    out = _matmul(yb, p["w_out_b"], "nn", F32, residual=out, name="ev_out_b")
    return out, (x, h, u, ya, xc, zr, zi, a_s, hs_s, hs, yb)


def _even_bwd(dout, p, saved, m, tp, nb):
    c = LRU_WIDTH
    x, h, u, ya, xc, zr, zi, a_s, hs_s, hs, yb = saved
    dy = _matmul(dout, p["w_out"], "nt", F32, name="ev_out_dx")
    d_w_out = jnp.concatenate([_matmul(ya, dout, "tn", F32, name="ev_out_dw_a"),
                               _matmul(yb, dout, "tn", F32, name="ev_out_dw_b")], axis=0)

    def post_bwd(rows, prevs, nexts, chans):
        dyb, gate, hv = rows
        return [dyb * hv * _gelu_grad(gate), dyb * _gelu(gate)], []

    dgate, dhs = _chan_call("ev_post_bwd", post_bwd, m, tp, c, 1, row_ins=[(dy, 1), (u, 4), (hs, 0)],
                            row_outs=[(F32,), (F32,)])
    h_prev = jnp.concatenate([jnp.zeros_like(hs_s[:1]), hs_s[:-1]], axis=0)
    du_s, da_s = _scan_bwd(_to_scan(dhs, nb, tp), a_s, h_prev)
    du = _from_scan(du_s, nb, tp)
    da = _from_scan(da_s, nb, tp)

    def lru_in_bwd(rows, prevs, nexts, chans):
        duv, dav, xcv, zrv, ziv = rows
        r, ig, sp, a, mult = _lru_gates(xcv, zrv, ziv, *chans)
        dxc = duv * mult * ig
        dig = duv * mult * xcv
        dmult = duv * ig * xcv
        dlog_a = dav * a - dmult * (a * a) / jnp.maximum(mult, 1e-30)
        dr = dlog_a * (-LRU_C * sp)
        dzr = dr * r * (1.0 - r)
        dzi = dig * ig * (1.0 - ig)
        dsp = jnp.sum(dlog_a * (-LRU_C * r), axis=0, keepdims=True)
        dlam = -dsp * _sigmoid(-chans[2])
        return ([dzr, dzi, dxc],
                [jnp.sum(dzr, axis=0, keepdims=True), jnp.sum(dzi, axis=0, keepdims=True), dlam])

    dzr, dzi, dxc, d_r_b, d_i_b, d_lam = _chan_call(
        "ev_lru_in_bwd", lru_in_bwd, m, tp, c, 1, row_ins=[(du, 0), (da, 0), (xc, 0), (zr, 0), (zi, 0)],
        chan_ins=[(p["gate_r_b"], 0), (p["gate_i_b"], 0), (p["lam"], 0)],
        row_outs=[(F32,), (F32,), (F32,)], red_outs=[(1,), (1,), (1,)])
    d_gate_r = _matmul(xc, dzr, "tn", F32, name="ev_gate_r_dw")
    d_gate_i = _matmul(xc, dzi, "tn", F32, name="ev_gate_i_dw")
    dxc = _matmul(dzr, p["gate_r"], "nt", F32, residual=dxc, name="ev_gate_r_dx")
    dxc = _matmul(dzi, p["gate_i"], "nt", F32, residual=dxc, name="ev_gate_i_dx")

    def conv_b_bwd(rows, prevs, nexts, chans):
        dxcv, xb = rows
        return ([_conv_dx(dxcv, nexts[0], chans[0])],
                [_conv_dw(dxcv, xb, prevs[0], 4), jnp.sum(dxcv, axis=0, keepdims=True)])

    dxb, d_conv_b, d_bias = _chan_call(
        "ev_conv_b_bwd", conv_b_bwd, m, tp, c, 1, row_ins=[(dxc, 0), (u, 3)], prev_ins=[(u, 3)], next_ins=[(dxc, 0)],
        chan_ins=[(p["conv_b"], 0)], row_outs=[(F32,)], red_outs=[(SUBLANES,), (1,)])

    def mix_a_bwd(rows, prevs, nexts, chans):
        dya, gb, gc, xa = rows
        (wa,) = chans
        taps = _taps(gc * xa, prevs[0] * prevs[1], 3)
        ca = _conv_taps(taps, wa)
        dca = dya * gb
        dpa = _conv_dx(dca, nexts[0] * nexts[1], wa)
        return [dya * ca, dpa * xa, dpa * gc], [_conv_dw_taps(dca, taps)]

    dgb, dgc, dxa, d_conv_a = _chan_call(
        "ev_mix_a_bwd", mix_a_bwd, m, tp, c, 1, row_ins=[(dy, 0), (u, 0), (u, 1), (u, 2)],
        prev_ins=[(u, 1), (u, 2)], next_ins=[(dy, 0), (u, 0)], chan_ins=[(p["conv_a"], 0)],
        row_outs=[(F32,), (F32,), (F32,)], red_outs=[(SUBLANES,)])
    du_all = jnp.concatenate([dgb, dgc, dxa, dxb, dgate], axis=1)
    d_w_in = _matmul(h, du_all, "tn", F32, name="ev_in_dw")
    dh = _matmul(du_all, p["w_in"], "nt", F32, name="ev_in_dx")
    dx, d_norm = _rms_bwd(x, p["norm"], dh, dout, "ev_norm_bwd")
    return dx, dict(norm=d_norm, w_in=d_w_in, conv_a=d_conv_a[:3], conv_b=d_conv_b[:4], conv_b_bias=d_bias,
                    gate_r=d_gate_r, gate_r_b=d_r_b, gate_i=d_gate_i, gate_i_b=d_i_b, lam=d_lam, w_out=d_w_out)


def _rope_tables(tp):
    pos = jnp.arange(tp, dtype=F32)
    inv_freq = ROPE_BASE ** (-jnp.arange(0, QK_ROPE, 2, dtype=F32) / QK_ROPE)
    ang = pos[:, None] * inv_freq[None, :]
    cos, sin = jnp.cos(ang), jnp.sin(ang)
    half = QK_ROPE // 2
    one = jnp.ones((tp, QK_NOPE), F32)
    z64 = jnp.zeros((tp, QK_NOPE), F32)
    zh = jnp.zeros((tp, half), F32)
    zt = jnp.zeros((tp, HEAD_PAD - QK_HEAD), F32)
    c_tab = jnp.concatenate([one, cos, cos, zt], axis=1)
    s_lo = jnp.concatenate([z64, -sin, zh, zt], axis=1)
    s_hi = jnp.concatenate([z64, zh, sin, zt], axis=1)
    return c_tab, s_lo, s_hi


def _rope(v, c_tab, s_lo, s_hi):
    half = QK_ROPE // 2
    return v * c_tab + pltpu.roll(v, HEAD_PAD - half, 1) * s_lo + pltpu.roll(v, half, 1) * s_hi


def _rope_t(dv, c_tab, s_lo, s_hi):
    half = QK_ROPE // 2
    return dv * c_tab + pltpu.roll(dv * s_lo, half, 1) + pltpu.roll(dv * s_hi, HEAD_PAD - half, 1)


def _rope_call(name, fn, m, tp, ins, tables, out_dtype, shared_pre=None):
    tm = _pick(tp, ROW_TILE)
    tps = tp // tm
    n = len(ins)
    width = MLA_HEADS * HEAD_PAD

    def body(*refs):
        tabs = [r[...] for r in refs[n:n + 3]]
        shared = [None if fc is None else shared_pre(refs[a][...].astype(F32), *tabs) for a, (_, fc) in enumerate(ins)]
        for hh in range(MLA_HEADS):
            lanes = slice(hh * HEAD_PAD, (hh + 1) * HEAD_PAD)
            vals = [refs[a][:, lanes].astype(F32) if shared[a] is None else shared[a] for a in range(n)]
            refs[n + 3][:, lanes] = fn(*vals, *tabs).astype(out_dtype)

    in_specs, args = [], []
    for arr, fixed_col in ins:
        if fixed_col is None:
            in_specs.append(pl.BlockSpec((tm, width), lambda i: (i, 0)))
        else:
            in_specs.append(pl.BlockSpec((tm, HEAD_PAD), lambda i, fc=fixed_col: (i, fc)))
        args.append(arr)
    for tab in tables:
        in_specs.append(pl.BlockSpec((tm, HEAD_PAD), lambda i: (lax.rem(i, tps), 0)))
        args.append(tab)
    return pl.pallas_call(
        body, out_shape=jax.ShapeDtypeStruct((m, width), out_dtype), grid=(m // tm,),
        in_specs=in_specs, out_specs=pl.BlockSpec((tm, width), lambda i: (i, 0)),
        compiler_params=_cparams(1), name=name)(*args)


def _rope_k_bwd(dk, tables, m, tp):
    tm = _pick(tp, ROW_TILE)
    tps = tp // tm

    def body(dk_ref, c_ref, lo_ref, hi_ref, o_ref):
        acc = dk_ref[:, 0:HEAD_PAD].astype(F32)
        for hh in range(1, MLA_HEADS):
            acc = acc + dk_ref[:, hh * HEAD_PAD:(hh + 1) * HEAD_PAD].astype(F32)
        d = pltpu.roll(_rope_t(acc, c_ref[...], lo_ref[...], hi_ref[...]), QK_NOPE, 1)
        lane = lax.broadcasted_iota(jnp.int32, d.shape, 1)
        o_ref[...] = jnp.where(lane < QK_ROPE, d, 0.0)

    tab = pl.BlockSpec((tm, HEAD_PAD), lambda i: (lax.rem(i, tps), 0))
    return pl.pallas_call(
        body, out_shape=jax.ShapeDtypeStruct((m, HEAD_PAD), F32), grid=(m // tm,),
        in_specs=[pl.BlockSpec((tm, MLA_HEADS * HEAD_PAD), lambda i: (i, 0)), tab, tab, tab],
        out_specs=pl.BlockSpec((tm, HEAD_PAD), lambda i: (i, 0)), compiler_params=_cparams(1),
        name="od_rope_k_bwd")(dk, *tables)


def _causal_mask(row0, col0, shape):
    rows = row0 + lax.broadcasted_iota(jnp.int32, shape, 0)
    cols = col0 + lax.broadcasted_iota(jnp.int32, shape, 1)
    return cols <= rows


NT = (((1,), (1,)), ((), ()))
TN = (((0,), (0,)), ((), ()))
HEADS_PER_STEP = 2
HEAD_STEPS = MLA_HEADS // HEADS_PER_STEP
STEP_LANES = HEADS_PER_STEP * HEAD_PAD


def _flash_fwd(q, k, v, nb, tp):
    tq = _pick(tp, ROW_TILE)
    nq = tp // tq

    def body(q_ref, k_ref, v_ref, o_ref, lse_ref):
        i = pl.program_id(2)
        qbs = [q_ref[:, hd * HEAD_PAD:(hd + 1) * HEAD_PAD] for hd in range(HEADS_PER_STEP)]

        def chunk(j, carry, masked, width=1):
            off = pl.multiple_of(j * tq, tq)
            out = []
            for hd in range(HEADS_PER_STEP):
                mx, acc = carry[hd]
                lanes = slice(hd * HEAD_PAD, (hd + 1) * HEAD_PAD)
                kb = k_ref[pl.ds(off, width * tq), lanes]
                vb = v_ref[pl.ds(off, width * tq), lanes]
                ones_lane = lax.broadcasted_iota(jnp.int32, vb.shape, 1) == V_HEAD
                vb = jnp.where(ones_lane, jnp.ones_like(vb), vb)
                s = lax.dot_general(qbs[hd], kb, NT, preferred_element_type=F32)
                if masked:
                    s = jnp.where(_causal_mask(0, 0, s.shape), s, NEG)
                m_new = jnp.maximum(mx, jnp.max(s, axis=1, keepdims=True))
                alpha = jnp.exp(mx - m_new)
                pr = jnp.exp(s - m_new)
                acc = alpha * acc + jnp.dot(pr.astype(BF16), vb, preferred_element_type=F32)
                out.append((m_new, acc))
            return tuple(out)

        one = (jnp.full((tq, 1), NEG, F32), jnp.zeros((tq, HEAD_PAD), F32))
        quads = i // 4
        carry = lax.fori_loop(0, quads, lambda jj, c: chunk(4 * jj, c, False, 4), (one,) * HEADS_PER_STEP)
        carry = lax.fori_loop(0, lax.rem(i, 4) // 2, lambda _, c: chunk(4 * quads, c, False, 2), carry)
        carry = lax.fori_loop(0, lax.rem(i, 2), lambda _, c: chunk(i - 1, c, False), carry)
        carry = chunk(i, carry, True)
        for hd in range(HEADS_PER_STEP):
            mx, acc = carry[hd]
            lanes = slice(hd * HEAD_PAD, (hd + 1) * HEAD_PAD)
            l = acc[:, V_HEAD:V_HEAD + 1]
            value_lane = lax.broadcasted_iota(jnp.int32, acc.shape, 1) < V_HEAD
            o_ref[:, lanes] = jnp.where(value_lane, acc / l, 0.0).astype(o_ref.dtype)
            lse_ref[:, lanes] = jnp.broadcast_to(mx + jnp.log(l), (tq, HEAD_PAD))

    qspec = pl.BlockSpec((tq, STEP_LANES), lambda b, hh, i: (b * nq + i, hh))
    kvspec = pl.BlockSpec((tp, STEP_LANES), lambda b, hh, i: (b, hh))
    shp = (nb * tp, MLA_HEADS * HEAD_PAD)
    return pl.pallas_call(
        body, out_shape=(jax.ShapeDtypeStruct(shp, BF16), jax.ShapeDtypeStruct(shp, F32)),
        grid=(nb, HEAD_STEPS, nq), in_specs=[qspec, kvspec, kvspec], out_specs=(qspec, qspec),
        compiler_params=_cparams(3), name="od_flash_fwd")(q, k, v)


def _flash_prep(o, do, lse_c, nb, tp):
    tq = _pick(tp, ROW_TILE)
    nq = tp // tq

    def body(o_ref, do_ref, lse_ref, lr_ref, dr_ref):
        for hh in range(MLA_HEADS):
            lanes = slice(hh * HEAD_PAD, (hh + 1) * HEAD_PAD)
            delta = jnp.sum(o_ref[:, lanes].astype(F32) * do_ref[:, lanes].astype(F32), axis=1, keepdims=True)
            lr_ref[hh] = jnp.transpose(lse_ref[:, lanes])[0:SUBLANES, :]
            dr_ref[hh] = jnp.transpose(jnp.broadcast_to(delta, (tq, HEAD_PAD)))[0:SUBLANES, :]

    qspec = pl.BlockSpec((tq, MLA_HEADS * HEAD_PAD), lambda b, i: (b * nq + i, 0))
    rspec = pl.BlockSpec((MLA_HEADS, None, SUBLANES, tq), lambda b, i: (b, i, 0, 0))
    rshape = jax.ShapeDtypeStruct((nb * MLA_HEADS, nq, SUBLANES, tq), F32)
    return pl.pallas_call(
        body, out_shape=(rshape, rshape), grid=(nb, nq), in_specs=[qspec, qspec, qspec],
        out_specs=(rspec, rspec), compiler_params=_cparams(2), name="od_flash_prep")(o, do, lse_c)


def _flash_bwd(q, k, v, do, lse_r, delta_r, nb, tp):
    tq = _pick(tp, ROW_TILE)
    nq = tp // tq

    def body(q_ref, k_ref, v_ref, do_ref, lse_ref, dl_ref, dq_ref, dk_ref, dv_ref):
        j = pl.program_id(2)

        @pl.when(j == 0)
        def _():
            dq_ref[...] = jnp.zeros_like(dq_ref)

        kbs = [k_ref[:, hd * HEAD_PAD:(hd + 1) * HEAD_PAD] for hd in range(HEADS_PER_STEP)]
        vbs = [v_ref[:, hd * HEAD_PAD:(hd + 1) * HEAD_PAD] for hd in range(HEADS_PER_STEP)]

        def chunk(i, carry, masked, width=1):
            off = pl.multiple_of(i * tq, tq)
            out = []
            for hd in range(HEADS_PER_STEP):
                dk, dv = carry[hd]
                lanes = slice(hd * HEAD_PAD, (hd + 1) * HEAD_PAD)
                qb = q_ref[pl.ds(off, width * tq), lanes]
                dob = do_ref[pl.ds(off, width * tq), lanes]
                lse = jnp.concatenate([lse_ref[hd, i + w][0:1, :] for w in range(width)], axis=1)
                delta = jnp.concatenate([dl_ref[hd, i + w][0:1, :] for w in range(width)], axis=1)
                st = lax.dot_general(kbs[hd], qb, NT, preferred_element_type=F32)
                pt = jnp.exp(st - lse)
                if masked:
                    keys = lax.broadcasted_iota(jnp.int32, st.shape, 0)
                    queries = lax.broadcasted_iota(jnp.int32, st.shape, 1)
                    pt = jnp.where(keys <= queries, pt, 0.0)
                dv = dv + jnp.dot(pt.astype(BF16), dob, preferred_element_type=F32)
                dpt = lax.dot_general(vbs[hd], dob, NT, preferred_element_type=F32)
                dst = (pt * (dpt - delta)).astype(BF16)
                dk = dk + jnp.dot(dst, qb, preferred_element_type=F32)
                dq_ref[pl.ds(off, width * tq), lanes] += lax.dot_general(dst, kbs[hd], TN,
                                                                         preferred_element_type=F32)
                out.append((dk, dv))
            return tuple(out)

        zero = jnp.zeros((tq, HEAD_PAD), F32)
        wide = jnp.minimum(nq - 1 - j, 1)
        first = j + 1 + wide
        rest = nq - first
        carry = ((zero, zero),) * HEADS_PER_STEP
        carry = lax.fori_loop(0, wide, lambda _, c: chunk(j, c, True, 2), carry)
        carry = lax.fori_loop(0, 1 - wide, lambda _, c: chunk(j, c, True), carry)
        carry = lax.fori_loop(0, rest // 4, lambda pp, c: chunk(first + 4 * pp, c, False, 4), carry)
        carry = lax.fori_loop(0, lax.rem(rest, 4) // 2, lambda _, c: chunk(first + 4 * (rest // 4), c, False, 2), carry)
        carry = lax.fori_loop(0, lax.rem(rest, 2), lambda _, c: chunk(nq - 1, c, False), carry)
        for hd in range(HEADS_PER_STEP):
            lanes = slice(hd * HEAD_PAD, (hd + 1) * HEAD_PAD)
            dk_ref[:, lanes] = carry[hd][0]
            dv_ref[:, lanes] = carry[hd][1].astype(dv_ref.dtype)

    tspec = pl.BlockSpec((tq, STEP_LANES), lambda b, hh, j: (b * nq + j, hh))
    fullspec = pl.BlockSpec((tp, STEP_LANES), lambda b, hh, j: (b, hh))
    rspec = pl.BlockSpec((HEADS_PER_STEP, nq, SUBLANES, tq), lambda b, hh, j: (b * HEAD_STEPS + hh, 0, 0, 0))
    shp = (nb * tp, MLA_HEADS * HEAD_PAD)
    return pl.pallas_call(
        body, out_shape=(jax.ShapeDtypeStruct(shp, F32), jax.ShapeDtypeStruct(shp, F32),
                         jax.ShapeDtypeStruct(shp, BF16)),
        grid=(nb, HEAD_STEPS, nq), in_specs=[fullspec, tspec, tspec, fullspec, rspec, rspec],
        out_specs=(fullspec, tspec, tspec), compiler_params=_cparams(3),
        name="od_flash_bwd")(q, k, v, do, lse_r, delta_r)


def _odd_fwd(x, p, tables, m, tp, nb):
    scale = QK_HEAD ** -0.5
    h = _rms_fwd(x, p["norm"], "od_norm")
    u = _matmul(h, p["w_in"], "nn", F32, name="od_in")
    cq = u[:, :Q_LORA]
    ckv = u[:, Q_LORA:Q_LORA + KV_LORA]
    cqn = _rms_fwd(cq, p["q_norm"], "od_q_norm")
    ckvn = _rms_fwd(ckv, p["kv_norm"], "od_kv_norm")
    q_raw = _matmul(cqn, p["w_uq"], "nn", F32, name="od_uq")
    k_raw = _matmul(ckvn, p["w_uk"], "nn", F32, name="od_uk")
    v = _matmul(ckvn, p["w_uv"], "nn", BF16, name="od_uv")
    q = _rope_call("od_rope_q", lambda qv, c, lo, hi: _rope(qv, c, lo, hi) * scale, m, tp, [(q_raw, None)], tables,
                   BF16)
    kr_col = (Q_LORA + KV_LORA) // HEAD_PAD
    k = _rope_call("od_rope_k", lambda kv, kr, c, lo, hi: kv + kr, m, tp, [(k_raw, None), (u, kr_col)], tables, BF16,
                   shared_pre=lambda uv, c, lo, hi: _rope(pltpu.roll(uv, QK_NOPE, 1), c, lo, hi))
    o, lse_c = _flash_fwd(q, k, v, nb, tp)
    out = _matmul(o, p["w_out"], "nn", F32, residual=x, name="od_out")
    return out, (x, h, cq, ckv, cqn, ckvn, q, k, v, o, lse_c)


def _odd_bwd(dout, p, tables, saved, m, tp, nb):
    scale = QK_HEAD ** -0.5
    x, h, cq, ckv, cqn, ckvn, q, k, v, o, lse_c = saved
    do = _matmul(dout, p["w_out"], "nt", BF16, name="od_out_dx")
    d_w_out = _matmul(o, dout, "tn", F32, name="od_out_dw")
    lse_r, delta_r = _flash_prep(o, do, lse_c, nb, tp)
    dq, dk, dv = _flash_bwd(q, k, v, do, lse_r, delta_r, nb, tp)
    dq_raw = _rope_call("od_rope_q_bwd", lambda d, c, lo, hi: _rope_t(d, c, lo, hi) * scale, m, tp, [(dq, None)],
                        tables, BF16)
    dkr = _rope_k_bwd(dk, tables, m, tp)
    d_w_uq = _matmul(cqn, dq_raw, "tn", F32, name="od_uq_dw")
    d_w_uk = _matmul(ckvn, dk, "tn", F32, name="od_uk_dw")
    d_w_uv = _matmul(ckvn, dv, "tn", F32, name="od_uv_dw")
    dcqn = _matmul(dq_raw, p["w_uq"], "nt", F32, name="od_uq_dx")
    dckvn = _matmul(dk, p["w_uk"], "nt", F32, name="od_uk_dx")
    dckvn = _matmul(dv, p["w_uv"], "nt", F32, residual=dckvn, name="od_uv_dx")
    dcq, d_q_norm = _rms_bwd(cq, p["q_norm"], dcqn, None, "od_q_norm_bwd")
    dckv, d_kv_norm = _rms_bwd(ckv, p["kv_norm"], dckvn, None, "od_kv_norm_bwd")
    du = jnp.concatenate([dcq, dckv, dkr], axis=1)
    d_w_in = _matmul(h, du, "tn", F32, name="od_in_dw")
    dh = _matmul(du, p["w_in"], "nt", F32, name="od_in_dx")
    dx, d_norm = _rms_bwd(x, p["norm"], dh, dout, "od_norm_bwd")
    return dx, dict(norm=d_norm, w_in=d_w_in, q_norm=d_q_norm, kv_norm=d_kv_norm, w_uq=d_w_uq, w_uk=d_w_uk,
                    w_uv=d_w_uv, w_out=d_w_out)


def _loss_head(hf, g, target, tp, t_real):
    m, c = hf.shape
    tm = _pick(tp, ROW_TILE)
    tps = tp // tm

    def body(x_ref, g_ref, t_ref, dx_ref, dg_ref, loss_ref):
        i = pl.program_id(0)
        xf = x_ref[...]
        r = lax.rsqrt(jnp.mean(xf * xf, axis=-1, keepdims=True) + EPS)
        xn = xf * r
        t_pos = lax.rem(i, tps) * tm + lax.broadcasted_iota(jnp.int32, (tm, 1), 0)
        valid = jnp.logical_and(t_pos >= N_META, t_pos < t_real)
        err = jnp.where(valid, xn * g_ref[...] - t_ref[...], 0.0)
        dyf = err * (1.0 / c)
        dyg = dyf * g_ref[...]
        dx_ref[...] = r * (dyg - xn * jnp.mean(dyg * xn, axis=-1, keepdims=True))

        @pl.when(i == 0)
        def _():
            dg_ref[...] = jnp.zeros_like(dg_ref)
            loss_ref[...] = jnp.zeros_like(loss_ref)

        dg_ref[...] += jnp.sum(dyf * xn, axis=0, keepdims=True)
        loss_ref[...] += (0.5 / c) * jnp.sum(jnp.sum(err * err, axis=1, keepdims=True), axis=0, keepdims=True)

    row = pl.BlockSpec((tm, c), lambda i: (i, 0))
    vec = pl.BlockSpec((1, c), lambda i: (0, 0))
    return pl.pallas_call(
        body, out_shape=(jax.ShapeDtypeStruct((m, c), F32), jax.ShapeDtypeStruct((1, c), F32),
                         jax.ShapeDtypeStruct((1, 1), F32)),
        grid=(m // tm,), in_specs=[row, vec, row], out_specs=(row, vec, pl.BlockSpec((1, 1), lambda i: (0, 0))),
        compiler_params=_cparams(1), name="loss_head")(hf, g, target)


def _meta_grad(dh0, nb, tp):
    d = dh0.shape[1]

    def body(x_ref, o_ref):
        @pl.when(pl.program_id(0) == 0)
        def _():
            o_ref[...] = jnp.zeros_like(o_ref)

        o_ref[...] += x_ref[...]

    return pl.pallas_call(
        body, out_shape=jax.ShapeDtypeStruct((N_META, d), F32), grid=(nb,),
        in_specs=[pl.BlockSpec((N_META, d), lambda b: (b * (tp // N_META), 0))],
        out_specs=pl.BlockSpec((N_META, d), lambda b: (0, 0)), compiler_params=_cparams(1), name="meta_grad")(dh0)


def _mesh_pos():
    x, y, c = lax.axis_index("x"), lax.axis_index("y"), lax.axis_index("c")
    return x, y, c


N_CHIP = 4
MESH_ID = pl.DeviceIdType.MESH


def _peer_chip(x, y, k):
    px = 1 - x if k & 2 else x
    py = 1 - y if k & 1 else y
    return px, py


def _all_gather(arrays):
    n = len(arrays)

    def body(*refs):
        srcs, outs = refs[:n], refs[n:2 * n]
        send_sems, recv_sems, local_sems = refs[2 * n:]
        x, y, c = _mesh_pos()
        me = 4 * x + 2 * y + c
        sibling = (x, y, 1 - c)

        def copy(a, sem, src, block, to):
            return pltpu.make_async_remote_copy(
                src_ref=src, dst_ref=outs[a].at[block], send_sem=send_sems.at[a, sem], recv_sem=recv_sems.at[a, sem],
                device_id=to, device_id_type=MESH_ID)

        local = [pltpu.make_async_copy(srcs[a], outs[a].at[me], local_sems.at[a]) for a in range(n)]
        for cp in local:
            cp.start()
        sends = [copy(a, 0, srcs[a], me, sibling) for a in range(n)]
        for k in range(1, N_CHIP):
            px, py = _peer_chip(x, y, k)
            sends += [copy(a, k, srcs[a], me, (px, py, c)) for a in range(n)]
        for cp in sends:
            cp.start()
        for k in range(1, N_CHIP):
            px, py = _peer_chip(x, y, k)
            block = 4 * px + 2 * py + c
            for a in range(n):
                copy(a, k, srcs[a], block, sibling).wait_recv()
            passed = [copy(a, N_CHIP - 1 + k, outs[a].at[block], block, sibling) for a in range(n)]
            for cp in passed:
                cp.start()
            sends += passed
        for a in range(n):
            copy(a, 0, srcs[a], 4 * x + 2 * y + (1 - c), sibling).wait_recv()
        for k in range(1, N_CHIP):
            px, py = _peer_chip(x, y, k)
            for a in range(n):
                copy(a, N_CHIP - 1 + k, srcs[a], 4 * px + 2 * py + (1 - c), sibling).wait_recv()
        for cp in sends:
            cp.wait_send()
        for cp in local:
            cp.wait()

    any_spec = pl.BlockSpec(memory_space=pl.ANY)
    out_shape = tuple(jax.ShapeDtypeStruct((N_DEV,) + a.shape, a.dtype) for a in arrays)
    return pl.pallas_call(
        body, out_shape=out_shape, in_specs=[any_spec] * n, out_specs=(any_spec,) * n,
        scratch_shapes=[pltpu.SemaphoreType.DMA((n, N_DEV - 1)), pltpu.SemaphoreType.DMA((n, N_DEV - 1)),
                        pltpu.SemaphoreType.DMA((n,))],
        name="weight_all_gather")(*arrays)


def _pair_exchange(arrays):
    n = len(arrays)

    def body(*refs):
        srcs, outs = refs[:n], refs[n:2 * n]
        send_sems, recv_sems = refs[2 * n:]
        x, y, c = _mesh_pos()
        copies = [pltpu.make_async_remote_copy(
            src_ref=srcs[a], dst_ref=outs[a], send_sem=send_sems.at[a], recv_sem=recv_sems.at[a],
            device_id=(x, y, 1 - c), device_id_type=MESH_ID) for a in range(n)]
        for cp in copies:
            cp.start()
        for cp in copies:
            cp.wait()

    any_spec = pl.BlockSpec(memory_space=pl.ANY)
    return pl.pallas_call(
        body, out_shape=tuple(jax.ShapeDtypeStruct(a.shape, a.dtype) for a in arrays), in_specs=[any_spec] * n,
        out_specs=(any_spec,) * n, scratch_shapes=[pltpu.SemaphoreType.DMA((n,)), pltpu.SemaphoreType.DMA((n,))],
        name="grad_pair_exchange")(*arrays)


def _chip_exchange(arrays):
    n = len(arrays)

    def body(*refs):
        srcs, outs = refs[:n], refs[n:2 * n]
        send_sems, recv_sems, local_sems = refs[2 * n:]
        x, y, c = _mesh_pos()
        q = 2 * x + y
        local = [pltpu.make_async_copy(srcs[a].at[q], outs[a].at[q], local_sems.at[a]) for a in range(n)]
        for cp in local:
            cp.start()

        def copy(a, k, to_q, from_q, px, py):
            return pltpu.make_async_remote_copy(
                src_ref=srcs[a].at[to_q], dst_ref=outs[a].at[from_q], send_sem=send_sems.at[a, k - 1],
                recv_sem=recv_sems.at[a, k - 1], device_id=(px, py, c), device_id_type=MESH_ID)

        sends = []
        for k in range(1, N_CHIP):
            px, py = _peer_chip(x, y, k)
            sends += [copy(a, k, 2 * px + py, q, px, py) for a in range(n)]
        for cp in sends:
            cp.start()
        for k in range(1, N_CHIP):
            px, py = _peer_chip(x, y, k)
            for a in range(n):
                copy(a, k, q, 2 * px + py, px, py).wait_recv()
        for cp in sends:
            cp.wait_send()
        for cp in local:
            cp.wait()

    any_spec = pl.BlockSpec(memory_space=pl.ANY)
    return pl.pallas_call(
        body, out_shape=tuple(jax.ShapeDtypeStruct(a.shape, a.dtype) for a in arrays), in_specs=[any_spec] * n,
        out_specs=(any_spec,) * n,
        scratch_shapes=[pltpu.SemaphoreType.DMA((n, N_CHIP - 1)), pltpu.SemaphoreType.DMA((n, N_CHIP - 1)),
                        pltpu.SemaphoreType.DMA((n,))],
        name="grad_chip_exchange")(*arrays)


REDUCE_BLOCK_BYTES = 512 * 1024


def _pair_add(a, b):
    p, r, c = a.shape
    tr = _reduce_rows(r, c)

    def body(a_ref, b_ref, o_ref):
        o_ref[...] = (a_ref[...].astype(F32) + b_ref[...].astype(F32)).astype(o_ref.dtype)

    blk = pl.BlockSpec((None, tr, c), lambda s, i: (s, i, 0))
    return pl.pallas_call(
        body, out_shape=jax.ShapeDtypeStruct(a.shape, a.dtype), grid=(p, r // tr), in_specs=[blk, blk], out_specs=blk,
        compiler_params=_cparams(2), name="grad_pair_add")(a, b)


def _reduce_rows(r, c):
    best = None
    for t in range(16, r + 1, 16):
        if r % t == 0 and t * c * 4 <= REDUCE_BLOCK_BYTES:
            best = t
    assert best is not None, (r, c)
    return best


def _reduce_adamw(parts, w, mom, vel):
    n_parts, r, c = parts.shape
    tr = _reduce_rows(r, c)
    c1 = 1.0 - ADAM_B1 ** ADAM_STEP
    c2 = 1.0 - ADAM_B2 ** ADAM_STEP

    def body(p_ref, w_ref, m_ref, v_ref, g_ref, d_ref, mo_ref, vo_ref):
        g = p_ref[0].astype(F32)
        for s in range(1, n_parts):
            g = g + p_ref[s].astype(F32)
        mn = ADAM_B1 * m_ref[...] + (1.0 - ADAM_B1) * g
        vn = ADAM_B2 * v_ref[...] + (1.0 - ADAM_B2) * (g * g)
        m_hat = mn / c1
        v_hat = vn / c2
        g_ref[...] = g
        d_ref[...] = -ADAM_LR * (m_hat / (jnp.sqrt(v_hat) + ADAM_EPS) + ADAM_WD * w_ref[...])
        mo_ref[...] = mn
        vo_ref[...] = vn

    blk = pl.BlockSpec((tr, c), lambda i: (i, 0))
    shp = jax.ShapeDtypeStruct((r, c), F32)
    return pl.pallas_call(
        body, out_shape=(shp, shp, shp, shp), grid=(r // tr,),
        in_specs=[pl.BlockSpec((n_parts, tr, c), lambda i: (0, i, 0)), blk, blk, blk], out_specs=(blk, blk, blk, blk),
        compiler_params=_cparams(1), name="reduce_adamw")(parts, w, mom, vel)


def _pack_rows(pieces, width, row_multiple, dtype):
    flat = jnp.concatenate([p.astype(dtype).reshape(-1) for p in pieces])
    rows = -(-flat.shape[0] // (width * row_multiple)) * row_multiple
    return jnp.pad(flat, (0, rows * width - flat.shape[0])).reshape(rows, width)


def _unshard(gathered, axis):
    moved = jnp.moveaxis(gathered, 0, axis)
    shape = list(moved.shape)
    shape[axis:axis + 2] = [shape[axis] * shape[axis + 1]]
    return moved.reshape(shape)


def _to_slots(full, axis):
    shape = list(full.shape)
    shape[axis:axis + 1] = [N_DEV, shape[axis] // N_DEV]
    return jnp.moveaxis(full.reshape(shape), axis, 0)


def _core_slots(full, axis, core):
    shape = list(full.shape)
    shape[axis:axis + 1] = [N_CHIP, 2, shape[axis] // N_DEV]
    picked = lax.dynamic_index_in_dim(full.reshape(shape), core, axis + 1, keepdims=False)
    return jnp.moveaxis(picked, axis, 0)


def _block_diag(w):
    hh, d, _ = w.shape
    eye = jnp.eye(hh, dtype=w.dtype)
    return (w[:, :, None, :] * eye[:, None, :, None]).reshape(hh * d, hh * d)


def _block_diag_t(full, hh):
    d = full.shape[0] // hh
    f4 = full.reshape(hh, d, hh, d)
    return jnp.stack([f4[i, :, i, :] for i in range(hh)], axis=0)


def _pad_heads(w, width):
    r = w.shape[0]
    w3 = w.reshape(r, MLA_HEADS, width)
    return jnp.pad(w3, ((0, 0), (0, 0), (0, HEAD_PAD - width))).reshape(r, MLA_HEADS * HEAD_PAD)


def _unpad_heads(w, width):
    r = w.shape[0]
    return w.reshape(r, MLA_HEADS, HEAD_PAD)[:, :, :width].reshape(r, MLA_HEADS * width)


def kernel(x, meta_tokens, ev_norm, ev_w_in, ev_conv_a, ev_conv_b, ev_conv_b_bias, ev_gate_r_w, ev_gate_r_b, ev_gate_i_w, ev_gate_i_b, ev_lru_lambda, ev_w_out, od_norm, od_w_in, od_q_norm, od_kv_norm, od_w_uq, od_w_ukv, od_w_out, ffn_norm, ffn_w_up, ffn_conv_w, ffn_conv_b, ffn_w_down, final_norm, loss_target, m_meta_tokens, m_ev_norm, m_ev_w_in, m_ev_conv_a, m_ev_conv_b, m_ev_conv_b_bias, m_ev_gate_r_w, m_ev_gate_r_b, m_ev_gate_i_w, m_ev_gate_i_b, m_ev_lru_lambda, m_ev_w_out, m_od_norm, m_od_w_in, m_od_q_norm, m_od_kv_norm, m_od_w_uq, m_od_w_ukv, m_od_w_out, m_ffn_norm, m_ffn_w_up, m_ffn_conv_w, m_ffn_conv_b, m_ffn_w_down, m_final_norm, v_meta_tokens, v_ev_norm, v_ev_w_in, v_ev_conv_a, v_ev_conv_b, v_ev_conv_b_bias, v_ev_gate_r_w, v_ev_gate_r_b, v_ev_gate_i_w, v_ev_gate_i_b, v_ev_lru_lambda, v_ev_w_out, v_od_norm, v_od_w_in, v_od_q_norm, v_od_kv_norm, v_od_w_uq, v_od_w_ukv, v_od_w_out, v_ffn_norm, v_ffn_w_up, v_ffn_conv_w, v_ffn_conv_b, v_ffn_w_down, v_final_norm):
    given = dict(locals())
    names = [n for n, _ in PARAMS]
    axis_of = dict(PARAMS)
    w_loc = {n: given[n] for n in names}
    m_loc = {n: given["m_" + n] for n in names}
    v_loc = {n: given["v_" + n] for n in names}
    sharded = [n for n in names if axis_of[n] is not None]
    replicated = [n for n in names if axis_of[n] is None]
    small = [n for n in sharded if n not in BIG]

    nb, seq, d = x.shape
    t_real = N_META + seq
    tp = -(-t_real // ROW_TILE) * ROW_TILE
    m = nb * tp

    small_pack = _pack_rows([w_loc[n] for n in small], LANES, SUBLANES, F32)
    gathered = _all_gather([w_loc[n].astype(BF16) for n in BIG] + [small_pack])
    full = {n: w_loc[n] for n in replicated}
    for n, g in zip(BIG, gathered[:-1]):
        full[n] = _unshard(g, axis_of[n])
    flat = gathered[-1].reshape(N_DEV, -1)
    off = 0
    for n in small:
        shard = w_loc[n].shape
        size = math.prod(shard)
        full[n] = _unshard(flat[:, off:off + size].reshape((N_DEV,) + shard), axis_of[n])
        off += size

    tables = _rope_tables(tp)

    def even_params(j):
        w_out = full["ev_w_out"][j]
        return dict(norm=full["ev_norm"][j][None], w_in=full["ev_w_in"][j], conv_a=full["ev_conv_a"][j],
                    conv_b=full["ev_conv_b"][j], conv_b_bias=full["ev_conv_b_bias"][j][None],
                    gate_r=_block_diag(full["ev_gate_r_w"][j]).astype(BF16),
                    gate_i=_block_diag(full["ev_gate_i_w"][j]).astype(BF16),
                    gate_r_b=full["ev_gate_r_b"][j][None], gate_i_b=full["ev_gate_i_b"][j][None],
                    lam=full["ev_lru_lambda"][j][None], w_out=w_out, w_out_a=w_out[:LRU_WIDTH],
                    w_out_b=w_out[LRU_WIDTH:])

    def odd_params(j):
        w_ukv = full["od_w_ukv"][j].reshape(KV_LORA, MLA_HEADS, QK_NOPE + V_HEAD)
        w_uk = w_ukv[:, :, :QK_NOPE].reshape(KV_LORA, MLA_HEADS * QK_NOPE)
        w_uv = w_ukv[:, :, QK_NOPE:].reshape(KV_LORA, MLA_HEADS * V_HEAD)
        w_out = full["od_w_out"][j].reshape(MLA_HEADS, V_HEAD, d)
        w_out = jnp.pad(w_out, ((0, 0), (0, HEAD_PAD - V_HEAD), (0, 0))).reshape(MLA_HEADS * HEAD_PAD, d)
        return dict(norm=full["od_norm"][j][None], w_in=jnp.pad(full["od_w_in"][j], ((0, 0), (0, ODD_IN_PAD - ODD_IN))),
                    q_norm=full["od_q_norm"][j][None], kv_norm=full["od_kv_norm"][j][None],
                    w_uq=_pad_heads(full["od_w_uq"][j], QK_HEAD), w_uk=_pad_heads(w_uk, QK_NOPE),
                    w_uv=_pad_heads(w_uv, V_HEAD), w_out=w_out)

    def ffn_params(layer):
        w_up = full["ffn_w_up"][layer]
        return dict(norm=full["ffn_norm"][layer][None], w_up=w_up, conv_w=full["ffn_conv_w"][layer],
                    conv_b=full["ffn_conv_b"][layer][None], w_down=full["ffn_w_down"][layer])

    meta = jnp.broadcast_to(full["meta_tokens"][None], (nb, N_META, d))
    h0 = jnp.concatenate([meta, x, jnp.zeros((nb, tp - t_real, d), F32)], axis=1).reshape(m, d)
    hcur = h0
    tape = []
    for layer in range(4):
        j = layer // 2
        if layer % 2 == 0:
            mp = even_params(j)
            hcur, saved = _even_fwd(hcur, mp, m, tp, nb)
        else:
            mp = odd_params(j)
            hcur, saved = _odd_fwd(hcur, mp, tables, m, tp, nb)
        fp = ffn_params(layer)
        hcur, fsaved = _ffn_fwd(hcur, fp, m, tp)
        tape.append((mp, saved, fp, fsaved))

    target = jnp.pad(loss_target, ((0, 0), (N_META, tp - t_real), (0, 0))).reshape(m, d)
    dh, d_final_norm, loss_part = _loss_head(hcur, full["final_norm"][None], target, tp, t_real)

    grads = {"final_norm": d_final_norm[0]}
    ev_g, od_g, ffn_g = [None, None], [None, None], [None] * 4
    for layer in reversed(range(4)):
        mp, saved, fp, fsaved = tape[layer]
        dh, ffn_g[layer] = _ffn_bwd(dh, fp, fsaved, m, tp)
        if layer % 2 == 0:
            dh, ev_g[layer // 2] = _even_bwd(dh, mp, saved, m, tp, nb)
        else:
            dh, od_g[layer // 2] = _odd_bwd(dh, mp, tables, saved, m, tp, nb)

    dh3 = dh.reshape(nb, tp, d)
    grad_x = dh3[:, N_META:t_real]
    grads["meta_tokens"] = _meta_grad(dh, nb, tp)

    def stack(lst, key, fn=lambda a: a):
        return jnp.stack([fn(g[key]) for g in lst], axis=0)

    grads["ev_norm"] = stack(ev_g, "norm", lambda a: a[0])
    grads["ev_w_in"] = stack(ev_g, "w_in")
    grads["ev_conv_a"] = stack(ev_g, "conv_a")
    grads["ev_conv_b"] = stack(ev_g, "conv_b")
    grads["ev_conv_b_bias"] = stack(ev_g, "conv_b_bias", lambda a: a[0])
    grads["ev_gate_r_w"] = stack(ev_g, "gate_r", lambda a: _block_diag_t(a, 8))
    grads["ev_gate_r_b"] = stack(ev_g, "gate_r_b", lambda a: a[0])
    grads["ev_gate_i_w"] = stack(ev_g, "gate_i", lambda a: _block_diag_t(a, 8))
    grads["ev_gate_i_b"] = stack(ev_g, "gate_i_b", lambda a: a[0])
    grads["ev_lru_lambda"] = stack(ev_g, "lam", lambda a: a[0])
    grads["ev_w_out"] = stack(ev_g, "w_out")
    grads["od_norm"] = stack(od_g, "norm", lambda a: a[0])
    grads["od_w_in"] = stack(od_g, "w_in", lambda a: a[:, :ODD_IN])
    grads["od_q_norm"] = stack(od_g, "q_norm", lambda a: a[0])
    grads["od_kv_norm"] = stack(od_g, "kv_norm", lambda a: a[0])
    grads["od_w_uq"] = stack(od_g, "w_uq", lambda a: _unpad_heads(a, QK_HEAD))

    def ukv(g):
        gk = g["w_uk"].reshape(KV_LORA, MLA_HEADS, HEAD_PAD)[:, :, :QK_NOPE]
        gv = g["w_uv"].reshape(KV_LORA, MLA_HEADS, HEAD_PAD)[:, :, :V_HEAD]
        return jnp.concatenate([gk, gv], axis=2).reshape(KV_LORA, MLA_HEADS * (QK_NOPE + V_HEAD))

    grads["od_w_ukv"] = jnp.stack([ukv(g) for g in od_g], axis=0)
    grads["od_w_out"] = stack(od_g, "w_out", lambda a: a.reshape(MLA_HEADS, HEAD_PAD, d)[:, :V_HEAD].reshape(-1, d))
    grads["ffn_norm"] = stack(ffn_g, "norm", lambda a: a[0])
    grads["ffn_w_up"] = stack(ffn_g, "w_up")
    grads["ffn_conv_w"] = stack(ffn_g, "conv_w")
    grads["ffn_conv_b"] = stack(ffn_g, "conv_b", lambda a: a[0])
    grads["ffn_w_down"] = stack(ffn_g, "w_down")

    order = small + replicated
    slot_parts = [_to_slots(grads[n], axis_of[n]).reshape(N_DEV, -1) for n in small]
    slot_parts += [jnp.broadcast_to(grads[n].reshape(1, -1), (N_DEV, grads[n].size)) for n in replicated]
    slot_parts.append(jnp.broadcast_to(loss_part, (N_DEV, 1)))
    g_flat = jnp.concatenate(slot_parts, axis=1)
    n_flat = g_flat.shape[1]
    rows = -(-n_flat // (1024 * 128)) * 128
    g_small = jnp.pad(g_flat, ((0, 0), (0, rows * 1024 - n_flat))).reshape(N_DEV, rows, 1024)

    def rows_of(n):
        shard = w_loc[n].shape
        return (math.prod(shard[:-1]), shard[-1])

    core = lax.axis_index("c")

    def core_slots(n, which):
        return _core_slots(grads[n], axis_of[n], which).astype(BF16).reshape((N_CHIP,) + rows_of(n))

    small_by_core = jnp.swapaxes(g_small.reshape((N_CHIP, 2) + g_small.shape[1:]), 0, 1)
    mine = [core_slots(n, core) for n in BIG] + [lax.dynamic_index_in_dim(small_by_core, core, 0, keepdims=False)]
    theirs = [core_slots(n, 1 - core) for n in BIG]
    theirs.append(lax.dynamic_index_in_dim(small_by_core, 1 - core, 0, keepdims=False))
    from_sibling = _pair_exchange(theirs)
    parts = _chip_exchange([_pair_add(a, b) for a, b in zip(mine, from_sibling)])

    g_out, d_out, m_out, v_out = {}, {}, {}, {}
    for n, part in zip(BIG, parts[:-1]):
        res = _reduce_adamw(part, *[t[n].reshape(rows_of(n)) for t in (w_loc, m_loc, v_loc)])
        for out, r in zip((g_out, d_out, m_out, v_out), res):
            out[n] = r.reshape(w_loc[n].shape)

    def flat_local(tree):
        flat = jnp.concatenate([tree[n].reshape(-1) for n in order])
        return jnp.pad(flat, (0, rows * 1024 - flat.shape[0])).reshape(rows, 1024)

    res = _reduce_adamw(parts[-1], flat_local(w_loc), flat_local(m_loc), flat_local(v_loc))
    loss = res[0].reshape(-1)[n_flat - 1]
    for out, r in zip((g_out, d_out, m_out, v_out), res):
        flat = r.reshape(-1)
        off = 0
        for n in order:
            size = w_loc[n].size
            out[n] = flat[off:off + size].reshape(w_loc[n].shape)
            off += size
    return (loss, grad_x, *[g_out[n] for n in names], *[d_out[n] for n in names], *[m_out[n] for n in names],
            *[v_out[n] for n in names])
```

```python
import math

import jax
import jax.numpy as jnp
from jax import lax
from jax.experimental import pallas as pl
from jax.experimental.pallas import tpu as pltpu

F32 = jnp.float32
BF16 = jnp.bfloat16

N_DEV = 8
N_META = 16
EPS = 1e-6
LRU_C = 8.0
MLA_HEADS = 16
QK_NOPE = 64
QK_ROPE = 32
QK_HEAD = QK_NOPE + QK_ROPE
V_HEAD = 64
HEAD_PAD = 128
Q_LORA = 384
KV_LORA = 256
ODD_IN = Q_LORA + KV_LORA + QK_ROPE
ODD_IN_PAD = 768
ROPE_BASE = 10000.0
LRU_WIDTH = 512
D_FF = 2816

ADAM_LR = 0.001
ADAM_B1 = 0.9
ADAM_B2 = 0.999
ADAM_EPS = 1e-08
ADAM_WD = 0.01
ADAM_STEP = 10

ROW_TILE = 384
SUBLANES = 8
HALO_ROWS = 16
LANES = 128
VMEM_LIMIT = 48 * 1024 * 1024
NEG = -1e30

PARAMS = (
    ("meta_tokens", 1), ("ev_norm", None), ("ev_w_in", 2), ("ev_conv_a", 2), ("ev_conv_b", 2),
    ("ev_conv_b_bias", None), ("ev_gate_r_w", None), ("ev_gate_r_b", None), ("ev_gate_i_w", None),
    ("ev_gate_i_b", None), ("ev_lru_lambda", None), ("ev_w_out", 1), ("od_norm", 1), ("od_w_in", 1),
    ("od_q_norm", 1), ("od_kv_norm", 1), ("od_w_uq", 2), ("od_w_ukv", 2), ("od_w_out", 1),
    ("ffn_norm", None), ("ffn_w_up", 2), ("ffn_conv_w", 2), ("ffn_conv_b", None), ("ffn_w_down", 1),
    ("final_norm", None),
)
BIG = ("ev_w_in", "ev_w_out", "od_w_in", "od_w_uq", "od_w_ukv", "od_w_out", "ffn_w_up", "ffn_w_down")


def _cparams(n_grid):
    return pltpu.CompilerParams(dimension_semantics=("arbitrary",) * n_grid, vmem_limit_bytes=VMEM_LIMIT)


def _pick(dim, target):
    if dim <= target:
        return dim
    best = None
    for t in range(LANES, target + 1, LANES):
        if dim % t == 0:
            best = t
    assert best is not None, (dim, target)
    return best


MATMUL_VMEM_BUDGET = 38 * 1024 * 1024
HBM_BYTES_PER_US = 3.0e6
MXU_FLOPS_PER_US = 9.0e8
ACC_BYTES_PER_US = 7.6e6
GRID_STEP_US = 0.35


def _tile_candidates(dim):
    return [t for t in range(LANES, dim + 1, LANES) if dim % t == 0] or [dim]


def _matmul_tiles(m, n, k, sa, sb, so, sr, transposed_lhs):
    best, best_cost = None, None
    for tm in _tile_candidates(m):
        for tn in _tile_candidates(n):
            for tk in _tile_candidates(k):
                nk = k // tk
                vmem = 2 * (tm * tk * sa + tk * tn * sb) + tm * tn * ((4 if nk > 1 else 0) + 2 * so + 2 * sr)
                vmem += (tm * tk * 2 if sa > 2 else 0) + (tk * tn * 2 if sb > 2 else 0) + tm * tn * 4
                if vmem > MATMUL_VMEM_BUDGET:
                    continue
                steps = (m // tm) * (n // tn) * nk
                traffic = m * k * sa * (n // tn) + k * n * sb * (m // tm) + m * n * (so + sr)
                acc_us = steps * tm * tn * 4 / ACC_BYTES_PER_US if nk > 1 else 0.0
                busy_us = 0.0 if transposed_lhs else 2.0 * m * n * k / MXU_FLOPS_PER_US + acc_us
                cost = max(traffic / HBM_BYTES_PER_US, busy_us) + steps * GRID_STEP_US
                if best_cost is None or cost < best_cost:
                    best, best_cost = (tm, tn, tk), cost
    assert best is not None, (m, n, k)
    return best


def _matmul(a, b, mode, out_dtype=F32, residual=None, name="mm", b_col_off=0):
    if mode == "nn":
        (m, k), (k2, n) = a.shape, b.shape
    elif mode == "nt":
        (m, k), n = a.shape, b.shape[0]
        k2 = k if b_col_off or b.shape[1] > k else b.shape[1]
    else:
        (k, m), (k2, n) = a.shape, b.shape
    assert k == k2, (a.shape, b.shape, mode)
    tm, tn, tk = _matmul_tiles(m, n, k, a.dtype.itemsize, b.dtype.itemsize, jnp.dtype(out_dtype).itemsize,
                               0 if residual is None else residual.dtype.itemsize, mode == "tn")
    nk = k // tk
    if mode == "tn":
        a_spec = pl.BlockSpec((tk, tm), lambda i, j, kk: (kk, i))
        dims = (((0,), (0,)), ((), ()))
    else:
        a_spec = pl.BlockSpec((tm, tk), lambda i, j, kk: (i, kk))
        dims = (((1,), (1 if mode == "nt" else 0,)), ((), ()))
    if mode == "nt":
        assert b_col_off % tk == 0, (b_col_off, tk)
        b_spec = pl.BlockSpec((tn, tk), lambda i, j, kk: (j, kk + b_col_off // tk))
    else:
        b_spec = pl.BlockSpec((tk, tn), lambda i, j, kk: (kk, j))
    o_spec = pl.BlockSpec((tm, tn), lambda i, j, kk: (i, j))
    has_res = residual is not None

    def body(*refs):
        a_ref, b_ref = refs[:2]
        r_ref = refs[2] if has_res else None
        o_ref = refs[3] if has_res else refs[2]
        part = lax.dot_general(a_ref[...].astype(BF16), b_ref[...].astype(BF16), dims, preferred_element_type=F32)

        def finish(out):
            if has_res:
                out = out + r_ref[...].astype(F32)
            o_ref[...] = out.astype(o_ref.dtype)

        if nk == 1:
            finish(part)
            return
        acc_ref = refs[-1]
        kk = pl.program_id(2)

        @pl.when(kk == 0)
        def _():
            acc_ref[...] = part

        @pl.when(kk > 0)
        def _():
            acc_ref[...] += part

        @pl.when(kk == nk - 1)
        def _():
            finish(acc_ref[...])

    in_specs = [a_spec, b_spec] + ([o_spec] if has_res else [])
    args = (a, b) + ((residual,) if has_res else ())
    return pl.pallas_call(
        body, out_shape=jax.ShapeDtypeStruct((m, n), out_dtype), grid=(m // tm, n // tn, nk),
        in_specs=in_specs, out_specs=o_spec, scratch_shapes=[pltpu.VMEM((tm, tn), F32)] if nk > 1 else [],
        compiler_params=_cparams(3), name=name)(*args)


def _rms_fwd(x, g, name):
    m, c = x.shape
    tm = _pick(m, ROW_TILE)

    def body(x_ref, g_ref, o_ref):
        xf = x_ref[...].astype(F32)
        r = lax.rsqrt(jnp.mean(xf * xf, axis=-1, keepdims=True) + EPS)
        o_ref[...] = (xf * r * g_ref[...]).astype(o_ref.dtype)

    return pl.pallas_call(
        body, out_shape=jax.ShapeDtypeStruct((m, c), BF16), grid=(m // tm,),
        in_specs=[pl.BlockSpec((tm, c), lambda i: (i, 0)), pl.BlockSpec((1, c), lambda i: (0, 0))],
        out_specs=pl.BlockSpec((tm, c), lambda i: (i, 0)), compiler_params=_cparams(1), name=name)(x, g)


def _rms_bwd(x, g, dy, residual, name):
    m, c = x.shape
    tm = _pick(m, ROW_TILE)
    has_res = residual is not None

    def body(*refs):
        if has_res:
            x_ref, g_ref, dy_ref, r_ref, dx_ref, dg_ref = refs
        else:
            x_ref, g_ref, dy_ref, dx_ref, dg_ref = refs
        xf = x_ref[...].astype(F32)
        dyf = dy_ref[...].astype(F32)
        r = lax.rsqrt(jnp.mean(xf * xf, axis=-1, keepdims=True) + EPS)
        xn = xf * r
        dyg = dyf * g_ref[...]
        dx = r * (dyg - xn * jnp.mean(dyg * xn, axis=-1, keepdims=True))
        if has_res:
            dx = dx + r_ref[...]
        dx_ref[...] = dx

        @pl.when(pl.program_id(0) == 0)
        def _():
            dg_ref[...] = jnp.zeros_like(dg_ref)

        dg_ref[...] += jnp.sum(dyf * xn, axis=0, keepdims=True)

    row = pl.BlockSpec((tm, c), lambda i: (i, 0))
    vec = pl.BlockSpec((1, c), lambda i: (0, 0))
    in_specs = [row, vec, row] + ([row] if has_res else [])
    args = (x, g, dy) + ((residual,) if has_res else ())
    return pl.pallas_call(
        body, out_shape=(jax.ShapeDtypeStruct((m, c), F32), jax.ShapeDtypeStruct((1, c), F32)), grid=(m // tm,),
        in_specs=in_specs, out_specs=(row, vec), compiler_params=_cparams(1), name=name)(*args)


def _chan_call(name, fn, m, tp, tc, ncol, row_ins=(), prev_ins=(), next_ins=(), chan_ins=(), row_outs=(),
               red_outs=(), row_split=1):
    tm = _pick(tp, ROW_TILE) // row_split
    tps = tp // tm
    nrow = m // tm
    halo_blocks = tm // HALO_ROWS
    last_halo = m // HALO_ROWS - 1
    n_in = len(row_ins) + len(prev_ins) + len(next_ins) + len(chan_ins)
    n_r, n_p, n_n = len(row_ins), len(prev_ins), len(next_ins)

    def body(*refs):
        i = pl.program_id(1)
        pos = lax.rem(i, tps)
        at_start = pos == 0
        at_end = pos == tps - 1
        rows = [r[...].astype(F32) for r in refs[:n_r]]
        prevs = [jnp.where(at_start, 0.0, r[...].astype(F32)[SUBLANES:]) for r in refs[n_r:n_r + n_p]]
        nexts = [jnp.where(at_end, 0.0, r[...].astype(F32)[:SUBLANES]) for r in refs[n_r + n_p:n_r + n_p + n_n]]
        chans = [r[...] for r in refs[n_r + n_p + n_n:n_in]]
        out_refs = refs[n_in:n_in + len(row_outs)]
        red_refs = refs[n_in + len(row_outs):]
        row_vals, red_vals = fn(rows, prevs, nexts, chans)
        for ref, val in zip(out_refs, row_vals):
            ref[...] = val.astype(ref.dtype)
        if red_refs:
            @pl.when(i == 0)
            def _():
                for ref in red_refs:
                    ref[...] = jnp.zeros_like(ref)

            for ref, val in zip(red_refs, red_vals):
                ref[...] += val

    in_specs, args = [], []
    for arr, off in row_ins:
        in_specs.append(pl.BlockSpec((tm, tc), lambda j, i, off=off: (i, j + off)))
        args.append(arr)
    for arr, off in prev_ins:
        in_specs.append(pl.BlockSpec((HALO_ROWS, tc),
                                     lambda j, i, off=off: (jnp.maximum(i * halo_blocks - 1, 0), j + off)))
        args.append(arr)
    for arr, off in next_ins:
        in_specs.append(pl.BlockSpec((HALO_ROWS, tc),
                                     lambda j, i, off=off: (jnp.minimum((i + 1) * halo_blocks, last_halo), j + off)))
        args.append(arr)
    for arr, off in chan_ins:
        in_specs.append(pl.BlockSpec((arr.shape[0], tc), lambda j, i, off=off: (0, j + off)))
        args.append(arr)
    out_shape, out_specs = [], []
    for (dt,) in row_outs:
        out_shape.append(jax.ShapeDtypeStruct((m, ncol * tc), dt))
        out_specs.append(pl.BlockSpec((tm, tc), lambda j, i: (i, j)))
    for (k,) in red_outs:
        out_shape.append(jax.ShapeDtypeStruct((k, ncol * tc), F32))
        out_specs.append(pl.BlockSpec((k, tc), lambda j, i: (0, j)))
    return pl.pallas_call(
        body, out_shape=tuple(out_shape), grid=(ncol, nrow), in_specs=in_specs, out_specs=tuple(out_specs),
        compiler_params=_cparams(2), name=name)(*args)


def _shift_down(x, prev8, s):
    if s == 0:
        return x
    tm, tc = x.shape
    groups = tm // SUBLANES
    xr = pltpu.roll(x.reshape(groups, SUBLANES, tc), s, 1)
    before = jnp.concatenate([pltpu.roll(prev8, s, 0)[None], xr[:-1]], axis=0)
    rid = lax.broadcasted_iota(jnp.int32, xr.shape, 1)
    return jnp.where(rid < s, before, xr).reshape(tm, tc)


def _shift_up(x, next8, s):
    if s == 0:
        return x
    tm, tc = x.shape
    groups = tm // SUBLANES
    xr = pltpu.roll(x.reshape(groups, SUBLANES, tc), SUBLANES - s, 1)
    after = jnp.concatenate([xr[1:], pltpu.roll(next8, SUBLANES - s, 0)[None]], axis=0)
    rid = lax.broadcasted_iota(jnp.int32, xr.shape, 1)
    return jnp.where(rid >= SUBLANES - s, after, xr).reshape(tm, tc)


def _taps(x, prev8, kw):
    return [_shift_down(x, prev8, kw - 1 - k) for k in range(kw)]


def _conv_taps(taps, w):
    y = w[0:1, :] * taps[0]
    for k in range(1, len(taps)):
        y = y + w[k:k + 1, :] * taps[k]
    return y


def _conv_dw_taps(dy, taps):
    shape = (SUBLANES, dy.shape[1])
    rid = lax.broadcasted_iota(jnp.int32, shape, 0)
    out = jnp.zeros(shape, F32)
    for k, tap in enumerate(taps):
        out = out + jnp.where(rid == k, jnp.sum(dy * tap, axis=0, keepdims=True), 0.0)
    return out


def _conv_fwd(x, prev8, w):
    return _conv_taps(_taps(x, prev8, w.shape[0]), w)


def _conv_dw(dy, x, prev8, kw):
    return _conv_dw_taps(dy, _taps(x, prev8, kw))


def _conv_dx(dy, next8, w):
    kw = w.shape[0]
    dx = w[kw - 1:kw, :] * dy
    for k in range(kw - 1):
        dx = dx + w[k:k + 1, :] * _shift_up(dy, next8, kw - 1 - k)
    return dx


def _sigmoid(x):
    return 1.0 / (1.0 + jnp.exp(-x))


def _sigmoid_tanh(x):
    return 0.5 + 0.5 * jnp.tanh(0.5 * x)


def _expm1(x):
    series = x * (1.0 + x * 0.5 * (1.0 + x * (1.0 / 3.0) * (1.0 + x * 0.25 * (1.0 + x * 0.2))))
    return jnp.where(jnp.abs(x) < 0.3, series, jnp.exp(x) - 1.0)


def _softplus_neg(lam):
    e = jnp.exp(-jnp.abs(lam))
    log1p = jnp.where(e < 1e-2, e * (1.0 - e * (0.5 - e * (1.0 / 3.0))), jnp.log(1.0 + e))
    return jnp.maximum(-lam, 0.0) + log1p


GELU_C = math.sqrt(2.0 / math.pi)


def _gelu(x):
    return 0.5 * x * (1.0 + jnp.tanh(GELU_C * (x + 0.044715 * x * x * x)))


def _gelu_grad(x):
    t = jnp.tanh(GELU_C * (x + 0.044715 * x * x * x))
    return 0.5 * (1.0 + t) + 0.5 * x * (1.0 - t * t) * GELU_C * (1.0 + 3.0 * 0.044715 * x * x)


FFN_COL_TILE = 1408


def _ffn_fwd(x, p, m, tp):
    h = _rms_fwd(x, p["norm"], "ffn_norm")
    u = _matmul(h, p["w_up"], "nn", BF16, name="ffn_up")
    tc = FFN_COL_TILE
    ncol = D_FF // tc

    def gate(rows, prevs, nexts, chans):
        ua, ug = rows
        wa, wg, ba, bg = chans
        a = _conv_fwd(ua, prevs[0], wa) + ba
        g = _conv_fwd(ug, prevs[1], wg) + bg
        return [a * _sigmoid_tanh(a) * g, a, g], []

    z, a_act, g_act = _chan_call(
        "ffn_gate", gate, m, tp, tc, ncol, row_ins=[(u, 0), (u, ncol)], prev_ins=[(u, 0), (u, ncol)],
        chan_ins=[(p["conv_w"], 0), (p["conv_w"], ncol), (p["conv_b"], 0), (p["conv_b"], ncol)],
        row_outs=[(BF16,), (BF16,), (BF16,)])
    out = _matmul(z, p["w_down"], "nn", F32, residual=x, name="ffn_down")
    return out, (x, h, u, z, a_act, g_act)


def _ffn_bwd(dout, p, saved, m, tp):
    x, h, u, z, a_act, g_act = saved
    tc = FFN_COL_TILE
    ncol = D_FF // tc
    dz = _matmul(dout, p["w_down"], "nt", F32, name="ffn_down_dx")
    d_w_down = _matmul(z, dout, "tn", F32, name="ffn_down_dw")

    def act_bwd(a, g, dzv):
        sg = _sigmoid_tanh(a)
        return dzv * g * (sg * (1.0 + a * (1.0 - sg))), dzv * a * sg

    def gate_bwd(rows, prevs, nexts, chans):
        ua, ug, dzv, a, g = rows
        da, dg = act_bwd(a, g, dzv)
        da_next, dg_next = act_bwd(nexts[1], nexts[2], nexts[0])
        ups_a = [_shift_up(da, da_next, 2 - k) for k in range(3)]
        ups_g = [_shift_up(dg, dg_next, 2 - k) for k in range(3)]
        return ([_conv_taps(ups_a, chans[0]), _conv_taps(ups_g, chans[1])],
                [_conv_dw_taps(ua, ups_a), _conv_dw_taps(ug, ups_g),
                 jnp.sum(da, axis=0, keepdims=True), jnp.sum(dg, axis=0, keepdims=True)])

    dua, dug, dcw_a, dcw_g, dcb_a, dcb_g = _chan_call(
        "ffn_gate_bwd", gate_bwd, m, tp, tc, ncol,
        row_ins=[(u, 0), (u, ncol), (dz, 0), (a_act, 0), (g_act, 0)], next_ins=[(dz, 0), (a_act, 0), (g_act, 0)],
        chan_ins=[(p["conv_w"], 0), (p["conv_w"], ncol)],
        row_outs=[(BF16,), (BF16,)], red_outs=[(SUBLANES,), (SUBLANES,), (1,), (1,)], row_split=2)
    d_w_up = jnp.concatenate([_matmul(h, dua, "tn", F32, name="ffn_up_dw_a"),
                              _matmul(h, dug, "tn", F32, name="ffn_up_dw_g")], axis=1)
    dh = _matmul(dua, p["w_up"], "nt", F32, name="ffn_up_dx_a")
    dh = _matmul(dug, p["w_up"], "nt", F32, residual=dh, name="ffn_up_dx_g", b_col_off=D_FF)
    dx, d_norm = _rms_bwd(x, p["norm"], dh, dout, "ffn_norm_bwd")
    d_conv_w = jnp.concatenate([dcw_a[:3], dcw_g[:3]], axis=1)
    d_conv_b = jnp.concatenate([dcb_a, dcb_g], axis=1)
    return dx, dict(norm=d_norm, w_up=d_w_up, conv_w=d_conv_w, conv_b=d_conv_b, w_down=d_w_down)


def _to_scan(x, nb, tp):
    return x.reshape(nb, tp, LRU_WIDTH // LANES, LANES).transpose(1, 0, 2, 3).reshape(tp, -1, LANES)


def _from_scan(x, nb, tp):
    return x.reshape(tp, nb, LRU_WIDTH // LANES, LANES).transpose(1, 0, 2, 3).reshape(nb * tp, LRU_WIDTH)


def _scan_fwd(a, u):
    t_len, s, _ = a.shape
    tc = _pick(t_len, 640)
    blk = pl.BlockSpec((tc, s, LANES), lambda i: (i, 0, 0))

    def body(a_ref, u_ref, h_ref, carry):
        @pl.when(pl.program_id(0) == 0)
        def _():
            carry[...] = jnp.zeros_like(carry)

        def step(t, h):
            h = a_ref[t] * h + u_ref[t]
            h_ref[t] = h
            return h

        carry[...] = lax.fori_loop(0, tc, step, carry[...], unroll=8)

    return pl.pallas_call(
        body, out_shape=jax.ShapeDtypeStruct(a.shape, F32), grid=(t_len // tc,), in_specs=[blk, blk], out_specs=blk,
        scratch_shapes=[pltpu.VMEM((s, LANES), F32)], compiler_params=_cparams(1), name="lru_scan")(a, u)


def _scan_bwd(dh, a, h_prev):
    t_len, s, _ = a.shape
    tc = _pick(t_len, 640)
    nb = t_len // tc
    blk = pl.BlockSpec((tc, s, LANES), lambda i: (nb - 1 - i, 0, 0))

    def body(dh_ref, a_ref, hp_ref, du_ref, da_ref, carry):
        @pl.when(pl.program_id(0) == 0)
        def _():
            carry[...] = jnp.zeros_like(carry)

        def step(k, c):
            t = tc - 1 - k
            d = dh_ref[t] + c
            du_ref[t] = d
            da_ref[t] = d * hp_ref[t]
            return a_ref[t] * d

        carry[...] = lax.fori_loop(0, tc, step, carry[...], unroll=8)

    shp = jax.ShapeDtypeStruct(a.shape, F32)
    return pl.pallas_call(
        body, out_shape=(shp, shp), grid=(nb,), in_specs=[blk, blk, blk], out_specs=(blk, blk),
        scratch_shapes=[pltpu.VMEM((s, LANES), F32)], compiler_params=_cparams(1), name="lru_scan_bwd")(dh, a, h_prev)


def _lru_gates(xc, zr, zi, r_b, i_b, lam):
    r = _sigmoid(zr + r_b)
    ig = _sigmoid(zi + i_b)
    sp = _softplus_neg(lam)
    log_a = -LRU_C * r * sp
    a = jnp.exp(log_a)
    mult = jnp.sqrt(-_expm1(2.0 * log_a))
    return r, ig, sp, a, mult


def _even_fwd(x, p, m, tp, nb):
    c = LRU_WIDTH
    h = _rms_fwd(x, p["norm"], "ev_norm")
    u = _matmul(h, p["w_in"], "nn", F32, name="ev_in")

    def pre(rows, prevs, nexts, chans):
        gb, gc, xa, xb = rows
        wa, wb, bias = chans
        pa = gc * xa
        ya = gb * _conv_fwd(pa, prevs[0] * prevs[1], wa)
        xc = _conv_fwd(xb, prevs[2], wb) + bias
        return [ya, xc], []

    ya, xc = _chan_call("ev_pre", pre, m, tp, c, 1, row_ins=[(u, 0), (u, 1), (u, 2), (u, 3)],
                        prev_ins=[(u, 1), (u, 2), (u, 3)],
                        chan_ins=[(p["conv_a"], 0), (p["conv_b"], 0), (p["conv_b_bias"], 0)],
                        row_outs=[(BF16,), (F32,)])
    zr = _matmul(xc, p["gate_r"], "nn", F32, name="ev_gate_r")
    zi = _matmul(xc, p["gate_i"], "nn", F32, name="ev_gate_i")

    def lru_in(rows, prevs, nexts, chans):
        xcv, zrv, ziv = rows
        r, ig, sp, a, mult = _lru_gates(xcv, zrv, ziv, *chans)
        return [a, mult * (ig * xcv)], []

    a, uu = _chan_call("ev_lru_in", lru_in, m, tp, c, 1, row_ins=[(xc, 0), (zr, 0), (zi, 0)],
                       chan_ins=[(p["gate_r_b"], 0), (p["gate_i_b"], 0), (p["lam"], 0)],
                       row_outs=[(F32,), (F32,)])
    a_s = _to_scan(a, nb, tp)
    hs_s = _scan_fwd(a_s, _to_scan(uu, nb, tp))
    hs = _from_scan(hs_s, nb, tp)

    def post(rows, prevs, nexts, chans):
        gate, hv = rows
        return [_gelu(gate) * hv], []

    (yb,) = _chan_call("ev_post", post, m, tp, c, 1, row_ins=[(u, 4), (hs, 0)], row_outs=[(BF16,)])
    out = _matmul(ya, p["w_out_a"], "nn", F32, residual=x, name="ev_out_a")
    out = _matmul(yb, p["w_out_b"], "nn", F32, residual=out, name="ev_out_b")
    return out, (x, h, u, ya, xc, zr, zi, a_s, hs_s, hs, yb)


def _even_bwd(dout, p, saved, m, tp, nb):
    c = LRU_WIDTH
    x, h, u, ya, xc, zr, zi, a_s, hs_s, hs, yb = saved
    dy = _matmul(dout, p["w_out"], "nt", F32, name="ev_out_dx")
    d_w_out = jnp.concatenate([_matmul(ya, dout, "tn", F32, name="ev_out_dw_a"),
                               _matmul(yb, dout, "tn", F32, name="ev_out_dw_b")], axis=0)

    def post_bwd(rows, prevs, nexts, chans):
        dyb, gate, hv = rows
        return [dyb * hv * _gelu_grad(gate), dyb * _gelu(gate)], []

    dgate, dhs = _chan_call("ev_post_bwd", post_bwd, m, tp, c, 1, row_ins=[(dy, 1), (u, 4), (hs, 0)],
                            row_outs=[(F32,), (F32,)])
    h_prev = jnp.concatenate([jnp.zeros_like(hs_s[:1]), hs_s[:-1]], axis=0)
    du_s, da_s = _scan_bwd(_to_scan(dhs, nb, tp), a_s, h_prev)
    du = _from_scan(du_s, nb, tp)
    da = _from_scan(da_s, nb, tp)

    def lru_in_bwd(rows, prevs, nexts, chans):
        duv, dav, xcv, zrv, ziv = rows
        r, ig, sp, a, mult = _lru_gates(xcv, zrv, ziv, *chans)
        dxc = duv * mult * ig
        dig = duv * mult * xcv
        dmult = duv * ig * xcv
        dlog_a = dav * a - dmult * (a * a) / jnp.maximum(mult, 1e-30)
        dr = dlog_a * (-LRU_C * sp)
        dzr = dr * r * (1.0 - r)
        dzi = dig * ig * (1.0 - ig)
        dsp = jnp.sum(dlog_a * (-LRU_C * r), axis=0, keepdims=True)
        dlam = -dsp * _sigmoid(-chans[2])
        return ([dzr, dzi, dxc],
                [jnp.sum(dzr, axis=0, keepdims=True), jnp.sum(dzi, axis=0, keepdims=True), dlam])

    dzr, dzi, dxc, d_r_b, d_i_b, d_lam = _chan_call(
        "ev_lru_in_bwd", lru_in_bwd, m, tp, c, 1, row_ins=[(du, 0), (da, 0), (xc, 0), (zr, 0), (zi, 0)],
        chan_ins=[(p["gate_r_b"], 0), (p["gate_i_b"], 0), (p["lam"], 0)],
        row_outs=[(F32,), (F32,), (F32,)], red_outs=[(1,), (1,), (1,)])
    d_gate_r = _matmul(xc, dzr, "tn", F32, name="ev_gate_r_dw")
    d_gate_i = _matmul(xc, dzi, "tn", F32, name="ev_gate_i_dw")
    dxc = _matmul(dzr, p["gate_r"], "nt", F32, residual=dxc, name="ev_gate_r_dx")
    dxc = _matmul(dzi, p["gate_i"], "nt", F32, residual=dxc, name="ev_gate_i_dx")

    def conv_b_bwd(rows, prevs, nexts, chans):
        dxcv, xb = rows
        return ([_conv_dx(dxcv, nexts[0], chans[0])],
                [_conv_dw(dxcv, xb, prevs[0], 4), jnp.sum(dxcv, axis=0, keepdims=True)])

    dxb, d_conv_b, d_bias = _chan_call(
        "ev_conv_b_bwd", conv_b_bwd, m, tp, c, 1, row_ins=[(dxc, 0), (u, 3)], prev_ins=[(u, 3)], next_ins=[(dxc, 0)],
        chan_ins=[(p["conv_b"], 0)], row_outs=[(F32,)], red_outs=[(SUBLANES,), (1,)])

    def mix_a_bwd(rows, prevs, nexts, chans):
        dya, gb, gc, xa = rows
        (wa,) = chans
        taps = _taps(gc * xa, prevs[0] * prevs[1], 3)
        ca = _conv_taps(taps, wa)
        dca = dya * gb
        dpa = _conv_dx(dca, nexts[0] * nexts[1], wa)
        return [dya * ca, dpa * xa, dpa * gc], [_conv_dw_taps(dca, taps)]

    dgb, dgc, dxa, d_conv_a = _chan_call(
        "ev_mix_a_bwd", mix_a_bwd, m, tp, c, 1, row_ins=[(dy, 0), (u, 0), (u, 1), (u, 2)],
        prev_ins=[(u, 1), (u, 2)], next_ins=[(dy, 0), (u, 0)], chan_ins=[(p["conv_a"], 0)],
        row_outs=[(F32,), (F32,), (F32,)], red_outs=[(SUBLANES,)])
    du_all = jnp.concatenate([dgb, dgc, dxa, dxb, dgate], axis=1)
    d_w_in = _matmul(h, du_all, "tn", F32, name="ev_in_dw")
    dh = _matmul(du_all, p["w_in"], "nt", F32, name="ev_in_dx")
    dx, d_norm = _rms_bwd(x, p["norm"], dh, dout, "ev_norm_bwd")
    return dx, dict(norm=d_norm, w_in=d_w_in, conv_a=d_conv_a[:3], conv_b=d_conv_b[:4], conv_b_bias=d_bias,
                    gate_r=d_gate_r, gate_r_b=d_r_b, gate_i=d_gate_i, gate_i_b=d_i_b, lam=d_lam, w_out=d_w_out)


def _rope_tables(tp):
    pos = jnp.arange(tp, dtype=F32)
    inv_freq = ROPE_BASE ** (-jnp.arange(0, QK_ROPE, 2, dtype=F32) / QK_ROPE)
    ang = pos[:, None] * inv_freq[None, :]
    cos, sin = jnp.cos(ang), jnp.sin(ang)
    half = QK_ROPE // 2
    one = jnp.ones((tp, QK_NOPE), F32)
    z64 = jnp.zeros((tp, QK_NOPE), F32)
    zh = jnp.zeros((tp, half), F32)
    zt = jnp.zeros((tp, HEAD_PAD - QK_HEAD), F32)
    c_tab = jnp.concatenate([one, cos, cos, zt], axis=1)
    s_lo = jnp.concatenate([z64, -sin, zh, zt], axis=1)
    s_hi = jnp.concatenate([z64, zh, sin, zt], axis=1)
    return c_tab, s_lo, s_hi


def _rope(v, c_tab, s_lo, s_hi):
    half = QK_ROPE // 2
    return v * c_tab + pltpu.roll(v, HEAD_PAD - half, 1) * s_lo + pltpu.roll(v, half, 1) * s_hi


def _rope_t(dv, c_tab, s_lo, s_hi):
    half = QK_ROPE // 2
    return dv * c_tab + pltpu.roll(dv * s_lo, half, 1) + pltpu.roll(dv * s_hi, HEAD_PAD - half, 1)


def _rope_call(name, fn, m, tp, ins, tables, out_dtype, shared_pre=None):
    tm = _pick(tp, ROW_TILE)
    tps = tp // tm
    n = len(ins)
    width = MLA_HEADS * HEAD_PAD

    def body(*refs):
        tabs = [r[...] for r in refs[n:n + 3]]
        shared = [None if fc is None else shared_pre(refs[a][...].astype(F32), *tabs) for a, (_, fc) in enumerate(ins)]
        for hh in range(MLA_HEADS):
            lanes = slice(hh * HEAD_PAD, (hh + 1) * HEAD_PAD)
            vals = [refs[a][:, lanes].astype(F32) if shared[a] is None else shared[a] for a in range(n)]
            refs[n + 3][:, lanes] = fn(*vals, *tabs).astype(out_dtype)

    in_specs, args = [], []
    for arr, fixed_col in ins:
        if fixed_col is None:
            in_specs.append(pl.BlockSpec((tm, width), lambda i: (i, 0)))
        else:
            in_specs.append(pl.BlockSpec((tm, HEAD_PAD), lambda i, fc=fixed_col: (i, fc)))
        args.append(arr)
    for tab in tables:
        in_specs.append(pl.BlockSpec((tm, HEAD_PAD), lambda i: (lax.rem(i, tps), 0)))
        args.append(tab)
    return pl.pallas_call(
        body, out_shape=jax.ShapeDtypeStruct((m, width), out_dtype), grid=(m // tm,),
        in_specs=in_specs, out_specs=pl.BlockSpec((tm, width), lambda i: (i, 0)),
        compiler_params=_cparams(1), name=name)(*args)


def _rope_k_bwd(dk, tables, m, tp):
    tm = _pick(tp, ROW_TILE)
    tps = tp // tm

    def body(dk_ref, c_ref, lo_ref, hi_ref, o_ref):
        acc = dk_ref[:, 0:HEAD_PAD].astype(F32)
        for hh in range(1, MLA_HEADS):
            acc = acc + dk_ref[:, hh * HEAD_PAD:(hh + 1) * HEAD_PAD].astype(F32)
        d = pltpu.roll(_rope_t(acc, c_ref[...], lo_ref[...], hi_ref[...]), QK_NOPE, 1)
        lane = lax.broadcasted_iota(jnp.int32, d.shape, 1)
        o_ref[...] = jnp.where(lane < QK_ROPE, d, 0.0)

    tab = pl.BlockSpec((tm, HEAD_PAD), lambda i: (lax.rem(i, tps), 0))
    return pl.pallas_call(
        body, out_shape=jax.ShapeDtypeStruct((m, HEAD_PAD), F32), grid=(m // tm,),
        in_specs=[pl.BlockSpec((tm, MLA_HEADS * HEAD_PAD), lambda i: (i, 0)), tab, tab, tab],
        out_specs=pl.BlockSpec((tm, HEAD_PAD), lambda i: (i, 0)), compiler_params=_cparams(1),
        name="od_rope_k_bwd")(dk, *tables)


def _causal_mask(row0, col0, shape):
    rows = row0 + lax.broadcasted_iota(jnp.int32, shape, 0)
    cols = col0 + lax.broadcasted_iota(jnp.int32, shape, 1)
    return cols <= rows


NT = (((1,), (1,)), ((), ()))
TN = (((0,), (0,)), ((), ()))
LOG2E = math.log2(math.e)
LN2 = math.log(2.0)
HEADS_PER_STEP = 2
HEAD_STEPS = MLA_HEADS // HEADS_PER_STEP
STEP_LANES = HEADS_PER_STEP * HEAD_PAD


def _flash_fwd(q, k, v, nb, tp):
    tq = _pick(tp, ROW_TILE)
    nq = tp // tq

    def body(q_ref, k_ref, v_ref, o_ref, lse_ref):
        i = pl.program_id(2)
        qbs = [q_ref[:, hd * HEAD_PAD:(hd + 1) * HEAD_PAD] for hd in range(HEADS_PER_STEP)]

        def chunk(j, carry, masked, width=1):
            off = pl.multiple_of(j * tq, tq)
            out = []
            for hd in range(HEADS_PER_STEP):
                mx, acc = carry[hd]
                lanes = slice(hd * HEAD_PAD, (hd + 1) * HEAD_PAD)
                kb = k_ref[pl.ds(off, width * tq), lanes]
                vb = v_ref[pl.ds(off, width * tq), lanes]
                ones_lane = lax.broadcasted_iota(jnp.int32, vb.shape, 1) == V_HEAD
                vb = jnp.where(ones_lane, jnp.ones_like(vb), vb)
                s = lax.dot_general(qbs[hd], kb, NT, preferred_element_type=F32)
                if masked:
                    s = jnp.where(_causal_mask(0, 0, s.shape), s, NEG)
                m_new = jnp.maximum(mx, jnp.max(s, axis=1, keepdims=True))
                alpha = jnp.exp2(mx - m_new)
                pr = jnp.exp2(s - m_new)
                acc = alpha * acc + jnp.dot(pr.astype(BF16), vb, preferred_element_type=F32)
                out.append((m_new, acc))
            return tuple(out)

        one = (jnp.full((tq, 1), NEG, F32), jnp.zeros((tq, HEAD_PAD), F32))
        quads = i // 4
        carry = lax.fori_loop(0, quads, lambda jj, c: chunk(4 * jj, c, False, 4), (one,) * HEADS_PER_STEP)
        carry = lax.fori_loop(0, lax.rem(i, 4) // 2, lambda _, c: chunk(4 * quads, c, False, 2), carry)
        carry = lax.fori_loop(0, lax.rem(i, 2), lambda _, c: chunk(i - 1, c, False), carry)
        carry = chunk(i, carry, True)
        for hd in range(HEADS_PER_STEP):
            mx, acc = carry[hd]
            lanes = slice(hd * HEAD_PAD, (hd + 1) * HEAD_PAD)
            l = acc[:, V_HEAD:V_HEAD + 1]
            value_lane = lax.broadcasted_iota(jnp.int32, acc.shape, 1) < V_HEAD
            o_ref[:, lanes] = jnp.where(value_lane, acc / l, 0.0).astype(o_ref.dtype)
            lse_ref[:, lanes] = jnp.broadcast_to(mx + jnp.log2(l), (tq, HEAD_PAD))

    qspec = pl.BlockSpec((tq, STEP_LANES), lambda b, hh, i: (b * nq + i, hh))
    kvspec = pl.BlockSpec((tp, STEP_LANES), lambda b, hh, i: (b, hh))
    shp = (nb * tp, MLA_HEADS * HEAD_PAD)
    return pl.pallas_call(
        body, out_shape=(jax.ShapeDtypeStruct(shp, BF16), jax.ShapeDtypeStruct(shp, F32)),
        grid=(nb, HEAD_STEPS, nq), in_specs=[qspec, kvspec, kvspec], out_specs=(qspec, qspec),
        compiler_params=_cparams(3), name="od_flash_fwd")(q, k, v)


def _flash_prep(o, do, lse_c, nb, tp):
    tq = _pick(tp, ROW_TILE)
    nq = tp // tq

    def body(o_ref, do_ref, lse_ref, lr_ref, dr_ref):
        for hh in range(MLA_HEADS):
            lanes = slice(hh * HEAD_PAD, (hh + 1) * HEAD_PAD)
            delta = jnp.sum(o_ref[:, lanes].astype(F32) * do_ref[:, lanes].astype(F32), axis=1, keepdims=True)
            lr_ref[hh] = jnp.transpose(lse_ref[:, lanes])[0:SUBLANES, :]
            dr_ref[hh] = jnp.transpose(jnp.broadcast_to(delta, (tq, HEAD_PAD)))[0:SUBLANES, :]

    qspec = pl.BlockSpec((tq, MLA_HEADS * HEAD_PAD), lambda b, i: (b * nq + i, 0))
    rspec = pl.BlockSpec((MLA_HEADS, None, SUBLANES, tq), lambda b, i: (b, i, 0, 0))
    rshape = jax.ShapeDtypeStruct((nb * MLA_HEADS, nq, SUBLANES, tq), F32)
    return pl.pallas_call(
        body, out_shape=(rshape, rshape), grid=(nb, nq), in_specs=[qspec, qspec, qspec],
        out_specs=(rspec, rspec), compiler_params=_cparams(2), name="od_flash_prep")(o, do, lse_c)


def _flash_bwd(q, k, v, do, lse_r, delta_r, nb, tp):
    tq = _pick(tp, ROW_TILE)
    nq = tp // tq

    def body(q_ref, k_ref, v_ref, do_ref, lse_ref, dl_ref, dq_ref, dk_ref, dv_ref):
        j = pl.program_id(2)

        @pl.when(j == 0)
        def _():
            dq_ref[...] = jnp.zeros_like(dq_ref)

        kbs = [k_ref[:, hd * HEAD_PAD:(hd + 1) * HEAD_PAD] for hd in range(HEADS_PER_STEP)]
        vbs = [v_ref[:, hd * HEAD_PAD:(hd + 1) * HEAD_PAD] for hd in range(HEADS_PER_STEP)]

        def chunk(i, carry, masked, width=1):
            off = pl.multiple_of(i * tq, tq)
            out = []
            for hd in range(HEADS_PER_STEP):
                dk, dv = carry[hd]
                lanes = slice(hd * HEAD_PAD, (hd + 1) * HEAD_PAD)
                qb = q_ref[pl.ds(off, width * tq), lanes]
                dob = do_ref[pl.ds(off, width * tq), lanes]
                lse = jnp.concatenate([lse_ref[hd, i + w][0:1, :] for w in range(width)], axis=1)
                delta = jnp.concatenate([dl_ref[hd, i + w][0:1, :] for w in range(width)], axis=1)
                st = lax.dot_general(kbs[hd], qb, NT, preferred_element_type=F32)
                pt = jnp.exp2(st - lse)
                if masked:
                    keys = lax.broadcasted_iota(jnp.int32, st.shape, 0)
                    queries = lax.broadcasted_iota(jnp.int32, st.shape, 1)
                    pt = jnp.where(keys <= queries, pt, 0.0)
                dv = dv + jnp.dot(pt.astype(BF16), dob, preferred_element_type=F32)
                dpt = lax.dot_general(vbs[hd], dob, NT, preferred_element_type=F32)
                dst = (pt * (dpt - delta)).astype(BF16)
                dk = dk + jnp.dot(dst, qb, preferred_element_type=F32)
                dq_ref[pl.ds(off, width * tq), lanes] += lax.dot_general(dst, kbs[hd], TN,
                                                                         preferred_element_type=F32)
                out.append((dk, dv))
            return tuple(out)

        zero = jnp.zeros((tq, HEAD_PAD), F32)
        wide = jnp.minimum(nq - 1 - j, 1)
        first = j + 1 + wide
        rest = nq - first
        carry = ((zero, zero),) * HEADS_PER_STEP
        carry = lax.fori_loop(0, wide, lambda _, c: chunk(j, c, True, 2), carry)
        carry = lax.fori_loop(0, 1 - wide, lambda _, c: chunk(j, c, True), carry)
        carry = lax.fori_loop(0, rest // 4, lambda pp, c: chunk(first + 4 * pp, c, False, 4), carry)
        carry = lax.fori_loop(0, lax.rem(rest, 4) // 2, lambda _, c: chunk(first + 4 * (rest // 4), c, False, 2), carry)
        carry = lax.fori_loop(0, lax.rem(rest, 2), lambda _, c: chunk(nq - 1, c, False), carry)
        for hd in range(HEADS_PER_STEP):
            lanes = slice(hd * HEAD_PAD, (hd + 1) * HEAD_PAD)
            dk_ref[:, lanes] = carry[hd][0] * LN2
            dv_ref[:, lanes] = carry[hd][1].astype(dv_ref.dtype)

    tspec = pl.BlockSpec((tq, STEP_LANES), lambda b, hh, j: (b * nq + j, hh))
    fullspec = pl.BlockSpec((tp, STEP_LANES), lambda b, hh, j: (b, hh))
    rspec = pl.BlockSpec((HEADS_PER_STEP, nq, SUBLANES, tq), lambda b, hh, j: (b * HEAD_STEPS + hh, 0, 0, 0))
    shp = (nb * tp, MLA_HEADS * HEAD_PAD)
    return pl.pallas_call(
        body, out_shape=(jax.ShapeDtypeStruct(shp, F32), jax.ShapeDtypeStruct(shp, F32),
                         jax.ShapeDtypeStruct(shp, BF16)),
        grid=(nb, HEAD_STEPS, nq), in_specs=[fullspec, tspec, tspec, fullspec, rspec, rspec],
        out_specs=(fullspec, tspec, tspec), compiler_params=_cparams(3),
        name="od_flash_bwd")(q, k, v, do, lse_r, delta_r)


def _odd_fwd(x, p, tables, m, tp, nb):
    scale = QK_HEAD ** -0.5
    h = _rms_fwd(x, p["norm"], "od_norm")
    u = _matmul(h, p["w_in"], "nn", F32, name="od_in")
    cq = u[:, :Q_LORA]
    ckv = u[:, Q_LORA:Q_LORA + KV_LORA]
    cqn = _rms_fwd(cq, p["q_norm"], "od_q_norm")
    ckvn = _rms_fwd(ckv, p["kv_norm"], "od_kv_norm")
    q_raw = _matmul(cqn, p["w_uq"], "nn", F32, name="od_uq")
    k_raw = _matmul(ckvn, p["w_uk"], "nn", F32, name="od_uk")
    v = _matmul(ckvn, p["w_uv"], "nn", BF16, name="od_uv")
    q = _rope_call("od_rope_q", lambda qv, c, lo, hi: _rope(qv, c, lo, hi) * (scale * LOG2E), m, tp,
                   [(q_raw, None)], tables, BF16)
    kr_col = (Q_LORA + KV_LORA) // HEAD_PAD
    k = _rope_call("od_rope_k", lambda kv, kr, c, lo, hi: kv + kr, m, tp, [(k_raw, None), (u, kr_col)], tables, BF16,
                   shared_pre=lambda uv, c, lo, hi: _rope(pltpu.roll(uv, QK_NOPE, 1), c, lo, hi))
    o, lse_c = _flash_fwd(q, k, v, nb, tp)
    out = _matmul(o, p["w_out"], "nn", F32, residual=x, name="od_out")
    return out, (x, h, cq, ckv, cqn, ckvn, q, k, v, o, lse_c)


def _odd_bwd(dout, p, tables, saved, m, tp, nb):
    scale = QK_HEAD ** -0.5
    x, h, cq, ckv, cqn, ckvn, q, k, v, o, lse_c = saved
    do = _matmul(dout, p["w_out"], "nt", BF16, name="od_out_dx")
    d_w_out = _matmul(o, dout, "tn", F32, name="od_out_dw")
    lse_r, delta_r = _flash_prep(o, do, lse_c, nb, tp)
    dq, dk, dv = _flash_bwd(q, k, v, do, lse_r, delta_r, nb, tp)
    dq_raw = _rope_call("od_rope_q_bwd", lambda d, c, lo, hi: _rope_t(d, c, lo, hi) * scale, m, tp, [(dq, None)],
                        tables, BF16)
    dkr = _rope_k_bwd(dk, tables, m, tp)
    d_w_uq = _matmul(cqn, dq_raw, "tn", F32, name="od_uq_dw")
    d_w_uk = _matmul(ckvn, dk, "tn", F32, name="od_uk_dw")
    d_w_uv = _matmul(ckvn, dv, "tn", F32, name="od_uv_dw")
    dcqn = _matmul(dq_raw, p["w_uq"], "nt", F32, name="od_uq_dx")
    dckvn = _matmul(dk, p["w_uk"], "nt", F32, name="od_uk_dx")
    dckvn = _matmul(dv, p["w_uv"], "nt", F32, residual=dckvn, name="od_uv_dx")
    dcq, d_q_norm = _rms_bwd(cq, p["q_norm"], dcqn, None, "od_q_norm_bwd")
    dckv, d_kv_norm = _rms_bwd(ckv, p["kv_norm"], dckvn, None, "od_kv_norm_bwd")
    du = jnp.concatenate([dcq, dckv, dkr], axis=1)
    d_w_in = _matmul(h, du, "tn", F32, name="od_in_dw")
    dh = _matmul(du, p["w_in"], "nt", F32, name="od_in_dx")
    dx, d_norm = _rms_bwd(x, p["norm"], dh, dout, "od_norm_bwd")
    return dx, dict(norm=d_norm, w_in=d_w_in, q_norm=d_q_norm, kv_norm=d_kv_norm, w_uq=d_w_uq, w_uk=d_w_uk,
                    w_uv=d_w_uv, w_out=d_w_out)


def _loss_head(hf, g, target, tp, t_real):
    m, c = hf.shape
    tm = _pick(tp, ROW_TILE)
    tps = tp // tm

    def body(x_ref, g_ref, t_ref, dx_ref, dg_ref, loss_ref):
        i = pl.program_id(0)
        xf = x_ref[...]
        r = lax.rsqrt(jnp.mean(xf * xf, axis=-1, keepdims=True) + EPS)
        xn = xf * r
        t_pos = lax.rem(i, tps) * tm + lax.broadcasted_iota(jnp.int32, (tm, 1), 0)
        valid = jnp.logical_and(t_pos >= N_META, t_pos < t_real)
        err = jnp.where(valid, xn * g_ref[...] - t_ref[...], 0.0)
        dyf = err * (1.0 / c)
        dyg = dyf * g_ref[...]
        dx_ref[...] = r * (dyg - xn * jnp.mean(dyg * xn, axis=-1, keepdims=True))

        @pl.when(i == 0)
        def _():
            dg_ref[...] = jnp.zeros_like(dg_ref)
            loss_ref[...] = jnp.zeros_like(loss_ref)

        dg_ref[...] += jnp.sum(dyf * xn, axis=0, keepdims=True)
        loss_ref[...] += (0.5 / c) * jnp.sum(jnp.sum(err * err, axis=1, keepdims=True), axis=0, keepdims=True)

    row = pl.BlockSpec((tm, c), lambda i: (i, 0))
    vec = pl.BlockSpec((1, c), lambda i: (0, 0))
    return pl.pallas_call(
        body, out_shape=(jax.ShapeDtypeStruct((m, c), F32), jax.ShapeDtypeStruct((1, c), F32),
                         jax.ShapeDtypeStruct((1, 1), F32)),
        grid=(m // tm,), in_specs=[row, vec, row], out_specs=(row, vec, pl.BlockSpec((1, 1), lambda i: (0, 0))),
        compiler_params=_cparams(1), name="loss_head")(hf, g, target)


def _meta_grad(dh0, nb, tp):
    d = dh0.shape[1]

    def body(x_ref, o_ref):
        @pl.when(pl.program_id(0) == 0)
        def _():
            o_ref[...] = jnp.zeros_like(o_ref)

        o_ref[...] += x_ref[...]

    return pl.pallas_call(
        body, out_shape=jax.ShapeDtypeStruct((N_META, d), F32), grid=(nb,),
        in_specs=[pl.BlockSpec((N_META, d), lambda b: (b * (tp // N_META), 0))],
        out_specs=pl.BlockSpec((N_META, d), lambda b: (0, 0)), compiler_params=_cparams(1), name="meta_grad")(dh0)


def _mesh_pos():
    x, y, c = lax.axis_index("x"), lax.axis_index("y"), lax.axis_index("c")
    return x, y, c


N_CHIP = 4
MESH_ID = pl.DeviceIdType.MESH


def _peer_chip(x, y, k):
    px = 1 - x if k & 2 else x
    py = 1 - y if k & 1 else y
    return px, py


def _all_gather(arrays):
    n = len(arrays)

    def body(*refs):
        srcs, outs = refs[:n], refs[n:2 * n]
        send_sems, recv_sems, local_sems = refs[2 * n:]
        x, y, c = _mesh_pos()
        me = 4 * x + 2 * y + c
        sibling = (x, y, 1 - c)

        def copy(a, sem, src, block, to):
            return pltpu.make_async_remote_copy(
                src_ref=src, dst_ref=outs[a].at[block], send_sem=send_sems.at[a, sem], recv_sem=recv_sems.at[a, sem],
                device_id=to, device_id_type=MESH_ID)

        local = [pltpu.make_async_copy(srcs[a], outs[a].at[me], local_sems.at[a]) for a in range(n)]
        for cp in local:
            cp.start()
        sends = [copy(a, 0, srcs[a], me, sibling) for a in range(n)]
        for k in range(1, N_CHIP):
            px, py = _peer_chip(x, y, k)
            sends += [copy(a, k, srcs[a], me, (px, py, c)) for a in range(n)]
        for cp in sends:
            cp.start()
        for k in range(1, N_CHIP):
            px, py = _peer_chip(x, y, k)
            block = 4 * px + 2 * py + c
            for a in range(n):
                copy(a, k, srcs[a], block, sibling).wait_recv()
            passed = [copy(a, N_CHIP - 1 + k, outs[a].at[block], block, sibling) for a in range(n)]
            for cp in passed:
                cp.start()
            sends += passed
        for a in range(n):
            copy(a, 0, srcs[a], 4 * x + 2 * y + (1 - c), sibling).wait_recv()
        for k in range(1, N_CHIP):
            px, py = _peer_chip(x, y, k)
            for a in range(n):
                copy(a, N_CHIP - 1 + k, srcs[a], 4 * px + 2 * py + (1 - c), sibling).wait_recv()
        for cp in sends:
            cp.wait_send()
        for cp in local:
            cp.wait()

    any_spec = pl.BlockSpec(memory_space=pl.ANY)
    out_shape = tuple(jax.ShapeDtypeStruct((N_DEV,) + a.shape, a.dtype) for a in arrays)
    return pl.pallas_call(
        body, out_shape=out_shape, in_specs=[any_spec] * n, out_specs=(any_spec,) * n,
        scratch_shapes=[pltpu.SemaphoreType.DMA((n, N_DEV - 1)), pltpu.SemaphoreType.DMA((n, N_DEV - 1)),
                        pltpu.SemaphoreType.DMA((n,))],
        name="weight_all_gather")(*arrays)


def _pair_exchange(arrays):
    n = len(arrays)

    def body(*refs):
        srcs, outs = refs[:n], refs[n:2 * n]
        send_sems, recv_sems = refs[2 * n:]
        x, y, c = _mesh_pos()
        copies = [pltpu.make_async_remote_copy(
            src_ref=srcs[a], dst_ref=outs[a], send_sem=send_sems.at[a], recv_sem=recv_sems.at[a],
            device_id=(x, y, 1 - c), device_id_type=MESH_ID) for a in range(n)]
        for cp in copies:
            cp.start()
        for cp in copies:
            cp.wait()

    any_spec = pl.BlockSpec(memory_space=pl.ANY)
    return pl.pallas_call(
        body, out_shape=tuple(jax.ShapeDtypeStruct(a.shape, a.dtype) for a in arrays), in_specs=[any_spec] * n,
        out_specs=(any_spec,) * n, scratch_shapes=[pltpu.SemaphoreType.DMA((n,)), pltpu.SemaphoreType.DMA((n,))],
        name="grad_pair_exchange")(*arrays)


def _chip_exchange(arrays):
    n = len(arrays)

    def body(*refs):
        srcs, outs = refs[:n], refs[n:2 * n]
        send_sems, recv_sems, local_sems = refs[2 * n:]
        x, y, c = _mesh_pos()
        q = 2 * x + y
        local = [pltpu.make_async_copy(srcs[a].at[q], outs[a].at[q], local_sems.at[a]) for a in range(n)]
        for cp in local:
            cp.start()

        def copy(a, k, to_q, from_q, px, py):
            return pltpu.make_async_remote_copy(
                src_ref=srcs[a].at[to_q], dst_ref=outs[a].at[from_q], send_sem=send_sems.at[a, k - 1],
                recv_sem=recv_sems.at[a, k - 1], device_id=(px, py, c), device_id_type=MESH_ID)

        sends = []
        for k in range(1, N_CHIP):
            px, py = _peer_chip(x, y, k)
            sends += [copy(a, k, 2 * px + py, q, px, py) for a in range(n)]
        for cp in sends:
            cp.start()
        for k in range(1, N_CHIP):
            px, py = _peer_chip(x, y, k)
            for a in range(n):
                copy(a, k, q, 2 * px + py, px, py).wait_recv()
        for cp in sends:
            cp.wait_send()
        for cp in local:
            cp.wait()

    any_spec = pl.BlockSpec(memory_space=pl.ANY)
    return pl.pallas_call(
        body, out_shape=tuple(jax.ShapeDtypeStruct(a.shape, a.dtype) for a in arrays), in_specs=[any_spec] * n,
        out_specs=(any_spec,) * n,
        scratch_shapes=[pltpu.SemaphoreType.DMA((n, N_CHIP - 1)), pltpu.SemaphoreType.DMA((n, N_CHIP - 1)),
                        pltpu.SemaphoreType.DMA((n,))],
        name="grad_chip_exchange")(*arrays)


REDUCE_BLOCK_BYTES = 512 * 1024


def _pair_add(a, b):
    p, r, c = a.shape
    tr = _reduce_rows(r, c)

    def body(a_ref, b_ref, o_ref):
        o_ref[...] = (a_ref[...].astype(F32) + b_ref[...].astype(F32)).astype(o_ref.dtype)

    blk = pl.BlockSpec((None, tr, c), lambda s, i: (s, i, 0))
    return pl.pallas_call(
        body, out_shape=jax.ShapeDtypeStruct(a.shape, a.dtype), grid=(p, r // tr), in_specs=[blk, blk], out_specs=blk,
        compiler_params=_cparams(2), name="grad_pair_add")(a, b)


def _reduce_rows(r, c):
    best = None
    for t in range(16, r + 1, 16):
        if r % t == 0 and t * c * 4 <= REDUCE_BLOCK_BYTES:
            best = t
    assert best is not None, (r, c)
    return best


def _reduce_adamw(parts, w, mom, vel):
    n_parts, r, c = parts.shape
    tr = _reduce_rows(r, c)
    c1 = 1.0 - ADAM_B1 ** ADAM_STEP
    c2 = 1.0 - ADAM_B2 ** ADAM_STEP

    def body(p_ref, w_ref, m_ref, v_ref, g_ref, d_ref, mo_ref, vo_ref):
        g = p_ref[0].astype(F32)
        for s in range(1, n_parts):
            g = g + p_ref[s].astype(F32)
        mn = ADAM_B1 * m_ref[...] + (1.0 - ADAM_B1) * g
        vn = ADAM_B2 * v_ref[...] + (1.0 - ADAM_B2) * (g * g)
        m_hat = mn / c1
        v_hat = vn / c2
        g_ref[...] = g
        d_ref[...] = -ADAM_LR * (m_hat / (jnp.sqrt(v_hat) + ADAM_EPS) + ADAM_WD * w_ref[...])
        mo_ref[...] = mn
        vo_ref[...] = vn

    blk = pl.BlockSpec((tr, c), lambda i: (i, 0))
    shp = jax.ShapeDtypeStruct((r, c), F32)
    return pl.pallas_call(
        body, out_shape=(shp, shp, shp, shp), grid=(r // tr,),
        in_specs=[pl.BlockSpec((n_parts, tr, c), lambda i: (0, i, 0)), blk, blk, blk], out_specs=(blk, blk, blk, blk),
        compiler_params=_cparams(1), name="reduce_adamw")(parts, w, mom, vel)


def _pack_rows(pieces, width, row_multiple, dtype):
    flat = jnp.concatenate([p.astype(dtype).reshape(-1) for p in pieces])
    rows = -(-flat.shape[0] // (width * row_multiple)) * row_multiple
    return jnp.pad(flat, (0, rows * width - flat.shape[0])).reshape(rows, width)


def _unshard(gathered, axis):
    moved = jnp.moveaxis(gathered, 0, axis)
    shape = list(moved.shape)
    shape[axis:axis + 2] = [shape[axis] * shape[axis + 1]]
    return moved.reshape(shape)


def _to_slots(full, axis):
    shape = list(full.shape)
    shape[axis:axis + 1] = [N_DEV, shape[axis] // N_DEV]
    return jnp.moveaxis(full.reshape(shape), axis, 0)


def _core_slots(full, axis, core):
    shape = list(full.shape)
    shape[axis:axis + 1] = [N_CHIP, 2, shape[axis] // N_DEV]
    picked = lax.dynamic_index_in_dim(full.reshape(shape), core, axis + 1, keepdims=False)
    return jnp.moveaxis(picked, axis, 0)


def _block_diag(w):
    hh, d, _ = w.shape
    eye = jnp.eye(hh, dtype=w.dtype)
    return (w[:, :, None, :] * eye[:, None, :, None]).reshape(hh * d, hh * d)


def _block_diag_t(full, hh):
    d = full.shape[0] // hh
    f4 = full.reshape(hh, d, hh, d)
    return jnp.stack([f4[i, :, i, :] for i in range(hh)], axis=0)


def _pad_heads(w, width):
    r = w.shape[0]
    w3 = w.reshape(r, MLA_HEADS, width)
    return jnp.pad(w3, ((0, 0), (0, 0), (0, HEAD_PAD - width))).reshape(r, MLA_HEADS * HEAD_PAD)


def _unpad_heads(w, width):
    r = w.shape[0]
    return w.reshape(r, MLA_HEADS, HEAD_PAD)[:, :, :width].reshape(r, MLA_HEADS * width)


def kernel(x, meta_tokens, ev_norm, ev_w_in, ev_conv_a, ev_conv_b, ev_conv_b_bias, ev_gate_r_w, ev_gate_r_b, ev_gate_i_w, ev_gate_i_b, ev_lru_lambda, ev_w_out, od_norm, od_w_in, od_q_norm, od_kv_norm, od_w_uq, od_w_ukv, od_w_out, ffn_norm, ffn_w_up, ffn_conv_w, ffn_conv_b, ffn_w_down, final_norm, loss_target, m_meta_tokens, m_ev_norm, m_ev_w_in, m_ev_conv_a, m_ev_conv_b, m_ev_conv_b_bias, m_ev_gate_r_w, m_ev_gate_r_b, m_ev_gate_i_w, m_ev_gate_i_b, m_ev_lru_lambda, m_ev_w_out, m_od_norm, m_od_w_in, m_od_q_norm, m_od_kv_norm, m_od_w_uq, m_od_w_ukv, m_od_w_out, m_ffn_norm, m_ffn_w_up, m_ffn_conv_w, m_ffn_conv_b, m_ffn_w_down, m_final_norm, v_meta_tokens, v_ev_norm, v_ev_w_in, v_ev_conv_a, v_ev_conv_b, v_ev_conv_b_bias, v_ev_gate_r_w, v_ev_gate_r_b, v_ev_gate_i_w, v_ev_gate_i_b, v_ev_lru_lambda, v_ev_w_out, v_od_norm, v_od_w_in, v_od_q_norm, v_od_kv_norm, v_od_w_uq, v_od_w_ukv, v_od_w_out, v_ffn_norm, v_ffn_w_up, v_ffn_conv_w, v_ffn_conv_b, v_ffn_w_down, v_final_norm):
    given = dict(locals())
    names = [n for n, _ in PARAMS]
    axis_of = dict(PARAMS)
    w_loc = {n: given[n] for n in names}
    m_loc = {n: given["m_" + n] for n in names}
    v_loc = {n: given["v_" + n] for n in names}
    sharded = [n for n in names if axis_of[n] is not None]
    replicated = [n for n in names if axis_of[n] is None]
    small = [n for n in sharded if n not in BIG]

    nb, seq, d = x.shape
    t_real = N_META + seq
    tp = -(-t_real // ROW_TILE) * ROW_TILE
    m = nb * tp

    small_pack = _pack_rows([w_loc[n] for n in small], LANES, SUBLANES, F32)
    gathered = _all_gather([w_loc[n].astype(BF16) for n in BIG] + [small_pack])
    full = {n: w_loc[n] for n in replicated}
    for n, g in zip(BIG, gathered[:-1]):
        full[n] = _unshard(g, axis_of[n])
    flat = gathered[-1].reshape(N_DEV, -1)
    off = 0
    for n in small:
        shard = w_loc[n].shape
        size = math.prod(shard)
        full[n] = _unshard(flat[:, off:off + size].reshape((N_DEV,) + shard), axis_of[n])
        off += size

    tables = _rope_tables(tp)

    def even_params(j):
        w_out = full["ev_w_out"][j]
        return dict(norm=full["ev_norm"][j][None], w_in=full["ev_w_in"][j], conv_a=full["ev_conv_a"][j],
                    conv_b=full["ev_conv_b"][j], conv_b_bias=full["ev_conv_b_bias"][j][None],
                    gate_r=_block_diag(full["ev_gate_r_w"][j]).astype(BF16),
                    gate_i=_block_diag(full["ev_gate_i_w"][j]).astype(BF16),
                    gate_r_b=full["ev_gate_r_b"][j][None], gate_i_b=full["ev_gate_i_b"][j][None],
                    lam=full["ev_lru_lambda"][j][None], w_out=w_out, w_out_a=w_out[:LRU_WIDTH],
                    w_out_b=w_out[LRU_WIDTH:])

    def odd_params(j):
        w_ukv = full["od_w_ukv"][j].reshape(KV_LORA, MLA_HEADS, QK_NOPE + V_HEAD)
        w_uk = w_ukv[:, :, :QK_NOPE].reshape(KV_LORA, MLA_HEADS * QK_NOPE)
        w_uv = w_ukv[:, :, QK_NOPE:].reshape(KV_LORA, MLA_HEADS * V_HEAD)
        w_out = full["od_w_out"][j].reshape(MLA_HEADS, V_HEAD, d)
        w_out = jnp.pad(w_out, ((0, 0), (0, HEAD_PAD - V_HEAD), (0, 0))).reshape(MLA_HEADS * HEAD_PAD, d)
        return dict(norm=full["od_norm"][j][None], w_in=jnp.pad(full["od_w_in"][j], ((0, 0), (0, ODD_IN_PAD - ODD_IN))),
                    q_norm=full["od_q_norm"][j][None], kv_norm=full["od_kv_norm"][j][None],
                    w_uq=_pad_heads(full["od_w_uq"][j], QK_HEAD), w_uk=_pad_heads(w_uk, QK_NOPE),
                    w_uv=_pad_heads(w_uv, V_HEAD), w_out=w_out)

    def ffn_params(layer):
        w_up = full["ffn_w_up"][layer]
        return dict(norm=full["ffn_norm"][layer][None], w_up=w_up, conv_w=full["ffn_conv_w"][layer],
                    conv_b=full["ffn_conv_b"][layer][None], w_down=full["ffn_w_down"][layer])

    meta = jnp.broadcast_to(full["meta_tokens"][None], (nb, N_META, d))
    h0 = jnp.concatenate([meta, x, jnp.zeros((nb, tp - t_real, d), F32)], axis=1).reshape(m, d)
    hcur = h0
    tape = []
    for layer in range(4):
        j = layer // 2
        if layer % 2 == 0:
            mp = even_params(j)
            hcur, saved = _even_fwd(hcur, mp, m, tp, nb)
        else:
            mp = odd_params(j)
            hcur, saved = _odd_fwd(hcur, mp, tables, m, tp, nb)
        fp = ffn_params(layer)
        hcur, fsaved = _ffn_fwd(hcur, fp, m, tp)
        tape.append((mp, saved, fp, fsaved))

    target = jnp.pad(loss_target, ((0, 0), (N_META, tp - t_real), (0, 0))).reshape(m, d)
    dh, d_final_norm, loss_part = _loss_head(hcur, full["final_norm"][None], target, tp, t_real)

    grads = {"final_norm": d_final_norm[0]}
    ev_g, od_g, ffn_g = [None, None], [None, None], [None] * 4
    for layer in reversed(range(4)):
        mp, saved, fp, fsaved = tape[layer]
        dh, ffn_g[layer] = _ffn_bwd(dh, fp, fsaved, m, tp)
        if layer % 2 == 0:
            dh, ev_g[layer // 2] = _even_bwd(dh, mp, saved, m, tp, nb)
        else:
            dh, od_g[layer // 2] = _odd_bwd(dh, mp, tables, saved, m, tp, nb)

    dh3 = dh.reshape(nb, tp, d)
    grad_x = dh3[:, N_META:t_real]
    grads["meta_tokens"] = _meta_grad(dh, nb, tp)

    def stack(lst, key, fn=lambda a: a):
        return jnp.stack([fn(g[key]) for g in lst], axis=0)

    grads["ev_norm"] = stack(ev_g, "norm", lambda a: a[0])
    grads["ev_w_in"] = stack(ev_g, "w_in")
    grads["ev_conv_a"] = stack(ev_g, "conv_a")
    grads["ev_conv_b"] = stack(ev_g, "conv_b")
    grads["ev_conv_b_bias"] = stack(ev_g, "conv_b_bias", lambda a: a[0])
    grads["ev_gate_r_w"] = stack(ev_g, "gate_r", lambda a: _block_diag_t(a, 8))
    grads["ev_gate_r_b"] = stack(ev_g, "gate_r_b", lambda a: a[0])
    grads["ev_gate_i_w"] = stack(ev_g, "gate_i", lambda a: _block_diag_t(a, 8))
    grads["ev_gate_i_b"] = stack(ev_g, "gate_i_b", lambda a: a[0])
    grads["ev_lru_lambda"] = stack(ev_g, "lam", lambda a: a[0])
    grads["ev_w_out"] = stack(ev_g, "w_out")
    grads["od_norm"] = stack(od_g, "norm", lambda a: a[0])
    grads["od_w_in"] = stack(od_g, "w_in", lambda a: a[:, :ODD_IN])
    grads["od_q_norm"] = stack(od_g, "q_norm", lambda a: a[0])
    grads["od_kv_norm"] = stack(od_g, "kv_norm", lambda a: a[0])
    grads["od_w_uq"] = stack(od_g, "w_uq", lambda a: _unpad_heads(a, QK_HEAD))

    def ukv(g):
        gk = g["w_uk"].reshape(KV_LORA, MLA_HEADS, HEAD_PAD)[:, :, :QK_NOPE]
        gv = g["w_uv"].reshape(KV_LORA, MLA_HEADS, HEAD_PAD)[:, :, :V_HEAD]
        return jnp.concatenate([gk, gv], axis=2).reshape(KV_LORA, MLA_HEADS * (QK_NOPE + V_HEAD))

    grads["od_w_ukv"] = jnp.stack([ukv(g) for g in od_g], axis=0)
    grads["od_w_out"] = stack(od_g, "w_out", lambda a: a.reshape(MLA_HEADS, HEAD_PAD, d)[:, :V_HEAD].reshape(-1, d))
    grads["ffn_norm"] = stack(ffn_g, "norm", lambda a: a[0])
    grads["ffn_w_up"] = stack(ffn_g, "w_up")
    grads["ffn_conv_w"] = stack(ffn_g, "conv_w")
    grads["ffn_conv_b"] = stack(ffn_g, "conv_b", lambda a: a[0])
    grads["ffn_w_down"] = stack(ffn_g, "w_down")

    order = small + replicated
    slot_parts = [_to_slots(grads[n], axis_of[n]).reshape(N_DEV, -1) for n in small]
    slot_parts += [jnp.broadcast_to(grads[n].reshape(1, -1), (N_DEV, grads[n].size)) for n in replicated]
    slot_parts.append(jnp.broadcast_to(loss_part, (N_DEV, 1)))
    g_flat = jnp.concatenate(slot_parts, axis=1)
    n_flat = g_flat.shape[1]
    rows = -(-n_flat // (1024 * 128)) * 128
    g_small = jnp.pad(g_flat, ((0, 0), (0, rows * 1024 - n_flat))).reshape(N_DEV, rows, 1024)

    def rows_of(n):
        shard = w_loc[n].shape
        return (math.prod(shard[:-1]), shard[-1])

    core = lax.axis_index("c")

    def core_slots(n, which):
        return _core_slots(grads[n], axis_of[n], which).astype(BF16).reshape((N_CHIP,) + rows_of(n))

    small_by_core = jnp.swapaxes(g_small.reshape((N_CHIP, 2) + g_small.shape[1:]), 0, 1)
    mine = [core_slots(n, core) for n in BIG] + [lax.dynamic_index_in_dim(small_by_core, core, 0, keepdims=False)]
    theirs = [core_slots(n, 1 - core) for n in BIG]
    theirs.append(lax.dynamic_index_in_dim(small_by_core, 1 - core, 0, keepdims=False))
    from_sibling = _pair_exchange(theirs)
    parts = _chip_exchange([_pair_add(a, b) for a, b in zip(mine, from_sibling)])

    g_out, d_out, m_out, v_out = {}, {}, {}, {}
    for n, part in zip(BIG, parts[:-1]):
        res = _reduce_adamw(part, *[t[n].reshape(rows_of(n)) for t in (w_loc, m_loc, v_loc)])
        for out, r in zip((g_out, d_out, m_out, v_out), res):
            out[n] = r.reshape(w_loc[n].shape)

    def flat_local(tree):
        flat = jnp.concatenate([tree[n].reshape(-1) for n in order])
        return jnp.pad(flat, (0, rows * 1024 - flat.shape[0])).reshape(rows, 1024)

    res = _reduce_adamw(parts[-1], flat_local(w_loc), flat_local(m_loc), flat_local(v_loc))
    loss = res[0].reshape(-1)[n_flat - 1]
    for out, r in zip((g_out, d_out, m_out, v_out), res):
        flat = r.reshape(-1)
        off = 0
        for n in order:
            size = w_loc[n].size
            out[n] = flat[off:off + size].reshape(w_loc[n].shape)
            off += size
    return (loss, grad_x, *[g_out[n] for n in names], *[d_out[n] for n in names], *[m_out[n] for n in names],
            *[v_out[n] for n in names])
```

```python
import math

import jax
import jax.numpy as jnp
from jax import lax
from jax.experimental import pallas as pl
from jax.experimental.pallas import tpu as pltpu

F32 = jnp.float32
BF16 = jnp.bfloat16

N_DEV = 8
N_META = 16
EPS = 1e-6
LRU_C = 8.0
MLA_HEADS = 16
QK_NOPE = 64
QK_ROPE = 32
QK_HEAD = QK_NOPE + QK_ROPE
V_HEAD = 64
HEAD_PAD = 128
Q_LORA = 384
KV_LORA = 256
ODD_IN = Q_LORA + KV_LORA + QK_ROPE
ODD_IN_PAD = 768
ROPE_BASE = 10000.0
LRU_WIDTH = 512
D_FF = 2816

ADAM_LR = 0.001
ADAM_B1 = 0.9
ADAM_B2 = 0.999
ADAM_EPS = 1e-08
ADAM_WD = 0.01
ADAM_STEP = 10

ROW_TILE = 384
SUBLANES = 8
HALO_ROWS = 16
LANES = 128
VMEM_LIMIT = 48 * 1024 * 1024
NEG = -1e30

PARAMS = (
    ("meta_tokens", 1), ("ev_norm", None), ("ev_w_in", 2), ("ev_conv_a", 2), ("ev_conv_b", 2),
    ("ev_conv_b_bias", None), ("ev_gate_r_w", None), ("ev_gate_r_b", None), ("ev_gate_i_w", None),
    ("ev_gate_i_b", None), ("ev_lru_lambda", None), ("ev_w_out", 1), ("od_norm", 1), ("od_w_in", 1),
    ("od_q_norm", 1), ("od_kv_norm", 1), ("od_w_uq", 2), ("od_w_ukv", 2), ("od_w_out", 1),
    ("ffn_norm", None), ("ffn_w_up", 2), ("ffn_conv_w", 2), ("ffn_conv_b", None), ("ffn_w_down", 1),
    ("final_norm", None),
)
BIG = ("ev_w_in", "ev_w_out", "od_w_in", "od_w_uq", "od_w_ukv", "od_w_out", "ffn_w_up", "ffn_w_down")


def _cparams(n_grid):
    return pltpu.CompilerParams(dimension_semantics=("arbitrary",) * n_grid, vmem_limit_bytes=VMEM_LIMIT)


def _pick(dim, target):
    if dim <= target:
        return dim
    best = None
    for t in range(LANES, target + 1, LANES):
        if dim % t == 0:
            best = t
    assert best is not None, (dim, target)
    return best


MATMUL_VMEM_BUDGET = 38 * 1024 * 1024
HBM_BYTES_PER_US = 3.0e6
MXU_FLOPS_PER_US = 9.0e8
ACC_BYTES_PER_US = 7.6e6
GRID_STEP_US = 0.35


def _tile_candidates(dim):
    return [t for t in range(LANES, dim + 1, LANES) if dim % t == 0] or [dim]


def _matmul_tiles(m, n, k, sa, sb, so, sr, transposed_lhs):
    best, best_cost = None, None
    for tm in _tile_candidates(m):
        for tn in _tile_candidates(n):
            for tk in _tile_candidates(k):
                nk = k // tk
                vmem = 2 * (tm * tk * sa + tk * tn * sb) + tm * tn * ((4 if nk > 1 else 0) + 2 * so + 2 * sr)
                vmem += (tm * tk * 2 if sa > 2 else 0) + (tk * tn * 2 if sb > 2 else 0) + tm * tn * 4
                if vmem > MATMUL_VMEM_BUDGET:
                    continue
                steps = (m // tm) * (n // tn) * nk
                traffic = m * k * sa * (n // tn) + k * n * sb * (m // tm) + m * n * (so + sr)
                acc_us = steps * tm * tn * 4 / ACC_BYTES_PER_US if nk > 1 else 0.0
                busy_us = 0.0 if transposed_lhs else 2.0 * m * n * k / MXU_FLOPS_PER_US + acc_us
                cost = max(traffic / HBM_BYTES_PER_US, busy_us) + steps * GRID_STEP_US
                if best_cost is None or cost < best_cost:
                    best, best_cost = (tm, tn, tk), cost
    assert best is not None, (m, n, k)
    return best


def _matmul(a, b, mode, out_dtype=F32, residual=None, name="mm", b_col_off=0):
    if mode == "nn":
        (m, k), (k2, n) = a.shape, b.shape
    elif mode == "nt":
        (m, k), n = a.shape, b.shape[0]
        k2 = k if b_col_off or b.shape[1] > k else b.shape[1]
    else:
        (k, m), (k2, n) = a.shape, b.shape
    assert k == k2, (a.shape, b.shape, mode)
    tm, tn, tk = _matmul_tiles(m, n, k, a.dtype.itemsize, b.dtype.itemsize, jnp.dtype(out_dtype).itemsize,
                               0 if residual is None else residual.dtype.itemsize, mode == "tn")
    nk = k // tk
    if mode == "tn":
        a_spec = pl.BlockSpec((tk, tm), lambda i, j, kk: (kk, i))
        dims = (((0,), (0,)), ((), ()))
    else:
        a_spec = pl.BlockSpec((tm, tk), lambda i, j, kk: (i, kk))
        dims = (((1,), (1 if mode == "nt" else 0,)), ((), ()))
    if mode == "nt":
        assert b_col_off % tk == 0, (b_col_off, tk)
        b_spec = pl.BlockSpec((tn, tk), lambda i, j, kk: (j, kk + b_col_off // tk))
    else:
        b_spec = pl.BlockSpec((tk, tn), lambda i, j, kk: (kk, j))
    o_spec = pl.BlockSpec((tm, tn), lambda i, j, kk: (i, j))
    has_res = residual is not None

    def body(*refs):
        a_ref, b_ref = refs[:2]
        r_ref = refs[2] if has_res else None
        o_ref = refs[3] if has_res else refs[2]
        part = lax.dot_general(a_ref[...].astype(BF16), b_ref[...].astype(BF16), dims, preferred_element_type=F32)

        def finish(out):
            if has_res:
                out = out + r_ref[...].astype(F32)
            o_ref[...] = out.astype(o_ref.dtype)

        if nk == 1:
            finish(part)
            return
        acc_ref = refs[-1]
        kk = pl.program_id(2)

        @pl.when(kk == 0)
        def _():
            acc_ref[...] = part

        @pl.when(kk > 0)
        def _():
            acc_ref[...] += part

        @pl.when(kk == nk - 1)
        def _():
            finish(acc_ref[...])

    in_specs = [a_spec, b_spec] + ([o_spec] if has_res else [])
    args = (a, b) + ((residual,) if has_res else ())
    return pl.pallas_call(
        body, out_shape=jax.ShapeDtypeStruct((m, n), out_dtype), grid=(m // tm, n // tn, nk),
        in_specs=in_specs, out_specs=o_spec, scratch_shapes=[pltpu.VMEM((tm, tn), F32)] if nk > 1 else [],
        compiler_params=_cparams(3), name=name)(*args)


def _rms_fwd(x, g, name, transposed=False):
    m, c = x.shape
    tm = _pick(m, ROW_TILE)

    def body(x_ref, g_ref, o_ref, *t_ref):
        xf = x_ref[...].astype(F32)
        r = lax.rsqrt(jnp.mean(xf * xf, axis=-1, keepdims=True) + EPS)
        y = xf * r * g_ref[...]
        o_ref[...] = y.astype(o_ref.dtype)
        if transposed:
            t_ref[0][...] = jnp.transpose(y).astype(BF16)

    out_shape = [jax.ShapeDtypeStruct((m, c), BF16)]
    out_specs = [pl.BlockSpec((tm, c), lambda i: (i, 0))]
    if transposed:
        out_shape.append(jax.ShapeDtypeStruct((c, m), BF16))
        out_specs.append(pl.BlockSpec((c, tm), lambda i: (0, i)))
    res = pl.pallas_call(
        body, out_shape=tuple(out_shape), grid=(m // tm,),
        in_specs=[pl.BlockSpec((tm, c), lambda i: (i, 0)), pl.BlockSpec((1, c), lambda i: (0, 0))],
        out_specs=tuple(out_specs), compiler_params=_cparams(1), name=name)(x, g)
    return res if transposed else res[0]


def _rms_bwd(x, g, dy, residual, name):
    m, c = x.shape
    tm = _pick(m, ROW_TILE)
    has_res = residual is not None

    def body(*refs):
        if has_res:
            x_ref, g_ref, dy_ref, r_ref, dx_ref, dg_ref = refs
        else:
            x_ref, g_ref, dy_ref, dx_ref, dg_ref = refs
        xf = x_ref[...].astype(F32)
        dyf = dy_ref[...].astype(F32)
        r = lax.rsqrt(jnp.mean(xf * xf, axis=-1, keepdims=True) + EPS)
        xn = xf * r
        dyg = dyf * g_ref[...]
        dx = r * (dyg - xn * jnp.mean(dyg * xn, axis=-1, keepdims=True))
        if has_res:
            dx = dx + r_ref[...]
        dx_ref[...] = dx

        @pl.when(pl.program_id(0) == 0)
        def _():
            dg_ref[...] = jnp.zeros_like(dg_ref)

        dg_ref[...] += jnp.sum(dyf * xn, axis=0, keepdims=True)

    row = pl.BlockSpec((tm, c), lambda i: (i, 0))
    vec = pl.BlockSpec((1, c), lambda i: (0, 0))
    in_specs = [row, vec, row] + ([row] if has_res else [])
    args = (x, g, dy) + ((residual,) if has_res else ())
    return pl.pallas_call(
        body, out_shape=(jax.ShapeDtypeStruct((m, c), F32), jax.ShapeDtypeStruct((1, c), F32)), grid=(m // tm,),
        in_specs=in_specs, out_specs=(row, vec), compiler_params=_cparams(1), name=name)(*args)


def _chan_call(name, fn, m, tp, tc, ncol, row_ins=(), prev_ins=(), next_ins=(), chan_ins=(), row_outs=(),
               red_outs=(), row_split=1):
    tm = _pick(tp, ROW_TILE) // row_split
    tps = tp // tm
    nrow = m // tm
    halo_blocks = tm // HALO_ROWS
    last_halo = m // HALO_ROWS - 1
    n_in = len(row_ins) + len(prev_ins) + len(next_ins) + len(chan_ins)
    n_r, n_p, n_n = len(row_ins), len(prev_ins), len(next_ins)

    def body(*refs):
        i = pl.program_id(1)
        pos = lax.rem(i, tps)
        at_start = pos == 0
        at_end = pos == tps - 1
        rows = [r[...].astype(F32) for r in refs[:n_r]]
        prevs = [jnp.where(at_start, 0.0, r[...].astype(F32)[SUBLANES:]) for r in refs[n_r:n_r + n_p]]
        nexts = [jnp.where(at_end, 0.0, r[...].astype(F32)[:SUBLANES]) for r in refs[n_r + n_p:n_r + n_p + n_n]]
        chans = [r[...] for r in refs[n_r + n_p + n_n:n_in]]
        out_refs = refs[n_in:n_in + len(row_outs)]
        red_refs = refs[n_in + len(row_outs):]
        row_vals, red_vals = fn(rows, prevs, nexts, chans)
        for ref, val in zip(out_refs, row_vals):
            ref[...] = val.astype(ref.dtype)
        if red_refs:
            @pl.when(i == 0)
            def _():
                for ref in red_refs:
                    ref[...] = jnp.zeros_like(ref)

            for ref, val in zip(red_refs, red_vals):
                ref[...] += val

    in_specs, args = [], []
    for arr, off in row_ins:
        in_specs.append(pl.BlockSpec((tm, tc), lambda j, i, off=off: (i, j + off)))
        args.append(arr)
    for arr, off in prev_ins:
        in_specs.append(pl.BlockSpec((HALO_ROWS, tc),
                                     lambda j, i, off=off: (jnp.maximum(i * halo_blocks - 1, 0), j + off)))
        args.append(arr)
    for arr, off in next_ins:
        in_specs.append(pl.BlockSpec((HALO_ROWS, tc),
                                     lambda j, i, off=off: (jnp.minimum((i + 1) * halo_blocks, last_halo), j + off)))
        args.append(arr)
    for arr, off in chan_ins:
        in_specs.append(pl.BlockSpec((arr.shape[0], tc), lambda j, i, off=off: (0, j + off)))
        args.append(arr)
    out_shape, out_specs = [], []
    for (dt,) in row_outs:
        out_shape.append(jax.ShapeDtypeStruct((m, ncol * tc), dt))
        out_specs.append(pl.BlockSpec((tm, tc), lambda j, i: (i, j)))
    for (k,) in red_outs:
        out_shape.append(jax.ShapeDtypeStruct((k, ncol * tc), F32))
        out_specs.append(pl.BlockSpec((k, tc), lambda j, i: (0, j)))
    return pl.pallas_call(
        body, out_shape=tuple(out_shape), grid=(ncol, nrow), in_specs=in_specs, out_specs=tuple(out_specs),
        compiler_params=_cparams(2), name=name)(*args)


def _shift_down(x, prev8, s):
    if s == 0:
        return x
    tm, tc = x.shape
    groups = tm // SUBLANES
    xr = pltpu.roll(x.reshape(groups, SUBLANES, tc), s, 1)
    before = jnp.concatenate([pltpu.roll(prev8, s, 0)[None], xr[:-1]], axis=0)
    rid = lax.broadcasted_iota(jnp.int32, xr.shape, 1)
    return jnp.where(rid < s, before, xr).reshape(tm, tc)


def _shift_up(x, next8, s):
    if s == 0:
        return x
    tm, tc = x.shape
    groups = tm // SUBLANES
    xr = pltpu.roll(x.reshape(groups, SUBLANES, tc), SUBLANES - s, 1)
    after = jnp.concatenate([xr[1:], pltpu.roll(next8, SUBLANES - s, 0)[None]], axis=0)
    rid = lax.broadcasted_iota(jnp.int32, xr.shape, 1)
    return jnp.where(rid >= SUBLANES - s, after, xr).reshape(tm, tc)


def _taps(x, prev8, kw):
    return [_shift_down(x, prev8, kw - 1 - k) for k in range(kw)]


def _conv_taps(taps, w):
    y = w[0:1, :] * taps[0]
    for k in range(1, len(taps)):
        y = y + w[k:k + 1, :] * taps[k]
    return y


def _conv_dw_taps(dy, taps):
    shape = (SUBLANES, dy.shape[1])
    rid = lax.broadcasted_iota(jnp.int32, shape, 0)
    out = jnp.zeros(shape, F32)
    for k, tap in enumerate(taps):
        out = out + jnp.where(rid == k, jnp.sum(dy * tap, axis=0, keepdims=True), 0.0)
    return out


def _conv_fwd(x, prev8, w):
    return _conv_taps(_taps(x, prev8, w.shape[0]), w)


def _conv_dw(dy, x, prev8, kw):
    return _conv_dw_taps(dy, _taps(x, prev8, kw))


def _conv_dx(dy, next8, w):
    kw = w.shape[0]
    dx = w[kw - 1:kw, :] * dy
    for k in range(kw - 1):
        dx = dx + w[k:k + 1, :] * _shift_up(dy, next8, kw - 1 - k)
    return dx


def _sigmoid(x):
    return 1.0 / (1.0 + jnp.exp(-x))


def _sigmoid_tanh(x):
    return 0.5 + 0.5 * jnp.tanh(0.5 * x)


def _expm1(x):
    series = x * (1.0 + x * 0.5 * (1.0 + x * (1.0 / 3.0) * (1.0 + x * 0.25 * (1.0 + x * 0.2))))
    return jnp.where(jnp.abs(x) < 0.3, series, jnp.exp(x) - 1.0)


def _softplus_neg(lam):
    e = jnp.exp(-jnp.abs(lam))
    log1p = jnp.where(e < 1e-2, e * (1.0 - e * (0.5 - e * (1.0 / 3.0))), jnp.log(1.0 + e))
    return jnp.maximum(-lam, 0.0) + log1p


GELU_C = math.sqrt(2.0 / math.pi)


def _gelu(x):
    return 0.5 * x * (1.0 + jnp.tanh(GELU_C * (x + 0.044715 * x * x * x)))


def _gelu_grad(x):
    t = jnp.tanh(GELU_C * (x + 0.044715 * x * x * x))
    return 0.5 * (1.0 + t) + 0.5 * x * (1.0 - t * t) * GELU_C * (1.0 + 3.0 * 0.044715 * x * x)


FFN_COL_TILE = 1408


def _ffn_fwd(x, p, m, tp):
    h, h_t = _rms_fwd(x, p["norm"], "ffn_norm", transposed=True)
    u = _matmul(h, p["w_up"], "nn", BF16, name="ffn_up")
    tc = FFN_COL_TILE
    ncol = D_FF // tc

    def gate(rows, prevs, nexts, chans):
        ua, ug = rows
        wa, wg, ba, bg = chans
        a = _conv_fwd(ua, prevs[0], wa) + ba
        g = _conv_fwd(ug, prevs[1], wg) + bg
        return [a * _sigmoid_tanh(a) * g, a, g], []

    z, a_act, g_act = _chan_call(
        "ffn_gate", gate, m, tp, tc, ncol, row_ins=[(u, 0), (u, ncol)], prev_ins=[(u, 0), (u, ncol)],
        chan_ins=[(p["conv_w"], 0), (p["conv_w"], ncol), (p["conv_b"], 0), (p["conv_b"], ncol)],
        row_outs=[(BF16,), (BF16,), (BF16,)])
    out = _matmul(z, p["w_down"], "nn", F32, residual=x, name="ffn_down")
    return out, (x, h_t, u, z, a_act, g_act)


def _ffn_bwd(dout, p, saved, m, tp):
    x, h_t, u, z, a_act, g_act = saved
    tc = FFN_COL_TILE
    ncol = D_FF // tc
    dz = _matmul(dout, p["w_down"], "nt", F32, name="ffn_down_dx")
    d_w_down = _matmul(z, dout, "tn", F32, name="ffn_down_dw")

    def act_bwd(a, g, dzv):
        sg = _sigmoid_tanh(a)
        return dzv * g * (sg * (1.0 + a * (1.0 - sg))), dzv * a * sg

    def gate_bwd(rows, prevs, nexts, chans):
        ua, ug, dzv, a, g = rows
        da, dg = act_bwd(a, g, dzv)
        da_next, dg_next = act_bwd(nexts[1], nexts[2], nexts[0])
        ups_a = [_shift_up(da, da_next, 2 - k) for k in range(3)]
        ups_g = [_shift_up(dg, dg_next, 2 - k) for k in range(3)]
        return ([_conv_taps(ups_a, chans[0]), _conv_taps(ups_g, chans[1])],
                [_conv_dw_taps(ua, ups_a), _conv_dw_taps(ug, ups_g),
                 jnp.sum(da, axis=0, keepdims=True), jnp.sum(dg, axis=0, keepdims=True)])

    dua, dug, dcw_a, dcw_g, dcb_a, dcb_g = _chan_call(
        "ffn_gate_bwd", gate_bwd, m, tp, tc, ncol,
        row_ins=[(u, 0), (u, ncol), (dz, 0), (a_act, 0), (g_act, 0)], next_ins=[(dz, 0), (a_act, 0), (g_act, 0)],
        chan_ins=[(p["conv_w"], 0), (p["conv_w"], ncol)],
        row_outs=[(BF16,), (BF16,)], red_outs=[(SUBLANES,), (SUBLANES,), (1,), (1,)], row_split=2)
    d_w_up = jnp.concatenate([_matmul(h_t, dua, "nn", F32, name="ffn_up_dw_a"),
                              _matmul(h_t, dug, "nn", F32, name="ffn_up_dw_g")], axis=1)
    dh = _matmul(dua, p["w_up"], "nt", F32, name="ffn_up_dx_a")
    dh = _matmul(dug, p["w_up"], "nt", F32, residual=dh, name="ffn_up_dx_g", b_col_off=D_FF)
    dx, d_norm = _rms_bwd(x, p["norm"], dh, dout, "ffn_norm_bwd")
    d_conv_w = jnp.concatenate([dcw_a[:3], dcw_g[:3]], axis=1)
    d_conv_b = jnp.concatenate([dcb_a, dcb_g], axis=1)
    return dx, dict(norm=d_norm, w_up=d_w_up, conv_w=d_conv_w, conv_b=d_conv_b, w_down=d_w_down)


def _to_scan(x, nb, tp):
    return x.reshape(nb, tp, LRU_WIDTH // LANES, LANES).transpose(1, 0, 2, 3).reshape(tp, -1, LANES)


def _from_scan(x, nb, tp):
    return x.reshape(tp, nb, LRU_WIDTH // LANES, LANES).transpose(1, 0, 2, 3).reshape(nb * tp, LRU_WIDTH)


def _scan_fwd(a, u):
    t_len, s, _ = a.shape
    tc = _pick(t_len, 640)
    blk = pl.BlockSpec((tc, s, LANES), lambda i: (i, 0, 0))

    def body(a_ref, u_ref, h_ref, carry):
        @pl.when(pl.program_id(0) == 0)
        def _():
            carry[...] = jnp.zeros_like(carry)

        def step(t, h):
            h = a_ref[t] * h + u_ref[t]
            h_ref[t] = h
            return h

        carry[...] = lax.fori_loop(0, tc, step, carry[...], unroll=8)

    return pl.pallas_call(
        body, out_shape=jax.ShapeDtypeStruct(a.shape, F32), grid=(t_len // tc,), in_specs=[blk, blk], out_specs=blk,
        scratch_shapes=[pltpu.VMEM((s, LANES), F32)], compiler_params=_cparams(1), name="lru_scan")(a, u)


def _scan_bwd(dh, a, h_prev):
    t_len, s, _ = a.shape
    tc = _pick(t_len, 640)
    nb = t_len // tc
    blk = pl.BlockSpec((tc, s, LANES), lambda i: (nb - 1 - i, 0, 0))

    def body(dh_ref, a_ref, hp_ref, du_ref, da_ref, carry):
        @pl.when(pl.program_id(0) == 0)
        def _():
            carry[...] = jnp.zeros_like(carry)

        def step(k, c):
            t = tc - 1 - k
            d = dh_ref[t] + c
            du_ref[t] = d
            da_ref[t] = d * hp_ref[t]
            return a_ref[t] * d

        carry[...] = lax.fori_loop(0, tc, step, carry[...], unroll=8)

    shp = jax.ShapeDtypeStruct(a.shape, F32)
    return pl.pallas_call(
        body, out_shape=(shp, shp), grid=(nb,), in_specs=[blk, blk, blk], out_specs=(blk, blk),
        scratch_shapes=[pltpu.VMEM((s, LANES), F32)], compiler_params=_cparams(1), name="lru_scan_bwd")(dh, a, h_prev)


def _lru_gates(xc, zr, zi, r_b, i_b, lam):
    r = _sigmoid(zr + r_b)
    ig = _sigmoid(zi + i_b)
    sp = _softplus_neg(lam)
    log_a = -LRU_C * r * sp
    a = jnp.exp(log_a)
    mult = jnp.sqrt(-_expm1(2.0 * log_a))
    return r, ig, sp, a, mult


def _even_fwd(x, p, m, tp, nb):
    c = LRU_WIDTH
    h = _rms_fwd(x, p["norm"], "ev_norm")
    u = _matmul(h, p["w_in"], "nn", F32, name="ev_in")

    def pre(rows, prevs, nexts, chans):
        gb, gc, xa, xb = rows
        wa, wb, bias = chans
        pa = gc * xa
        ya = gb * _conv_fwd(pa, prevs[0] * prevs[1], wa)
        xc = _conv_fwd(xb, prevs[2], wb) + bias
        return [ya, xc], []

    ya, xc = _chan_call("ev_pre", pre, m, tp, c, 1, row_ins=[(u, 0), (u, 1), (u, 2), (u, 3)],
                        prev_ins=[(u, 1), (u, 2), (u, 3)],
                        chan_ins=[(p["conv_a"], 0), (p["conv_b"], 0), (p["conv_b_bias"], 0)],
                        row_outs=[(BF16,), (F32,)])
    zr = _matmul(xc, p["gate_r"], "nn", F32, name="ev_gate_r")
    zi = _matmul(xc, p["gate_i"], "nn", F32, name="ev_gate_i")

    def lru_in(rows, prevs, nexts, chans):
        xcv, zrv, ziv = rows
        r, ig, sp, a, mult = _lru_gates(xcv, zrv, ziv, *chans)
        return [a, mult * (ig * xcv)], []

    a, uu = _chan_call("ev_lru_in", lru_in, m, tp, c, 1, row_ins=[(xc, 0), (zr, 0), (zi, 0)],
                       chan_ins=[(p["gate_r_b"], 0), (p["gate_i_b"], 0), (p["lam"], 0)],
                       row_outs=[(F32,), (F32,)])
    a_s = _to_scan(a, nb, tp)
    hs_s = _scan_fwd(a_s, _to_scan(uu, nb, tp))
    hs = _from_scan(hs_s, nb, tp)

    def post(rows, prevs, nexts, chans):
        gate, hv = rows
        return [_gelu(gate) * hv], []

    (yb,) = _chan_call("ev_post", post, m, tp, c, 1, row_ins=[(u, 4), (hs, 0)], row_outs=[(BF16,)])
    out = _matmul(ya, p["w_out_a"], "nn", F32, residual=x, name="ev_out_a")
    out = _matmul(yb, p["w_out_b"], "nn", F32, residual=out, name="ev_out_b")
    return out, (x, h, u, ya, xc, zr, zi, a_s, hs_s, hs, yb)


def _even_bwd(dout, p, saved, m, tp, nb):
    c = LRU_WIDTH
    x, h, u, ya, xc, zr, zi, a_s, hs_s, hs, yb = saved
    dy = _matmul(dout, p["w_out"], "nt", F32, name="ev_out_dx")
    d_w_out = jnp.concatenate([_matmul(ya, dout, "tn", F32, name="ev_out_dw_a"),
                               _matmul(yb, dout, "tn", F32, name="ev_out_dw_b")], axis=0)

    def post_bwd(rows, prevs, nexts, chans):
        dyb, gate, hv = rows
        return [dyb * hv * _gelu_grad(gate), dyb * _gelu(gate)], []

    dgate, dhs = _chan_call("ev_post_bwd", post_bwd, m, tp, c, 1, row_ins=[(dy, 1), (u, 4), (hs, 0)],
                            row_outs=[(F32,), (F32,)])
    h_prev = jnp.concatenate([jnp.zeros_like(hs_s[:1]), hs_s[:-1]], axis=0)
    du_s, da_s = _scan_bwd(_to_scan(dhs, nb, tp), a_s, h_prev)
    du = _from_scan(du_s, nb, tp)
    da = _from_scan(da_s, nb, tp)

    def lru_in_bwd(rows, prevs, nexts, chans):
        duv, dav, xcv, zrv, ziv = rows
        r, ig, sp, a, mult = _lru_gates(xcv, zrv, ziv, *chans)
        dxc = duv * mult * ig
        dig = duv * mult * xcv
        dmult = duv * ig * xcv
        dlog_a = dav * a - dmult * (a * a) / jnp.maximum(mult, 1e-30)
        dr = dlog_a * (-LRU_C * sp)
        dzr = dr * r * (1.0 - r)
        dzi = dig * ig * (1.0 - ig)
        dsp = jnp.sum(dlog_a * (-LRU_C * r), axis=0, keepdims=True)
        dlam = -dsp * _sigmoid(-chans[2])
        return ([dzr, dzi, dxc],
                [jnp.sum(dzr, axis=0, keepdims=True), jnp.sum(dzi, axis=0, keepdims=True), dlam])

    dzr, dzi, dxc, d_r_b, d_i_b, d_lam = _chan_call(
        "ev_lru_in_bwd", lru_in_bwd, m, tp, c, 1, row_ins=[(du, 0), (da, 0), (xc, 0), (zr, 0), (zi, 0)],
        chan_ins=[(p["gate_r_b"], 0), (p["gate_i_b"], 0), (p["lam"], 0)],
        row_outs=[(F32,), (F32,), (F32,)], red_outs=[(1,), (1,), (1,)])
    d_gate_r = _matmul(xc, dzr, "tn", F32, name="ev_gate_r_dw")
    d_gate_i = _matmul(xc, dzi, "tn", F32, name="ev_gate_i_dw")
    dxc = _matmul(dzr, p["gate_r"], "nt", F32, residual=dxc, name="ev_gate_r_dx")
    dxc = _matmul(dzi, p["gate_i"], "nt", F32, residual=dxc, name="ev_gate_i_dx")

    def conv_b_bwd(rows, prevs, nexts, chans):
        dxcv, xb = rows
        return ([_conv_dx(dxcv, nexts[0], chans[0])],
                [_conv_dw(dxcv, xb, prevs[0], 4), jnp.sum(dxcv, axis=0, keepdims=True)])

    dxb, d_conv_b, d_bias = _chan_call(
        "ev_conv_b_bwd", conv_b_bwd, m, tp, c, 1, row_ins=[(dxc, 0), (u, 3)], prev_ins=[(u, 3)], next_ins=[(dxc, 0)],
        chan_ins=[(p["conv_b"], 0)], row_outs=[(F32,)], red_outs=[(SUBLANES,), (1,)])

    def mix_a_bwd(rows, prevs, nexts, chans):
        dya, gb, gc, xa = rows
        (wa,) = chans
        taps = _taps(gc * xa, prevs[0] * prevs[1], 3)
        ca = _conv_taps(taps, wa)
        dca = dya * gb
        dpa = _conv_dx(dca, nexts[0] * nexts[1], wa)
        return [dya * ca, dpa * xa, dpa * gc], [_conv_dw_taps(dca, taps)]

    dgb, dgc, dxa, d_conv_a = _chan_call(
        "ev_mix_a_bwd", mix_a_bwd, m, tp, c, 1, row_ins=[(dy, 0), (u, 0), (u, 1), (u, 2)],
        prev_ins=[(u, 1), (u, 2)], next_ins=[(dy, 0), (u, 0)], chan_ins=[(p["conv_a"], 0)],
        row_outs=[(F32,), (F32,), (F32,)], red_outs=[(SUBLANES,)])
    du_all = jnp.concatenate([dgb, dgc, dxa, dxb, dgate], axis=1)
    d_w_in = _matmul(h, du_all, "tn", F32, name="ev_in_dw")
    dh = _matmul(du_all, p["w_in"], "nt", F32, name="ev_in_dx")
    dx, d_norm = _rms_bwd(x, p["norm"], dh, dout, "ev_norm_bwd")
    return dx, dict(norm=d_norm, w_in=d_w_in, conv_a=d_conv_a[:3], conv_b=d_conv_b[:4], conv_b_bias=d_bias,
                    gate_r=d_gate_r, gate_r_b=d_r_b, gate_i=d_gate_i, gate_i_b=d_i_b, lam=d_lam, w_out=d_w_out)


def _rope_tables(tp):
    pos = jnp.arange(tp, dtype=F32)
    inv_freq = ROPE_BASE ** (-jnp.arange(0, QK_ROPE, 2, dtype=F32) / QK_ROPE)
    ang = pos[:, None] * inv_freq[None, :]
    cos, sin = jnp.cos(ang), jnp.sin(ang)
    half = QK_ROPE // 2
    one = jnp.ones((tp, QK_NOPE), F32)
    z64 = jnp.zeros((tp, QK_NOPE), F32)
    zh = jnp.zeros((tp, half), F32)
    zt = jnp.zeros((tp, HEAD_PAD - QK_HEAD), F32)
    c_tab = jnp.concatenate([one, cos, cos, zt], axis=1)
    s_lo = jnp.concatenate([z64, -sin, zh, zt], axis=1)
    s_hi = jnp.concatenate([z64, zh, sin, zt], axis=1)
    return c_tab, s_lo, s_hi


def _rope(v, c_tab, s_lo, s_hi):
    half = QK_ROPE // 2
    return v * c_tab + pltpu.roll(v, HEAD_PAD - half, 1) * s_lo + pltpu.roll(v, half, 1) * s_hi


def _rope_t(dv, c_tab, s_lo, s_hi):
    half = QK_ROPE // 2
    return dv * c_tab + pltpu.roll(dv * s_lo, half, 1) + pltpu.roll(dv * s_hi, HEAD_PAD - half, 1)


def _rope_call(name, fn, m, tp, ins, tables, out_dtype, shared_pre=None):
    tm = _pick(tp, ROW_TILE)
    tps = tp // tm
    n = len(ins)
    width = MLA_HEADS * HEAD_PAD

    def body(*refs):
        tabs = [r[...] for r in refs[n:n + 3]]
        shared = [None if fc is None else shared_pre(refs[a][...].astype(F32), *tabs) for a, (_, fc) in enumerate(ins)]
        for hh in range(MLA_HEADS):
            lanes = slice(hh * HEAD_PAD, (hh + 1) * HEAD_PAD)
            vals = [refs[a][:, lanes].astype(F32) if shared[a] is None else shared[a] for a in range(n)]
            refs[n + 3][:, lanes] = fn(*vals, *tabs).astype(out_dtype)

    in_specs, args = [], []
    for arr, fixed_col in ins:
        if fixed_col is None:
            in_specs.append(pl.BlockSpec((tm, width), lambda i: (i, 0)))
        else:
            in_specs.append(pl.BlockSpec((tm, HEAD_PAD), lambda i, fc=fixed_col: (i, fc)))
        args.append(arr)
    for tab in tables:
        in_specs.append(pl.BlockSpec((tm, HEAD_PAD), lambda i: (lax.rem(i, tps), 0)))
        args.append(tab)
    return pl.pallas_call(
        body, out_shape=jax.ShapeDtypeStruct((m, width), out_dtype), grid=(m // tm,),
        in_specs=in_specs, out_specs=pl.BlockSpec((tm, width), lambda i: (i, 0)),
        compiler_params=_cparams(1), name=name)(*args)


def _rope_k_bwd(dk, tables, m, tp):
    tm = _pick(tp, ROW_TILE)
    tps = tp // tm

    def body(dk_ref, c_ref, lo_ref, hi_ref, o_ref):
        acc = dk_ref[:, 0:HEAD_PAD].astype(F32)
        for hh in range(1, MLA_HEADS):
            acc = acc + dk_ref[:, hh * HEAD_PAD:(hh + 1) * HEAD_PAD].astype(F32)
        d = pltpu.roll(_rope_t(acc, c_ref[...], lo_ref[...], hi_ref[...]), QK_NOPE, 1)
        lane = lax.broadcasted_iota(jnp.int32, d.shape, 1)
        o_ref[...] = jnp.where(lane < QK_ROPE, d, 0.0)

    tab = pl.BlockSpec((tm, HEAD_PAD), lambda i: (lax.rem(i, tps), 0))
    return pl.pallas_call(
        body, out_shape=jax.ShapeDtypeStruct((m, HEAD_PAD), F32), grid=(m // tm,),
        in_specs=[pl.BlockSpec((tm, MLA_HEADS * HEAD_PAD), lambda i: (i, 0)), tab, tab, tab],
        out_specs=pl.BlockSpec((tm, HEAD_PAD), lambda i: (i, 0)), compiler_params=_cparams(1),
        name="od_rope_k_bwd")(dk, *tables)


def _causal_mask(row0, col0, shape):
    rows = row0 + lax.broadcasted_iota(jnp.int32, shape, 0)
    cols = col0 + lax.broadcasted_iota(jnp.int32, shape, 1)
    return cols <= rows


NT = (((1,), (1,)), ((), ()))
TN = (((0,), (0,)), ((), ()))
LOG2E = math.log2(math.e)
LN2 = math.log(2.0)
HEADS_PER_STEP = 2
HEAD_STEPS = MLA_HEADS // HEADS_PER_STEP
STEP_LANES = HEADS_PER_STEP * HEAD_PAD


def _flash_fwd(q, k, v, nb, tp):
    tq = _pick(tp, ROW_TILE)
    nq = tp // tq

    def body(q_ref, k_ref, v_ref, o_ref, lse_ref):
        i = pl.program_id(2)
        qbs = [q_ref[:, hd * HEAD_PAD:(hd + 1) * HEAD_PAD] for hd in range(HEADS_PER_STEP)]

        def chunk(j, carry, masked, width=1):
            off = pl.multiple_of(j * tq, tq)
            out = []
            for hd in range(HEADS_PER_STEP):
                mx, acc = carry[hd]
                lanes = slice(hd * HEAD_PAD, (hd + 1) * HEAD_PAD)
                kb = k_ref[pl.ds(off, width * tq), lanes]
                vb = v_ref[pl.ds(off, width * tq), lanes]
                ones_lane = lax.broadcasted_iota(jnp.int32, vb.shape, 1) == V_HEAD
                vb = jnp.where(ones_lane, jnp.ones_like(vb), vb)
                s = lax.dot_general(qbs[hd], kb, NT, preferred_element_type=F32)
                if masked:
                    s = jnp.where(_causal_mask(0, 0, s.shape), s, NEG)
                m_new = jnp.maximum(mx, jnp.max(s, axis=1, keepdims=True))
                alpha = jnp.exp2(mx - m_new)
                pr = jnp.exp2(s - m_new)
                acc = alpha * acc + jnp.dot(pr.astype(BF16), vb, preferred_element_type=F32)
                out.append((m_new, acc))
            return tuple(out)

        one = (jnp.full((tq, 1), NEG, F32), jnp.zeros((tq, HEAD_PAD), F32))
        quads = i // 4
        carry = lax.fori_loop(0, quads, lambda jj, c: chunk(4 * jj, c, False, 4), (one,) * HEADS_PER_STEP)
        carry = lax.fori_loop(0, lax.rem(i, 4) // 2, lambda _, c: chunk(4 * quads, c, False, 2), carry)
        carry = lax.fori_loop(0, lax.rem(i, 2), lambda _, c: chunk(i - 1, c, False), carry)
        carry = chunk(i, carry, True)
        for hd in range(HEADS_PER_STEP):
            mx, acc = carry[hd]
            lanes = slice(hd * HEAD_PAD, (hd + 1) * HEAD_PAD)
            l = acc[:, V_HEAD:V_HEAD + 1]
            value_lane = lax.broadcasted_iota(jnp.int32, acc.shape, 1) < V_HEAD
            o_ref[:, lanes] = jnp.where(value_lane, acc / l, 0.0).astype(o_ref.dtype)
            lse_ref[:, lanes] = jnp.broadcast_to(mx + jnp.log2(l), (tq, HEAD_PAD))

    qspec = pl.BlockSpec((tq, STEP_LANES), lambda b, hh, i: (b * nq + i, hh))
    kvspec = pl.BlockSpec((tp, STEP_LANES), lambda b, hh, i: (b, hh))
    shp = (nb * tp, MLA_HEADS * HEAD_PAD)
    return pl.pallas_call(
        body, out_shape=(jax.ShapeDtypeStruct(shp, BF16), jax.ShapeDtypeStruct(shp, F32)),
        grid=(nb, HEAD_STEPS, nq), in_specs=[qspec, kvspec, kvspec], out_specs=(qspec, qspec),
        compiler_params=_cparams(3), name="od_flash_fwd")(q, k, v)


def _flash_prep(o, do, lse_c, nb, tp):
    tq = _pick(tp, ROW_TILE)
    nq = tp // tq

    def body(o_ref, do_ref, lse_ref, lr_ref, dr_ref):
        for hh in range(MLA_HEADS):
            lanes = slice(hh * HEAD_PAD, (hh + 1) * HEAD_PAD)
            delta = jnp.sum(o_ref[:, lanes].astype(F32) * do_ref[:, lanes].astype(F32), axis=1, keepdims=True)
            lr_ref[hh] = jnp.transpose(lse_ref[:, lanes])[0:SUBLANES, :]
            dr_ref[hh] = jnp.transpose(jnp.broadcast_to(delta, (tq, HEAD_PAD)))[0:SUBLANES, :]

    qspec = pl.BlockSpec((tq, MLA_HEADS * HEAD_PAD), lambda b, i: (b * nq + i, 0))
    rspec = pl.BlockSpec((MLA_HEADS, None, SUBLANES, tq), lambda b, i: (b, i, 0, 0))
    rshape = jax.ShapeDtypeStruct((nb * MLA_HEADS, nq, SUBLANES, tq), F32)
    return pl.pallas_call(
        body, out_shape=(rshape, rshape), grid=(nb, nq), in_specs=[qspec, qspec, qspec],
        out_specs=(rspec, rspec), compiler_params=_cparams(2), name="od_flash_prep")(o, do, lse_c)


def _flash_bwd(q, k, v, do, lse_r, delta_r, nb, tp):
    tq = _pick(tp, ROW_TILE)
    nq = tp // tq

    def body(q_ref, k_ref, v_ref, do_ref, lse_ref, dl_ref, dq_ref, dk_ref, dv_ref):
        j = pl.program_id(2)

        @pl.when(j == 0)
        def _():
            dq_ref[...] = jnp.zeros_like(dq_ref)

        kbs = [k_ref[:, hd * HEAD_PAD:(hd + 1) * HEAD_PAD] for hd in range(HEADS_PER_STEP)]
        vbs = [v_ref[:, hd * HEAD_PAD:(hd + 1) * HEAD_PAD] for hd in range(HEADS_PER_STEP)]

        def chunk(i, carry, masked, width=1):
            off = pl.multiple_of(i * tq, tq)
            out = []
            for hd in range(HEADS_PER_STEP):
                dk, dv = carry[hd]
                lanes = slice(hd * HEAD_PAD, (hd + 1) * HEAD_PAD)
                qb = q_ref[pl.ds(off, width * tq), lanes]
                dob = do_ref[pl.ds(off, width * tq), lanes]
                lse = jnp.concatenate([lse_ref[hd, i + w][0:1, :] for w in range(width)], axis=1)
                delta = jnp.concatenate([dl_ref[hd, i + w][0:1, :] for w in range(width)], axis=1)
                st = lax.dot_general(kbs[hd], qb, NT, preferred_element_type=F32)
                pt = jnp.exp2(st - lse)
                if masked:
                    keys = lax.broadcasted_iota(jnp.int32, st.shape, 0)
                    queries = lax.broadcasted_iota(jnp.int32, st.shape, 1)
                    pt = jnp.where(keys <= queries, pt, 0.0)
                dv = dv + jnp.dot(pt.astype(BF16), dob, preferred_element_type=F32)
                dpt = lax.dot_general(vbs[hd], dob, NT, preferred_element_type=F32)
                dst = (pt * (dpt - delta)).astype(BF16)
                dk = dk + jnp.dot(dst, qb, preferred_element_type=F32)
                dq_ref[pl.ds(off, width * tq), lanes] += lax.dot_general(dst, kbs[hd], TN,
                                                                         preferred_element_type=F32)
                out.append((dk, dv))
            return tuple(out)

        zero = jnp.zeros((tq, HEAD_PAD), F32)
        wide = jnp.minimum(nq - 1 - j, 1)
        first = j + 1 + wide
        rest = nq - first
        carry = ((zero, zero),) * HEADS_PER_STEP
        carry = lax.fori_loop(0, wide, lambda _, c: chunk(j, c, True, 2), carry)
        carry = lax.fori_loop(0, 1 - wide, lambda _, c: chunk(j, c, True), carry)
        carry = lax.fori_loop(0, rest // 4, lambda pp, c: chunk(first + 4 * pp, c, False, 4), carry)
        carry = lax.fori_loop(0, lax.rem(rest, 4) // 2, lambda _, c: chunk(first + 4 * (rest // 4), c, False, 2), carry)
        carry = lax.fori_loop(0, lax.rem(rest, 2), lambda _, c: chunk(nq - 1, c, False), carry)
        for hd in range(HEADS_PER_STEP):
            lanes = slice(hd * HEAD_PAD, (hd + 1) * HEAD_PAD)
            dk_ref[:, lanes] = carry[hd][0] * LN2
            dv_ref[:, lanes] = carry[hd][1].astype(dv_ref.dtype)

    tspec = pl.BlockSpec((tq, STEP_LANES), lambda b, hh, j: (b * nq + j, hh))
    fullspec = pl.BlockSpec((tp, STEP_LANES), lambda b, hh, j: (b, hh))
    rspec = pl.BlockSpec((HEADS_PER_STEP, nq, SUBLANES, tq), lambda b, hh, j: (b * HEAD_STEPS + hh, 0, 0, 0))
    shp = (nb * tp, MLA_HEADS * HEAD_PAD)
    return pl.pallas_call(
        body, out_shape=(jax.ShapeDtypeStruct(shp, F32), jax.ShapeDtypeStruct(shp, F32),
                         jax.ShapeDtypeStruct(shp, BF16)),
        grid=(nb, HEAD_STEPS, nq), in_specs=[fullspec, tspec, tspec, fullspec, rspec, rspec],
        out_specs=(fullspec, tspec, tspec), compiler_params=_cparams(3),
        name="od_flash_bwd")(q, k, v, do, lse_r, delta_r)


def _odd_fwd(x, p, tables, m, tp, nb):
    scale = QK_HEAD ** -0.5
    h = _rms_fwd(x, p["norm"], "od_norm")
    u = _matmul(h, p["w_in"], "nn", F32, name="od_in")
    cq = u[:, :Q_LORA]
    ckv = u[:, Q_LORA:Q_LORA + KV_LORA]
    cqn = _rms_fwd(cq, p["q_norm"], "od_q_norm")
    ckvn = _rms_fwd(ckv, p["kv_norm"], "od_kv_norm")
    q_raw = _matmul(cqn, p["w_uq"], "nn", F32, name="od_uq")
    k_raw = _matmul(ckvn, p["w_uk"], "nn", F32, name="od_uk")
    v = _matmul(ckvn, p["w_uv"], "nn", BF16, name="od_uv")
    q = _rope_call("od_rope_q", lambda qv, c, lo, hi: _rope(qv, c, lo, hi) * (scale * LOG2E), m, tp,
                   [(q_raw, None)], tables, BF16)
    kr_col = (Q_LORA + KV_LORA) // HEAD_PAD
    k = _rope_call("od_rope_k", lambda kv, kr, c, lo, hi: kv + kr, m, tp, [(k_raw, None), (u, kr_col)], tables, BF16,
                   shared_pre=lambda uv, c, lo, hi: _rope(pltpu.roll(uv, QK_NOPE, 1), c, lo, hi))
    o, lse_c = _flash_fwd(q, k, v, nb, tp)
    out = _matmul(o, p["w_out"], "nn", F32, residual=x, name="od_out")
    return out, (x, h, cq, ckv, cqn, ckvn, q, k, v, o, lse_c)


def _odd_bwd(dout, p, tables, saved, m, tp, nb):
    scale = QK_HEAD ** -0.5
    x, h, cq, ckv, cqn, ckvn, q, k, v, o, lse_c = saved
    do = _matmul(dout, p["w_out"], "nt", BF16, name="od_out_dx")
    d_w_out = _matmul(o, dout, "tn", F32, name="od_out_dw")
    lse_r, delta_r = _flash_prep(o, do, lse_c, nb, tp)
    dq, dk, dv = _flash_bwd(q, k, v, do, lse_r, delta_r, nb, tp)
    dq_raw = _rope_call("od_rope_q_bwd", lambda d, c, lo, hi: _rope_t(d, c, lo, hi) * scale, m, tp, [(dq, None)],
                        tables, BF16)
    dkr = _rope_k_bwd(dk, tables, m, tp)
    d_w_uq = _matmul(cqn, dq_raw, "tn", F32, name="od_uq_dw")
    d_w_uk = _matmul(ckvn, dk, "tn", F32, name="od_uk_dw")
    d_w_uv = _matmul(ckvn, dv, "tn", F32, name="od_uv_dw")
    dcqn = _matmul(dq_raw, p["w_uq"], "nt", F32, name="od_uq_dx")
    dckvn = _matmul(dk, p["w_uk"], "nt", F32, name="od_uk_dx")
    dckvn = _matmul(dv, p["w_uv"], "nt", F32, residual=dckvn, name="od_uv_dx")
    dcq, d_q_norm = _rms_bwd(cq, p["q_norm"], dcqn, None, "od_q_norm_bwd")
    dckv, d_kv_norm = _rms_bwd(ckv, p["kv_norm"], dckvn, None, "od_kv_norm_bwd")
    du = jnp.concatenate([dcq, dckv, dkr], axis=1)
    d_w_in = _matmul(h, du, "tn", F32, name="od_in_dw")
    dh = _matmul(du, p["w_in"], "nt", F32, name="od_in_dx")
    dx, d_norm = _rms_bwd(x, p["norm"], dh, dout, "od_norm_bwd")
    return dx, dict(norm=d_norm, w_in=d_w_in, q_norm=d_q_norm, kv_norm=d_kv_norm, w_uq=d_w_uq, w_uk=d_w_uk,
                    w_uv=d_w_uv, w_out=d_w_out)


def _loss_head(hf, g, target, tp, t_real):
    m, c = hf.shape
    tm = _pick(tp, ROW_TILE)
    tps = tp // tm

    def body(x_ref, g_ref, t_ref, dx_ref, dg_ref, loss_ref):
        i = pl.program_id(0)
        xf = x_ref[...]
        r = lax.rsqrt(jnp.mean(xf * xf, axis=-1, keepdims=True) + EPS)
        xn = xf * r
        t_pos = lax.rem(i, tps) * tm + lax.broadcasted_iota(jnp.int32, (tm, 1), 0)
        valid = jnp.logical_and(t_pos >= N_META, t_pos < t_real)
        err = jnp.where(valid, xn * g_ref[...] - t_ref[...], 0.0)
        dyf = err * (1.0 / c)
        dyg = dyf * g_ref[...]
        dx_ref[...] = r * (dyg - xn * jnp.mean(dyg * xn, axis=-1, keepdims=True))

        @pl.when(i == 0)
        def _():
            dg_ref[...] = jnp.zeros_like(dg_ref)
            loss_ref[...] = jnp.zeros_like(loss_ref)

        dg_ref[...] += jnp.sum(dyf * xn, axis=0, keepdims=True)
        loss_ref[...] += (0.5 / c) * jnp.sum(jnp.sum(err * err, axis=1, keepdims=True), axis=0, keepdims=True)

    row = pl.BlockSpec((tm, c), lambda i: (i, 0))
    vec = pl.BlockSpec((1, c), lambda i: (0, 0))
    return pl.pallas_call(
        body, out_shape=(jax.ShapeDtypeStruct((m, c), F32), jax.ShapeDtypeStruct((1, c), F32),
                         jax.ShapeDtypeStruct((1, 1), F32)),
        grid=(m // tm,), in_specs=[row, vec, row], out_specs=(row, vec, pl.BlockSpec((1, 1), lambda i: (0, 0))),
        compiler_params=_cparams(1), name="loss_head")(hf, g, target)


def _meta_grad(dh0, nb, tp):
    d = dh0.shape[1]

    def body(x_ref, o_ref):
        @pl.when(pl.program_id(0) == 0)
        def _():
            o_ref[...] = jnp.zeros_like(o_ref)

        o_ref[...] += x_ref[...]

    return pl.pallas_call(
        body, out_shape=jax.ShapeDtypeStruct((N_META, d), F32), grid=(nb,),
        in_specs=[pl.BlockSpec((N_META, d), lambda b: (b * (tp // N_META), 0))],
        out_specs=pl.BlockSpec((N_META, d), lambda b: (0, 0)), compiler_params=_cparams(1), name="meta_grad")(dh0)


def _mesh_pos():
    x, y, c = lax.axis_index("x"), lax.axis_index("y"), lax.axis_index("c")
    return x, y, c


N_CHIP = 4
MESH_ID = pl.DeviceIdType.MESH


def _peer_chip(x, y, k):
    px = 1 - x if k & 2 else x
    py = 1 - y if k & 1 else y
    return px, py


def _all_gather(arrays):
    n = len(arrays)

    def body(*refs):
        srcs, outs = refs[:n], refs[n:2 * n]
        send_sems, recv_sems, local_sems = refs[2 * n:]
        x, y, c = _mesh_pos()
        me = 4 * x + 2 * y + c
        sibling = (x, y, 1 - c)

        def copy(a, sem, src, block, to):
            return pltpu.make_async_remote_copy(
                src_ref=src, dst_ref=outs[a].at[block], send_sem=send_sems.at[a, sem], recv_sem=recv_sems.at[a, sem],
                device_id=to, device_id_type=MESH_ID)

        local = [pltpu.make_async_copy(srcs[a], outs[a].at[me], local_sems.at[a]) for a in range(n)]
        for cp in local:
            cp.start()
        sends = [copy(a, 0, srcs[a], me, sibling) for a in range(n)]
        for k in range(1, N_CHIP):
            px, py = _peer_chip(x, y, k)
            sends += [copy(a, k, srcs[a], me, (px, py, c)) for a in range(n)]
        for cp in sends:
            cp.start()
        for k in range(1, N_CHIP):
            px, py = _peer_chip(x, y, k)
            block = 4 * px + 2 * py + c
            for a in range(n):
                copy(a, k, srcs[a], block, sibling).wait_recv()
            passed = [copy(a, N_CHIP - 1 + k, outs[a].at[block], block, sibling) for a in range(n)]
            for cp in passed:
                cp.start()
            sends += passed
        for a in range(n):
            copy(a, 0, srcs[a], 4 * x + 2 * y + (1 - c), sibling).wait_recv()
        for k in range(1, N_CHIP):
            px, py = _peer_chip(x, y, k)
            for a in range(n):
                copy(a, N_CHIP - 1 + k, srcs[a], 4 * px + 2 * py + (1 - c), sibling).wait_recv()
        for cp in sends:
            cp.wait_send()
        for cp in local:
            cp.wait()

    any_spec = pl.BlockSpec(memory_space=pl.ANY)
    out_shape = tuple(jax.ShapeDtypeStruct((N_DEV,) + a.shape, a.dtype) for a in arrays)
    return pl.pallas_call(
        body, out_shape=out_shape, in_specs=[any_spec] * n, out_specs=(any_spec,) * n,
        scratch_shapes=[pltpu.SemaphoreType.DMA((n, N_DEV - 1)), pltpu.SemaphoreType.DMA((n, N_DEV - 1)),
                        pltpu.SemaphoreType.DMA((n,))],
        name="weight_all_gather")(*arrays)


def _pair_exchange(arrays):
    n = len(arrays)

    def body(*refs):
        srcs, outs = refs[:n], refs[n:2 * n]
        send_sems, recv_sems = refs[2 * n:]
        x, y, c = _mesh_pos()
        copies = [pltpu.make_async_remote_copy(
            src_ref=srcs[a], dst_ref=outs[a], send_sem=send_sems.at[a], recv_sem=recv_sems.at[a],
            device_id=(x, y, 1 - c), device_id_type=MESH_ID) for a in range(n)]
        for cp in copies:
            cp.start()
        for cp in copies:
            cp.wait()

    any_spec = pl.BlockSpec(memory_space=pl.ANY)
    return pl.pallas_call(
        body, out_shape=tuple(jax.ShapeDtypeStruct(a.shape, a.dtype) for a in arrays), in_specs=[any_spec] * n,
        out_specs=(any_spec,) * n, scratch_shapes=[pltpu.SemaphoreType.DMA((n,)), pltpu.SemaphoreType.DMA((n,))],
        name="grad_pair_exchange")(*arrays)


def _chip_exchange(arrays):
    n = len(arrays)

    def body(*refs):
        srcs, outs = refs[:n], refs[n:2 * n]
        send_sems, recv_sems, local_sems = refs[2 * n:]
        x, y, c = _mesh_pos()
        q = 2 * x + y
        local = [pltpu.make_async_copy(srcs[a].at[q], outs[a].at[q], local_sems.at[a]) for a in range(n)]
        for cp in local:
            cp.start()

        def copy(a, k, to_q, from_q, px, py):
            return pltpu.make_async_remote_copy(
                src_ref=srcs[a].at[to_q], dst_ref=outs[a].at[from_q], send_sem=send_sems.at[a, k - 1],
                recv_sem=recv_sems.at[a, k - 1], device_id=(px, py, c), device_id_type=MESH_ID)

        sends = []
        for k in range(1, N_CHIP):
            px, py = _peer_chip(x, y, k)
            sends += [copy(a, k, 2 * px + py, q, px, py) for a in range(n)]
        for cp in sends:
            cp.start()
        for k in range(1, N_CHIP):
            px, py = _peer_chip(x, y, k)
            for a in range(n):
                copy(a, k, q, 2 * px + py, px, py).wait_recv()
        for cp in sends:
            cp.wait_send()
        for cp in local:
            cp.wait()

    any_spec = pl.BlockSpec(memory_space=pl.ANY)
    return pl.pallas_call(
        body, out_shape=tuple(jax.ShapeDtypeStruct(a.shape, a.dtype) for a in arrays), in_specs=[any_spec] * n,
        out_specs=(any_spec,) * n,
        scratch_shapes=[pltpu.SemaphoreType.DMA((n, N_CHIP - 1)), pltpu.SemaphoreType.DMA((n, N_CHIP - 1)),
                        pltpu.SemaphoreType.DMA((n,))],
        name="grad_chip_exchange")(*arrays)


REDUCE_BLOCK_BYTES = 512 * 1024


def _pair_add(a, b):
    p, r, c = a.shape
    tr = _reduce_rows(r, c)

    def body(a_ref, b_ref, o_ref):
        o_ref[...] = (a_ref[...].astype(F32) + b_ref[...].astype(F32)).astype(o_ref.dtype)

    blk = pl.BlockSpec((None, tr, c), lambda s, i: (s, i, 0))
    return pl.pallas_call(
        body, out_shape=jax.ShapeDtypeStruct(a.shape, a.dtype), grid=(p, r // tr), in_specs=[blk, blk], out_specs=blk,
        compiler_params=_cparams(2), name="grad_pair_add")(a, b)


def _reduce_rows(r, c):
    best = None
    for t in range(16, r + 1, 16):
        if r % t == 0 and t * c * 4 <= REDUCE_BLOCK_BYTES:
            best = t
    assert best is not None, (r, c)
    return best


def _reduce_adamw(parts, w, mom, vel):
    n_parts, r, c = parts.shape
    tr = _reduce_rows(r, c)
    c1 = 1.0 - ADAM_B1 ** ADAM_STEP
    c2 = 1.0 - ADAM_B2 ** ADAM_STEP

    def body(p_ref, w_ref, m_ref, v_ref, g_ref, d_ref, mo_ref, vo_ref):
        g = p_ref[0].astype(F32)
        for s in range(1, n_parts):
            g = g + p_ref[s].astype(F32)
        mn = ADAM_B1 * m_ref[...] + (1.0 - ADAM_B1) * g
        vn = ADAM_B2 * v_ref[...] + (1.0 - ADAM_B2) * (g * g)
        m_hat = mn / c1
        v_hat = vn / c2
        g_ref[...] = g
        d_ref[...] = -ADAM_LR * (m_hat / (jnp.sqrt(v_hat) + ADAM_EPS) + ADAM_WD * w_ref[...])
        mo_ref[...] = mn
        vo_ref[...] = vn

    blk = pl.BlockSpec((tr, c), lambda i: (i, 0))
    shp = jax.ShapeDtypeStruct((r, c), F32)
    return pl.pallas_call(
        body, out_shape=(shp, shp, shp, shp), grid=(r // tr,),
        in_specs=[pl.BlockSpec((n_parts, tr, c), lambda i: (0, i, 0)), blk, blk, blk], out_specs=(blk, blk, blk, blk),
        compiler_params=_cparams(1), name="reduce_adamw")(parts, w, mom, vel)


def _pack_rows(pieces, width, row_multiple, dtype):
    flat = jnp.concatenate([p.astype(dtype).reshape(-1) for p in pieces])
    rows = -(-flat.shape[0] // (width * row_multiple)) * row_multiple
    return jnp.pad(flat, (0, rows * width - flat.shape[0])).reshape(rows, width)


def _unshard(gathered, axis):
    moved = jnp.moveaxis(gathered, 0, axis)
    shape = list(moved.shape)
    shape[axis:axis + 2] = [shape[axis] * shape[axis + 1]]
    return moved.reshape(shape)


def _to_slots(full, axis):
    shape = list(full.shape)
    shape[axis:axis + 1] = [N_DEV, shape[axis] // N_DEV]
    return jnp.moveaxis(full.reshape(shape), axis, 0)


def _core_slots(full, axis, core):
    shape = list(full.shape)
    shape[axis:axis + 1] = [N_CHIP, 2, shape[axis] // N_DEV]
    picked = lax.dynamic_index_in_dim(full.reshape(shape), core, axis + 1, keepdims=False)
    return jnp.moveaxis(picked, axis, 0)


def _block_diag(w):
    hh, d, _ = w.shape
    eye = jnp.eye(hh, dtype=w.dtype)
    return (w[:, :, None, :] * eye[:, None, :, None]).reshape(hh * d, hh * d)


def _block_diag_t(full, hh):
    d = full.shape[0] // hh
    f4 = full.reshape(hh, d, hh, d)
    return jnp.stack([f4[i, :, i, :] for i in range(hh)], axis=0)


def _pad_heads(w, width):
    r = w.shape[0]
    w3 = w.reshape(r, MLA_HEADS, width)
    return jnp.pad(w3, ((0, 0), (0, 0), (0, HEAD_PAD - width))).reshape(r, MLA_HEADS * HEAD_PAD)


def _unpad_heads(w, width):
    r = w.shape[0]
    return w.reshape(r, MLA_HEADS, HEAD_PAD)[:, :, :width].reshape(r, MLA_HEADS * width)


def kernel(x, meta_tokens, ev_norm, ev_w_in, ev_conv_a, ev_conv_b, ev_conv_b_bias, ev_gate_r_w, ev_gate_r_b, ev_gate_i_w, ev_gate_i_b, ev_lru_lambda, ev_w_out, od_norm, od_w_in, od_q_norm, od_kv_norm, od_w_uq, od_w_ukv, od_w_out, ffn_norm, ffn_w_up, ffn_conv_w, ffn_conv_b, ffn_w_down, final_norm, loss_target, m_meta_tokens, m_ev_norm, m_ev_w_in, m_ev_conv_a, m_ev_conv_b, m_ev_conv_b_bias, m_ev_gate_r_w, m_ev_gate_r_b, m_ev_gate_i_w, m_ev_gate_i_b, m_ev_lru_lambda, m_ev_w_out, m_od_norm, m_od_w_in, m_od_q_norm, m_od_kv_norm, m_od_w_uq, m_od_w_ukv, m_od_w_out, m_ffn_norm, m_ffn_w_up, m_ffn_conv_w, m_ffn_conv_b, m_ffn_w_down, m_final_norm, v_meta_tokens, v_ev_norm, v_ev_w_in, v_ev_conv_a, v_ev_conv_b, v_ev_conv_b_bias, v_ev_gate_r_w, v_ev_gate_r_b, v_ev_gate_i_w, v_ev_gate_i_b, v_ev_lru_lambda, v_ev_w_out, v_od_norm, v_od_w_in, v_od_q_norm, v_od_kv_norm, v_od_w_uq, v_od_w_ukv, v_od_w_out, v_ffn_norm, v_ffn_w_up, v_ffn_conv_w, v_ffn_conv_b, v_ffn_w_down, v_final_norm):
    given = dict(locals())
    names = [n for n, _ in PARAMS]
    axis_of = dict(PARAMS)
    w_loc = {n: given[n] for n in names}
    m_loc = {n: given["m_" + n] for n in names}
    v_loc = {n: given["v_" + n] for n in names}
    sharded = [n for n in names if axis_of[n] is not None]
    replicated = [n for n in names if axis_of[n] is None]
    small = [n for n in sharded if n not in BIG]

    nb, seq, d = x.shape
    t_real = N_META + seq
    tp = -(-t_real // ROW_TILE) * ROW_TILE
    m = nb * tp

    small_pack = _pack_rows([w_loc[n] for n in small], LANES, SUBLANES, F32)
    gathered = _all_gather([w_loc[n].astype(BF16) for n in BIG] + [small_pack])
    full = {n: w_loc[n] for n in replicated}
    for n, g in zip(BIG, gathered[:-1]):
        full[n] = _unshard(g, axis_of[n])
    flat = gathered[-1].reshape(N_DEV, -1)
    off = 0
    for n in small:
        shard = w_loc[n].shape
        size = math.prod(shard)
        full[n] = _unshard(flat[:, off:off + size].reshape((N_DEV,) + shard), axis_of[n])
        off += size

    tables = _rope_tables(tp)

    def even_params(j):
        w_out = full["ev_w_out"][j]
        return dict(norm=full["ev_norm"][j][None], w_in=full["ev_w_in"][j], conv_a=full["ev_conv_a"][j],
                    conv_b=full["ev_conv_b"][j], conv_b_bias=full["ev_conv_b_bias"][j][None],
                    gate_r=_block_diag(full["ev_gate_r_w"][j]).astype(BF16),
                    gate_i=_block_diag(full["ev_gate_i_w"][j]).astype(BF16),
                    gate_r_b=full["ev_gate_r_b"][j][None], gate_i_b=full["ev_gate_i_b"][j][None],
                    lam=full["ev_lru_lambda"][j][None], w_out=w_out, w_out_a=w_out[:LRU_WIDTH],
                    w_out_b=w_out[LRU_WIDTH:])

    def odd_params(j):
        w_ukv = full["od_w_ukv"][j].reshape(KV_LORA, MLA_HEADS, QK_NOPE + V_HEAD)
        w_uk = w_ukv[:, :, :QK_NOPE].reshape(KV_LORA, MLA_HEADS * QK_NOPE)
        w_uv = w_ukv[:, :, QK_NOPE:].reshape(KV_LORA, MLA_HEADS * V_HEAD)
        w_out = full["od_w_out"][j].reshape(MLA_HEADS, V_HEAD, d)
        w_out = jnp.pad(w_out, ((0, 0), (0, HEAD_PAD - V_HEAD), (0, 0))).reshape(MLA_HEADS * HEAD_PAD, d)
        return dict(norm=full["od_norm"][j][None], w_in=jnp.pad(full["od_w_in"][j], ((0, 0), (0, ODD_IN_PAD - ODD_IN))),
                    q_norm=full["od_q_norm"][j][None], kv_norm=full["od_kv_norm"][j][None],
                    w_uq=_pad_heads(full["od_w_uq"][j], QK_HEAD), w_uk=_pad_heads(w_uk, QK_NOPE),
                    w_uv=_pad_heads(w_uv, V_HEAD), w_out=w_out)

    def ffn_params(layer):
        w_up = full["ffn_w_up"][layer]
        return dict(norm=full["ffn_norm"][layer][None], w_up=w_up, conv_w=full["ffn_conv_w"][layer],
                    conv_b=full["ffn_conv_b"][layer][None], w_down=full["ffn_w_down"][layer])

    meta = jnp.broadcast_to(full["meta_tokens"][None], (nb, N_META, d))
    h0 = jnp.concatenate([meta, x, jnp.zeros((nb, tp - t_real, d), F32)], axis=1).reshape(m, d)
    hcur = h0
    tape = []
    for layer in range(4):
        j = layer // 2
        if layer % 2 == 0:
            mp = even_params(j)
            hcur, saved = _even_fwd(hcur, mp, m, tp, nb)
        else:
            mp = odd_params(j)
            hcur, saved = _odd_fwd(hcur, mp, tables, m, tp, nb)
        fp = ffn_params(layer)
        hcur, fsaved = _ffn_fwd(hcur, fp, m, tp)
        tape.append((mp, saved, fp, fsaved))

    target = jnp.pad(loss_target, ((0, 0), (N_META, tp - t_real), (0, 0))).reshape(m, d)
    dh, d_final_norm, loss_part = _loss_head(hcur, full["final_norm"][None], target, tp, t_real)

    grads = {"final_norm": d_final_norm[0]}
    ev_g, od_g, ffn_g = [None, None], [None, None], [None] * 4
    for layer in reversed(range(4)):
        mp, saved, fp, fsaved = tape[layer]
        dh, ffn_g[layer] = _ffn_bwd(dh, fp, fsaved, m, tp)
        if layer % 2 == 0:
            dh, ev_g[layer // 2] = _even_bwd(dh, mp, saved, m, tp, nb)
        else:
            dh, od_g[layer // 2] = _odd_bwd(dh, mp, tables, saved, m, tp, nb)

    dh3 = dh.reshape(nb, tp, d)
    grad_x = dh3[:, N_META:t_real]
    grads["meta_tokens"] = _meta_grad(dh, nb, tp)

    def stack(lst, key, fn=lambda a: a):
        return jnp.stack([fn(g[key]) for g in lst], axis=0)

    grads["ev_norm"] = stack(ev_g, "norm", lambda a: a[0])
    grads["ev_w_in"] = stack(ev_g, "w_in")
    grads["ev_conv_a"] = stack(ev_g, "conv_a")
    grads["ev_conv_b"] = stack(ev_g, "conv_b")
    grads["ev_conv_b_bias"] = stack(ev_g, "conv_b_bias", lambda a: a[0])
    grads["ev_gate_r_w"] = stack(ev_g, "gate_r", lambda a: _block_diag_t(a, 8))
    grads["ev_gate_r_b"] = stack(ev_g, "gate_r_b", lambda a: a[0])
    grads["ev_gate_i_w"] = stack(ev_g, "gate_i", lambda a: _block_diag_t(a, 8))
    grads["ev_gate_i_b"] = stack(ev_g, "gate_i_b", lambda a: a[0])
    grads["ev_lru_lambda"] = stack(ev_g, "lam", lambda a: a[0])
    grads["ev_w_out"] = stack(ev_g, "w_out")
    grads["od_norm"] = stack(od_g, "norm", lambda a: a[0])
    grads["od_w_in"] = stack(od_g, "w_in", lambda a: a[:, :ODD_IN])
    grads["od_q_norm"] = stack(od_g, "q_norm", lambda a: a[0])
    grads["od_kv_norm"] = stack(od_g, "kv_norm", lambda a: a[0])
    grads["od_w_uq"] = stack(od_g, "w_uq", lambda a: _unpad_heads(a, QK_HEAD))

    def ukv(g):
        gk = g["w_uk"].reshape(KV_LORA, MLA_HEADS, HEAD_PAD)[:, :, :QK_NOPE]
        gv = g["w_uv"].reshape(KV_LORA, MLA_HEADS, HEAD_PAD)[:, :, :V_HEAD]
        return jnp.concatenate([gk, gv], axis=2).reshape(KV_LORA, MLA_HEADS * (QK_NOPE + V_HEAD))

    grads["od_w_ukv"] = jnp.stack([ukv(g) for g in od_g], axis=0)
    grads["od_w_out"] = stack(od_g, "w_out", lambda a: a.reshape(MLA_HEADS, HEAD_PAD, d)[:, :V_HEAD].reshape(-1, d))
    grads["ffn_norm"] = stack(ffn_g, "norm", lambda a: a[0])
    grads["ffn_w_up"] = stack(ffn_g, "w_up")
    grads["ffn_conv_w"] = stack(ffn_g, "conv_w")
    grads["ffn_conv_b"] = stack(ffn_g, "conv_b", lambda a: a[0])
    grads["ffn_w_down"] = stack(ffn_g, "w_down")

    order = small + replicated
    slot_parts = [_to_slots(grads[n], axis_of[n]).reshape(N_DEV, -1) for n in small]
    slot_parts += [jnp.broadcast_to(grads[n].reshape(1, -1), (N_DEV, grads[n].size)) for n in replicated]
    slot_parts.append(jnp.broadcast_to(loss_part, (N_DEV, 1)))
    g_flat = jnp.concatenate(slot_parts, axis=1)
    n_flat = g_flat.shape[1]
    rows = -(-n_flat // (1024 * 128)) * 128
    g_small = jnp.pad(g_flat, ((0, 0), (0, rows * 1024 - n_flat))).reshape(N_DEV, rows, 1024)

    def rows_of(n):
        shard = w_loc[n].shape
        return (math.prod(shard[:-1]), shard[-1])

    core = lax.axis_index("c")

    def core_slots(n, which):
        return _core_slots(grads[n], axis_of[n], which).astype(BF16).reshape((N_CHIP,) + rows_of(n))

    small_by_core = jnp.swapaxes(g_small.reshape((N_CHIP, 2) + g_small.shape[1:]), 0, 1)
    mine = [core_slots(n, core) for n in BIG] + [lax.dynamic_index_in_dim(small_by_core, core, 0, keepdims=False)]
    theirs = [core_slots(n, 1 - core) for n in BIG]
    theirs.append(lax.dynamic_index_in_dim(small_by_core, 1 - core, 0, keepdims=False))
    from_sibling = _pair_exchange(theirs)
    parts = _chip_exchange([_pair_add(a, b) for a, b in zip(mine, from_sibling)])

    g_out, d_out, m_out, v_out = {}, {}, {}, {}
    for n, part in zip(BIG, parts[:-1]):
        res = _reduce_adamw(part, *[t[n].reshape(rows_of(n)) for t in (w_loc, m_loc, v_loc)])
        for out, r in zip((g_out, d_out, m_out, v_out), res):
            out[n] = r.reshape(w_loc[n].shape)

    def flat_local(tree):
        flat = jnp.concatenate([tree[n].reshape(-1) for n in order])
        return jnp.pad(flat, (0, rows * 1024 - flat.shape[0])).reshape(rows, 1024)

    res = _reduce_adamw(parts[-1], flat_local(w_loc), flat_local(m_loc), flat_local(v_loc))
    loss = res[0].reshape(-1)[n_flat - 1]
    for out, r in zip((g_out, d_out, m_out, v_out), res):
        flat = r.reshape(-1)
        off = 0
        for n in order:
            size = w_loc[n].size
            out[n] = flat[off:off + size].reshape(w_loc[n].shape)
            off += size
    return (loss, grad_x, *[g_out[n] for n in names], *[d_out[n] for n in names], *[m_out[n] for n in names],
            *[v_out[n] for n in names])
```
